```python
import math
import jax, jax.numpy as jnp
from jax import lax
import numpy as np

D_MODEL = 1024
BATCH = 16
SEQ = 2048
DEPTH = 1

N_META = 16
D_MIX = D_MODEL
C_CONV = D_MIX // 2
CONV_GROUPS = 8
CONV_WIDTH = 31
GLA_V = D_MIX - C_CONV
GLA_HEADS = 4
GLA_DV = GLA_V // GLA_HEADS
GLA_DK = GLA_DV // 2
GLA_K = GLA_HEADS * GLA_DK
GATE_RANK = 16
GATE_TAU = 16.0
CHUNK = 64
META_PAD = CHUNK - N_META
D_FF = int(math.ceil(D_MODEL * 8 / 3 / 256) * 256)
RMS_EPS = 1e-6
LN_EPS = 1e-5

IN_SPLITS = [C_CONV, C_CONV, GLA_K, GLA_K, GLA_V, GLA_V, GATE_RANK]
D_IN = sum(IN_SPLITS)

kernel_name = "hymba_conformer_conv_gla_hybrid"


def rms_norm(x, g):
    xf = x.astype(jnp.float32)
    y = xf * lax.rsqrt(jnp.mean(xf * xf, axis=-1, keepdims=True) + RMS_EPS)
    return (y * g.astype(jnp.float32)).astype(x.dtype)


def conv_group(u_val, u_gate, conv_w, conv_b, ln_g, ln_b):
    v = u_val * jax.nn.sigmoid(u_gate)
    y = lax.conv_general_dilated(
        v, conv_w[:, None, :].astype(v.dtype),
        window_strides=(1,), padding=((CONV_WIDTH - 1, 0),),
        dimension_numbers=("NWC", "WIO", "NWC"),
        feature_group_count=C_CONV) + conv_b
    yf = y.astype(jnp.float32)
    mu = jnp.mean(yf, axis=-1, keepdims=True)
    var = jnp.mean(jnp.square(yf - mu), axis=-1, keepdims=True)
    yf = (yf - mu) * lax.rsqrt(var + LN_EPS) * ln_g.astype(jnp.float32) + ln_b.astype(jnp.float32)
    return jax.nn.silu(yf).astype(u_val.dtype)


def _to_chunks(t, d):
    b, lp, _ = t.shape
    return t.reshape(b, lp // CHUNK, CHUNK, GLA_HEADS, d).transpose(0, 1, 3, 2, 4)


def gla_group(q, k, v, g, gate_lr, w_gate2, gate_b, norm_g):
    dt = q.dtype
    bsz, seq_len, _ = q.shape
    f = lambda t: t.astype(jnp.float32)
    log_a = jax.nn.log_sigmoid(f(gate_lr) @ f(w_gate2) + f(gate_b)) / GATE_TAU
    pad = lambda t: jnp.pad(f(t), ((0, 0), (META_PAD, 0), (0, 0)))
    qc = _to_chunks(pad(q), GLA_DK) * (GLA_DK ** -0.5)
    kc = _to_chunks(pad(k), GLA_DK)
    vc = _to_chunks(pad(v), GLA_DV)
    bc = jnp.cumsum(_to_chunks(pad(log_a), GLA_DK), axis=3)
    b_last = bc[:, :, :, -1:, :]

    q_in = qc * jnp.exp(bc)
    k_in = kc * jnp.exp(-bc)
    causal = jnp.tril(jnp.ones((CHUNK, CHUNK), dtype=bool))
    scores = jnp.einsum("bnhid,bnhjd->bnhij", q_in, k_in)
    scores = jnp.where(causal, scores, 0.0)
    o_intra = jnp.einsum("bnhij,bnhje->bnhie", scores, vc)

    kv = jnp.einsum("bnhjd,bnhje->bnhde", kc * jnp.exp(b_last - bc), vc)
    decay = jnp.exp(b_last[:, :, :, 0, :])

    def step(state, inp):
        dec, kv_n = inp
        return dec[..., None] * state + kv_n, state

    s0 = jnp.zeros((bsz, GLA_HEADS, GLA_DK, GLA_DV), jnp.float32)
    _, s_prev = lax.scan(step, s0, (decay.swapaxes(0, 1), kv.swapaxes(0, 1)))
    s_prev = s_prev.swapaxes(0, 1)
    o_inter = jnp.einsum("bnhid,bnhde->bnhie", q_in, s_prev)

    o = (o_intra + o_inter).transpose(0, 1, 3, 2, 4).reshape(bsz, -1, GLA_HEADS, GLA_DV)[:, META_PAD:]
    o = o * lax.rsqrt(jnp.mean(o * o, axis=-1, keepdims=True) + RMS_EPS) * f(norm_g)
    o = o * jax.nn.silu(f(g)).reshape(bsz, seq_len, GLA_HEADS, GLA_DV)
    return o.reshape(bsz, seq_len, GLA_V).astype(dt)


def swiglu(x, w_gate, w_up, w_down):
    return (jax.nn.silu(x @ w_gate) * (x @ w_up)) @ w_down


def _fwd_setup_inputs(seed: int = 0) -> dict:
    key = jax.random.key(seed)
    ks = jax.random.split(key, 20)
    n = lambda k, shape, s: jax.random.normal(k, shape, jnp.float32) * s
    return {
        "x": n(ks[0], (BATCH, SEQ, D_MODEL), 1.0),
        "meta_tokens": n(ks[1], (N_META, D_MODEL), 1.0),
        "norm_mix_g": 1.0 + n(ks[2], (DEPTH, D_MODEL), 0.02),
        "w_in": n(ks[3], (DEPTH, D_MODEL, D_IN), D_MODEL ** -0.5),
        "conv_w": n(ks[4], (DEPTH, CONV_WIDTH, C_CONV), CONV_WIDTH ** -0.5),
        "conv_b": n(ks[5], (DEPTH, C_CONV), 0.02),
        "conv_ln_g": 1.0 + n(ks[6], (DEPTH, C_CONV), 0.02),
        "conv_ln_b": n(ks[7], (DEPTH, C_CONV), 0.02),
        "gla_w_gate2": n(ks[8], (DEPTH, GATE_RANK, GLA_K), GATE_RANK ** -0.5),
        "gla_gate_b": n(ks[9], (DEPTH, GLA_K), 0.1),
        "gla_norm_g": 1.0 + n(ks[10], (DEPTH, GLA_DV), 0.02),
        "w_out": n(ks[11], (DEPTH, D_MIX, D_MODEL), D_MIX ** -0.5),
        "norm_ffn_g": 1.0 + n(ks[12], (DEPTH, D_MODEL), 0.02),
        "w_ffn_gate": n(ks[13], (DEPTH, D_MODEL, D_FF), D_MODEL ** -0.5),
        "w_ffn_up": n(ks[14], (DEPTH, D_MODEL, D_FF), D_MODEL ** -0.5),
        "w_ffn_down": n(ks[15], (DEPTH, D_FF, D_MODEL), D_FF ** -0.5),
        "norm_final_g": 1.0 + n(ks[16], (D_MODEL,), 0.02),
    }


def _fwd_reference(x, meta_tokens, norm_mix_g, w_in, conv_w, conv_b, conv_ln_g, conv_ln_b,
              gla_w_gate2, gla_gate_b, gla_norm_g, w_out, norm_ffn_g, w_ffn_gate,
              w_ffn_up, w_ffn_down, norm_final_g):
    bsz = x.shape[0]
    meta = jnp.broadcast_to(meta_tokens[None].astype(x.dtype), (bsz, N_META, D_MODEL))
    h = jnp.concatenate([meta, x], axis=1)
    split_idx = list(np.cumsum(IN_SPLITS)[:-1])
    for l in range(DEPTH):
        u = rms_norm(h, norm_mix_g[l]) @ w_in[l]
        c_val, c_gate, q, k, v, g, gate_lr = jnp.split(u, split_idx, axis=-1)
        y_conv = conv_group(c_val, c_gate, conv_w[l], conv_b[l], conv_ln_g[l], conv_ln_b[l])
        y_gla = gla_group(q, k, v, g, gate_lr, gla_w_gate2[l], gla_gate_b[l], gla_norm_g[l])
        h = h + jnp.concatenate([y_conv, y_gla], axis=-1) @ w_out[l]
        h = h + swiglu(rms_norm(h, norm_ffn_g[l]), w_ffn_gate[l], w_ffn_up[l], w_ffn_down[l])
    y = rms_norm(h, norm_final_g)
    return y[:, N_META:]


import jax as _jax
import jax.numpy as _jnp

TWIN_FORMAT = 'train_step'
FWD_PARAMS = ['x', 'meta_tokens', 'norm_mix_g', 'w_in', 'conv_w', 'conv_b', 'conv_ln_g', 'conv_ln_b', 'gla_w_gate2', 'gla_gate_b', 'gla_norm_g', 'w_out', 'norm_ffn_g', 'w_ffn_gate', 'w_ffn_up', 'w_ffn_down', 'norm_final_g']
TWIN_WEIGHTS = ['meta_tokens', 'norm_mix_g', 'w_in', 'conv_w', 'conv_b', 'conv_ln_g', 'conv_ln_b', 'gla_w_gate2', 'gla_gate_b', 'gla_norm_g', 'w_out', 'norm_ffn_g', 'w_ffn_gate', 'w_ffn_up', 'w_ffn_down', 'norm_final_g']
TWIN_DIFF_INPUT = 'x'
TWIN_INPUTS = ['x', 'meta_tokens', 'norm_mix_g', 'w_in', 'conv_w', 'conv_b', 'conv_ln_g', 'conv_ln_b', 'gla_w_gate2', 'gla_gate_b', 'gla_norm_g', 'w_out', 'norm_ffn_g', 'w_ffn_gate', 'w_ffn_up', 'w_ffn_down', 'norm_final_g', 'loss_target', 'm_meta_tokens', 'm_norm_mix_g', 'm_w_in', 'm_conv_w', 'm_conv_b', 'm_conv_ln_g', 'm_conv_ln_b', 'm_gla_w_gate2', 'm_gla_gate_b', 'm_gla_norm_g', 'm_w_out', 'm_norm_ffn_g', 'm_w_ffn_gate', 'm_w_ffn_up', 'm_w_ffn_down', 'm_norm_final_g', 'v_meta_tokens', 'v_norm_mix_g', 'v_w_in', 'v_conv_w', 'v_conv_b', 'v_conv_ln_g', 'v_conv_ln_b', 'v_gla_w_gate2', 'v_gla_gate_b', 'v_gla_norm_g', 'v_w_out', 'v_norm_ffn_g', 'v_w_ffn_gate', 'v_w_ffn_up', 'v_w_ffn_down', 'v_norm_final_g']
TWIN_OUTPUTS = ['loss', 'grad_x', 'grad_meta_tokens', 'grad_norm_mix_g', 'grad_w_in', 'grad_conv_w', 'grad_conv_b', 'grad_conv_ln_g', 'grad_conv_ln_b', 'grad_gla_w_gate2', 'grad_gla_gate_b', 'grad_gla_norm_g', 'grad_w_out', 'grad_norm_ffn_g', 'grad_w_ffn_gate', 'grad_w_ffn_up', 'grad_w_ffn_down', 'grad_norm_final_g', 'delta_meta_tokens', 'delta_norm_mix_g', 'delta_w_in', 'delta_conv_w', 'delta_conv_b', 'delta_conv_ln_g', 'delta_conv_ln_b', 'delta_gla_w_gate2', 'delta_gla_gate_b', 'delta_gla_norm_g', 'delta_w_out', 'delta_norm_ffn_g', 'delta_w_ffn_gate', 'delta_w_ffn_up', 'delta_w_ffn_down', 'delta_norm_final_g', 'new_m_meta_tokens', 'new_m_norm_mix_g', 'new_m_w_in', 'new_m_conv_w', 'new_m_conv_b', 'new_m_conv_ln_g', 'new_m_conv_ln_b', 'new_m_gla_w_gate2', 'new_m_gla_gate_b', 'new_m_gla_norm_g', 'new_m_w_out', 'new_m_norm_ffn_g', 'new_m_w_ffn_gate', 'new_m_w_ffn_up', 'new_m_w_ffn_down', 'new_m_norm_final_g', 'new_v_meta_tokens', 'new_v_norm_mix_g', 'new_v_w_in', 'new_v_conv_w', 'new_v_conv_b', 'new_v_conv_ln_g', 'new_v_conv_ln_b', 'new_v_gla_w_gate2', 'new_v_gla_gate_b', 'new_v_gla_norm_g', 'new_v_w_out', 'new_v_norm_ffn_g', 'new_v_w_ffn_gate', 'new_v_w_ffn_up', 'new_v_w_ffn_down', 'new_v_norm_final_g']
TWIN_LEAF_KINDS = {'loss': 'loss', 'grad_x': 'grad_x', 'grad_meta_tokens': 'grad_w', 'grad_norm_mix_g': 'grad_w', 'grad_w_in': 'grad_w', 'grad_conv_w': 'grad_w', 'grad_conv_b': 'grad_w', 'grad_conv_ln_g': 'grad_w', 'grad_conv_ln_b': 'grad_w', 'grad_gla_w_gate2': 'grad_w', 'grad_gla_gate_b': 'grad_w', 'grad_gla_norm_g': 'grad_w', 'grad_w_out': 'grad_w', 'grad_norm_ffn_g': 'grad_w', 'grad_w_ffn_gate': 'grad_w', 'grad_w_ffn_up': 'grad_w', 'grad_w_ffn_down': 'grad_w', 'grad_norm_final_g': 'grad_w', 'delta_meta_tokens': 'delta_w', 'delta_norm_mix_g': 'delta_w', 'delta_w_in': 'delta_w', 'delta_conv_w': 'delta_w', 'delta_conv_b': 'delta_w', 'delta_conv_ln_g': 'delta_w', 'delta_conv_ln_b': 'delta_w', 'delta_gla_w_gate2': 'delta_w', 'delta_gla_gate_b': 'delta_w', 'delta_gla_norm_g': 'delta_w', 'delta_w_out': 'delta_w', 'delta_norm_ffn_g': 'delta_w', 'delta_w_ffn_gate': 'delta_w', 'delta_w_ffn_up': 'delta_w', 'delta_w_ffn_down': 'delta_w', 'delta_norm_final_g': 'delta_w', 'new_m_meta_tokens': 'new_m', 'new_m_norm_mix_g': 'new_m', 'new_m_w_in': 'new_m', 'new_m_conv_w': 'new_m', 'new_m_conv_b': 'new_m', 'new_m_conv_ln_g': 'new_m', 'new_m_conv_ln_b': 'new_m', 'new_m_gla_w_gate2': 'new_m', 'new_m_gla_gate_b': 'new_m', 'new_m_gla_norm_g': 'new_m', 'new_m_w_out': 'new_m', 'new_m_norm_ffn_g': 'new_m', 'new_m_w_ffn_gate': 'new_m', 'new_m_w_ffn_up': 'new_m', 'new_m_w_ffn_down': 'new_m', 'new_m_norm_final_g': 'new_m', 'new_v_meta_tokens': 'new_v', 'new_v_norm_mix_g': 'new_v', 'new_v_w_in': 'new_v', 'new_v_conv_w': 'new_v', 'new_v_conv_b': 'new_v', 'new_v_conv_ln_g': 'new_v', 'new_v_conv_ln_b': 'new_v', 'new_v_gla_w_gate2': 'new_v', 'new_v_gla_gate_b': 'new_v', 'new_v_gla_norm_g': 'new_v', 'new_v_w_out': 'new_v', 'new_v_norm_ffn_g': 'new_v', 'new_v_w_ffn_gate': 'new_v', 'new_v_w_ffn_up': 'new_v', 'new_v_w_ffn_down': 'new_v', 'new_v_norm_final_g': 'new_v'}


def _forward(args):
    return _fwd_reference(*[args[k] for k in FWD_PARAMS])


def _output_shape():
    out = _jax.eval_shape(lambda: _forward(_fwd_setup_inputs(0)))
    return out.shape, out.dtype

N_MICROBATCH = 1
ADAM_LR = 0.001
ADAM_B1 = 0.9
ADAM_B2 = 0.999
ADAM_EPS = 1e-08
ADAM_WD = 0.01
ADAM_STEP = 10
PER_EXAMPLE_BATCH_AXIS = {'x': 0, 'loss_target': 0}
SHARED_INPUTS = []
_WEIGHT_DTYPES = {'meta_tokens': _jnp.float32, 'norm_mix_g': _jnp.float32, 'w_in': _jnp.float32, 'conv_w': _jnp.float32, 'conv_b': _jnp.float32, 'conv_ln_g': _jnp.float32, 'conv_ln_b': _jnp.float32, 'gla_w_gate2': _jnp.float32, 'gla_gate_b': _jnp.float32, 'gla_norm_g': _jnp.float32, 'w_out': _jnp.float32, 'norm_ffn_g': _jnp.float32, 'w_ffn_gate': _jnp.float32, 'w_ffn_up': _jnp.float32, 'w_ffn_down': _jnp.float32, 'norm_final_g': _jnp.float32}
MOMENT_SCALE = {'meta_tokens': 6.553450e-03, 'norm_mix_g': 1.764959e-01, 'w_in': 1.047846e-01, 'conv_w': 1.038458e-01, 'conv_b': 2.072593e-01, 'conv_ln_g': 1.343122e-01, 'conv_ln_b': 1.133050e-01, 'gla_w_gate2': 1.580142e-02, 'gla_gate_b': 6.048570e-02, 'gla_norm_g': 1.951791e-01, 'w_out': 1.008458e-01, 'norm_ffn_g': 1.184377e-01, 'w_ffn_gate': 5.048233e-02, 'w_ffn_up': 4.879075e-02, 'w_ffn_down': 8.077080e-02, 'norm_final_g': 3.203533e+01}


def _to_microbatches(a, axis):
    t = _jnp.moveaxis(a, axis, 0)
    t = t.reshape((N_MICROBATCH, t.shape[0] // N_MICROBATCH) + t.shape[1:])
    return _jnp.moveaxis(t, 1, axis + 1)


def setup_inputs(seed: int = 0) -> dict:
    inp = _fwd_setup_inputs(seed)
    key = _jax.random.fold_in(_jax.random.key(seed), 7919)
    shape, _ = _output_shape()
    out = dict(inp)
    out["loss_target"] = _jax.random.normal(_jax.random.fold_in(key, 0), shape, _jnp.float32)
    for i, name in enumerate(TWIN_WEIGHTS):
        w = inp[name].astype(_jnp.float32)
        if MOMENT_SCALE is None:
            s = _jnp.sqrt(_jnp.mean(_jnp.square(w)) + 1e-30)
        else:
            s = MOMENT_SCALE[name]
        km, kv = _jax.random.split(_jax.random.fold_in(key, i + 1))
        out[name] = w
        out["m_" + name] = s * _jax.random.normal(km, w.shape, _jnp.float32)
        out["v_" + name] = (s * s) * _jax.random.uniform(kv, w.shape, _jnp.float32, 0.5, 1.5)
    if N_MICROBATCH > 1:
        for name, axis in PER_EXAMPLE_BATCH_AXIS.items():
            out[name] = _to_microbatches(out[name], axis)
    return {'x': out['x'], 'meta_tokens': out['meta_tokens'], 'norm_mix_g': out['norm_mix_g'], 'w_in': out['w_in'], 'conv_w': out['conv_w'], 'conv_b': out['conv_b'], 'conv_ln_g': out['conv_ln_g'], 'conv_ln_b': out['conv_ln_b'], 'gla_w_gate2': out['gla_w_gate2'], 'gla_gate_b': out['gla_gate_b'], 'gla_norm_g': out['gla_norm_g'], 'w_out': out['w_out'], 'norm_ffn_g': out['norm_ffn_g'], 'w_ffn_gate': out['w_ffn_gate'], 'w_ffn_up': out['w_ffn_up'], 'w_ffn_down': out['w_ffn_down'], 'norm_final_g': out['norm_final_g'], 'loss_target': out['loss_target'], 'm_meta_tokens': out['m_meta_tokens'], 'm_norm_mix_g': out['m_norm_mix_g'], 'm_w_in': out['m_w_in'], 'm_conv_w': out['m_conv_w'], 'm_conv_b': out['m_conv_b'], 'm_conv_ln_g': out['m_conv_ln_g'], 'm_conv_ln_b': out['m_conv_ln_b'], 'm_gla_w_gate2': out['m_gla_w_gate2'], 'm_gla_gate_b': out['m_gla_gate_b'], 'm_gla_norm_g': out['m_gla_norm_g'], 'm_w_out': out['m_w_out'], 'm_norm_ffn_g': out['m_norm_ffn_g'], 'm_w_ffn_gate': out['m_w_ffn_gate'], 'm_w_ffn_up': out['m_w_ffn_up'], 'm_w_ffn_down': out['m_w_ffn_down'], 'm_norm_final_g': out['m_norm_final_g'], 'v_meta_tokens': out['v_meta_tokens'], 'v_norm_mix_g': out['v_norm_mix_g'], 'v_w_in': out['v_w_in'], 'v_conv_w': out['v_conv_w'], 'v_conv_b': out['v_conv_b'], 'v_conv_ln_g': out['v_conv_ln_g'], 'v_conv_ln_b': out['v_conv_ln_b'], 'v_gla_w_gate2': out['v_gla_w_gate2'], 'v_gla_gate_b': out['v_gla_gate_b'], 'v_gla_norm_g': out['v_gla_norm_g'], 'v_w_out': out['v_w_out'], 'v_norm_ffn_g': out['v_norm_ffn_g'], 'v_w_ffn_gate': out['v_w_ffn_gate'], 'v_w_ffn_up': out['v_w_ffn_up'], 'v_w_ffn_down': out['v_w_ffn_down'], 'v_norm_final_g': out['v_norm_final_g']}


def _loss(weights, diff, rest, loss_target):
    with _jax.named_scope("forward"):
        args = {**rest, TWIN_DIFF_INPUT: diff, **{k: w.astype(_WEIGHT_DTYPES[k]) for k, w in weights.items()}}
        y = _forward(args)
    with _jax.named_scope("loss_head"):
        err = _jnp.square(y.astype(_jnp.float32) - loss_target)
        return 0.5 * _jnp.sum(_jnp.mean(err, axis=-1)) if err.ndim else 0.5 * err


def _adamw(w, g, m, v):
    m = ADAM_B1 * m + (1.0 - ADAM_B1) * g
    v = ADAM_B2 * v + (1.0 - ADAM_B2) * _jnp.square(g)
    m_hat = m / (1.0 - ADAM_B1 ** ADAM_STEP)
    v_hat = v / (1.0 - ADAM_B2 ** ADAM_STEP)
    delta = -ADAM_LR * (m_hat / (_jnp.sqrt(v_hat) + ADAM_EPS) + ADAM_WD * w)
    return delta, m, v


def reference(x, meta_tokens, norm_mix_g, w_in, conv_w, conv_b, conv_ln_g, conv_ln_b, gla_w_gate2, gla_gate_b, gla_norm_g, w_out, norm_ffn_g, w_ffn_gate, w_ffn_up, w_ffn_down, norm_final_g, loss_target, m_meta_tokens, m_norm_mix_g, m_w_in, m_conv_w, m_conv_b, m_conv_ln_g, m_conv_ln_b, m_gla_w_gate2, m_gla_gate_b, m_gla_norm_g, m_w_out, m_norm_ffn_g, m_w_ffn_gate, m_w_ffn_up, m_w_ffn_down, m_norm_final_g, v_meta_tokens, v_norm_mix_g, v_w_in, v_conv_w, v_conv_b, v_conv_ln_g, v_conv_ln_b, v_gla_w_gate2, v_gla_gate_b, v_gla_norm_g, v_w_out, v_norm_ffn_g, v_w_ffn_gate, v_w_ffn_up, v_w_ffn_down, v_norm_final_g):
    given = dict(x=x, meta_tokens=meta_tokens, norm_mix_g=norm_mix_g, w_in=w_in, conv_w=conv_w, conv_b=conv_b, conv_ln_g=conv_ln_g, conv_ln_b=conv_ln_b, gla_w_gate2=gla_w_gate2, gla_gate_b=gla_gate_b, gla_norm_g=gla_norm_g, w_out=w_out, norm_ffn_g=norm_ffn_g, w_ffn_gate=w_ffn_gate, w_ffn_up=w_ffn_up, w_ffn_down=w_ffn_down, norm_final_g=norm_final_g, loss_target=loss_target, m_meta_tokens=m_meta_tokens, m_norm_mix_g=m_norm_mix_g, m_w_in=m_w_in, m_conv_w=m_conv_w, m_conv_b=m_conv_b, m_conv_ln_g=m_conv_ln_g, m_conv_ln_b=m_conv_ln_b, m_gla_w_gate2=m_gla_w_gate2, m_gla_gate_b=m_gla_gate_b, m_gla_norm_g=m_gla_norm_g, m_w_out=m_w_out, m_norm_ffn_g=m_norm_ffn_g, m_w_ffn_gate=m_w_ffn_gate, m_w_ffn_up=m_w_ffn_up, m_w_ffn_down=m_w_ffn_down, m_norm_final_g=m_norm_final_g, v_meta_tokens=v_meta_tokens, v_norm_mix_g=v_norm_mix_g, v_w_in=v_w_in, v_conv_w=v_conv_w, v_conv_b=v_conv_b, v_conv_ln_g=v_conv_ln_g, v_conv_ln_b=v_conv_ln_b, v_gla_w_gate2=v_gla_w_gate2, v_gla_gate_b=v_gla_gate_b, v_gla_norm_g=v_gla_norm_g, v_w_out=v_w_out, v_norm_ffn_g=v_norm_ffn_g, v_w_ffn_gate=v_w_ffn_gate, v_w_ffn_up=v_w_ffn_up, v_w_ffn_down=v_w_ffn_down, v_norm_final_g=v_norm_final_g)
    weights = {n: given[n] for n in TWIN_WEIGHTS}
    shared = {n: given[n] for n in SHARED_INPUTS}
    per_example = {n: given[n] for n in ['x']}
    grad_fn = _jax.value_and_grad(_loss, argnums=(0, 1))

    def one_microbatch(ex, loss_target):
        ex = dict(ex)
        diff = ex.pop(TWIN_DIFF_INPUT)
        return grad_fn(weights, diff, {**shared, **ex}, loss_target)

    if N_MICROBATCH == 1:
        loss, (grad_w, grad_x) = one_microbatch(per_example, given["loss_target"])
    else:
        def body(carry, xs):
            loss_sum, grad_sum = carry
            l_k, (gw_k, gx_k) = one_microbatch(xs[0], xs[1])
            with _jax.named_scope("update"):
                return (loss_sum + l_k, _jax.tree.map(_jnp.add, grad_sum, gw_k)), gx_k

        init = (_jnp.zeros((), _jnp.float32), _jax.tree.map(_jnp.zeros_like, weights))
        (loss, grad_w), grad_x = _jax.lax.scan(body, init, (per_example, given["loss_target"]))
    with _jax.named_scope("update"):
        delta_w, new_m, new_v = {}, {}, {}
        for n in TWIN_WEIGHTS:
            delta_w[n], new_m[n], new_v[n] = _adamw(weights[n], grad_w[n], given["m_" + n], given["v_" + n])
    return (loss, grad_x, *[grad_w[n] for n in TWIN_WEIGHTS], *[delta_w[n] for n in TWIN_WEIGHTS],
            *[new_m[n] for n in TWIN_WEIGHTS], *[new_v[n] for n in TWIN_WEIGHTS])
```

```python
import functools

import jax
import jax.numpy as jnp
from jax import lax
from jax.experimental import pallas as pl
from jax.experimental.pallas import tpu as pltpu

F32 = jnp.float32
BF16 = jnp.bfloat16

D = 1024
N_META = 16
C_CONV = 512
CONV_W = 31
GLA_H = 4
GLA_DK = 64
GLA_DV = 128
GLA_K = GLA_H * GLA_DK
GLA_V = GLA_H * GLA_DV
RANK = 16
RANK_P = 128
TAU = 16.0
CHUNK = 64
LEAD = CHUNK
ZROWS = LEAD - N_META
D_IN = 2 * C_CONV + 2 * GLA_K + 2 * GLA_V + RANK
D_INP = D_IN - RANK + RANK_P
D_FF = 2816
FF_CHUNK = 1408
RMS_EPS = 1e-6
LN_EPS = 1e-5
N_DEV = 8

ADAM_LR = 0.001
ADAM_B1 = 0.9
ADAM_B2 = 0.999
ADAM_EPS = 1e-08
ADAM_WD = 0.01
ADAM_STEP = 10

VMEM_LIMIT = 60 * 1024 * 1024
MESH = pl.DeviceIdType.MESH

_NN = (((1,), (0,)), ((), ()))
_NT = (((1,), (1,)), ((), ()))
_TN = (((0,), (0,)), ((), ()))


def _dot(a, b, dims=_NN):
    return lax.dot_general(a, b, dims, preferred_element_type=F32)


def _sigmoid(x):
    return 1.0 / (1.0 + jnp.exp(-x))


def _row_tile(rows, target):
    best = None
    for t in range(16, min(rows, target) + 1, 16):
        if rows % t == 0:
            best = t
    assert best is not None, rows
    return best


def _params(sem=None):
    return pltpu.CompilerParams(dimension_semantics=sem, vmem_limit_bytes=VMEM_LIMIT)


def _whole_vmem():
    return pl.BlockSpec(memory_space=pltpu.VMEM)


def _rows(tm, width):
    return pl.BlockSpec((tm, width), lambda i: (i, 0))


def _fixed(shape):
    return pl.BlockSpec(shape, lambda *_: (0,) * len(shape))


def _fwd_inproj(h0, g1, w_in):
    rows = h0.shape[0]
    tm = _row_tile(rows, 528)

    def body(h_ref, g_ref, w_ref, uc_ref, qk_ref, vg_ref, lr_ref, n1_ref):
        h = h_ref[...]
        r = lax.rsqrt(jnp.mean(h * h, axis=-1, keepdims=True) + RMS_EPS)
        n = (h * r * g_ref[...]).astype(BF16)
        n1_ref[...] = n
        uc_ref[...] = _dot(n, w_ref[:, 0:1024])
        qk_ref[...] = _dot(n, w_ref[:, 1024:1536])
        vg_ref[...] = _dot(n, w_ref[:, 1536:2560])
        lr_ref[...] = _dot(n, w_ref[:, 2560:2688])

    return pl.pallas_call(
        body, name="fwd_inproj", grid=(rows // tm,),
        in_specs=[_rows(tm, D), _fixed((1, D)), _whole_vmem()],
        out_specs=[_rows(tm, 1024), _rows(tm, 512), _rows(tm, 1024), _rows(tm, RANK_P), _rows(tm, D)],
        out_shape=[jax.ShapeDtypeStruct((rows, 1024), F32), jax.ShapeDtypeStruct((rows, 512), F32),
                   jax.ShapeDtypeStruct((rows, 1024), F32), jax.ShapeDtypeStruct((rows, RANK_P), F32),
                   jax.ShapeDtypeStruct((rows, D), BF16)],
        compiler_params=_params(("parallel",)),
    )(h0, g1, w_in)


def _fwd_outproj(yc, yg, h0, w_out, g2):
    rows = h0.shape[0]
    tm = _row_tile(rows, 528)

    def body(yc_ref, yg_ref, h_ref, w_ref, g_ref, h1_ref, n2_ref):
        h1 = h_ref[...] + _dot(yc_ref[...], w_ref[0:C_CONV, :]) + _dot(yg_ref[...], w_ref[C_CONV:D, :])
        h1_ref[...] = h1
        r = lax.rsqrt(jnp.mean(h1 * h1, axis=-1, keepdims=True) + RMS_EPS)
        n2_ref[...] = (h1 * r * g_ref[...]).astype(BF16)

    return pl.pallas_call(
        body, name="fwd_outproj", grid=(rows // tm,),
        in_specs=[_rows(tm, C_CONV), _rows(tm, GLA_V), _rows(tm, D), _whole_vmem(), _fixed((1, D))],
        out_specs=[_rows(tm, D), _rows(tm, D)],
        out_shape=[jax.ShapeDtypeStruct((rows, D), F32), jax.ShapeDtypeStruct((rows, D), BF16)],
        compiler_params=_params(("parallel",)),
    )(yc, yg, h0, w_out, g2)


def _ffn_rows(h1, n2, tgt, wg, wu, wd, g2, g3, rows_per_example):
    rows = h1.shape[0]
    tm = _row_tile(rows, 352)
    n_ff = D_FF // FF_CHUNK

    def body(h1_ref, n2_ref, t_ref, wg_ref, wu_ref, wd_ref, g2_ref, g3_ref,
             f_ref, da_ref, db_ref, dh2_ref, dh1_ref, dh1b_ref, part_ref):
        i = pl.program_id(0)
        n2 = n2_ref[...]
        y2 = jnp.zeros((tm, D), F32)
        for c in range(n_ff):
            cs = slice(c * FF_CHUNK, (c + 1) * FF_CHUNK)
            a = _dot(n2, wg_ref[:, cs])
            b = _dot(n2, wu_ref[:, cs])
            f = (a * _sigmoid(a) * b).astype(BF16)
            f_ref[:, cs] = f
            da_ref[:, cs] = a.astype(BF16)
            db_ref[:, cs] = b.astype(BF16)
            y2 = y2 + _dot(f, wd_ref[cs, :])
        h1 = h1_ref[...]
        h2 = h1 + y2
        r3 = lax.rsqrt(jnp.mean(h2 * h2, axis=-1, keepdims=True) + RMS_EPS)
        xh3 = h2 * r3
        g3 = g3_ref[...]
        pos = (i * tm + lax.broadcasted_iota(jnp.int32, (tm, 1), 0)) % rows_per_example
        valid = pos >= LEAD
        err = jnp.where(valid, xh3 * g3 - t_ref[...], 0.0)
        loss = 0.5 / D * jnp.sum(jnp.sum(err * err, axis=-1, keepdims=True), axis=0, keepdims=True)
        dy = err * (1.0 / D)
        dg3 = jnp.sum(dy * xh3, axis=0, keepdims=True)
        dxh = dy * g3
        dh2 = r3 * (dxh - xh3 * jnp.mean(dxh * xh3, axis=-1, keepdims=True))
        dh2b = dh2.astype(BF16)
        dh2_ref[...] = dh2b
        dn2 = jnp.zeros((tm, D), F32)
        for c in range(n_ff):
            cs = slice(c * FF_CHUNK, (c + 1) * FF_CHUNK)
            df = _dot(dh2b, wd_ref[cs, :], _NT)
            a = da_ref[:, cs].astype(F32)
            b = db_ref[:, cs].astype(F32)
            sg = _sigmoid(a)
            da = (df * b * sg * (1.0 + a * (1.0 - sg))).astype(BF16)
            db = (df * a * sg).astype(BF16)
            da_ref[:, cs] = da
            db_ref[:, cs] = db
            dn2 = dn2 + _dot(da, wg_ref[:, cs], _NT) + _dot(db, wu_ref[:, cs], _NT)
        r2 = lax.rsqrt(jnp.mean(h1 * h1, axis=-1, keepdims=True) + RMS_EPS)
        xh2 = h1 * r2
        dg2 = jnp.sum(dn2 * xh2, axis=0, keepdims=True)
        dxh2 = dn2 * g2_ref[...]
        dh1 = dh2 + r2 * (dxh2 - xh2 * jnp.mean(dxh2 * xh2, axis=-1, keepdims=True))
        dh1_ref[...] = dh1
        dh1b_ref[...] = dh1.astype(BF16)

        @pl.when(i == 0)
        def _():
            part_ref[...] = jnp.zeros_like(part_ref)

        part_ref[0:1, :] += dg3
        part_ref[1:2, :] += dg2
        part_ref[2:3, :] += jnp.broadcast_to(loss, (1, D))

    return pl.pallas_call(
        body, name="ffn_rows", grid=(rows // tm,),
        in_specs=[_rows(tm, D), _rows(tm, D), _rows(tm, D), _whole_vmem(), _whole_vmem(), _whole_vmem(),
                  _fixed((1, D)), _fixed((1, D))],
        out_specs=[_rows(tm, D_FF), _rows(tm, D_FF), _rows(tm, D_FF), _rows(tm, D), _rows(tm, D), _rows(tm, D),
                   _fixed((8, D))],
        out_shape=[jax.ShapeDtypeStruct((rows, D_FF), BF16)] * 3
        + [jax.ShapeDtypeStruct((rows, D), BF16), jax.ShapeDtypeStruct((rows, D), F32),
           jax.ShapeDtypeStruct((rows, D), BF16), jax.ShapeDtypeStruct((8, D), F32)],
        compiler_params=_params(("arbitrary",)),
    )(h1, n2, tgt, wg, wu, wd, g2, g3)


def _bwd_outproj(dh1b, w_out):
    rows = dh1b.shape[0]
    tm = _row_tile(rows, 528)

    def body(d_ref, w_ref, dyc_ref, dyg_ref):
        d = d_ref[...]
        dyc_ref[...] = _dot(d, w_ref[0:C_CONV, :], _NT)
        dyg_ref[...] = _dot(d, w_ref[C_CONV:D, :], _NT)

    return pl.pallas_call(
        body, name="bwd_outproj", grid=(rows // tm,),
        in_specs=[_rows(tm, D), _whole_vmem()],
        out_specs=[_rows(tm, C_CONV), _rows(tm, GLA_V)],
        out_shape=[jax.ShapeDtypeStruct((rows, C_CONV), F32), jax.ShapeDtypeStruct((rows, GLA_V), F32)],
        compiler_params=_params(("parallel",)),
    )(dh1b, w_out)


def _bwd_inproj(duc, dqk, dvg, dlr, dh1, h0, w_in, g1):
    rows = h0.shape[0]
    tm = _row_tile(rows, 528)

    def body(duc_ref, dqk_ref, dvg_ref, dlr_ref, dh1_ref, h_ref, w_ref, g_ref, dh0_ref, part_ref):
        dn = (_dot(duc_ref[...], w_ref[:, 0:1024], _NT) + _dot(dqk_ref[...], w_ref[:, 1024:1536], _NT)
              + _dot(dvg_ref[...], w_ref[:, 1536:2560], _NT) + _dot(dlr_ref[...], w_ref[:, 2560:2688], _NT))
        h = h_ref[...]
        r = lax.rsqrt(jnp.mean(h * h, axis=-1, keepdims=True) + RMS_EPS)
        xh = h * r
        dg = jnp.sum(dn * xh, axis=0, keepdims=True)
        dxh = dn * g_ref[...]
        dh0_ref[...] = dh1_ref[...] + r * (dxh - xh * jnp.mean(dxh * xh, axis=-1, keepdims=True))

        @pl.when(pl.program_id(0) == 0)
        def _():
            part_ref[...] = jnp.zeros_like(part_ref)

        part_ref[0:1, :] += dg

    return pl.pallas_call(
        body, name="bwd_inproj", grid=(rows // tm,),
        in_specs=[_rows(tm, 1024), _rows(tm, 512), _rows(tm, 1024), _rows(tm, RANK_P), _rows(tm, D), _rows(tm, D),
                  _whole_vmem(), _fixed((1, D))],
        out_specs=[_rows(tm, D), _fixed((8, D))],
        out_shape=[jax.ShapeDtypeStruct((rows, D), F32), jax.ShapeDtypeStruct((8, D), F32)],
        compiler_params=_params(("arbitrary",)),
    )(duc, dqk, dvg, dlr, dh1, h0, w_in, g1)


def _matmul_tn(a, b, name):
    rows, m = a.shape
    n = b.shape[1]
    tk = _row_tile(rows, 528)
    tn = n if n <= 1024 else FF_CHUNK
    tm_ = m if m <= 1024 else FF_CHUNK
    assert n % tn == 0 and m % tm_ == 0
    nk = rows // tk

    def body(a_ref, b_ref, o_ref, acc_ref):
        k = pl.program_id(2)

        @pl.when(k == 0)
        def _():
            acc_ref[...] = jnp.zeros_like(acc_ref)

        acc_ref[...] += _dot(a_ref[...], b_ref[...], _TN)

        @pl.when(k == nk - 1)
        def _():
            o_ref[...] = acc_ref[...].astype(BF16)

    return pl.pallas_call(
        body, name=name, grid=(m // tm_, n // tn, nk),
        in_specs=[pl.BlockSpec((tk, tm_), lambda i, j, k: (k, i)), pl.BlockSpec((tk, tn), lambda i, j, k: (k, j))],
        out_specs=pl.BlockSpec((tm_, tn), lambda i, j, k: (i, j)),
        out_shape=jax.ShapeDtypeStruct((m, n), BF16),
        scratch_shapes=[pltpu.VMEM((tm_, tn), F32)],
        compiler_params=_params(("parallel", "parallel", "arbitrary")),
    )(a, b)


HALO = 32
LANES = 256


def _shift_up(win, k):
    if k == 0:
        return win[0:CHUNK]
    return pltpu.roll(win, CHUNK + HALO - k, 0)[0:CHUNK]


def _glu_into(uc_ref, vs_ref, n_chunk):
    vs_ref[0:CHUNK, :] = jnp.zeros((CHUNK, C_CONV), F32)

    def glu(i, carry):
        base = pl.multiple_of(i * CHUNK, CHUNK)
        val = uc_ref[pl.ds(base, CHUNK), 0:C_CONV]
        gate = uc_ref[pl.ds(base, CHUNK), C_CONV:2 * C_CONV]
        vs_ref[pl.ds(base + CHUNK, CHUNK), :] = val * _sigmoid(gate)
        return carry

    lax.fori_loop(0, n_chunk, glu, 0)


def _fwd_conv(uc, conv_w, conv_b, ln_g, ln_b, n_ex):
    rows = uc.shape[0]
    lp = rows // n_ex
    n_chunk = lp // CHUNK

    def body(uc_ref, w_ref, b_ref, lg_ref, lb_ref, ypre_ref, yc_ref, vs_ref):
        _glu_into(uc_ref, vs_ref, n_chunk)

        def conv(i, carry):
            base = pl.multiple_of(i * CHUNK, CHUNK)
            for lb in range(C_CONV // LANES):
                ls = slice(lb * LANES, (lb + 1) * LANES)
                win = vs_ref[pl.ds(base + CHUNK - HALO, CHUNK + HALO), ls]
                acc = jnp.broadcast_to(b_ref[:, ls], (CHUNK, LANES))
                for j in range(CONV_W):
                    acc = acc + w_ref[j:j + 1, ls] * _shift_up(win, HALO - (CONV_W - 1) + j)
                ypre_ref[pl.ds(base, CHUNK), ls] = acc
            y = ypre_ref[pl.ds(base, CHUNK), :]
            mu = jnp.mean(y, axis=-1, keepdims=True)
            yc_ = y - mu
            rstd = lax.rsqrt(jnp.mean(yc_ * yc_, axis=-1, keepdims=True) + LN_EPS)
            s = yc_ * rstd * lg_ref[...] + lb_ref[...]
            yc_ref[pl.ds(base, CHUNK), :] = (s * _sigmoid(s)).astype(BF16)
            return carry

        lax.fori_loop(0, n_chunk, conv, 0)

    ex = lambda w: pl.BlockSpec((lp, w), lambda b: (b, 0))
    return pl.pallas_call(
        body, name="fwd_conv", grid=(n_ex,),
        in_specs=[ex(2 * C_CONV), _fixed((CONV_W, C_CONV)), _fixed((1, C_CONV)), _fixed((1, C_CONV)), _fixed((1, C_CONV))],
        out_specs=[ex(C_CONV), ex(C_CONV)],
        out_shape=[jax.ShapeDtypeStruct((rows, C_CONV), F32), jax.ShapeDtypeStruct((rows, C_CONV), BF16)],
        scratch_shapes=[pltpu.VMEM((lp + CHUNK, C_CONV), F32)],
        compiler_params=_params(("parallel",)),
    )(uc, conv_w, conv_b, ln_g, ln_b)


def _bwd_conv(uc, ypre, dyc, conv_w, ln_g, ln_b, n_ex):
    rows = uc.shape[0]
    lp = rows // n_ex
    n_chunk = lp // CHUNK

    def body(uc_ref, ypre_ref, dyc_ref, w_ref, lg_ref, lb_ref, duc_ref, dw_ref, dvec_ref, vs_ref, dys_ref, dwacc_ref):
        _glu_into(uc_ref, vs_ref, n_chunk)
        dys_ref[pl.ds(lp, CHUNK), :] = jnp.zeros((CHUNK, C_CONV), F32)
        dwacc_ref[...] = jnp.zeros_like(dwacc_ref)

        def ln_bwd(i, carry):
            dcb, dlg, dlb = carry
            base = pl.multiple_of(i * CHUNK, CHUNK)
            y = ypre_ref[pl.ds(base, CHUNK), :]
            mu = jnp.mean(y, axis=-1, keepdims=True)
            yc_ = y - mu
            rstd = lax.rsqrt(jnp.mean(yc_ * yc_, axis=-1, keepdims=True) + LN_EPS)
            xh = yc_ * rstd
            s = xh * lg_ref[...] + lb_ref[...]
            sg = _sigmoid(s)
            ds = dyc_ref[pl.ds(base, CHUNK), :] * (sg * (1.0 + s * (1.0 - sg)))
            dxh = ds * lg_ref[...]
            dy = rstd * (dxh - jnp.mean(dxh, axis=-1, keepdims=True) - xh * jnp.mean(dxh * xh, axis=-1, keepdims=True))
            dys_ref[pl.ds(base, CHUNK), :] = dy
            return (dcb + jnp.sum(dy, axis=0, keepdims=True), dlg + jnp.sum(ds * xh, axis=0, keepdims=True),
                    dlb + jnp.sum(ds, axis=0, keepdims=True))

        zero = jnp.zeros((1, C_CONV), F32)
        dcb, dlg, dlb = lax.fori_loop(0, n_chunk, ln_bwd, (zero, zero, zero))

        @pl.when(pl.program_id(0) == 0)
        def _():
            dvec_ref[...] = jnp.zeros_like(dvec_ref)
            dw_ref[...] = jnp.zeros_like(dw_ref)

        dvec_ref[0:1, :] += dcb
        dvec_ref[1:2, :] += dlg
        dvec_ref[2:3, :] += dlb

        def taps(i, carry):
            base = pl.multiple_of(i * CHUNK, CHUNK)
            for lb in range(C_CONV // LANES):
                ls = slice(lb * LANES, (lb + 1) * LANES)
                dwin = dys_ref[pl.ds(base, CHUNK + HALO), ls]
                vwin = vs_ref[pl.ds(base + CHUNK - HALO, CHUNK + HALO), ls]
                dy = dwin[0:CHUNK]
                acc = jnp.zeros((CHUNK, LANES), F32)
                for j in range(CONV_W):
                    acc = acc + w_ref[j:j + 1, ls] * _shift_up(dwin, CONV_W - 1 - j)
                    prod = dy * _shift_up(vwin, HALO - (CONV_W - 1) + j)
                    dwacc_ref[8 * j:8 * j + 8, ls] += jnp.sum(prod.reshape(CHUNK // 8, 8, LANES), axis=0)
                val = uc_ref[pl.ds(base, CHUNK), ls]
                gate = uc_ref[pl.ds(base, CHUNK), C_CONV + lb * LANES:C_CONV + (lb + 1) * LANES]
                sg = _sigmoid(gate)
                duc_ref[pl.ds(base, CHUNK), ls] = (acc * sg).astype(BF16)
                duc_ref[pl.ds(base, CHUNK), C_CONV + lb * LANES:C_CONV + (lb + 1) * LANES] = (
                    acc * val * sg * (1.0 - sg)).astype(BF16)
            return carry

        lax.fori_loop(0, n_chunk, taps, 0)
        for j in range(CONV_W):
            dw_ref[j:j + 1, :] += jnp.sum(dwacc_ref[8 * j:8 * j + 8, :], axis=0, keepdims=True)

    ex = lambda w: pl.BlockSpec((lp, w), lambda b: (b, 0))
    return pl.pallas_call(
        body, name="bwd_conv", grid=(n_ex,),
        in_specs=[ex(2 * C_CONV), ex(C_CONV), ex(C_CONV), _fixed((CONV_W, C_CONV)), _fixed((1, C_CONV)), _fixed((1, C_CONV))],
        out_specs=[ex(2 * C_CONV), _fixed((32, C_CONV)), _fixed((8, C_CONV))],
        out_shape=[jax.ShapeDtypeStruct((rows, 2 * C_CONV), BF16), jax.ShapeDtypeStruct((32, C_CONV), F32),
                   jax.ShapeDtypeStruct((8, C_CONV), F32)],
        scratch_shapes=[pltpu.VMEM((lp + CHUNK, C_CONV), F32), pltpu.VMEM((lp + CHUNK, C_CONV), F32),
                        pltpu.VMEM((8 * 32, C_CONV), F32)],
        compiler_params=_params(("arbitrary",)),
    )(uc, ypre, dyc, conv_w, ln_g, ln_b)


def _seg_chunks(n_chunk):
    return max(c for c in (11, 3, 1) if n_chunk % c == 0)


def _head_masks():
    lane = lax.broadcasted_iota(jnp.int32, (1, GLA_K), 1)
    return [jnp.where(lane // GLA_DK == h, 1.0, 0.0).astype(F32) for h in range(GLA_H)]


def _cumsum_rows(x):
    row = lax.broadcasted_iota(jnp.int32, x.shape, 0)
    s = 1
    while s < CHUNK:
        x = x + jnp.where(row >= s, pltpu.roll(x, s, 0), 0.0)
        s *= 2
    return x


def _rev_cumsum_rows(x):
    row = lax.broadcasted_iota(jnp.int32, x.shape, 0)
    s = 1
    while s < CHUNK:
        x = x + jnp.where(row < CHUNK - s, pltpu.roll(x, CHUNK - s, 0), 0.0)
        s *= 2
    return x


def _gate_terms(lr_ref, w2_ref, gb_ref, rs, first_pos):
    z = _dot(lr_ref[rs, :].astype(BF16), w2_ref[...]) + gb_ref[...]
    la = (jnp.minimum(z, 0.0) - jnp.log(1.0 + jnp.exp(-jnp.abs(z)))) * (1.0 / TAU)
    pos = first_pos + lax.broadcasted_iota(jnp.int32, (CHUNK, 1), 0)
    live = pos >= ZROWS
    la = jnp.where(live, la, 0.0)
    return z, live, _cumsum_rows(la)


def _fwd_gla(qk, vg, lr, w2p, gb, ng, n_ex):
    rows = qk.shape[0]
    lp = rows // n_ex
    n_chunk = lp // CHUNK
    sc = _seg_chunks(n_chunk)
    n_seg = n_chunk // sc
    seg = sc * CHUNK

    def body(qk_ref, vg_ref, lr_ref, w2_ref, gb_ref, ng_ref, yg_ref, o_ref, st_ref, state_ref):
        sidx = pl.program_id(1)

        @pl.when(sidx == 0)
        def _():
            state_ref[...] = jnp.zeros_like(state_ref)

        masks = _head_masks()
        ri = lax.broadcasted_iota(jnp.int32, (CHUNK, CHUNK), 0)
        ci_ = lax.broadcasted_iota(jnp.int32, (CHUNK, CHUNK), 1)
        causal = ri >= ci_

        def chunk(ci, carry):
            base = pl.multiple_of(ci * CHUNK, CHUNK)
            rs = pl.ds(base, CHUNK)
            _, _, bcum = _gate_terms(lr_ref, w2_ref, gb_ref, rs, (sidx * sc + ci) * CHUNK)
            bl = bcum[CHUNK - 1:CHUNK, :]
            q = qk_ref[rs, 0:GLA_K]
            k = qk_ref[rs, GLA_K:2 * GLA_K]
            qt = q * (GLA_DK ** -0.5) * jnp.exp(bcum)
            kt = (k * jnp.exp(-bcum)).astype(BF16)
            kh = (k * jnp.exp(bl - bcum)).astype(BF16)
            vb = vg_ref[rs, 0:GLA_V].astype(BF16)
            st_ref[ci] = state_ref[...]
            for h in range(GLA_H):
                hs = slice(h * GLA_DV, (h + 1) * GLA_DV)
                qm = (qt * masks[h]).astype(BF16)
                a = jnp.where(causal, _dot(qm, kt, _NT), 0.0)
                o = _dot(a.astype(BF16), vb[:, hs]) + _dot(qm, state_ref[hs, :].astype(BF16), _NT)
                o_ref[rs, hs] = o
                ro = lax.rsqrt(jnp.mean(o * o, axis=-1, keepdims=True) + RMS_EPS)
                g = vg_ref[rs, GLA_V + h * GLA_DV:GLA_V + (h + 1) * GLA_DV]
                yg_ref[rs, hs] = (o * ro * ng_ref[...] * g * _sigmoid(g)).astype(BF16)
            state_ref[...] = state_ref[...] * jnp.exp(bl) + _dot(vb, kh, _TN)
            return carry

        lax.fori_loop(0, sc, chunk, 0)

    sg = lambda w: pl.BlockSpec((seg, w), lambda b, s: (b * n_seg + s, 0))
    return pl.pallas_call(
        body, name="fwd_gla", grid=(n_ex, n_seg),
        in_specs=[sg(2 * GLA_K), sg(2 * GLA_V), sg(RANK_P), _fixed((RANK_P, GLA_K)), _fixed((1, GLA_K)), _fixed((1, GLA_DV))],
        out_specs=[sg(GLA_V), sg(GLA_V), pl.BlockSpec((sc, GLA_V, GLA_K), lambda b, s: (b * n_seg + s, 0, 0))],
        out_shape=[jax.ShapeDtypeStruct((rows, GLA_V), BF16), jax.ShapeDtypeStruct((rows, GLA_V), F32),
                   jax.ShapeDtypeStruct((n_ex * n_chunk, GLA_V, GLA_K), F32)],
        scratch_shapes=[pltpu.VMEM((GLA_V, GLA_K), F32)],
        compiler_params=_params(("parallel", "arbitrary")),
    )(qk, vg, lr, w2p, gb, ng)


def _bwd_gla(qk, vg, lr, o, st, dyg, w2p, gb, ng, n_ex):
    rows = qk.shape[0]
    lp = rows // n_ex
    n_chunk = lp // CHUNK
    sc = _seg_chunks(n_chunk)
    n_seg = n_chunk // sc
    seg = sc * CHUNK

    def body(qk_ref, vg_ref, lr_ref, o_ref, st_ref, dyg_ref, w2_ref, gb_ref, ng_ref,
             dqk_ref, dvg_ref, dlr_ref, dw2_ref, dvec_ref, gt_ref, dz_ref):
        step = pl.program_id(1)
        sidx = n_seg - 1 - step

        @pl.when(step == 0)
        def _():
            gt_ref[...] = jnp.zeros_like(gt_ref)

        @pl.when((step == 0) & (pl.program_id(0) == 0))
        def _():
            dw2_ref[...] = jnp.zeros_like(dw2_ref)
            dvec_ref[...] = jnp.zeros_like(dvec_ref)

        masks = _head_masks()
        ri = lax.broadcasted_iota(jnp.int32, (CHUNK, CHUNK), 0)
        ci_ = lax.broadcasted_iota(jnp.int32, (CHUNK, CHUNK), 1)
        causal = ri >= ci_
        last_row = lax.broadcasted_iota(jnp.int32, (CHUNK, 1), 0) == CHUNK - 1
        ng = ng_ref[...]

        def chunk(ii, dng):
            ci = sc - 1 - ii
            base = pl.multiple_of(ci * CHUNK, CHUNK)
            rs = pl.ds(base, CHUNK)
            z, live, bcum = _gate_terms(lr_ref, w2_ref, gb_ref, rs, (sidx * sc + ci) * CHUNK)
            bl = bcum[CHUNK - 1:CHUNK, :]
            ebl = jnp.exp(bl)
            q = qk_ref[rs, 0:GLA_K]
            k = qk_ref[rs, GLA_K:2 * GLA_K]
            eb = jnp.exp(bcum)
            enb = jnp.exp(-bcum)
            ehb = jnp.exp(bl - bcum)
            qt = q * (GLA_DK ** -0.5) * eb
            kt = k * enb
            kh = k * ehb
            qtb = qt.astype(BF16)
            ktb = kt.astype(BF16)
            vb = vg_ref[rs, 0:GLA_V].astype(BF16)
            gt = gt_ref[...]
            gtb = gt.astype(BF16)
            s_in = st_ref[ci]
            dqt = jnp.zeros((CHUNK, GLA_K), F32)
            dkt = jnp.zeros((CHUNK, GLA_K), F32)
            dkh = jnp.zeros((CHUNK, GLA_K), F32)
            dbl = jnp.zeros((1, GLA_K), F32)
            dos = []
            for h in range(GLA_H):
                hs = slice(h * GLA_DV, (h + 1) * GLA_DV)
                gs = slice(GLA_V + h * GLA_DV, GLA_V + (h + 1) * GLA_DV)
                oh = o_ref[rs, hs]
                ro = lax.rsqrt(jnp.mean(oh * oh, axis=-1, keepdims=True) + RMS_EPS)
                on = oh * ro
                g = vg_ref[rs, gs]
                sg = _sigmoid(g)
                dout = dyg_ref[rs, hs]
                dvg_ref[rs, gs] = (dout * on * ng * (sg * (1.0 + g * (1.0 - sg)))).astype(BF16)
                dw = dout * g * sg
                dng = dng + jnp.sum(dw * on, axis=0, keepdims=True)
                don = dw * ng
                do = ro * (don - on * jnp.mean(don * on, axis=-1, keepdims=True))
                dob = do.astype(BF16)
                dos.append(dob)
                qm = (qt * masks[h]).astype(BF16)
                a = jnp.where(causal, _dot(qm, ktb, _NT), 0.0).astype(BF16)
                da = jnp.where(causal, _dot(dob, vb[:, hs], _NT), 0.0).astype(BF16)
                gth = gtb[hs, :]
                dv = _dot(a, dob, _TN) + _dot((kh * masks[h]).astype(BF16), gth, _NT)
                dvg_ref[rs, hs] = dv.astype(BF16)
                dkh = dkh + masks[h] * _dot(vb[:, hs], gth)
                dqt = dqt + masks[h] * (_dot(da, ktb) + _dot(dob, s_in[hs, :].astype(BF16)))
                dkt = dkt + masks[h] * _dot(da, qtb, _TN)
                dbl = dbl + masks[h] * jnp.sum(gt[hs, :] * s_in[hs, :], axis=0, keepdims=True)
            dbl = dbl * ebl + jnp.sum(dkh * kh, axis=0, keepdims=True)
            dqk_ref[rs, 0:GLA_K] = (dqt * (GLA_DK ** -0.5) * eb).astype(BF16)
            dqk_ref[rs, GLA_K:2 * GLA_K] = (dkt * enb + dkh * ehb).astype(BF16)
            db = dqt * qt - dkt * kt - dkh * kh
            db = jnp.where(last_row, db + dbl, db)
            dla = jnp.where(live, _rev_cumsum_rows(db), 0.0)
            dz_ref[rs, :] = dla * (1.0 / TAU) * (1.0 - _sigmoid(z))
            gt_ref[...] = _dot(jnp.concatenate(dos, axis=1), qtb, _TN) + gt * ebl
            return dng

        dng = lax.fori_loop(0, sc, chunk, jnp.zeros((1, GLA_DV), F32))
        dz = dz_ref[...]
        dzb = dz.astype(BF16)
        dlr_ref[...] = _dot(dzb, w2_ref[...], _NT).astype(BF16)
        dw2_ref[...] += _dot(lr_ref[...].astype(BF16), dzb, _TN)
        dvec_ref[0:1, :] += jnp.sum(dz, axis=0, keepdims=True)
        dvec_ref[1:2, 0:GLA_DV] += dng

    sg_ = lambda w: pl.BlockSpec((seg, w), lambda b, s: (b * n_seg + n_seg - 1 - s, 0))
    return pl.pallas_call(
        body, name="bwd_gla", grid=(n_ex, n_seg),
        in_specs=[sg_(2 * GLA_K), sg_(2 * GLA_V), sg_(RANK_P), sg_(GLA_V),
                  pl.BlockSpec((sc, GLA_V, GLA_K), lambda b, s: (b * n_seg + n_seg - 1 - s, 0, 0)), sg_(GLA_V),
                  _fixed((RANK_P, GLA_K)), _fixed((1, GLA_K)), _fixed((1, GLA_DV))],
        out_specs=[sg_(2 * GLA_K), sg_(2 * GLA_V), sg_(RANK_P), _fixed((RANK_P, GLA_K)), _fixed((8, GLA_K))],
        out_shape=[jax.ShapeDtypeStruct((rows, 2 * GLA_K), BF16), jax.ShapeDtypeStruct((rows, 2 * GLA_V), BF16),
                   jax.ShapeDtypeStruct((rows, RANK_P), BF16), jax.ShapeDtypeStruct((RANK_P, GLA_K), F32),
                   jax.ShapeDtypeStruct((8, GLA_K), F32)],
        scratch_shapes=[pltpu.VMEM((GLA_V, GLA_K), F32), pltpu.VMEM((seg, GLA_K), F32)],
        compiler_params=_params(("arbitrary", "arbitrary")),
    )(qk, vg, lr, o, st, dyg, w2p, gb, ng)


def _local_step(x, tgt, p):
    n_ex, seq, _ = x.shape
    lp = seq + LEAD
    rows = n_ex * lp
    meta = jnp.broadcast_to(p["meta"][None], (n_ex, N_META, D))
    h0 = jnp.concatenate([jnp.zeros((n_ex, ZROWS, D), F32), meta, x], axis=1).reshape(rows, D)
    tgt_p = jnp.pad(tgt, ((0, 0), (LEAD, 0), (0, 0))).reshape(rows, D)

    uc, qk, vg, lr, n1 = _fwd_inproj(h0, p["g1"], p["w_in"])
    ypre, yc = _fwd_conv(uc, p["conv_w"], p["conv_b"], p["ln_g"], p["ln_b"], n_ex)
    yg, o, st = _fwd_gla(qk, vg, lr, p["w2"], p["gb"], p["ng"], n_ex)
    h1, n2 = _fwd_outproj(yc, yg, h0, p["w_out"], p["g2"])
    f, da, db, dh2, dh1, dh1b, part = _ffn_rows(h1, n2, tgt_p, p["wg"], p["wu"], p["wd"], p["g2"], p["g3"], lp)
    g = {}
    g["wd"] = _matmul_tn(f, dh2, "dw_down")
    g["wg"] = _matmul_tn(n2, da, "dw_gate")
    g["wu"] = _matmul_tn(n2, db, "dw_up")
    dyc, dyg = _bwd_outproj(dh1b, p["w_out"])
    g["w_out"] = jnp.concatenate([_matmul_tn(yc, dh1b, "dw_out_conv"), _matmul_tn(yg, dh1b, "dw_out_gla")], axis=0)
    duc, dcw, dcvec = _bwd_conv(uc, ypre, dyc, p["conv_w"], p["ln_g"], p["ln_b"], n_ex)
    dqk, dvg, dlr, dw2, dgvec = _bwd_gla(qk, vg, lr, o, st, dyg, p["w2"], p["gb"], p["ng"], n_ex)
    dh0, part0 = _bwd_inproj(duc, dqk, dvg, dlr, dh1, h0, p["w_in"], p["g1"])
    g["w_in"] = jnp.concatenate(
        [_matmul_tn(n1, duc, "dw_in_conv"), _matmul_tn(n1, dqk, "dw_in_qk"), _matmul_tn(n1, dvg, "dw_in_vg"),
         _matmul_tn(n1, dlr, "dw_in_lr")[:, 0:RANK]], axis=1)
    dh0 = dh0.reshape(n_ex, lp, D)
    g["meta"] = jnp.sum(dh0[:, ZROWS:LEAD], axis=0)
    g["g1"] = part0[0:1]
    g["conv_w"] = dcw[0:CONV_W]
    g["conv_b"] = dcvec[0:1]
    g["ln_g"] = dcvec[1:2]
    g["ln_b"] = dcvec[2:3]
    g["w2"] = dw2[0:RANK]
    g["gb"] = dgvec[0:1]
    g["ng"] = dgvec[1:2, 0:GLA_DV]
    g["g2"] = part[1:2]
    g["g3"] = part[0:1]
    g["loss"] = part[2:3, 0:128]
    return dh0[:, LEAD:], g


W_IN_S = D_IN // N_DEV
W_OUT_S = D // N_DEV
FF_S = D_FF // N_DEV
CONV_S = C_CONV // N_DEV
GATE_S = GLA_K // N_DEV
BIG_ROWS = (D * W_IN_S + W_OUT_S * D + 3 * D * FF_S) // 128
BIG_PAD = 12288
UPD_TILE = 1536
SMALL_ROWS = 40


def _as_rows(a):
    return a.reshape(-1, 128)


def _halves(a32):
    return lax.bitcast_convert_type(a32, BF16).reshape(-1, 128)


def _unhalves(a16):
    lead = a16.shape[:-2]
    return lax.bitcast_convert_type(a16.reshape(*lead, a16.shape[-2] // 2, 128, 2), F32)


def _pack_big(w_in, w_out, wg, wu, wd, dtype):
    parts = [_as_rows(t.astype(dtype)) for t in (w_in, w_out, wg, wu, wd)]
    return jnp.concatenate(parts + [jnp.zeros((BIG_PAD - BIG_ROWS, 128), dtype)], axis=0)


def _unpack_big(p):
    o0 = D * W_IN_S // 128
    o1 = o0 + W_OUT_S * D // 128
    o2 = o1 + D * FF_S // 128
    o3 = o2 + D * FF_S // 128
    return (p[0:o0].reshape(1, D, W_IN_S), p[o0:o1].reshape(1, W_OUT_S, D), p[o1:o2].reshape(1, D, FF_S),
            p[o2:o3].reshape(1, D, FF_S), p[o3:BIG_ROWS].reshape(1, FF_S, D))


def _unpack_big_all(p):
    o0 = D * W_IN_S // 128
    o1 = o0 + W_OUT_S * D // 128
    o2 = o1 + D * FF_S // 128
    o3 = o2 + D * FF_S // 128
    return (_join_cols(p[:, 0:o0].reshape(N_DEV, D, W_IN_S)), p[:, o0:o1].reshape(D, D),
            _join_cols(p[:, o1:o2].reshape(N_DEV, D, FF_S)), _join_cols(p[:, o2:o3].reshape(N_DEV, D, FF_S)),
            p[:, o3:BIG_ROWS].reshape(D_FF, D))


def _unpack_small_sharded_all(s):
    return (_join_cols(s[:, 0:16]), _join_cols(s[:, 16:32].reshape(N_DEV, 32, CONV_S)[:, 0:CONV_W]),
            _join_cols(s[:, 32:36].reshape(N_DEV, RANK, GATE_S)))


def _pack_small_sharded(meta, conv_w, w2):
    cw = jnp.concatenate([conv_w, jnp.zeros((1, CONV_S), F32)], axis=0)
    return jnp.concatenate([meta, _as_rows(cw), _as_rows(w2), jnp.zeros((SMALL_ROWS - 36, 128), F32)], axis=0)


def _unpack_small_sharded(p):
    return p[0:16], p[16:32].reshape(32, CONV_S)[0:CONV_W], p[32:36].reshape(RANK, GATE_S)


_REPL = (("g1", D), ("conv_b", C_CONV), ("ln_g", C_CONV), ("ln_b", C_CONV), ("gb", GLA_K), ("ng", GLA_DV), ("g2", D), ("g3", D))


def _pack_small_repl(vals, last_row):
    return jnp.concatenate([_as_rows(vals[k]) for k, _ in _REPL] + [last_row], axis=0)


def _unpack_small_repl(p):
    out, r = {}, 0
    for k, n in _REPL:
        out[k] = p[r:r + n // 128].reshape(1, n)
        r += n // 128
    return out, p[r:r + 1]


def _split_cols(a, width):
    return a.reshape(a.shape[0], N_DEV, width).transpose(1, 0, 2)


def _join_cols(a):
    return a.transpose(1, 0, 2).reshape(a.shape[1], -1)


def _position():
    return lax.axis_index("x"), lax.axis_index("y"), lax.axis_index("c")


def _all_gather(xp):
    r = xp.shape[0]

    def body(x_ref, out_ref, send_sems, recv_sems, local_sem):
        x, y, c = _position()
        me, sibling = (x, y, c), (x, y, 1 - c)
        chips = [(1 - x, y), (x, 1 - y), (1 - x, 1 - y)]

        def blk(px, py, pc):
            return out_ref.at[4 * px + 2 * py + pc]

        def copy(k, block, to, src=None):
            return pltpu.make_async_remote_copy(
                src_ref=blk(*block) if src is None else src, dst_ref=blk(*block),
                send_sem=send_sems.at[k], recv_sem=recv_sems.at[k], device_id=to, device_id_type=MESH)

        mine = pltpu.make_async_copy(x_ref, blk(*me), local_sem)
        mine.start()
        first = [copy(0, me, sibling, src=x_ref)]
        first += [copy(1 + j, me, (*chip, c), src=x_ref) for j, chip in enumerate(chips)]
        for cp in first:
            cp.start()
        passed = [copy(4 + j, (*chip, c), sibling) for j, chip in enumerate(chips)]
        for j, chip in enumerate(chips):
            copy(1 + j, (*chip, c), me).wait_recv()
            passed[j].start()
        copy(0, sibling, me).wait_recv()
        for j, chip in enumerate(chips):
            copy(4 + j, (*chip, 1 - c), me).wait_recv()
        for cp in first + passed:
            cp.wait_send()
        mine.wait()

    return pl.pallas_call(
        body, name="all_gather",
        out_shape=jax.ShapeDtypeStruct((N_DEV, r, 128), xp.dtype),
        in_specs=[pl.BlockSpec(memory_space=pl.ANY)], out_specs=pl.BlockSpec(memory_space=pl.ANY),
        scratch_shapes=[pltpu.SemaphoreType.DMA((7,)), pltpu.SemaphoreType.DMA((7,)), pltpu.SemaphoreType.DMA(())],
    )(xp)


def _exchange(gp):
    r = gp.shape[1]

    def body(g_ref, out_ref, send_sems, recv_sems, local_sem):
        x, y, c = _position()
        me = 4 * x + 2 * y + c
        mine = pltpu.make_async_copy(g_ref.at[me], out_ref.at[me], local_sem)
        mine.start()
        copies = []
        for k in range(1, N_DEV):
            px = 1 - x if k & 4 else x
            py = 1 - y if k & 2 else y
            pc = 1 - c if k & 1 else c
            cp = pltpu.make_async_remote_copy(
                src_ref=g_ref.at[4 * px + 2 * py + pc], dst_ref=out_ref.at[me],
                send_sem=send_sems.at[k - 1], recv_sem=recv_sems.at[k - 1], device_id=(px, py, pc), device_id_type=MESH)
            cp.start()
            copies.append(cp)
        for cp in copies:
            cp.wait_recv()
        for cp in copies:
            cp.wait_send()
        mine.wait()

    return pl.pallas_call(
        body, name="exchange",
        out_shape=jax.ShapeDtypeStruct(gp.shape, gp.dtype),
        in_specs=[pl.BlockSpec(memory_space=pl.ANY)], out_specs=pl.BlockSpec(memory_space=pl.ANY),
        scratch_shapes=[pltpu.SemaphoreType.DMA((7,)), pltpu.SemaphoreType.DMA((7,)), pltpu.SemaphoreType.DMA(())],
    )(gp)


def _adamw(w, g, m, v):
    m = ADAM_B1 * m + (1.0 - ADAM_B1) * g
    v = ADAM_B2 * v + (1.0 - ADAM_B2) * (g * g)
    m_hat = m / (1.0 - ADAM_B1 ** ADAM_STEP)
    v_hat = v / (1.0 - ADAM_B2 ** ADAM_STEP)
    return -ADAM_LR * (m_hat / (jnp.sqrt(v_hat) + ADAM_EPS) + ADAM_WD * w), m, v


def _update(parts, w, m, v, tile, name):
    r = w.shape[0]

    def body(p_ref, w_ref, m_ref, v_ref, g_ref, d_ref, nm_ref, nv_ref):
        g = p_ref[0].astype(F32)
        for s in range(1, N_DEV):
            g = g + p_ref[s].astype(F32)
        g_ref[...] = g
        d_ref[...], nm_ref[...], nv_ref[...] = _adamw(w_ref[...], g, m_ref[...], v_ref[...])

    row = _rows(tile, 128)
    return pl.pallas_call(
        body, name=name, grid=(r // tile,),
        in_specs=[pl.BlockSpec((N_DEV, tile, 128), lambda i: (0, i, 0)), row, row, row],
        out_specs=[row] * 4, out_shape=[jax.ShapeDtypeStruct((r, 128), F32)] * 4,
        compiler_params=_params(("parallel",)),
    )(parts, w, m, v)


_WEIGHTS = ("meta_tokens", "norm_mix_g", "w_in", "conv_w", "conv_b", "conv_ln_g", "conv_ln_b", "gla_w_gate2", "gla_gate_b",
            "gla_norm_g", "w_out", "norm_ffn_g", "w_ffn_gate", "w_ffn_up", "w_ffn_down", "norm_final_g")
_SHORT = dict(norm_mix_g="g1", conv_b="conv_b", conv_ln_g="ln_g", conv_ln_b="ln_b", gla_gate_b="gb", gla_norm_g="ng",
              norm_ffn_g="g2", norm_final_g="g3")


def _packs(t):
    big = _pack_big(t["w_in"][0], t["w_out"][0], t["w_ffn_gate"][0], t["w_ffn_up"][0], t["w_ffn_down"][0], F32)
    sharded = _pack_small_sharded(t["meta_tokens"], t["conv_w"][0], t["gla_w_gate2"][0])
    repl = _pack_small_repl({s: t[n].reshape(1, -1) for n, s in _SHORT.items()}, jnp.zeros((1, 128), F32))
    return big, jnp.concatenate([sharded, repl], axis=0)


def _unpacks(big, small):
    t = {}
    t["w_in"], t["w_out"], t["w_ffn_gate"], t["w_ffn_up"], t["w_ffn_down"] = _unpack_big(big)
    meta, conv_w, w2 = _unpack_small_sharded(small[0:SMALL_ROWS])
    t["meta_tokens"], t["conv_w"], t["gla_w_gate2"] = meta, conv_w[None], w2[None]
    repl, last = _unpack_small_repl(small[SMALL_ROWS:2 * SMALL_ROWS])
    for n, s in _SHORT.items():
        t[n] = repl[s].reshape(-1) if n == "norm_final_g" else repl[s]
    return t, last


def kernel(x, meta_tokens, norm_mix_g, w_in, conv_w, conv_b, conv_ln_g, conv_ln_b, gla_w_gate2, gla_gate_b, gla_norm_g, w_out, norm_ffn_g, w_ffn_gate, w_ffn_up, w_ffn_down, norm_final_g, loss_target, m_meta_tokens, m_norm_mix_g, m_w_in, m_conv_w, m_conv_b, m_conv_ln_g, m_conv_ln_b, m_gla_w_gate2, m_gla_gate_b, m_gla_norm_g, m_w_out, m_norm_ffn_g, m_w_ffn_gate, m_w_ffn_up, m_w_ffn_down, m_norm_final_g, v_meta_tokens, v_norm_mix_g, v_w_in, v_conv_w, v_conv_b, v_conv_ln_g, v_conv_ln_b, v_gla_w_gate2, v_gla_gate_b, v_gla_norm_g, v_w_out, v_norm_ffn_g, v_w_ffn_gate, v_w_ffn_up, v_w_ffn_down, v_norm_final_g):
    given = dict(locals())
    w = {n: given[n] for n in _WEIGHTS}
    m = {n: given["m_" + n] for n in _WEIGHTS}
    v = {n: given["v_" + n] for n in _WEIGHTS}
    w_big, w_small = _packs(w)
    m_big, m_small = _packs(m)
    v_big, v_small = _packs(v)

    gathered = _all_gather(jnp.concatenate([w_big.astype(BF16), _halves(w_small[0:SMALL_ROWS])], axis=0))
    a_in, a_out, a_g, a_u, a_d = _unpack_big_all(gathered[:, 0:BIG_PAD])
    s_meta, s_conv, s_w2 = _unpack_small_sharded_all(_unhalves(gathered[:, BIG_PAD:]))
    p = dict(
        meta=s_meta, conv_w=s_conv,
        w2=jnp.concatenate([s_w2, jnp.zeros((RANK_P - RANK, GLA_K), F32)], axis=0).astype(BF16),
        w_in=jnp.concatenate([a_in, jnp.zeros((D, D_INP - D_IN), BF16)], axis=1), w_out=a_out, wg=a_g, wu=a_u, wd=a_d,
        g1=norm_mix_g, conv_b=conv_b, ln_g=conv_ln_g, ln_b=conv_ln_b, gb=gla_gate_b, ng=gla_norm_g, g2=norm_ffn_g,
        g3=norm_final_g.reshape(1, D))

    grad_x, g = _local_step(x, loss_target, p)

    big = jnp.concatenate(
        [_split_cols(g["w_in"], W_IN_S).reshape(N_DEV, -1, 128), g["w_out"].reshape(N_DEV, -1, 128),
         _split_cols(g["wg"], FF_S).reshape(N_DEV, -1, 128), _split_cols(g["wu"], FF_S).reshape(N_DEV, -1, 128),
         g["wd"].reshape(N_DEV, -1, 128), jnp.zeros((N_DEV, BIG_PAD - BIG_ROWS, 128), BF16)], axis=1)
    sharded = jnp.stack([_pack_small_sharded(a, b, c) for a, b, c in zip(
        _split_cols(g["meta"], 128), _split_cols(g["conv_w"], CONV_S), _split_cols(g["w2"], GATE_S))])
    repl = _pack_small_repl(g, g["loss"])
    small = jnp.concatenate([sharded, jnp.broadcast_to(repl[None], (N_DEV, SMALL_ROWS, 128))], axis=1)
    parts = _exchange(jnp.concatenate([big, _halves(small).reshape(N_DEV, -1, 128)], axis=1))

    out_big = _update(parts, w_big, m_big, v_big, UPD_TILE, "update_matrices")
    out_small = _update(_unhalves(parts[:, BIG_PAD:]), w_small, m_small, v_small, 2 * SMALL_ROWS, "update_vectors")
    fams = [_unpacks(b, s) for b, s in zip(out_big, out_small)]
    loss = fams[0][1][0, 0]
    outs = [loss, grad_x]
    for t, _ in fams:
        outs += [t[n] for n in _WEIGHTS]
    return tuple(outs)
```

```python
import functools

import jax
import jax.numpy as jnp
from jax import lax
from jax.experimental import pallas as pl
from jax.experimental.pallas import tpu as pltpu

F32 = jnp.float32
BF16 = jnp.bfloat16

D = 1024
N_META = 16
C_CONV = 512
CONV_W = 31
GLA_H = 4
GLA_DK = 64
GLA_DV = 128
GLA_K = GLA_H * GLA_DK
GLA_V = GLA_H * GLA_DV
RANK = 16
RANK_P = 128
TAU = 16.0
CHUNK = 64
LEAD = CHUNK
ZROWS = LEAD - N_META
D_IN = 2 * C_CONV + 2 * GLA_K + 2 * GLA_V + RANK
D_INP = D_IN - RANK + RANK_P
D_FF = 2816
FF_CHUNK = 1408
RMS_EPS = 1e-6
LN_EPS = 1e-5
N_DEV = 8

ADAM_LR = 0.001
ADAM_B1 = 0.9
ADAM_B2 = 0.999
ADAM_EPS = 1e-08
ADAM_WD = 0.01
ADAM_STEP = 10

VMEM_LIMIT = 60 * 1024 * 1024
MESH = pl.DeviceIdType.MESH

_NN = (((1,), (0,)), ((), ()))
_NT = (((1,), (1,)), ((), ()))
_TN = (((0,), (0,)), ((), ()))


def _dot(a, b, dims=_NN):
    return lax.dot_general(a, b, dims, preferred_element_type=F32)


def _sigmoid(x):
    return 1.0 / (1.0 + jnp.exp(-x))


def _row_tile(rows, target):
    best = None
    for t in range(16, min(rows, target) + 1, 16):
        if rows % t == 0:
            best = t
    assert best is not None, rows
    return best


def _params(sem=None):
    return pltpu.CompilerParams(dimension_semantics=sem, vmem_limit_bytes=VMEM_LIMIT)


def _whole_vmem():
    return pl.BlockSpec(memory_space=pltpu.VMEM)


def _rows(tm, width):
    return pl.BlockSpec((tm, width), lambda i: (i, 0))


def _fixed(shape):
    return pl.BlockSpec(shape, lambda *_: (0,) * len(shape))


def _fwd_inproj(h0, g1, w_in):
    rows = h0.shape[0]
    tm = _row_tile(rows, 528)

    def body(h_ref, g_ref, w_ref, uc_ref, qk_ref, vg_ref, lr_ref, n1_ref):
        h = h_ref[...]
        r = lax.rsqrt(jnp.mean(h * h, axis=-1, keepdims=True) + RMS_EPS)
        n = (h * r * g_ref[...]).astype(BF16)
        n1_ref[...] = n
        uc_ref[...] = _dot(n, w_ref[:, 0:1024])
        qk_ref[...] = _dot(n, w_ref[:, 1024:1536])
        vg_ref[...] = _dot(n, w_ref[:, 1536:2560])
        lr_ref[...] = _dot(n, w_ref[:, 2560:2688])

    return pl.pallas_call(
        body, name="fwd_inproj", grid=(rows // tm,),
        in_specs=[_rows(tm, D), _fixed((1, D)), _whole_vmem()],
        out_specs=[_rows(tm, 1024), _rows(tm, 512), _rows(tm, 1024), _rows(tm, RANK_P), _rows(tm, D)],
        out_shape=[jax.ShapeDtypeStruct((rows, 1024), F32), jax.ShapeDtypeStruct((rows, 512), F32),
                   jax.ShapeDtypeStruct((rows, 1024), F32), jax.ShapeDtypeStruct((rows, RANK_P), F32),
                   jax.ShapeDtypeStruct((rows, D), BF16)],
        compiler_params=_params(("parallel",)),
    )(h0, g1, w_in)


def _fwd_outproj(yc, yg, h0, w_out, g2):
    rows = h0.shape[0]
    tm = _row_tile(rows, 528)

    def body(yc_ref, yg_ref, h_ref, w_ref, g_ref, h1_ref, n2_ref):
        h1 = h_ref[...] + _dot(yc_ref[...], w_ref[0:C_CONV, :]) + _dot(yg_ref[...], w_ref[C_CONV:D, :])
        h1_ref[...] = h1
        r = lax.rsqrt(jnp.mean(h1 * h1, axis=-1, keepdims=True) + RMS_EPS)
        n2_ref[...] = (h1 * r * g_ref[...]).astype(BF16)

    return pl.pallas_call(
        body, name="fwd_outproj", grid=(rows // tm,),
        in_specs=[_rows(tm, C_CONV), _rows(tm, GLA_V), _rows(tm, D), _whole_vmem(), _fixed((1, D))],
        out_specs=[_rows(tm, D), _rows(tm, D)],
        out_shape=[jax.ShapeDtypeStruct((rows, D), F32), jax.ShapeDtypeStruct((rows, D), BF16)],
        compiler_params=_params(("parallel",)),
    )(yc, yg, h0, w_out, g2)


def _ffn_rows(h1, n2, tgt, wg, wu, wd, g2, g3, rows_per_example):
    rows = h1.shape[0]
    tm = _row_tile(rows, 352)
    n_ff = D_FF // FF_CHUNK

    def body(h1_ref, n2_ref, t_ref, wg_ref, wu_ref, wd_ref, g2_ref, g3_ref,
             f_ref, da_ref, db_ref, dh2_ref, dh1_ref, dh1b_ref, part_ref):
        i = pl.program_id(0)
        n2 = n2_ref[...]
        y2 = jnp.zeros((tm, D), F32)
        for c in range(n_ff):
            cs = slice(c * FF_CHUNK, (c + 1) * FF_CHUNK)
            a = _dot(n2, wg_ref[:, cs])
            b = _dot(n2, wu_ref[:, cs])
            f = (a * _sigmoid(a) * b).astype(BF16)
            f_ref[:, cs] = f
            da_ref[:, cs] = a.astype(BF16)
            db_ref[:, cs] = b.astype(BF16)
            y2 = y2 + _dot(f, wd_ref[cs, :])
        h1 = h1_ref[...]
        h2 = h1 + y2
        r3 = lax.rsqrt(jnp.mean(h2 * h2, axis=-1, keepdims=True) + RMS_EPS)
        xh3 = h2 * r3
        g3 = g3_ref[...]
        pos = (i * tm + lax.broadcasted_iota(jnp.int32, (tm, 1), 0)) % rows_per_example
        valid = pos >= LEAD
        err = jnp.where(valid, xh3 * g3 - t_ref[...], 0.0)
        loss = 0.5 / D * jnp.sum(jnp.sum(err * err, axis=-1, keepdims=True), axis=0, keepdims=True)
        dy = err * (1.0 / D)
        dg3 = jnp.sum(dy * xh3, axis=0, keepdims=True)
        dxh = dy * g3
        dh2 = r3 * (dxh - xh3 * jnp.mean(dxh * xh3, axis=-1, keepdims=True))
        dh2b = dh2.astype(BF16)
        dh2_ref[...] = dh2b
        dn2 = jnp.zeros((tm, D), F32)
        for c in range(n_ff):
            cs = slice(c * FF_CHUNK, (c + 1) * FF_CHUNK)
            df = _dot(dh2b, wd_ref[cs, :], _NT)
            a = da_ref[:, cs].astype(F32)
            b = db_ref[:, cs].astype(F32)
            sg = _sigmoid(a)
            da = (df * b * sg * (1.0 + a * (1.0 - sg))).astype(BF16)
            db = (df * a * sg).astype(BF16)
            da_ref[:, cs] = da
            db_ref[:, cs] = db
            dn2 = dn2 + _dot(da, wg_ref[:, cs], _NT) + _dot(db, wu_ref[:, cs], _NT)
        r2 = lax.rsqrt(jnp.mean(h1 * h1, axis=-1, keepdims=True) + RMS_EPS)
        xh2 = h1 * r2
        dg2 = jnp.sum(dn2 * xh2, axis=0, keepdims=True)
        dxh2 = dn2 * g2_ref[...]
        dh1 = dh2 + r2 * (dxh2 - xh2 * jnp.mean(dxh2 * xh2, axis=-1, keepdims=True))
        dh1_ref[...] = dh1
        dh1b_ref[...] = dh1.astype(BF16)

        @pl.when(i == 0)
        def _():
            part_ref[...] = jnp.zeros_like(part_ref)

        part_ref[0:1, :] += dg3
        part_ref[1:2, :] += dg2
        part_ref[2:3, :] += jnp.broadcast_to(loss, (1, D))

    return pl.pallas_call(
        body, name="ffn_rows", grid=(rows // tm,),
        in_specs=[_rows(tm, D), _rows(tm, D), _rows(tm, D), _whole_vmem(), _whole_vmem(), _whole_vmem(),
                  _fixed((1, D)), _fixed((1, D))],
        out_specs=[_rows(tm, D_FF), _rows(tm, D_FF), _rows(tm, D_FF), _rows(tm, D), _rows(tm, D), _rows(tm, D),
                   _fixed((8, D))],
        out_shape=[jax.ShapeDtypeStruct((rows, D_FF), BF16)] * 3
        + [jax.ShapeDtypeStruct((rows, D), BF16), jax.ShapeDtypeStruct((rows, D), F32),
           jax.ShapeDtypeStruct((rows, D), BF16), jax.ShapeDtypeStruct((8, D), F32)],
        compiler_params=_params(("arbitrary",)),
    )(h1, n2, tgt, wg, wu, wd, g2, g3)


def _bwd_outproj(dh1b, w_out):
    rows = dh1b.shape[0]
    tm = _row_tile(rows, 528)

    def body(d_ref, w_ref, dyc_ref, dyg_ref):
        d = d_ref[...]
        dyc_ref[...] = _dot(d, w_ref[0:C_CONV, :], _NT)
        dyg_ref[...] = _dot(d, w_ref[C_CONV:D, :], _NT)

    return pl.pallas_call(
        body, name="bwd_outproj", grid=(rows // tm,),
        in_specs=[_rows(tm, D), _whole_vmem()],
        out_specs=[_rows(tm, C_CONV), _rows(tm, GLA_V)],
        out_shape=[jax.ShapeDtypeStruct((rows, C_CONV), F32), jax.ShapeDtypeStruct((rows, GLA_V), F32)],
        compiler_params=_params(("parallel",)),
    )(dh1b, w_out)


def _bwd_inproj(duc, dqk, dvg, dlr, dh1, h0, w_in, g1, rows_per_example):
    rows = h0.shape[0]
    tm = _row_tile(rows_per_example, 528)
    tiles_per_example = rows_per_example // tm

    def body(duc_ref, dqk_ref, dvg_ref, dlr_ref, dh1_ref, h_ref, w_ref, g_ref, dh0_ref, part_ref, dmeta_ref):
        dn = (_dot(duc_ref[...], w_ref[:, 0:1024], _NT) + _dot(dqk_ref[...], w_ref[:, 1024:1536], _NT)
              + _dot(dvg_ref[...], w_ref[:, 1536:2560], _NT) + _dot(dlr_ref[...], w_ref[:, 2560:2688], _NT))
        h = h_ref[...]
        r = lax.rsqrt(jnp.mean(h * h, axis=-1, keepdims=True) + RMS_EPS)
        xh = h * r
        dg = jnp.sum(dn * xh, axis=0, keepdims=True)
        dxh = dn * g_ref[...]
        dh0 = dh1_ref[...] + r * (dxh - xh * jnp.mean(dxh * xh, axis=-1, keepdims=True))
        dh0_ref[...] = dh0
        i = pl.program_id(0)

        @pl.when(i == 0)
        def _():
            part_ref[...] = jnp.zeros_like(part_ref)
            dmeta_ref[...] = jnp.zeros_like(dmeta_ref)

        part_ref[0:1, :] += dg

        @pl.when(i % tiles_per_example == 0)
        def _():
            dmeta_ref[...] += dh0[ZROWS:LEAD, :]

    return pl.pallas_call(
        body, name="bwd_inproj", grid=(rows // tm,),
        in_specs=[_rows(tm, 1024), _rows(tm, 512), _rows(tm, 1024), _rows(tm, RANK_P), _rows(tm, D), _rows(tm, D),
                  _whole_vmem(), _fixed((1, D))],
        out_specs=[_rows(tm, D), _fixed((8, D)), _fixed((N_META, D))],
        out_shape=[jax.ShapeDtypeStruct((rows, D), F32), jax.ShapeDtypeStruct((8, D), F32),
                   jax.ShapeDtypeStruct((N_META, D), F32)],
        compiler_params=_params(("arbitrary",)),
    )(duc, dqk, dvg, dlr, dh1, h0, w_in, g1)


def _dw_blocked(a, bs, width, name):
    rows, m = a.shape
    ws = [b.shape[1] for b in bs]
    assert sum(ws) >= N_DEV * width
    tk = _row_tile(rows, 528)
    nk = rows // tk

    def body(a_ref, *refs):
        b_refs, o_ref, acc_ref = refs[:len(bs)], refs[len(bs)], refs[len(bs) + 1]
        k = pl.program_id(0)

        @pl.when(k == 0)
        def _():
            acc_ref[...] = jnp.zeros_like(acc_ref)

        at = a_ref[...].T
        off = 0
        for b_ref, w in zip(b_refs, ws):
            acc_ref[:, off:off + w] += _dot(at, b_ref[...])
            off += w

        @pl.when(k == nk - 1)
        def _():
            for d in range(N_DEV):
                o_ref[d] = acc_ref[:, d * width:(d + 1) * width].astype(BF16)

    return pl.pallas_call(
        body, name=name, grid=(nk,),
        in_specs=[_rows(tk, m)] + [_rows(tk, w) for w in ws],
        out_specs=_fixed((N_DEV, m, width)),
        out_shape=jax.ShapeDtypeStruct((N_DEV, m, width), BF16),
        scratch_shapes=[pltpu.VMEM((m, sum(ws)), F32)],
        compiler_params=_params(("arbitrary",)),
    )(a, *bs)


def _dw_out(yc, yg, dh1b):
    rows = yc.shape[0]
    tk = _row_tile(rows, 528)
    nk = rows // tk

    def body(yc_ref, yg_ref, d_ref, o_ref, acc_ref):
        k = pl.program_id(0)

        @pl.when(k == 0)
        def _():
            acc_ref[...] = jnp.zeros_like(acc_ref)

        d = d_ref[...]
        acc_ref[0:C_CONV, :] += _dot(yc_ref[...], d, _TN)
        acc_ref[C_CONV:D, :] += _dot(yg_ref[...], d, _TN)

        @pl.when(k == nk - 1)
        def _():
            o_ref[...] = acc_ref[...].astype(BF16)

    return pl.pallas_call(
        body, name="dw_out", grid=(nk,),
        in_specs=[_rows(tk, C_CONV), _rows(tk, GLA_V), _rows(tk, D)],
        out_specs=_fixed((D, D)), out_shape=jax.ShapeDtypeStruct((D, D), BF16),
        scratch_shapes=[pltpu.VMEM((D, D), F32)],
        compiler_params=_params(("arbitrary",)),
    )(yc, yg, dh1b)


def _matmul_tn(a, b, name):
    rows, m = a.shape
    n = b.shape[1]
    tk = _row_tile(rows, 528)
    tn = n if n <= 1024 else FF_CHUNK
    tm_ = m if m <= 1024 else FF_CHUNK
    assert n % tn == 0 and m % tm_ == 0
    nk = rows // tk

    def body(a_ref, b_ref, o_ref, acc_ref):
        k = pl.program_id(2)

        @pl.when(k == 0)
        def _():
            acc_ref[...] = jnp.zeros_like(acc_ref)

        acc_ref[...] += _dot(a_ref[...], b_ref[...], _TN)

        @pl.when(k == nk - 1)
        def _():
            o_ref[...] = acc_ref[...].astype(BF16)

    return pl.pallas_call(
        body, name=name, grid=(m // tm_, n // tn, nk),
        in_specs=[pl.BlockSpec((tk, tm_), lambda i, j, k: (k, i)), pl.BlockSpec((tk, tn), lambda i, j, k: (k, j))],
        out_specs=pl.BlockSpec((tm_, tn), lambda i, j, k: (i, j)),
        out_shape=jax.ShapeDtypeStruct((m, n), BF16),
        scratch_shapes=[pltpu.VMEM((tm_, tn), F32)],
        compiler_params=_params(("parallel", "parallel", "arbitrary")),
    )(a, b)


HALO = 32
LANES = 256


def _shift_up(win, k):
    if k == 0:
        return win[0:CHUNK]
    return pltpu.roll(win, CHUNK + HALO - k, 0)[0:CHUNK]


def _glu_into(uc_ref, vs_ref, n_chunk):
    vs_ref[0:CHUNK, :] = jnp.zeros((CHUNK, C_CONV), F32)

    def glu(i, carry):
        base = pl.multiple_of(i * CHUNK, CHUNK)
        val = uc_ref[pl.ds(base, CHUNK), 0:C_CONV]
        gate = uc_ref[pl.ds(base, CHUNK), C_CONV:2 * C_CONV]
        vs_ref[pl.ds(base + CHUNK, CHUNK), :] = val * _sigmoid(gate)
        return carry

    lax.fori_loop(0, n_chunk, glu, 0)


def _fwd_conv(uc, conv_w, conv_b, ln_g, ln_b, n_ex):
    rows = uc.shape[0]
    lp = rows // n_ex
    n_chunk = lp // CHUNK

    def body(uc_ref, w_ref, b_ref, lg_ref, lb_ref, ypre_ref, yc_ref, vs_ref):
        _glu_into(uc_ref, vs_ref, n_chunk)

        def conv(i, carry):
            base = pl.multiple_of(i * CHUNK, CHUNK)
            for lb in range(C_CONV // LANES):
                ls = slice(lb * LANES, (lb + 1) * LANES)
                win = vs_ref[pl.ds(base + CHUNK - HALO, CHUNK + HALO), ls]
                acc = jnp.broadcast_to(b_ref[:, ls], (CHUNK, LANES))
                for j in range(CONV_W):
                    acc = acc + w_ref[j:j + 1, ls] * _shift_up(win, HALO - (CONV_W - 1) + j)
                ypre_ref[pl.ds(base, CHUNK), ls] = acc
            y = ypre_ref[pl.ds(base, CHUNK), :]
            mu = jnp.mean(y, axis=-1, keepdims=True)
            yc_ = y - mu
            rstd = lax.rsqrt(jnp.mean(yc_ * yc_, axis=-1, keepdims=True) + LN_EPS)
            s = yc_ * rstd * lg_ref[...] + lb_ref[...]
            yc_ref[pl.ds(base, CHUNK), :] = (s * _sigmoid(s)).astype(BF16)
            return carry

        lax.fori_loop(0, n_chunk, conv, 0)

    ex = lambda w: pl.BlockSpec((lp, w), lambda b: (b, 0))
    return pl.pallas_call(
        body, name="fwd_conv", grid=(n_ex,),
        in_specs=[ex(2 * C_CONV), _fixed((32, C_CONV)), _fixed((1, C_CONV)), _fixed((1, C_CONV)), _fixed((1, C_CONV))],
        out_specs=[ex(C_CONV), ex(C_CONV)],
        out_shape=[jax.ShapeDtypeStruct((rows, C_CONV), F32), jax.ShapeDtypeStruct((rows, C_CONV), BF16)],
        scratch_shapes=[pltpu.VMEM((lp + CHUNK, C_CONV), F32)],
        compiler_params=_params(("parallel",)),
    )(uc, conv_w, conv_b, ln_g, ln_b)


def _bwd_conv(uc, ypre, dyc, conv_w, ln_g, ln_b, n_ex):
    rows = uc.shape[0]
    lp = rows // n_ex
    n_chunk = lp // CHUNK

    def body(uc_ref, ypre_ref, dyc_ref, w_ref, lg_ref, lb_ref, duc_ref, dw_ref, dvec_ref, vs_ref, dys_ref, dwacc_ref):
        _glu_into(uc_ref, vs_ref, n_chunk)
        dys_ref[pl.ds(lp, CHUNK), :] = jnp.zeros((CHUNK, C_CONV), F32)
        dwacc_ref[...] = jnp.zeros_like(dwacc_ref)

        def ln_bwd(i, carry):
            dcb, dlg, dlb = carry
            base = pl.multiple_of(i * CHUNK, CHUNK)
            y = ypre_ref[pl.ds(base, CHUNK), :]
            mu = jnp.mean(y, axis=-1, keepdims=True)
            yc_ = y - mu
            rstd = lax.rsqrt(jnp.mean(yc_ * yc_, axis=-1, keepdims=True) + LN_EPS)
            xh = yc_ * rstd
            s = xh * lg_ref[...] + lb_ref[...]
            sg = _sigmoid(s)
            ds = dyc_ref[pl.ds(base, CHUNK), :] * (sg * (1.0 + s * (1.0 - sg)))
            dxh = ds * lg_ref[...]
            dy = rstd * (dxh - jnp.mean(dxh, axis=-1, keepdims=True) - xh * jnp.mean(dxh * xh, axis=-1, keepdims=True))
            dys_ref[pl.ds(base, CHUNK), :] = dy
            return (dcb + jnp.sum(dy, axis=0, keepdims=True), dlg + jnp.sum(ds * xh, axis=0, keepdims=True),
                    dlb + jnp.sum(ds, axis=0, keepdims=True))

        zero = jnp.zeros((1, C_CONV), F32)
        dcb, dlg, dlb = lax.fori_loop(0, n_chunk, ln_bwd, (zero, zero, zero))

        @pl.when(pl.program_id(0) == 0)
        def _():
            dvec_ref[...] = jnp.zeros_like(dvec_ref)
            dw_ref[...] = jnp.zeros_like(dw_ref)

        dvec_ref[0:1, :] += dcb
        dvec_ref[1:2, :] += dlg
        dvec_ref[2:3, :] += dlb

        def taps(i, carry):
            base = pl.multiple_of(i * CHUNK, CHUNK)
            for lb in range(C_CONV // LANES):
                ls = slice(lb * LANES, (lb + 1) * LANES)
                dwin = dys_ref[pl.ds(base, CHUNK + HALO), ls]
                vwin = vs_ref[pl.ds(base + CHUNK - HALO, CHUNK + HALO), ls]
                dy = dwin[0:CHUNK]
                acc = jnp.zeros((CHUNK, LANES), F32)
                for j in range(CONV_W):
                    acc = acc + w_ref[j:j + 1, ls] * _shift_up(dwin, CONV_W - 1 - j)
                    prod = dy * _shift_up(vwin, HALO - (CONV_W - 1) + j)
                    dwacc_ref[8 * j:8 * j + 8, ls] += jnp.sum(prod.reshape(CHUNK // 8, 8, LANES), axis=0)
                val = uc_ref[pl.ds(base, CHUNK), ls]
                gate = uc_ref[pl.ds(base, CHUNK), C_CONV + lb * LANES:C_CONV + (lb + 1) * LANES]
                sg = _sigmoid(gate)
                duc_ref[pl.ds(base, CHUNK), ls] = (acc * sg).astype(BF16)
                duc_ref[pl.ds(base, CHUNK), C_CONV + lb * LANES:C_CONV + (lb + 1) * LANES] = (
                    acc * val * sg * (1.0 - sg)).astype(BF16)
            return carry

        lax.fori_loop(0, n_chunk, taps, 0)
        for j in range(CONV_W):
            dw_ref[j:j + 1, :] += jnp.sum(dwacc_ref[8 * j:8 * j + 8, :], axis=0, keepdims=True)

    ex = lambda w: pl.BlockSpec((lp, w), lambda b: (b, 0))
    return pl.pallas_call(
        body, name="bwd_conv", grid=(n_ex,),
        in_specs=[ex(2 * C_CONV), ex(C_CONV), ex(C_CONV), _fixed((32, C_CONV)), _fixed((1, C_CONV)), _fixed((1, C_CONV))],
        out_specs=[ex(2 * C_CONV), _fixed((32, C_CONV)), _fixed((8, C_CONV))],
        out_shape=[jax.ShapeDtypeStruct((rows, 2 * C_CONV), BF16), jax.ShapeDtypeStruct((32, C_CONV), F32),
                   jax.ShapeDtypeStruct((8, C_CONV), F32)],
        scratch_shapes=[pltpu.VMEM((lp + CHUNK, C_CONV), F32), pltpu.VMEM((lp + CHUNK, C_CONV), F32),
                        pltpu.VMEM((8 * 32, C_CONV), F32)],
        compiler_params=_params(("arbitrary",)),
    )(uc, ypre, dyc, conv_w, ln_g, ln_b)


def _seg_chunks(n_chunk):
    return max(c for c in (11, 3, 1) if n_chunk % c == 0)


def _head_masks():
    lane = lax.broadcasted_iota(jnp.int32, (1, GLA_K), 1)
    return [jnp.where(lane // GLA_DK == h, 1.0, 0.0).astype(F32) for h in range(GLA_H)]


def _cumsum_rows(x):
    row = lax.broadcasted_iota(jnp.int32, x.shape, 0)
    s = 1
    while s < CHUNK:
        x = x + jnp.where(row >= s, pltpu.roll(x, s, 0), 0.0)
        s *= 2
    return x


def _rev_cumsum_rows(x):
    row = lax.broadcasted_iota(jnp.int32, x.shape, 0)
    s = 1
    while s < CHUNK:
        x = x + jnp.where(row < CHUNK - s, pltpu.roll(x, CHUNK - s, 0), 0.0)
        s *= 2
    return x


def _gate_terms(lr_ref, w2_ref, gb_ref, rs, first_pos):
    z = _dot(lr_ref[rs, :].astype(BF16), w2_ref[...]) + gb_ref[...]
    la = (jnp.minimum(z, 0.0) - jnp.log(1.0 + jnp.exp(-jnp.abs(z)))) * (1.0 / TAU)
    pos = first_pos + lax.broadcasted_iota(jnp.int32, (CHUNK, 1), 0)
    live = pos >= ZROWS
    la = jnp.where(live, la, 0.0)
    return z, live, _cumsum_rows(la)


def _fwd_gla(qk, vg, lr, w2p, gb, ng, n_ex):
    rows = qk.shape[0]
    lp = rows // n_ex
    n_chunk = lp // CHUNK
    sc = _seg_chunks(n_chunk)
    n_seg = n_chunk // sc
    seg = sc * CHUNK

    def body(qk_ref, vg_ref, lr_ref, w2_ref, gb_ref, ng_ref, yg_ref, o_ref, st_ref, state_ref):
        sidx = pl.program_id(1)

        @pl.when(sidx == 0)
        def _():
            state_ref[...] = jnp.zeros_like(state_ref)

        masks = _head_masks()
        ri = lax.broadcasted_iota(jnp.int32, (CHUNK, CHUNK), 0)
        ci_ = lax.broadcasted_iota(jnp.int32, (CHUNK, CHUNK), 1)
        causal = ri >= ci_

        def chunk(ci, carry):
            base = pl.multiple_of(ci * CHUNK, CHUNK)
            rs = pl.ds(base, CHUNK)
            _, _, bcum = _gate_terms(lr_ref, w2_ref, gb_ref, rs, (sidx * sc + ci) * CHUNK)
            bl = bcum[CHUNK - 1:CHUNK, :]
            q = qk_ref[rs, 0:GLA_K]
            k = qk_ref[rs, GLA_K:2 * GLA_K]
            qt = q * (GLA_DK ** -0.5) * jnp.exp(bcum)
            kt = (k * jnp.exp(-bcum)).astype(BF16)
            kh = (k * jnp.exp(bl - bcum)).astype(BF16)
            vb = vg_ref[rs, 0:GLA_V].astype(BF16)
            st_ref[ci] = state_ref[...]
            for h in range(GLA_H):
                hs = slice(h * GLA_DV, (h + 1) * GLA_DV)
                qm = (qt * masks[h]).astype(BF16)
                a = jnp.where(causal, _dot(qm, kt, _NT), 0.0)
                o = _dot(a.astype(BF16), vb[:, hs]) + _dot(qm, state_ref[hs, :].astype(BF16), _NT)
                o_ref[rs, hs] = o
                ro = lax.rsqrt(jnp.mean(o * o, axis=-1, keepdims=True) + RMS_EPS)
                g = vg_ref[rs, GLA_V + h * GLA_DV:GLA_V + (h + 1) * GLA_DV]
                yg_ref[rs, hs] = (o * ro * ng_ref[...] * g * _sigmoid(g)).astype(BF16)
            state_ref[...] = state_ref[...] * jnp.exp(bl) + _dot(vb, kh, _TN)
            return carry

        lax.fori_loop(0, sc, chunk, 0)

    sg = lambda w: pl.BlockSpec((seg, w), lambda b, s: (b * n_seg + s, 0))
    return pl.pallas_call(
        body, name="fwd_gla", grid=(n_ex, n_seg),
        in_specs=[sg(2 * GLA_K), sg(2 * GLA_V), sg(RANK_P), _fixed((RANK_P, GLA_K)), _fixed((1, GLA_K)), _fixed((1, GLA_DV))],
        out_specs=[sg(GLA_V), sg(GLA_V), pl.BlockSpec((sc, GLA_V, GLA_K), lambda b, s: (b * n_seg + s, 0, 0))],
        out_shape=[jax.ShapeDtypeStruct((rows, GLA_V), BF16), jax.ShapeDtypeStruct((rows, GLA_V), F32),
                   jax.ShapeDtypeStruct((n_ex * n_chunk, GLA_V, GLA_K), F32)],
        scratch_shapes=[pltpu.VMEM((GLA_V, GLA_K), F32)],
        compiler_params=_params(("parallel", "arbitrary")),
    )(qk, vg, lr, w2p, gb, ng)


def _bwd_gla(qk, vg, lr, o, st, dyg, w2p, gb, ng, n_ex):
    rows = qk.shape[0]
    lp = rows // n_ex
    n_chunk = lp // CHUNK
    sc = _seg_chunks(n_chunk)
    n_seg = n_chunk // sc
    seg = sc * CHUNK

    def body(qk_ref, vg_ref, lr_ref, o_ref, st_ref, dyg_ref, w2_ref, gb_ref, ng_ref,
             dqk_ref, dvg_ref, dlr_ref, dw2_ref, dvec_ref, gt_ref, dz_ref):
        step = pl.program_id(1)
        sidx = n_seg - 1 - step

        @pl.when(step == 0)
        def _():
            gt_ref[...] = jnp.zeros_like(gt_ref)

        @pl.when((step == 0) & (pl.program_id(0) == 0))
        def _():
            dw2_ref[...] = jnp.zeros_like(dw2_ref)
            dvec_ref[...] = jnp.zeros_like(dvec_ref)

        masks = _head_masks()
        ri = lax.broadcasted_iota(jnp.int32, (CHUNK, CHUNK), 0)
        ci_ = lax.broadcasted_iota(jnp.int32, (CHUNK, CHUNK), 1)
        causal = ri >= ci_
        last_row = lax.broadcasted_iota(jnp.int32, (CHUNK, 1), 0) == CHUNK - 1
        ng = ng_ref[...]

        def chunk(ii, dng):
            ci = sc - 1 - ii
            base = pl.multiple_of(ci * CHUNK, CHUNK)
            rs = pl.ds(base, CHUNK)
            z, live, bcum = _gate_terms(lr_ref, w2_ref, gb_ref, rs, (sidx * sc + ci) * CHUNK)
            bl = bcum[CHUNK - 1:CHUNK, :]
            ebl = jnp.exp(bl)
            q = qk_ref[rs, 0:GLA_K]
            k = qk_ref[rs, GLA_K:2 * GLA_K]
            eb = jnp.exp(bcum)
            enb = jnp.exp(-bcum)
            ehb = jnp.exp(bl - bcum)
            qt = q * (GLA_DK ** -0.5) * eb
            kt = k * enb
            kh = k * ehb
            qtb = qt.astype(BF16)
            ktb = kt.astype(BF16)
            vb = vg_ref[rs, 0:GLA_V].astype(BF16)
            gt = gt_ref[...]
            gtb = gt.astype(BF16)
            s_in = st_ref[ci]
            dqt = jnp.zeros((CHUNK, GLA_K), F32)
            dkt = jnp.zeros((CHUNK, GLA_K), F32)
            dkh = jnp.zeros((CHUNK, GLA_K), F32)
            dbl = jnp.zeros((1, GLA_K), F32)
            dos = []
            for h in range(GLA_H):
                hs = slice(h * GLA_DV, (h + 1) * GLA_DV)
                gs = slice(GLA_V + h * GLA_DV, GLA_V + (h + 1) * GLA_DV)
                oh = o_ref[rs, hs]
                ro = lax.rsqrt(jnp.mean(oh * oh, axis=-1, keepdims=True) + RMS_EPS)
                on = oh * ro
                g = vg_ref[rs, gs]
                sg = _sigmoid(g)
                dout = dyg_ref[rs, hs]
                dvg_ref[rs, gs] = (dout * on * ng * (sg * (1.0 + g * (1.0 - sg)))).astype(BF16)
                dw = dout * g * sg
                dng = dng + jnp.sum(dw * on, axis=0, keepdims=True)
                don = dw * ng
                do = ro * (don - on * jnp.mean(don * on, axis=-1, keepdims=True))
                dob = do.astype(BF16)
                dos.append(dob)
                qm = (qt * masks[h]).astype(BF16)
                a = jnp.where(causal, _dot(qm, ktb, _NT), 0.0).astype(BF16)
                da = jnp.where(causal, _dot(dob, vb[:, hs], _NT), 0.0).astype(BF16)
                gth = gtb[hs, :]
                dv = _dot(a, dob, _TN) + _dot((kh * masks[h]).astype(BF16), gth, _NT)
                dvg_ref[rs, hs] = dv.astype(BF16)
                dkh = dkh + masks[h] * _dot(vb[:, hs], gth)
                dqt = dqt + masks[h] * (_dot(da, ktb) + _dot(dob, s_in[hs, :].astype(BF16)))
                dkt = dkt + masks[h] * _dot(da, qtb, _TN)
                dbl = dbl + masks[h] * jnp.sum(gt[hs, :] * s_in[hs, :], axis=0, keepdims=True)
            dbl = dbl * ebl + jnp.sum(dkh * kh, axis=0, keepdims=True)
            dqk_ref[rs, 0:GLA_K] = (dqt * (GLA_DK ** -0.5) * eb).astype(BF16)
            dqk_ref[rs, GLA_K:2 * GLA_K] = (dkt * enb + dkh * ehb).astype(BF16)
            db = dqt * qt - dkt * kt - dkh * kh
            db = jnp.where(last_row, db + dbl, db)
            dla = jnp.where(live, _rev_cumsum_rows(db), 0.0)
            dz_ref[rs, :] = dla * (1.0 / TAU) * (1.0 - _sigmoid(z))
            gt_ref[...] = _dot(jnp.concatenate(dos, axis=1), qtb, _TN) + gt * ebl
            return dng

        dng = lax.fori_loop(0, sc, chunk, jnp.zeros((1, GLA_DV), F32))
        dz = dz_ref[...]
        dzb = dz.astype(BF16)
        dlr_ref[...] = _dot(dzb, w2_ref[...], _NT).astype(BF16)
        dw2_ref[...] += _dot(lr_ref[...].astype(BF16), dzb, _TN)
        dvec_ref[0:1, :] += jnp.sum(dz, axis=0, keepdims=True)
        dvec_ref[1:2, 0:GLA_DV] += dng

    sg_ = lambda w: pl.BlockSpec((seg, w), lambda b, s: (b * n_seg + n_seg - 1 - s, 0))
    return pl.pallas_call(
        body, name="bwd_gla", grid=(n_ex, n_seg),
        in_specs=[sg_(2 * GLA_K), sg_(2 * GLA_V), sg_(RANK_P), sg_(GLA_V),
                  pl.BlockSpec((sc, GLA_V, GLA_K), lambda b, s: (b * n_seg + n_seg - 1 - s, 0, 0)), sg_(GLA_V),
                  _fixed((RANK_P, GLA_K)), _fixed((1, GLA_K)), _fixed((1, GLA_DV))],
        out_specs=[sg_(2 * GLA_K), sg_(2 * GLA_V), sg_(RANK_P), _fixed((RANK_P, GLA_K)), _fixed((8, GLA_K))],
        out_shape=[jax.ShapeDtypeStruct((rows, 2 * GLA_K), BF16), jax.ShapeDtypeStruct((rows, 2 * GLA_V), BF16),
                   jax.ShapeDtypeStruct((rows, RANK_P), BF16), jax.ShapeDtypeStruct((RANK_P, GLA_K), F32),
                   jax.ShapeDtypeStruct((8, GLA_K), F32)],
        scratch_shapes=[pltpu.VMEM((GLA_V, GLA_K), F32), pltpu.VMEM((seg, GLA_K), F32)],
        compiler_params=_params(("arbitrary", "arbitrary")),
    )(qk, vg, lr, o, st, dyg, w2p, gb, ng)


def _local_step(x, tgt, p):
    n_ex, seq, _ = x.shape
    lp = seq + LEAD
    rows = n_ex * lp
    meta = jnp.broadcast_to(p["meta"][None], (n_ex, N_META, D))
    h0 = jnp.concatenate([jnp.zeros((n_ex, ZROWS, D), F32), meta, x], axis=1).reshape(rows, D)
    tgt_p = jnp.pad(tgt, ((0, 0), (LEAD, 0), (0, 0))).reshape(rows, D)

    uc, qk, vg, lr, n1 = _fwd_inproj(h0, p["g1"], p["w_in"])
    ypre, yc = _fwd_conv(uc, p["conv_w"], p["conv_b"], p["ln_g"], p["ln_b"], n_ex)
    yg, o, st = _fwd_gla(qk, vg, lr, p["w2"], p["gb"], p["ng"], n_ex)
    h1, n2 = _fwd_outproj(yc, yg, h0, p["w_out"], p["g2"])
    f, da, db, dh2, dh1, dh1b, part = _ffn_rows(h1, n2, tgt_p, p["wg"], p["wu"], p["wd"], p["g2"], p["g3"], lp)
    g = {}
    g["wd"] = _matmul_tn(f, dh2, "dw_down").reshape(N_DEV, FF_S, D)
    g["wg"] = _dw_blocked(n2, [da], FF_S, "dw_gate")
    g["wu"] = _dw_blocked(n2, [db], FF_S, "dw_up")
    dyc, dyg = _bwd_outproj(dh1b, p["w_out"])
    g["w_out"] = _dw_out(yc, yg, dh1b).reshape(N_DEV, W_OUT_S, D)
    duc, g["conv_w"], g["conv_vec"] = _bwd_conv(uc, ypre, dyc, p["conv_w"], p["ln_g"], p["ln_b"], n_ex)
    dqk, dvg, dlr, g["w2"], g["gla_vec"] = _bwd_gla(qk, vg, lr, o, st, dyg, p["w2"], p["gb"], p["ng"], n_ex)
    dh0, g["in_vec"], g["meta"] = _bwd_inproj(duc, dqk, dvg, dlr, dh1, h0, p["w_in"], p["g1"], lp)
    g["w_in"] = _dw_blocked(n1, [duc, dqk, dvg, dlr], W_IN_S, "dw_in")
    g["ffn_vec"] = part
    return dh0.reshape(n_ex, lp, D)[:, LEAD:], g


W_IN_S = D_IN // N_DEV
W_OUT_S = D // N_DEV
FF_S = D_FF // N_DEV
CONV_S = C_CONV // N_DEV
GATE_S = GLA_K // N_DEV
SMALL_PACK = 64
CONV_ROW = 16
GATE_ROW = 48
VEC_ROWS = 16
_VEC_ROWS = (("norm_mix_g", D), ("conv_b", C_CONV), ("conv_ln_g", C_CONV), ("conv_ln_b", C_CONV), ("gla_gate_b", GLA_K),
             ("gla_norm_g", GLA_DV), ("norm_ffn_g", D), ("norm_final_g", D))
LOSS_ROW = len(_VEC_ROWS)


def _position():
    return lax.axis_index("x"), lax.axis_index("y"), lax.axis_index("c")


def _any():
    return pl.BlockSpec(memory_space=pl.ANY)


def _all_gather(mats, meta, conv_w, w2):
    n_mat = len(mats)
    n_t = n_mat + 1

    def body(*refs):
        ins = refs[0:n_mat]
        meta_ref, cw_ref, w2_ref = refs[n_mat:n_mat + 3]
        outs = refs[n_mat + 3:n_mat + 3 + n_t]
        stage = refs[n_mat + 3 + n_t:n_mat + 3 + 2 * n_t]
        send_sems, recv_sems, local_sems = refs[n_mat + 3 + 2 * n_t:]
        for s_ref, w_ref in zip(stage, ins):
            s_ref[...] = w_ref[...].astype(BF16)
        sp = stage[n_mat]
        sp[...] = jnp.zeros_like(sp)
        sp[0:N_META, :] = meta_ref[...]
        sp[CONV_ROW:CONV_ROW + CONV_W, 0:CONV_S] = cw_ref[...]
        sp[GATE_ROW:GATE_ROW + RANK, 0:GATE_S] = w2_ref[...]

        x, y, c = _position()
        me, sibling = (x, y, c), (x, y, 1 - c)
        chips = [(1 - x, y), (x, 1 - y), (1 - x, 1 - y)]

        def blk(t, p):
            return outs[t].at[4 * p[0] + 2 * p[1] + p[2]]

        def copy(t, k, block, to, staged=False):
            return pltpu.make_async_remote_copy(
                src_ref=stage[t] if staged else blk(t, block), dst_ref=blk(t, block),
                send_sem=send_sems.at[t, k], recv_sem=recv_sems.at[t, k], device_id=to, device_id_type=MESH)

        mine = [pltpu.make_async_copy(stage[t], blk(t, me), local_sems.at[t]) for t in range(n_t)]
        for cp in mine:
            cp.start()
        first = []
        for t in range(n_t):
            first.append(copy(t, 0, me, sibling, staged=True))
            first += [copy(t, 1 + j, me, (*chip, c), staged=True) for j, chip in enumerate(chips)]
        for cp in first:
            cp.start()
        passed = []
        for t in range(n_t):
            for j, chip in enumerate(chips):
                copy(t, 1 + j, (*chip, c), me).wait_recv()
                passed.append(copy(t, 4 + j, (*chip, c), sibling))
                passed[-1].start()
        for t in range(n_t):
            copy(t, 0, sibling, me).wait_recv()
            for j, chip in enumerate(chips):
                copy(t, 4 + j, (*chip, 1 - c), me).wait_recv()
        for cp in first + passed:
            cp.wait_send()
        for cp in mine:
            cp.wait()

    shapes = [(N_DEV,) + m.shape for m in mats]
    return pl.pallas_call(
        body, name="all_gather",
        out_shape=[jax.ShapeDtypeStruct(s, BF16) for s in shapes] + [jax.ShapeDtypeStruct((N_DEV, SMALL_PACK, 128), F32)],
        in_specs=[_whole_vmem()] * (n_mat + 3), out_specs=[_any()] * n_t,
        scratch_shapes=[pltpu.VMEM(m.shape, BF16) for m in mats] + [pltpu.VMEM((SMALL_PACK, 128), F32)]
        + [pltpu.SemaphoreType.DMA((n_t, 7)), pltpu.SemaphoreType.DMA((n_t, 7)), pltpu.SemaphoreType.DMA((n_t,))],
        compiler_params=pltpu.CompilerParams(vmem_limit_bytes=VMEM_LIMIT),
    )(*mats, meta, conv_w, w2)


def _unshard_in(a_in, a_small):
    def body(a_ref, s_ref, w_ref, meta_ref, cw_ref, w2_ref):
        w_ref[:, D_IN:D_INP] = jnp.zeros((D, D_INP - D_IN), BF16)
        w2_ref[...] = jnp.zeros_like(w2_ref)
        for d in range(N_DEV):
            w_ref[:, d * W_IN_S:(d + 1) * W_IN_S] = a_ref[d]
            meta_ref[:, d * 128:(d + 1) * 128] = s_ref[d, 0:N_META, :]
            cw_ref[:, d * CONV_S:(d + 1) * CONV_S] = s_ref[d, CONV_ROW:CONV_ROW + 32, 0:CONV_S]
            w2_ref[0:RANK, d * GATE_S:(d + 1) * GATE_S] = s_ref[d, GATE_ROW:GATE_ROW + RANK, 0:GATE_S].astype(BF16)

    return pl.pallas_call(
        body, name="unshard_in",
        out_shape=[jax.ShapeDtypeStruct((D, D_INP), BF16), jax.ShapeDtypeStruct((N_META, D), F32),
                   jax.ShapeDtypeStruct((32, C_CONV), F32), jax.ShapeDtypeStruct((RANK_P, GLA_K), BF16)],
        compiler_params=pltpu.CompilerParams(vmem_limit_bytes=VMEM_LIMIT),
    )(a_in, a_small)


def _unshard_ffn(a_g, a_u):
    def body(g_ref, u_ref, wg_ref, wu_ref):
        for d in range(N_DEV):
            wg_ref[:, d * FF_S:(d + 1) * FF_S] = g_ref[d]
            wu_ref[:, d * FF_S:(d + 1) * FF_S] = u_ref[d]

    return pl.pallas_call(
        body, name="unshard_ffn", out_shape=[jax.ShapeDtypeStruct((D, D_FF), BF16)] * 2,
        compiler_params=pltpu.CompilerParams(vmem_limit_bytes=VMEM_LIMIT),
    )(a_g, a_u)


def _exchange(mats, g):
    n_mat = len(mats)
    n_t = n_mat + 2

    def body(*refs):
        ins = refs[0:n_mat]
        meta_ref, cw_ref, w2_ref, in_vec, ffn_vec, conv_vec, gla_vec = refs[n_mat:n_mat + 7]
        outs = refs[n_mat + 7:n_mat + 7 + n_t]
        sp, vp, send_sems, recv_sems, local_sems = refs[n_mat + 7 + n_t:]
        sp[...] = jnp.zeros_like(sp)
        for d in range(N_DEV):
            sp[d, 0:N_META, :] = meta_ref[:, d * 128:(d + 1) * 128]
            sp[d, CONV_ROW:CONV_ROW + 32, 0:CONV_S] = cw_ref[:, d * CONV_S:(d + 1) * CONV_S]
            sp[d, GATE_ROW:GATE_ROW + RANK, 0:GATE_S] = w2_ref[0:RANK, d * GATE_S:(d + 1) * GATE_S]
        vp[...] = jnp.zeros_like(vp)
        vp[0:1, :] = in_vec[0:1, :]
        vp[1:4, 0:C_CONV] = conv_vec[0:3, :]
        vp[4:5, 0:GLA_K] = gla_vec[0:1, :]
        vp[5:6, 0:GLA_DV] = gla_vec[1:2, 0:GLA_DV]
        vp[6:7, :] = ffn_vec[1:2, :]
        vp[7:8, :] = ffn_vec[0:1, :]
        vp[LOSS_ROW:LOSS_ROW + 1, :] = ffn_vec[2:3, :]

        x, y, c = _position()
        me = 4 * x + 2 * y + c
        mine = [pltpu.make_async_copy(sp.at[me], outs[n_mat].at[me], local_sems.at[0]),
                pltpu.make_async_copy(vp, outs[n_mat + 1].at[me], local_sems.at[1])]
        for cp in mine:
            cp.start()
        copies = []
        for k in range(1, N_DEV):
            px = 1 - x if k & 4 else x
            py = 1 - y if k & 2 else y
            pc = 1 - c if k & 1 else c
            to = 4 * px + 2 * py + pc
            for t in range(n_t):
                src = ins[t].at[to] if t < n_mat else (sp.at[to] if t == n_mat else vp)
                copies.append(pltpu.make_async_remote_copy(
                    src_ref=src, dst_ref=outs[t].at[me], send_sem=send_sems.at[t, k - 1], recv_sem=recv_sems.at[t, k - 1],
                    device_id=(px, py, pc), device_id_type=MESH))
                copies[-1].start()
        for cp in copies:
            cp.wait_recv()
        for cp in copies:
            cp.wait_send()
        for cp in mine:
            cp.wait()

    return pl.pallas_call(
        body, name="exchange",
        out_shape=[jax.ShapeDtypeStruct(m.shape, BF16) for m in mats]
        + [jax.ShapeDtypeStruct((N_DEV, SMALL_PACK, 128), F32), jax.ShapeDtypeStruct((N_DEV, VEC_ROWS, D), F32)],
        in_specs=[_any()] * n_mat + [_whole_vmem()] * 7, out_specs=[_any()] * n_t,
        scratch_shapes=[pltpu.VMEM((N_DEV, SMALL_PACK, 128), F32), pltpu.VMEM((VEC_ROWS, D), F32),
                        pltpu.SemaphoreType.DMA((n_t, 7)), pltpu.SemaphoreType.DMA((n_t, 7)), pltpu.SemaphoreType.DMA((2,))],
    )(*mats, g["meta"], g["conv_w"], g["w2"], g["in_vec"], g["ffn_vec"], g["conv_vec"], g["gla_vec"])


def _adamw(w, g, m, v):
    m = ADAM_B1 * m + (1.0 - ADAM_B1) * g
    v = ADAM_B2 * v + (1.0 - ADAM_B2) * (g * g)
    m_hat = m / (1.0 - ADAM_B1 ** ADAM_STEP)
    v_hat = v / (1.0 - ADAM_B2 ** ADAM_STEP)
    return -ADAM_LR * (m_hat / (jnp.sqrt(v_hat) + ADAM_EPS) + ADAM_WD * w), m, v


def _update_matrix(recv, own, me, w, m, v, name):
    _, r, c = recv.shape
    tr = _row_tile(r, 256)

    def body(me_ref, recv_ref, own_ref, w_ref, m_ref, v_ref, g_ref, d_ref, nm_ref, nv_ref):
        g = jnp.zeros((tr, c), F32)
        for s in range(N_DEV):
            g = g + jnp.where(me_ref[0] == s, own_ref[...], recv_ref[s]).astype(F32)
        g_ref[...] = g
        d_ref[...], nm_ref[...], nv_ref[...] = _adamw(w_ref[...], g, m_ref[...], v_ref[...])

    one = pl.BlockSpec((None, tr, c), lambda i, me_ref: (0, i, 0))
    return pl.pallas_call(
        body, name=name,
        grid_spec=pltpu.PrefetchScalarGridSpec(
            num_scalar_prefetch=1, grid=(r // tr,),
            in_specs=[pl.BlockSpec((N_DEV, tr, c), lambda i, me_ref: (0, i, 0)),
                      pl.BlockSpec((None, tr, c), lambda i, me_ref: (me_ref[0], i, 0)), one, one, one],
            out_specs=[one] * 4),
        out_shape=[jax.ShapeDtypeStruct((1, r, c), F32)] * 4,
        compiler_params=_params(("parallel",)),
    )(me, recv, own, w, m, v)


_SMALL = ("meta_tokens", "conv_w", "gla_w_gate2") + tuple(n for n, _ in _VEC_ROWS)


def _update_small(srecv, vrecv, w, m, v):
    n = len(_SMALL)

    def body(*refs):
        s_ref, v_ref = refs[0:2]
        w_refs, m_refs, v_refs = refs[2:2 + n], refs[2 + n:2 + 2 * n], refs[2 + 2 * n:2 + 3 * n]
        outs = refs[2 + 3 * n:]
        ssum, vsum = s_ref[0], v_ref[0]
        for s in range(1, N_DEV):
            ssum = ssum + s_ref[s]
            vsum = vsum + v_ref[s]
        grads = [ssum[0:N_META, :], ssum[CONV_ROW:CONV_ROW + CONV_W, 0:CONV_S], ssum[GATE_ROW:GATE_ROW + RANK, 0:GATE_S]]
        grads += [vsum[i:i + 1, 0:width] for i, (_, width) in enumerate(_VEC_ROWS)]
        for i, g in enumerate(grads):
            d, nm, nv = _adamw(w_refs[i][...], g, m_refs[i][...], v_refs[i][...])
            outs[i][...] = g
            outs[n + i][...] = d
            outs[2 * n + i][...] = nm
            outs[3 * n + i][...] = nv
        outs[4 * n][...] = vsum[LOSS_ROW:LOSS_ROW + 1, 0:128]

    shapes = [jax.ShapeDtypeStruct(t.shape, F32) for t in w]
    res = pl.pallas_call(
        body, name="update_small", out_shape=shapes * 4 + [jax.ShapeDtypeStruct((1, 128), F32)],
    )(srecv, vrecv, *w, *m, *v)
    return res[0:n], res[n:2 * n], res[2 * n:3 * n], res[3 * n:4 * n], res[4 * n]


_WEIGHTS = ("meta_tokens", "norm_mix_g", "w_in", "conv_w", "conv_b", "conv_ln_g", "conv_ln_b", "gla_w_gate2", "gla_gate_b",
            "gla_norm_g", "w_out", "norm_ffn_g", "w_ffn_gate", "w_ffn_up", "w_ffn_down", "norm_final_g")
_MATRICES = ("w_in", "w_out", "w_ffn_gate", "w_ffn_up", "w_ffn_down")


def kernel(x, meta_tokens, norm_mix_g, w_in, conv_w, conv_b, conv_ln_g, conv_ln_b, gla_w_gate2, gla_gate_b, gla_norm_g, w_out, norm_ffn_g, w_ffn_gate, w_ffn_up, w_ffn_down, norm_final_g, loss_target, m_meta_tokens, m_norm_mix_g, m_w_in, m_conv_w, m_conv_b, m_conv_ln_g, m_conv_ln_b, m_gla_w_gate2, m_gla_gate_b, m_gla_norm_g, m_w_out, m_norm_ffn_g, m_w_ffn_gate, m_w_ffn_up, m_w_ffn_down, m_norm_final_g, v_meta_tokens, v_norm_mix_g, v_w_in, v_conv_w, v_conv_b, v_conv_ln_g, v_conv_ln_b, v_gla_w_gate2, v_gla_gate_b, v_gla_norm_g, v_w_out, v_norm_ffn_g, v_w_ffn_gate, v_w_ffn_up, v_w_ffn_down, v_norm_final_g):
    given = dict(locals())
    two_d = lambda a: a.reshape(1, -1) if a.ndim == 1 else a.reshape(a.shape[-2:])
    fams = [{n: given[pre + n] for n in _WEIGHTS} for pre in ("", "m_", "v_")]
    w = fams[0]

    a_in, a_out, a_g, a_u, a_d, a_small = _all_gather(
        [two_d(w[n]) for n in _MATRICES], w["meta_tokens"], two_d(w["conv_w"]), two_d(w["gla_w_gate2"]))
    w_in, meta, conv_taps, w2 = _unshard_in(a_in, a_small)
    wg, wu = _unshard_ffn(a_g, a_u)
    p = dict(meta=meta, conv_w=conv_taps, w2=w2, w_in=w_in, w_out=a_out.reshape(D, D), wg=wg, wu=wu, wd=a_d.reshape(D_FF, D),
             g1=norm_mix_g, conv_b=conv_b, ln_g=conv_ln_g, ln_b=conv_ln_b, gb=gla_gate_b, ng=gla_norm_g, g2=norm_ffn_g,
             g3=two_d(norm_final_g))

    grad_x, g = _local_step(x, loss_target, p)

    mine = [g["w_in"], g["w_out"], g["wg"], g["wu"], g["wd"]]
    *recv, srecv, vrecv = _exchange(mine, g)

    x_, y_, c_ = _position()
    me = (4 * x_ + 2 * y_ + c_).astype(jnp.int32).reshape(1)
    res = {}
    for n, r_, o_ in zip(_MATRICES, recv, mine):
        res[n] = _update_matrix(r_, o_, me, *[f[n] for f in fams], "update_" + n)
    small = _update_small(srecv, vrecv, *[[two_d(f[n]) for n in _SMALL] for f in fams])
    for i, n in enumerate(_SMALL):
        res[n] = [fam[i].reshape(w[n].shape) for fam in small[0:4]]
    outs = [small[4][0, 0], grad_x]
    for k in range(4):
        outs += [res[n][k] for n in _WEIGHTS]
    return tuple(outs)
```

```python
import functools

import jax
import jax.numpy as jnp
from jax import lax
from jax.experimental import pallas as pl
from jax.experimental.pallas import tpu as pltpu

F32 = jnp.float32
BF16 = jnp.bfloat16

D = 1024
N_META = 16
C_CONV = 512
CONV_W = 31
GLA_H = 4
GLA_DK = 64
GLA_DV = 128
GLA_K = GLA_H * GLA_DK
GLA_V = GLA_H * GLA_DV
RANK = 16
RANK_P = 128
TAU = 16.0
CHUNK = 64
LEAD = CHUNK
ZROWS = LEAD - N_META
D_IN = 2 * C_CONV + 2 * GLA_K + 2 * GLA_V + RANK
D_INP = D_IN - RANK + RANK_P
D_FF = 2816
FF_CHUNK = 1408
RMS_EPS = 1e-6
LN_EPS = 1e-5
N_DEV = 8

ADAM_LR = 0.001
ADAM_B1 = 0.9
ADAM_B2 = 0.999
ADAM_EPS = 1e-08
ADAM_WD = 0.01
ADAM_STEP = 10

VMEM_LIMIT = 60 * 1024 * 1024
MESH = pl.DeviceIdType.MESH

_NN = (((1,), (0,)), ((), ()))
_NT = (((1,), (1,)), ((), ()))
_TN = (((0,), (0,)), ((), ()))


def _dot(a, b, dims=_NN):
    return lax.dot_general(a, b, dims, preferred_element_type=F32)


def _sigmoid(x):
    return 1.0 / (1.0 + jnp.exp(-x))


def _row_tile(rows, target):
    best = None
    for t in range(16, min(rows, target) + 1, 16):
        if rows % t == 0:
            best = t
    assert best is not None, rows
    return best


def _params(sem=None):
    return pltpu.CompilerParams(dimension_semantics=sem, vmem_limit_bytes=VMEM_LIMIT)


def _whole_vmem():
    return pl.BlockSpec(memory_space=pltpu.VMEM)


def _rows(tm, width):
    return pl.BlockSpec((tm, width), lambda i: (i, 0))


def _fixed(shape):
    return pl.BlockSpec(shape, lambda *_: (0,) * len(shape))


def _fwd_inproj(h0, g1, w_in):
    rows = h0.shape[0]
    tm = _row_tile(rows, 528)

    def body(h_ref, g_ref, w_ref, uc_ref, qk_ref, vg_ref, lr_ref, n1_ref):
        h = h_ref[...]
        r = lax.rsqrt(jnp.mean(h * h, axis=-1, keepdims=True) + RMS_EPS)
        n = (h * r * g_ref[...]).astype(BF16)
        n1_ref[...] = n
        uc_ref[...] = _dot(n, w_ref[:, 0:1024])
        qk_ref[...] = _dot(n, w_ref[:, 1024:1536])
        vg_ref[...] = _dot(n, w_ref[:, 1536:2560])
        lr_ref[...] = _dot(n, w_ref[:, 2560:2688])

    return pl.pallas_call(
        body, name="fwd_inproj", grid=(rows // tm,),
        in_specs=[_rows(tm, D), _fixed((1, D)), _whole_vmem()],
        out_specs=[_rows(tm, 1024), _rows(tm, 512), _rows(tm, 1024), _rows(tm, RANK_P), _rows(tm, D)],
        out_shape=[jax.ShapeDtypeStruct((rows, 1024), F32), jax.ShapeDtypeStruct((rows, 512), F32),
                   jax.ShapeDtypeStruct((rows, 1024), F32), jax.ShapeDtypeStruct((rows, RANK_P), F32),
                   jax.ShapeDtypeStruct((rows, D), BF16)],
        compiler_params=_params(("parallel",)),
    )(h0, g1, w_in)


def _fwd_outproj(yc, yg, h0, w_out, g2):
    rows = h0.shape[0]
    tm = _row_tile(rows, 528)

    def body(yc_ref, yg_ref, h_ref, w_ref, g_ref, h1_ref, n2_ref):
        h1 = h_ref[...] + _dot(yc_ref[...], w_ref[0:C_CONV, :]) + _dot(yg_ref[...], w_ref[C_CONV:D, :])
        h1_ref[...] = h1
        r = lax.rsqrt(jnp.mean(h1 * h1, axis=-1, keepdims=True) + RMS_EPS)
        n2_ref[...] = (h1 * r * g_ref[...]).astype(BF16)

    return pl.pallas_call(
        body, name="fwd_outproj", grid=(rows // tm,),
        in_specs=[_rows(tm, C_CONV), _rows(tm, GLA_V), _rows(tm, D), _whole_vmem(), _fixed((1, D))],
        out_specs=[_rows(tm, D), _rows(tm, D)],
        out_shape=[jax.ShapeDtypeStruct((rows, D), F32), jax.ShapeDtypeStruct((rows, D), BF16)],
        compiler_params=_params(("parallel",)),
    )(yc, yg, h0, w_out, g2)


def _ffn_rows(h1, n2, tgt, wg, wu, wd, g2, g3, rows_per_example):
    rows = h1.shape[0]
    tm = _row_tile(rows, 352)
    n_ff = D_FF // FF_CHUNK

    def body(h1_ref, n2_ref, t_ref, wg_ref, wu_ref, wd_ref, g2_ref, g3_ref,
             f_ref, da_ref, db_ref, dh2_ref, dh1_ref, dh1b_ref, part_ref):
        i = pl.program_id(0)
        n2 = n2_ref[...]
        y2 = jnp.zeros((tm, D), F32)
        for c in range(n_ff):
            cs = slice(c * FF_CHUNK, (c + 1) * FF_CHUNK)
            a = _dot(n2, wg_ref[:, cs])
            b = _dot(n2, wu_ref[:, cs])
            f = (a * _sigmoid(a) * b).astype(BF16)
            f_ref[:, cs] = f
            da_ref[:, cs] = a.astype(BF16)
            db_ref[:, cs] = b.astype(BF16)
            y2 = y2 + _dot(f, wd_ref[cs, :])
        h1 = h1_ref[...]
        h2 = h1 + y2
        r3 = lax.rsqrt(jnp.mean(h2 * h2, axis=-1, keepdims=True) + RMS_EPS)
        xh3 = h2 * r3
        g3 = g3_ref[...]
        pos = (i * tm + lax.broadcasted_iota(jnp.int32, (tm, 1), 0)) % rows_per_example
        valid = pos >= LEAD
        err = jnp.where(valid, xh3 * g3 - t_ref[...], 0.0)
        loss = 0.5 / D * jnp.sum(jnp.sum(err * err, axis=-1, keepdims=True), axis=0, keepdims=True)
        dy = err * (1.0 / D)
        dg3 = jnp.sum(dy * xh3, axis=0, keepdims=True)
        dxh = dy * g3
        dh2 = r3 * (dxh - xh3 * jnp.mean(dxh * xh3, axis=-1, keepdims=True))
        dh2b = dh2.astype(BF16)
        dh2_ref[...] = dh2b
        dn2 = jnp.zeros((tm, D), F32)
        for c in range(n_ff):
            cs = slice(c * FF_CHUNK, (c + 1) * FF_CHUNK)
            df = _dot(dh2b, wd_ref[cs, :], _NT)
            a = da_ref[:, cs].astype(F32)
            b = db_ref[:, cs].astype(F32)
            sg = _sigmoid(a)
            da = (df * b * sg * (1.0 + a * (1.0 - sg))).astype(BF16)
            db = (df * a * sg).astype(BF16)
            da_ref[:, cs] = da
            db_ref[:, cs] = db
            dn2 = dn2 + _dot(da, wg_ref[:, cs], _NT) + _dot(db, wu_ref[:, cs], _NT)
        r2 = lax.rsqrt(jnp.mean(h1 * h1, axis=-1, keepdims=True) + RMS_EPS)
        xh2 = h1 * r2
        dg2 = jnp.sum(dn2 * xh2, axis=0, keepdims=True)
        dxh2 = dn2 * g2_ref[...]
        dh1 = dh2 + r2 * (dxh2 - xh2 * jnp.mean(dxh2 * xh2, axis=-1, keepdims=True))
        dh1_ref[...] = dh1
        dh1b_ref[...] = dh1.astype(BF16)

        @pl.when(i == 0)
        def _():
            part_ref[...] = jnp.zeros_like(part_ref)

        part_ref[0:1, :] += dg3
        part_ref[1:2, :] += dg2
        part_ref[2:3, :] += jnp.broadcast_to(loss, (1, D))

    return pl.pallas_call(
        body, name="ffn_rows", grid=(rows // tm,),
        in_specs=[_rows(tm, D), _rows(tm, D), _rows(tm, D), _whole_vmem(), _whole_vmem(), _whole_vmem(),
                  _fixed((1, D)), _fixed((1, D))],
        out_specs=[_rows(tm, D_FF), _rows(tm, D_FF), _rows(tm, D_FF), _rows(tm, D), _rows(tm, D), _rows(tm, D),
                   _fixed((8, D))],
        out_shape=[jax.ShapeDtypeStruct((rows, D_FF), BF16)] * 3
        + [jax.ShapeDtypeStruct((rows, D), BF16), jax.ShapeDtypeStruct((rows, D), F32),
           jax.ShapeDtypeStruct((rows, D), BF16), jax.ShapeDtypeStruct((8, D), F32)],
        compiler_params=_params(("arbitrary",)),
    )(h1, n2, tgt, wg, wu, wd, g2, g3)


def _bwd_outproj(dh1b, w_out, token):
    rows = dh1b.shape[0]
    tm = _row_tile(rows, 528)

    def body(d_ref, w_ref, token_ref, dyc_ref, dyg_ref):
        d = d_ref[...]
        dyc_ref[...] = _dot(d, w_ref[0:C_CONV, :], _NT)
        dyg_ref[...] = _dot(d, w_ref[C_CONV:D, :], _NT)

    return pl.pallas_call(
        body, name="bwd_outproj", grid=(rows // tm,),
        in_specs=[_rows(tm, D), _whole_vmem(), _fixed((8, 128))],
        out_specs=[_rows(tm, C_CONV), _rows(tm, GLA_V)],
        out_shape=[jax.ShapeDtypeStruct((rows, C_CONV), F32), jax.ShapeDtypeStruct((rows, GLA_V), F32)],
        compiler_params=_params(("parallel",)),
    )(dh1b, w_out, token)


def _bwd_inproj(duc, dqk, dvg, dlr, dh1, h0, w_in, g1, rows_per_example):
    rows = h0.shape[0]
    tm = _row_tile(rows_per_example, 528)
    tiles_per_example = rows_per_example // tm

    def body(duc_ref, dqk_ref, dvg_ref, dlr_ref, dh1_ref, h_ref, w_ref, g_ref, dh0_ref, part_ref, dmeta_ref):
        dn = (_dot(duc_ref[...], w_ref[:, 0:1024], _NT) + _dot(dqk_ref[...], w_ref[:, 1024:1536], _NT)
              + _dot(dvg_ref[...], w_ref[:, 1536:2560], _NT) + _dot(dlr_ref[...], w_ref[:, 2560:2688], _NT))
        h = h_ref[...]
        r = lax.rsqrt(jnp.mean(h * h, axis=-1, keepdims=True) + RMS_EPS)
        xh = h * r
        dg = jnp.sum(dn * xh, axis=0, keepdims=True)
        dxh = dn * g_ref[...]
        dh0 = dh1_ref[...] + r * (dxh - xh * jnp.mean(dxh * xh, axis=-1, keepdims=True))
        dh0_ref[...] = dh0
        i = pl.program_id(0)

        @pl.when(i == 0)
        def _():
            part_ref[...] = jnp.zeros_like(part_ref)
            dmeta_ref[...] = jnp.zeros_like(dmeta_ref)

        part_ref[0:1, :] += dg

        @pl.when(i % tiles_per_example == 0)
        def _():
            dmeta_ref[...] += dh0[ZROWS:LEAD, :]

    return pl.pallas_call(
        body, name="bwd_inproj", grid=(rows // tm,),
        in_specs=[_rows(tm, 1024), _rows(tm, 512), _rows(tm, 1024), _rows(tm, RANK_P), _rows(tm, D), _rows(tm, D),
                  _whole_vmem(), _fixed((1, D))],
        out_specs=[_rows(tm, D), _fixed((8, D)), _fixed((N_META, D))],
        out_shape=[jax.ShapeDtypeStruct((rows, D), F32), jax.ShapeDtypeStruct((8, D), F32),
                   jax.ShapeDtypeStruct((N_META, D), F32)],
        compiler_params=_params(("arbitrary",)),
    )(duc, dqk, dvg, dlr, dh1, h0, w_in, g1)


def _dw_blocked(a, bs, width, name):
    rows, m = a.shape
    ws = [b.shape[1] for b in bs]
    assert sum(ws) >= N_DEV * width
    tk = _row_tile(rows, 528)
    nk = rows // tk

    def body(a_ref, *refs):
        b_refs, o_ref, acc_ref = refs[:len(bs)], refs[len(bs)], refs[len(bs) + 1]
        k = pl.program_id(0)

        @pl.when(k == 0)
        def _():
            acc_ref[...] = jnp.zeros_like(acc_ref)

        at = a_ref[...].T
        off = 0
        for b_ref, w in zip(b_refs, ws):
            acc_ref[:, off:off + w] += _dot(at, b_ref[...])
            off += w

        @pl.when(k == nk - 1)
        def _():
            for d in range(N_DEV):
                o_ref[d] = acc_ref[:, d * width:(d + 1) * width].astype(BF16)

    return pl.pallas_call(
        body, name=name, grid=(nk,),
        in_specs=[_rows(tk, m)] + [_rows(tk, w) for w in ws],
        out_specs=_fixed((N_DEV, m, width)),
        out_shape=jax.ShapeDtypeStruct((N_DEV, m, width), BF16),
        scratch_shapes=[pltpu.VMEM((m, sum(ws)), F32)],
        compiler_params=_params(("arbitrary",)),
    )(a, *bs)


def _dw_out(yc, yg, dh1b):
    rows = yc.shape[0]
    tk = _row_tile(rows, 528)
    nk = rows // tk

    def body(yc_ref, yg_ref, d_ref, o_ref, acc_ref):
        k = pl.program_id(0)

        @pl.when(k == 0)
        def _():
            acc_ref[...] = jnp.zeros_like(acc_ref)

        d = d_ref[...]
        acc_ref[0:C_CONV, :] += _dot(yc_ref[...], d, _TN)
        acc_ref[C_CONV:D, :] += _dot(yg_ref[...], d, _TN)

        @pl.when(k == nk - 1)
        def _():
            o_ref[...] = acc_ref[...].astype(BF16)

    return pl.pallas_call(
        body, name="dw_out", grid=(nk,),
        in_specs=[_rows(tk, C_CONV), _rows(tk, GLA_V), _rows(tk, D)],
        out_specs=_fixed((D, D)), out_shape=jax.ShapeDtypeStruct((D, D), BF16),
        scratch_shapes=[pltpu.VMEM((D, D), F32)],
        compiler_params=_params(("arbitrary",)),
    )(yc, yg, dh1b)


def _matmul_tn(a, b, name):
    rows, m = a.shape
    n = b.shape[1]
    tk = _row_tile(rows, 528)
    tn = n if n <= 1024 else FF_CHUNK
    tm_ = m if m <= 1024 else FF_CHUNK
    assert n % tn == 0 and m % tm_ == 0
    nk = rows // tk

    def body(a_ref, b_ref, o_ref, acc_ref):
        k = pl.program_id(2)

        @pl.when(k == 0)
        def _():
            acc_ref[...] = jnp.zeros_like(acc_ref)

        acc_ref[...] += _dot(a_ref[...], b_ref[...], _TN)

        @pl.when(k == nk - 1)
        def _():
            o_ref[...] = acc_ref[...].astype(BF16)

    return pl.pallas_call(
        body, name=name, grid=(m // tm_, n // tn, nk),
        in_specs=[pl.BlockSpec((tk, tm_), lambda i, j, k: (k, i)), pl.BlockSpec((tk, tn), lambda i, j, k: (k, j))],
        out_specs=pl.BlockSpec((tm_, tn), lambda i, j, k: (i, j)),
        out_shape=jax.ShapeDtypeStruct((m, n), BF16),
        scratch_shapes=[pltpu.VMEM((tm_, tn), F32)],
        compiler_params=_params(("parallel", "parallel", "arbitrary")),
    )(a, b)


HALO = 32
LANES = 256


def _shift_up(win, k):
    if k == 0:
        return win[0:CHUNK]
    return pltpu.roll(win, CHUNK + HALO - k, 0)[0:CHUNK]


def _glu_into(uc_ref, vs_ref, n_chunk):
    vs_ref[0:CHUNK, :] = jnp.zeros((CHUNK, C_CONV), F32)

    def glu(i, carry):
        base = pl.multiple_of(i * CHUNK, CHUNK)
        val = uc_ref[pl.ds(base, CHUNK), 0:C_CONV]
        gate = uc_ref[pl.ds(base, CHUNK), C_CONV:2 * C_CONV]
        vs_ref[pl.ds(base + CHUNK, CHUNK), :] = val * _sigmoid(gate)
        return carry

    lax.fori_loop(0, n_chunk, glu, 0)


def _fwd_conv(uc, conv_w, conv_b, ln_g, ln_b, n_ex):
    rows = uc.shape[0]
    lp = rows // n_ex
    n_chunk = lp // CHUNK

    def body(uc_ref, w_ref, b_ref, lg_ref, lb_ref, ypre_ref, yc_ref, vs_ref):
        _glu_into(uc_ref, vs_ref, n_chunk)

        def conv(i, carry):
            base = pl.multiple_of(i * CHUNK, CHUNK)
            for lb in range(C_CONV // LANES):
                ls = slice(lb * LANES, (lb + 1) * LANES)
                win = vs_ref[pl.ds(base + CHUNK - HALO, CHUNK + HALO), ls]
                acc = jnp.broadcast_to(b_ref[:, ls], (CHUNK, LANES))
                for j in range(CONV_W):
                    acc = acc + w_ref[j:j + 1, ls] * _shift_up(win, HALO - (CONV_W - 1) + j)
                ypre_ref[pl.ds(base, CHUNK), ls] = acc
            y = ypre_ref[pl.ds(base, CHUNK), :]
            mu = jnp.mean(y, axis=-1, keepdims=True)
            yc_ = y - mu
            rstd = lax.rsqrt(jnp.mean(yc_ * yc_, axis=-1, keepdims=True) + LN_EPS)
            s = yc_ * rstd * lg_ref[...] + lb_ref[...]
            yc_ref[pl.ds(base, CHUNK), :] = (s * _sigmoid(s)).astype(BF16)
            return carry

        lax.fori_loop(0, n_chunk, conv, 0)

    ex = lambda w: pl.BlockSpec((lp, w), lambda b: (b, 0))
    return pl.pallas_call(
        body, name="fwd_conv", grid=(n_ex,),
        in_specs=[ex(2 * C_CONV), _fixed((32, C_CONV)), _fixed((1, C_CONV)), _fixed((1, C_CONV)), _fixed((1, C_CONV))],
        out_specs=[ex(C_CONV), ex(C_CONV)],
        out_shape=[jax.ShapeDtypeStruct((rows, C_CONV), F32), jax.ShapeDtypeStruct((rows, C_CONV), BF16)],
        scratch_shapes=[pltpu.VMEM((lp + CHUNK, C_CONV), F32)],
        compiler_params=_params(("parallel",)),
    )(uc, conv_w, conv_b, ln_g, ln_b)


def _bwd_conv(uc, ypre, dyc, conv_w, ln_g, ln_b, token, n_ex):
    rows = uc.shape[0]
    lp = rows // n_ex
    n_chunk = lp // CHUNK

    def body(uc_ref, ypre_ref, dyc_ref, w_ref, lg_ref, lb_ref, token_ref, duc_ref, dw_ref, dvec_ref, vs_ref, dys_ref,
             dwacc_ref):
        _glu_into(uc_ref, vs_ref, n_chunk)
        dys_ref[pl.ds(lp, CHUNK), :] = jnp.zeros((CHUNK, C_CONV), F32)
        dwacc_ref[...] = jnp.zeros_like(dwacc_ref)

        def ln_bwd(i, carry):
            dcb, dlg, dlb = carry
            base = pl.multiple_of(i * CHUNK, CHUNK)
            y = ypre_ref[pl.ds(base, CHUNK), :]
            mu = jnp.mean(y, axis=-1, keepdims=True)
            yc_ = y - mu
            rstd = lax.rsqrt(jnp.mean(yc_ * yc_, axis=-1, keepdims=True) + LN_EPS)
            xh = yc_ * rstd
            s = xh * lg_ref[...] + lb_ref[...]
            sg = _sigmoid(s)
            ds = dyc_ref[pl.ds(base, CHUNK), :] * (sg * (1.0 + s * (1.0 - sg)))
            dxh = ds * lg_ref[...]
            dy = rstd * (dxh - jnp.mean(dxh, axis=-1, keepdims=True) - xh * jnp.mean(dxh * xh, axis=-1, keepdims=True))
            dys_ref[pl.ds(base, CHUNK), :] = dy
            return (dcb + jnp.sum(dy, axis=0, keepdims=True), dlg + jnp.sum(ds * xh, axis=0, keepdims=True),
                    dlb + jnp.sum(ds, axis=0, keepdims=True))

        zero = jnp.zeros((1, C_CONV), F32)
        dcb, dlg, dlb = lax.fori_loop(0, n_chunk, ln_bwd, (zero, zero, zero))

        @pl.when(pl.program_id(0) == 0)
        def _():
            dvec_ref[...] = jnp.zeros_like(dvec_ref)
            dw_ref[...] = jnp.zeros_like(dw_ref)

        dvec_ref[0:1, :] += dcb
        dvec_ref[1:2, :] += dlg
        dvec_ref[2:3, :] += dlb

        def taps(i, carry):
            base = pl.multiple_of(i * CHUNK, CHUNK)
            for lb in range(C_CONV // LANES):
                ls = slice(lb * LANES, (lb + 1) * LANES)
                dwin = dys_ref[pl.ds(base, CHUNK + HALO), ls]
                vwin = vs_ref[pl.ds(base + CHUNK - HALO, CHUNK + HALO), ls]
                dy = dwin[0:CHUNK]
                acc = jnp.zeros((CHUNK, LANES), F32)
                for j in range(CONV_W):
                    acc = acc + w_ref[j:j + 1, ls] * _shift_up(dwin, CONV_W - 1 - j)
                    prod = dy * _shift_up(vwin, HALO - (CONV_W - 1) + j)
                    dwacc_ref[8 * j:8 * j + 8, ls] += jnp.sum(prod.reshape(CHUNK // 8, 8, LANES), axis=0)
                val = uc_ref[pl.ds(base, CHUNK), ls]
                gate = uc_ref[pl.ds(base, CHUNK), C_CONV + lb * LANES:C_CONV + (lb + 1) * LANES]
                sg = _sigmoid(gate)
                duc_ref[pl.ds(base, CHUNK), ls] = (acc * sg).astype(BF16)
                duc_ref[pl.ds(base, CHUNK), C_CONV + lb * LANES:C_CONV + (lb + 1) * LANES] = (
                    acc * val * sg * (1.0 - sg)).astype(BF16)
            return carry

        lax.fori_loop(0, n_chunk, taps, 0)
        for j in range(CONV_W):
            dw_ref[j:j + 1, :] += jnp.sum(dwacc_ref[8 * j:8 * j + 8, :], axis=0, keepdims=True)

    ex = lambda w: pl.BlockSpec((lp, w), lambda b: (b, 0))
    return pl.pallas_call(
        body, name="bwd_conv", grid=(n_ex,),
        in_specs=[ex(2 * C_CONV), ex(C_CONV), ex(C_CONV), _fixed((32, C_CONV)), _fixed((1, C_CONV)), _fixed((1, C_CONV)),
                  _fixed((8, 128))],
        out_specs=[ex(2 * C_CONV), _fixed((32, C_CONV)), _fixed((8, C_CONV))],
        out_shape=[jax.ShapeDtypeStruct((rows, 2 * C_CONV), BF16), jax.ShapeDtypeStruct((32, C_CONV), F32),
                   jax.ShapeDtypeStruct((8, C_CONV), F32)],
        scratch_shapes=[pltpu.VMEM((lp + CHUNK, C_CONV), F32), pltpu.VMEM((lp + CHUNK, C_CONV), F32),
                        pltpu.VMEM((8 * 32, C_CONV), F32)],
        compiler_params=_params(("arbitrary",)),
    )(uc, ypre, dyc, conv_w, ln_g, ln_b, token)


def _seg_chunks(n_chunk):
    return max(c for c in (11, 3, 1) if n_chunk % c == 0)


def _head_masks():
    lane = lax.broadcasted_iota(jnp.int32, (1, GLA_K), 1)
    return [jnp.where(lane // GLA_DK == h, 1.0, 0.0).astype(F32) for h in range(GLA_H)]


def _cumsum_rows(x):
    row = lax.broadcasted_iota(jnp.int32, x.shape, 0)
    s = 1
    while s < CHUNK:
        x = x + jnp.where(row >= s, pltpu.roll(x, s, 0), 0.0)
        s *= 2
    return x


def _rev_cumsum_rows(x):
    row = lax.broadcasted_iota(jnp.int32, x.shape, 0)
    s = 1
    while s < CHUNK:
        x = x + jnp.where(row < CHUNK - s, pltpu.roll(x, CHUNK - s, 0), 0.0)
        s *= 2
    return x


def _gate_terms(lr_ref, w2_ref, gb_ref, rs, first_pos):
    z = _dot(lr_ref[rs, :].astype(BF16), w2_ref[...]) + gb_ref[...]
    la = (jnp.minimum(z, 0.0) - jnp.log(1.0 + jnp.exp(-jnp.abs(z)))) * (1.0 / TAU)
    pos = first_pos + lax.broadcasted_iota(jnp.int32, (CHUNK, 1), 0)
    live = pos >= ZROWS
    la = jnp.where(live, la, 0.0)
    return z, live, _cumsum_rows(la)


def _fwd_gla(qk, vg, lr, w2p, gb, ng, n_ex):
    rows = qk.shape[0]
    lp = rows // n_ex
    n_chunk = lp // CHUNK
    sc = _seg_chunks(n_chunk)
    n_seg = n_chunk // sc
    seg = sc * CHUNK

    def body(qk_ref, vg_ref, lr_ref, w2_ref, gb_ref, ng_ref, yg_ref, o_ref, st_ref, state_ref):
        sidx = pl.program_id(1)

        @pl.when(sidx == 0)
        def _():
            state_ref[...] = jnp.zeros_like(state_ref)

        masks = _head_masks()
        ri = lax.broadcasted_iota(jnp.int32, (CHUNK, CHUNK), 0)
        ci_ = lax.broadcasted_iota(jnp.int32, (CHUNK, CHUNK), 1)
        causal = ri >= ci_

        def chunk(ci, carry):
            base = pl.multiple_of(ci * CHUNK, CHUNK)
            rs = pl.ds(base, CHUNK)
            _, _, bcum = _gate_terms(lr_ref, w2_ref, gb_ref, rs, (sidx * sc + ci) * CHUNK)
            bl = bcum[CHUNK - 1:CHUNK, :]
            q = qk_ref[rs, 0:GLA_K]
            k = qk_ref[rs, GLA_K:2 * GLA_K]
            qt = q * (GLA_DK ** -0.5) * jnp.exp(bcum)
            kt = (k * jnp.exp(-bcum)).astype(BF16)
            kh = (k * jnp.exp(bl - bcum)).astype(BF16)
            vb = vg_ref[rs, 0:GLA_V].astype(BF16)
            st_ref[ci] = state_ref[...]
            for h in range(GLA_H):
                hs = slice(h * GLA_DV, (h + 1) * GLA_DV)
                qm = (qt * masks[h]).astype(BF16)
                a = jnp.where(causal, _dot(qm, kt, _NT), 0.0)
                o = _dot(a.astype(BF16), vb[:, hs]) + _dot(qm, state_ref[hs, :].astype(BF16), _NT)
                o_ref[rs, hs] = o
                ro = lax.rsqrt(jnp.mean(o * o, axis=-1, keepdims=True) + RMS_EPS)
                g = vg_ref[rs, GLA_V + h * GLA_DV:GLA_V + (h + 1) * GLA_DV]
                yg_ref[rs, hs] = (o * ro * ng_ref[...] * g * _sigmoid(g)).astype(BF16)
            state_ref[...] = state_ref[...] * jnp.exp(bl) + _dot(vb, kh, _TN)
            return carry

        lax.fori_loop(0, sc, chunk, 0)

    sg = lambda w: pl.BlockSpec((seg, w), lambda b, s: (b * n_seg + s, 0))
    return pl.pallas_call(
        body, name="fwd_gla", grid=(n_ex, n_seg),
        in_specs=[sg(2 * GLA_K), sg(2 * GLA_V), sg(RANK_P), _fixed((RANK_P, GLA_K)), _fixed((1, GLA_K)), _fixed((1, GLA_DV))],
        out_specs=[sg(GLA_V), sg(GLA_V), pl.BlockSpec((sc, GLA_V, GLA_K), lambda b, s: (b * n_seg + s, 0, 0))],
        out_shape=[jax.ShapeDtypeStruct((rows, GLA_V), BF16), jax.ShapeDtypeStruct((rows, GLA_V), F32),
                   jax.ShapeDtypeStruct((n_ex * n_chunk, GLA_V, GLA_K), F32)],
        scratch_shapes=[pltpu.VMEM((GLA_V, GLA_K), F32)],
        compiler_params=_params(("parallel", "arbitrary")),
    )(qk, vg, lr, w2p, gb, ng)


def _bwd_gla(qk, vg, lr, o, st, dyg, w2p, gb, ng, n_ex):
    rows = qk.shape[0]
    lp = rows // n_ex
    n_chunk = lp // CHUNK
    sc = _seg_chunks(n_chunk)
    n_seg = n_chunk // sc
    seg = sc * CHUNK

    def body(qk_ref, vg_ref, lr_ref, o_ref, st_ref, dyg_ref, w2_ref, gb_ref, ng_ref,
             dqk_ref, dvg_ref, dlr_ref, dw2_ref, dvec_ref, gt_ref, dz_ref):
        step = pl.program_id(1)
        sidx = n_seg - 1 - step

        @pl.when(step == 0)
        def _():
            gt_ref[...] = jnp.zeros_like(gt_ref)

        @pl.when((step == 0) & (pl.program_id(0) == 0))
        def _():
            dw2_ref[...] = jnp.zeros_like(dw2_ref)
            dvec_ref[...] = jnp.zeros_like(dvec_ref)

        masks = _head_masks()
        ri = lax.broadcasted_iota(jnp.int32, (CHUNK, CHUNK), 0)
        ci_ = lax.broadcasted_iota(jnp.int32, (CHUNK, CHUNK), 1)
        causal = ri >= ci_
        last_row = lax.broadcasted_iota(jnp.int32, (CHUNK, 1), 0) == CHUNK - 1
        ng = ng_ref[...]

        def chunk(ii, dng):
            ci = sc - 1 - ii
            base = pl.multiple_of(ci * CHUNK, CHUNK)
            rs = pl.ds(base, CHUNK)
            z, live, bcum = _gate_terms(lr_ref, w2_ref, gb_ref, rs, (sidx * sc + ci) * CHUNK)
            bl = bcum[CHUNK - 1:CHUNK, :]
            ebl = jnp.exp(bl)
            q = qk_ref[rs, 0:GLA_K]
            k = qk_ref[rs, GLA_K:2 * GLA_K]
            eb = jnp.exp(bcum)
            enb = jnp.exp(-bcum)
            ehb = jnp.exp(bl - bcum)
            qt = q * (GLA_DK ** -0.5) * eb
            kt = k * enb
            kh = k * ehb
            qtb = qt.astype(BF16)
            ktb = kt.astype(BF16)
            vb = vg_ref[rs, 0:GLA_V].astype(BF16)
            gt = gt_ref[...]
            gtb = gt.astype(BF16)
            s_in = st_ref[ci]
            dqt = jnp.zeros((CHUNK, GLA_K), F32)
            dkt = jnp.zeros((CHUNK, GLA_K), F32)
            dkh = jnp.zeros((CHUNK, GLA_K), F32)
            dbl = jnp.zeros((1, GLA_K), F32)
            dos = []
            for h in range(GLA_H):
                hs = slice(h * GLA_DV, (h + 1) * GLA_DV)
                gs = slice(GLA_V + h * GLA_DV, GLA_V + (h + 1) * GLA_DV)
                oh = o_ref[rs, hs]
                ro = lax.rsqrt(jnp.mean(oh * oh, axis=-1, keepdims=True) + RMS_EPS)
                on = oh * ro
                g = vg_ref[rs, gs]
                sg = _sigmoid(g)
                dout = dyg_ref[rs, hs]
                dvg_ref[rs, gs] = (dout * on * ng * (sg * (1.0 + g * (1.0 - sg)))).astype(BF16)
                dw = dout * g * sg
                dng = dng + jnp.sum(dw * on, axis=0, keepdims=True)
                don = dw * ng
                do = ro * (don - on * jnp.mean(don * on, axis=-1, keepdims=True))
                dob = do.astype(BF16)
                dos.append(dob)
                qm = (qt * masks[h]).astype(BF16)
                a = jnp.where(causal, _dot(qm, ktb, _NT), 0.0).astype(BF16)
                da = jnp.where(causal, _dot(dob, vb[:, hs], _NT), 0.0).astype(BF16)
                gth = gtb[hs, :]
                dv = _dot(a, dob, _TN) + _dot((kh * masks[h]).astype(BF16), gth, _NT)
                dvg_ref[rs, hs] = dv.astype(BF16)
                dkh = dkh + masks[h] * _dot(vb[:, hs], gth)
                dqt = dqt + masks[h] * (_dot(da, ktb) + _dot(dob, s_in[hs, :].astype(BF16)))
                dkt = dkt + masks[h] * _dot(da, qtb, _TN)
                dbl = dbl + masks[h] * jnp.sum(gt[hs, :] * s_in[hs, :], axis=0, keepdims=True)
            dbl = dbl * ebl + jnp.sum(dkh * kh, axis=0, keepdims=True)
            dqk_ref[rs, 0:GLA_K] = (dqt * (GLA_DK ** -0.5) * eb).astype(BF16)
            dqk_ref[rs, GLA_K:2 * GLA_K] = (dkt * enb + dkh * ehb).astype(BF16)
            db = dqt * qt - dkt * kt - dkh * kh
            db = jnp.where(last_row, db + dbl, db)
            dla = jnp.where(live, _rev_cumsum_rows(db), 0.0)
            dz_ref[rs, :] = dla * (1.0 / TAU) * (1.0 - _sigmoid(z))
            gt_ref[...] = _dot(jnp.concatenate(dos, axis=1), qtb, _TN) + gt * ebl
            return dng

        dng = lax.fori_loop(0, sc, chunk, jnp.zeros((1, GLA_DV), F32))
        dz = dz_ref[...]
        dzb = dz.astype(BF16)
        dlr_ref[...] = _dot(dzb, w2_ref[...], _NT).astype(BF16)
        dw2_ref[...] += _dot(lr_ref[...].astype(BF16), dzb, _TN)
        dvec_ref[0:1, :] += jnp.sum(dz, axis=0, keepdims=True)
        dvec_ref[1:2, 0:GLA_DV] += dng

    sg_ = lambda w: pl.BlockSpec((seg, w), lambda b, s: (b * n_seg + n_seg - 1 - s, 0))
    return pl.pallas_call(
        body, name="bwd_gla", grid=(n_ex, n_seg),
        in_specs=[sg_(2 * GLA_K), sg_(2 * GLA_V), sg_(RANK_P), sg_(GLA_V),
                  pl.BlockSpec((sc, GLA_V, GLA_K), lambda b, s: (b * n_seg + n_seg - 1 - s, 0, 0)), sg_(GLA_V),
                  _fixed((RANK_P, GLA_K)), _fixed((1, GLA_K)), _fixed((1, GLA_DV))],
        out_specs=[sg_(2 * GLA_K), sg_(2 * GLA_V), sg_(RANK_P), _fixed((RANK_P, GLA_K)), _fixed((8, GLA_K))],
        out_shape=[jax.ShapeDtypeStruct((rows, 2 * GLA_K), BF16), jax.ShapeDtypeStruct((rows, 2 * GLA_V), BF16),
                   jax.ShapeDtypeStruct((rows, RANK_P), BF16), jax.ShapeDtypeStruct((RANK_P, GLA_K), F32),
                   jax.ShapeDtypeStruct((8, GLA_K), F32)],
        scratch_shapes=[pltpu.VMEM((GLA_V, GLA_K), F32), pltpu.VMEM((seg, GLA_K), F32)],
        compiler_params=_params(("arbitrary", "arbitrary")),
    )(qk, vg, lr, o, st, dyg, w2p, gb, ng)


def _local_step(x, tgt, p, late_weights, send_early):
    n_ex, seq, _ = x.shape
    lp = seq + LEAD
    rows = n_ex * lp
    meta = jnp.broadcast_to(p["meta"][None], (n_ex, N_META, D))
    h0 = jnp.concatenate([jnp.zeros((n_ex, ZROWS, D), F32), meta, x], axis=1).reshape(rows, D)
    tgt_p = jnp.pad(tgt, ((0, 0), (LEAD, 0), (0, 0))).reshape(rows, D)

    uc, qk, vg, lr, n1 = _fwd_inproj(h0, p["g1"], p["w_in"])
    ypre, yc = _fwd_conv(uc, p["conv_w"], p["conv_b"], p["ln_g"], p["ln_b"], n_ex)
    yg, o, st = _fwd_gla(qk, vg, lr, p["w2"], p["gb"], p["ng"], n_ex)
    w_out, wg, wu, wd = late_weights(yg)
    h1, n2 = _fwd_outproj(yc, yg, h0, w_out, p["g2"])
    f, da, db, dh2, dh1, dh1b, part = _ffn_rows(h1, n2, tgt_p, wg, wu, wd, p["g2"], p["g3"], lp)
    g = {}
    token = send_early("ffn", [_dw_blocked(n2, [da], FF_S, "dw_gate"), _dw_blocked(n2, [db], FF_S, "dw_up"),
                               _matmul_tn(f, dh2, "dw_down").reshape(N_DEV, FF_S, D)])
    dyc, dyg = _bwd_outproj(dh1b, w_out, token)
    token = send_early("out", [_dw_out(yc, yg, dh1b).reshape(N_DEV, W_OUT_S, D)])
    duc, g["conv_w"], g["conv_vec"] = _bwd_conv(uc, ypre, dyc, p["conv_w"], p["ln_g"], p["ln_b"], token, n_ex)
    dqk, dvg, dlr, g["w2"], g["gla_vec"] = _bwd_gla(qk, vg, lr, o, st, dyg, p["w2"], p["gb"], p["ng"], n_ex)
    dh0, g["in_vec"], g["meta"] = _bwd_inproj(duc, dqk, dvg, dlr, dh1, h0, p["w_in"], p["g1"], lp)
    g["w_in"] = _dw_blocked(n1, [duc, dqk, dvg, dlr], W_IN_S, "dw_in")
    g["ffn_vec"] = part
    return dh0.reshape(n_ex, lp, D)[:, LEAD:], g


W_IN_S = D_IN // N_DEV
W_OUT_S = D // N_DEV
FF_S = D_FF // N_DEV
CONV_S = C_CONV // N_DEV
GATE_S = GLA_K // N_DEV
SMALL_PACK = 64
CONV_ROW = 16
GATE_ROW = 48
VEC_ROWS = 16
_VEC_ROWS = (("norm_mix_g", D), ("conv_b", C_CONV), ("conv_ln_g", C_CONV), ("conv_ln_b", C_CONV), ("gla_gate_b", GLA_K),
             ("gla_norm_g", GLA_DV), ("norm_ffn_g", D), ("norm_final_g", D))
LOSS_ROW = len(_VEC_ROWS)


def _position():
    return lax.axis_index("x"), lax.axis_index("y"), lax.axis_index("c")


def _any():
    return pl.BlockSpec(memory_space=pl.ANY)


def _all_gather(mats, meta, conv_w, w2, n_now):
    n_mat = len(mats)
    n_t = n_mat + 1
    n_later = n_mat - n_now
    now = list(range(n_now)) + [n_mat]

    def body(*refs):
        ins = refs[0:n_mat]
        meta_ref, cw_ref, w2_ref = refs[n_mat:n_mat + 3]
        outs = refs[n_mat + 3:n_mat + 3 + n_t]
        later = refs[n_mat + 3 + n_t:n_mat + 3 + n_t + n_later]
        stage_now = refs[n_mat + 3 + n_t + n_later:n_mat + 4 + n_t + n_later + n_now]
        send_sems, recv_sems, local_sems = refs[n_mat + 4 + n_t + n_later + n_now:]
        stage = list(stage_now[0:n_now]) + list(later) + [stage_now[n_now]]
        for s_ref, w_ref in zip(stage, ins):
            s_ref[...] = w_ref[...].astype(BF16)
        sp = stage[n_mat]
        sp[...] = jnp.zeros_like(sp)
        sp[0:N_META, :] = meta_ref[...]
        sp[CONV_ROW:CONV_ROW + CONV_W, 0:CONV_S] = cw_ref[...]
        sp[GATE_ROW:GATE_ROW + RANK, 0:GATE_S] = w2_ref[...]

        x, y, c = _position()
        me, sibling = (x, y, c), (x, y, 1 - c)
        chips = [(1 - x, y), (x, 1 - y), (1 - x, 1 - y)]

        def blk(t, p):
            return outs[t].at[4 * p[0] + 2 * p[1] + p[2]]

        def copy(t, k, block, to, staged=False):
            return pltpu.make_async_remote_copy(
                src_ref=stage[t] if staged else blk(t, block), dst_ref=blk(t, block),
                send_sem=send_sems.at[t, k], recv_sem=recv_sems.at[t, k], device_id=to, device_id_type=MESH)

        mine = [pltpu.make_async_copy(stage[t], blk(t, me), local_sems.at[t]) for t in range(n_t)]
        for cp in mine:
            cp.start()
        first = []
        for t in now:
            first.append(copy(t, 0, me, sibling, staged=True))
            first += [copy(t, 1 + j, me, (*chip, c), staged=True) for j, chip in enumerate(chips)]
        for cp in first:
            cp.start()
        passed = []
        for t in now:
            for j, chip in enumerate(chips):
                copy(t, 1 + j, (*chip, c), me).wait_recv()
                passed.append(copy(t, 4 + j, (*chip, c), sibling))
                passed[-1].start()
        for t in now:
            copy(t, 0, sibling, me).wait_recv()
            for j, chip in enumerate(chips):
                copy(t, 4 + j, (*chip, 1 - c), me).wait_recv()
        for cp in first + passed:
            cp.wait_send()
        for cp in mine:
            cp.wait()

    shapes = [(N_DEV,) + m.shape for m in mats]
    res = pl.pallas_call(
        body, name="all_gather",
        out_shape=[jax.ShapeDtypeStruct(s, BF16) for s in shapes] + [jax.ShapeDtypeStruct((N_DEV, SMALL_PACK, 128), F32)]
        + [jax.ShapeDtypeStruct(m.shape, BF16) for m in mats[n_now:]],
        in_specs=[_whole_vmem()] * (n_mat + 3), out_specs=[_any()] * n_t + [_whole_vmem()] * n_later,
        scratch_shapes=[pltpu.VMEM(m.shape, BF16) for m in mats[:n_now]] + [pltpu.VMEM((SMALL_PACK, 128), F32)]
        + [pltpu.SemaphoreType.DMA((n_t, 7)), pltpu.SemaphoreType.DMA((n_t, 7)), pltpu.SemaphoreType.DMA((n_t,))],
        compiler_params=pltpu.CompilerParams(vmem_limit_bytes=VMEM_LIMIT),
    )(*mats, meta, conv_w, w2)
    return res[0:n_mat], res[n_mat], res[n_t:]


_HBM = pl.BlockSpec(memory_space=pltpu.HBM)
_SEM = pl.BlockSpec(memory_space=pltpu.SEMAPHORE)
_EFFECT = pltpu.SideEffectType.DATAFLOW_SIDE_EFFECTING


def _peers():
    x, y, c = _position()
    out = []
    for k in range(1, N_DEV):
        px = 1 - x if k & 4 else x
        py = 1 - y if k & 2 else y
        pc = 1 - c if k & 1 else c
        out.append((k, (px, py, pc), 4 * px + 2 * py + pc))
    return out, 4 * x + 2 * y + c


def _in_hbm(a):
    return pltpu.with_memory_space_constraint(a, pltpu.HBM)


def _send_start(name, srcs, lands, scatter):
    n = len(srcs)

    def body(*refs):
        src_refs, land_refs = refs[0:n], refs[n:2 * n]
        send_sems, recv_sems = refs[2 * n:2 * n + 2]
        token = refs[4 * n + 2]
        peers, me = _peers()
        for k, pos, to in peers:
            for t in range(n):
                pltpu.make_async_remote_copy(
                    src_ref=src_refs[t].at[to] if scatter else src_refs[t], dst_ref=land_refs[t].at[me],
                    send_sem=send_sems.at[7 * t + k - 1], recv_sem=recv_sems.at[7 * t + k - 1],
                    device_id=pos, device_id_type=MESH).start()
        token[...] = jnp.zeros_like(token)

    bufs = list(srcs) + list(lands)
    res = pl.pallas_call(
        body, name=name,
        out_shape=(pltpu.SemaphoreType.DMA((7 * n,)), pltpu.SemaphoreType.DMA((7 * n,)),
                   *[pltpu.HBM(b.shape, b.dtype) for b in bufs], jax.ShapeDtypeStruct((8, 128), F32)),
        in_specs=[_HBM] * (2 * n), out_specs=(_SEM, _SEM, *[_HBM] * (2 * n), _whole_vmem()),
        input_output_aliases={i: 2 + i for i in range(2 * n)},
        compiler_params=pltpu.CompilerParams(has_side_effects=_EFFECT),
    )(*[_in_hbm(b) for b in bufs])
    return res[0], res[1], res[2:2 + n], res[2 + n:2 + 2 * n], res[2 + 2 * n]


def _send_wait(name, send_sems, recv_sems, srcs, lands, scatter, after):
    n = len(srcs)

    def body(*refs):
        src_refs, land_refs = refs[0:n], refs[n:2 * n]
        send_sems, recv_sems = refs[2 * n:2 * n + 2]
        peers, me = _peers()
        for k, pos, to in peers:
            for t in range(n):
                cp = pltpu.make_async_remote_copy(
                    src_ref=src_refs[t].at[to] if scatter else src_refs[t], dst_ref=land_refs[t].at[me],
                    send_sem=send_sems.at[7 * t + k - 1], recv_sem=recv_sems.at[7 * t + k - 1],
                    device_id=pos, device_id_type=MESH)
                cp.wait_send()
                cp.wait_recv()

    bufs = list(srcs) + list(lands)
    res = pl.pallas_call(
        body, name=name,
        out_shape=tuple(pltpu.HBM(b.shape, b.dtype) for b in bufs),
        in_specs=[_HBM] * (2 * n) + [_SEM, _SEM, _any()], out_specs=tuple([_HBM] * (2 * n)),
        input_output_aliases={i: i for i in range(2 * n)},
        compiler_params=pltpu.CompilerParams(has_side_effects=_EFFECT),
    )(*bufs, send_sems, recv_sems, after)
    return res[0:n], res[n:2 * n]


def _unshard_in(a_in, a_small, token):
    def body(a_ref, s_ref, token_ref, w_ref, meta_ref, cw_ref, w2_ref):
        w_ref[:, D_IN:D_INP] = jnp.zeros((D, D_INP - D_IN), BF16)
        w2_ref[...] = jnp.zeros_like(w2_ref)
        for d in range(N_DEV):
            w_ref[:, d * W_IN_S:(d + 1) * W_IN_S] = a_ref[d]
            meta_ref[:, d * 128:(d + 1) * 128] = s_ref[d, 0:N_META, :]
            cw_ref[:, d * CONV_S:(d + 1) * CONV_S] = s_ref[d, CONV_ROW:CONV_ROW + 32, 0:CONV_S]
            w2_ref[0:RANK, d * GATE_S:(d + 1) * GATE_S] = s_ref[d, GATE_ROW:GATE_ROW + RANK, 0:GATE_S].astype(BF16)

    return pl.pallas_call(
        body, name="unshard_in",
        out_shape=[jax.ShapeDtypeStruct((D, D_INP), BF16), jax.ShapeDtypeStruct((N_META, D), F32),
                   jax.ShapeDtypeStruct((32, C_CONV), F32), jax.ShapeDtypeStruct((RANK_P, GLA_K), BF16)],
        compiler_params=pltpu.CompilerParams(vmem_limit_bytes=VMEM_LIMIT),
    )(a_in, a_small, token)


def _unshard_ffn(a_g, a_u):
    def body(g_ref, u_ref, wg_ref, wu_ref):
        for d in range(N_DEV):
            wg_ref[:, d * FF_S:(d + 1) * FF_S] = g_ref[d]
            wu_ref[:, d * FF_S:(d + 1) * FF_S] = u_ref[d]

    return pl.pallas_call(
        body, name="unshard_ffn", out_shape=[jax.ShapeDtypeStruct((D, D_FF), BF16)] * 2,
        compiler_params=pltpu.CompilerParams(vmem_limit_bytes=VMEM_LIMIT),
    )(a_g, a_u)


def _exchange(mats, g):
    n_mat = len(mats)
    n_t = n_mat + 2

    def body(*refs):
        ins = refs[0:n_mat]
        meta_ref, cw_ref, w2_ref, in_vec, ffn_vec, conv_vec, gla_vec = refs[n_mat:n_mat + 7]
        outs = refs[n_mat + 7:n_mat + 7 + n_t]
        sp, vp, send_sems, recv_sems, local_sems = refs[n_mat + 7 + n_t:]
        sp[...] = jnp.zeros_like(sp)
        for d in range(N_DEV):
            sp[d, 0:N_META, :] = meta_ref[:, d * 128:(d + 1) * 128]
            sp[d, CONV_ROW:CONV_ROW + 32, 0:CONV_S] = cw_ref[:, d * CONV_S:(d + 1) * CONV_S]
            sp[d, GATE_ROW:GATE_ROW + RANK, 0:GATE_S] = w2_ref[0:RANK, d * GATE_S:(d + 1) * GATE_S]
        vp[...] = jnp.zeros_like(vp)
        vp[0:1, :] = in_vec[0:1, :]
        vp[1:4, 0:C_CONV] = conv_vec[0:3, :]
        vp[4:5, 0:GLA_K] = gla_vec[0:1, :]
        vp[5:6, 0:GLA_DV] = gla_vec[1:2, 0:GLA_DV]
        vp[6:7, :] = ffn_vec[1:2, :]
        vp[7:8, :] = ffn_vec[0:1, :]
        vp[LOSS_ROW:LOSS_ROW + 1, :] = ffn_vec[2:3, :]

        x, y, c = _position()
        me = 4 * x + 2 * y + c
        mine = [pltpu.make_async_copy(sp.at[me], outs[n_mat].at[me], local_sems.at[0]),
                pltpu.make_async_copy(vp, outs[n_mat + 1].at[me], local_sems.at[1])]
        for cp in mine:
            cp.start()
        copies = []
        for k in range(1, N_DEV):
            px = 1 - x if k & 4 else x
            py = 1 - y if k & 2 else y
            pc = 1 - c if k & 1 else c
            to = 4 * px + 2 * py + pc
            for t in range(n_t):
                src = ins[t].at[to] if t < n_mat else (sp.at[to] if t == n_mat else vp)
                copies.append(pltpu.make_async_remote_copy(
                    src_ref=src, dst_ref=outs[t].at[me], send_sem=send_sems.at[t, k - 1], recv_sem=recv_sems.at[t, k - 1],
                    device_id=(px, py, pc), device_id_type=MESH))
                copies[-1].start()
        for cp in copies:
            cp.wait_recv()
        for cp in copies:
            cp.wait_send()
        for cp in mine:
            cp.wait()

    return pl.pallas_call(
        body, name="exchange",
        out_shape=[jax.ShapeDtypeStruct(m.shape, BF16) for m in mats]
        + [jax.ShapeDtypeStruct((N_DEV, SMALL_PACK, 128), F32), jax.ShapeDtypeStruct((N_DEV, VEC_ROWS, D), F32)],
        in_specs=[_any()] * n_mat + [_whole_vmem()] * 7, out_specs=[_any()] * n_t,
        scratch_shapes=[pltpu.VMEM((N_DEV, SMALL_PACK, 128), F32), pltpu.VMEM((VEC_ROWS, D), F32),
                        pltpu.SemaphoreType.DMA((n_t, 7)), pltpu.SemaphoreType.DMA((n_t, 7)), pltpu.SemaphoreType.DMA((2,))],
    )(*mats, g["meta"], g["conv_w"], g["w2"], g["in_vec"], g["ffn_vec"], g["conv_vec"], g["gla_vec"])


def _adamw(w, g, m, v):
    m = ADAM_B1 * m + (1.0 - ADAM_B1) * g
    v = ADAM_B2 * v + (1.0 - ADAM_B2) * (g * g)
    m_hat = m / (1.0 - ADAM_B1 ** ADAM_STEP)
    v_hat = v / (1.0 - ADAM_B2 ** ADAM_STEP)
    return -ADAM_LR * (m_hat / (jnp.sqrt(v_hat) + ADAM_EPS) + ADAM_WD * w), m, v


def _update_matrix(recv, own, me, w, m, v, name):
    _, r, c = recv.shape
    tr = _row_tile(r, 256)

    def body(me_ref, recv_ref, own_ref, w_ref, m_ref, v_ref, g_ref, d_ref, nm_ref, nv_ref):
        g = jnp.zeros((tr, c), F32)
        for s in range(N_DEV):
            g = g + jnp.where(me_ref[0] == s, own_ref[...], recv_ref[s]).astype(F32)
        g_ref[...] = g
        d_ref[...], nm_ref[...], nv_ref[...] = _adamw(w_ref[...], g, m_ref[...], v_ref[...])

    one = pl.BlockSpec((None, tr, c), lambda i, me_ref: (0, i, 0))
    return pl.pallas_call(
        body, name=name,
        grid_spec=pltpu.PrefetchScalarGridSpec(
            num_scalar_prefetch=1, grid=(r // tr,),
            in_specs=[pl.BlockSpec((N_DEV, tr, c), lambda i, me_ref: (0, i, 0)),
                      pl.BlockSpec((None, tr, c), lambda i, me_ref: (me_ref[0], i, 0)), one, one, one],
            out_specs=[one] * 4),
        out_shape=[jax.ShapeDtypeStruct((1, r, c), F32)] * 4,
        compiler_params=_params(("parallel",)),
    )(me, recv, own, w, m, v)


_SMALL = ("meta_tokens", "conv_w", "gla_w_gate2") + tuple(n for n, _ in _VEC_ROWS)


def _update_small(srecv, vrecv, w, m, v):
    n = len(_SMALL)

    def body(*refs):
        s_ref, v_ref = refs[0:2]
        w_refs, m_refs, v_refs = refs[2:2 + n], refs[2 + n:2 + 2 * n], refs[2 + 2 * n:2 + 3 * n]
        outs = refs[2 + 3 * n:]
        ssum, vsum = s_ref[0], v_ref[0]
        for s in range(1, N_DEV):
            ssum = ssum + s_ref[s]
            vsum = vsum + v_ref[s]
        grads = [ssum[0:N_META, :], ssum[CONV_ROW:CONV_ROW + CONV_W, 0:CONV_S], ssum[GATE_ROW:GATE_ROW + RANK, 0:GATE_S]]
        grads += [vsum[i:i + 1, 0:width] for i, (_, width) in enumerate(_VEC_ROWS)]
        for i, g in enumerate(grads):
            d, nm, nv = _adamw(w_refs[i][...], g, m_refs[i][...], v_refs[i][...])
            outs[i][...] = g
            outs[n + i][...] = d
            outs[2 * n + i][...] = nm
            outs[3 * n + i][...] = nv
        outs[4 * n][...] = vsum[LOSS_ROW:LOSS_ROW + 1, 0:128]

    shapes = [jax.ShapeDtypeStruct(t.shape, F32) for t in w]
    res = pl.pallas_call(
        body, name="update_small", out_shape=shapes * 4 + [jax.ShapeDtypeStruct((1, 128), F32)],
    )(srecv, vrecv, *w, *m, *v)
    return res[0:n], res[n:2 * n], res[2 * n:3 * n], res[3 * n:4 * n], res[4 * n]


_WEIGHTS = ("meta_tokens", "norm_mix_g", "w_in", "conv_w", "conv_b", "conv_ln_g", "conv_ln_b", "gla_w_gate2", "gla_gate_b",
            "gla_norm_g", "w_out", "norm_ffn_g", "w_ffn_gate", "w_ffn_up", "w_ffn_down", "norm_final_g")
_MATRICES = ("w_in", "w_out", "w_ffn_gate", "w_ffn_up", "w_ffn_down")


def kernel(x, meta_tokens, norm_mix_g, w_in, conv_w, conv_b, conv_ln_g, conv_ln_b, gla_w_gate2, gla_gate_b, gla_norm_g, w_out, norm_ffn_g, w_ffn_gate, w_ffn_up, w_ffn_down, norm_final_g, loss_target, m_meta_tokens, m_norm_mix_g, m_w_in, m_conv_w, m_conv_b, m_conv_ln_g, m_conv_ln_b, m_gla_w_gate2, m_gla_gate_b, m_gla_norm_g, m_w_out, m_norm_ffn_g, m_w_ffn_gate, m_w_ffn_up, m_w_ffn_down, m_norm_final_g, v_meta_tokens, v_norm_mix_g, v_w_in, v_conv_w, v_conv_b, v_conv_ln_g, v_conv_ln_b, v_gla_w_gate2, v_gla_gate_b, v_gla_norm_g, v_w_out, v_norm_ffn_g, v_w_ffn_gate, v_w_ffn_up, v_w_ffn_down, v_norm_final_g):
    given = dict(locals())
    two_d = lambda a: a.reshape(1, -1) if a.ndim == 1 else a.reshape(a.shape[-2:])
    fams = [{n: given[pre + n] for n in _WEIGHTS} for pre in ("", "m_", "v_")]
    w = fams[0]

    bufs, a_small, shards = _all_gather(
        [two_d(w[n]) for n in _MATRICES], w["meta_tokens"], two_d(w["conv_w"]), two_d(w["gla_w_gate2"]), 1)
    gather = _send_start("gather_start", shards, bufs[1:], False)
    w_in, meta, conv_taps, w2 = _unshard_in(bufs[0], a_small, gather[4])
    p = dict(meta=meta, conv_w=conv_taps, w2=w2, w_in=w_in, g1=norm_mix_g, conv_b=conv_b, ln_g=conv_ln_g, ln_b=conv_ln_b,
             gb=gla_gate_b, ng=gla_norm_g, g2=norm_ffn_g, g3=two_d(norm_final_g))

    def late_weights(after):
        _, (a_out, a_g, a_u, a_d) = _send_wait("gather_wait", *gather[0:4], False, after)
        wg, wu = _unshard_ffn(a_g, a_u)
        return a_out.reshape(D, D), wg, wu, a_d.reshape(D_FF, D)

    sent = {}

    def send_early(tag, mats):
        lands = [_in_hbm(lax.empty(m_.shape, m_.dtype)) for m_ in mats]
        sent[tag] = _send_start("scatter_" + tag + "_start", mats, lands, True)
        return sent[tag][4]

    grad_x, g = _local_step(x, loss_target, p, late_weights, send_early)

    r_in, srecv, vrecv = _exchange([g["w_in"]], g)
    own, recv = {"w_in": g["w_in"]}, {"w_in": r_in}
    for tag, names in (("ffn", ("w_ffn_gate", "w_ffn_up", "w_ffn_down")), ("out", ("w_out",))):
        srcs, lands = _send_wait("scatter_" + tag + "_wait", *sent[tag][0:4], True, srecv)
        own.update(zip(names, srcs))
        recv.update(zip(names, lands))

    x_, y_, c_ = _position()
    me = (4 * x_ + 2 * y_ + c_).astype(jnp.int32).reshape(1)
    res = {}
    for n in _MATRICES:
        res[n] = _update_matrix(recv[n], own[n], me, *[f[n] for f in fams], "update_" + n)
    small = _update_small(srecv, vrecv, *[[two_d(f[n]) for n in _SMALL] for f in fams])
    for i, n in enumerate(_SMALL):
        res[n] = [fam[i].reshape(w[n].shape) for fam in small[0:4]]
    outs = [small[4][0, 0], grad_x]
    for k in range(4):
        outs += [res[n][k] for n in _WEIGHTS]
    return tuple(outs)
```

```python
import functools

import jax
import jax.numpy as jnp
from jax import lax
from jax.experimental import pallas as pl
from jax.experimental.pallas import tpu as pltpu

F32 = jnp.float32
BF16 = jnp.bfloat16

D = 1024
N_META = 16
C_CONV = 512
CONV_W = 31
GLA_H = 4
GLA_DK = 64
GLA_DV = 128
GLA_K = GLA_H * GLA_DK
GLA_V = GLA_H * GLA_DV
RANK = 16
RANK_P = 128
TAU = 16.0
CHUNK = 64
LEAD = CHUNK
ZROWS = LEAD - N_META
D_IN = 2 * C_CONV + 2 * GLA_K + 2 * GLA_V + RANK
D_INP = D_IN - RANK + RANK_P
D_FF = 2816
FF_CHUNK = 1408
RMS_EPS = 1e-6
LN_EPS = 1e-5
N_DEV = 8

ADAM_LR = 0.001
ADAM_B1 = 0.9
ADAM_B2 = 0.999
ADAM_EPS = 1e-08
ADAM_WD = 0.01
ADAM_STEP = 10

VMEM_LIMIT = 60 * 1024 * 1024
ROW_TILE = 1056
FFN_ROW_TILE = 352
DW_ROW_TILE = 1408
MESH = pl.DeviceIdType.MESH

_NN = (((1,), (0,)), ((), ()))
_NT = (((1,), (1,)), ((), ()))
_TN = (((0,), (0,)), ((), ()))


def _dot(a, b, dims=_NN):
    return lax.dot_general(a, b, dims, preferred_element_type=F32)


def _sigmoid(x):
    return 1.0 / (1.0 + jnp.exp(-x))


def _row_tile(rows, target):
    best = None
    for t in range(16, min(rows, target) + 1, 16):
        if rows % t == 0:
            best = t
    assert best is not None, rows
    return best


def _params(sem=None):
    return pltpu.CompilerParams(dimension_semantics=sem, vmem_limit_bytes=VMEM_LIMIT)


def _whole_vmem():
    return pl.BlockSpec(memory_space=pltpu.VMEM)


def _rows(tm, width):
    return pl.BlockSpec((tm, width), lambda i: (i, 0))


def _fixed(shape):
    return pl.BlockSpec(shape, lambda *_: (0,) * len(shape))


def _fwd_inproj(h0, g1, w_in):
    rows = h0.shape[0]
    tm = _row_tile(rows, ROW_TILE)

    def body(h_ref, g_ref, w_ref, uc_ref, qk_ref, vg_ref, lr_ref, n1_ref):
        h = h_ref[...]
        r = lax.rsqrt(jnp.mean(h * h, axis=-1, keepdims=True) + RMS_EPS)
        n = (h * r * g_ref[...]).astype(BF16)
        n1_ref[...] = n
        uc_ref[...] = _dot(n, w_ref[:, 0:1024])
        qk_ref[...] = _dot(n, w_ref[:, 1024:1536])
        vg_ref[...] = _dot(n, w_ref[:, 1536:2560])
        lr_ref[...] = _dot(n, w_ref[:, 2560:2688])

    return pl.pallas_call(
        body, name="fwd_inproj", grid=(rows // tm,),
        in_specs=[_rows(tm, D), _fixed((1, D)), _whole_vmem()],
        out_specs=[_rows(tm, 1024), _rows(tm, 512), _rows(tm, 1024), _rows(tm, RANK_P), _rows(tm, D)],
        out_shape=[jax.ShapeDtypeStruct((rows, 1024), F32), jax.ShapeDtypeStruct((rows, 512), F32),
                   jax.ShapeDtypeStruct((rows, 1024), F32), jax.ShapeDtypeStruct((rows, RANK_P), F32),
                   jax.ShapeDtypeStruct((rows, D), BF16)],
        compiler_params=_params(("parallel",)),
    )(h0, g1, w_in)


def _fwd_outproj(yc, yg, h0, w_out, g2):
    rows = h0.shape[0]
    tm = _row_tile(rows, ROW_TILE)

    def body(yc_ref, yg_ref, h_ref, w_ref, g_ref, h1_ref, n2_ref):
        h1 = h_ref[...] + _dot(yc_ref[...], w_ref[0:C_CONV, :]) + _dot(yg_ref[...], w_ref[C_CONV:D, :])
        h1_ref[...] = h1
        r = lax.rsqrt(jnp.mean(h1 * h1, axis=-1, keepdims=True) + RMS_EPS)
        n2_ref[...] = (h1 * r * g_ref[...]).astype(BF16)

    return pl.pallas_call(
        body, name="fwd_outproj", grid=(rows // tm,),
        in_specs=[_rows(tm, C_CONV), _rows(tm, GLA_V), _rows(tm, D), _whole_vmem(), _fixed((1, D))],
        out_specs=[_rows(tm, D), _rows(tm, D)],
        out_shape=[jax.ShapeDtypeStruct((rows, D), F32), jax.ShapeDtypeStruct((rows, D), BF16)],
        compiler_params=_params(("parallel",)),
    )(yc, yg, h0, w_out, g2)


def _ffn_rows(h1, n2, tgt, wg, wu, wd, g2, g3, rows_per_example):
    rows = h1.shape[0]
    tm = _row_tile(rows, FFN_ROW_TILE)
    n_ff = D_FF // FF_CHUNK

    def body(h1_ref, n2_ref, t_ref, wg_ref, wu_ref, wd_ref, g2_ref, g3_ref,
             f_ref, da_ref, db_ref, dh2_ref, dh1_ref, dh1b_ref, part_ref):
        i = pl.program_id(0)
        n2 = n2_ref[...]
        y2 = jnp.zeros((tm, D), F32)
        for c in range(n_ff):
            cs = slice(c * FF_CHUNK, (c + 1) * FF_CHUNK)
            a = _dot(n2, wg_ref[:, cs])
            b = _dot(n2, wu_ref[:, cs])
            f = (a * _sigmoid(a) * b).astype(BF16)
            f_ref[:, cs] = f
            da_ref[:, cs] = a.astype(BF16)
            db_ref[:, cs] = b.astype(BF16)
            y2 = y2 + _dot(f, wd_ref[cs, :])
        h1 = h1_ref[...]
        h2 = h1 + y2
        r3 = lax.rsqrt(jnp.mean(h2 * h2, axis=-1, keepdims=True) + RMS_EPS)
        xh3 = h2 * r3
        g3 = g3_ref[...]
        pos = (i * tm + lax.broadcasted_iota(jnp.int32, (tm, 1), 0)) % rows_per_example
        valid = pos >= LEAD
        err = jnp.where(valid, xh3 * g3 - t_ref[...], 0.0)
        loss = 0.5 / D * jnp.sum(jnp.sum(err * err, axis=-1, keepdims=True), axis=0, keepdims=True)
        dy = err * (1.0 / D)
        dg3 = jnp.sum(dy * xh3, axis=0, keepdims=True)
        dxh = dy * g3
        dh2 = r3 * (dxh - xh3 * jnp.mean(dxh * xh3, axis=-1, keepdims=True))
        dh2b = dh2.astype(BF16)
        dh2_ref[...] = dh2b
        dn2 = jnp.zeros((tm, D), F32)
        for c in range(n_ff):
            cs = slice(c * FF_CHUNK, (c + 1) * FF_CHUNK)
            df = _dot(dh2b, wd_ref[cs, :], _NT)
            a = da_ref[:, cs].astype(F32)
            b = db_ref[:, cs].astype(F32)
            sg = _sigmoid(a)
            da = (df * b * sg * (1.0 + a * (1.0 - sg))).astype(BF16)
            db = (df * a * sg).astype(BF16)
            da_ref[:, cs] = da
            db_ref[:, cs] = db
            dn2 = dn2 + _dot(da, wg_ref[:, cs], _NT) + _dot(db, wu_ref[:, cs], _NT)
        r2 = lax.rsqrt(jnp.mean(h1 * h1, axis=-1, keepdims=True) + RMS_EPS)
        xh2 = h1 * r2
        dg2 = jnp.sum(dn2 * xh2, axis=0, keepdims=True)
        dxh2 = dn2 * g2_ref[...]
        dh1 = dh2 + r2 * (dxh2 - xh2 * jnp.mean(dxh2 * xh2, axis=-1, keepdims=True))
        dh1_ref[...] = dh1
        dh1b_ref[...] = dh1.astype(BF16)

        @pl.when(i == 0)
        def _():
            part_ref[...] = jnp.zeros_like(part_ref)

        part_ref[0:1, :] += dg3
        part_ref[1:2, :] += dg2
        part_ref[2:3, :] += jnp.broadcast_to(loss, (1, D))

    return pl.pallas_call(
        body, name="ffn_rows", grid=(rows // tm,),
        in_specs=[_rows(tm, D), _rows(tm, D), _rows(tm, D), _whole_vmem(), _whole_vmem(), _whole_vmem(),
                  _fixed((1, D)), _fixed((1, D))],
        out_specs=[_rows(tm, D_FF), _rows(tm, D_FF), _rows(tm, D_FF), _rows(tm, D), _rows(tm, D), _rows(tm, D),
                   _fixed((8, D))],
        out_shape=[jax.ShapeDtypeStruct((rows, D_FF), BF16)] * 3
        + [jax.ShapeDtypeStruct((rows, D), BF16), jax.ShapeDtypeStruct((rows, D), F32),
           jax.ShapeDtypeStruct((rows, D), BF16), jax.ShapeDtypeStruct((8, D), F32)],
        compiler_params=_params(("arbitrary",)),
    )(h1, n2, tgt, wg, wu, wd, g2, g3)


def _bwd_outproj(dh1b, w_out, token):
    rows = dh1b.shape[0]
    tm = _row_tile(rows, ROW_TILE)

    def body(d_ref, w_ref, token_ref, dyc_ref, dyg_ref):
        d = d_ref[...]
        dyc_ref[...] = _dot(d, w_ref[0:C_CONV, :], _NT)
        dyg_ref[...] = _dot(d, w_ref[C_CONV:D, :], _NT)

    return pl.pallas_call(
        body, name="bwd_outproj", grid=(rows // tm,),
        in_specs=[_rows(tm, D), _whole_vmem(), _fixed((8, 128))],
        out_specs=[_rows(tm, C_CONV), _rows(tm, GLA_V)],
        out_shape=[jax.ShapeDtypeStruct((rows, C_CONV), F32), jax.ShapeDtypeStruct((rows, GLA_V), F32)],
        compiler_params=_params(("parallel",)),
    )(dh1b, w_out, token)


def _bwd_inproj(duc, dqk, dvg, dlr, dh1, h0, w_in, g1, rows_per_example):
    rows = h0.shape[0]
    tm = _row_tile(rows_per_example, ROW_TILE)
    tiles_per_example = rows_per_example // tm

    def body(duc_ref, dqk_ref, dvg_ref, dlr_ref, dh1_ref, h_ref, w_ref, g_ref, dh0_ref, part_ref, dmeta_ref):
        dn = (_dot(duc_ref[...], w_ref[:, 0:1024], _NT) + _dot(dqk_ref[...], w_ref[:, 1024:1536], _NT)
              + _dot(dvg_ref[...], w_ref[:, 1536:2560], _NT) + _dot(dlr_ref[...], w_ref[:, 2560:2688], _NT))
        h = h_ref[...]
        r = lax.rsqrt(jnp.mean(h * h, axis=-1, keepdims=True) + RMS_EPS)
        xh = h * r
        dg = jnp.sum(dn * xh, axis=0, keepdims=True)
        dxh = dn * g_ref[...]
        dh0 = dh1_ref[...] + r * (dxh - xh * jnp.mean(dxh * xh, axis=-1, keepdims=True))
        dh0_ref[...] = dh0
        i = pl.program_id(0)

        @pl.when(i == 0)
        def _():
            part_ref[...] = jnp.zeros_like(part_ref)
            dmeta_ref[...] = jnp.zeros_like(dmeta_ref)

        part_ref[0:1, :] += dg

        @pl.when(i % tiles_per_example == 0)
        def _():
            dmeta_ref[...] += dh0[ZROWS:LEAD, :]

    return pl.pallas_call(
        body, name="bwd_inproj", grid=(rows // tm,),
        in_specs=[_rows(tm, 1024), _rows(tm, 512), _rows(tm, 1024), _rows(tm, RANK_P), _rows(tm, D), _rows(tm, D),
                  _whole_vmem(), _fixed((1, D))],
        out_specs=[_rows(tm, D), _fixed((8, D)), _fixed((N_META, D))],
        out_shape=[jax.ShapeDtypeStruct((rows, D), F32), jax.ShapeDtypeStruct((8, D), F32),
                   jax.ShapeDtypeStruct((N_META, D), F32)],
        compiler_params=_params(("arbitrary",)),
    )(duc, dqk, dvg, dlr, dh1, h0, w_in, g1)


def _dw_blocked(a, bs, width, name):
    rows, m = a.shape
    ws = [b.shape[1] for b in bs]
    assert sum(ws) >= N_DEV * width
    tk = _row_tile(rows, DW_ROW_TILE)
    nk = rows // tk

    def body(a_ref, *refs):
        b_refs, o_ref, acc_ref = refs[:len(bs)], refs[len(bs)], refs[len(bs) + 1]
        k = pl.program_id(0)

        @pl.when(k == 0)
        def _():
            acc_ref[...] = jnp.zeros_like(acc_ref)

        at = a_ref[...].T
        off = 0
        for b_ref, w in zip(b_refs, ws):
            acc_ref[:, off:off + w] += _dot(at, b_ref[...])
            off += w

        @pl.when(k == nk - 1)
        def _():
            for d in range(N_DEV):
                o_ref[d] = acc_ref[:, d * width:(d + 1) * width].astype(BF16)

    return pl.pallas_call(
        body, name=name, grid=(nk,),
        in_specs=[_rows(tk, m)] + [_rows(tk, w) for w in ws],
        out_specs=_fixed((N_DEV, m, width)),
        out_shape=jax.ShapeDtypeStruct((N_DEV, m, width), BF16),
        scratch_shapes=[pltpu.VMEM((m, sum(ws)), F32)],
        compiler_params=_params(("arbitrary",)),
    )(a, *bs)


def _dw_out(yc, yg, dh1b):
    rows = yc.shape[0]
    tk = _row_tile(rows, DW_ROW_TILE)
    nk = rows // tk

    def body(yc_ref, yg_ref, d_ref, o_ref, acc_ref):
        k = pl.program_id(0)

        @pl.when(k == 0)
        def _():
            acc_ref[...] = jnp.zeros_like(acc_ref)

        d = d_ref[...]
        acc_ref[0:C_CONV, :] += _dot(yc_ref[...], d, _TN)
        acc_ref[C_CONV:D, :] += _dot(yg_ref[...], d, _TN)

        @pl.when(k == nk - 1)
        def _():
            o_ref[...] = acc_ref[...].astype(BF16)

    return pl.pallas_call(
        body, name="dw_out", grid=(nk,),
        in_specs=[_rows(tk, C_CONV), _rows(tk, GLA_V), _rows(tk, D)],
        out_specs=_fixed((D, D)), out_shape=jax.ShapeDtypeStruct((D, D), BF16),
        scratch_shapes=[pltpu.VMEM((D, D), F32)],
        compiler_params=_params(("arbitrary",)),
    )(yc, yg, dh1b)


def _matmul_tn(a, b, name):
    rows, m = a.shape
    n = b.shape[1]
    tk = _row_tile(rows, DW_ROW_TILE)
    tn = n if n <= 1024 else FF_CHUNK
    tm_ = m if m <= 1024 else FF_CHUNK
    assert n % tn == 0 and m % tm_ == 0
    nk = rows // tk

    def body(a_ref, b_ref, o_ref, acc_ref):
        k = pl.program_id(2)

        @pl.when(k == 0)
        def _():
            acc_ref[...] = jnp.zeros_like(acc_ref)

        acc_ref[...] += _dot(a_ref[...], b_ref[...], _TN)

        @pl.when(k == nk - 1)
        def _():
            o_ref[...] = acc_ref[...].astype(BF16)

    return pl.pallas_call(
        body, name=name, grid=(m // tm_, n // tn, nk),
        in_specs=[pl.BlockSpec((tk, tm_), lambda i, j, k: (k, i)), pl.BlockSpec((tk, tn), lambda i, j, k: (k, j))],
        out_specs=pl.BlockSpec((tm_, tn), lambda i, j, k: (i, j)),
        out_shape=jax.ShapeDtypeStruct((m, n), BF16),
        scratch_shapes=[pltpu.VMEM((tm_, tn), F32)],
        compiler_params=_params(("parallel", "parallel", "arbitrary")),
    )(a, b)


HALO = 32
LANES = 256


def _shifted(win, offsets):
    for r in range(8):
        js = [j for j, k in enumerate(offsets) if k % 8 == r]
        if js:
            rolled = win if r == 0 else pltpu.roll(win, CHUNK + HALO - r, 0)
            for j in js:
                yield j, rolled[offsets[j] - r:offsets[j] - r + CHUNK]


def _glu_into(uc_ref, vs_ref, n_chunk):
    vs_ref[0:CHUNK, :] = jnp.zeros((CHUNK, C_CONV), F32)

    def glu(i, carry):
        base = pl.multiple_of(i * CHUNK, CHUNK)
        val = uc_ref[pl.ds(base, CHUNK), 0:C_CONV]
        gate = uc_ref[pl.ds(base, CHUNK), C_CONV:2 * C_CONV]
        vs_ref[pl.ds(base + CHUNK, CHUNK), :] = val * _sigmoid(gate)
        return carry

    lax.fori_loop(0, n_chunk, glu, 0)


def _fwd_conv(uc, conv_w, conv_b, ln_g, ln_b, n_ex):
    rows = uc.shape[0]
    lp = rows // n_ex
    n_chunk = lp // CHUNK

    def body(uc_ref, w_ref, b_ref, lg_ref, lb_ref, ypre_ref, yc_ref, vs_ref):
        _glu_into(uc_ref, vs_ref, n_chunk)

        def conv(i, carry):
            base = pl.multiple_of(i * CHUNK, CHUNK)
            for lb in range(C_CONV // LANES):
                ls = slice(lb * LANES, (lb + 1) * LANES)
                win = vs_ref[pl.ds(base + CHUNK - HALO, CHUNK + HALO), ls]
                acc = jnp.broadcast_to(b_ref[:, ls], (CHUNK, LANES))
                for j, rows_j in _shifted(win, [HALO - (CONV_W - 1) + j for j in range(CONV_W)]):
                    acc = acc + w_ref[j:j + 1, ls] * rows_j
                ypre_ref[pl.ds(base, CHUNK), ls] = acc
            y = ypre_ref[pl.ds(base, CHUNK), :]
            mu = jnp.mean(y, axis=-1, keepdims=True)
            yc_ = y - mu
            rstd = lax.rsqrt(jnp.mean(yc_ * yc_, axis=-1, keepdims=True) + LN_EPS)
            s = yc_ * rstd * lg_ref[...] + lb_ref[...]
            yc_ref[pl.ds(base, CHUNK), :] = (s * _sigmoid(s)).astype(BF16)
            return carry

        lax.fori_loop(0, n_chunk, conv, 0)

    ex = lambda w: pl.BlockSpec((lp, w), lambda b: (b, 0))
    return pl.pallas_call(
        body, name="fwd_conv", grid=(n_ex,),
        in_specs=[ex(2 * C_CONV), _fixed((32, C_CONV)), _fixed((1, C_CONV)), _fixed((1, C_CONV)), _fixed((1, C_CONV))],
        out_specs=[ex(C_CONV), ex(C_CONV)],
        out_shape=[jax.ShapeDtypeStruct((rows, C_CONV), F32), jax.ShapeDtypeStruct((rows, C_CONV), BF16)],
        scratch_shapes=[pltpu.VMEM((lp + CHUNK, C_CONV), F32)],
        compiler_params=_params(("parallel",)),
    )(uc, conv_w, conv_b, ln_g, ln_b)


def _bwd_conv(uc, ypre, dyc, conv_w, ln_g, ln_b, token, n_ex):
    rows = uc.shape[0]
    lp = rows // n_ex
    n_chunk = lp // CHUNK

    def body(uc_ref, ypre_ref, dyc_ref, w_ref, lg_ref, lb_ref, token_ref, duc_ref, dw_ref, dvec_ref, vs_ref, dys_ref,
             dwacc_ref):
        _glu_into(uc_ref, vs_ref, n_chunk)
        dys_ref[pl.ds(lp, CHUNK), :] = jnp.zeros((CHUNK, C_CONV), F32)
        dwacc_ref[...] = jnp.zeros_like(dwacc_ref)

        def ln_bwd(i, carry):
            dcb, dlg, dlb = carry
            base = pl.multiple_of(i * CHUNK, CHUNK)
            y = ypre_ref[pl.ds(base, CHUNK), :]
            mu = jnp.mean(y, axis=-1, keepdims=True)
            yc_ = y - mu
            rstd = lax.rsqrt(jnp.mean(yc_ * yc_, axis=-1, keepdims=True) + LN_EPS)
            xh = yc_ * rstd
            s = xh * lg_ref[...] + lb_ref[...]
            sg = _sigmoid(s)
            ds = dyc_ref[pl.ds(base, CHUNK), :] * (sg * (1.0 + s * (1.0 - sg)))
            dxh = ds * lg_ref[...]
            dy = rstd * (dxh - jnp.mean(dxh, axis=-1, keepdims=True) - xh * jnp.mean(dxh * xh, axis=-1, keepdims=True))
            dys_ref[pl.ds(base, CHUNK), :] = dy
            return (dcb + jnp.sum(dy, axis=0, keepdims=True), dlg + jnp.sum(ds * xh, axis=0, keepdims=True),
                    dlb + jnp.sum(ds, axis=0, keepdims=True))

        zero = jnp.zeros((1, C_CONV), F32)
        dcb, dlg, dlb = lax.fori_loop(0, n_chunk, ln_bwd, (zero, zero, zero))

        @pl.when(pl.program_id(0) == 0)
        def _():
            dvec_ref[...] = jnp.zeros_like(dvec_ref)
            dw_ref[...] = jnp.zeros_like(dw_ref)

        dvec_ref[0:1, :] += dcb
        dvec_ref[1:2, :] += dlg
        dvec_ref[2:3, :] += dlb

        def taps(i, carry):
            base = pl.multiple_of(i * CHUNK, CHUNK)
            for lb in range(C_CONV // LANES):
                ls = slice(lb * LANES, (lb + 1) * LANES)
                dwin = dys_ref[pl.ds(base, CHUNK + HALO), ls]
                vwin = vs_ref[pl.ds(base + CHUNK - HALO, CHUNK + HALO), ls]
                dy = dwin[0:CHUNK]
                acc = jnp.zeros((CHUNK, LANES), F32)
                for j, rows_j in _shifted(dwin, [CONV_W - 1 - j for j in range(CONV_W)]):
                    acc = acc + w_ref[j:j + 1, ls] * rows_j
                for j, rows_j in _shifted(vwin, [HALO - (CONV_W - 1) + j for j in range(CONV_W)]):
                    dwacc_ref[8 * j:8 * j + 8, ls] += jnp.sum((dy * rows_j).reshape(CHUNK // 8, 8, LANES), axis=0)
                val = uc_ref[pl.ds(base, CHUNK), ls]
                gate = uc_ref[pl.ds(base, CHUNK), C_CONV + lb * LANES:C_CONV + (lb + 1) * LANES]
                sg = _sigmoid(gate)
                duc_ref[pl.ds(base, CHUNK), ls] = (acc * sg).astype(BF16)
                duc_ref[pl.ds(base, CHUNK), C_CONV + lb * LANES:C_CONV + (lb + 1) * LANES] = (
                    acc * val * sg * (1.0 - sg)).astype(BF16)
            return carry

        lax.fori_loop(0, n_chunk, taps, 0)
        for j in range(CONV_W):
            dw_ref[j:j + 1, :] += jnp.sum(dwacc_ref[8 * j:8 * j + 8, :], axis=0, keepdims=True)

    ex = lambda w: pl.BlockSpec((lp, w), lambda b: (b, 0))
    return pl.pallas_call(
        body, name="bwd_conv", grid=(n_ex,),
        in_specs=[ex(2 * C_CONV), ex(C_CONV), ex(C_CONV), _fixed((32, C_CONV)), _fixed((1, C_CONV)), _fixed((1, C_CONV)),
                  _fixed((8, 128))],
        out_specs=[ex(2 * C_CONV), _fixed((32, C_CONV)), _fixed((8, C_CONV))],
        out_shape=[jax.ShapeDtypeStruct((rows, 2 * C_CONV), BF16), jax.ShapeDtypeStruct((32, C_CONV), F32),
                   jax.ShapeDtypeStruct((8, C_CONV), F32)],
        scratch_shapes=[pltpu.VMEM((lp + CHUNK, C_CONV), F32), pltpu.VMEM((lp + CHUNK, C_CONV), F32),
                        pltpu.VMEM((8 * 32, C_CONV), F32)],
        compiler_params=_params(("arbitrary",)),
    )(uc, ypre, dyc, conv_w, ln_g, ln_b, token)


def _seg_chunks(n_chunk):
    return max(c for c in (11, 3, 1) if n_chunk % c == 0)


def _block_mask(shape, row_block, lane_block):
    return (lax.broadcasted_iota(jnp.int32, shape, 0) // row_block) == (lax.broadcasted_iota(jnp.int32, shape, 1) // lane_block)


def _per_head_rows(x, mask):
    return jnp.where(mask, jnp.concatenate([x] * GLA_H, axis=0), 0)


def _fold_heads(full, lane_block):
    lane = lax.broadcasted_iota(jnp.int32, (1, full.shape[1]), 1) // lane_block
    out = jnp.where(lane == 0, full[0:CHUNK], 0.0)
    for h in range(1, GLA_H):
        out = out + jnp.where(lane == h, full[h * CHUNK:(h + 1) * CHUNK], 0.0)
    return out


def _causal_heads():
    return (lax.broadcasted_iota(jnp.int32, (CHUNK, GLA_H * CHUNK), 1) % CHUNK) <= lax.broadcasted_iota(
        jnp.int32, (CHUNK, GLA_H * CHUNK), 0)


def _cumsum_rows(x):
    row = lax.broadcasted_iota(jnp.int32, x.shape, 0)
    s = 1
    while s < CHUNK:
        x = x + jnp.where(row >= s, pltpu.roll(x, s, 0), 0.0)
        s *= 2
    return x


def _rev_cumsum_rows(x):
    row = lax.broadcasted_iota(jnp.int32, x.shape, 0)
    s = 1
    while s < CHUNK:
        x = x + jnp.where(row < CHUNK - s, pltpu.roll(x, CHUNK - s, 0), 0.0)
        s *= 2
    return x


def _gate_terms(lr_ref, w2_ref, gb_ref, rs, first_pos):
    z = _dot(lr_ref[rs, :].astype(BF16), w2_ref[...]) + gb_ref[...]
    la = (jnp.minimum(z, 0.0) - jnp.log(1.0 + jnp.exp(-jnp.abs(z)))) * (1.0 / TAU)
    pos = first_pos + lax.broadcasted_iota(jnp.int32, (CHUNK, 1), 0)
    live = pos >= ZROWS
    la = jnp.where(live, la, 0.0)
    return z, live, _cumsum_rows(la)


def _fwd_gla(qk, vg, lr, w2p, gb, ng, n_ex):
    rows = qk.shape[0]
    lp = rows // n_ex
    n_chunk = lp // CHUNK
    sc = _seg_chunks(n_chunk)
    n_seg = n_chunk // sc
    seg = sc * CHUNK

    def body(qk_ref, vg_ref, lr_ref, w2_ref, gb_ref, ng_ref, yg_ref, o_ref, st_ref, state_ref):
        sidx = pl.program_id(1)

        @pl.when(sidx == 0)
        def _():
            state_ref[...] = jnp.zeros_like(state_ref)

        causal = _causal_heads()
        k_mask = _block_mask((GLA_H * CHUNK, GLA_K), CHUNK, GLA_DK)
        v_mask = _block_mask((GLA_H * CHUNK, GLA_V), CHUNK, GLA_DV)
        s_mask = _block_mask((GLA_V, GLA_K), GLA_DV, GLA_DK)

        def chunk(ci, carry):
            base = pl.multiple_of(ci * CHUNK, CHUNK)
            rs = pl.ds(base, CHUNK)
            _, _, bcum = _gate_terms(lr_ref, w2_ref, gb_ref, rs, (sidx * sc + ci) * CHUNK)
            bl = bcum[CHUNK - 1:CHUNK, :]
            q = qk_ref[rs, 0:GLA_K]
            k = qk_ref[rs, GLA_K:2 * GLA_K]
            qt = (q * (GLA_DK ** -0.5) * jnp.exp(bcum)).astype(BF16)
            kt = (k * jnp.exp(-bcum)).astype(BF16)
            kh = (k * jnp.exp(bl - bcum)).astype(BF16)
            vb = vg_ref[rs, 0:GLA_V].astype(BF16)
            state = state_ref[...]
            st_ref[ci] = state
            a = jnp.where(causal, _dot(qt, _per_head_rows(kt, k_mask), _NT), 0.0)
            o = _dot(a.astype(BF16), _per_head_rows(vb, v_mask)) + _dot(qt, state.astype(BF16), _NT)
            o_ref[rs, :] = o
            for h in range(GLA_H):
                hs = slice(h * GLA_DV, (h + 1) * GLA_DV)
                oh = o[:, hs]
                ro = lax.rsqrt(jnp.mean(oh * oh, axis=-1, keepdims=True) + RMS_EPS)
                g = vg_ref[rs, GLA_V + h * GLA_DV:GLA_V + (h + 1) * GLA_DV]
                yg_ref[rs, hs] = (oh * ro * ng_ref[...] * g * _sigmoid(g)).astype(BF16)
            state_ref[...] = state * jnp.exp(bl) + jnp.where(s_mask, _dot(vb, kh, _TN), 0.0)
            return carry

        lax.fori_loop(0, sc, chunk, 0)

    sg = lambda w: pl.BlockSpec((seg, w), lambda b, s: (b * n_seg + s, 0))
    return pl.pallas_call(
        body, name="fwd_gla", grid=(n_ex, n_seg),
        in_specs=[sg(2 * GLA_K), sg(2 * GLA_V), sg(RANK_P), _fixed((RANK_P, GLA_K)), _fixed((1, GLA_K)), _fixed((1, GLA_DV))],
        out_specs=[sg(GLA_V), sg(GLA_V), pl.BlockSpec((sc, GLA_V, GLA_K), lambda b, s: (b * n_seg + s, 0, 0))],
        out_shape=[jax.ShapeDtypeStruct((rows, GLA_V), BF16), jax.ShapeDtypeStruct((rows, GLA_V), F32),
                   jax.ShapeDtypeStruct((n_ex * n_chunk, GLA_V, GLA_K), F32)],
        scratch_shapes=[pltpu.VMEM((GLA_V, GLA_K), F32)],
        compiler_params=_params(("parallel", "arbitrary")),
    )(qk, vg, lr, w2p, gb, ng)


def _bwd_gla(qk, vg, lr, o, st, dyg, w2p, gb, ng, n_ex):
    rows = qk.shape[0]
    lp = rows // n_ex
    n_chunk = lp // CHUNK
    sc = _seg_chunks(n_chunk)
    n_seg = n_chunk // sc
    seg = sc * CHUNK

    def body(qk_ref, vg_ref, lr_ref, o_ref, st_ref, dyg_ref, w2_ref, gb_ref, ng_ref,
             dqk_ref, dvg_ref, dlr_ref, dw2_ref, dvec_ref, gt_ref, dz_ref):
        step = pl.program_id(1)
        sidx = n_seg - 1 - step

        @pl.when(step == 0)
        def _():
            gt_ref[...] = jnp.zeros_like(gt_ref)

        @pl.when((step == 0) & (pl.program_id(0) == 0))
        def _():
            dw2_ref[...] = jnp.zeros_like(dw2_ref)
            dvec_ref[...] = jnp.zeros_like(dvec_ref)

        causal = _causal_heads()
        k_mask = _block_mask((GLA_H * CHUNK, GLA_K), CHUNK, GLA_DK)
        v_mask = _block_mask((GLA_H * CHUNK, GLA_V), CHUNK, GLA_DV)
        s_mask = _block_mask((GLA_V, GLA_K), GLA_DV, GLA_DK)
        last_row = lax.broadcasted_iota(jnp.int32, (CHUNK, 1), 0) == CHUNK - 1
        ng = ng_ref[...]

        def chunk(ii, dng):
            ci = sc - 1 - ii
            base = pl.multiple_of(ci * CHUNK, CHUNK)
            rs = pl.ds(base, CHUNK)
            z, live, bcum = _gate_terms(lr_ref, w2_ref, gb_ref, rs, (sidx * sc + ci) * CHUNK)
            bl = bcum[CHUNK - 1:CHUNK, :]
            ebl = jnp.exp(bl)
            q = qk_ref[rs, 0:GLA_K]
            k = qk_ref[rs, GLA_K:2 * GLA_K]
            eb = jnp.exp(bcum)
            enb = jnp.exp(-bcum)
            ehb = jnp.exp(bl - bcum)
            qt = q * (GLA_DK ** -0.5) * eb
            kt = k * enb
            kh = k * ehb
            qtb = qt.astype(BF16)
            vb = vg_ref[rs, 0:GLA_V].astype(BF16)
            k_rows = _per_head_rows(kt.astype(BF16), k_mask)
            v_rows = _per_head_rows(vb, v_mask)
            gt = gt_ref[...]
            gtb = gt.astype(BF16)
            s_in = st_ref[ci]
            dos = []
            for h in range(GLA_H):
                hs = slice(h * GLA_DV, (h + 1) * GLA_DV)
                gs = slice(GLA_V + h * GLA_DV, GLA_V + (h + 1) * GLA_DV)
                oh = o_ref[rs, hs]
                ro = lax.rsqrt(jnp.mean(oh * oh, axis=-1, keepdims=True) + RMS_EPS)
                on = oh * ro
                g = vg_ref[rs, gs]
                sg = _sigmoid(g)
                dout = dyg_ref[rs, hs]
                dvg_ref[rs, gs] = (dout * on * ng * (sg * (1.0 + g * (1.0 - sg)))).astype(BF16)
                dw = dout * g * sg
                dng = dng + jnp.sum(dw * on, axis=0, keepdims=True)
                don = dw * ng
                dos.append((ro * (don - on * jnp.mean(don * on, axis=-1, keepdims=True))).astype(BF16))
            dob = jnp.concatenate(dos, axis=1)
            a = jnp.where(causal, _dot(qtb, k_rows, _NT), 0.0).astype(BF16)
            da = jnp.where(causal, _dot(dob, v_rows, _NT), 0.0).astype(BF16)
            dv = _fold_heads(_dot(a, dob, _TN), GLA_DV) + _dot(kh.astype(BF16), gtb, _NT)
            dvg_ref[rs, 0:GLA_V] = dv.astype(BF16)
            dkh = _dot(vb, gtb)
            dqt = _dot(da, k_rows) + _dot(dob, s_in.astype(BF16))
            dkt = _fold_heads(_dot(da, qtb, _TN), GLA_DK)
            dbl = jnp.sum(gt * s_in, axis=0, keepdims=True) * ebl + jnp.sum(dkh * kh, axis=0, keepdims=True)
            dqk_ref[rs, 0:GLA_K] = (dqt * (GLA_DK ** -0.5) * eb).astype(BF16)
            dqk_ref[rs, GLA_K:2 * GLA_K] = (dkt * enb + dkh * ehb).astype(BF16)
            db = dqt * qt - dkt * kt - dkh * kh
            db = jnp.where(last_row, db + dbl, db)
            dla = jnp.where(live, _rev_cumsum_rows(db), 0.0)
            dz_ref[rs, :] = dla * (1.0 / TAU) * (1.0 - _sigmoid(z))
            gt_ref[...] = jnp.where(s_mask, _dot(dob, qtb, _TN), 0.0) + gt * ebl
            return dng

        dng = lax.fori_loop(0, sc, chunk, jnp.zeros((1, GLA_DV), F32))
        dz = dz_ref[...]
        dzb = dz.astype(BF16)
        dlr_ref[...] = _dot(dzb, w2_ref[...], _NT).astype(BF16)
        dw2_ref[...] += _dot(lr_ref[...].astype(BF16), dzb, _TN)
        dvec_ref[0:1, :] += jnp.sum(dz, axis=0, keepdims=True)
        dvec_ref[1:2, 0:GLA_DV] += dng

    sg_ = lambda w: pl.BlockSpec((seg, w), lambda b, s: (b * n_seg + n_seg - 1 - s, 0))
    return pl.pallas_call(
        body, name="bwd_gla", grid=(n_ex, n_seg),
        in_specs=[sg_(2 * GLA_K), sg_(2 * GLA_V), sg_(RANK_P), sg_(GLA_V),
                  pl.BlockSpec((sc, GLA_V, GLA_K), lambda b, s: (b * n_seg + n_seg - 1 - s, 0, 0)), sg_(GLA_V),
                  _fixed((RANK_P, GLA_K)), _fixed((1, GLA_K)), _fixed((1, GLA_DV))],
        out_specs=[sg_(2 * GLA_K), sg_(2 * GLA_V), sg_(RANK_P), _fixed((RANK_P, GLA_K)), _fixed((8, GLA_K))],
        out_shape=[jax.ShapeDtypeStruct((rows, 2 * GLA_K), BF16), jax.ShapeDtypeStruct((rows, 2 * GLA_V), BF16),
                   jax.ShapeDtypeStruct((rows, RANK_P), BF16), jax.ShapeDtypeStruct((RANK_P, GLA_K), F32),
                   jax.ShapeDtypeStruct((8, GLA_K), F32)],
        scratch_shapes=[pltpu.VMEM((GLA_V, GLA_K), F32), pltpu.VMEM((seg, GLA_K), F32)],
        compiler_params=_params(("arbitrary", "arbitrary")),
    )(qk, vg, lr, o, st, dyg, w2p, gb, ng)


def _local_step(x, tgt, p, late_weights, send_early):
    n_ex, seq, _ = x.shape
    lp = seq + LEAD
    rows = n_ex * lp
    meta = jnp.broadcast_to(p["meta"][None], (n_ex, N_META, D))
    h0 = jnp.concatenate([jnp.zeros((n_ex, ZROWS, D), F32), meta, x], axis=1).reshape(rows, D)
    tgt_p = jnp.pad(tgt, ((0, 0), (LEAD, 0), (0, 0))).reshape(rows, D)

    uc, qk, vg, lr, n1 = _fwd_inproj(h0, p["g1"], p["w_in"])
    ypre, yc = _fwd_conv(uc, p["conv_w"], p["conv_b"], p["ln_g"], p["ln_b"], n_ex)
    yg, o, st = _fwd_gla(qk, vg, lr, p["w2"], p["gb"], p["ng"], n_ex)
    w_out, wg, wu, wd = late_weights(yg)
    h1, n2 = _fwd_outproj(yc, yg, h0, w_out, p["g2"])
    f, da, db, dh2, dh1, dh1b, part = _ffn_rows(h1, n2, tgt_p, wg, wu, wd, p["g2"], p["g3"], lp)
    g = {}
    token = send_early("ffn", [_dw_blocked(n2, [da], FF_S, "dw_gate"), _dw_blocked(n2, [db], FF_S, "dw_up"),
                               _matmul_tn(f, dh2, "dw_down").reshape(N_DEV, FF_S, D)])
    dyc, dyg = _bwd_outproj(dh1b, w_out, token)
    token = send_early("out", [_dw_out(yc, yg, dh1b).reshape(N_DEV, W_OUT_S, D)])
    duc, g["conv_w"], g["conv_vec"] = _bwd_conv(uc, ypre, dyc, p["conv_w"], p["ln_g"], p["ln_b"], token, n_ex)
    dqk, dvg, dlr, g["w2"], g["gla_vec"] = _bwd_gla(qk, vg, lr, o, st, dyg, p["w2"], p["gb"], p["ng"], n_ex)
    dh0, g["in_vec"], g["meta"] = _bwd_inproj(duc, dqk, dvg, dlr, dh1, h0, p["w_in"], p["g1"], lp)
    g["w_in"] = _dw_blocked(n1, [duc, dqk, dvg, dlr], W_IN_S, "dw_in")
    g["ffn_vec"] = part
    return dh0.reshape(n_ex, lp, D)[:, LEAD:], g


W_IN_S = D_IN // N_DEV
W_OUT_S = D // N_DEV
FF_S = D_FF // N_DEV
CONV_S = C_CONV // N_DEV
GATE_S = GLA_K // N_DEV
SMALL_PACK = 64
CONV_ROW = 16
GATE_ROW = 48
VEC_ROWS = 16
_VEC_ROWS = (("norm_mix_g", D), ("conv_b", C_CONV), ("conv_ln_g", C_CONV), ("conv_ln_b", C_CONV), ("gla_gate_b", GLA_K),
             ("gla_norm_g", GLA_DV), ("norm_ffn_g", D), ("norm_final_g", D))
LOSS_ROW = len(_VEC_ROWS)


def _position():
    return lax.axis_index("x"), lax.axis_index("y"), lax.axis_index("c")


def _any():
    return pl.BlockSpec(memory_space=pl.ANY)


def _all_gather(mats, meta, conv_w, w2, n_now):
    n_mat = len(mats)
    n_t = n_mat + 1
    n_later = n_mat - n_now
    now = list(range(n_now)) + [n_mat]

    def body(*refs):
        ins = refs[0:n_mat]
        meta_ref, cw_ref, w2_ref = refs[n_mat:n_mat + 3]
        outs = refs[n_mat + 3:n_mat + 3 + n_t]
        later = refs[n_mat + 3 + n_t:n_mat + 3 + n_t + n_later]
        stage_now = refs[n_mat + 3 + n_t + n_later:n_mat + 4 + n_t + n_later + n_now]
        send_sems, recv_sems, local_sems = refs[n_mat + 4 + n_t + n_later + n_now:]
        stage = list(stage_now[0:n_now]) + list(later) + [stage_now[n_now]]
        for s_ref, w_ref in zip(stage, ins):
            s_ref[...] = w_ref[...].astype(BF16)
        sp = stage[n_mat]
        sp[...] = jnp.zeros_like(sp)
        sp[0:N_META, :] = meta_ref[...]
        sp[CONV_ROW:CONV_ROW + CONV_W, 0:CONV_S] = cw_ref[...]
        sp[GATE_ROW:GATE_ROW + RANK, 0:GATE_S] = w2_ref[...]

        x, y, c = _position()
        me, sibling = (x, y, c), (x, y, 1 - c)
        chips = [(1 - x, y), (x, 1 - y), (1 - x, 1 - y)]

        def blk(t, p):
            return outs[t].at[4 * p[0] + 2 * p[1] + p[2]]

        def copy(t, k, block, to, staged=False):
            return pltpu.make_async_remote_copy(
                src_ref=stage[t] if staged else blk(t, block), dst_ref=blk(t, block),
                send_sem=send_sems.at[t, k], recv_sem=recv_sems.at[t, k], device_id=to, device_id_type=MESH)

        mine = [pltpu.make_async_copy(stage[t], blk(t, me), local_sems.at[t]) for t in range(n_t)]
        for cp in mine:
            cp.start()
        first = []
        for t in now:
            first.append(copy(t, 0, me, sibling, staged=True))
            first += [copy(t, 1 + j, me, (*chip, c), staged=True) for j, chip in enumerate(chips)]
        for cp in first:
            cp.start()
        passed = []
        for t in now:
            for j, chip in enumerate(chips):
                copy(t, 1 + j, (*chip, c), me).wait_recv()
                passed.append(copy(t, 4 + j, (*chip, c), sibling))
                passed[-1].start()
        for t in now:
            copy(t, 0, sibling, me).wait_recv()
            for j, chip in enumerate(chips):
                copy(t, 4 + j, (*chip, 1 - c), me).wait_recv()
        for cp in first + passed:
            cp.wait_send()
        for cp in mine:
            cp.wait()

    shapes = [(N_DEV,) + m.shape for m in mats]
    res = pl.pallas_call(
        body, name="all_gather",
        out_shape=[jax.ShapeDtypeStruct(s, BF16) for s in shapes] + [jax.ShapeDtypeStruct((N_DEV, SMALL_PACK, 128), F32)]
        + [jax.ShapeDtypeStruct(m.shape, BF16) for m in mats[n_now:]],
        in_specs=[_whole_vmem()] * (n_mat + 3), out_specs=[_any()] * n_t + [_whole_vmem()] * n_later,
        scratch_shapes=[pltpu.VMEM(m.shape, BF16) for m in mats[:n_now]] + [pltpu.VMEM((SMALL_PACK, 128), F32)]
        + [pltpu.SemaphoreType.DMA((n_t, 7)), pltpu.SemaphoreType.DMA((n_t, 7)), pltpu.SemaphoreType.DMA((n_t,))],
        compiler_params=pltpu.CompilerParams(vmem_limit_bytes=VMEM_LIMIT),
    )(*mats, meta, conv_w, w2)
    return res[0:n_mat], res[n_mat], res[n_t:]


_HBM = pl.BlockSpec(memory_space=pltpu.HBM)
_SEM = pl.BlockSpec(memory_space=pltpu.SEMAPHORE)
_EFFECT = pltpu.SideEffectType.DATAFLOW_SIDE_EFFECTING


def _peers():
    x, y, c = _position()
    out = []
    for k in range(1, N_DEV):
        px = 1 - x if k & 4 else x
        py = 1 - y if k & 2 else y
        pc = 1 - c if k & 1 else c
        out.append((k, (px, py, pc), 4 * px + 2 * py + pc))
    return out, 4 * x + 2 * y + c


def _in_hbm(a):
    return pltpu.with_memory_space_constraint(a, pltpu.HBM)


def _send_start(name, srcs, lands, scatter):
    n = len(srcs)

    def body(*refs):
        src_refs, land_refs = refs[0:n], refs[n:2 * n]
        send_sems, recv_sems = refs[2 * n:2 * n + 2]
        token = refs[4 * n + 2]
        peers, me = _peers()
        for k, pos, to in peers:
            for t in range(n):
                pltpu.make_async_remote_copy(
                    src_ref=src_refs[t].at[to] if scatter else src_refs[t], dst_ref=land_refs[t].at[me],
                    send_sem=send_sems.at[7 * t + k - 1], recv_sem=recv_sems.at[7 * t + k - 1],
                    device_id=pos, device_id_type=MESH).start()
        token[...] = jnp.zeros_like(token)

    bufs = list(srcs) + list(lands)
    res = pl.pallas_call(
        body, name=name,
        out_shape=(pltpu.SemaphoreType.DMA((7 * n,)), pltpu.SemaphoreType.DMA((7 * n,)),
                   *[pltpu.HBM(b.shape, b.dtype) for b in bufs], jax.ShapeDtypeStruct((8, 128), F32)),
        in_specs=[_HBM] * (2 * n), out_specs=(_SEM, _SEM, *[_HBM] * (2 * n), _whole_vmem()),
        input_output_aliases={i: 2 + i for i in range(2 * n)},
        compiler_params=pltpu.CompilerParams(has_side_effects=_EFFECT),
    )(*[_in_hbm(b) for b in bufs])
    return res[0], res[1], res[2:2 + n], res[2 + n:2 + 2 * n], res[2 + 2 * n]


def _send_wait(name, send_sems, recv_sems, srcs, lands, scatter, after):
    n = len(srcs)

    def body(*refs):
        src_refs, land_refs = refs[0:n], refs[n:2 * n]
        send_sems, recv_sems = refs[2 * n:2 * n + 2]
        peers, me = _peers()
        for k, pos, to in peers:
            for t in range(n):
                cp = pltpu.make_async_remote_copy(
                    src_ref=src_refs[t].at[to] if scatter else src_refs[t], dst_ref=land_refs[t].at[me],
                    send_sem=send_sems.at[7 * t + k - 1], recv_sem=recv_sems.at[7 * t + k - 1],
                    device_id=pos, device_id_type=MESH)
                cp.wait_send()
                cp.wait_recv()

    bufs = list(srcs) + list(lands)
    res = pl.pallas_call(
        body, name=name,
        out_shape=tuple(pltpu.HBM(b.shape, b.dtype) for b in bufs),
        in_specs=[_HBM] * (2 * n) + [_SEM, _SEM, _any()], out_specs=tuple([_HBM] * (2 * n)),
        input_output_aliases={i: i for i in range(2 * n)},
        compiler_params=pltpu.CompilerParams(has_side_effects=_EFFECT),
    )(*bufs, send_sems, recv_sems, after)
    return res[0:n], res[n:2 * n]


def _unshard_in(a_in, a_small, token):
    def body(a_ref, s_ref, token_ref, w_ref, meta_ref, cw_ref, w2_ref):
        w_ref[:, D_IN:D_INP] = jnp.zeros((D, D_INP - D_IN), BF16)
        w2_ref[...] = jnp.zeros_like(w2_ref)
        for d in range(N_DEV):
            w_ref[:, d * W_IN_S:(d + 1) * W_IN_S] = a_ref[d]
            meta_ref[:, d * 128:(d + 1) * 128] = s_ref[d, 0:N_META, :]
            cw_ref[:, d * CONV_S:(d + 1) * CONV_S] = s_ref[d, CONV_ROW:CONV_ROW + 32, 0:CONV_S]
            w2_ref[0:RANK, d * GATE_S:(d + 1) * GATE_S] = s_ref[d, GATE_ROW:GATE_ROW + RANK, 0:GATE_S].astype(BF16)

    return pl.pallas_call(
        body, name="unshard_in",
        out_shape=[jax.ShapeDtypeStruct((D, D_INP), BF16), jax.ShapeDtypeStruct((N_META, D), F32),
                   jax.ShapeDtypeStruct((32, C_CONV), F32), jax.ShapeDtypeStruct((RANK_P, GLA_K), BF16)],
        compiler_params=pltpu.CompilerParams(vmem_limit_bytes=VMEM_LIMIT),
    )(a_in, a_small, token)


def _unshard_ffn(a_g, a_u):
    def body(g_ref, u_ref, wg_ref, wu_ref):
        for d in range(N_DEV):
            wg_ref[:, d * FF_S:(d + 1) * FF_S] = g_ref[d]
            wu_ref[:, d * FF_S:(d + 1) * FF_S] = u_ref[d]

    return pl.pallas_call(
        body, name="unshard_ffn", out_shape=[jax.ShapeDtypeStruct((D, D_FF), BF16)] * 2,
        compiler_params=pltpu.CompilerParams(vmem_limit_bytes=VMEM_LIMIT),
    )(a_g, a_u)


def _exchange(mats, g):
    n_mat = len(mats)
    n_t = n_mat + 2

    def body(*refs):
        ins = refs[0:n_mat]
        meta_ref, cw_ref, w2_ref, in_vec, ffn_vec, conv_vec, gla_vec = refs[n_mat:n_mat + 7]
        outs = refs[n_mat + 7:n_mat + 7 + n_t]
        sp, vp, send_sems, recv_sems, local_sems = refs[n_mat + 7 + n_t:]
        sp[...] = jnp.zeros_like(sp)
        for d in range(N_DEV):
            sp[d, 0:N_META, :] = meta_ref[:, d * 128:(d + 1) * 128]
            sp[d, CONV_ROW:CONV_ROW + 32, 0:CONV_S] = cw_ref[:, d * CONV_S:(d + 1) * CONV_S]
            sp[d, GATE_ROW:GATE_ROW + RANK, 0:GATE_S] = w2_ref[0:RANK, d * GATE_S:(d + 1) * GATE_S]
        vp[...] = jnp.zeros_like(vp)
        vp[0:1, :] = in_vec[0:1, :]
        vp[1:4, 0:C_CONV] = conv_vec[0:3, :]
        vp[4:5, 0:GLA_K] = gla_vec[0:1, :]
        vp[5:6, 0:GLA_DV] = gla_vec[1:2, 0:GLA_DV]
        vp[6:7, :] = ffn_vec[1:2, :]
        vp[7:8, :] = ffn_vec[0:1, :]
        vp[LOSS_ROW:LOSS_ROW + 1, :] = ffn_vec[2:3, :]

        x, y, c = _position()
        me = 4 * x + 2 * y + c
        mine = [pltpu.make_async_copy(sp.at[me], outs[n_mat].at[me], local_sems.at[0]),
                pltpu.make_async_copy(vp, outs[n_mat + 1].at[me], local_sems.at[1])]
        for cp in mine:
            cp.start()
        copies = []
        for k in range(1, N_DEV):
            px = 1 - x if k & 4 else x
            py = 1 - y if k & 2 else y
            pc = 1 - c if k & 1 else c
            to = 4 * px + 2 * py + pc
            for t in range(n_t):
                src = ins[t].at[to] if t < n_mat else (sp.at[to] if t == n_mat else vp)
                copies.append(pltpu.make_async_remote_copy(
                    src_ref=src, dst_ref=outs[t].at[me], send_sem=send_sems.at[t, k - 1], recv_sem=recv_sems.at[t, k - 1],
                    device_id=(px, py, pc), device_id_type=MESH))
                copies[-1].start()
        for cp in copies:
            cp.wait_recv()
        for cp in copies:
            cp.wait_send()
        for cp in mine:
            cp.wait()

    return pl.pallas_call(
        body, name="exchange",
        out_shape=[jax.ShapeDtypeStruct(m.shape, BF16) for m in mats]
        + [jax.ShapeDtypeStruct((N_DEV, SMALL_PACK, 128), F32), jax.ShapeDtypeStruct((N_DEV, VEC_ROWS, D), F32)],
        in_specs=[_any()] * n_mat + [_whole_vmem()] * 7, out_specs=[_any()] * n_t,
        scratch_shapes=[pltpu.VMEM((N_DEV, SMALL_PACK, 128), F32), pltpu.VMEM((VEC_ROWS, D), F32),
                        pltpu.SemaphoreType.DMA((n_t, 7)), pltpu.SemaphoreType.DMA((n_t, 7)), pltpu.SemaphoreType.DMA((2,))],
    )(*mats, g["meta"], g["conv_w"], g["w2"], g["in_vec"], g["ffn_vec"], g["conv_vec"], g["gla_vec"])


def _adamw(w, g, m, v):
    m = ADAM_B1 * m + (1.0 - ADAM_B1) * g
    v = ADAM_B2 * v + (1.0 - ADAM_B2) * (g * g)
    m_hat = m / (1.0 - ADAM_B1 ** ADAM_STEP)
    v_hat = v / (1.0 - ADAM_B2 ** ADAM_STEP)
    return -ADAM_LR * (m_hat / (jnp.sqrt(v_hat) + ADAM_EPS) + ADAM_WD * w), m, v


def _update_matrix(recv, own, me, w, m, v, name):
    _, r, c = recv.shape
    tr = _row_tile(r, 256)

    def body(me_ref, recv_ref, own_ref, w_ref, m_ref, v_ref, g_ref, d_ref, nm_ref, nv_ref):
        g = jnp.zeros((tr, c), F32)
        for s in range(N_DEV):
            g = g + jnp.where(me_ref[0] == s, own_ref[...], recv_ref[s]).astype(F32)
        g_ref[...] = g
        d_ref[...], nm_ref[...], nv_ref[...] = _adamw(w_ref[...], g, m_ref[...], v_ref[...])

    one = pl.BlockSpec((None, tr, c), lambda i, me_ref: (0, i, 0))
    return pl.pallas_call(
        body, name=name,
        grid_spec=pltpu.PrefetchScalarGridSpec(
            num_scalar_prefetch=1, grid=(r // tr,),
            in_specs=[pl.BlockSpec((N_DEV, tr, c), lambda i, me_ref: (0, i, 0)),
                      pl.BlockSpec((None, tr, c), lambda i, me_ref: (me_ref[0], i, 0)), one, one, one],
            out_specs=[one] * 4),
        out_shape=[jax.ShapeDtypeStruct((1, r, c), F32)] * 4,
        compiler_params=_params(("parallel",)),
    )(me, recv, own, w, m, v)


_SMALL = ("meta_tokens", "conv_w", "gla_w_gate2") + tuple(n for n, _ in _VEC_ROWS)


def _update_small(srecv, vrecv, w, m, v):
    n = len(_SMALL)

    def body(*refs):
        s_ref, v_ref = refs[0:2]
        w_refs, m_refs, v_refs = refs[2:2 + n], refs[2 + n:2 + 2 * n], refs[2 + 2 * n:2 + 3 * n]
        outs = refs[2 + 3 * n:]
        ssum, vsum = s_ref[0], v_ref[0]
        for s in range(1, N_DEV):
            ssum = ssum + s_ref[s]
            vsum = vsum + v_ref[s]
        grads = [ssum[0:N_META, :], ssum[CONV_ROW:CONV_ROW + CONV_W, 0:CONV_S], ssum[GATE_ROW:GATE_ROW + RANK, 0:GATE_S]]
        grads += [vsum[i:i + 1, 0:width] for i, (_, width) in enumerate(_VEC_ROWS)]
        for i, g in enumerate(grads):
            d, nm, nv = _adamw(w_refs[i][...], g, m_refs[i][...], v_refs[i][...])
            outs[i][...] = g
            outs[n + i][...] = d
            outs[2 * n + i][...] = nm
            outs[3 * n + i][...] = nv
        outs[4 * n][...] = vsum[LOSS_ROW:LOSS_ROW + 1, 0:128]

    shapes = [jax.ShapeDtypeStruct(t.shape, F32) for t in w]
    res = pl.pallas_call(
        body, name="update_small", out_shape=shapes * 4 + [jax.ShapeDtypeStruct((1, 128), F32)],
    )(srecv, vrecv, *w, *m, *v)
    return res[0:n], res[n:2 * n], res[2 * n:3 * n], res[3 * n:4 * n], res[4 * n]


_WEIGHTS = ("meta_tokens", "norm_mix_g", "w_in", "conv_w", "conv_b", "conv_ln_g", "conv_ln_b", "gla_w_gate2", "gla_gate_b",
            "gla_norm_g", "w_out", "norm_ffn_g", "w_ffn_gate", "w_ffn_up", "w_ffn_down", "norm_final_g")
_MATRICES = ("w_in", "w_out", "w_ffn_gate", "w_ffn_up", "w_ffn_down")


def kernel(x, meta_tokens, norm_mix_g, w_in, conv_w, conv_b, conv_ln_g, conv_ln_b, gla_w_gate2, gla_gate_b, gla_norm_g, w_out, norm_ffn_g, w_ffn_gate, w_ffn_up, w_ffn_down, norm_final_g, loss_target, m_meta_tokens, m_norm_mix_g, m_w_in, m_conv_w, m_conv_b, m_conv_ln_g, m_conv_ln_b, m_gla_w_gate2, m_gla_gate_b, m_gla_norm_g, m_w_out, m_norm_ffn_g, m_w_ffn_gate, m_w_ffn_up, m_w_ffn_down, m_norm_final_g, v_meta_tokens, v_norm_mix_g, v_w_in, v_conv_w, v_conv_b, v_conv_ln_g, v_conv_ln_b, v_gla_w_gate2, v_gla_gate_b, v_gla_norm_g, v_w_out, v_norm_ffn_g, v_w_ffn_gate, v_w_ffn_up, v_w_ffn_down, v_norm_final_g):
    given = dict(locals())
    two_d = lambda a: a.reshape(1, -1) if a.ndim == 1 else a.reshape(a.shape[-2:])
    fams = [{n: given[pre + n] for n in _WEIGHTS} for pre in ("", "m_", "v_")]
    w = fams[0]

    bufs, a_small, shards = _all_gather(
        [two_d(w[n]) for n in _MATRICES], w["meta_tokens"], two_d(w["conv_w"]), two_d(w["gla_w_gate2"]), 1)
    gather = _send_start("gather_start", shards, bufs[1:], False)
    w_in, meta, conv_taps, w2 = _unshard_in(bufs[0], a_small, gather[4])
    p = dict(meta=meta, conv_w=conv_taps, w2=w2, w_in=w_in, g1=norm_mix_g, conv_b=conv_b, ln_g=conv_ln_g, ln_b=conv_ln_b,
             gb=gla_gate_b, ng=gla_norm_g, g2=norm_ffn_g, g3=two_d(norm_final_g))

    def late_weights(after):
        _, (a_out, a_g, a_u, a_d) = _send_wait("gather_wait", *gather[0:4], False, after)
        wg, wu = _unshard_ffn(a_g, a_u)
        return a_out.reshape(D, D), wg, wu, a_d.reshape(D_FF, D)

    sent = {}

    def send_early(tag, mats):
        lands = [_in_hbm(lax.empty(m_.shape, m_.dtype)) for m_ in mats]
        sent[tag] = _send_start("scatter_" + tag + "_start", mats, lands, True)
        return sent[tag][4]

    grad_x, g = _local_step(x, loss_target, p, late_weights, send_early)

    r_in, srecv, vrecv = _exchange([g["w_in"]], g)
    own, recv = {"w_in": g["w_in"]}, {"w_in": r_in}
    for tag, names in (("ffn", ("w_ffn_gate", "w_ffn_up", "w_ffn_down")), ("out", ("w_out",))):
        srcs, lands = _send_wait("scatter_" + tag + "_wait", *sent[tag][0:4], True, srecv)
        own.update(zip(names, srcs))
        recv.update(zip(names, lands))

    x_, y_, c_ = _position()
    me = (4 * x_ + 2 * y_ + c_).astype(jnp.int32).reshape(1)
    res = {}
    for n in _MATRICES:
        res[n] = _update_matrix(recv[n], own[n], me, *[f[n] for f in fams], "update_" + n)
    small = _update_small(srecv, vrecv, *[[two_d(f[n]) for n in _SMALL] for f in fams])
    for i, n in enumerate(_SMALL):
        res[n] = [fam[i].reshape(w[n].shape) for fam in small[0:4]]
    outs = [small[4][0, 0], grad_x]
    for k in range(4):
        outs += [res[n][k] for n in _WEIGHTS]
    return tuple(outs)
```

```python
import functools

import jax
import jax.numpy as jnp
from jax import lax
from jax.experimental import pallas as pl
from jax.experimental.pallas import tpu as pltpu

F32 = jnp.float32
BF16 = jnp.bfloat16

D = 1024
N_META = 16
C_CONV = 512
CONV_W = 31
GLA_H = 4
GLA_DK = 64
GLA_DV = 128
GLA_K = GLA_H * GLA_DK
GLA_V = GLA_H * GLA_DV
RANK = 16
RANK_P = 128
TAU = 16.0
CHUNK = 64
LEAD = CHUNK
ZROWS = LEAD - N_META
D_IN = 2 * C_CONV + 2 * GLA_K + 2 * GLA_V + RANK
D_INP = D_IN - RANK + RANK_P
D_FF = 2816
FF_CHUNK = 1408
FF_SPLIT = (0, 1536, D_FF)
RMS_EPS = 1e-6
LN_EPS = 1e-5
N_DEV = 8

ADAM_LR = 0.001
ADAM_B1 = 0.9
ADAM_B2 = 0.999
ADAM_EPS = 1e-08
ADAM_WD = 0.01
ADAM_STEP = 10

VMEM_LIMIT = 60 * 1024 * 1024
ROW_TILE = 1056
FFN_ROW_TILE = 352
DW_ROW_TILE = 1408
MESH = pl.DeviceIdType.MESH

_NN = (((1,), (0,)), ((), ()))
_NT = (((1,), (1,)), ((), ()))
_TN = (((0,), (0,)), ((), ()))


def _dot(a, b, dims=_NN):
    return lax.dot_general(a, b, dims, preferred_element_type=F32)


def _sigmoid(x):
    return 1.0 / (1.0 + jnp.exp(-x))


def _row_tile(rows, target):
    best = None
    for t in range(16, min(rows, target) + 1, 16):
        if rows % t == 0:
            best = t
    assert best is not None, rows
    return best


def _params(sem=None):
    return pltpu.CompilerParams(dimension_semantics=sem, vmem_limit_bytes=VMEM_LIMIT)


def _whole_vmem():
    return pl.BlockSpec(memory_space=pltpu.VMEM)


def _rows(tm, width):
    return pl.BlockSpec((tm, width), lambda i: (i, 0))


def _fixed(shape):
    return pl.BlockSpec(shape, lambda *_: (0,) * len(shape))


def _fwd_inproj(h0, g1, w_in):
    rows = h0.shape[0]
    tm = _row_tile(rows, ROW_TILE)

    def body(h_ref, g_ref, w_ref, uc_ref, qk_ref, vg_ref, lr_ref, n1_ref):
        h = h_ref[...]
        r = lax.rsqrt(jnp.mean(h * h, axis=-1, keepdims=True) + RMS_EPS)
        n = (h * r * g_ref[...]).astype(BF16)
        n1_ref[...] = n
        uc_ref[...] = _dot(n, w_ref[:, 0:1024])
        qk_ref[...] = _dot(n, w_ref[:, 1024:1536])
        vg_ref[...] = _dot(n, w_ref[:, 1536:2560])
        lr_ref[...] = _dot(n, w_ref[:, 2560:2688])

    return pl.pallas_call(
        body, name="fwd_inproj", grid=(rows // tm,),
        in_specs=[_rows(tm, D), _fixed((1, D)), _whole_vmem()],
        out_specs=[_rows(tm, 1024), _rows(tm, 512), _rows(tm, 1024), _rows(tm, RANK_P), _rows(tm, D)],
        out_shape=[jax.ShapeDtypeStruct((rows, 1024), F32), jax.ShapeDtypeStruct((rows, 512), F32),
                   jax.ShapeDtypeStruct((rows, 1024), F32), jax.ShapeDtypeStruct((rows, RANK_P), F32),
                   jax.ShapeDtypeStruct((rows, D), BF16)],
        compiler_params=_params(("parallel",)),
    )(h0, g1, w_in)


def _fwd_outproj(yc, yg, h0, w_out, g2, token):
    rows = h0.shape[0]
    tm = _row_tile(rows, ROW_TILE)

    def body(yc_ref, yg_ref, h_ref, w_ref, g_ref, token_ref, h1_ref, n2_ref):
        h1 = h_ref[...] + _dot(yc_ref[...], w_ref[0:C_CONV, :]) + _dot(yg_ref[...], w_ref[C_CONV:D, :])
        h1_ref[...] = h1
        r = lax.rsqrt(jnp.mean(h1 * h1, axis=-1, keepdims=True) + RMS_EPS)
        n2_ref[...] = (h1 * r * g_ref[...]).astype(BF16)

    return pl.pallas_call(
        body, name="fwd_outproj", grid=(rows // tm,),
        in_specs=[_rows(tm, C_CONV), _rows(tm, GLA_V), _rows(tm, D), _whole_vmem(), _fixed((1, D)), _fixed((8, 128))],
        out_specs=[_rows(tm, D), _rows(tm, D)],
        out_shape=[jax.ShapeDtypeStruct((rows, D), F32), jax.ShapeDtypeStruct((rows, D), BF16)],
        compiler_params=_params(("parallel",)),
    )(yc, yg, h0, w_out, g2, token)


def _ffn_rows(h1, n2, tgt, wg, wu, wd, g2, g3, rows_per_example):
    rows = h1.shape[0]
    tm = _row_tile(rows, FFN_ROW_TILE)
    ff_blocks = [slice(lo, hi) for lo, hi in zip(FF_SPLIT[:-1], FF_SPLIT[1:])]

    def body(h1_ref, n2_ref, t_ref, wg_ref, wu_ref, wd_ref, g2_ref, g3_ref,
             f_ref, da_ref, db_ref, dh2_ref, dh1_ref, dh1b_ref, part_ref):
        i = pl.program_id(0)
        n2 = n2_ref[...]
        y2 = jnp.zeros((tm, D), F32)
        for cs in ff_blocks:
            a = _dot(n2, wg_ref[:, cs])
            b = _dot(n2, wu_ref[:, cs])
            f = (a * _sigmoid(a) * b).astype(BF16)
            f_ref[:, cs] = f
            da_ref[:, cs] = a.astype(BF16)
            db_ref[:, cs] = b.astype(BF16)
            y2 = y2 + _dot(f, wd_ref[cs, :])
        h1 = h1_ref[...]
        h2 = h1 + y2
        r3 = lax.rsqrt(jnp.mean(h2 * h2, axis=-1, keepdims=True) + RMS_EPS)
        xh3 = h2 * r3
        g3 = g3_ref[...]
        pos = (i * tm + lax.broadcasted_iota(jnp.int32, (tm, 1), 0)) % rows_per_example
        valid = pos >= LEAD
        err = jnp.where(valid, xh3 * g3 - t_ref[...], 0.0)
        loss = 0.5 / D * jnp.sum(jnp.sum(err * err, axis=-1, keepdims=True), axis=0, keepdims=True)
        dy = err * (1.0 / D)
        dg3 = jnp.sum(dy * xh3, axis=0, keepdims=True)
        dxh = dy * g3
        dh2 = r3 * (dxh - xh3 * jnp.mean(dxh * xh3, axis=-1, keepdims=True))
        dh2b = dh2.astype(BF16)
        dh2_ref[...] = dh2b
        dn2 = jnp.zeros((tm, D), F32)
        for cs in ff_blocks:
            df = _dot(dh2b, wd_ref[cs, :], _NT)
            a = da_ref[:, cs].astype(F32)
            b = db_ref[:, cs].astype(F32)
            sg = _sigmoid(a)
            da = (df * b * sg * (1.0 + a * (1.0 - sg))).astype(BF16)
            db = (df * a * sg).astype(BF16)
            da_ref[:, cs] = da
            db_ref[:, cs] = db
            dn2 = dn2 + _dot(da, wg_ref[:, cs], _NT) + _dot(db, wu_ref[:, cs], _NT)
        r2 = lax.rsqrt(jnp.mean(h1 * h1, axis=-1, keepdims=True) + RMS_EPS)
        xh2 = h1 * r2
        dg2 = jnp.sum(dn2 * xh2, axis=0, keepdims=True)
        dxh2 = dn2 * g2_ref[...]
        dh1 = dh2 + r2 * (dxh2 - xh2 * jnp.mean(dxh2 * xh2, axis=-1, keepdims=True))
        dh1_ref[...] = dh1
        dh1b_ref[...] = dh1.astype(BF16)

        @pl.when(i == 0)
        def _():
            part_ref[...] = jnp.zeros_like(part_ref)

        part_ref[0:1, :] += dg3
        part_ref[1:2, :] += dg2
        part_ref[2:3, :] += jnp.broadcast_to(loss, (1, D))

    return pl.pallas_call(
        body, name="ffn_rows", grid=(rows // tm,),
        in_specs=[_rows(tm, D), _rows(tm, D), _rows(tm, D), _whole_vmem(), _whole_vmem(), _whole_vmem(),
                  _fixed((1, D)), _fixed((1, D))],
        out_specs=[_rows(tm, D_FF), _rows(tm, D_FF), _rows(tm, D_FF), _rows(tm, D), _rows(tm, D), _rows(tm, D),
                   _fixed((8, D))],
        out_shape=[jax.ShapeDtypeStruct((rows, D_FF), BF16)] * 3
        + [jax.ShapeDtypeStruct((rows, D), BF16), jax.ShapeDtypeStruct((rows, D), F32),
           jax.ShapeDtypeStruct((rows, D), BF16), jax.ShapeDtypeStruct((8, D), F32)],
        compiler_params=_params(("arbitrary",)),
    )(h1, n2, tgt, wg, wu, wd, g2, g3)


def _bwd_outproj(dh1b, w_out, token):
    rows = dh1b.shape[0]
    tm = _row_tile(rows, ROW_TILE)

    def body(d_ref, w_ref, token_ref, dyc_ref, dyg_ref):
        d = d_ref[...]
        dyc_ref[...] = _dot(d, w_ref[0:C_CONV, :], _NT)
        dyg_ref[...] = _dot(d, w_ref[C_CONV:D, :], _NT)

    return pl.pallas_call(
        body, name="bwd_outproj", grid=(rows // tm,),
        in_specs=[_rows(tm, D), _whole_vmem(), _fixed((8, 128))],
        out_specs=[_rows(tm, C_CONV), _rows(tm, GLA_V)],
        out_shape=[jax.ShapeDtypeStruct((rows, C_CONV), F32), jax.ShapeDtypeStruct((rows, GLA_V), F32)],
        compiler_params=_params(("parallel",)),
    )(dh1b, w_out, token)


def _bwd_inproj(duc, dqk, dvg, dlr, dh1, h0, w_in, g1, rows_per_example):
    rows = h0.shape[0]
    tm = _row_tile(rows_per_example, ROW_TILE)
    tiles_per_example = rows_per_example // tm

    def body(duc_ref, dqk_ref, dvg_ref, dlr_ref, dh1_ref, h_ref, w_ref, g_ref, dh0_ref, part_ref, dmeta_ref):
        dn = (_dot(duc_ref[...], w_ref[:, 0:1024], _NT) + _dot(dqk_ref[...], w_ref[:, 1024:1536], _NT)
              + _dot(dvg_ref[...], w_ref[:, 1536:2560], _NT) + _dot(dlr_ref[...], w_ref[:, 2560:2688], _NT))
        h = h_ref[...]
        r = lax.rsqrt(jnp.mean(h * h, axis=-1, keepdims=True) + RMS_EPS)
        xh = h * r
        dg = jnp.sum(dn * xh, axis=0, keepdims=True)
        dxh = dn * g_ref[...]
        dh0 = dh1_ref[...] + r * (dxh - xh * jnp.mean(dxh * xh, axis=-1, keepdims=True))
        dh0_ref[...] = dh0
        i = pl.program_id(0)

        @pl.when(i == 0)
        def _():
            part_ref[...] = jnp.zeros_like(part_ref)
            dmeta_ref[...] = jnp.zeros_like(dmeta_ref)

        part_ref[0:1, :] += dg

        @pl.when(i % tiles_per_example == 0)
        def _():
            dmeta_ref[...] += dh0[ZROWS:LEAD, :]

    return pl.pallas_call(
        body, name="bwd_inproj", grid=(rows // tm,),
        in_specs=[_rows(tm, 1024), _rows(tm, 512), _rows(tm, 1024), _rows(tm, RANK_P), _rows(tm, D), _rows(tm, D),
                  _whole_vmem(), _fixed((1, D))],
        out_specs=[_rows(tm, D), _fixed((8, D)), _fixed((N_META, D))],
        out_shape=[jax.ShapeDtypeStruct((rows, D), F32), jax.ShapeDtypeStruct((8, D), F32),
                   jax.ShapeDtypeStruct((N_META, D), F32)],
        compiler_params=_params(("arbitrary",)),
    )(duc, dqk, dvg, dlr, dh1, h0, w_in, g1)


def _dw_blocked(a, bs, width, name):
    rows, m = a.shape
    ws = [b.shape[1] for b in bs]
    assert sum(ws) >= N_DEV * width
    tk = _row_tile(rows, DW_ROW_TILE)
    nk = rows // tk

    def body(a_ref, *refs):
        b_refs, o_ref, acc_ref = refs[:len(bs)], refs[len(bs)], refs[len(bs) + 1]
        k = pl.program_id(0)

        @pl.when(k == 0)
        def _():
            acc_ref[...] = jnp.zeros_like(acc_ref)

        at = a_ref[...].T
        off = 0
        for b_ref, w in zip(b_refs, ws):
            acc_ref[:, off:off + w] += _dot(at, b_ref[...])
            off += w

        @pl.when(k == nk - 1)
        def _():
            for d in range(N_DEV):
                o_ref[d] = acc_ref[:, d * width:(d + 1) * width].astype(BF16)

    return pl.pallas_call(
        body, name=name, grid=(nk,),
        in_specs=[_rows(tk, m)] + [_rows(tk, w) for w in ws],
        out_specs=_fixed((N_DEV, m, width)),
        out_shape=jax.ShapeDtypeStruct((N_DEV, m, width), BF16),
        scratch_shapes=[pltpu.VMEM((m, sum(ws)), F32)],
        compiler_params=_params(("arbitrary",)),
    )(a, *bs)


def _dw_out(yc, yg, dh1b):
    rows = yc.shape[0]
    tk = _row_tile(rows, DW_ROW_TILE)
    nk = rows // tk

    def body(yc_ref, yg_ref, d_ref, o_ref, acc_ref):
        k = pl.program_id(0)

        @pl.when(k == 0)
        def _():
            acc_ref[...] = jnp.zeros_like(acc_ref)

        d = d_ref[...]
        acc_ref[0:C_CONV, :] += _dot(yc_ref[...], d, _TN)
        acc_ref[C_CONV:D, :] += _dot(yg_ref[...], d, _TN)

        @pl.when(k == nk - 1)
        def _():
            o_ref[...] = acc_ref[...].astype(BF16)

    return pl.pallas_call(
        body, name="dw_out", grid=(nk,),
        in_specs=[_rows(tk, C_CONV), _rows(tk, GLA_V), _rows(tk, D)],
        out_specs=_fixed((D, D)), out_shape=jax.ShapeDtypeStruct((D, D), BF16),
        scratch_shapes=[pltpu.VMEM((D, D), F32)],
        compiler_params=_params(("arbitrary",)),
    )(yc, yg, dh1b)


def _matmul_tn(a, b, name):
    rows, m = a.shape
    n = b.shape[1]
    tk = _row_tile(rows, DW_ROW_TILE)
    tn = n if n <= 1024 else FF_CHUNK
    tm_ = m if m <= 1024 else FF_CHUNK
    assert n % tn == 0 and m % tm_ == 0
    nk = rows // tk

    def body(a_ref, b_ref, o_ref, acc_ref):
        k = pl.program_id(2)

        @pl.when(k == 0)
        def _():
            acc_ref[...] = jnp.zeros_like(acc_ref)

        acc_ref[...] += _dot(a_ref[...], b_ref[...], _TN)

        @pl.when(k == nk - 1)
        def _():
            o_ref[...] = acc_ref[...].astype(BF16)

    return pl.pallas_call(
        body, name=name, grid=(m // tm_, n // tn, nk),
        in_specs=[pl.BlockSpec((tk, tm_), lambda i, j, k: (k, i)), pl.BlockSpec((tk, tn), lambda i, j, k: (k, j))],
        out_specs=pl.BlockSpec((tm_, tn), lambda i, j, k: (i, j)),
        out_shape=jax.ShapeDtypeStruct((m, n), BF16),
        scratch_shapes=[pltpu.VMEM((tm_, tn), F32)],
        compiler_params=_params(("parallel", "parallel", "arbitrary")),
    )(a, b)


HALO = 32
LANES = 128


def _shifted(win, offsets):
    for r in range(8):
        js = [j for j, k in enumerate(offsets) if k % 8 == r]
        if js:
            rolled = win if r == 0 else pltpu.roll(win, CHUNK + HALO - r, 0)
            for j in js:
                yield j, rolled[offsets[j] - r:offsets[j] - r + CHUNK]


def _glu_into(uc_ref, vs_ref, n_chunk):
    vs_ref[0:CHUNK, :] = jnp.zeros((CHUNK, C_CONV), F32)

    def glu(i, carry):
        base = pl.multiple_of(i * CHUNK, CHUNK)
        val = uc_ref[pl.ds(base, CHUNK), 0:C_CONV]
        gate = uc_ref[pl.ds(base, CHUNK), C_CONV:2 * C_CONV]
        vs_ref[pl.ds(base + CHUNK, CHUNK), :] = val * _sigmoid(gate)
        return carry

    lax.fori_loop(0, n_chunk, glu, 0)


def _fwd_conv(uc, conv_w, conv_b, ln_g, ln_b, n_ex):
    rows = uc.shape[0]
    lp = rows // n_ex
    n_chunk = lp // CHUNK

    def body(uc_ref, w_ref, b_ref, lg_ref, lb_ref, ypre_ref, yc_ref, vs_ref):
        _glu_into(uc_ref, vs_ref, n_chunk)

        def conv(i, carry):
            base = pl.multiple_of(i * CHUNK, CHUNK)
            for lb in range(C_CONV // LANES):
                ls = slice(lb * LANES, (lb + 1) * LANES)
                win = vs_ref[pl.ds(base + CHUNK - HALO, CHUNK + HALO), ls]
                acc = jnp.broadcast_to(b_ref[:, ls], (CHUNK, LANES))
                for j, rows_j in _shifted(win, [HALO - (CONV_W - 1) + j for j in range(CONV_W)]):
                    acc = acc + w_ref[j:j + 1, ls] * rows_j
                ypre_ref[pl.ds(base, CHUNK), ls] = acc
            y = ypre_ref[pl.ds(base, CHUNK), :]
            mu = jnp.mean(y, axis=-1, keepdims=True)
            yc_ = y - mu
            rstd = lax.rsqrt(jnp.mean(yc_ * yc_, axis=-1, keepdims=True) + LN_EPS)
            s = yc_ * rstd * lg_ref[...] + lb_ref[...]
            yc_ref[pl.ds(base, CHUNK), :] = (s * _sigmoid(s)).astype(BF16)
            return carry

        lax.fori_loop(0, n_chunk, conv, 0)

    ex = lambda w: pl.BlockSpec((lp, w), lambda b: (b, 0))
    return pl.pallas_call(
        body, name="fwd_conv", grid=(n_ex,),
        in_specs=[ex(2 * C_CONV), _fixed((32, C_CONV)), _fixed((1, C_CONV)), _fixed((1, C_CONV)), _fixed((1, C_CONV))],
        out_specs=[ex(C_CONV), ex(C_CONV)],
        out_shape=[jax.ShapeDtypeStruct((rows, C_CONV), F32), jax.ShapeDtypeStruct((rows, C_CONV), BF16)],
        scratch_shapes=[pltpu.VMEM((lp + CHUNK, C_CONV), F32)],
        compiler_params=_params(("parallel",)),
    )(uc, conv_w, conv_b, ln_g, ln_b)


def _bwd_conv(uc, ypre, dyc, conv_w, ln_g, ln_b, token, n_ex):
    rows = uc.shape[0]
    lp = rows // n_ex
    n_chunk = lp // CHUNK

    def body(uc_ref, ypre_ref, dyc_ref, w_ref, lg_ref, lb_ref, token_ref, duc_ref, dw_ref, dvec_ref, vs_ref, dys_ref,
             dwacc_ref):
        _glu_into(uc_ref, vs_ref, n_chunk)
        dys_ref[pl.ds(lp, CHUNK), :] = jnp.zeros((CHUNK, C_CONV), F32)
        dwacc_ref[...] = jnp.zeros_like(dwacc_ref)

        def ln_bwd(i, carry):
            dcb, dlg, dlb = carry
            base = pl.multiple_of(i * CHUNK, CHUNK)
            y = ypre_ref[pl.ds(base, CHUNK), :]
            mu = jnp.mean(y, axis=-1, keepdims=True)
            yc_ = y - mu
            rstd = lax.rsqrt(jnp.mean(yc_ * yc_, axis=-1, keepdims=True) + LN_EPS)
            xh = yc_ * rstd
            s = xh * lg_ref[...] + lb_ref[...]
            sg = _sigmoid(s)
            ds = dyc_ref[pl.ds(base, CHUNK), :] * (sg * (1.0 + s * (1.0 - sg)))
            dxh = ds * lg_ref[...]
            dy = rstd * (dxh - jnp.mean(dxh, axis=-1, keepdims=True) - xh * jnp.mean(dxh * xh, axis=-1, keepdims=True))
            dys_ref[pl.ds(base, CHUNK), :] = dy
            return (dcb + jnp.sum(dy, axis=0, keepdims=True), dlg + jnp.sum(ds * xh, axis=0, keepdims=True),
                    dlb + jnp.sum(ds, axis=0, keepdims=True))

        zero = jnp.zeros((1, C_CONV), F32)
        dcb, dlg, dlb = lax.fori_loop(0, n_chunk, ln_bwd, (zero, zero, zero))

        @pl.when(pl.program_id(0) == 0)
        def _():
            dvec_ref[...] = jnp.zeros_like(dvec_ref)
            dw_ref[...] = jnp.zeros_like(dw_ref)

        dvec_ref[0:1, :] += dcb
        dvec_ref[1:2, :] += dlg
        dvec_ref[2:3, :] += dlb

        def taps(i, carry):
            base = pl.multiple_of(i * CHUNK, CHUNK)
            for lb in range(C_CONV // LANES):
                ls = slice(lb * LANES, (lb + 1) * LANES)
                dwin = dys_ref[pl.ds(base, CHUNK + HALO), ls]
                vwin = vs_ref[pl.ds(base + CHUNK - HALO, CHUNK + HALO), ls]
                dy = dwin[0:CHUNK]
                acc = jnp.zeros((CHUNK, LANES), F32)
                for j, rows_j in _shifted(dwin, [CONV_W - 1 - j for j in range(CONV_W)]):
                    acc = acc + w_ref[j:j + 1, ls] * rows_j
                for j, rows_j in _shifted(vwin, [HALO - (CONV_W - 1) + j for j in range(CONV_W)]):
                    dwacc_ref[8 * j:8 * j + 8, ls] += jnp.sum((dy * rows_j).reshape(CHUNK // 8, 8, LANES), axis=0)
                val = uc_ref[pl.ds(base, CHUNK), ls]
                gate = uc_ref[pl.ds(base, CHUNK), C_CONV + lb * LANES:C_CONV + (lb + 1) * LANES]
                sg = _sigmoid(gate)
                duc_ref[pl.ds(base, CHUNK), ls] = (acc * sg).astype(BF16)
                duc_ref[pl.ds(base, CHUNK), C_CONV + lb * LANES:C_CONV + (lb + 1) * LANES] = (
                    acc * val * sg * (1.0 - sg)).astype(BF16)
            return carry

        lax.fori_loop(0, n_chunk, taps, 0)
        for j in range(CONV_W):
            dw_ref[j:j + 1, :] += jnp.sum(dwacc_ref[8 * j:8 * j + 8, :], axis=0, keepdims=True)

    ex = lambda w: pl.BlockSpec((lp, w), lambda b: (b, 0))
    return pl.pallas_call(
        body, name="bwd_conv", grid=(n_ex,),
        in_specs=[ex(2 * C_CONV), ex(C_CONV), ex(C_CONV), _fixed((32, C_CONV)), _fixed((1, C_CONV)), _fixed((1, C_CONV)),
                  _fixed((8, 128))],
        out_specs=[ex(2 * C_CONV), _fixed((32, C_CONV)), _fixed((8, C_CONV))],
        out_shape=[jax.ShapeDtypeStruct((rows, 2 * C_CONV), BF16), jax.ShapeDtypeStruct((32, C_CONV), F32),
                   jax.ShapeDtypeStruct((8, C_CONV), F32)],
        scratch_shapes=[pltpu.VMEM((lp + CHUNK, C_CONV), F32), pltpu.VMEM((lp + CHUNK, C_CONV), F32),
                        pltpu.VMEM((8 * 32, C_CONV), F32)],
        compiler_params=_params(("arbitrary",)),
    )(uc, ypre, dyc, conv_w, ln_g, ln_b, token)


def _seg_chunks(n_chunk):
    return max(c for c in (11, 3, 1) if n_chunk % c == 0)


def _block_mask(shape, row_block, lane_block):
    return (lax.broadcasted_iota(jnp.int32, shape, 0) // row_block) == (lax.broadcasted_iota(jnp.int32, shape, 1) // lane_block)


def _per_head_rows(x, mask):
    return jnp.where(mask, jnp.concatenate([x] * GLA_H, axis=0), 0)


def _fold_heads(full, lane_block):
    lane = lax.broadcasted_iota(jnp.int32, (1, full.shape[1]), 1) // lane_block
    out = jnp.where(lane == 0, full[0:CHUNK], 0.0)
    for h in range(1, GLA_H):
        out = out + jnp.where(lane == h, full[h * CHUNK:(h + 1) * CHUNK], 0.0)
    return out


def _causal_heads():
    return (lax.broadcasted_iota(jnp.int32, (CHUNK, GLA_H * CHUNK), 1) % CHUNK) <= lax.broadcasted_iota(
        jnp.int32, (CHUNK, GLA_H * CHUNK), 0)


def _cumsum_rows(x):
    row = lax.broadcasted_iota(jnp.int32, x.shape, 0)
    s = 1
    while s < CHUNK:
        x = x + jnp.where(row >= s, pltpu.roll(x, s, 0), 0.0)
        s *= 2
    return x


def _rev_cumsum_rows(x):
    row = lax.broadcasted_iota(jnp.int32, x.shape, 0)
    s = 1
    while s < CHUNK:
        x = x + jnp.where(row < CHUNK - s, pltpu.roll(x, CHUNK - s, 0), 0.0)
        s *= 2
    return x


def _gate_terms(lr_ref, w2_ref, gb_ref, rs, first_pos):
    z = _dot(lr_ref[rs, :].astype(BF16), w2_ref[...]) + gb_ref[...]
    la = (jnp.minimum(z, 0.0) - jnp.log(1.0 + jnp.exp(-jnp.abs(z)))) * (1.0 / TAU)
    pos = first_pos + lax.broadcasted_iota(jnp.int32, (CHUNK, 1), 0)
    live = pos >= ZROWS
    la = jnp.where(live, la, 0.0)
    return z, live, _cumsum_rows(la)


def _fwd_gla(qk, vg, lr, w2p, gb, ng, n_ex):
    rows = qk.shape[0]
    lp = rows // n_ex
    n_chunk = lp // CHUNK
    sc = _seg_chunks(n_chunk)
    n_seg = n_chunk // sc
    seg = sc * CHUNK

    def body(qk_ref, vg_ref, lr_ref, w2_ref, gb_ref, ng_ref, yg_ref, o_ref, st_ref, state_ref):
        sidx = pl.program_id(1)

        @pl.when(sidx == 0)
        def _():
            state_ref[...] = jnp.zeros_like(state_ref)

        causal = _causal_heads()
        k_mask = _block_mask((GLA_H * CHUNK, GLA_K), CHUNK, GLA_DK)
        v_mask = _block_mask((GLA_H * CHUNK, GLA_V), CHUNK, GLA_DV)
        s_mask = _block_mask((GLA_V, GLA_K), GLA_DV, GLA_DK)

        def chunk(ci, carry):
            base = pl.multiple_of(ci * CHUNK, CHUNK)
            rs = pl.ds(base, CHUNK)
            _, _, bcum = _gate_terms(lr_ref, w2_ref, gb_ref, rs, (sidx * sc + ci) * CHUNK)
            bl = bcum[CHUNK - 1:CHUNK, :]
            q = qk_ref[rs, 0:GLA_K]
            k = qk_ref[rs, GLA_K:2 * GLA_K]
            qt = (q * (GLA_DK ** -0.5) * jnp.exp(bcum)).astype(BF16)
            kt = (k * jnp.exp(-bcum)).astype(BF16)
            kh = (k * jnp.exp(bl - bcum)).astype(BF16)
            vb = vg_ref[rs, 0:GLA_V].astype(BF16)
            state = state_ref[...]
            st_ref[ci] = state
            a = jnp.where(causal, _dot(qt, _per_head_rows(kt, k_mask), _NT), 0.0)
            o = _dot(a.astype(BF16), _per_head_rows(vb, v_mask)) + _dot(qt, state.astype(BF16), _NT)
            o_ref[rs, :] = o
            for h in range(GLA_H):
                hs = slice(h * GLA_DV, (h + 1) * GLA_DV)
                oh = o[:, hs]
                ro = lax.rsqrt(jnp.mean(oh * oh, axis=-1, keepdims=True) + RMS_EPS)
                g = vg_ref[rs, GLA_V + h * GLA_DV:GLA_V + (h + 1) * GLA_DV]
                yg_ref[rs, hs] = (oh * ro * ng_ref[...] * g * _sigmoid(g)).astype(BF16)
            state_ref[...] = state * jnp.exp(bl) + jnp.where(s_mask, _dot(vb, kh, _TN), 0.0)
            return carry

        lax.fori_loop(0, sc, chunk, 0)

    sg = lambda w: pl.BlockSpec((seg, w), lambda b, s: (b * n_seg + s, 0))
    return pl.pallas_call(
        body, name="fwd_gla", grid=(n_ex, n_seg),
        in_specs=[sg(2 * GLA_K), sg(2 * GLA_V), sg(RANK_P), _fixed((RANK_P, GLA_K)), _fixed((1, GLA_K)), _fixed((1, GLA_DV))],
        out_specs=[sg(GLA_V), sg(GLA_V), pl.BlockSpec((sc, GLA_V, GLA_K), lambda b, s: (b * n_seg + s, 0, 0))],
        out_shape=[jax.ShapeDtypeStruct((rows, GLA_V), BF16), jax.ShapeDtypeStruct((rows, GLA_V), F32),
                   jax.ShapeDtypeStruct((n_ex * n_chunk, GLA_V, GLA_K), F32)],
        scratch_shapes=[pltpu.VMEM((GLA_V, GLA_K), F32)],
        compiler_params=_params(("parallel", "arbitrary")),
    )(qk, vg, lr, w2p, gb, ng)


def _bwd_gla(qk, vg, lr, o, st, dyg, w2p, gb, ng, n_ex):
    rows = qk.shape[0]
    lp = rows // n_ex
    n_chunk = lp // CHUNK
    sc = _seg_chunks(n_chunk)
    n_seg = n_chunk // sc
    seg = sc * CHUNK

    def body(qk_ref, vg_ref, lr_ref, o_ref, st_ref, dyg_ref, w2_ref, gb_ref, ng_ref,
             dqk_ref, dvg_ref, dlr_ref, dw2_ref, dvec_ref, gt_ref, dz_ref):
        step = pl.program_id(1)
        sidx = n_seg - 1 - step

        @pl.when(step == 0)
        def _():
            gt_ref[...] = jnp.zeros_like(gt_ref)

        @pl.when((step == 0) & (pl.program_id(0) == 0))
        def _():
            dw2_ref[...] = jnp.zeros_like(dw2_ref)
            dvec_ref[...] = jnp.zeros_like(dvec_ref)

        causal = _causal_heads()
        k_mask = _block_mask((GLA_H * CHUNK, GLA_K), CHUNK, GLA_DK)
        v_mask = _block_mask((GLA_H * CHUNK, GLA_V), CHUNK, GLA_DV)
        s_mask = _block_mask((GLA_V, GLA_K), GLA_DV, GLA_DK)
        last_row = lax.broadcasted_iota(jnp.int32, (CHUNK, 1), 0) == CHUNK - 1
        ng = ng_ref[...]

        def chunk(ii, dng):
            ci = sc - 1 - ii
            base = pl.multiple_of(ci * CHUNK, CHUNK)
            rs = pl.ds(base, CHUNK)
            z, live, bcum = _gate_terms(lr_ref, w2_ref, gb_ref, rs, (sidx * sc + ci) * CHUNK)
            bl = bcum[CHUNK - 1:CHUNK, :]
            ebl = jnp.exp(bl)
            q = qk_ref[rs, 0:GLA_K]
            k = qk_ref[rs, GLA_K:2 * GLA_K]
            eb = jnp.exp(bcum)
            enb = jnp.exp(-bcum)
            ehb = jnp.exp(bl - bcum)
            qt = q * (GLA_DK ** -0.5) * eb
            kt = k * enb
            kh = k * ehb
            qtb = qt.astype(BF16)
            vb = vg_ref[rs, 0:GLA_V].astype(BF16)
            k_rows = _per_head_rows(kt.astype(BF16), k_mask)
            v_rows = _per_head_rows(vb, v_mask)
            gt = gt_ref[...]
            gtb = gt.astype(BF16)
            s_in = st_ref[ci]
            dos = []
            for h in range(GLA_H):
                hs = slice(h * GLA_DV, (h + 1) * GLA_DV)
                gs = slice(GLA_V + h * GLA_DV, GLA_V + (h + 1) * GLA_DV)
                oh = o_ref[rs, hs]
                ro = lax.rsqrt(jnp.mean(oh * oh, axis=-1, keepdims=True) + RMS_EPS)
                on = oh * ro
                g = vg_ref[rs, gs]
                sg = _sigmoid(g)
                dout = dyg_ref[rs, hs]
                dvg_ref[rs, gs] = (dout * on * ng * (sg * (1.0 + g * (1.0 - sg)))).astype(BF16)
                dw = dout * g * sg
                dng = dng + jnp.sum(dw * on, axis=0, keepdims=True)
                don = dw * ng
                dos.append((ro * (don - on * jnp.mean(don * on, axis=-1, keepdims=True))).astype(BF16))
            dob = jnp.concatenate(dos, axis=1)
            a = jnp.where(causal, _dot(qtb, k_rows, _NT), 0.0).astype(BF16)
            da = jnp.where(causal, _dot(dob, v_rows, _NT), 0.0).astype(BF16)
            dv = _fold_heads(_dot(a, dob, _TN), GLA_DV) + _dot(kh.astype(BF16), gtb, _NT)
            dvg_ref[rs, 0:GLA_V] = dv.astype(BF16)
            dkh = _dot(vb, gtb)
            dqt = _dot(da, k_rows) + _dot(dob, s_in.astype(BF16))
            dkt = _fold_heads(_dot(da, qtb, _TN), GLA_DK)
            dbl = jnp.sum(gt * s_in, axis=0, keepdims=True) * ebl + jnp.sum(dkh * kh, axis=0, keepdims=True)
            dqk_ref[rs, 0:GLA_K] = (dqt * (GLA_DK ** -0.5) * eb).astype(BF16)
            dqk_ref[rs, GLA_K:2 * GLA_K] = (dkt * enb + dkh * ehb).astype(BF16)
            db = dqt * qt - dkt * kt - dkh * kh
            db = jnp.where(last_row, db + dbl, db)
            dla = jnp.where(live, _rev_cumsum_rows(db), 0.0)
            dz_ref[rs, :] = dla * (1.0 / TAU) * (1.0 - _sigmoid(z))
            gt_ref[...] = jnp.where(s_mask, _dot(dob, qtb, _TN), 0.0) + gt * ebl
            return dng

        dng = lax.fori_loop(0, sc, chunk, jnp.zeros((1, GLA_DV), F32))
        dz = dz_ref[...]
        dzb = dz.astype(BF16)
        dlr_ref[...] = _dot(dzb, w2_ref[...], _NT).astype(BF16)
        dw2_ref[...] += _dot(lr_ref[...].astype(BF16), dzb, _TN)
        dvec_ref[0:1, :] += jnp.sum(dz, axis=0, keepdims=True)
        dvec_ref[1:2, 0:GLA_DV] += dng

    sg_ = lambda w: pl.BlockSpec((seg, w), lambda b, s: (b * n_seg + n_seg - 1 - s, 0))
    return pl.pallas_call(
        body, name="bwd_gla", grid=(n_ex, n_seg),
        in_specs=[sg_(2 * GLA_K), sg_(2 * GLA_V), sg_(RANK_P), sg_(GLA_V),
                  pl.BlockSpec((sc, GLA_V, GLA_K), lambda b, s: (b * n_seg + n_seg - 1 - s, 0, 0)), sg_(GLA_V),
                  _fixed((RANK_P, GLA_K)), _fixed((1, GLA_K)), _fixed((1, GLA_DV))],
        out_specs=[sg_(2 * GLA_K), sg_(2 * GLA_V), sg_(RANK_P), _fixed((RANK_P, GLA_K)), _fixed((8, GLA_K))],
        out_shape=[jax.ShapeDtypeStruct((rows, 2 * GLA_K), BF16), jax.ShapeDtypeStruct((rows, 2 * GLA_V), BF16),
                   jax.ShapeDtypeStruct((rows, RANK_P), BF16), jax.ShapeDtypeStruct((RANK_P, GLA_K), F32),
                   jax.ShapeDtypeStruct((8, GLA_K), F32)],
        scratch_shapes=[pltpu.VMEM((GLA_V, GLA_K), F32), pltpu.VMEM((seg, GLA_K), F32)],
        compiler_params=_params(("arbitrary", "arbitrary")),
    )(qk, vg, lr, o, st, dyg, w2p, gb, ng)


def _local_step(x, tgt, p, w_out, pass_on, late_weights, send_early):
    n_ex, seq, _ = x.shape
    lp = seq + LEAD
    rows = n_ex * lp
    meta = jnp.broadcast_to(p["meta"][None], (n_ex, N_META, D))
    h0 = jnp.concatenate([jnp.zeros((n_ex, ZROWS, D), F32), meta, x], axis=1).reshape(rows, D)
    tgt_p = jnp.pad(tgt, ((0, 0), (LEAD, 0), (0, 0))).reshape(rows, D)

    uc, qk, vg, lr, n1 = _fwd_inproj(h0, p["g1"], p["w_in"])
    ypre, yc = _fwd_conv(uc, p["conv_w"], p["conv_b"], p["ln_g"], p["ln_b"], n_ex)
    yg, o, st = _fwd_gla(qk, vg, lr, p["w2"], p["gb"], p["ng"], n_ex)
    h1, n2 = _fwd_outproj(yc, yg, h0, w_out, p["g2"], pass_on(yg))
    wg, wu, wd = late_weights(n2)
    f, da, db, dh2, dh1, dh1b, part = _ffn_rows(h1, n2, tgt_p, wg, wu, wd, p["g2"], p["g3"], lp)
    g = {}
    token = send_early("ffn", [_dw_blocked(n2, [da], FF_S, "dw_gate"), _dw_blocked(n2, [db], FF_S, "dw_up"),
                               _matmul_tn(f, dh2, "dw_down").reshape(N_DEV, FF_S, D)])
    dyc, dyg = _bwd_outproj(dh1b, w_out, token)
    token = send_early("out", [_dw_out(yc, yg, dh1b).reshape(N_DEV, W_OUT_S, D)])
    duc, g["conv_w"], g["conv_vec"] = _bwd_conv(uc, ypre, dyc, p["conv_w"], p["ln_g"], p["ln_b"], token, n_ex)
    dqk, dvg, dlr, g["w2"], g["gla_vec"] = _bwd_gla(qk, vg, lr, o, st, dyg, p["w2"], p["gb"], p["ng"], n_ex)
    dh0, g["in_vec"], g["meta"] = _bwd_inproj(duc, dqk, dvg, dlr, dh1, h0, p["w_in"], p["g1"], lp)
    g["w_in"] = _dw_blocked(n1, [duc, dqk, dvg, dlr], W_IN_S, "dw_in")
    g["ffn_vec"] = part
    return dh0.reshape(n_ex, lp, D)[:, LEAD:], g


W_IN_S = D_IN // N_DEV
W_OUT_S = D // N_DEV
FF_S = D_FF // N_DEV
CONV_S = C_CONV // N_DEV
GATE_S = GLA_K // N_DEV
SMALL_PACK = 64
CONV_ROW = 16
GATE_ROW = 48
VEC_ROWS = 16
_VEC_ROWS = (("norm_mix_g", D), ("conv_b", C_CONV), ("conv_ln_g", C_CONV), ("conv_ln_b", C_CONV), ("gla_gate_b", GLA_K),
             ("gla_norm_g", GLA_DV), ("norm_ffn_g", D), ("norm_final_g", D))
LOSS_ROW = len(_VEC_ROWS)


def _position():
    return lax.axis_index("x"), lax.axis_index("y"), lax.axis_index("c")


def _any():
    return pl.BlockSpec(memory_space=pl.ANY)


def _all_gather(mats, meta, conv_w, w2, n_now):
    n_mat = len(mats)
    n_t = n_mat + 1
    n_later = n_mat - n_now
    now = list(range(n_now)) + [n_mat]

    def body(*refs):
        ins = refs[0:n_mat]
        meta_ref, cw_ref, w2_ref = refs[n_mat:n_mat + 3]
        outs = refs[n_mat + 3:n_mat + 3 + n_t]
        later = refs[n_mat + 3 + n_t:n_mat + 3 + n_t + n_later]
        stage_now = refs[n_mat + 3 + n_t + n_later:n_mat + 4 + n_t + n_later + n_now]
        send_sems, recv_sems, local_sems = refs[n_mat + 4 + n_t + n_later + n_now:]
        stage = list(stage_now[0:n_now]) + list(later) + [stage_now[n_now]]
        for s_ref, w_ref in zip(stage, ins):
            s_ref[...] = w_ref[...].astype(BF16)
        sp = stage[n_mat]
        sp[...] = jnp.zeros_like(sp)
        sp[0:N_META, :] = meta_ref[...]
        sp[CONV_ROW:CONV_ROW + CONV_W, 0:CONV_S] = cw_ref[...]
        sp[GATE_ROW:GATE_ROW + RANK, 0:GATE_S] = w2_ref[...]

        x, y, c = _position()
        me, sibling = (x, y, c), (x, y, 1 - c)
        chips = [(1 - x, y), (x, 1 - y), (1 - x, 1 - y)]

        def blk(t, p):
            return outs[t].at[4 * p[0] + 2 * p[1] + p[2]]

        def copy(t, k, block, to, staged=False):
            return pltpu.make_async_remote_copy(
                src_ref=stage[t] if staged else blk(t, block), dst_ref=blk(t, block),
                send_sem=send_sems.at[t, k], recv_sem=recv_sems.at[t, k], device_id=to, device_id_type=MESH)

        mine = [pltpu.make_async_copy(stage[t], blk(t, me), local_sems.at[t]) for t in range(n_t)]
        for cp in mine:
            cp.start()
        first = []
        for t in now:
            first.append(copy(t, 0, me, sibling, staged=True))
            first += [copy(t, 1 + j, me, (*chip, c), staged=True) for j, chip in enumerate(chips)]
        for cp in first:
            cp.start()
        passed = []
        for t in now:
            for j, chip in enumerate(chips):
                copy(t, 1 + j, (*chip, c), me).wait_recv()
                passed.append(copy(t, 4 + j, (*chip, c), sibling))
                passed[-1].start()
        for t in now:
            copy(t, 0, sibling, me).wait_recv()
            for j, chip in enumerate(chips):
                copy(t, 4 + j, (*chip, 1 - c), me).wait_recv()
        for cp in first + passed:
            cp.wait_send()
        for cp in mine:
            cp.wait()

    shapes = [(N_DEV,) + m.shape for m in mats]
    res = pl.pallas_call(
        body, name="all_gather",
        out_shape=[jax.ShapeDtypeStruct(s, BF16) for s in shapes] + [jax.ShapeDtypeStruct((N_DEV, SMALL_PACK, 128), F32)]
        + [jax.ShapeDtypeStruct(m.shape, BF16) for m in mats[n_now:]],
        in_specs=[_whole_vmem()] * (n_mat + 3), out_specs=[_any()] * n_t + [_whole_vmem()] * n_later,
        scratch_shapes=[pltpu.VMEM(m.shape, BF16) for m in mats[:n_now]] + [pltpu.VMEM((SMALL_PACK, 128), F32)]
        + [pltpu.SemaphoreType.DMA((n_t, 7)), pltpu.SemaphoreType.DMA((n_t, 7)), pltpu.SemaphoreType.DMA((n_t,))],
        compiler_params=pltpu.CompilerParams(vmem_limit_bytes=VMEM_LIMIT),
    )(*mats, meta, conv_w, w2)
    return res[0:n_mat], res[n_mat], res[n_t:]


_HBM = pl.BlockSpec(memory_space=pltpu.HBM)
_SEM = pl.BlockSpec(memory_space=pltpu.SEMAPHORE)
_EFFECT = pltpu.SideEffectType.DATAFLOW_SIDE_EFFECTING


_N_ROUTES = {"scatter": 7, "first": 4, "forward": 3}


def _routes(mode):
    x, y, c = _position()
    me = 4 * x + 2 * y + c
    if mode == "scatter":
        out = []
        for k in range(1, N_DEV):
            px = 1 - x if k & 4 else x
            py = 1 - y if k & 2 else y
            pc = 1 - c if k & 1 else c
            out.append(((px, py, pc), 4 * px + 2 * py + pc, me))
        return out
    if mode == "first":
        return [(pos, None, me) for pos in ((x, y, 1 - c), (1 - x, y, c), (x, 1 - y, c), (1 - x, 1 - y, c))]
    assert mode == "forward"
    return [((x, y, 1 - c), 4 * px + 2 * py + c, 4 * px + 2 * py + c) for px, py in ((1 - x, y), (x, 1 - y), (1 - x, 1 - y))]


def _route_copies(mode, n, src_refs, land_refs, send_sems, recv_sems):
    nr = _N_ROUTES[mode]
    for i, (pos, src_blk, dst_blk) in enumerate(_routes(mode)):
        for t in range(n):
            src = land_refs[t] if mode == "forward" else src_refs[t]
            yield pltpu.make_async_remote_copy(
                src_ref=src if src_blk is None else src.at[src_blk], dst_ref=land_refs[t].at[dst_blk],
                send_sem=send_sems.at[nr * t + i], recv_sem=recv_sems.at[nr * t + i], device_id=pos, device_id_type=MESH)


def _in_hbm(a):
    return pltpu.with_memory_space_constraint(a, pltpu.HBM)


def _send_start(name, srcs, lands, mode):
    n, ns = len(lands), len(srcs)
    nsem = _N_ROUTES[mode] * n

    def body(*refs):
        src_refs, land_refs = refs[0:ns], refs[ns:ns + n]
        send_sems, recv_sems = refs[ns + n:ns + n + 2]
        token = refs[2 * (ns + n) + 2]
        for cp in _route_copies(mode, n, src_refs, land_refs, send_sems, recv_sems):
            cp.start()
        token[...] = jnp.zeros_like(token)

    bufs = list(srcs) + list(lands)
    res = pl.pallas_call(
        body, name=name,
        out_shape=(pltpu.SemaphoreType.DMA((nsem,)), pltpu.SemaphoreType.DMA((nsem,)),
                   *[pltpu.HBM(b.shape, b.dtype) for b in bufs], jax.ShapeDtypeStruct((8, 128), F32)),
        in_specs=[_HBM] * len(bufs), out_specs=(_SEM, _SEM, *[_HBM] * len(bufs), _whole_vmem()),
        input_output_aliases={i: 2 + i for i in range(len(bufs))},
        compiler_params=pltpu.CompilerParams(has_side_effects=_EFFECT),
    )(*[_in_hbm(b) for b in bufs])
    return res[0], res[1], res[2:2 + ns], res[2 + ns:2 + ns + n], res[2 + ns + n]


def _send_wait(name, send_sems, recv_sems, srcs, lands, mode, after):
    n, ns = len(lands), len(srcs)

    def body(*refs):
        src_refs, land_refs = refs[0:ns], refs[ns:ns + n]
        send_sems, recv_sems = refs[ns + n:ns + n + 2]
        for cp in _route_copies(mode, n, src_refs, land_refs, send_sems, recv_sems):
            cp.wait_send()
            cp.wait_recv()

    bufs = list(srcs) + list(lands)
    res = pl.pallas_call(
        body, name=name,
        out_shape=tuple(pltpu.HBM(b.shape, b.dtype) for b in bufs),
        in_specs=[_HBM] * len(bufs) + [_SEM, _SEM, _any()], out_specs=tuple([_HBM] * len(bufs)),
        input_output_aliases={i: i for i in range(len(bufs))},
        compiler_params=pltpu.CompilerParams(has_side_effects=_EFFECT),
    )(*bufs, send_sems, recv_sems, after)
    return res[0:ns], res[ns:ns + n]


def _unshard_in(a_in, a_small, token):
    def body(a_ref, s_ref, token_ref, w_ref, meta_ref, cw_ref, w2_ref):
        w_ref[:, D_IN:D_INP] = jnp.zeros((D, D_INP - D_IN), BF16)
        w2_ref[...] = jnp.zeros_like(w2_ref)
        for d in range(N_DEV):
            w_ref[:, d * W_IN_S:(d + 1) * W_IN_S] = a_ref[d]
            meta_ref[:, d * 128:(d + 1) * 128] = s_ref[d, 0:N_META, :]
            cw_ref[:, d * CONV_S:(d + 1) * CONV_S] = s_ref[d, CONV_ROW:CONV_ROW + 32, 0:CONV_S]
            w2_ref[0:RANK, d * GATE_S:(d + 1) * GATE_S] = s_ref[d, GATE_ROW:GATE_ROW + RANK, 0:GATE_S].astype(BF16)

    return pl.pallas_call(
        body, name="unshard_in",
        out_shape=[jax.ShapeDtypeStruct((D, D_INP), BF16), jax.ShapeDtypeStruct((N_META, D), F32),
                   jax.ShapeDtypeStruct((32, C_CONV), F32), jax.ShapeDtypeStruct((RANK_P, GLA_K), BF16)],
        compiler_params=pltpu.CompilerParams(vmem_limit_bytes=VMEM_LIMIT),
    )(a_in, a_small, token)


def _unshard_ffn(a_g, a_u):
    def body(g_ref, u_ref, wg_ref, wu_ref):
        for d in range(N_DEV):
            wg_ref[:, d * FF_S:(d + 1) * FF_S] = g_ref[d]
            wu_ref[:, d * FF_S:(d + 1) * FF_S] = u_ref[d]

    return pl.pallas_call(
        body, name="unshard_ffn", out_shape=[jax.ShapeDtypeStruct((D, D_FF), BF16)] * 2,
        compiler_params=pltpu.CompilerParams(vmem_limit_bytes=VMEM_LIMIT),
    )(a_g, a_u)


def _pack_small(g):
    def body(meta_ref, cw_ref, w2_ref, in_vec, ffn_vec, conv_vec, gla_vec, sp, vp):
        sp[...] = jnp.zeros_like(sp)
        vp[...] = jnp.zeros_like(vp)
        for d in range(N_DEV):
            sp[d, 0:N_META, :] = meta_ref[:, d * 128:(d + 1) * 128]
            sp[d, CONV_ROW:CONV_ROW + 32, 0:CONV_S] = cw_ref[:, d * CONV_S:(d + 1) * CONV_S]
            sp[d, GATE_ROW:GATE_ROW + RANK, 0:GATE_S] = w2_ref[0:RANK, d * GATE_S:(d + 1) * GATE_S]
            vp[d, 0:1, :] = in_vec[0:1, :]
            vp[d, 1:4, 0:C_CONV] = conv_vec[0:3, :]
            vp[d, 4:5, 0:GLA_K] = gla_vec[0:1, :]
            vp[d, 5:6, 0:GLA_DV] = gla_vec[1:2, 0:GLA_DV]
            vp[d, 6:7, :] = ffn_vec[1:2, :]
            vp[d, 7:8, :] = ffn_vec[0:1, :]
            vp[d, LOSS_ROW:LOSS_ROW + 1, :] = ffn_vec[2:3, :]

    return pl.pallas_call(
        body, name="pack_small",
        out_shape=[jax.ShapeDtypeStruct((N_DEV, SMALL_PACK, 128), F32), jax.ShapeDtypeStruct((N_DEV, VEC_ROWS, D), F32)],
    )(g["meta"], g["conv_w"], g["w2"], g["in_vec"], g["ffn_vec"], g["conv_vec"], g["gla_vec"])


def _adamw(w, g, m, v):
    m = ADAM_B1 * m + (1.0 - ADAM_B1) * g
    v = ADAM_B2 * v + (1.0 - ADAM_B2) * (g * g)
    m_hat = m / (1.0 - ADAM_B1 ** ADAM_STEP)
    v_hat = v / (1.0 - ADAM_B2 ** ADAM_STEP)
    return -ADAM_LR * (m_hat / (jnp.sqrt(v_hat) + ADAM_EPS) + ADAM_WD * w), m, v


def _update_matrix(recv, own, me, w, m, v, name):
    _, r, c = recv.shape
    tr = _row_tile(r, 256)

    def body(me_ref, recv_ref, own_ref, w_ref, m_ref, v_ref, g_ref, d_ref, nm_ref, nv_ref):
        g = jnp.zeros((tr, c), F32)
        for s in range(N_DEV):
            g = g + jnp.where(me_ref[0] == s, own_ref[...], recv_ref[s]).astype(F32)
        g_ref[...] = g
        d_ref[...], nm_ref[...], nv_ref[...] = _adamw(w_ref[...], g, m_ref[...], v_ref[...])

    one = pl.BlockSpec((None, tr, c), lambda i, me_ref: (0, i, 0))
    return pl.pallas_call(
        body, name=name,
        grid_spec=pltpu.PrefetchScalarGridSpec(
            num_scalar_prefetch=1, grid=(r // tr,),
            in_specs=[pl.BlockSpec((N_DEV, tr, c), lambda i, me_ref: (0, i, 0)),
                      pl.BlockSpec((None, tr, c), lambda i, me_ref: (me_ref[0], i, 0)), one, one, one],
            out_specs=[one] * 4),
        out_shape=[jax.ShapeDtypeStruct((1, r, c), F32)] * 4,
        compiler_params=_params(("parallel",)),
    )(me, recv, own, w, m, v)


_SMALL = ("meta_tokens", "conv_w", "gla_w_gate2") + tuple(n for n, _ in _VEC_ROWS)


def _update_small(me, srecv, vrecv, sown, vown, w, m, v):
    n = len(_SMALL)

    def body(*refs):
        me_ref, s_ref, v_ref, so_ref, vo_ref = refs[0:5]
        w_refs, m_refs, v_refs = refs[5:5 + n], refs[5 + n:5 + 2 * n], refs[5 + 2 * n:5 + 3 * n]
        outs = refs[5 + 3 * n:]
        ssum = jnp.zeros((SMALL_PACK, 128), F32)
        vsum = jnp.zeros((VEC_ROWS, D), F32)
        for s in range(N_DEV):
            ssum = ssum + jnp.where(me_ref[0] == s, so_ref[s], s_ref[s])
            vsum = vsum + jnp.where(me_ref[0] == s, vo_ref[s], v_ref[s])
        grads = [ssum[0:N_META, :], ssum[CONV_ROW:CONV_ROW + CONV_W, 0:CONV_S], ssum[GATE_ROW:GATE_ROW + RANK, 0:GATE_S]]
        grads += [vsum[i:i + 1, 0:width] for i, (_, width) in enumerate(_VEC_ROWS)]
        for i, g in enumerate(grads):
            d, nm, nv = _adamw(w_refs[i][...], g, m_refs[i][...], v_refs[i][...])
            outs[i][...] = g
            outs[n + i][...] = d
            outs[2 * n + i][...] = nm
            outs[3 * n + i][...] = nv
        outs[4 * n][...] = vsum[LOSS_ROW:LOSS_ROW + 1, 0:128]

    shapes = [jax.ShapeDtypeStruct(t.shape, F32) for t in w]
    res = pl.pallas_call(
        body, name="update_small", out_shape=shapes * 4 + [jax.ShapeDtypeStruct((1, 128), F32)],
        in_specs=[pl.BlockSpec(memory_space=pltpu.SMEM)] + [_whole_vmem()] * (4 + 3 * n),
    )(me, srecv, vrecv, sown, vown, *w, *m, *v)
    return res[0:n], res[n:2 * n], res[2 * n:3 * n], res[3 * n:4 * n], res[4 * n]


_WEIGHTS = ("meta_tokens", "norm_mix_g", "w_in", "conv_w", "conv_b", "conv_ln_g", "conv_ln_b", "gla_w_gate2", "gla_gate_b",
            "gla_norm_g", "w_out", "norm_ffn_g", "w_ffn_gate", "w_ffn_up", "w_ffn_down", "norm_final_g")
_MATRICES = ("w_in", "w_out", "w_ffn_gate", "w_ffn_up", "w_ffn_down")


def kernel(x, meta_tokens, norm_mix_g, w_in, conv_w, conv_b, conv_ln_g, conv_ln_b, gla_w_gate2, gla_gate_b, gla_norm_g, w_out, norm_ffn_g, w_ffn_gate, w_ffn_up, w_ffn_down, norm_final_g, loss_target, m_meta_tokens, m_norm_mix_g, m_w_in, m_conv_w, m_conv_b, m_conv_ln_g, m_conv_ln_b, m_gla_w_gate2, m_gla_gate_b, m_gla_norm_g, m_w_out, m_norm_ffn_g, m_w_ffn_gate, m_w_ffn_up, m_w_ffn_down, m_norm_final_g, v_meta_tokens, v_norm_mix_g, v_w_in, v_conv_w, v_conv_b, v_conv_ln_g, v_conv_ln_b, v_gla_w_gate2, v_gla_gate_b, v_gla_norm_g, v_w_out, v_norm_ffn_g, v_w_ffn_gate, v_w_ffn_up, v_w_ffn_down, v_norm_final_g):
    given = dict(locals())
    two_d = lambda a: a.reshape(1, -1) if a.ndim == 1 else a.reshape(a.shape[-2:])
    fams = [{n: given[pre + n] for n in _WEIGHTS} for pre in ("", "m_", "v_")]
    w = fams[0]

    bufs, a_small, shards = _all_gather(
        [two_d(w[n]) for n in _MATRICES], w["meta_tokens"], two_d(w["conv_w"]), two_d(w["gla_w_gate2"]), 2)
    gather = _send_start("gather_first_start", shards, bufs[2:], "first")
    w_in, meta, conv_taps, w2 = _unshard_in(bufs[0], a_small, gather[4])
    p = dict(meta=meta, conv_w=conv_taps, w2=w2, w_in=w_in, g1=norm_mix_g, conv_b=conv_b, ln_g=conv_ln_g, ln_b=conv_ln_b,
             gb=gla_gate_b, ng=gla_norm_g, g2=norm_ffn_g, g3=two_d(norm_final_g))
    passed = {}

    def pass_on(after):
        _, lands = _send_wait("gather_first_wait", *gather[0:4], "first", after)
        passed["sent"] = _send_start("gather_forward_start", [], lands, "forward")
        return passed["sent"][4]

    def late_weights(after):
        _, (a_g, a_u, a_d) = _send_wait("gather_forward_wait", *passed["sent"][0:4], "forward", after)
        wg, wu = _unshard_ffn(a_g, a_u)
        return wg, wu, a_d.reshape(D_FF, D)

    sent = {}

    def send_early(tag, mats):
        lands = [_in_hbm(lax.empty(m_.shape, m_.dtype)) for m_ in mats]
        sent[tag] = _send_start("scatter_" + tag + "_start", mats, lands, "scatter")
        return sent[tag][4]

    grad_x, g = _local_step(x, loss_target, p, bufs[1].reshape(D, D), pass_on, late_weights, send_early)

    token = send_early("last", [g["w_in"], *_pack_small(g)])
    x_, y_, c_ = _position()
    me = (4 * x_ + 2 * y_ + c_).astype(jnp.int32).reshape(1)
    res = {}
    for tag, names in (("ffn", ("w_ffn_gate", "w_ffn_up", "w_ffn_down")), ("out", ("w_out",))):
        own, recv = _send_wait("scatter_" + tag + "_wait", *sent[tag][0:4], "scatter", token)
        for n, o_, r_ in zip(names, own, recv):
            res[n] = _update_matrix(r_, o_, me, *[f[n] for f in fams], "update_" + n)
    (o_in, sown, vown), (r_in, srecv, vrecv) = _send_wait(
        "scatter_last_wait", *sent["last"][0:4], "scatter", res["w_out"][1])
    res["w_in"] = _update_matrix(r_in, o_in, me, *[f["w_in"] for f in fams], "update_w_in")
    small = _update_small(me, srecv, vrecv, sown, vown, *[[two_d(f[n]) for n in _SMALL] for f in fams])
    for i, n in enumerate(_SMALL):
        res[n] = [fam[i].reshape(w[n].shape) for fam in small[0:4]]
    outs = [small[4][0, 0], grad_x]
    for k in range(4):
        outs += [res[n][k] for n in _WEIGHTS]
    return tuple(outs)
```

```python
import functools

import jax
import jax.numpy as jnp
from jax import lax
from jax.experimental import pallas as pl
from jax.experimental.pallas import tpu as pltpu

F32 = jnp.float32
BF16 = jnp.bfloat16

D = 1024
N_META = 16
C_CONV = 512
CONV_W = 31
GLA_H = 4
GLA_DK = 64
GLA_DV = 128
GLA_K = GLA_H * GLA_DK
GLA_V = GLA_H * GLA_DV
RANK = 16
RANK_P = 128
TAU = 16.0
CHUNK = 64
LEAD = CHUNK
ZROWS = LEAD - N_META
D_IN = 2 * C_CONV + 2 * GLA_K + 2 * GLA_V + RANK
D_INP = D_IN - RANK + RANK_P
D_FF = 2816
FF_CHUNK = 1408
FF_SPLIT = (0, 1536, D_FF)
RMS_EPS = 1e-6
LN_EPS = 1e-5
N_DEV = 8

ADAM_LR = 0.001
ADAM_B1 = 0.9
ADAM_B2 = 0.999
ADAM_EPS = 1e-08
ADAM_WD = 0.01
ADAM_STEP = 10

VMEM_LIMIT = 60 * 1024 * 1024
ROW_TILE = 1056
FFN_ROW_TILE = 352
DW_ROW_TILE = 1408
MESH = pl.DeviceIdType.MESH

_NN = (((1,), (0,)), ((), ()))
_NT = (((1,), (1,)), ((), ()))
_TN = (((0,), (0,)), ((), ()))


def _dot(a, b, dims=_NN):
    return lax.dot_general(a, b, dims, preferred_element_type=F32)


def _sigmoid(x):
    return 1.0 / (1.0 + jnp.exp(-x))


def _row_tile(rows, target):
    best = None
    for t in range(16, min(rows, target) + 1, 16):
        if rows % t == 0:
            best = t
    assert best is not None, rows
    return best


def _params(sem=None):
    return pltpu.CompilerParams(dimension_semantics=sem, vmem_limit_bytes=VMEM_LIMIT)


def _whole_vmem():
    return pl.BlockSpec(memory_space=pltpu.VMEM)


def _rows(tm, width):
    return pl.BlockSpec((tm, width), lambda i: (i, 0))


def _fixed(shape):
    return pl.BlockSpec(shape, lambda *_: (0,) * len(shape))


def _fwd_inproj(h0, g1, w_in):
    rows = h0.shape[0]
    tm = _row_tile(rows, ROW_TILE)

    def body(h_ref, g_ref, w_ref, uc_ref, qk_ref, vg_ref, lr_ref, n1_ref):
        h = h_ref[...]
        r = lax.rsqrt(jnp.mean(h * h, axis=-1, keepdims=True) + RMS_EPS)
        n = (h * r * g_ref[...]).astype(BF16)
        n1_ref[...] = n
        uc_ref[...] = _dot(n, w_ref[:, 0:1024])
        qk_ref[...] = _dot(n, w_ref[:, 1024:1536])
        vg_ref[...] = _dot(n, w_ref[:, 1536:2560])
        lr_ref[...] = _dot(n, w_ref[:, 2560:2688])

    return pl.pallas_call(
        body, name="fwd_inproj", grid=(rows // tm,),
        in_specs=[_rows(tm, D), _fixed((1, D)), _whole_vmem()],
        out_specs=[_rows(tm, 1024), _rows(tm, 512), _rows(tm, 1024), _rows(tm, RANK_P), _rows(tm, D)],
        out_shape=[jax.ShapeDtypeStruct((rows, 1024), F32), jax.ShapeDtypeStruct((rows, 512), F32),
                   jax.ShapeDtypeStruct((rows, 1024), F32), jax.ShapeDtypeStruct((rows, RANK_P), F32),
                   jax.ShapeDtypeStruct((rows, D), BF16)],
        compiler_params=_params(("parallel",)),
    )(h0, g1, w_in)


def _fwd_outproj(yc, yg, h0, w_out, g2, token):
    rows = h0.shape[0]
    tm = _row_tile(rows, ROW_TILE)

    def body(yc_ref, yg_ref, h_ref, w_ref, g_ref, token_ref, h1_ref, n2_ref):
        h1 = h_ref[...] + _dot(yc_ref[...], w_ref[0:C_CONV, :]) + _dot(yg_ref[...], w_ref[C_CONV:D, :])
        h1_ref[...] = h1
        r = lax.rsqrt(jnp.mean(h1 * h1, axis=-1, keepdims=True) + RMS_EPS)
        n2_ref[...] = (h1 * r * g_ref[...]).astype(BF16)

    return pl.pallas_call(
        body, name="fwd_outproj", grid=(rows // tm,),
        in_specs=[_rows(tm, C_CONV), _rows(tm, GLA_V), _rows(tm, D), _whole_vmem(), _fixed((1, D)), _fixed((8, 128))],
        out_specs=[_rows(tm, D), _rows(tm, D)],
        out_shape=[jax.ShapeDtypeStruct((rows, D), F32), jax.ShapeDtypeStruct((rows, D), BF16)],
        compiler_params=_params(("parallel",)),
    )(yc, yg, h0, w_out, g2, token)


def _ffn_rows(h1, n2, tgt, wg, wu, wd, g2, g3, rows_per_example):
    rows = h1.shape[0]
    tm = _row_tile(rows, FFN_ROW_TILE)
    ff_blocks = [slice(lo, hi) for lo, hi in zip(FF_SPLIT[:-1], FF_SPLIT[1:])]

    def body(h1_ref, n2_ref, t_ref, wg_ref, wu_ref, wd_ref, g2_ref, g3_ref,
             f_ref, da_ref, db_ref, dh2_ref, dh1_ref, dh1b_ref, part_ref):
        i = pl.program_id(0)
        n2 = n2_ref[...]
        y2 = jnp.zeros((tm, D), F32)
        for cs in ff_blocks:
            a = _dot(n2, wg_ref[cs, :], _NT)
            b = _dot(n2, wu_ref[cs, :], _NT)
            f = (a * _sigmoid(a) * b).astype(BF16)
            f_ref[:, cs] = f
            da_ref[:, cs] = a.astype(BF16)
            db_ref[:, cs] = b.astype(BF16)
            y2 = y2 + _dot(f, wd_ref[cs, :])
        h1 = h1_ref[...]
        h2 = h1 + y2
        r3 = lax.rsqrt(jnp.mean(h2 * h2, axis=-1, keepdims=True) + RMS_EPS)
        xh3 = h2 * r3
        g3 = g3_ref[...]
        pos = (i * tm + lax.broadcasted_iota(jnp.int32, (tm, 1), 0)) % rows_per_example
        valid = pos >= LEAD
        err = jnp.where(valid, xh3 * g3 - t_ref[...], 0.0)
        loss = 0.5 / D * jnp.sum(jnp.sum(err * err, axis=-1, keepdims=True), axis=0, keepdims=True)
        dy = err * (1.0 / D)
        dg3 = jnp.sum(dy * xh3, axis=0, keepdims=True)
        dxh = dy * g3
        dh2 = r3 * (dxh - xh3 * jnp.mean(dxh * xh3, axis=-1, keepdims=True))
        dh2b = dh2.astype(BF16)
        dh2_ref[...] = dh2b
        dn2 = jnp.zeros((tm, D), F32)
        for cs in ff_blocks:
            df = _dot(dh2b, wd_ref[cs, :], _NT)
            a = da_ref[:, cs].astype(F32)
            b = db_ref[:, cs].astype(F32)
            sg = _sigmoid(a)
            da = (df * b * sg * (1.0 + a * (1.0 - sg))).astype(BF16)
            db = (df * a * sg).astype(BF16)
            da_ref[:, cs] = da
            db_ref[:, cs] = db
            dn2 = dn2 + _dot(da, wg_ref[cs, :]) + _dot(db, wu_ref[cs, :])
        r2 = lax.rsqrt(jnp.mean(h1 * h1, axis=-1, keepdims=True) + RMS_EPS)
        xh2 = h1 * r2
        dg2 = jnp.sum(dn2 * xh2, axis=0, keepdims=True)
        dxh2 = dn2 * g2_ref[...]
        dh1 = dh2 + r2 * (dxh2 - xh2 * jnp.mean(dxh2 * xh2, axis=-1, keepdims=True))
        dh1_ref[...] = dh1
        dh1b_ref[...] = dh1.astype(BF16)

        @pl.when(i == 0)
        def _():
            part_ref[...] = jnp.zeros_like(part_ref)

        part_ref[0:1, :] += dg3
        part_ref[1:2, :] += dg2
        part_ref[2:3, :] += jnp.broadcast_to(loss, (1, D))

    return pl.pallas_call(
        body, name="ffn_rows", grid=(rows // tm,),
        in_specs=[_rows(tm, D), _rows(tm, D), _rows(tm, D), _whole_vmem(), _whole_vmem(), _whole_vmem(),
                  _fixed((1, D)), _fixed((1, D))],
        out_specs=[_rows(tm, D_FF), _rows(tm, D_FF), _rows(tm, D_FF), _rows(tm, D), _rows(tm, D), _rows(tm, D),
                   _fixed((8, D))],
        out_shape=[jax.ShapeDtypeStruct((rows, D_FF), BF16)] * 3
        + [jax.ShapeDtypeStruct((rows, D), BF16), jax.ShapeDtypeStruct((rows, D), F32),
           jax.ShapeDtypeStruct((rows, D), BF16), jax.ShapeDtypeStruct((8, D), F32)],
        compiler_params=_params(("arbitrary",)),
    )(h1, n2, tgt, wg, wu, wd, g2, g3)


def _bwd_outproj(dh1b, w_out, token):
    rows = dh1b.shape[0]
    tm = _row_tile(rows, ROW_TILE)

    def body(d_ref, w_ref, token_ref, dyc_ref, dyg_ref):
        d = d_ref[...]
        dyc_ref[...] = _dot(d, w_ref[0:C_CONV, :], _NT)
        dyg_ref[...] = _dot(d, w_ref[C_CONV:D, :], _NT)

    return pl.pallas_call(
        body, name="bwd_outproj", grid=(rows // tm,),
        in_specs=[_rows(tm, D), _whole_vmem(), _fixed((8, 128))],
        out_specs=[_rows(tm, C_CONV), _rows(tm, GLA_V)],
        out_shape=[jax.ShapeDtypeStruct((rows, C_CONV), F32), jax.ShapeDtypeStruct((rows, GLA_V), F32)],
        compiler_params=_params(("parallel",)),
    )(dh1b, w_out, token)


def _bwd_inproj(duc, dqk, dvg, dlr, dh1, h0, w_in, g1, rows_per_example):
    rows = h0.shape[0]
    tm = _row_tile(rows_per_example, ROW_TILE)
    tiles_per_example = rows_per_example // tm

    def body(duc_ref, dqk_ref, dvg_ref, dlr_ref, dh1_ref, h_ref, w_ref, g_ref, dh0_ref, part_ref, dmeta_ref):
        dn = (_dot(duc_ref[...], w_ref[:, 0:1024], _NT) + _dot(dqk_ref[...], w_ref[:, 1024:1536], _NT)
              + _dot(dvg_ref[...], w_ref[:, 1536:2560], _NT) + _dot(dlr_ref[...], w_ref[:, 2560:2688], _NT))
        h = h_ref[...]
        r = lax.rsqrt(jnp.mean(h * h, axis=-1, keepdims=True) + RMS_EPS)
        xh = h * r
        dg = jnp.sum(dn * xh, axis=0, keepdims=True)
        dxh = dn * g_ref[...]
        dh0 = dh1_ref[...] + r * (dxh - xh * jnp.mean(dxh * xh, axis=-1, keepdims=True))
        dh0_ref[...] = dh0
        i = pl.program_id(0)

        @pl.when(i == 0)
        def _():
            part_ref[...] = jnp.zeros_like(part_ref)
            dmeta_ref[...] = jnp.zeros_like(dmeta_ref)

        part_ref[0:1, :] += dg

        @pl.when(i % tiles_per_example == 0)
        def _():
            dmeta_ref[...] += dh0[ZROWS:LEAD, :]

    return pl.pallas_call(
        body, name="bwd_inproj", grid=(rows // tm,),
        in_specs=[_rows(tm, 1024), _rows(tm, 512), _rows(tm, 1024), _rows(tm, RANK_P), _rows(tm, D), _rows(tm, D),
                  _whole_vmem(), _fixed((1, D))],
        out_specs=[_rows(tm, D), _fixed((8, D)), _fixed((N_META, D))],
        out_shape=[jax.ShapeDtypeStruct((rows, D), F32), jax.ShapeDtypeStruct((8, D), F32),
                   jax.ShapeDtypeStruct((N_META, D), F32)],
        compiler_params=_params(("arbitrary",)),
    )(duc, dqk, dvg, dlr, dh1, h0, w_in, g1)


def _dw_blocked(a, bs, width, name):
    rows, m = a.shape
    ws = [b.shape[1] for b in bs]
    assert sum(ws) >= N_DEV * width
    tk = _row_tile(rows, DW_ROW_TILE)
    nk = rows // tk

    def body(a_ref, *refs):
        b_refs, o_ref, acc_ref = refs[:len(bs)], refs[len(bs)], refs[len(bs) + 1]
        k = pl.program_id(0)

        @pl.when(k == 0)
        def _():
            acc_ref[...] = jnp.zeros_like(acc_ref)

        at = a_ref[...].T
        off = 0
        for b_ref, w in zip(b_refs, ws):
            acc_ref[:, off:off + w] += _dot(at, b_ref[...])
            off += w

        @pl.when(k == nk - 1)
        def _():
            for d in range(N_DEV):
                o_ref[d] = acc_ref[:, d * width:(d + 1) * width].astype(BF16)

    return pl.pallas_call(
        body, name=name, grid=(nk,),
        in_specs=[_rows(tk, m)] + [_rows(tk, w) for w in ws],
        out_specs=_fixed((N_DEV, m, width)),
        out_shape=jax.ShapeDtypeStruct((N_DEV, m, width), BF16),
        scratch_shapes=[pltpu.VMEM((m, sum(ws)), F32)],
        compiler_params=_params(("arbitrary",)),
    )(a, *bs)


def _dw_out(yc, yg, dh1b):
    rows = yc.shape[0]
    tk = _row_tile(rows, DW_ROW_TILE)
    nk = rows // tk

    def body(yc_ref, yg_ref, d_ref, o_ref, acc_ref):
        k = pl.program_id(0)

        @pl.when(k == 0)
        def _():
            acc_ref[...] = jnp.zeros_like(acc_ref)

        d = d_ref[...]
        acc_ref[0:C_CONV, :] += _dot(yc_ref[...], d, _TN)
        acc_ref[C_CONV:D, :] += _dot(yg_ref[...], d, _TN)

        @pl.when(k == nk - 1)
        def _():
            o_ref[...] = acc_ref[...].astype(BF16)

    return pl.pallas_call(
        body, name="dw_out", grid=(nk,),
        in_specs=[_rows(tk, C_CONV), _rows(tk, GLA_V), _rows(tk, D)],
        out_specs=_fixed((D, D)), out_shape=jax.ShapeDtypeStruct((D, D), BF16),
        scratch_shapes=[pltpu.VMEM((D, D), F32)],
        compiler_params=_params(("arbitrary",)),
    )(yc, yg, dh1b)


def _matmul_tn(a, b, name):
    rows, m = a.shape
    n = b.shape[1]
    tk = _row_tile(rows, DW_ROW_TILE)
    tn = n if n <= 1024 else FF_CHUNK
    tm_ = m if m <= 1024 else FF_CHUNK
    assert n % tn == 0 and m % tm_ == 0
    nk = rows // tk

    def body(a_ref, b_ref, o_ref, acc_ref):
        k = pl.program_id(2)

        @pl.when(k == 0)
        def _():
            acc_ref[...] = jnp.zeros_like(acc_ref)

        acc_ref[...] += _dot(a_ref[...], b_ref[...], _TN)

        @pl.when(k == nk - 1)
        def _():
            o_ref[...] = acc_ref[...].astype(BF16)

    return pl.pallas_call(
        body, name=name, grid=(m // tm_, n // tn, nk),
        in_specs=[pl.BlockSpec((tk, tm_), lambda i, j, k: (k, i)), pl.BlockSpec((tk, tn), lambda i, j, k: (k, j))],
        out_specs=pl.BlockSpec((tm_, tn), lambda i, j, k: (i, j)),
        out_shape=jax.ShapeDtypeStruct((m, n), BF16),
        scratch_shapes=[pltpu.VMEM((tm_, tn), F32)],
        compiler_params=_params(("parallel", "parallel", "arbitrary")),
    )(a, b)


HALO = 32
LANES = 128


def _shifted(win, offsets):
    for r in range(8):
        js = [j for j, k in enumerate(offsets) if k % 8 == r]
        if js:
            rolled = win if r == 0 else pltpu.roll(win, CHUNK + HALO - r, 0)
            for j in js:
                yield j, rolled[offsets[j] - r:offsets[j] - r + CHUNK]


def _glu_into(uc_ref, vs_ref, n_chunk):
    vs_ref[0:CHUNK, :] = jnp.zeros((CHUNK, C_CONV), F32)

    def glu(i, carry):
        base = pl.multiple_of(i * CHUNK, CHUNK)
        val = uc_ref[pl.ds(base, CHUNK), 0:C_CONV]
        gate = uc_ref[pl.ds(base, CHUNK), C_CONV:2 * C_CONV]
        vs_ref[pl.ds(base + CHUNK, CHUNK), :] = val * _sigmoid(gate)
        return carry

    lax.fori_loop(0, n_chunk, glu, 0)


def _fwd_conv(uc, conv_w, conv_b, ln_g, ln_b, n_ex):
    rows = uc.shape[0]
    lp = rows // n_ex
    n_chunk = lp // CHUNK

    def body(uc_ref, w_ref, b_ref, lg_ref, lb_ref, ypre_ref, yc_ref, vs_ref):
        _glu_into(uc_ref, vs_ref, n_chunk)

        def conv(i, carry):
            base = pl.multiple_of(i * CHUNK, CHUNK)
            for lb in range(C_CONV // LANES):
                ls = slice(lb * LANES, (lb + 1) * LANES)
                win = vs_ref[pl.ds(base + CHUNK - HALO, CHUNK + HALO), ls]
                acc = jnp.broadcast_to(b_ref[:, ls], (CHUNK, LANES))
                for j, rows_j in _shifted(win, [HALO - (CONV_W - 1) + j for j in range(CONV_W)]):
                    acc = acc + w_ref[j:j + 1, ls] * rows_j
                ypre_ref[pl.ds(base, CHUNK), ls] = acc
            y = ypre_ref[pl.ds(base, CHUNK), :]
            mu = jnp.mean(y, axis=-1, keepdims=True)
            yc_ = y - mu
            rstd = lax.rsqrt(jnp.mean(yc_ * yc_, axis=-1, keepdims=True) + LN_EPS)
            s = yc_ * rstd * lg_ref[...] + lb_ref[...]
            yc_ref[pl.ds(base, CHUNK), :] = (s * _sigmoid(s)).astype(BF16)
            return carry

        lax.fori_loop(0, n_chunk, conv, 0)

    ex = lambda w: pl.BlockSpec((lp, w), lambda b: (b, 0))
    return pl.pallas_call(
        body, name="fwd_conv", grid=(n_ex,),
        in_specs=[ex(2 * C_CONV), _fixed((32, C_CONV)), _fixed((1, C_CONV)), _fixed((1, C_CONV)), _fixed((1, C_CONV))],
        out_specs=[ex(C_CONV), ex(C_CONV)],
        out_shape=[jax.ShapeDtypeStruct((rows, C_CONV), F32), jax.ShapeDtypeStruct((rows, C_CONV), BF16)],
        scratch_shapes=[pltpu.VMEM((lp + CHUNK, C_CONV), F32)],
        compiler_params=_params(("parallel",)),
    )(uc, conv_w, conv_b, ln_g, ln_b)


def _bwd_conv(uc, ypre, dyc, conv_w, ln_g, ln_b, token, n_ex):
    rows = uc.shape[0]
    lp = rows // n_ex
    n_chunk = lp // CHUNK

    def body(uc_ref, ypre_ref, dyc_ref, w_ref, lg_ref, lb_ref, token_ref, duc_ref, dw_ref, dvec_ref, vs_ref, dys_ref,
             dwacc_ref):
        _glu_into(uc_ref, vs_ref, n_chunk)
        dys_ref[pl.ds(lp, CHUNK), :] = jnp.zeros((CHUNK, C_CONV), F32)
        dwacc_ref[...] = jnp.zeros_like(dwacc_ref)

        def ln_bwd(i, carry):
            dcb, dlg, dlb = carry
            base = pl.multiple_of(i * CHUNK, CHUNK)
            y = ypre_ref[pl.ds(base, CHUNK), :]
            mu = jnp.mean(y, axis=-1, keepdims=True)
            yc_ = y - mu
            rstd = lax.rsqrt(jnp.mean(yc_ * yc_, axis=-1, keepdims=True) + LN_EPS)
            xh = yc_ * rstd
            s = xh * lg_ref[...] + lb_ref[...]
            sg = _sigmoid(s)
            ds = dyc_ref[pl.ds(base, CHUNK), :] * (sg * (1.0 + s * (1.0 - sg)))
            dxh = ds * lg_ref[...]
            dy = rstd * (dxh - jnp.mean(dxh, axis=-1, keepdims=True) - xh * jnp.mean(dxh * xh, axis=-1, keepdims=True))
            dys_ref[pl.ds(base, CHUNK), :] = dy
            return (dcb + jnp.sum(dy, axis=0, keepdims=True), dlg + jnp.sum(ds * xh, axis=0, keepdims=True),
                    dlb + jnp.sum(ds, axis=0, keepdims=True))

        zero = jnp.zeros((1, C_CONV), F32)
        dcb, dlg, dlb = lax.fori_loop(0, n_chunk, ln_bwd, (zero, zero, zero))

        @pl.when(pl.program_id(0) == 0)
        def _():
            dvec_ref[...] = jnp.zeros_like(dvec_ref)
            dw_ref[...] = jnp.zeros_like(dw_ref)

        dvec_ref[0:1, :] += dcb
        dvec_ref[1:2, :] += dlg
        dvec_ref[2:3, :] += dlb

        def taps(i, carry):
            base = pl.multiple_of(i * CHUNK, CHUNK)
            for lb in range(C_CONV // LANES):
                ls = slice(lb * LANES, (lb + 1) * LANES)
                dwin = dys_ref[pl.ds(base, CHUNK + HALO), ls]
                vwin = vs_ref[pl.ds(base + CHUNK - HALO, CHUNK + HALO), ls]
                dy = dwin[0:CHUNK]
                acc = jnp.zeros((CHUNK, LANES), F32)
                for j, rows_j in _shifted(dwin, [CONV_W - 1 - j for j in range(CONV_W)]):
                    acc = acc + w_ref[j:j + 1, ls] * rows_j
                for j, rows_j in _shifted(vwin, [HALO - (CONV_W - 1) + j for j in range(CONV_W)]):
                    dwacc_ref[8 * j:8 * j + 8, ls] += jnp.sum((dy * rows_j).reshape(CHUNK // 8, 8, LANES), axis=0)
                val = uc_ref[pl.ds(base, CHUNK), ls]
                gate = uc_ref[pl.ds(base, CHUNK), C_CONV + lb * LANES:C_CONV + (lb + 1) * LANES]
                sg = _sigmoid(gate)
                duc_ref[pl.ds(base, CHUNK), ls] = (acc * sg).astype(BF16)
                duc_ref[pl.ds(base, CHUNK), C_CONV + lb * LANES:C_CONV + (lb + 1) * LANES] = (
                    acc * val * sg * (1.0 - sg)).astype(BF16)
            return carry

        lax.fori_loop(0, n_chunk, taps, 0)
        for j in range(CONV_W):
            dw_ref[j:j + 1, :] += jnp.sum(dwacc_ref[8 * j:8 * j + 8, :], axis=0, keepdims=True)

    ex = lambda w: pl.BlockSpec((lp, w), lambda b: (b, 0))
    return pl.pallas_call(
        body, name="bwd_conv", grid=(n_ex,),
        in_specs=[ex(2 * C_CONV), ex(C_CONV), ex(C_CONV), _fixed((32, C_CONV)), _fixed((1, C_CONV)), _fixed((1, C_CONV)),
                  _fixed((8, 128))],
        out_specs=[ex(2 * C_CONV), _fixed((32, C_CONV)), _fixed((8, C_CONV))],
        out_shape=[jax.ShapeDtypeStruct((rows, 2 * C_CONV), BF16), jax.ShapeDtypeStruct((32, C_CONV), F32),
                   jax.ShapeDtypeStruct((8, C_CONV), F32)],
        scratch_shapes=[pltpu.VMEM((lp + CHUNK, C_CONV), F32), pltpu.VMEM((lp + CHUNK, C_CONV), F32),
                        pltpu.VMEM((8 * 32, C_CONV), F32)],
        compiler_params=_params(("arbitrary",)),
    )(uc, ypre, dyc, conv_w, ln_g, ln_b, token)


def _seg_chunks(n_chunk):
    return max(c for c in (11, 3, 1) if n_chunk % c == 0)


def _block_mask(shape, row_block, lane_block):
    return (lax.broadcasted_iota(jnp.int32, shape, 0) // row_block) == (lax.broadcasted_iota(jnp.int32, shape, 1) // lane_block)


def _per_head_rows(x, mask):
    return jnp.where(mask, jnp.concatenate([x] * GLA_H, axis=0), 0)


def _fold_heads(full, lane_block):
    lane = lax.broadcasted_iota(jnp.int32, (1, full.shape[1]), 1) // lane_block
    out = jnp.where(lane == 0, full[0:CHUNK], 0.0)
    for h in range(1, GLA_H):
        out = out + jnp.where(lane == h, full[h * CHUNK:(h + 1) * CHUNK], 0.0)
    return out


def _causal_heads():
    return (lax.broadcasted_iota(jnp.int32, (CHUNK, GLA_H * CHUNK), 1) % CHUNK) <= lax.broadcasted_iota(
        jnp.int32, (CHUNK, GLA_H * CHUNK), 0)


def _cumsum_rows(x):
    row = lax.broadcasted_iota(jnp.int32, x.shape, 0)
    s = 1
    while s < CHUNK:
        x = x + jnp.where(row >= s, pltpu.roll(x, s, 0), 0.0)
        s *= 2
    return x


def _rev_cumsum_rows(x):
    row = lax.broadcasted_iota(jnp.int32, x.shape, 0)
    s = 1
    while s < CHUNK:
        x = x + jnp.where(row < CHUNK - s, pltpu.roll(x, CHUNK - s, 0), 0.0)
        s *= 2
    return x


def _gate_terms(lr_ref, w2_ref, gb_ref, rs, first_pos):
    z = _dot(lr_ref[rs, :].astype(BF16), w2_ref[...]) + gb_ref[...]
    la = (jnp.minimum(z, 0.0) - jnp.log(1.0 + jnp.exp(-jnp.abs(z)))) * (1.0 / TAU)
    pos = first_pos + lax.broadcasted_iota(jnp.int32, (CHUNK, 1), 0)
    live = pos >= ZROWS
    la = jnp.where(live, la, 0.0)
    return z, live, _cumsum_rows(la)


def _fwd_gla(qk, vg, lr, w2p, gb, ng, n_ex):
    rows = qk.shape[0]
    lp = rows // n_ex
    n_chunk = lp // CHUNK
    sc = _seg_chunks(n_chunk)
    n_seg = n_chunk // sc
    seg = sc * CHUNK

    def body(qk_ref, vg_ref, lr_ref, w2_ref, gb_ref, ng_ref, yg_ref, o_ref, st_ref, state_ref):
        sidx = pl.program_id(1)

        @pl.when(sidx == 0)
        def _():
            state_ref[...] = jnp.zeros_like(state_ref)

        causal = _causal_heads()
        k_mask = _block_mask((GLA_H * CHUNK, GLA_K), CHUNK, GLA_DK)
        v_mask = _block_mask((GLA_H * CHUNK, GLA_V), CHUNK, GLA_DV)
        s_mask = _block_mask((GLA_V, GLA_K), GLA_DV, GLA_DK)

        def chunk(ci, carry):
            base = pl.multiple_of(ci * CHUNK, CHUNK)
            rs = pl.ds(base, CHUNK)
            _, _, bcum = _gate_terms(lr_ref, w2_ref, gb_ref, rs, (sidx * sc + ci) * CHUNK)
            bl = bcum[CHUNK - 1:CHUNK, :]
            q = qk_ref[rs, 0:GLA_K]
            k = qk_ref[rs, GLA_K:2 * GLA_K]
            qt = (q * (GLA_DK ** -0.5) * jnp.exp(bcum)).astype(BF16)
            kt = (k * jnp.exp(-bcum)).astype(BF16)
            kh = (k * jnp.exp(bl - bcum)).astype(BF16)
            vb = vg_ref[rs, 0:GLA_V].astype(BF16)
            state = state_ref[...]
            st_ref[ci] = state
            a = jnp.where(causal, _dot(qt, _per_head_rows(kt, k_mask), _NT), 0.0)
            o = _dot(a.astype(BF16), _per_head_rows(vb, v_mask)) + _dot(qt, state.astype(BF16), _NT)
            o_ref[rs, :] = o
            for h in range(GLA_H):
                hs = slice(h * GLA_DV, (h + 1) * GLA_DV)
                oh = o[:, hs]
                ro = lax.rsqrt(jnp.mean(oh * oh, axis=-1, keepdims=True) + RMS_EPS)
                g = vg_ref[rs, GLA_V + h * GLA_DV:GLA_V + (h + 1) * GLA_DV]
                yg_ref[rs, hs] = (oh * ro * ng_ref[...] * g * _sigmoid(g)).astype(BF16)
            state_ref[...] = state * jnp.exp(bl) + jnp.where(s_mask, _dot(vb, kh, _TN), 0.0)
            return carry

        lax.fori_loop(0, sc, chunk, 0)

    sg = lambda w: pl.BlockSpec((seg, w), lambda b, s: (b * n_seg + s, 0))
    return pl.pallas_call(
        body, name="fwd_gla", grid=(n_ex, n_seg),
        in_specs=[sg(2 * GLA_K), sg(2 * GLA_V), sg(RANK_P), _fixed((RANK_P, GLA_K)), _fixed((1, GLA_K)), _fixed((1, GLA_DV))],
        out_specs=[sg(GLA_V), sg(GLA_V), pl.BlockSpec((sc, GLA_V, GLA_K), lambda b, s: (b * n_seg + s, 0, 0))],
        out_shape=[jax.ShapeDtypeStruct((rows, GLA_V), BF16), jax.ShapeDtypeStruct((rows, GLA_V), F32),
                   jax.ShapeDtypeStruct((n_ex * n_chunk, GLA_V, GLA_K), F32)],
        scratch_shapes=[pltpu.VMEM((GLA_V, GLA_K), F32)],
        compiler_params=_params(("parallel", "arbitrary")),
    )(qk, vg, lr, w2p, gb, ng)


def _bwd_gla(qk, vg, lr, o, st, dyg, w2p, gb, ng, n_ex):
    rows = qk.shape[0]
    lp = rows // n_ex
    n_chunk = lp // CHUNK
    sc = _seg_chunks(n_chunk)
    n_seg = n_chunk // sc
    seg = sc * CHUNK

    def body(qk_ref, vg_ref, lr_ref, o_ref, st_ref, dyg_ref, w2_ref, gb_ref, ng_ref,
             dqk_ref, dvg_ref, dlr_ref, dw2_ref, dvec_ref, gt_ref, dz_ref):
        step = pl.program_id(1)
        sidx = n_seg - 1 - step

        @pl.when(step == 0)
        def _():
            gt_ref[...] = jnp.zeros_like(gt_ref)

        @pl.when((step == 0) & (pl.program_id(0) == 0))
        def _():
            dw2_ref[...] = jnp.zeros_like(dw2_ref)
            dvec_ref[...] = jnp.zeros_like(dvec_ref)

        causal = _causal_heads()
        k_mask = _block_mask((GLA_H * CHUNK, GLA_K), CHUNK, GLA_DK)
        v_mask = _block_mask((GLA_H * CHUNK, GLA_V), CHUNK, GLA_DV)
        s_mask = _block_mask((GLA_V, GLA_K), GLA_DV, GLA_DK)
        last_row = lax.broadcasted_iota(jnp.int32, (CHUNK, 1), 0) == CHUNK - 1
        ng = ng_ref[...]

        def chunk(ii, dng):
            ci = sc - 1 - ii
            base = pl.multiple_of(ci * CHUNK, CHUNK)
            rs = pl.ds(base, CHUNK)
            z, live, bcum = _gate_terms(lr_ref, w2_ref, gb_ref, rs, (sidx * sc + ci) * CHUNK)
            bl = bcum[CHUNK - 1:CHUNK, :]
            ebl = jnp.exp(bl)
            q = qk_ref[rs, 0:GLA_K]
            k = qk_ref[rs, GLA_K:2 * GLA_K]
            eb = jnp.exp(bcum)
            enb = jnp.exp(-bcum)
            ehb = jnp.exp(bl - bcum)
            qt = q * (GLA_DK ** -0.5) * eb
            kt = k * enb
            kh = k * ehb
            qtb = qt.astype(BF16)
            vb = vg_ref[rs, 0:GLA_V].astype(BF16)
            k_rows = _per_head_rows(kt.astype(BF16), k_mask)
            v_rows = _per_head_rows(vb, v_mask)
            gt = gt_ref[...]
            gtb = gt.astype(BF16)
            s_in = st_ref[ci]
            dos = []
            for h in range(GLA_H):
                hs = slice(h * GLA_DV, (h + 1) * GLA_DV)
                gs = slice(GLA_V + h * GLA_DV, GLA_V + (h + 1) * GLA_DV)
                oh = o_ref[rs, hs]
                ro = lax.rsqrt(jnp.mean(oh * oh, axis=-1, keepdims=True) + RMS_EPS)
                on = oh * ro
                g = vg_ref[rs, gs]
                sg = _sigmoid(g)
                dout = dyg_ref[rs, hs]
                dvg_ref[rs, gs] = (dout * on * ng * (sg * (1.0 + g * (1.0 - sg)))).astype(BF16)
                dw = dout * g * sg
                dng = dng + jnp.sum(dw * on, axis=0, keepdims=True)
                don = dw * ng
                dos.append((ro * (don - on * jnp.mean(don * on, axis=-1, keepdims=True))).astype(BF16))
            dob = jnp.concatenate(dos, axis=1)
            a = jnp.where(causal, _dot(qtb, k_rows, _NT), 0.0).astype(BF16)
            da = jnp.where(causal, _dot(dob, v_rows, _NT), 0.0).astype(BF16)
            dv = _fold_heads(_dot(a, dob, _TN), GLA_DV) + _dot(kh.astype(BF16), gtb, _NT)
            dvg_ref[rs, 0:GLA_V] = dv.astype(BF16)
            dkh = _dot(vb, gtb)
            dqt = _dot(da, k_rows) + _dot(dob, s_in.astype(BF16))
            dkt = _fold_heads(_dot(da, qtb, _TN), GLA_DK)
            dbl = jnp.sum(gt * s_in, axis=0, keepdims=True) * ebl + jnp.sum(dkh * kh, axis=0, keepdims=True)
            dqk_ref[rs, 0:GLA_K] = (dqt * (GLA_DK ** -0.5) * eb).astype(BF16)
            dqk_ref[rs, GLA_K:2 * GLA_K] = (dkt * enb + dkh * ehb).astype(BF16)
            db = dqt * qt - dkt * kt - dkh * kh
            db = jnp.where(last_row, db + dbl, db)
            dla = jnp.where(live, _rev_cumsum_rows(db), 0.0)
            dz_ref[rs, :] = dla * (1.0 / TAU) * (1.0 - _sigmoid(z))
            gt_ref[...] = jnp.where(s_mask, _dot(dob, qtb, _TN), 0.0) + gt * ebl
            return dng

        dng = lax.fori_loop(0, sc, chunk, jnp.zeros((1, GLA_DV), F32))
        dz = dz_ref[...]
        dzb = dz.astype(BF16)
        dlr_ref[...] = _dot(dzb, w2_ref[...], _NT).astype(BF16)
        dw2_ref[...] += _dot(lr_ref[...].astype(BF16), dzb, _TN)
        dvec_ref[0:1, :] += jnp.sum(dz, axis=0, keepdims=True)
        dvec_ref[1:2, 0:GLA_DV] += dng

    sg_ = lambda w: pl.BlockSpec((seg, w), lambda b, s: (b * n_seg + n_seg - 1 - s, 0))
    return pl.pallas_call(
        body, name="bwd_gla", grid=(n_ex, n_seg),
        in_specs=[sg_(2 * GLA_K), sg_(2 * GLA_V), sg_(RANK_P), sg_(GLA_V),
                  pl.BlockSpec((sc, GLA_V, GLA_K), lambda b, s: (b * n_seg + n_seg - 1 - s, 0, 0)), sg_(GLA_V),
                  _fixed((RANK_P, GLA_K)), _fixed((1, GLA_K)), _fixed((1, GLA_DV))],
        out_specs=[sg_(2 * GLA_K), sg_(2 * GLA_V), sg_(RANK_P), _fixed((RANK_P, GLA_K)), _fixed((8, GLA_K))],
        out_shape=[jax.ShapeDtypeStruct((rows, 2 * GLA_K), BF16), jax.ShapeDtypeStruct((rows, 2 * GLA_V), BF16),
                   jax.ShapeDtypeStruct((rows, RANK_P), BF16), jax.ShapeDtypeStruct((RANK_P, GLA_K), F32),
                   jax.ShapeDtypeStruct((8, GLA_K), F32)],
        scratch_shapes=[pltpu.VMEM((GLA_V, GLA_K), F32), pltpu.VMEM((seg, GLA_K), F32)],
        compiler_params=_params(("arbitrary", "arbitrary")),
    )(qk, vg, lr, o, st, dyg, w2p, gb, ng)


def _local_step(x, tgt, p, w_out, pass_on, late_weights, send_early):
    n_ex, seq, _ = x.shape
    lp = seq + LEAD
    rows = n_ex * lp
    meta = jnp.broadcast_to(p["meta"][None], (n_ex, N_META, D))
    h0 = jnp.concatenate([jnp.zeros((n_ex, ZROWS, D), F32), meta, x], axis=1).reshape(rows, D)
    tgt_p = jnp.pad(tgt, ((0, 0), (LEAD, 0), (0, 0))).reshape(rows, D)

    uc, qk, vg, lr, n1 = _fwd_inproj(h0, p["g1"], p["w_in"])
    ypre, yc = _fwd_conv(uc, p["conv_w"], p["conv_b"], p["ln_g"], p["ln_b"], n_ex)
    yg, o, st = _fwd_gla(qk, vg, lr, p["w2"], p["gb"], p["ng"], n_ex)
    h1, n2 = _fwd_outproj(yc, yg, h0, w_out, p["g2"], pass_on(yg))
    wg, wu, wd = late_weights(n2)
    f, da, db, dh2, dh1, dh1b, part = _ffn_rows(h1, n2, tgt_p, wg, wu, wd, p["g2"], p["g3"], lp)
    g = {}
    token = send_early("ffn", [_matmul_tn(a_, b_, name).reshape(N_DEV, FF_S, D) for a_, b_, name in (
        (da, n2, "dw_gate"), (db, n2, "dw_up"), (f, dh2, "dw_down"))])
    dyc, dyg = _bwd_outproj(dh1b, w_out, token)
    token = send_early("out", [_dw_out(yc, yg, dh1b).reshape(N_DEV, W_OUT_S, D)])
    duc, g["conv_w"], g["conv_vec"] = _bwd_conv(uc, ypre, dyc, p["conv_w"], p["ln_g"], p["ln_b"], token, n_ex)
    dqk, dvg, dlr, g["w2"], g["gla_vec"] = _bwd_gla(qk, vg, lr, o, st, dyg, p["w2"], p["gb"], p["ng"], n_ex)
    dh0, g["in_vec"], g["meta"] = _bwd_inproj(duc, dqk, dvg, dlr, dh1, h0, p["w_in"], p["g1"], lp)
    g["w_in"] = _dw_blocked(n1, [duc, dqk, dvg, dlr], W_IN_S, "dw_in")
    g["ffn_vec"] = part
    return dh0.reshape(n_ex, lp, D)[:, LEAD:], g


W_IN_S = D_IN // N_DEV
W_OUT_S = D // N_DEV
FF_S = D_FF // N_DEV
CONV_S = C_CONV // N_DEV
GATE_S = GLA_K // N_DEV
SMALL_PACK = 64
CONV_ROW = 16
GATE_ROW = 48
VEC_ROWS = 16
_VEC_ROWS = (("norm_mix_g", D), ("conv_b", C_CONV), ("conv_ln_g", C_CONV), ("conv_ln_b", C_CONV), ("gla_gate_b", GLA_K),
             ("gla_norm_g", GLA_DV), ("norm_ffn_g", D), ("norm_final_g", D))
LOSS_ROW = len(_VEC_ROWS)


def _position():
    return lax.axis_index("x"), lax.axis_index("y"), lax.axis_index("c")


def _any():
    return pl.BlockSpec(memory_space=pl.ANY)


def _all_gather(mats, meta, conv_w, w2, n_now):
    n_mat = len(mats)
    n_t = n_mat + 1
    n_later = n_mat - n_now
    now = list(range(n_now)) + [n_mat]

    def body(*refs):
        ins = refs[0:n_mat]
        meta_ref, cw_ref, w2_ref = refs[n_mat:n_mat + 3]
        outs = refs[n_mat + 3:n_mat + 3 + n_t]
        later = refs[n_mat + 3 + n_t:n_mat + 3 + n_t + n_later]
        stage_now = refs[n_mat + 3 + n_t + n_later:n_mat + 4 + n_t + n_later + n_now]
        send_sems, recv_sems, local_sems = refs[n_mat + 4 + n_t + n_later + n_now:]
        stage = list(stage_now[0:n_now]) + list(later) + [stage_now[n_now]]
        for s_ref, w_ref in zip(stage, ins):
            s_ref[...] = w_ref[...].astype(BF16)
        sp = stage[n_mat]
        sp[...] = jnp.zeros_like(sp)
        sp[0:N_META, :] = meta_ref[...]
        sp[CONV_ROW:CONV_ROW + CONV_W, 0:CONV_S] = cw_ref[...]
        sp[GATE_ROW:GATE_ROW + RANK, 0:GATE_S] = w2_ref[...]

        x, y, c = _position()
        me, sibling = (x, y, c), (x, y, 1 - c)
        chips = [(1 - x, y), (x, 1 - y), (1 - x, 1 - y)]

        def blk(t, p):
            return outs[t].at[4 * p[0] + 2 * p[1] + p[2]]

        def copy(t, k, block, to, staged=False):
            return pltpu.make_async_remote_copy(
                src_ref=stage[t] if staged else blk(t, block), dst_ref=blk(t, block),
                send_sem=send_sems.at[t, k], recv_sem=recv_sems.at[t, k], device_id=to, device_id_type=MESH)

        mine = [pltpu.make_async_copy(stage[t], blk(t, me), local_sems.at[t]) for t in range(n_t)]
        for cp in mine:
            cp.start()
        first = []
        for t in now:
            first.append(copy(t, 0, me, sibling, staged=True))
            first += [copy(t, 1 + j, me, (*chip, c), staged=True) for j, chip in enumerate(chips)]
        for cp in first:
            cp.start()
        passed = []
        for t in now:
            for j, chip in enumerate(chips):
                copy(t, 1 + j, (*chip, c), me).wait_recv()
                passed.append(copy(t, 4 + j, (*chip, c), sibling))
                passed[-1].start()
        for t in now:
            copy(t, 0, sibling, me).wait_recv()
            for j, chip in enumerate(chips):
                copy(t, 4 + j, (*chip, 1 - c), me).wait_recv()
        for cp in first + passed:
            cp.wait_send()
        for cp in mine:
            cp.wait()

    shapes = [(N_DEV,) + m.shape for m in mats]
    res = pl.pallas_call(
        body, name="all_gather",
        out_shape=[jax.ShapeDtypeStruct(s, BF16) for s in shapes] + [jax.ShapeDtypeStruct((N_DEV, SMALL_PACK, 128), F32)]
        + [jax.ShapeDtypeStruct(m.shape, BF16) for m in mats[n_now:]],
        in_specs=[_whole_vmem()] * (n_mat + 3), out_specs=[_any()] * n_t + [_whole_vmem()] * n_later,
        scratch_shapes=[pltpu.VMEM(m.shape, BF16) for m in mats[:n_now]] + [pltpu.VMEM((SMALL_PACK, 128), F32)]
        + [pltpu.SemaphoreType.DMA((n_t, 7)), pltpu.SemaphoreType.DMA((n_t, 7)), pltpu.SemaphoreType.DMA((n_t,))],
        compiler_params=pltpu.CompilerParams(vmem_limit_bytes=VMEM_LIMIT),
    )(*mats, meta, conv_w, w2)
    return res[0:n_mat], res[n_mat], res[n_t:]


_HBM = pl.BlockSpec(memory_space=pltpu.HBM)
_SEM = pl.BlockSpec(memory_space=pltpu.SEMAPHORE)
_EFFECT = pltpu.SideEffectType.DATAFLOW_SIDE_EFFECTING


_N_ROUTES = {"scatter": 7, "first": 4, "forward": 3}


def _routes(mode):
    x, y, c = _position()
    me = 4 * x + 2 * y + c
    if mode == "scatter":
        out = []
        for k in range(1, N_DEV):
            px = 1 - x if k & 4 else x
            py = 1 - y if k & 2 else y
            pc = 1 - c if k & 1 else c
            out.append(((px, py, pc), 4 * px + 2 * py + pc, me))
        return out
    if mode == "first":
        return [(pos, None, me) for pos in ((x, y, 1 - c), (1 - x, y, c), (x, 1 - y, c), (1 - x, 1 - y, c))]
    assert mode == "forward"
    return [((x, y, 1 - c), 4 * px + 2 * py + c, 4 * px + 2 * py + c) for px, py in ((1 - x, y), (x, 1 - y), (1 - x, 1 - y))]


def _route_copies(mode, n, src_refs, land_refs, send_sems, recv_sems):
    nr = _N_ROUTES[mode]
    for i, (pos, src_blk, dst_blk) in enumerate(_routes(mode)):
        for t in range(n):
            src = land_refs[t] if mode == "forward" else src_refs[t]
            yield pltpu.make_async_remote_copy(
                src_ref=src if src_blk is None else src.at[src_blk], dst_ref=land_refs[t].at[dst_blk],
                send_sem=send_sems.at[nr * t + i], recv_sem=recv_sems.at[nr * t + i], device_id=pos, device_id_type=MESH)


def _in_hbm(a):
    return pltpu.with_memory_space_constraint(a, pltpu.HBM)


def _send_start(name, srcs, lands, mode):
    n, ns = len(lands), len(srcs)
    nsem = _N_ROUTES[mode] * n

    def body(*refs):
        src_refs, land_refs = refs[0:ns], refs[ns:ns + n]
        send_sems, recv_sems = refs[ns + n:ns + n + 2]
        token = refs[2 * (ns + n) + 2]
        for cp in _route_copies(mode, n, src_refs, land_refs, send_sems, recv_sems):
            cp.start()
        token[...] = jnp.zeros_like(token)

    bufs = list(srcs) + list(lands)
    res = pl.pallas_call(
        body, name=name,
        out_shape=(pltpu.SemaphoreType.DMA((nsem,)), pltpu.SemaphoreType.DMA((nsem,)),
                   *[pltpu.HBM(b.shape, b.dtype) for b in bufs], jax.ShapeDtypeStruct((8, 128), F32)),
        in_specs=[_HBM] * len(bufs), out_specs=(_SEM, _SEM, *[_HBM] * len(bufs), _whole_vmem()),
        input_output_aliases={i: 2 + i for i in range(len(bufs))},
        compiler_params=pltpu.CompilerParams(has_side_effects=_EFFECT),
    )(*[_in_hbm(b) for b in bufs])
    return res[0], res[1], res[2:2 + ns], res[2 + ns:2 + ns + n], res[2 + ns + n]


def _send_wait(name, send_sems, recv_sems, srcs, lands, mode, after):
    n, ns = len(lands), len(srcs)

    def body(*refs):
        src_refs, land_refs = refs[0:ns], refs[ns:ns + n]
        send_sems, recv_sems = refs[ns + n:ns + n + 2]
        for cp in _route_copies(mode, n, src_refs, land_refs, send_sems, recv_sems):
            cp.wait_send()
            cp.wait_recv()

    bufs = list(srcs) + list(lands)
    res = pl.pallas_call(
        body, name=name,
        out_shape=tuple(pltpu.HBM(b.shape, b.dtype) for b in bufs),
        in_specs=[_HBM] * len(bufs) + [_SEM, _SEM, _any()], out_specs=tuple([_HBM] * len(bufs)),
        input_output_aliases={i: i for i in range(len(bufs))},
        compiler_params=pltpu.CompilerParams(has_side_effects=_EFFECT),
    )(*bufs, send_sems, recv_sems, after)
    return res[0:ns], res[ns:ns + n]


def _unshard_in(a_in, a_small, token):
    def body(a_ref, s_ref, token_ref, w_ref, meta_ref, cw_ref, w2_ref):
        w_ref[:, D_IN:D_INP] = jnp.zeros((D, D_INP - D_IN), BF16)
        w2_ref[...] = jnp.zeros_like(w2_ref)
        for d in range(N_DEV):
            w_ref[:, d * W_IN_S:(d + 1) * W_IN_S] = a_ref[d]
            meta_ref[:, d * 128:(d + 1) * 128] = s_ref[d, 0:N_META, :]
            cw_ref[:, d * CONV_S:(d + 1) * CONV_S] = s_ref[d, CONV_ROW:CONV_ROW + 32, 0:CONV_S]
            w2_ref[0:RANK, d * GATE_S:(d + 1) * GATE_S] = s_ref[d, GATE_ROW:GATE_ROW + RANK, 0:GATE_S].astype(BF16)

    return pl.pallas_call(
        body, name="unshard_in",
        out_shape=[jax.ShapeDtypeStruct((D, D_INP), BF16), jax.ShapeDtypeStruct((N_META, D), F32),
                   jax.ShapeDtypeStruct((32, C_CONV), F32), jax.ShapeDtypeStruct((RANK_P, GLA_K), BF16)],
        compiler_params=pltpu.CompilerParams(vmem_limit_bytes=VMEM_LIMIT),
    )(a_in, a_small, token)


def _pack_small(g):
    def body(meta_ref, cw_ref, w2_ref, in_vec, ffn_vec, conv_vec, gla_vec, sp, vp):
        sp[...] = jnp.zeros_like(sp)
        vp[...] = jnp.zeros_like(vp)
        for d in range(N_DEV):
            sp[d, 0:N_META, :] = meta_ref[:, d * 128:(d + 1) * 128]
            sp[d, CONV_ROW:CONV_ROW + 32, 0:CONV_S] = cw_ref[:, d * CONV_S:(d + 1) * CONV_S]
            sp[d, GATE_ROW:GATE_ROW + RANK, 0:GATE_S] = w2_ref[0:RANK, d * GATE_S:(d + 1) * GATE_S]
            vp[d, 0:1, :] = in_vec[0:1, :]
            vp[d, 1:4, 0:C_CONV] = conv_vec[0:3, :]
            vp[d, 4:5, 0:GLA_K] = gla_vec[0:1, :]
            vp[d, 5:6, 0:GLA_DV] = gla_vec[1:2, 0:GLA_DV]
            vp[d, 6:7, :] = ffn_vec[1:2, :]
            vp[d, 7:8, :] = ffn_vec[0:1, :]
            vp[d, LOSS_ROW:LOSS_ROW + 1, :] = ffn_vec[2:3, :]

    return pl.pallas_call(
        body, name="pack_small",
        out_shape=[jax.ShapeDtypeStruct((N_DEV, SMALL_PACK, 128), F32), jax.ShapeDtypeStruct((N_DEV, VEC_ROWS, D), F32)],
    )(g["meta"], g["conv_w"], g["w2"], g["in_vec"], g["ffn_vec"], g["conv_vec"], g["gla_vec"])


def _adamw(w, g, m, v):
    m = ADAM_B1 * m + (1.0 - ADAM_B1) * g
    v = ADAM_B2 * v + (1.0 - ADAM_B2) * (g * g)
    m_hat = m / (1.0 - ADAM_B1 ** ADAM_STEP)
    v_hat = v / (1.0 - ADAM_B2 ** ADAM_STEP)
    return -ADAM_LR * (m_hat / (jnp.sqrt(v_hat) + ADAM_EPS) + ADAM_WD * w), m, v


def _update_matrix(recv, own, me, w, m, v, name):
    _, r, c = recv.shape
    tr = _row_tile(r, 256)

    def body(me_ref, recv_ref, own_ref, w_ref, m_ref, v_ref, g_ref, d_ref, nm_ref, nv_ref):
        g = jnp.zeros((tr, c), F32)
        for s in range(N_DEV):
            g = g + jnp.where(me_ref[0] == s, own_ref[...], recv_ref[s]).astype(F32)
        g_ref[...] = g
        d_ref[...], nm_ref[...], nv_ref[...] = _adamw(w_ref[...], g, m_ref[...], v_ref[...])

    one = pl.BlockSpec((None, tr, c), lambda i, me_ref: (0, i, 0))
    return pl.pallas_call(
        body, name=name,
        grid_spec=pltpu.PrefetchScalarGridSpec(
            num_scalar_prefetch=1, grid=(r // tr,),
            in_specs=[pl.BlockSpec((N_DEV, tr, c), lambda i, me_ref: (0, i, 0)),
                      pl.BlockSpec((None, tr, c), lambda i, me_ref: (me_ref[0], i, 0)), one, one, one],
            out_specs=[one] * 4),
        out_shape=[jax.ShapeDtypeStruct((1, r, c), F32)] * 4,
        compiler_params=_params(("parallel",)),
    )(me, recv, own, w, m, v)


_SMALL = ("meta_tokens", "conv_w", "gla_w_gate2") + tuple(n for n, _ in _VEC_ROWS)


def _update_small(me, srecv, vrecv, sown, vown, w, m, v):
    n = len(_SMALL)

    def body(*refs):
        me_ref, s_ref, v_ref, so_ref, vo_ref = refs[0:5]
        w_refs, m_refs, v_refs = refs[5:5 + n], refs[5 + n:5 + 2 * n], refs[5 + 2 * n:5 + 3 * n]
        outs = refs[5 + 3 * n:]
        ssum = jnp.zeros((SMALL_PACK, 128), F32)
        vsum = jnp.zeros((VEC_ROWS, D), F32)
        for s in range(N_DEV):
            ssum = ssum + jnp.where(me_ref[0] == s, so_ref[s], s_ref[s])
            vsum = vsum + jnp.where(me_ref[0] == s, vo_ref[s], v_ref[s])
        grads = [ssum[0:N_META, :], ssum[CONV_ROW:CONV_ROW + CONV_W, 0:CONV_S], ssum[GATE_ROW:GATE_ROW + RANK, 0:GATE_S]]
        grads += [vsum[i:i + 1, 0:width] for i, (_, width) in enumerate(_VEC_ROWS)]
        for i, g in enumerate(grads):
            d, nm, nv = _adamw(w_refs[i][...], g, m_refs[i][...], v_refs[i][...])
            outs[i][...] = g
            outs[n + i][...] = d
            outs[2 * n + i][...] = nm
            outs[3 * n + i][...] = nv
        outs[4 * n][...] = vsum[LOSS_ROW:LOSS_ROW + 1, 0:128]

    shapes = [jax.ShapeDtypeStruct(t.shape, F32) for t in w]
    res = pl.pallas_call(
        body, name="update_small", out_shape=shapes * 4 + [jax.ShapeDtypeStruct((1, 128), F32)],
        in_specs=[pl.BlockSpec(memory_space=pltpu.SMEM)] + [_whole_vmem()] * (4 + 3 * n),
    )(me, srecv, vrecv, sown, vown, *w, *m, *v)
    return res[0:n], res[n:2 * n], res[2 * n:3 * n], res[3 * n:4 * n], res[4 * n]


_WEIGHTS = ("meta_tokens", "norm_mix_g", "w_in", "conv_w", "conv_b", "conv_ln_g", "conv_ln_b", "gla_w_gate2", "gla_gate_b",
            "gla_norm_g", "w_out", "norm_ffn_g", "w_ffn_gate", "w_ffn_up", "w_ffn_down", "norm_final_g")
_MATRICES = ("w_in", "w_out", "w_ffn_gate", "w_ffn_up", "w_ffn_down")
_TRANSPOSED = ("w_ffn_gate", "w_ffn_up")


def kernel(x, meta_tokens, norm_mix_g, w_in, conv_w, conv_b, conv_ln_g, conv_ln_b, gla_w_gate2, gla_gate_b, gla_norm_g, w_out, norm_ffn_g, w_ffn_gate, w_ffn_up, w_ffn_down, norm_final_g, loss_target, m_meta_tokens, m_norm_mix_g, m_w_in, m_conv_w, m_conv_b, m_conv_ln_g, m_conv_ln_b, m_gla_w_gate2, m_gla_gate_b, m_gla_norm_g, m_w_out, m_norm_ffn_g, m_w_ffn_gate, m_w_ffn_up, m_w_ffn_down, m_norm_final_g, v_meta_tokens, v_norm_mix_g, v_w_in, v_conv_w, v_conv_b, v_conv_ln_g, v_conv_ln_b, v_gla_w_gate2, v_gla_gate_b, v_gla_norm_g, v_w_out, v_norm_ffn_g, v_w_ffn_gate, v_w_ffn_up, v_w_ffn_down, v_norm_final_g):
    given = dict(locals())
    two_d = lambda a: a.reshape(1, -1) if a.ndim == 1 else a.reshape(a.shape[-2:])
    fams = [{n: given[pre + n] for n in _WEIGHTS} for pre in ("", "m_", "v_")]
    for f in fams:
        for n in _TRANSPOSED:
            f[n] = f[n].transpose(0, 2, 1)
    w = fams[0]

    bufs, a_small, shards = _all_gather(
        [two_d(w[n]) for n in _MATRICES], w["meta_tokens"], two_d(w["conv_w"]), two_d(w["gla_w_gate2"]), 2)
    gather = _send_start("gather_first_start", shards, bufs[2:], "first")
    w_in, meta, conv_taps, w2 = _unshard_in(bufs[0], a_small, gather[4])
    p = dict(meta=meta, conv_w=conv_taps, w2=w2, w_in=w_in, g1=norm_mix_g, conv_b=conv_b, ln_g=conv_ln_g, ln_b=conv_ln_b,
             gb=gla_gate_b, ng=gla_norm_g, g2=norm_ffn_g, g3=two_d(norm_final_g))
    passed = {}

    def pass_on(after):
        _, lands = _send_wait("gather_first_wait", *gather[0:4], "first", after)
        passed["sent"] = _send_start("gather_forward_start", [], lands, "forward")
        return passed["sent"][4]

    def late_weights(after):
        _, (a_g, a_u, a_d) = _send_wait("gather_forward_wait", *passed["sent"][0:4], "forward", after)
        return a_g.reshape(D_FF, D), a_u.reshape(D_FF, D), a_d.reshape(D_FF, D)

    sent = {}

    def send_early(tag, mats):
        lands = [_in_hbm(lax.empty(m_.shape, m_.dtype)) for m_ in mats]
        sent[tag] = _send_start("scatter_" + tag + "_start", mats, lands, "scatter")
        return sent[tag][4]

    grad_x, g = _local_step(x, loss_target, p, bufs[1].reshape(D, D), pass_on, late_weights, send_early)

    token = send_early("last", [g["w_in"], *_pack_small(g)])
    x_, y_, c_ = _position()
    me = (4 * x_ + 2 * y_ + c_).astype(jnp.int32).reshape(1)
    res = {}
    for tag, names in (("ffn", ("w_ffn_gate", "w_ffn_up", "w_ffn_down")), ("out", ("w_out",))):
        own, recv = _send_wait("scatter_" + tag + "_wait", *sent[tag][0:4], "scatter", token)
        for n, o_, r_ in zip(names, own, recv):
            res[n] = _update_matrix(r_, o_, me, *[f[n] for f in fams], "update_" + n)
    (o_in, sown, vown), (r_in, srecv, vrecv) = _send_wait(
        "scatter_last_wait", *sent["last"][0:4], "scatter", res["w_out"][1])
    res["w_in"] = _update_matrix(r_in, o_in, me, *[f["w_in"] for f in fams], "update_w_in")
    small = _update_small(me, srecv, vrecv, sown, vown, *[[two_d(f[n]) for n in _SMALL] for f in fams])
    for i, n in enumerate(_SMALL):
        res[n] = [fam[i].reshape(w[n].shape) for fam in small[0:4]]
    for n in _TRANSPOSED:
        res[n] = [t.transpose(0, 2, 1) for t in res[n]]
    outs = [small[4][0, 0], grad_x]
    for k in range(4):
        outs += [res[n][k] for n in _WEIGHTS]
    return tuple(outs)
```

```python
import functools

import jax
import jax.numpy as jnp
from jax import lax
from jax.experimental import pallas as pl
from jax.experimental.pallas import tpu as pltpu

F32 = jnp.float32
BF16 = jnp.bfloat16

D = 1024
N_META = 16
C_CONV = 512
CONV_W = 31
GLA_H = 4
GLA_DK = 64
GLA_DV = 128
GLA_K = GLA_H * GLA_DK
GLA_V = GLA_H * GLA_DV
RANK = 16
RANK_P = 128
TAU = 16.0
CHUNK = 64
LEAD = CHUNK
ZROWS = LEAD - N_META
D_IN = 2 * C_CONV + 2 * GLA_K + 2 * GLA_V + RANK
D_INP = D_IN - RANK + RANK_P
D_FF = 2816
FF_CHUNK = 1408
FF_SPLIT = (0, 1536, D_FF)
RMS_EPS = 1e-6
LN_EPS = 1e-5
N_DEV = 8

ADAM_LR = 0.001
ADAM_B1 = 0.9
ADAM_B2 = 0.999
ADAM_EPS = 1e-08
ADAM_WD = 0.01
ADAM_STEP = 10

VMEM_LIMIT = 60 * 1024 * 1024
ROW_TILE = 1056
FFN_ROW_TILE = 352
DW_ROW_TILE = 1408
MESH = pl.DeviceIdType.MESH

_NN = (((1,), (0,)), ((), ()))
_NT = (((1,), (1,)), ((), ()))
_TN = (((0,), (0,)), ((), ()))


def _dot(a, b, dims=_NN):
    return lax.dot_general(a, b, dims, preferred_element_type=F32)


def _sigmoid(x):
    return 1.0 / (1.0 + jnp.exp(-x))


def _row_tile(rows, target):
    best = None
    for t in range(16, min(rows, target) + 1, 16):
        if rows % t == 0:
            best = t
    assert best is not None, rows
    return best


def _params(sem=None):
    return pltpu.CompilerParams(dimension_semantics=sem, vmem_limit_bytes=VMEM_LIMIT)


def _whole_vmem():
    return pl.BlockSpec(memory_space=pltpu.VMEM)


def _rows(tm, width):
    return pl.BlockSpec((tm, width), lambda i: (i, 0))


def _fixed(shape):
    return pl.BlockSpec(shape, lambda *_: (0,) * len(shape))


def _fwd_inproj(h0, g1, w_in):
    rows = h0.shape[0]
    tm = _row_tile(rows, ROW_TILE)

    def body(h_ref, g_ref, w_ref, uc_ref, qk_ref, vg_ref, lr_ref, n1_ref):
        h = h_ref[...]
        r = lax.rsqrt(jnp.mean(h * h, axis=-1, keepdims=True) + RMS_EPS)
        n = (h * r * g_ref[...]).astype(BF16)
        n1_ref[...] = n
        uc_ref[...] = _dot(n, w_ref[:, 0:1024])
        qk_ref[...] = _dot(n, w_ref[:, 1024:1536])
        vg_ref[...] = _dot(n, w_ref[:, 1536:2560])
        lr_ref[...] = _dot(n, w_ref[:, 2560:2688])

    return pl.pallas_call(
        body, name="fwd_inproj", grid=(rows // tm,),
        in_specs=[_rows(tm, D), _fixed((1, D)), _whole_vmem()],
        out_specs=[_rows(tm, 1024), _rows(tm, 512), _rows(tm, 1024), _rows(tm, RANK_P), _rows(tm, D)],
        out_shape=[jax.ShapeDtypeStruct((rows, 1024), F32), jax.ShapeDtypeStruct((rows, 512), F32),
                   jax.ShapeDtypeStruct((rows, 1024), F32), jax.ShapeDtypeStruct((rows, RANK_P), F32),
                   jax.ShapeDtypeStruct((rows, D), BF16)],
        compiler_params=_params(("parallel",)),
    )(h0, g1, w_in)


def _fwd_outproj(yc, yg, h0, w_out, g2, token):
    rows = h0.shape[0]
    tm = _row_tile(rows, ROW_TILE)

    def body(yc_ref, yg_ref, h_ref, w_ref, g_ref, token_ref, h1_ref, n2_ref):
        h1 = h_ref[...] + _dot(yc_ref[...], w_ref[0:C_CONV, :]) + _dot(yg_ref[...], w_ref[C_CONV:D, :])
        h1_ref[...] = h1
        r = lax.rsqrt(jnp.mean(h1 * h1, axis=-1, keepdims=True) + RMS_EPS)
        n2_ref[...] = (h1 * r * g_ref[...]).astype(BF16)

    return pl.pallas_call(
        body, name="fwd_outproj", grid=(rows // tm,),
        in_specs=[_rows(tm, C_CONV), _rows(tm, GLA_V), _rows(tm, D), _whole_vmem(), _fixed((1, D)), _fixed((8, 128))],
        out_specs=[_rows(tm, D), _rows(tm, D)],
        out_shape=[jax.ShapeDtypeStruct((rows, D), F32), jax.ShapeDtypeStruct((rows, D), BF16)],
        compiler_params=_params(("parallel",)),
    )(yc, yg, h0, w_out, g2, token)


def _ffn_rows(h1, n2, tgt, wg, wu, wd, g2, g3, rows_per_example):
    rows = h1.shape[0]
    tm = _row_tile(rows, FFN_ROW_TILE)
    ff_blocks = [slice(lo, hi) for lo, hi in zip(FF_SPLIT[:-1], FF_SPLIT[1:])]

    def body(h1_ref, n2_ref, t_ref, wg_ref, wu_ref, wd_ref, g2_ref, g3_ref,
             f_ref, da_ref, db_ref, dh2_ref, dh1_ref, dh1b_ref, part_ref):
        i = pl.program_id(0)
        n2 = n2_ref[...]
        y2 = jnp.zeros((tm, D), F32)
        for cs in ff_blocks:
            a = _dot(n2, wg_ref[cs, :], _NT)
            b = _dot(n2, wu_ref[cs, :], _NT)
            f = (a * _sigmoid(a) * b).astype(BF16)
            f_ref[:, cs] = f
            da_ref[:, cs] = a.astype(BF16)
            db_ref[:, cs] = b.astype(BF16)
            y2 = y2 + _dot(f, wd_ref[cs, :])
        h1 = h1_ref[...]
        h2 = h1 + y2
        r3 = lax.rsqrt(jnp.mean(h2 * h2, axis=-1, keepdims=True) + RMS_EPS)
        xh3 = h2 * r3
        g3 = g3_ref[...]
        pos = (i * tm + lax.broadcasted_iota(jnp.int32, (tm, 1), 0)) % rows_per_example
        valid = pos >= LEAD
        err = jnp.where(valid, xh3 * g3 - t_ref[...], 0.0)
        loss = 0.5 / D * jnp.sum(jnp.sum(err * err, axis=-1, keepdims=True), axis=0, keepdims=True)
        dy = err * (1.0 / D)
        dg3 = jnp.sum(dy * xh3, axis=0, keepdims=True)
        dxh = dy * g3
        dh2 = r3 * (dxh - xh3 * jnp.mean(dxh * xh3, axis=-1, keepdims=True))
        dh2b = dh2.astype(BF16)
        dh2_ref[...] = dh2b
        dn2 = jnp.zeros((tm, D), F32)
        for cs in ff_blocks:
            df = _dot(dh2b, wd_ref[cs, :], _NT)
            a = da_ref[:, cs].astype(F32)
            b = db_ref[:, cs].astype(F32)
            sg = _sigmoid(a)
            da = (df * b * sg * (1.0 + a * (1.0 - sg))).astype(BF16)
            db = (df * a * sg).astype(BF16)
            da_ref[:, cs] = da
            db_ref[:, cs] = db
            dn2 = dn2 + _dot(da, wg_ref[cs, :]) + _dot(db, wu_ref[cs, :])
        r2 = lax.rsqrt(jnp.mean(h1 * h1, axis=-1, keepdims=True) + RMS_EPS)
        xh2 = h1 * r2
        dg2 = jnp.sum(dn2 * xh2, axis=0, keepdims=True)
        dxh2 = dn2 * g2_ref[...]
        dh1 = dh2 + r2 * (dxh2 - xh2 * jnp.mean(dxh2 * xh2, axis=-1, keepdims=True))
        dh1_ref[...] = dh1
        dh1b_ref[...] = dh1.astype(BF16)

        @pl.when(i == 0)
        def _():
            part_ref[...] = jnp.zeros_like(part_ref)

        part_ref[0:1, :] += dg3
        part_ref[1:2, :] += dg2
        part_ref[2:3, :] += jnp.broadcast_to(loss, (1, D))

    return pl.pallas_call(
        body, name="ffn_rows", grid=(rows // tm,),
        in_specs=[_rows(tm, D), _rows(tm, D), _rows(tm, D), _whole_vmem(), _whole_vmem(), _whole_vmem(),
                  _fixed((1, D)), _fixed((1, D))],
        out_specs=[_rows(tm, D_FF), _rows(tm, D_FF), _rows(tm, D_FF), _rows(tm, D), _rows(tm, D), _rows(tm, D),
                   _fixed((8, D))],
        out_shape=[jax.ShapeDtypeStruct((rows, D_FF), BF16)] * 3
        + [jax.ShapeDtypeStruct((rows, D), BF16), jax.ShapeDtypeStruct((rows, D), F32),
           jax.ShapeDtypeStruct((rows, D), BF16), jax.ShapeDtypeStruct((8, D), F32)],
        compiler_params=_params(("arbitrary",)),
    )(h1, n2, tgt, wg, wu, wd, g2, g3)


def _bwd_outproj(dh1b, w_out, token):
    rows = dh1b.shape[0]
    tm = _row_tile(rows, ROW_TILE)

    def body(d_ref, w_ref, token_ref, dyc_ref, dyg_ref):
        d = d_ref[...]
        dyc_ref[...] = _dot(d, w_ref[0:C_CONV, :], _NT)
        dyg_ref[...] = _dot(d, w_ref[C_CONV:D, :], _NT)

    return pl.pallas_call(
        body, name="bwd_outproj", grid=(rows // tm,),
        in_specs=[_rows(tm, D), _whole_vmem(), _fixed((8, 128))],
        out_specs=[_rows(tm, C_CONV), _rows(tm, GLA_V)],
        out_shape=[jax.ShapeDtypeStruct((rows, C_CONV), F32), jax.ShapeDtypeStruct((rows, GLA_V), F32)],
        compiler_params=_params(("parallel",)),
    )(dh1b, w_out, token)


def _bwd_inproj(duc, dqk, dvg, dlr, dh1, h0, w_in, g1, token, rows_per_example):
    rows = h0.shape[0]
    tm = _row_tile(rows_per_example, ROW_TILE)
    tiles_per_example = rows_per_example // tm

    def body(duc_ref, dqk_ref, dvg_ref, dlr_ref, dh1_ref, h_ref, w_ref, g_ref, token_ref, dh0_ref, part_ref, dmeta_ref):
        dn = (_dot(duc_ref[...], w_ref[:, 0:1024], _NT) + _dot(dqk_ref[...], w_ref[:, 1024:1536], _NT)
              + _dot(dvg_ref[...], w_ref[:, 1536:2560], _NT) + _dot(dlr_ref[...], w_ref[:, 2560:2688], _NT))
        h = h_ref[...]
        r = lax.rsqrt(jnp.mean(h * h, axis=-1, keepdims=True) + RMS_EPS)
        xh = h * r
        dg = jnp.sum(dn * xh, axis=0, keepdims=True)
        dxh = dn * g_ref[...]
        dh0 = dh1_ref[...] + r * (dxh - xh * jnp.mean(dxh * xh, axis=-1, keepdims=True))
        dh0_ref[...] = dh0
        i = pl.program_id(0)

        @pl.when(i == 0)
        def _():
            part_ref[...] = jnp.zeros_like(part_ref)
            dmeta_ref[...] = jnp.zeros_like(dmeta_ref)

        part_ref[0:1, :] += dg

        @pl.when(i % tiles_per_example == 0)
        def _():
            dmeta_ref[...] += dh0[ZROWS:LEAD, :]

    return pl.pallas_call(
        body, name="bwd_inproj", grid=(rows // tm,),
        in_specs=[_rows(tm, 1024), _rows(tm, 512), _rows(tm, 1024), _rows(tm, RANK_P), _rows(tm, D), _rows(tm, D),
                  _whole_vmem(), _fixed((1, D)), _fixed((8, 128))],
        out_specs=[_rows(tm, D), _fixed((8, D)), _fixed((N_META, D))],
        out_shape=[jax.ShapeDtypeStruct((rows, D), F32), jax.ShapeDtypeStruct((8, D), F32),
                   jax.ShapeDtypeStruct((N_META, D), F32)],
        compiler_params=_params(("arbitrary",)),
    )(duc, dqk, dvg, dlr, dh1, h0, w_in, g1, token)


def _dw_blocked(a, bs, width, name):
    rows, m = a.shape
    ws = [b.shape[1] for b in bs]
    assert sum(ws) >= N_DEV * width
    tk = _row_tile(rows, DW_ROW_TILE)
    nk = rows // tk

    def body(a_ref, *refs):
        b_refs, o_ref, acc_ref = refs[:len(bs)], refs[len(bs)], refs[len(bs) + 1]
        k = pl.program_id(0)

        @pl.when(k == 0)
        def _():
            acc_ref[...] = jnp.zeros_like(acc_ref)

        at = a_ref[...].T
        off = 0
        for b_ref, w in zip(b_refs, ws):
            acc_ref[:, off:off + w] += _dot(at, b_ref[...])
            off += w

        @pl.when(k == nk - 1)
        def _():
            for d in range(N_DEV):
                o_ref[d] = acc_ref[:, d * width:(d + 1) * width].astype(BF16)

    return pl.pallas_call(
        body, name=name, grid=(nk,),
        in_specs=[_rows(tk, m)] + [_rows(tk, w) for w in ws],
        out_specs=_fixed((N_DEV, m, width)),
        out_shape=jax.ShapeDtypeStruct((N_DEV, m, width), BF16),
        scratch_shapes=[pltpu.VMEM((m, sum(ws)), F32)],
        compiler_params=_params(("arbitrary",)),
    )(a, *bs)


def _dw_out(yc, yg, dh1b):
    rows = yc.shape[0]
    tk = _row_tile(rows, DW_ROW_TILE)
    nk = rows // tk

    def body(yc_ref, yg_ref, d_ref, o_ref, acc_ref):
        k = pl.program_id(0)

        @pl.when(k == 0)
        def _():
            acc_ref[...] = jnp.zeros_like(acc_ref)

        d = d_ref[...]
        acc_ref[0:C_CONV, :] += _dot(yc_ref[...], d, _TN)
        acc_ref[C_CONV:D, :] += _dot(yg_ref[...], d, _TN)

        @pl.when(k == nk - 1)
        def _():
            o_ref[...] = acc_ref[...].astype(BF16)

    return pl.pallas_call(
        body, name="dw_out", grid=(nk,),
        in_specs=[_rows(tk, C_CONV), _rows(tk, GLA_V), _rows(tk, D)],
        out_specs=_fixed((D, D)), out_shape=jax.ShapeDtypeStruct((D, D), BF16),
        scratch_shapes=[pltpu.VMEM((D, D), F32)],
        compiler_params=_params(("arbitrary",)),
    )(yc, yg, dh1b)


def _matmul_tn(a, b, name):
    rows, m = a.shape
    n = b.shape[1]
    tk = _row_tile(rows, DW_ROW_TILE)
    tn = n if n <= 1024 else FF_CHUNK
    tm_ = m if m <= 1024 else FF_CHUNK
    assert n % tn == 0 and m % tm_ == 0
    nk = rows // tk

    def body(a_ref, b_ref, o_ref, acc_ref):
        k = pl.program_id(2)

        @pl.when(k == 0)
        def _():
            acc_ref[...] = jnp.zeros_like(acc_ref)

        acc_ref[...] += _dot(a_ref[...], b_ref[...], _TN)

        @pl.when(k == nk - 1)
        def _():
            o_ref[...] = acc_ref[...].astype(BF16)

    return pl.pallas_call(
        body, name=name, grid=(m // tm_, n // tn, nk),
        in_specs=[pl.BlockSpec((tk, tm_), lambda i, j, k: (k, i)), pl.BlockSpec((tk, tn), lambda i, j, k: (k, j))],
        out_specs=pl.BlockSpec((tm_, tn), lambda i, j, k: (i, j)),
        out_shape=jax.ShapeDtypeStruct((m, n), BF16),
        scratch_shapes=[pltpu.VMEM((tm_, tn), F32)],
        compiler_params=_params(("parallel", "parallel", "arbitrary")),
    )(a, b)


HALO = 32
LANES = 128


def _shifted(win, offsets):
    for r in range(8):
        js = [j for j, k in enumerate(offsets) if k % 8 == r]
        if js:
            rolled = win if r == 0 else pltpu.roll(win, CHUNK + HALO - r, 0)
            for j in js:
                yield j, rolled[offsets[j] - r:offsets[j] - r + CHUNK]


def _glu_into(uc_ref, vs_ref, n_chunk):
    vs_ref[0:CHUNK, :] = jnp.zeros((CHUNK, C_CONV), F32)

    def glu(i, carry):
        base = pl.multiple_of(i * CHUNK, CHUNK)
        val = uc_ref[pl.ds(base, CHUNK), 0:C_CONV]
        gate = uc_ref[pl.ds(base, CHUNK), C_CONV:2 * C_CONV]
        vs_ref[pl.ds(base + CHUNK, CHUNK), :] = val * _sigmoid(gate)
        return carry

    lax.fori_loop(0, n_chunk, glu, 0)


def _fwd_conv(uc, conv_w, conv_b, ln_g, ln_b, n_ex):
    rows = uc.shape[0]
    lp = rows // n_ex
    n_chunk = lp // CHUNK

    def body(uc_ref, w_ref, b_ref, lg_ref, lb_ref, ypre_ref, yc_ref, vs_ref):
        _glu_into(uc_ref, vs_ref, n_chunk)

        def conv(i, carry):
            base = pl.multiple_of(i * CHUNK, CHUNK)
            for lb in range(C_CONV // LANES):
                ls = slice(lb * LANES, (lb + 1) * LANES)
                win = vs_ref[pl.ds(base + CHUNK - HALO, CHUNK + HALO), ls]
                acc = jnp.broadcast_to(b_ref[:, ls], (CHUNK, LANES))
                for j, rows_j in _shifted(win, [HALO - (CONV_W - 1) + j for j in range(CONV_W)]):
                    acc = acc + w_ref[j:j + 1, ls] * rows_j
                ypre_ref[pl.ds(base, CHUNK), ls] = acc
            y = ypre_ref[pl.ds(base, CHUNK), :]
            mu = jnp.mean(y, axis=-1, keepdims=True)
            yc_ = y - mu
            rstd = lax.rsqrt(jnp.mean(yc_ * yc_, axis=-1, keepdims=True) + LN_EPS)
            s = yc_ * rstd * lg_ref[...] + lb_ref[...]
            yc_ref[pl.ds(base, CHUNK), :] = (s * _sigmoid(s)).astype(BF16)
            return carry

        lax.fori_loop(0, n_chunk, conv, 0)

    ex = lambda w: pl.BlockSpec((lp, w), lambda b: (b, 0))
    return pl.pallas_call(
        body, name="fwd_conv", grid=(n_ex,),
        in_specs=[ex(2 * C_CONV), _fixed((32, C_CONV)), _fixed((1, C_CONV)), _fixed((1, C_CONV)), _fixed((1, C_CONV))],
        out_specs=[ex(C_CONV), ex(C_CONV)],
        out_shape=[jax.ShapeDtypeStruct((rows, C_CONV), F32), jax.ShapeDtypeStruct((rows, C_CONV), BF16)],
        scratch_shapes=[pltpu.VMEM((lp + CHUNK, C_CONV), F32)],
        compiler_params=_params(("parallel",)),
    )(uc, conv_w, conv_b, ln_g, ln_b)


def _bwd_conv(uc, ypre, dyc, conv_w, ln_g, ln_b, token, n_ex):
    rows = uc.shape[0]
    lp = rows // n_ex
    n_chunk = lp // CHUNK

    def body(uc_ref, ypre_ref, dyc_ref, w_ref, lg_ref, lb_ref, token_ref, duc_ref, dw_ref, dvec_ref, vs_ref, dys_ref,
             dwacc_ref):
        _glu_into(uc_ref, vs_ref, n_chunk)
        dys_ref[pl.ds(lp, CHUNK), :] = jnp.zeros((CHUNK, C_CONV), F32)
        dwacc_ref[...] = jnp.zeros_like(dwacc_ref)

        def ln_bwd(i, carry):
            dcb, dlg, dlb = carry
            base = pl.multiple_of(i * CHUNK, CHUNK)
            y = ypre_ref[pl.ds(base, CHUNK), :]
            mu = jnp.mean(y, axis=-1, keepdims=True)
            yc_ = y - mu
            rstd = lax.rsqrt(jnp.mean(yc_ * yc_, axis=-1, keepdims=True) + LN_EPS)
            xh = yc_ * rstd
            s = xh * lg_ref[...] + lb_ref[...]
            sg = _sigmoid(s)
            ds = dyc_ref[pl.ds(base, CHUNK), :] * (sg * (1.0 + s * (1.0 - sg)))
            dxh = ds * lg_ref[...]
            dy = rstd * (dxh - jnp.mean(dxh, axis=-1, keepdims=True) - xh * jnp.mean(dxh * xh, axis=-1, keepdims=True))
            dys_ref[pl.ds(base, CHUNK), :] = dy
            return (dcb + jnp.sum(dy, axis=0, keepdims=True), dlg + jnp.sum(ds * xh, axis=0, keepdims=True),
                    dlb + jnp.sum(ds, axis=0, keepdims=True))

        zero = jnp.zeros((1, C_CONV), F32)
        dcb, dlg, dlb = lax.fori_loop(0, n_chunk, ln_bwd, (zero, zero, zero))

        @pl.when(pl.program_id(0) == 0)
        def _():
            dvec_ref[...] = jnp.zeros_like(dvec_ref)
            dw_ref[...] = jnp.zeros_like(dw_ref)

        dvec_ref[0:1, :] += dcb
        dvec_ref[1:2, :] += dlg
        dvec_ref[2:3, :] += dlb

        def taps(i, carry):
            base = pl.multiple_of(i * CHUNK, CHUNK)
            for lb in range(C_CONV // LANES):
                ls = slice(lb * LANES, (lb + 1) * LANES)
                dwin = dys_ref[pl.ds(base, CHUNK + HALO), ls]
                vwin = vs_ref[pl.ds(base + CHUNK - HALO, CHUNK + HALO), ls]
                dy = dwin[0:CHUNK]
                acc = jnp.zeros((CHUNK, LANES), F32)
                for j, rows_j in _shifted(dwin, [CONV_W - 1 - j for j in range(CONV_W)]):
                    acc = acc + w_ref[j:j + 1, ls] * rows_j
                for j, rows_j in _shifted(vwin, [HALO - (CONV_W - 1) + j for j in range(CONV_W)]):
                    dwacc_ref[8 * j:8 * j + 8, ls] += jnp.sum((dy * rows_j).reshape(CHUNK // 8, 8, LANES), axis=0)
                val = uc_ref[pl.ds(base, CHUNK), ls]
                gate = uc_ref[pl.ds(base, CHUNK), C_CONV + lb * LANES:C_CONV + (lb + 1) * LANES]
                sg = _sigmoid(gate)
                duc_ref[pl.ds(base, CHUNK), ls] = (acc * sg).astype(BF16)
                duc_ref[pl.ds(base, CHUNK), C_CONV + lb * LANES:C_CONV + (lb + 1) * LANES] = (
                    acc * val * sg * (1.0 - sg)).astype(BF16)
            return carry

        lax.fori_loop(0, n_chunk, taps, 0)
        for j in range(CONV_W):
            dw_ref[j:j + 1, :] += jnp.sum(dwacc_ref[8 * j:8 * j + 8, :], axis=0, keepdims=True)

    ex = lambda w: pl.BlockSpec((lp, w), lambda b: (b, 0))
    return pl.pallas_call(
        body, name="bwd_conv", grid=(n_ex,),
        in_specs=[ex(2 * C_CONV), ex(C_CONV), ex(C_CONV), _fixed((32, C_CONV)), _fixed((1, C_CONV)), _fixed((1, C_CONV)),
                  _fixed((8, 128))],
        out_specs=[ex(2 * C_CONV), _fixed((32, C_CONV)), _fixed((8, C_CONV))],
        out_shape=[jax.ShapeDtypeStruct((rows, 2 * C_CONV), BF16), jax.ShapeDtypeStruct((32, C_CONV), F32),
                   jax.ShapeDtypeStruct((8, C_CONV), F32)],
        scratch_shapes=[pltpu.VMEM((lp + CHUNK, C_CONV), F32), pltpu.VMEM((lp + CHUNK, C_CONV), F32),
                        pltpu.VMEM((8 * 32, C_CONV), F32)],
        compiler_params=_params(("arbitrary",)),
    )(uc, ypre, dyc, conv_w, ln_g, ln_b, token)


def _seg_chunks(n_chunk):
    return max(c for c in (11, 3, 1) if n_chunk % c == 0)


def _block_mask(shape, row_block, lane_block):
    return (lax.broadcasted_iota(jnp.int32, shape, 0) // row_block) == (lax.broadcasted_iota(jnp.int32, shape, 1) // lane_block)


def _per_head_rows(x, mask):
    return jnp.where(mask, jnp.concatenate([x] * GLA_H, axis=0), 0)


def _fold_heads(full, lane_block):
    lane = lax.broadcasted_iota(jnp.int32, (1, full.shape[1]), 1) // lane_block
    out = jnp.where(lane == 0, full[0:CHUNK], 0.0)
    for h in range(1, GLA_H):
        out = out + jnp.where(lane == h, full[h * CHUNK:(h + 1) * CHUNK], 0.0)
    return out


def _causal_heads():
    return (lax.broadcasted_iota(jnp.int32, (CHUNK, GLA_H * CHUNK), 1) % CHUNK) <= lax.broadcasted_iota(
        jnp.int32, (CHUNK, GLA_H * CHUNK), 0)


def _cumsum_rows(x):
    row = lax.broadcasted_iota(jnp.int32, x.shape, 0)
    s = 1
    while s < CHUNK:
        x = x + jnp.where(row >= s, pltpu.roll(x, s, 0), 0.0)
        s *= 2
    return x


def _rev_cumsum_rows(x):
    row = lax.broadcasted_iota(jnp.int32, x.shape, 0)
    s = 1
    while s < CHUNK:
        x = x + jnp.where(row < CHUNK - s, pltpu.roll(x, CHUNK - s, 0), 0.0)
        s *= 2
    return x


def _gate_terms(lr_ref, w2_ref, gb_ref, rs, first_pos):
    z = _dot(lr_ref[rs, :].astype(BF16), w2_ref[...]) + gb_ref[...]
    la = (jnp.minimum(z, 0.0) - jnp.log(1.0 + jnp.exp(-jnp.abs(z)))) * (1.0 / TAU)
    pos = first_pos + lax.broadcasted_iota(jnp.int32, (CHUNK, 1), 0)
    live = pos >= ZROWS
    la = jnp.where(live, la, 0.0)
    return z, live, _cumsum_rows(la)


def _fwd_gla(qk, vg, lr, w2p, gb, ng, n_ex):
    rows = qk.shape[0]
    lp = rows // n_ex
    n_chunk = lp // CHUNK
    sc = _seg_chunks(n_chunk)
    n_seg = n_chunk // sc
    seg = sc * CHUNK

    def body(qk_ref, vg_ref, lr_ref, w2_ref, gb_ref, ng_ref, yg_ref, o_ref, st_ref, state_ref):
        sidx = pl.program_id(1)

        @pl.when(sidx == 0)
        def _():
            state_ref[...] = jnp.zeros_like(state_ref)

        causal = _causal_heads()
        k_mask = _block_mask((GLA_H * CHUNK, GLA_K), CHUNK, GLA_DK)
        v_mask = _block_mask((GLA_H * CHUNK, GLA_V), CHUNK, GLA_DV)
        s_mask = _block_mask((GLA_V, GLA_K), GLA_DV, GLA_DK)

        def chunk(ci, carry):
            base = pl.multiple_of(ci * CHUNK, CHUNK)
            rs = pl.ds(base, CHUNK)
            _, _, bcum = _gate_terms(lr_ref, w2_ref, gb_ref, rs, (sidx * sc + ci) * CHUNK)
            bl = bcum[CHUNK - 1:CHUNK, :]
            q = qk_ref[rs, 0:GLA_K]
            k = qk_ref[rs, GLA_K:2 * GLA_K]
            qt = (q * (GLA_DK ** -0.5) * jnp.exp(bcum)).astype(BF16)
            kt = (k * jnp.exp(-bcum)).astype(BF16)
            kh = (k * jnp.exp(bl - bcum)).astype(BF16)
            vb = vg_ref[rs, 0:GLA_V].astype(BF16)
            state = state_ref[...]
            st_ref[ci] = state
            a = jnp.where(causal, _dot(qt, _per_head_rows(kt, k_mask), _NT), 0.0)
            o = _dot(a.astype(BF16), _per_head_rows(vb, v_mask)) + _dot(qt, state.astype(BF16), _NT)
            o_ref[rs, :] = o
            for h in range(GLA_H):
                hs = slice(h * GLA_DV, (h + 1) * GLA_DV)
                oh = o[:, hs]
                ro = lax.rsqrt(jnp.mean(oh * oh, axis=-1, keepdims=True) + RMS_EPS)
                g = vg_ref[rs, GLA_V + h * GLA_DV:GLA_V + (h + 1) * GLA_DV]
                yg_ref[rs, hs] = (oh * ro * ng_ref[...] * g * _sigmoid(g)).astype(BF16)
            state_ref[...] = state * jnp.exp(bl) + jnp.where(s_mask, _dot(vb, kh, _TN), 0.0)
            return carry

        lax.fori_loop(0, sc, chunk, 0)

    sg = lambda w: pl.BlockSpec((seg, w), lambda b, s: (b * n_seg + s, 0))
    return pl.pallas_call(
        body, name="fwd_gla", grid=(n_ex, n_seg),
        in_specs=[sg(2 * GLA_K), sg(2 * GLA_V), sg(RANK_P), _fixed((RANK_P, GLA_K)), _fixed((1, GLA_K)), _fixed((1, GLA_DV))],
        out_specs=[sg(GLA_V), sg(GLA_V), pl.BlockSpec((sc, GLA_V, GLA_K), lambda b, s: (b * n_seg + s, 0, 0))],
        out_shape=[jax.ShapeDtypeStruct((rows, GLA_V), BF16), jax.ShapeDtypeStruct((rows, GLA_V), F32),
                   jax.ShapeDtypeStruct((n_ex * n_chunk, GLA_V, GLA_K), F32)],
        scratch_shapes=[pltpu.VMEM((GLA_V, GLA_K), F32)],
        compiler_params=_params(("parallel", "arbitrary")),
    )(qk, vg, lr, w2p, gb, ng)


def _bwd_gla(qk, vg, lr, o, st, dyg, w2p, gb, ng, n_ex):
    rows = qk.shape[0]
    lp = rows // n_ex
    n_chunk = lp // CHUNK
    sc = _seg_chunks(n_chunk)
    n_seg = n_chunk // sc
    seg = sc * CHUNK

    def body(qk_ref, vg_ref, lr_ref, o_ref, st_ref, dyg_ref, w2_ref, gb_ref, ng_ref,
             dqk_ref, dvg_ref, dlr_ref, dw2_ref, dvec_ref, gt_ref, dz_ref):
        step = pl.program_id(1)
        sidx = n_seg - 1 - step

        @pl.when(step == 0)
        def _():
            gt_ref[...] = jnp.zeros_like(gt_ref)

        @pl.when((step == 0) & (pl.program_id(0) == 0))
        def _():
            dw2_ref[...] = jnp.zeros_like(dw2_ref)
            dvec_ref[...] = jnp.zeros_like(dvec_ref)

        causal = _causal_heads()
        k_mask = _block_mask((GLA_H * CHUNK, GLA_K), CHUNK, GLA_DK)
        v_mask = _block_mask((GLA_H * CHUNK, GLA_V), CHUNK, GLA_DV)
        s_mask = _block_mask((GLA_V, GLA_K), GLA_DV, GLA_DK)
        last_row = lax.broadcasted_iota(jnp.int32, (CHUNK, 1), 0) == CHUNK - 1
        ng = ng_ref[...]

        def chunk(ii, dng):
            ci = sc - 1 - ii
            base = pl.multiple_of(ci * CHUNK, CHUNK)
            rs = pl.ds(base, CHUNK)
            z, live, bcum = _gate_terms(lr_ref, w2_ref, gb_ref, rs, (sidx * sc + ci) * CHUNK)
            bl = bcum[CHUNK - 1:CHUNK, :]
            ebl = jnp.exp(bl)
            q = qk_ref[rs, 0:GLA_K]
            k = qk_ref[rs, GLA_K:2 * GLA_K]
            eb = jnp.exp(bcum)
            enb = jnp.exp(-bcum)
            ehb = jnp.exp(bl - bcum)
            qt = q * (GLA_DK ** -0.5) * eb
            kt = k * enb
            kh = k * ehb
            qtb = qt.astype(BF16)
            vb = vg_ref[rs, 0:GLA_V].astype(BF16)
            k_rows = _per_head_rows(kt.astype(BF16), k_mask)
            v_rows = _per_head_rows(vb, v_mask)
            gt = gt_ref[...]
            gtb = gt.astype(BF16)
            s_in = st_ref[ci]
            dos = []
            for h in range(GLA_H):
                hs = slice(h * GLA_DV, (h + 1) * GLA_DV)
                gs = slice(GLA_V + h * GLA_DV, GLA_V + (h + 1) * GLA_DV)
                oh = o_ref[rs, hs]
                ro = lax.rsqrt(jnp.mean(oh * oh, axis=-1, keepdims=True) + RMS_EPS)
                on = oh * ro
                g = vg_ref[rs, gs]
                sg = _sigmoid(g)
                dout = dyg_ref[rs, hs]
                dvg_ref[rs, gs] = (dout * on * ng * (sg * (1.0 + g * (1.0 - sg)))).astype(BF16)
                dw = dout * g * sg
                dng = dng + jnp.sum(dw * on, axis=0, keepdims=True)
                don = dw * ng
                dos.append((ro * (don - on * jnp.mean(don * on, axis=-1, keepdims=True))).astype(BF16))
            dob = jnp.concatenate(dos, axis=1)
            a = jnp.where(causal, _dot(qtb, k_rows, _NT), 0.0).astype(BF16)
            da = jnp.where(causal, _dot(dob, v_rows, _NT), 0.0).astype(BF16)
            dv = _fold_heads(_dot(a, dob, _TN), GLA_DV) + _dot(kh.astype(BF16), gtb, _NT)
            dvg_ref[rs, 0:GLA_V] = dv.astype(BF16)
            dkh = _dot(vb, gtb)
            dqt = _dot(da, k_rows) + _dot(dob, s_in.astype(BF16))
            dkt = _fold_heads(_dot(da, qtb, _TN), GLA_DK)
            dbl = jnp.sum(gt * s_in, axis=0, keepdims=True) * ebl + jnp.sum(dkh * kh, axis=0, keepdims=True)
            dqk_ref[rs, 0:GLA_K] = (dqt * (GLA_DK ** -0.5) * eb).astype(BF16)
            dqk_ref[rs, GLA_K:2 * GLA_K] = (dkt * enb + dkh * ehb).astype(BF16)
            db = dqt * qt - dkt * kt - dkh * kh
            db = jnp.where(last_row, db + dbl, db)
            dla = jnp.where(live, _rev_cumsum_rows(db), 0.0)
            dz_ref[rs, :] = dla * (1.0 / TAU) * (1.0 - _sigmoid(z))
            gt_ref[...] = jnp.where(s_mask, _dot(dob, qtb, _TN), 0.0) + gt * ebl
            return dng

        dng = lax.fori_loop(0, sc, chunk, jnp.zeros((1, GLA_DV), F32))
        dz = dz_ref[...]
        dzb = dz.astype(BF16)
        dlr_ref[...] = _dot(dzb, w2_ref[...], _NT).astype(BF16)
        dw2_ref[...] += _dot(lr_ref[...].astype(BF16), dzb, _TN)
        dvec_ref[0:1, :] += jnp.sum(dz, axis=0, keepdims=True)
        dvec_ref[1:2, 0:GLA_DV] += dng

    sg_ = lambda w: pl.BlockSpec((seg, w), lambda b, s: (b * n_seg + n_seg - 1 - s, 0))
    return pl.pallas_call(
        body, name="bwd_gla", grid=(n_ex, n_seg),
        in_specs=[sg_(2 * GLA_K), sg_(2 * GLA_V), sg_(RANK_P), sg_(GLA_V),
                  pl.BlockSpec((sc, GLA_V, GLA_K), lambda b, s: (b * n_seg + n_seg - 1 - s, 0, 0)), sg_(GLA_V),
                  _fixed((RANK_P, GLA_K)), _fixed((1, GLA_K)), _fixed((1, GLA_DV))],
        out_specs=[sg_(2 * GLA_K), sg_(2 * GLA_V), sg_(RANK_P), _fixed((RANK_P, GLA_K)), _fixed((8, GLA_K))],
        out_shape=[jax.ShapeDtypeStruct((rows, 2 * GLA_K), BF16), jax.ShapeDtypeStruct((rows, 2 * GLA_V), BF16),
                   jax.ShapeDtypeStruct((rows, RANK_P), BF16), jax.ShapeDtypeStruct((RANK_P, GLA_K), F32),
                   jax.ShapeDtypeStruct((8, GLA_K), F32)],
        scratch_shapes=[pltpu.VMEM((GLA_V, GLA_K), F32), pltpu.VMEM((seg, GLA_K), F32)],
        compiler_params=_params(("arbitrary", "arbitrary")),
    )(qk, vg, lr, o, st, dyg, w2p, gb, ng)


def _pad_rows(x, tgt):
    return jnp.pad(x, ((0, 0), (LEAD, 0), (0, 0))), jnp.pad(tgt, ((0, 0), (LEAD, 0), (0, 0)))


def _local_step(h0, tgt_p, p, w_out, pass_on, late_weights, send_early):
    n_ex, lp, _ = h0.shape
    rows = n_ex * lp
    meta = jnp.broadcast_to(p["meta"][None], (n_ex, N_META, D))
    h0 = lax.dynamic_update_slice(h0, meta, (0, ZROWS, 0)).reshape(rows, D)
    tgt_p = tgt_p.reshape(rows, D)

    uc, qk, vg, lr, n1 = _fwd_inproj(h0, p["g1"], p["w_in"])
    ypre, yc = _fwd_conv(uc, p["conv_w"], p["conv_b"], p["ln_g"], p["ln_b"], n_ex)
    yg, o, st = _fwd_gla(qk, vg, lr, p["w2"], p["gb"], p["ng"], n_ex)
    h1, n2 = _fwd_outproj(yc, yg, h0, w_out, p["g2"], pass_on(yg))
    wg, wu, wd = late_weights(n2)
    f, da, db, dh2, dh1, dh1b, part = _ffn_rows(h1, n2, tgt_p, wg, wu, wd, p["g2"], p["g3"], lp)
    g = {}
    token = send_early("ffn", [_matmul_tn(a_, b_, name).reshape(N_DEV, FF_S, D) for a_, b_, name in (
        (da, n2, "dw_gate"), (db, n2, "dw_up"), (f, dh2, "dw_down"))])
    dyc, dyg = _bwd_outproj(dh1b, w_out, token)
    token = send_early("out", [_dw_out(yc, yg, dh1b).reshape(N_DEV, W_OUT_S, D)])
    duc, g["conv_w"], g["conv_vec"] = _bwd_conv(uc, ypre, dyc, p["conv_w"], p["ln_g"], p["ln_b"], token, n_ex)
    dqk, dvg, dlr, g["w2"], g["gla_vec"] = _bwd_gla(qk, vg, lr, o, st, dyg, p["w2"], p["gb"], p["ng"], n_ex)
    token = send_early("in", [_dw_blocked(n1, [duc, dqk, dvg, dlr], W_IN_S, "dw_in")])
    dh0, g["in_vec"], g["meta"] = _bwd_inproj(duc, dqk, dvg, dlr, dh1, h0, p["w_in"], p["g1"], token, lp)
    g["ffn_vec"] = part
    return dh0.reshape(n_ex, lp, D)[:, LEAD:], g


W_IN_S = D_IN // N_DEV
W_OUT_S = D // N_DEV
FF_S = D_FF // N_DEV
CONV_S = C_CONV // N_DEV
GATE_S = GLA_K // N_DEV
SMALL_PACK = 64
CONV_ROW = 16
GATE_ROW = 48
VEC_ROWS = 16
_VEC_ROWS = (("norm_mix_g", D), ("conv_b", C_CONV), ("conv_ln_g", C_CONV), ("conv_ln_b", C_CONV), ("gla_gate_b", GLA_K),
             ("gla_norm_g", GLA_DV), ("norm_ffn_g", D), ("norm_final_g", D))
LOSS_ROW = len(_VEC_ROWS)


def _position():
    return lax.axis_index("x"), lax.axis_index("y"), lax.axis_index("c")


def _any():
    return pl.BlockSpec(memory_space=pl.ANY)


def _stage(mats, meta, conv_w, w2):
    n_t = len(mats) + 1

    def body(*refs):
        ins = refs[0:n_t - 1]
        meta_ref, cw_ref, w2_ref = refs[n_t - 1:n_t + 2]
        lands = refs[n_t + 2:2 * n_t + 2]
        shards = refs[2 * n_t + 2:3 * n_t + 2]
        sems = refs[3 * n_t + 2]
        for s_ref, w_ref in zip(shards, ins):
            s_ref[...] = w_ref[...].astype(BF16)
        sp = shards[n_t - 1]
        sp[...] = jnp.zeros_like(sp)
        sp[0:N_META, :] = meta_ref[...]
        sp[CONV_ROW:CONV_ROW + CONV_W, 0:CONV_S] = cw_ref[...]
        sp[GATE_ROW:GATE_ROW + RANK, 0:GATE_S] = w2_ref[...]
        x, y, c = _position()
        mine = [pltpu.make_async_copy(shards[t], lands[t].at[4 * x + 2 * y + c], sems.at[t]) for t in range(n_t)]
        for cp in mine:
            cp.start()
        for cp in mine:
            cp.wait()

    shard_shapes = [jax.ShapeDtypeStruct(m.shape, BF16) for m in mats] + [jax.ShapeDtypeStruct((SMALL_PACK, 128), F32)]
    res = pl.pallas_call(
        body, name="stage",
        out_shape=[jax.ShapeDtypeStruct((N_DEV,) + s.shape, s.dtype) for s in shard_shapes] + shard_shapes,
        in_specs=[_whole_vmem()] * (n_t + 2), out_specs=[_any()] * n_t + [_whole_vmem()] * n_t,
        scratch_shapes=[pltpu.SemaphoreType.DMA((n_t,))],
        compiler_params=pltpu.CompilerParams(vmem_limit_bytes=VMEM_LIMIT),
    )(*mats, meta, conv_w, w2)
    return res[0:n_t], res[n_t:]


_HBM = pl.BlockSpec(memory_space=pltpu.HBM)
_SEM = pl.BlockSpec(memory_space=pltpu.SEMAPHORE)
_EFFECT = pltpu.SideEffectType.DATAFLOW_SIDE_EFFECTING


_N_ROUTES = {"scatter": 7, "first": 4, "forward": 3}


def _routes(mode):
    x, y, c = _position()
    me = 4 * x + 2 * y + c
    if mode == "scatter":
        out = []
        for k in range(1, N_DEV):
            px = 1 - x if k & 4 else x
            py = 1 - y if k & 2 else y
            pc = 1 - c if k & 1 else c
            out.append(((px, py, pc), 4 * px + 2 * py + pc, me))
        return out
    if mode == "first":
        return [(pos, None, me) for pos in ((x, y, 1 - c), (1 - x, y, c), (x, 1 - y, c), (1 - x, 1 - y, c))]
    assert mode == "forward"
    return [((x, y, 1 - c), 4 * px + 2 * py + c, 4 * px + 2 * py + c) for px, py in ((1 - x, y), (x, 1 - y), (1 - x, 1 - y))]


def _route_copies(mode, n, src_refs, land_refs, send_sems, recv_sems):
    nr = _N_ROUTES[mode]
    for i, (pos, src_blk, dst_blk) in enumerate(_routes(mode)):
        for t in range(n):
            src = land_refs[t] if mode == "forward" else src_refs[t]
            yield pltpu.make_async_remote_copy(
                src_ref=src if src_blk is None else src.at[src_blk], dst_ref=land_refs[t].at[dst_blk],
                send_sem=send_sems.at[nr * t + i], recv_sem=recv_sems.at[nr * t + i], device_id=pos, device_id_type=MESH)


def _in_hbm(a):
    return pltpu.with_memory_space_constraint(a, pltpu.HBM)


def _send_start(name, srcs, lands, mode, after):
    n, ns = len(lands), len(srcs)
    nsem = _N_ROUTES[mode] * n

    def body(*refs):
        src_refs, land_refs = refs[0:ns], refs[ns:ns + n]
        send_sems, recv_sems = refs[ns + n + 1:ns + n + 3]
        token = refs[2 * (ns + n) + 3]
        for cp in _route_copies(mode, n, src_refs, land_refs, send_sems, recv_sems):
            cp.start()
        token[...] = jnp.zeros_like(token)

    bufs = list(srcs) + list(lands)
    res = pl.pallas_call(
        body, name=name,
        out_shape=(pltpu.SemaphoreType.DMA((nsem,)), pltpu.SemaphoreType.DMA((nsem,)),
                   *[pltpu.HBM(b.shape, b.dtype) for b in bufs], jax.ShapeDtypeStruct((8, 128), F32)),
        in_specs=[_HBM] * len(bufs) + [_any()], out_specs=(_SEM, _SEM, *[_HBM] * len(bufs), _whole_vmem()),
        input_output_aliases={i: 2 + i for i in range(len(bufs))},
        compiler_params=pltpu.CompilerParams(has_side_effects=_EFFECT),
    )(*[_in_hbm(b) for b in bufs], after)
    return res[0], res[1], res[2:2 + ns], res[2 + ns:2 + ns + n], res[2 + ns + n]


def _send_wait(name, send_sems, recv_sems, srcs, lands, mode, after):
    n, ns = len(lands), len(srcs)

    def body(*refs):
        src_refs, land_refs = refs[0:ns], refs[ns:ns + n]
        send_sems, recv_sems = refs[ns + n:ns + n + 2]
        for cp in _route_copies(mode, n, src_refs, land_refs, send_sems, recv_sems):
            cp.wait_send()
            cp.wait_recv()

    bufs = list(srcs) + list(lands)
    res = pl.pallas_call(
        body, name=name,
        out_shape=tuple(pltpu.HBM(b.shape, b.dtype) for b in bufs),
        in_specs=[_HBM] * len(bufs) + [_SEM, _SEM, _any()], out_specs=tuple([_HBM] * len(bufs)),
        input_output_aliases={i: i for i in range(len(bufs))},
        compiler_params=pltpu.CompilerParams(has_side_effects=_EFFECT),
    )(*bufs, send_sems, recv_sems, after)
    return res[0:ns], res[ns:ns + n]


def _unshard_in(a_in, a_small, token):
    def body(a_ref, s_ref, token_ref, w_ref, meta_ref, cw_ref, w2_ref):
        w_ref[:, D_IN:D_INP] = jnp.zeros((D, D_INP - D_IN), BF16)
        w2_ref[...] = jnp.zeros_like(w2_ref)
        for d in range(N_DEV):
            w_ref[:, d * W_IN_S:(d + 1) * W_IN_S] = a_ref[d]
            meta_ref[:, d * 128:(d + 1) * 128] = s_ref[d, 0:N_META, :]
            cw_ref[:, d * CONV_S:(d + 1) * CONV_S] = s_ref[d, CONV_ROW:CONV_ROW + 32, 0:CONV_S]
            w2_ref[0:RANK, d * GATE_S:(d + 1) * GATE_S] = s_ref[d, GATE_ROW:GATE_ROW + RANK, 0:GATE_S].astype(BF16)

    return pl.pallas_call(
        body, name="unshard_in",
        out_shape=[jax.ShapeDtypeStruct((D, D_INP), BF16), jax.ShapeDtypeStruct((N_META, D), F32),
                   jax.ShapeDtypeStruct((32, C_CONV), F32), jax.ShapeDtypeStruct((RANK_P, GLA_K), BF16)],
        compiler_params=pltpu.CompilerParams(vmem_limit_bytes=VMEM_LIMIT),
    )(a_in, a_small, token)


def _pack_small(g):
    def body(meta_ref, cw_ref, w2_ref, in_vec, ffn_vec, conv_vec, gla_vec, sp, vp):
        sp[...] = jnp.zeros_like(sp)
        vp[...] = jnp.zeros_like(vp)
        for d in range(N_DEV):
            sp[d, 0:N_META, :] = meta_ref[:, d * 128:(d + 1) * 128]
            sp[d, CONV_ROW:CONV_ROW + 32, 0:CONV_S] = cw_ref[:, d * CONV_S:(d + 1) * CONV_S]
            sp[d, GATE_ROW:GATE_ROW + RANK, 0:GATE_S] = w2_ref[0:RANK, d * GATE_S:(d + 1) * GATE_S]
            vp[d, 0:1, :] = in_vec[0:1, :]
            vp[d, 1:4, 0:C_CONV] = conv_vec[0:3, :]
            vp[d, 4:5, 0:GLA_K] = gla_vec[0:1, :]
            vp[d, 5:6, 0:GLA_DV] = gla_vec[1:2, 0:GLA_DV]
            vp[d, 6:7, :] = ffn_vec[1:2, :]
            vp[d, 7:8, :] = ffn_vec[0:1, :]
            vp[d, LOSS_ROW:LOSS_ROW + 1, :] = ffn_vec[2:3, :]

    return pl.pallas_call(
        body, name="pack_small",
        out_shape=[jax.ShapeDtypeStruct((N_DEV, SMALL_PACK, 128), F32), jax.ShapeDtypeStruct((N_DEV, VEC_ROWS, D), F32)],
    )(g["meta"], g["conv_w"], g["w2"], g["in_vec"], g["ffn_vec"], g["conv_vec"], g["gla_vec"])


def _adamw(w, g, m, v):
    m = ADAM_B1 * m + (1.0 - ADAM_B1) * g
    v = ADAM_B2 * v + (1.0 - ADAM_B2) * (g * g)
    m_hat = m / (1.0 - ADAM_B1 ** ADAM_STEP)
    v_hat = v / (1.0 - ADAM_B2 ** ADAM_STEP)
    return -ADAM_LR * (m_hat / (jnp.sqrt(v_hat) + ADAM_EPS) + ADAM_WD * w), m, v


def _update_matrix(recv, own, me, w, m, v, name):
    _, r, c = recv.shape
    tr = _row_tile(r, 256)

    def body(me_ref, recv_ref, own_ref, w_ref, m_ref, v_ref, g_ref, d_ref, nm_ref, nv_ref):
        g = jnp.zeros((tr, c), F32)
        for s in range(N_DEV):
            g = g + jnp.where(me_ref[0] == s, own_ref[...], recv_ref[s]).astype(F32)
        g_ref[...] = g
        d_ref[...], nm_ref[...], nv_ref[...] = _adamw(w_ref[...], g, m_ref[...], v_ref[...])

    one = pl.BlockSpec((None, tr, c), lambda i, me_ref: (0, i, 0))
    return pl.pallas_call(
        body, name=name,
        grid_spec=pltpu.PrefetchScalarGridSpec(
            num_scalar_prefetch=1, grid=(r // tr,),
            in_specs=[pl.BlockSpec((N_DEV, tr, c), lambda i, me_ref: (0, i, 0)),
                      pl.BlockSpec((None, tr, c), lambda i, me_ref: (me_ref[0], i, 0)), one, one, one],
            out_specs=[one] * 4),
        out_shape=[jax.ShapeDtypeStruct((1, r, c), F32)] * 4,
        compiler_params=_params(("parallel",)),
    )(me, recv, own, w, m, v)


_SMALL = ("meta_tokens", "conv_w", "gla_w_gate2") + tuple(n for n, _ in _VEC_ROWS)


def _update_small(me, srecv, vrecv, sown, vown, w, m, v):
    n = len(_SMALL)

    def body(*refs):
        me_ref, s_ref, v_ref, so_ref, vo_ref = refs[0:5]
        w_refs, m_refs, v_refs = refs[5:5 + n], refs[5 + n:5 + 2 * n], refs[5 + 2 * n:5 + 3 * n]
        outs = refs[5 + 3 * n:]
        ssum = jnp.zeros((SMALL_PACK, 128), F32)
        vsum = jnp.zeros((VEC_ROWS, D), F32)
        for s in range(N_DEV):
            ssum = ssum + jnp.where(me_ref[0] == s, so_ref[s], s_ref[s])
            vsum = vsum + jnp.where(me_ref[0] == s, vo_ref[s], v_ref[s])
        grads = [ssum[0:N_META, :], ssum[CONV_ROW:CONV_ROW + CONV_W, 0:CONV_S], ssum[GATE_ROW:GATE_ROW + RANK, 0:GATE_S]]
        grads += [vsum[i:i + 1, 0:width] for i, (_, width) in enumerate(_VEC_ROWS)]
        for i, g in enumerate(grads):
            d, nm, nv = _adamw(w_refs[i][...], g, m_refs[i][...], v_refs[i][...])
            outs[i][...] = g
            outs[n + i][...] = d
            outs[2 * n + i][...] = nm
            outs[3 * n + i][...] = nv
        outs[4 * n][...] = vsum[LOSS_ROW:LOSS_ROW + 1, 0:128]

    shapes = [jax.ShapeDtypeStruct(t.shape, F32) for t in w]
    res = pl.pallas_call(
        body, name="update_small", out_shape=shapes * 4 + [jax.ShapeDtypeStruct((1, 128), F32)],
        in_specs=[pl.BlockSpec(memory_space=pltpu.SMEM)] + [_whole_vmem()] * (4 + 3 * n),
    )(me, srecv, vrecv, sown, vown, *w, *m, *v)
    return res[0:n], res[n:2 * n], res[2 * n:3 * n], res[3 * n:4 * n], res[4 * n]


_WEIGHTS = ("meta_tokens", "norm_mix_g", "w_in", "conv_w", "conv_b", "conv_ln_g", "conv_ln_b", "gla_w_gate2", "gla_gate_b",
            "gla_norm_g", "w_out", "norm_ffn_g", "w_ffn_gate", "w_ffn_up", "w_ffn_down", "norm_final_g")
_MATRICES = ("w_in", "w_out", "w_ffn_gate", "w_ffn_up", "w_ffn_down")
_TRANSPOSED = ("w_ffn_gate", "w_ffn_up")


def kernel(x, meta_tokens, norm_mix_g, w_in, conv_w, conv_b, conv_ln_g, conv_ln_b, gla_w_gate2, gla_gate_b, gla_norm_g, w_out, norm_ffn_g, w_ffn_gate, w_ffn_up, w_ffn_down, norm_final_g, loss_target, m_meta_tokens, m_norm_mix_g, m_w_in, m_conv_w, m_conv_b, m_conv_ln_g, m_conv_ln_b, m_gla_w_gate2, m_gla_gate_b, m_gla_norm_g, m_w_out, m_norm_ffn_g, m_w_ffn_gate, m_w_ffn_up, m_w_ffn_down, m_norm_final_g, v_meta_tokens, v_norm_mix_g, v_w_in, v_conv_w, v_conv_b, v_conv_ln_g, v_conv_ln_b, v_gla_w_gate2, v_gla_gate_b, v_gla_norm_g, v_w_out, v_norm_ffn_g, v_w_ffn_gate, v_w_ffn_up, v_w_ffn_down, v_norm_final_g):
    given = dict(locals())
    two_d = lambda a: a.reshape(1, -1) if a.ndim == 1 else a.reshape(a.shape[-2:])
    fams = [{n: given[pre + n] for n in _WEIGHTS} for pre in ("", "m_", "v_")]
    for f in fams:
        for n in _TRANSPOSED:
            f[n] = f[n].transpose(0, 2, 1)
    w = fams[0]

    lands, shards = _stage([two_d(w[n]) for n in _MATRICES], w["meta_tokens"], two_d(w["conv_w"]), two_d(w["gla_w_gate2"]))
    soon, later = (0, 1, 5), (2, 3, 4)
    pick = lambda seq, idx: [seq[i] for i in idx]
    first = _send_start("gather_first_start", pick(shards, soon), pick(lands, soon), "first", shards[0])
    ffn_first = _send_start("gather_ffn_first_start", pick(shards, later), pick(lands, later), "first", first[4])
    h0, tgt_p = _pad_rows(x, loss_target)
    _, arrived = _send_wait("gather_first_wait", *first[0:4], "first", h0)
    forward = _send_start("gather_forward_start", [], arrived, "forward", tgt_p)
    _, (a_in, a_out, a_small) = _send_wait("gather_forward_wait", *forward[0:4], "forward", forward[4])
    w_in, meta, conv_taps, w2 = _unshard_in(a_in, a_small, ffn_first[4])
    p = dict(meta=meta, conv_w=conv_taps, w2=w2, w_in=w_in, g1=norm_mix_g, conv_b=conv_b, ln_g=conv_ln_g, ln_b=conv_ln_b,
             gb=gla_gate_b, ng=gla_norm_g, g2=norm_ffn_g, g3=two_d(norm_final_g))
    passed = {}

    def pass_on(after):
        _, arrived_ffn = _send_wait("gather_ffn_first_wait", *ffn_first[0:4], "first", after)
        passed["sent"] = _send_start("gather_ffn_forward_start", [], arrived_ffn, "forward", after)
        return passed["sent"][4]

    def late_weights(after):
        _, (a_g, a_u, a_d) = _send_wait("gather_ffn_forward_wait", *passed["sent"][0:4], "forward", after)
        return a_g.reshape(D_FF, D), a_u.reshape(D_FF, D), a_d.reshape(D_FF, D)

    sent = {}

    def send_early(tag, mats):
        landing = [_in_hbm(lax.empty(m_.shape, m_.dtype)) for m_ in mats]
        sent[tag] = _send_start("scatter_" + tag + "_start", mats, landing, "scatter", mats[0])
        return sent[tag][4]

    grad_x, g = _local_step(h0, tgt_p, p, a_out.reshape(D, D), pass_on, late_weights, send_early)

    token = send_early("small", list(_pack_small(g)))
    x_, y_, c_ = _position()
    me = (4 * x_ + 2 * y_ + c_).astype(jnp.int32).reshape(1)
    res = {}
    for tag, names in (("ffn", ("w_ffn_gate", "w_ffn_up", "w_ffn_down")), ("out", ("w_out",)), ("in", ("w_in",))):
        own, recv = _send_wait("scatter_" + tag + "_wait", *sent[tag][0:4], "scatter", token)
        for n, o_, r_ in zip(names, own, recv):
            res[n] = _update_matrix(r_, o_, me, *[f[n] for f in fams], "update_" + n)
            token = res[n][1]
    (sown, vown), (srecv, vrecv) = _send_wait("scatter_small_wait", *sent["small"][0:4], "scatter", token)
    small = _update_small(me, srecv, vrecv, sown, vown, *[[two_d(f[n]) for n in _SMALL] for f in fams])
    for i, n in enumerate(_SMALL):
        res[n] = [fam[i].reshape(w[n].shape) for fam in small[0:4]]
    for n in _TRANSPOSED:
        res[n] = [t.transpose(0, 2, 1) for t in res[n]]
    outs = [small[4][0, 0], grad_x]
    for k in range(4):
        outs += [res[n][k] for n in _WEIGHTS]
    return tuple(outs)
```

```python
import functools

import jax
import jax.numpy as jnp
from jax import lax
from jax.experimental import pallas as pl
from jax.experimental.pallas import tpu as pltpu

F32 = jnp.float32
BF16 = jnp.bfloat16

D = 1024
N_META = 16
C_CONV = 512
CONV_W = 31
GLA_H = 4
GLA_DK = 64
GLA_DV = 128
GLA_K = GLA_H * GLA_DK
GLA_V = GLA_H * GLA_DV
RANK = 16
RANK_P = 128
TAU = 16.0
CHUNK = 64
LEAD = CHUNK
ZROWS = LEAD - N_META
D_IN = 2 * C_CONV + 2 * GLA_K + 2 * GLA_V + RANK
D_INP = D_IN - RANK + RANK_P
D_FF = 2816
FF_CHUNK = 1408
FF_SPLIT = (0, 1536, D_FF)
RMS_EPS = 1e-6
LN_EPS = 1e-5
N_DEV = 8

ADAM_LR = 0.001
ADAM_B1 = 0.9
ADAM_B2 = 0.999
ADAM_EPS = 1e-08
ADAM_WD = 0.01
ADAM_STEP = 10

VMEM_LIMIT = 60 * 1024 * 1024
ROW_TILE = 1056
FFN_ROW_TILE = 352
DW_ROW_TILE = 1408
MESH = pl.DeviceIdType.MESH

_NN = (((1,), (0,)), ((), ()))
_NT = (((1,), (1,)), ((), ()))
_TN = (((0,), (0,)), ((), ()))


def _dot(a, b, dims=_NN):
    return lax.dot_general(a, b, dims, preferred_element_type=F32)


def _sigmoid(x):
    return 1.0 / (1.0 + jnp.exp(-x))


def _row_tile(rows, target):
    best = None
    for t in range(16, min(rows, target) + 1, 16):
        if rows % t == 0:
            best = t
    assert best is not None, rows
    return best


def _params(sem=None):
    return pltpu.CompilerParams(dimension_semantics=sem, vmem_limit_bytes=VMEM_LIMIT)


def _whole_vmem():
    return pl.BlockSpec(memory_space=pltpu.VMEM)


def _rows(tm, width):
    return pl.BlockSpec((tm, width), lambda i: (i, 0))


def _fixed(shape):
    return pl.BlockSpec(shape, lambda *_: (0,) * len(shape))


def _fwd_inproj(h0, g1, w_in):
    rows = h0.shape[0]
    tm = _row_tile(rows, ROW_TILE)

    def body(h_ref, g_ref, w_ref, uc_ref, qk_ref, vg_ref, lr_ref, n1_ref):
        h = h_ref[...]
        r = lax.rsqrt(jnp.mean(h * h, axis=-1, keepdims=True) + RMS_EPS)
        n = (h * r * g_ref[...]).astype(BF16)
        n1_ref[...] = n
        uc_ref[...] = _dot(n, w_ref[:, 0:1024])
        qk_ref[...] = _dot(n, w_ref[:, 1024:1536])
        vg_ref[...] = _dot(n, w_ref[:, 1536:2560])
        lr_ref[...] = _dot(n, w_ref[:, 2560:2688])

    return pl.pallas_call(
        body, name="fwd_inproj", grid=(rows // tm,),
        in_specs=[_rows(tm, D), _fixed((1, D)), _whole_vmem()],
        out_specs=[_rows(tm, 1024), _rows(tm, 512), _rows(tm, 1024), _rows(tm, RANK_P), _rows(tm, D)],
        out_shape=[jax.ShapeDtypeStruct((rows, 1024), F32), jax.ShapeDtypeStruct((rows, 512), F32),
                   jax.ShapeDtypeStruct((rows, 1024), F32), jax.ShapeDtypeStruct((rows, RANK_P), F32),
                   jax.ShapeDtypeStruct((rows, D), BF16)],
        compiler_params=_params(("parallel",)),
    )(h0, g1, w_in)


def _fwd_outproj(yc, yg, h0, w_out, g2, token):
    rows = h0.shape[0]
    tm = _row_tile(rows, ROW_TILE)

    def body(yc_ref, yg_ref, h_ref, w_ref, g_ref, token_ref, h1_ref, n2_ref):
        h1 = h_ref[...] + _dot(yc_ref[...], w_ref[0:C_CONV, :]) + _dot(yg_ref[...], w_ref[C_CONV:D, :])
        h1_ref[...] = h1
        r = lax.rsqrt(jnp.mean(h1 * h1, axis=-1, keepdims=True) + RMS_EPS)
        n2_ref[...] = (h1 * r * g_ref[...]).astype(BF16)

    return pl.pallas_call(
        body, name="fwd_outproj", grid=(rows // tm,),
        in_specs=[_rows(tm, C_CONV), _rows(tm, GLA_V), _rows(tm, D), _whole_vmem(), _fixed((1, D)), _fixed((8, 128))],
        out_specs=[_rows(tm, D), _rows(tm, D)],
        out_shape=[jax.ShapeDtypeStruct((rows, D), F32), jax.ShapeDtypeStruct((rows, D), BF16)],
        compiler_params=_params(("parallel",)),
    )(yc, yg, h0, w_out, g2, token)


def _ffn_rows(h1, n2, tgt, wg, wu, wd, g2, g3, rows_per_example):
    rows = h1.shape[0]
    tm = _row_tile(rows, FFN_ROW_TILE)
    ff_blocks = [slice(lo, hi) for lo, hi in zip(FF_SPLIT[:-1], FF_SPLIT[1:])]

    def body(h1_ref, n2_ref, t_ref, wg_ref, wu_ref, wd_ref, g2_ref, g3_ref,
             f_ref, da_ref, db_ref, dh2_ref, dh1_ref, dh1b_ref, part_ref):
        i = pl.program_id(0)
        n2 = n2_ref[...]
        y2 = jnp.zeros((tm, D), F32)
        for cs in ff_blocks:
            a = _dot(n2, wg_ref[cs, :], _NT)
            b = _dot(n2, wu_ref[cs, :], _NT)
            f = (a * _sigmoid(a) * b).astype(BF16)
            f_ref[:, cs] = f
            da_ref[:, cs] = a.astype(BF16)
            db_ref[:, cs] = b.astype(BF16)
            y2 = y2 + _dot(f, wd_ref[cs, :])
        h1 = h1_ref[...]
        h2 = h1 + y2
        r3 = lax.rsqrt(jnp.mean(h2 * h2, axis=-1, keepdims=True) + RMS_EPS)
        xh3 = h2 * r3
        g3 = g3_ref[...]
        pos = (i * tm + lax.broadcasted_iota(jnp.int32, (tm, 1), 0)) % rows_per_example
        valid = pos >= LEAD
        err = jnp.where(valid, xh3 * g3 - t_ref[...], 0.0)
        loss = 0.5 / D * jnp.sum(jnp.sum(err * err, axis=-1, keepdims=True), axis=0, keepdims=True)
        dy = err * (1.0 / D)
        dg3 = jnp.sum(dy * xh3, axis=0, keepdims=True)
        dxh = dy * g3
        dh2 = r3 * (dxh - xh3 * jnp.mean(dxh * xh3, axis=-1, keepdims=True))
        dh2b = dh2.astype(BF16)
        dh2_ref[...] = dh2b
        dn2 = jnp.zeros((tm, D), F32)
        for cs in ff_blocks:
            df = _dot(dh2b, wd_ref[cs, :], _NT)
            a = da_ref[:, cs].astype(F32)
            b = db_ref[:, cs].astype(F32)
            sg = _sigmoid(a)
            da = (df * b * sg * (1.0 + a * (1.0 - sg))).astype(BF16)
            db = (df * a * sg).astype(BF16)
            da_ref[:, cs] = da
            db_ref[:, cs] = db
            dn2 = dn2 + _dot(da, wg_ref[cs, :]) + _dot(db, wu_ref[cs, :])
        r2 = lax.rsqrt(jnp.mean(h1 * h1, axis=-1, keepdims=True) + RMS_EPS)
        xh2 = h1 * r2
        dg2 = jnp.sum(dn2 * xh2, axis=0, keepdims=True)
        dxh2 = dn2 * g2_ref[...]
        dh1 = dh2 + r2 * (dxh2 - xh2 * jnp.mean(dxh2 * xh2, axis=-1, keepdims=True))
        dh1_ref[...] = dh1
        dh1b_ref[...] = dh1.astype(BF16)

        @pl.when(i == 0)
        def _():
            part_ref[...] = jnp.zeros_like(part_ref)

        part_ref[0:1, :] += dg3
        part_ref[1:2, :] += dg2
        part_ref[2:3, :] += jnp.broadcast_to(loss, (1, D))

    return pl.pallas_call(
        body, name="ffn_rows", grid=(rows // tm,),
        in_specs=[_rows(tm, D), _rows(tm, D), _rows(tm, D), _whole_vmem(), _whole_vmem(), _whole_vmem(),
                  _fixed((1, D)), _fixed((1, D))],
        out_specs=[_rows(tm, D_FF), _rows(tm, D_FF), _rows(tm, D_FF), _rows(tm, D), _rows(tm, D), _rows(tm, D),
                   _fixed((8, D))],
        out_shape=[jax.ShapeDtypeStruct((rows, D_FF), BF16)] * 3
        + [jax.ShapeDtypeStruct((rows, D), BF16), jax.ShapeDtypeStruct((rows, D), F32),
           jax.ShapeDtypeStruct((rows, D), BF16), jax.ShapeDtypeStruct((8, D), F32)],
        compiler_params=_params(("arbitrary",)),
    )(h1, n2, tgt, wg, wu, wd, g2, g3)


def _bwd_outproj(dh1b, w_out, token):
    rows = dh1b.shape[0]
    tm = _row_tile(rows, ROW_TILE)

    def body(d_ref, w_ref, token_ref, dyc_ref, dyg_ref):
        d = d_ref[...]
        dyc_ref[...] = _dot(d, w_ref[0:C_CONV, :], _NT)
        dyg_ref[...] = _dot(d, w_ref[C_CONV:D, :], _NT)

    return pl.pallas_call(
        body, name="bwd_outproj", grid=(rows // tm,),
        in_specs=[_rows(tm, D), _whole_vmem(), _fixed((8, 128))],
        out_specs=[_rows(tm, C_CONV), _rows(tm, GLA_V)],
        out_shape=[jax.ShapeDtypeStruct((rows, C_CONV), F32), jax.ShapeDtypeStruct((rows, GLA_V), F32)],
        compiler_params=_params(("parallel",)),
    )(dh1b, w_out, token)


def _bwd_inproj(duc, dqk, dvg, dlr, dh1, h0, w_in, g1, token, rows_per_example):
    rows = h0.shape[0]
    tm = _row_tile(rows_per_example, ROW_TILE)
    tiles_per_example = rows_per_example // tm

    def body(duc_ref, dqk_ref, dvg_ref, dlr_ref, dh1_ref, h_ref, w_ref, g_ref, token_ref, dh0_ref, part_ref, dmeta_ref):
        dn = (_dot(duc_ref[...], w_ref[:, 0:1024], _NT) + _dot(dqk_ref[...], w_ref[:, 1024:1536], _NT)
              + _dot(dvg_ref[...], w_ref[:, 1536:2560], _NT) + _dot(dlr_ref[...], w_ref[:, 2560:2688], _NT))
        h = h_ref[...]
        r = lax.rsqrt(jnp.mean(h * h, axis=-1, keepdims=True) + RMS_EPS)
        xh = h * r
        dg = jnp.sum(dn * xh, axis=0, keepdims=True)
        dxh = dn * g_ref[...]
        dh0 = dh1_ref[...] + r * (dxh - xh * jnp.mean(dxh * xh, axis=-1, keepdims=True))
        dh0_ref[...] = dh0
        i = pl.program_id(0)

        @pl.when(i == 0)
        def _():
            part_ref[...] = jnp.zeros_like(part_ref)
            dmeta_ref[...] = jnp.zeros_like(dmeta_ref)

        part_ref[0:1, :] += dg

        @pl.when(i % tiles_per_example == 0)
        def _():
            dmeta_ref[...] += dh0[ZROWS:LEAD, :]

    return pl.pallas_call(
        body, name="bwd_inproj", grid=(rows // tm,),
        in_specs=[_rows(tm, 1024), _rows(tm, 512), _rows(tm, 1024), _rows(tm, RANK_P), _rows(tm, D), _rows(tm, D),
                  _whole_vmem(), _fixed((1, D)), _fixed((8, 128))],
        out_specs=[_rows(tm, D), _fixed((8, D)), _fixed((N_META, D))],
        out_shape=[jax.ShapeDtypeStruct((rows, D), F32), jax.ShapeDtypeStruct((8, D), F32),
                   jax.ShapeDtypeStruct((N_META, D), F32)],
        compiler_params=_params(("arbitrary",)),
    )(duc, dqk, dvg, dlr, dh1, h0, w_in, g1, token)


def _dw_blocked(a, bs, width, name):
    rows, m = a.shape
    ws = [b.shape[1] for b in bs]
    assert sum(ws) >= N_DEV * width
    tk = _row_tile(rows, DW_ROW_TILE)
    nk = rows // tk

    def body(a_ref, *refs):
        b_refs, o_ref, acc_ref = refs[:len(bs)], refs[len(bs)], refs[len(bs) + 1]
        k = pl.program_id(0)

        @pl.when(k == 0)
        def _():
            acc_ref[...] = jnp.zeros_like(acc_ref)

        at = a_ref[...].T
        off = 0
        for b_ref, w in zip(b_refs, ws):
            acc_ref[:, off:off + w] += _dot(at, b_ref[...])
            off += w

        @pl.when(k == nk - 1)
        def _():
            for d in range(N_DEV):
                o_ref[d] = acc_ref[:, d * width:(d + 1) * width].astype(BF16)

    return pl.pallas_call(
        body, name=name, grid=(nk,),
        in_specs=[_rows(tk, m)] + [_rows(tk, w) for w in ws],
        out_specs=_fixed((N_DEV, m, width)),
        out_shape=jax.ShapeDtypeStruct((N_DEV, m, width), BF16),
        scratch_shapes=[pltpu.VMEM((m, sum(ws)), F32)],
        compiler_params=_params(("arbitrary",)),
    )(a, *bs)


def _dw_out(yc, yg, dh1b):
    rows = yc.shape[0]
    tk = _row_tile(rows, DW_ROW_TILE)
    nk = rows // tk

    def body(yc_ref, yg_ref, d_ref, o_ref, acc_ref):
        k = pl.program_id(0)

        @pl.when(k == 0)
        def _():
            acc_ref[...] = jnp.zeros_like(acc_ref)

        d = d_ref[...]
        acc_ref[0:C_CONV, :] += _dot(yc_ref[...], d, _TN)
        acc_ref[C_CONV:D, :] += _dot(yg_ref[...], d, _TN)

        @pl.when(k == nk - 1)
        def _():
            o_ref[...] = acc_ref[...].astype(BF16)

    return pl.pallas_call(
        body, name="dw_out", grid=(nk,),
        in_specs=[_rows(tk, C_CONV), _rows(tk, GLA_V), _rows(tk, D)],
        out_specs=_fixed((D, D)), out_shape=jax.ShapeDtypeStruct((D, D), BF16),
        scratch_shapes=[pltpu.VMEM((D, D), F32)],
        compiler_params=_params(("arbitrary",)),
    )(yc, yg, dh1b)


def _matmul_tn(a, b, name):
    rows, m = a.shape
    n = b.shape[1]
    tk = _row_tile(rows, DW_ROW_TILE)
    tn = n if n <= 1024 else FF_CHUNK
    tm_ = m if m <= 1024 else FF_CHUNK
    assert n % tn == 0 and m % tm_ == 0
    nk = rows // tk

    def body(a_ref, b_ref, o_ref, acc_ref):
        k = pl.program_id(2)

        @pl.when(k == 0)
        def _():
            acc_ref[...] = jnp.zeros_like(acc_ref)

        acc_ref[...] += _dot(a_ref[...], b_ref[...], _TN)

        @pl.when(k == nk - 1)
        def _():
            o_ref[...] = acc_ref[...].astype(BF16)

    return pl.pallas_call(
        body, name=name, grid=(m // tm_, n // tn, nk),
        in_specs=[pl.BlockSpec((tk, tm_), lambda i, j, k: (k, i)), pl.BlockSpec((tk, tn), lambda i, j, k: (k, j))],
        out_specs=pl.BlockSpec((tm_, tn), lambda i, j, k: (i, j)),
        out_shape=jax.ShapeDtypeStruct((m, n), BF16),
        scratch_shapes=[pltpu.VMEM((tm_, tn), F32)],
        compiler_params=_params(("parallel", "parallel", "arbitrary")),
    )(a, b)


HALO = 32
LANES = 128


def _shifted(win, offsets):
    for r in range(8):
        js = [j for j, k in enumerate(offsets) if k % 8 == r]
        if js:
            rolled = win if r == 0 else pltpu.roll(win, CHUNK + HALO - r, 0)
            for j in js:
                yield j, rolled[offsets[j] - r:offsets[j] - r + CHUNK]


def _glu_into(uc_ref, vs_ref, n_chunk):
    vs_ref[0:CHUNK, :] = jnp.zeros((CHUNK, C_CONV), F32)

    def glu(i, carry):
        base = pl.multiple_of(i * CHUNK, CHUNK)
        val = uc_ref[pl.ds(base, CHUNK), 0:C_CONV]
        gate = uc_ref[pl.ds(base, CHUNK), C_CONV:2 * C_CONV]
        vs_ref[pl.ds(base + CHUNK, CHUNK), :] = val * _sigmoid(gate)
        return carry

    lax.fori_loop(0, n_chunk, glu, 0)


def _fwd_conv(uc, conv_w, conv_b, ln_g, ln_b, token, n_ex):
    rows = uc.shape[0]
    lp = rows // n_ex
    n_chunk = lp // CHUNK

    def body(uc_ref, w_ref, b_ref, lg_ref, lb_ref, token_ref, ypre_ref, yc_ref, vs_ref):
        _glu_into(uc_ref, vs_ref, n_chunk)

        def conv(i, carry):
            base = pl.multiple_of(i * CHUNK, CHUNK)
            for lb in range(C_CONV // LANES):
                ls = slice(lb * LANES, (lb + 1) * LANES)
                win = vs_ref[pl.ds(base + CHUNK - HALO, CHUNK + HALO), ls]
                acc = jnp.broadcast_to(b_ref[:, ls], (CHUNK, LANES))
                for j, rows_j in _shifted(win, [HALO - (CONV_W - 1) + j for j in range(CONV_W)]):
                    acc = acc + w_ref[j:j + 1, ls] * rows_j
                ypre_ref[pl.ds(base, CHUNK), ls] = acc
            y = ypre_ref[pl.ds(base, CHUNK), :]
            mu = jnp.mean(y, axis=-1, keepdims=True)
            yc_ = y - mu
            rstd = lax.rsqrt(jnp.mean(yc_ * yc_, axis=-1, keepdims=True) + LN_EPS)
            s = yc_ * rstd * lg_ref[...] + lb_ref[...]
            yc_ref[pl.ds(base, CHUNK), :] = (s * _sigmoid(s)).astype(BF16)
            return carry

        lax.fori_loop(0, n_chunk, conv, 0)

    ex = lambda w: pl.BlockSpec((lp, w), lambda b: (b, 0))
    return pl.pallas_call(
        body, name="fwd_conv", grid=(n_ex,),
        in_specs=[ex(2 * C_CONV), _fixed((32, C_CONV)), _fixed((1, C_CONV)), _fixed((1, C_CONV)), _fixed((1, C_CONV)),
                  _fixed((8, 128))],
        out_specs=[ex(C_CONV), ex(C_CONV)],
        out_shape=[jax.ShapeDtypeStruct((rows, C_CONV), F32), jax.ShapeDtypeStruct((rows, C_CONV), BF16)],
        scratch_shapes=[pltpu.VMEM((lp + CHUNK, C_CONV), F32)],
        compiler_params=_params(("parallel",)),
    )(uc, conv_w, conv_b, ln_g, ln_b, token)


def _bwd_conv(uc, ypre, dyc, conv_w, ln_g, ln_b, token, n_ex):
    rows = uc.shape[0]
    lp = rows // n_ex
    n_chunk = lp // CHUNK

    def body(uc_ref, ypre_ref, dyc_ref, w_ref, lg_ref, lb_ref, token_ref, duc_ref, dw_ref, dvec_ref, vs_ref, dys_ref,
             dwacc_ref):
        _glu_into(uc_ref, vs_ref, n_chunk)
        dys_ref[pl.ds(lp, CHUNK), :] = jnp.zeros((CHUNK, C_CONV), F32)
        dwacc_ref[...] = jnp.zeros_like(dwacc_ref)

        def ln_bwd(i, carry):
            dcb, dlg, dlb = carry
            base = pl.multiple_of(i * CHUNK, CHUNK)
            y = ypre_ref[pl.ds(base, CHUNK), :]
            mu = jnp.mean(y, axis=-1, keepdims=True)
            yc_ = y - mu
            rstd = lax.rsqrt(jnp.mean(yc_ * yc_, axis=-1, keepdims=True) + LN_EPS)
            xh = yc_ * rstd
            s = xh * lg_ref[...] + lb_ref[...]
            sg = _sigmoid(s)
            ds = dyc_ref[pl.ds(base, CHUNK), :] * (sg * (1.0 + s * (1.0 - sg)))
            dxh = ds * lg_ref[...]
            dy = rstd * (dxh - jnp.mean(dxh, axis=-1, keepdims=True) - xh * jnp.mean(dxh * xh, axis=-1, keepdims=True))
            dys_ref[pl.ds(base, CHUNK), :] = dy
            return (dcb + jnp.sum(dy, axis=0, keepdims=True), dlg + jnp.sum(ds * xh, axis=0, keepdims=True),
                    dlb + jnp.sum(ds, axis=0, keepdims=True))

        zero = jnp.zeros((1, C_CONV), F32)
        dcb, dlg, dlb = lax.fori_loop(0, n_chunk, ln_bwd, (zero, zero, zero))

        @pl.when(pl.program_id(0) == 0)
        def _():
            dvec_ref[...] = jnp.zeros_like(dvec_ref)
            dw_ref[...] = jnp.zeros_like(dw_ref)

        dvec_ref[0:1, :] += dcb
        dvec_ref[1:2, :] += dlg
        dvec_ref[2:3, :] += dlb

        def taps(i, carry):
            base = pl.multiple_of(i * CHUNK, CHUNK)
            for lb in range(C_CONV // LANES):
                ls = slice(lb * LANES, (lb + 1) * LANES)
                dwin = dys_ref[pl.ds(base, CHUNK + HALO), ls]
                vwin = vs_ref[pl.ds(base + CHUNK - HALO, CHUNK + HALO), ls]
                dy = dwin[0:CHUNK]
                acc = jnp.zeros((CHUNK, LANES), F32)
                for j, rows_j in _shifted(dwin, [CONV_W - 1 - j for j in range(CONV_W)]):
                    acc = acc + w_ref[j:j + 1, ls] * rows_j
                for j, rows_j in _shifted(vwin, [HALO - (CONV_W - 1) + j for j in range(CONV_W)]):
                    dwacc_ref[8 * j:8 * j + 8, ls] += jnp.sum((dy * rows_j).reshape(CHUNK // 8, 8, LANES), axis=0)
                val = uc_ref[pl.ds(base, CHUNK), ls]
                gate = uc_ref[pl.ds(base, CHUNK), C_CONV + lb * LANES:C_CONV + (lb + 1) * LANES]
                sg = _sigmoid(gate)
                duc_ref[pl.ds(base, CHUNK), ls] = (acc * sg).astype(BF16)
                duc_ref[pl.ds(base, CHUNK), C_CONV + lb * LANES:C_CONV + (lb + 1) * LANES] = (
                    acc * val * sg * (1.0 - sg)).astype(BF16)
            return carry

        lax.fori_loop(0, n_chunk, taps, 0)
        for j in range(CONV_W):
            dw_ref[j:j + 1, :] += jnp.sum(dwacc_ref[8 * j:8 * j + 8, :], axis=0, keepdims=True)

    ex = lambda w: pl.BlockSpec((lp, w), lambda b: (b, 0))
    return pl.pallas_call(
        body, name="bwd_conv", grid=(n_ex,),
        in_specs=[ex(2 * C_CONV), ex(C_CONV), ex(C_CONV), _fixed((32, C_CONV)), _fixed((1, C_CONV)), _fixed((1, C_CONV)),
                  _fixed((8, 128))],
        out_specs=[ex(2 * C_CONV), _fixed((32, C_CONV)), _fixed((8, C_CONV))],
        out_shape=[jax.ShapeDtypeStruct((rows, 2 * C_CONV), BF16), jax.ShapeDtypeStruct((32, C_CONV), F32),
                   jax.ShapeDtypeStruct((8, C_CONV), F32)],
        scratch_shapes=[pltpu.VMEM((lp + CHUNK, C_CONV), F32), pltpu.VMEM((lp + CHUNK, C_CONV), F32),
                        pltpu.VMEM((8 * 32, C_CONV), F32)],
        compiler_params=_params(("arbitrary",)),
    )(uc, ypre, dyc, conv_w, ln_g, ln_b, token)


def _seg_chunks(n_chunk):
    return max(c for c in (11, 3, 1) if n_chunk % c == 0)


def _block_mask(shape, row_block, lane_block):
    return (lax.broadcasted_iota(jnp.int32, shape, 0) // row_block) == (lax.broadcasted_iota(jnp.int32, shape, 1) // lane_block)


def _per_head_rows(x, mask):
    return jnp.where(mask, jnp.concatenate([x] * GLA_H, axis=0), 0)


def _fold_heads(full, lane_block):
    lane = lax.broadcasted_iota(jnp.int32, (1, full.shape[1]), 1) // lane_block
    out = jnp.where(lane == 0, full[0:CHUNK], 0.0)
    for h in range(1, GLA_H):
        out = out + jnp.where(lane == h, full[h * CHUNK:(h + 1) * CHUNK], 0.0)
    return out


def _causal_heads():
    return (lax.broadcasted_iota(jnp.int32, (CHUNK, GLA_H * CHUNK), 1) % CHUNK) <= lax.broadcasted_iota(
        jnp.int32, (CHUNK, GLA_H * CHUNK), 0)


def _cumsum_rows(x):
    row = lax.broadcasted_iota(jnp.int32, x.shape, 0)
    s = 1
    while s < CHUNK:
        x = x + jnp.where(row >= s, pltpu.roll(x, s, 0), 0.0)
        s *= 2
    return x


def _rev_cumsum_rows(x):
    row = lax.broadcasted_iota(jnp.int32, x.shape, 0)
    s = 1
    while s < CHUNK:
        x = x + jnp.where(row < CHUNK - s, pltpu.roll(x, CHUNK - s, 0), 0.0)
        s *= 2
    return x


def _gate_terms(lr_ref, w2_ref, gb_ref, rs, first_pos):
    z = _dot(lr_ref[rs, :].astype(BF16), w2_ref[...]) + gb_ref[...]
    la = (jnp.minimum(z, 0.0) - jnp.log(1.0 + jnp.exp(-jnp.abs(z)))) * (1.0 / TAU)
    pos = first_pos + lax.broadcasted_iota(jnp.int32, (CHUNK, 1), 0)
    live = pos >= ZROWS
    la = jnp.where(live, la, 0.0)
    return z, live, _cumsum_rows(la)


def _fwd_gla(qk, vg, lr, w2p, gb, ng, n_ex):
    rows = qk.shape[0]
    lp = rows // n_ex
    n_chunk = lp // CHUNK
    sc = _seg_chunks(n_chunk)
    n_seg = n_chunk // sc
    seg = sc * CHUNK

    def body(qk_ref, vg_ref, lr_ref, w2_ref, gb_ref, ng_ref, yg_ref, o_ref, st_ref, state_ref):
        sidx = pl.program_id(1)

        @pl.when(sidx == 0)
        def _():
            state_ref[...] = jnp.zeros_like(state_ref)

        causal = _causal_heads()
        k_mask = _block_mask((GLA_H * CHUNK, GLA_K), CHUNK, GLA_DK)
        v_mask = _block_mask((GLA_H * CHUNK, GLA_V), CHUNK, GLA_DV)
        s_mask = _block_mask((GLA_V, GLA_K), GLA_DV, GLA_DK)

        def chunk(ci, carry):
            base = pl.multiple_of(ci * CHUNK, CHUNK)
            rs = pl.ds(base, CHUNK)
            _, _, bcum = _gate_terms(lr_ref, w2_ref, gb_ref, rs, (sidx * sc + ci) * CHUNK)
            bl = bcum[CHUNK - 1:CHUNK, :]
            q = qk_ref[rs, 0:GLA_K]
            k = qk_ref[rs, GLA_K:2 * GLA_K]
            qt = (q * (GLA_DK ** -0.5) * jnp.exp(bcum)).astype(BF16)
            kt = (k * jnp.exp(-bcum)).astype(BF16)
            kh = (k * jnp.exp(bl - bcum)).astype(BF16)
            vb = vg_ref[rs, 0:GLA_V].astype(BF16)
            state = state_ref[...]
            st_ref[ci] = state
            a = jnp.where(causal, _dot(qt, _per_head_rows(kt, k_mask), _NT), 0.0)
            o = _dot(a.astype(BF16), _per_head_rows(vb, v_mask)) + _dot(qt, state.astype(BF16), _NT)
            o_ref[rs, :] = o
            for h in range(GLA_H):
                hs = slice(h * GLA_DV, (h + 1) * GLA_DV)
                oh = o[:, hs]
                ro = lax.rsqrt(jnp.mean(oh * oh, axis=-1, keepdims=True) + RMS_EPS)
                g = vg_ref[rs, GLA_V + h * GLA_DV:GLA_V + (h + 1) * GLA_DV]
                yg_ref[rs, hs] = (oh * ro * ng_ref[...] * g * _sigmoid(g)).astype(BF16)
            state_ref[...] = state * jnp.exp(bl) + jnp.where(s_mask, _dot(vb, kh, _TN), 0.0)
            return carry

        lax.fori_loop(0, sc, chunk, 0)

    sg = lambda w: pl.BlockSpec((seg, w), lambda b, s: (b * n_seg + s, 0))
    return pl.pallas_call(
        body, name="fwd_gla", grid=(n_ex, n_seg),
        in_specs=[sg(2 * GLA_K), sg(2 * GLA_V), sg(RANK_P), _fixed((RANK_P, GLA_K)), _fixed((1, GLA_K)), _fixed((1, GLA_DV))],
        out_specs=[sg(GLA_V), sg(GLA_V), pl.BlockSpec((sc, GLA_V, GLA_K), lambda b, s: (b * n_seg + s, 0, 0))],
        out_shape=[jax.ShapeDtypeStruct((rows, GLA_V), BF16), jax.ShapeDtypeStruct((rows, GLA_V), F32),
                   jax.ShapeDtypeStruct((n_ex * n_chunk, GLA_V, GLA_K), F32)],
        scratch_shapes=[pltpu.VMEM((GLA_V, GLA_K), F32)],
        compiler_params=_params(("parallel", "arbitrary")),
    )(qk, vg, lr, w2p, gb, ng)


def _bwd_gla(qk, vg, lr, o, st, dyg, w2p, gb, ng, n_ex):
    rows = qk.shape[0]
    lp = rows // n_ex
    n_chunk = lp // CHUNK
    sc = _seg_chunks(n_chunk)
    n_seg = n_chunk // sc
    seg = sc * CHUNK

    def body(qk_ref, vg_ref, lr_ref, o_ref, st_ref, dyg_ref, w2_ref, gb_ref, ng_ref,
             dqk_ref, dvg_ref, dlr_ref, dw2_ref, dvec_ref, gt_ref, dz_ref):
        step = pl.program_id(1)
        sidx = n_seg - 1 - step

        @pl.when(step == 0)
        def _():
            gt_ref[...] = jnp.zeros_like(gt_ref)

        @pl.when((step == 0) & (pl.program_id(0) == 0))
        def _():
            dw2_ref[...] = jnp.zeros_like(dw2_ref)
            dvec_ref[...] = jnp.zeros_like(dvec_ref)

        causal = _causal_heads()
        k_mask = _block_mask((GLA_H * CHUNK, GLA_K), CHUNK, GLA_DK)
        v_mask = _block_mask((GLA_H * CHUNK, GLA_V), CHUNK, GLA_DV)
        s_mask = _block_mask((GLA_V, GLA_K), GLA_DV, GLA_DK)
        last_row = lax.broadcasted_iota(jnp.int32, (CHUNK, 1), 0) == CHUNK - 1
        ng = ng_ref[...]

        def chunk(ii, dng):
            ci = sc - 1 - ii
            base = pl.multiple_of(ci * CHUNK, CHUNK)
            rs = pl.ds(base, CHUNK)
            z, live, bcum = _gate_terms(lr_ref, w2_ref, gb_ref, rs, (sidx * sc + ci) * CHUNK)
            bl = bcum[CHUNK - 1:CHUNK, :]
            ebl = jnp.exp(bl)
            q = qk_ref[rs, 0:GLA_K]
            k = qk_ref[rs, GLA_K:2 * GLA_K]
            eb = jnp.exp(bcum)
            enb = jnp.exp(-bcum)
            ehb = jnp.exp(bl - bcum)
            qt = q * (GLA_DK ** -0.5) * eb
            kt = k * enb
            kh = k * ehb
            qtb = qt.astype(BF16)
            vb = vg_ref[rs, 0:GLA_V].astype(BF16)
            k_rows = _per_head_rows(kt.astype(BF16), k_mask)
            v_rows = _per_head_rows(vb, v_mask)
            gt = gt_ref[...]
            gtb = gt.astype(BF16)
            s_in = st_ref[ci]
            dos = []
            for h in range(GLA_H):
                hs = slice(h * GLA_DV, (h + 1) * GLA_DV)
                gs = slice(GLA_V + h * GLA_DV, GLA_V + (h + 1) * GLA_DV)
                oh = o_ref[rs, hs]
                ro = lax.rsqrt(jnp.mean(oh * oh, axis=-1, keepdims=True) + RMS_EPS)
                on = oh * ro
                g = vg_ref[rs, gs]
                sg = _sigmoid(g)
                dout = dyg_ref[rs, hs]
                dvg_ref[rs, gs] = (dout * on * ng * (sg * (1.0 + g * (1.0 - sg)))).astype(BF16)
                dw = dout * g * sg
                dng = dng + jnp.sum(dw * on, axis=0, keepdims=True)
                don = dw * ng
                dos.append((ro * (don - on * jnp.mean(don * on, axis=-1, keepdims=True))).astype(BF16))
            dob = jnp.concatenate(dos, axis=1)
            a = jnp.where(causal, _dot(qtb, k_rows, _NT), 0.0).astype(BF16)
            da = jnp.where(causal, _dot(dob, v_rows, _NT), 0.0).astype(BF16)
            dv = _fold_heads(_dot(a, dob, _TN), GLA_DV) + _dot(kh.astype(BF16), gtb, _NT)
            dvg_ref[rs, 0:GLA_V] = dv.astype(BF16)
            dkh = _dot(vb, gtb)
            dqt = _dot(da, k_rows) + _dot(dob, s_in.astype(BF16))
            dkt = _fold_heads(_dot(da, qtb, _TN), GLA_DK)
            dbl = jnp.sum(gt * s_in, axis=0, keepdims=True) * ebl + jnp.sum(dkh * kh, axis=0, keepdims=True)
            dqk_ref[rs, 0:GLA_K] = (dqt * (GLA_DK ** -0.5) * eb).astype(BF16)
            dqk_ref[rs, GLA_K:2 * GLA_K] = (dkt * enb + dkh * ehb).astype(BF16)
            db = dqt * qt - dkt * kt - dkh * kh
            db = jnp.where(last_row, db + dbl, db)
            dla = jnp.where(live, _rev_cumsum_rows(db), 0.0)
            dz_ref[rs, :] = dla * (1.0 / TAU) * (1.0 - _sigmoid(z))
            gt_ref[...] = jnp.where(s_mask, _dot(dob, qtb, _TN), 0.0) + gt * ebl
            return dng

        dng = lax.fori_loop(0, sc, chunk, jnp.zeros((1, GLA_DV), F32))
        dz = dz_ref[...]
        dzb = dz.astype(BF16)
        dlr_ref[...] = _dot(dzb, w2_ref[...], _NT).astype(BF16)
        dw2_ref[...] += _dot(lr_ref[...].astype(BF16), dzb, _TN)
        dvec_ref[0:1, :] += jnp.sum(dz, axis=0, keepdims=True)
        dvec_ref[1:2, 0:GLA_DV] += dng

    sg_ = lambda w: pl.BlockSpec((seg, w), lambda b, s: (b * n_seg + n_seg - 1 - s, 0))
    return pl.pallas_call(
        body, name="bwd_gla", grid=(n_ex, n_seg),
        in_specs=[sg_(2 * GLA_K), sg_(2 * GLA_V), sg_(RANK_P), sg_(GLA_V),
                  pl.BlockSpec((sc, GLA_V, GLA_K), lambda b, s: (b * n_seg + n_seg - 1 - s, 0, 0)), sg_(GLA_V),
                  _fixed((RANK_P, GLA_K)), _fixed((1, GLA_K)), _fixed((1, GLA_DV))],
        out_specs=[sg_(2 * GLA_K), sg_(2 * GLA_V), sg_(RANK_P), _fixed((RANK_P, GLA_K)), _fixed((8, GLA_K))],
        out_shape=[jax.ShapeDtypeStruct((rows, 2 * GLA_K), BF16), jax.ShapeDtypeStruct((rows, 2 * GLA_V), BF16),
                   jax.ShapeDtypeStruct((rows, RANK_P), BF16), jax.ShapeDtypeStruct((RANK_P, GLA_K), F32),
                   jax.ShapeDtypeStruct((8, GLA_K), F32)],
        scratch_shapes=[pltpu.VMEM((GLA_V, GLA_K), F32), pltpu.VMEM((seg, GLA_K), F32)],
        compiler_params=_params(("arbitrary", "arbitrary")),
    )(qk, vg, lr, o, st, dyg, w2p, gb, ng)


def _pad_rows(x, tgt):
    both = jnp.pad(jnp.stack([x, tgt]), ((0, 0), (0, 0), (LEAD, 0), (0, 0)))
    return both[0], both[1]


def _local_step(h0, tgt_p, p, pass_on, late_weights, send_early):
    n_ex, lp, _ = h0.shape
    rows = n_ex * lp
    meta = jnp.broadcast_to(p["meta"][None], (n_ex, N_META, D))
    h0 = lax.dynamic_update_slice(h0, meta, (0, ZROWS, 0)).reshape(rows, D)
    tgt_p = tgt_p.reshape(rows, D)

    uc, qk, vg, lr, n1 = _fwd_inproj(h0, p["g1"], p["w_in"])
    yg, o, st = _fwd_gla(qk, vg, lr, p["w2"], p["gb"], p["ng"], n_ex)
    token = pass_on(yg)
    ypre, yc = _fwd_conv(uc, p["conv_w"], p["conv_b"], p["ln_g"], p["ln_b"], token, n_ex)
    w_out, wg, wu, wd = late_weights(yc)
    h1, n2 = _fwd_outproj(yc, yg, h0, w_out, p["g2"], token)
    f, da, db, dh2, dh1, dh1b, part = _ffn_rows(h1, n2, tgt_p, wg, wu, wd, p["g2"], p["g3"], lp)
    g = {}
    token = send_early("ffn", [_matmul_tn(a_, b_, name).reshape(N_DEV, FF_S, D) for a_, b_, name in (
        (da, n2, "dw_gate"), (db, n2, "dw_up"), (f, dh2, "dw_down"))])
    dyc, dyg = _bwd_outproj(dh1b, w_out, token)
    token = send_early("out", [_dw_out(yc, yg, dh1b).reshape(N_DEV, W_OUT_S, D)])
    duc, g["conv_w"], g["conv_vec"] = _bwd_conv(uc, ypre, dyc, p["conv_w"], p["ln_g"], p["ln_b"], token, n_ex)
    dqk, dvg, dlr, g["w2"], g["gla_vec"] = _bwd_gla(qk, vg, lr, o, st, dyg, p["w2"], p["gb"], p["ng"], n_ex)
    token = send_early("in", [_dw_blocked(n1, [duc, dqk, dvg, dlr], W_IN_S, "dw_in")])
    dh0, g["in_vec"], g["meta"] = _bwd_inproj(duc, dqk, dvg, dlr, dh1, h0, p["w_in"], p["g1"], token, lp)
    g["ffn_vec"] = part
    return dh0.reshape(n_ex, lp, D)[:, LEAD:], g


W_IN_S = D_IN // N_DEV
W_OUT_S = D // N_DEV
FF_S = D_FF // N_DEV
CONV_S = C_CONV // N_DEV
GATE_S = GLA_K // N_DEV
SMALL_PACK = 64
CONV_ROW = 16
GATE_ROW = 48
VEC_ROWS = 16
_VEC_ROWS = (("norm_mix_g", D), ("conv_b", C_CONV), ("conv_ln_g", C_CONV), ("conv_ln_b", C_CONV), ("gla_gate_b", GLA_K),
             ("gla_norm_g", GLA_DV), ("norm_ffn_g", D), ("norm_final_g", D))
LOSS_ROW = len(_VEC_ROWS)


def _position():
    return lax.axis_index("x"), lax.axis_index("y"), lax.axis_index("c")


def _any():
    return pl.BlockSpec(memory_space=pl.ANY)


def _stage(mats, meta, conv_w, w2):
    n_t = len(mats) + 1

    def body(*refs):
        ins = refs[0:n_t - 1]
        meta_ref, cw_ref, w2_ref = refs[n_t - 1:n_t + 2]
        lands = refs[n_t + 2:2 * n_t + 2]
        shards = refs[2 * n_t + 2:3 * n_t + 2]
        sems = refs[3 * n_t + 2]
        for s_ref, w_ref in zip(shards, ins):
            s_ref[...] = w_ref[...].astype(BF16)
        sp = shards[n_t - 1]
        sp[...] = jnp.zeros_like(sp)
        sp[0:N_META, :] = meta_ref[...]
        sp[CONV_ROW:CONV_ROW + CONV_W, 0:CONV_S] = cw_ref[...]
        sp[GATE_ROW:GATE_ROW + RANK, 0:GATE_S] = w2_ref[...]
        x, y, c = _position()
        mine = [pltpu.make_async_copy(shards[t], lands[t].at[4 * x + 2 * y + c], sems.at[t]) for t in range(n_t)]
        for cp in mine:
            cp.start()
        for cp in mine:
            cp.wait()

    shard_shapes = [jax.ShapeDtypeStruct(m.shape, BF16) for m in mats] + [jax.ShapeDtypeStruct((SMALL_PACK, 128), F32)]
    res = pl.pallas_call(
        body, name="stage",
        out_shape=[jax.ShapeDtypeStruct((N_DEV,) + s.shape, s.dtype) for s in shard_shapes] + shard_shapes,
        in_specs=[_whole_vmem()] * (n_t + 2), out_specs=[_any()] * n_t + [_whole_vmem()] * n_t,
        scratch_shapes=[pltpu.SemaphoreType.DMA((n_t,))],
        compiler_params=pltpu.CompilerParams(vmem_limit_bytes=VMEM_LIMIT),
    )(*mats, meta, conv_w, w2)
    return res[0:n_t], res[n_t:]


_HBM = pl.BlockSpec(memory_space=pltpu.HBM)
_SEM = pl.BlockSpec(memory_space=pltpu.SEMAPHORE)
_EFFECT = pltpu.SideEffectType.DATAFLOW_SIDE_EFFECTING


_N_ROUTES = {"scatter": 7, "first": 4, "forward": 3}


def _routes(mode):
    x, y, c = _position()
    me = 4 * x + 2 * y + c
    if mode == "scatter":
        out = []
        for k in range(1, N_DEV):
            px = 1 - x if k & 4 else x
            py = 1 - y if k & 2 else y
            pc = 1 - c if k & 1 else c
            out.append(((px, py, pc), 4 * px + 2 * py + pc, me))
        return out
    if mode == "first":
        return [(pos, None, me) for pos in ((x, y, 1 - c), (1 - x, y, c), (x, 1 - y, c), (1 - x, 1 - y, c))]
    assert mode == "forward"
    return [((x, y, 1 - c), 4 * px + 2 * py + c, 4 * px + 2 * py + c) for px, py in ((1 - x, y), (x, 1 - y), (1 - x, 1 - y))]


def _route_copies(mode, n, src_refs, land_refs, send_sems, recv_sems):
    nr = _N_ROUTES[mode]
    for i, (pos, src_blk, dst_blk) in enumerate(_routes(mode)):
        for t in range(n):
            src = land_refs[t] if mode == "forward" else src_refs[t]
            yield pltpu.make_async_remote_copy(
                src_ref=src if src_blk is None else src.at[src_blk], dst_ref=land_refs[t].at[dst_blk],
                send_sem=send_sems.at[nr * t + i], recv_sem=recv_sems.at[nr * t + i], device_id=pos, device_id_type=MESH)


def _in_hbm(a):
    return pltpu.with_memory_space_constraint(a, pltpu.HBM)


def _send_start(name, srcs, lands, mode, after):
    n, ns = len(lands), len(srcs)
    nsem = _N_ROUTES[mode] * n

    def body(*refs):
        src_refs, land_refs = refs[0:ns], refs[ns:ns + n]
        send_sems, recv_sems = refs[ns + n + 1:ns + n + 3]
        token = refs[2 * (ns + n) + 3]
        for cp in _route_copies(mode, n, src_refs, land_refs, send_sems, recv_sems):
            cp.start()
        token[...] = jnp.zeros_like(token)

    bufs = list(srcs) + list(lands)
    res = pl.pallas_call(
        body, name=name,
        out_shape=(pltpu.SemaphoreType.DMA((nsem,)), pltpu.SemaphoreType.DMA((nsem,)),
                   *[pltpu.HBM(b.shape, b.dtype) for b in bufs], jax.ShapeDtypeStruct((8, 128), F32)),
        in_specs=[_HBM] * len(bufs) + [_any()], out_specs=(_SEM, _SEM, *[_HBM] * len(bufs), _whole_vmem()),
        input_output_aliases={i: 2 + i for i in range(len(bufs))},
        compiler_params=pltpu.CompilerParams(has_side_effects=_EFFECT),
    )(*[_in_hbm(b) for b in bufs], after)
    return res[0], res[1], res[2:2 + ns], res[2 + ns:2 + ns + n], res[2 + ns + n]


def _send_wait(name, send_sems, recv_sems, srcs, lands, mode, after):
    n, ns = len(lands), len(srcs)

    def body(*refs):
        src_refs, land_refs = refs[0:ns], refs[ns:ns + n]
        send_sems, recv_sems = refs[ns + n:ns + n + 2]
        for cp in _route_copies(mode, n, src_refs, land_refs, send_sems, recv_sems):
            cp.wait_send()
            cp.wait_recv()

    bufs = list(srcs) + list(lands)
    res = pl.pallas_call(
        body, name=name,
        out_shape=tuple(pltpu.HBM(b.shape, b.dtype) for b in bufs),
        in_specs=[_HBM] * len(bufs) + [_SEM, _SEM, _any()], out_specs=tuple([_HBM] * len(bufs)),
        input_output_aliases={i: i for i in range(len(bufs))},
        compiler_params=pltpu.CompilerParams(has_side_effects=_EFFECT),
    )(*bufs, send_sems, recv_sems, after)
    return res[0:ns], res[ns:ns + n]


def _unshard_in(a_in, a_small, token):
    def body(a_ref, s_ref, token_ref, w_ref, meta_ref, cw_ref, w2_ref):
        w_ref[:, D_IN:D_INP] = jnp.zeros((D, D_INP - D_IN), BF16)
        w2_ref[...] = jnp.zeros_like(w2_ref)
        for d in range(N_DEV):
            w_ref[:, d * W_IN_S:(d + 1) * W_IN_S] = a_ref[d]
            meta_ref[:, d * 128:(d + 1) * 128] = s_ref[d, 0:N_META, :]
            cw_ref[:, d * CONV_S:(d + 1) * CONV_S] = s_ref[d, CONV_ROW:CONV_ROW + 32, 0:CONV_S]
            w2_ref[0:RANK, d * GATE_S:(d + 1) * GATE_S] = s_ref[d, GATE_ROW:GATE_ROW + RANK, 0:GATE_S].astype(BF16)

    return pl.pallas_call(
        body, name="unshard_in",
        out_shape=[jax.ShapeDtypeStruct((D, D_INP), BF16), jax.ShapeDtypeStruct((N_META, D), F32),
                   jax.ShapeDtypeStruct((32, C_CONV), F32), jax.ShapeDtypeStruct((RANK_P, GLA_K), BF16)],
        compiler_params=pltpu.CompilerParams(vmem_limit_bytes=VMEM_LIMIT),
    )(a_in, a_small, token)


def _pack_small(g):
    def body(meta_ref, cw_ref, w2_ref, in_vec, ffn_vec, conv_vec, gla_vec, sp, vp):
        sp[...] = jnp.zeros_like(sp)
        vp[...] = jnp.zeros_like(vp)
        for d in range(N_DEV):
            sp[d, 0:N_META, :] = meta_ref[:, d * 128:(d + 1) * 128]
            sp[d, CONV_ROW:CONV_ROW + 32, 0:CONV_S] = cw_ref[:, d * CONV_S:(d + 1) * CONV_S]
            sp[d, GATE_ROW:GATE_ROW + RANK, 0:GATE_S] = w2_ref[0:RANK, d * GATE_S:(d + 1) * GATE_S]
            vp[d, 0:1, :] = in_vec[0:1, :]
            vp[d, 1:4, 0:C_CONV] = conv_vec[0:3, :]
            vp[d, 4:5, 0:GLA_K] = gla_vec[0:1, :]
            vp[d, 5:6, 0:GLA_DV] = gla_vec[1:2, 0:GLA_DV]
            vp[d, 6:7, :] = ffn_vec[1:2, :]
            vp[d, 7:8, :] = ffn_vec[0:1, :]
            vp[d, LOSS_ROW:LOSS_ROW + 1, :] = ffn_vec[2:3, :]

    return pl.pallas_call(
        body, name="pack_small",
        out_shape=[jax.ShapeDtypeStruct((N_DEV, SMALL_PACK, 128), F32), jax.ShapeDtypeStruct((N_DEV, VEC_ROWS, D), F32)],
    )(g["meta"], g["conv_w"], g["w2"], g["in_vec"], g["ffn_vec"], g["conv_vec"], g["gla_vec"])


def _adamw(w, g, m, v):
    m = ADAM_B1 * m + (1.0 - ADAM_B1) * g
    v = ADAM_B2 * v + (1.0 - ADAM_B2) * (g * g)
    m_hat = m / (1.0 - ADAM_B1 ** ADAM_STEP)
    v_hat = v / (1.0 - ADAM_B2 ** ADAM_STEP)
    return -ADAM_LR * (m_hat / (jnp.sqrt(v_hat) + ADAM_EPS) + ADAM_WD * w), m, v


def _update_matrix(recv, own, me, w, m, v, name):
    _, r, c = recv.shape
    tr = _row_tile(r, 256)

    def body(me_ref, recv_ref, own_ref, w_ref, m_ref, v_ref, g_ref, d_ref, nm_ref, nv_ref):
        g = jnp.zeros((tr, c), F32)
        for s in range(N_DEV):
            g = g + jnp.where(me_ref[0] == s, own_ref[...], recv_ref[s]).astype(F32)
        g_ref[...] = g
        d_ref[...], nm_ref[...], nv_ref[...] = _adamw(w_ref[...], g, m_ref[...], v_ref[...])

    one = pl.BlockSpec((None, tr, c), lambda i, me_ref: (0, i, 0))
    return pl.pallas_call(
        body, name=name,
        grid_spec=pltpu.PrefetchScalarGridSpec(
            num_scalar_prefetch=1, grid=(r // tr,),
            in_specs=[pl.BlockSpec((N_DEV, tr, c), lambda i, me_ref: (0, i, 0)),
                      pl.BlockSpec((None, tr, c), lambda i, me_ref: (me_ref[0], i, 0)), one, one, one],
            out_specs=[one] * 4),
        out_shape=[jax.ShapeDtypeStruct((1, r, c), F32)] * 4,
        compiler_params=_params(("parallel",)),
    )(me, recv, own, w, m, v)


_SMALL = ("meta_tokens", "conv_w", "gla_w_gate2") + tuple(n for n, _ in _VEC_ROWS)


def _update_small(me, srecv, vrecv, sown, vown, w, m, v):
    n = len(_SMALL)

    def body(*refs):
        me_ref, s_ref, v_ref, so_ref, vo_ref = refs[0:5]
        w_refs, m_refs, v_refs = refs[5:5 + n], refs[5 + n:5 + 2 * n], refs[5 + 2 * n:5 + 3 * n]
        outs = refs[5 + 3 * n:]
        ssum = jnp.zeros((SMALL_PACK, 128), F32)
        vsum = jnp.zeros((VEC_ROWS, D), F32)
        for s in range(N_DEV):
            ssum = ssum + jnp.where(me_ref[0] == s, so_ref[s], s_ref[s])
            vsum = vsum + jnp.where(me_ref[0] == s, vo_ref[s], v_ref[s])
        grads = [ssum[0:N_META, :], ssum[CONV_ROW:CONV_ROW + CONV_W, 0:CONV_S], ssum[GATE_ROW:GATE_ROW + RANK, 0:GATE_S]]
        grads += [vsum[i:i + 1, 0:width] for i, (_, width) in enumerate(_VEC_ROWS)]
        for i, g in enumerate(grads):
            d, nm, nv = _adamw(w_refs[i][...], g, m_refs[i][...], v_refs[i][...])
            outs[i][...] = g
            outs[n + i][...] = d
            outs[2 * n + i][...] = nm
            outs[3 * n + i][...] = nv
        outs[4 * n][...] = vsum[LOSS_ROW:LOSS_ROW + 1, 0:128]

    shapes = [jax.ShapeDtypeStruct(t.shape, F32) for t in w]
    res = pl.pallas_call(
        body, name="update_small", out_shape=shapes * 4 + [jax.ShapeDtypeStruct((1, 128), F32)],
        in_specs=[pl.BlockSpec(memory_space=pltpu.SMEM)] + [_whole_vmem()] * (4 + 3 * n),
    )(me, srecv, vrecv, sown, vown, *w, *m, *v)
    return res[0:n], res[n:2 * n], res[2 * n:3 * n], res[3 * n:4 * n], res[4 * n]


_WEIGHTS = ("meta_tokens", "norm_mix_g", "w_in", "conv_w", "conv_b", "conv_ln_g", "conv_ln_b", "gla_w_gate2", "gla_gate_b",
            "gla_norm_g", "w_out", "norm_ffn_g", "w_ffn_gate", "w_ffn_up", "w_ffn_down", "norm_final_g")
_MATRICES = ("w_in", "w_out", "w_ffn_gate", "w_ffn_up", "w_ffn_down")
_TRANSPOSED = ("w_ffn_gate", "w_ffn_up")


def kernel(x, meta_tokens, norm_mix_g, w_in, conv_w, conv_b, conv_ln_g, conv_ln_b, gla_w_gate2, gla_gate_b, gla_norm_g, w_out, norm_ffn_g, w_ffn_gate, w_ffn_up, w_ffn_down, norm_final_g, loss_target, m_meta_tokens, m_norm_mix_g, m_w_in, m_conv_w, m_conv_b, m_conv_ln_g, m_conv_ln_b, m_gla_w_gate2, m_gla_gate_b, m_gla_norm_g, m_w_out, m_norm_ffn_g, m_w_ffn_gate, m_w_ffn_up, m_w_ffn_down, m_norm_final_g, v_meta_tokens, v_norm_mix_g, v_w_in, v_conv_w, v_conv_b, v_conv_ln_g, v_conv_ln_b, v_gla_w_gate2, v_gla_gate_b, v_gla_norm_g, v_w_out, v_norm_ffn_g, v_w_ffn_gate, v_w_ffn_up, v_w_ffn_down, v_norm_final_g):
    given = dict(locals())
    two_d = lambda a: a.reshape(1, -1) if a.ndim == 1 else a.reshape(a.shape[-2:])
    fams = [{n: given[pre + n] for n in _WEIGHTS} for pre in ("", "m_", "v_")]
    for f in fams:
        for n in _TRANSPOSED:
            f[n] = f[n].transpose(0, 2, 1)
    w = fams[0]

    lands, shards = _stage([two_d(w[n]) for n in _MATRICES], w["meta_tokens"], two_d(w["conv_w"]), two_d(w["gla_w_gate2"]))
    soon, later = (0, 5), (1, 2, 3, 4)
    pick = lambda seq, idx: [seq[i] for i in idx]
    first = _send_start("gather_first_start", pick(shards, soon), pick(lands, soon), "first", norm_mix_g)
    ffn_first = _send_start("gather_ffn_first_start", pick(shards, later), pick(lands, later), "first", first[4])
    h0, tgt_p = _pad_rows(x, loss_target)
    _, arrived = _send_wait("gather_first_wait", *first[0:4], "first", tgt_p)
    forward = _send_start("gather_forward_start", [], arrived, "forward", ffn_first[4])
    _, (a_in, a_small) = _send_wait("gather_forward_wait", *forward[0:4], "forward", forward[4])
    w_in, meta, conv_taps, w2 = _unshard_in(a_in, a_small, forward[4])
    p = dict(meta=meta, conv_w=conv_taps, w2=w2, w_in=w_in, g1=norm_mix_g, conv_b=conv_b, ln_g=conv_ln_g, ln_b=conv_ln_b,
             gb=gla_gate_b, ng=gla_norm_g, g2=norm_ffn_g, g3=two_d(norm_final_g))
    passed = {}

    def pass_on(after):
        _, arrived_ffn = _send_wait("gather_ffn_first_wait", *ffn_first[0:4], "first", after)
        passed["sent"] = _send_start("gather_ffn_forward_start", [], arrived_ffn, "forward", after)
        return passed["sent"][4]

    def late_weights(after):
        _, (a_out, a_g, a_u, a_d) = _send_wait("gather_ffn_forward_wait", *passed["sent"][0:4], "forward", after)
        return a_out.reshape(D, D), a_g.reshape(D_FF, D), a_u.reshape(D_FF, D), a_d.reshape(D_FF, D)

    sent = {}

    def send_early(tag, mats):
        landing = [_in_hbm(lax.empty(m_.shape, m_.dtype)) for m_ in mats]
        sent[tag] = _send_start("scatter_" + tag + "_start", mats, landing, "scatter", norm_mix_g)
        return sent[tag][4]

    grad_x, g = _local_step(h0, tgt_p, p, pass_on, late_weights, send_early)

    token = send_early("small", list(_pack_small(g)))
    x_, y_, c_ = _position()
    me = (4 * x_ + 2 * y_ + c_).astype(jnp.int32).reshape(1)
    res = {}
    for tag, names in (("ffn", ("w_ffn_gate", "w_ffn_up", "w_ffn_down")), ("out", ("w_out",)), ("in", ("w_in",))):
        own, recv = _send_wait("scatter_" + tag + "_wait", *sent[tag][0:4], "scatter", token)
        for n, o_, r_ in zip(names, own, recv):
            res[n] = _update_matrix(r_, o_, me, *[f[n] for f in fams], "update_" + n)
            token = res[n][1]
    (sown, vown), (srecv, vrecv) = _send_wait("scatter_small_wait", *sent["small"][0:4], "scatter", token)
    small = _update_small(me, srecv, vrecv, sown, vown, *[[two_d(f[n]) for n in _SMALL] for f in fams])
    for i, n in enumerate(_SMALL):
        res[n] = [fam[i].reshape(w[n].shape) for fam in small[0:4]]
    for n in _TRANSPOSED:
        res[n] = [t.transpose(0, 2, 1) for t in res[n]]
    outs = [small[4][0, 0], grad_x]
    for k in range(4):
        outs += [res[n][k] for n in _WEIGHTS]
    return tuple(outs)
```

```python
import functools

import jax
import jax.numpy as jnp
from jax import lax
from jax.experimental import pallas as pl
from jax.experimental.pallas import tpu as pltpu

F32 = jnp.float32
BF16 = jnp.bfloat16

D = 1024
N_META = 16
C_CONV = 512
CONV_W = 31
GLA_H = 4
GLA_DK = 64
GLA_DV = 128
GLA_K = GLA_H * GLA_DK
GLA_V = GLA_H * GLA_DV
RANK = 16
RANK_P = 128
TAU = 16.0
CHUNK = 64
LEAD = CHUNK
ZROWS = LEAD - N_META
D_IN = 2 * C_CONV + 2 * GLA_K + 2 * GLA_V + RANK
D_INP = D_IN - RANK + RANK_P
D_FF = 2816
FF_CHUNK = 1408
FF_SPLIT = (0, 1536, D_FF)
RMS_EPS = 1e-6
LN_EPS = 1e-5
N_DEV = 8

ADAM_LR = 0.001
ADAM_B1 = 0.9
ADAM_B2 = 0.999
ADAM_EPS = 1e-08
ADAM_WD = 0.01
ADAM_STEP = 10

VMEM_LIMIT = 60 * 1024 * 1024
ROW_TILE = 1056
FFN_ROW_TILE = 352
DW_ROW_TILE = 1408
MESH = pl.DeviceIdType.MESH

_NN = (((1,), (0,)), ((), ()))
_NT = (((1,), (1,)), ((), ()))
_TN = (((0,), (0,)), ((), ()))


def _dot(a, b, dims=_NN):
    return lax.dot_general(a, b, dims, preferred_element_type=F32)


def _sigmoid(x):
    return 1.0 / (1.0 + jnp.exp(-x))


def _row_tile(rows, target):
    best = None
    for t in range(16, min(rows, target) + 1, 16):
        if rows % t == 0:
            best = t
    assert best is not None, rows
    return best


def _params(sem=None):
    return pltpu.CompilerParams(dimension_semantics=sem, vmem_limit_bytes=VMEM_LIMIT)


def _whole_vmem():
    return pl.BlockSpec(memory_space=pltpu.VMEM)


def _rows(tm, width):
    return pl.BlockSpec((tm, width), lambda i: (i, 0))


def _fixed(shape):
    return pl.BlockSpec(shape, lambda *_: (0,) * len(shape))


def _fwd_inproj(h0, g1, w_in):
    rows = h0.shape[0]
    tm = _row_tile(rows, ROW_TILE)

    def body(h_ref, g_ref, w_ref, uc_ref, qk_ref, vg_ref, lr_ref, n1_ref):
        h = h_ref[...]
        r = lax.rsqrt(jnp.mean(h * h, axis=-1, keepdims=True) + RMS_EPS)
        n = (h * r * g_ref[...]).astype(BF16)
        n1_ref[...] = n
        uc_ref[...] = _dot(n, w_ref[:, 0:1024])
        qk_ref[...] = _dot(n, w_ref[:, 1024:1536])
        vg_ref[...] = _dot(n, w_ref[:, 1536:2560])
        lr_ref[...] = _dot(n, w_ref[:, 2560:2688])

    return pl.pallas_call(
        body, name="fwd_inproj", grid=(rows // tm,),
        in_specs=[_rows(tm, D), _fixed((1, D)), _whole_vmem()],
        out_specs=[_rows(tm, 1024), _rows(tm, 512), _rows(tm, 1024), _rows(tm, RANK_P), _rows(tm, D)],
        out_shape=[jax.ShapeDtypeStruct((rows, 1024), F32), jax.ShapeDtypeStruct((rows, 512), F32),
                   jax.ShapeDtypeStruct((rows, 1024), F32), jax.ShapeDtypeStruct((rows, RANK_P), F32),
                   jax.ShapeDtypeStruct((rows, D), BF16)],
        compiler_params=_params(("parallel",)),
    )(h0, g1, w_in)


def _fwd_outproj(yc, yg, h0, w_out, g2, token):
    rows = h0.shape[0]
    tm = _row_tile(rows, ROW_TILE)

    def body(yc_ref, yg_ref, h_ref, w_ref, g_ref, token_ref, h1_ref, n2_ref):
        h1 = h_ref[...] + _dot(yc_ref[...], w_ref[0:C_CONV, :]) + _dot(yg_ref[...], w_ref[C_CONV:D, :])
        h1_ref[...] = h1
        r = lax.rsqrt(jnp.mean(h1 * h1, axis=-1, keepdims=True) + RMS_EPS)
        n2_ref[...] = (h1 * r * g_ref[...]).astype(BF16)

    return pl.pallas_call(
        body, name="fwd_outproj", grid=(rows // tm,),
        in_specs=[_rows(tm, C_CONV), _rows(tm, GLA_V), _rows(tm, D), _whole_vmem(), _fixed((1, D)), _fixed((8, 128))],
        out_specs=[_rows(tm, D), _rows(tm, D)],
        out_shape=[jax.ShapeDtypeStruct((rows, D), F32), jax.ShapeDtypeStruct((rows, D), BF16)],
        compiler_params=_params(("parallel",)),
    )(yc, yg, h0, w_out, g2, token)


def _ffn_rows(h1, n2, tgt, wg, wu, wd, g2, g3, rows_per_example):
    rows = h1.shape[0]
    tm = _row_tile(rows, FFN_ROW_TILE)
    ff_blocks = [slice(lo, hi) for lo, hi in zip(FF_SPLIT[:-1], FF_SPLIT[1:])]

    def body(h1_ref, n2_ref, t_ref, wg_ref, wu_ref, wd_ref, g2_ref, g3_ref,
             f_ref, da_ref, db_ref, dh2_ref, dh1_ref, dh1b_ref, part_ref):
        i = pl.program_id(0)
        n2 = n2_ref[...]
        y2 = jnp.zeros((tm, D), F32)
        for cs in ff_blocks:
            a = _dot(n2, wg_ref[cs, :], _NT)
            b = _dot(n2, wu_ref[cs, :], _NT)
            f = (a * _sigmoid(a) * b).astype(BF16)
            f_ref[:, cs] = f
            da_ref[:, cs] = a.astype(BF16)
            db_ref[:, cs] = b.astype(BF16)
            y2 = y2 + _dot(f, wd_ref[cs, :])
        h1 = h1_ref[...]
        h2 = h1 + y2
        r3 = lax.rsqrt(jnp.mean(h2 * h2, axis=-1, keepdims=True) + RMS_EPS)
        xh3 = h2 * r3
        g3 = g3_ref[...]
        pos = (i * tm + lax.broadcasted_iota(jnp.int32, (tm, 1), 0)) % rows_per_example
        valid = pos >= LEAD
        err = jnp.where(valid, xh3 * g3 - t_ref[...], 0.0)
        loss = 0.5 / D * jnp.sum(jnp.sum(err * err, axis=-1, keepdims=True), axis=0, keepdims=True)
        dy = err * (1.0 / D)
        dg3 = jnp.sum(dy * xh3, axis=0, keepdims=True)
        dxh = dy * g3
        dh2 = r3 * (dxh - xh3 * jnp.mean(dxh * xh3, axis=-1, keepdims=True))
        dh2b = dh2.astype(BF16)
        dh2_ref[...] = dh2b
        dn2 = jnp.zeros((tm, D), F32)
        for cs in ff_blocks:
            df = _dot(dh2b, wd_ref[cs, :], _NT)
            a = da_ref[:, cs].astype(F32)
            b = db_ref[:, cs].astype(F32)
            sg = _sigmoid(a)
            da = (df * b * sg * (1.0 + a * (1.0 - sg))).astype(BF16)
            db = (df * a * sg).astype(BF16)
            da_ref[:, cs] = da
            db_ref[:, cs] = db
            dn2 = dn2 + _dot(da, wg_ref[cs, :]) + _dot(db, wu_ref[cs, :])
        r2 = lax.rsqrt(jnp.mean(h1 * h1, axis=-1, keepdims=True) + RMS_EPS)
        xh2 = h1 * r2
        dg2 = jnp.sum(dn2 * xh2, axis=0, keepdims=True)
        dxh2 = dn2 * g2_ref[...]
        dh1 = dh2 + r2 * (dxh2 - xh2 * jnp.mean(dxh2 * xh2, axis=-1, keepdims=True))
        dh1_ref[...] = dh1
        dh1b_ref[...] = dh1.astype(BF16)

        @pl.when(i == 0)
        def _():
            part_ref[...] = jnp.zeros_like(part_ref)

        part_ref[0:1, :] += dg3
        part_ref[1:2, :] += dg2
        part_ref[2:3, :] += jnp.broadcast_to(loss, (1, D))

    return pl.pallas_call(
        body, name="ffn_rows", grid=(rows // tm,),
        in_specs=[_rows(tm, D), _rows(tm, D), _rows(tm, D), _whole_vmem(), _whole_vmem(), _whole_vmem(),
                  _fixed((1, D)), _fixed((1, D))],
        out_specs=[_rows(tm, D_FF), _rows(tm, D_FF), _rows(tm, D_FF), _rows(tm, D), _rows(tm, D), _rows(tm, D),
                   _fixed((8, D))],
        out_shape=[jax.ShapeDtypeStruct((rows, D_FF), BF16)] * 3
        + [jax.ShapeDtypeStruct((rows, D), BF16), jax.ShapeDtypeStruct((rows, D), F32),
           jax.ShapeDtypeStruct((rows, D), BF16), jax.ShapeDtypeStruct((8, D), F32)],
        compiler_params=_params(("arbitrary",)),
    )(h1, n2, tgt, wg, wu, wd, g2, g3)


def _bwd_outproj(dh1b, w_out, token):
    rows = dh1b.shape[0]
    tm = _row_tile(rows, ROW_TILE)

    def body(d_ref, w_ref, token_ref, dyc_ref, dyg_ref):
        d = d_ref[...]
        dyc_ref[...] = _dot(d, w_ref[0:C_CONV, :], _NT)
        dyg_ref[...] = _dot(d, w_ref[C_CONV:D, :], _NT)

    return pl.pallas_call(
        body, name="bwd_outproj", grid=(rows // tm,),
        in_specs=[_rows(tm, D), _whole_vmem(), _fixed((8, 128))],
        out_specs=[_rows(tm, C_CONV), _rows(tm, GLA_V)],
        out_shape=[jax.ShapeDtypeStruct((rows, C_CONV), F32), jax.ShapeDtypeStruct((rows, GLA_V), F32)],
        compiler_params=_params(("parallel",)),
    )(dh1b, w_out, token)


def _bwd_inproj(duc, dqk, dvg, dlr, dh1, h0, w_in, g1, token, rows_per_example):
    rows = h0.shape[0]
    n_ex = rows // rows_per_example
    tm = _row_tile(rows_per_example, ROW_TILE)
    tiles_per_example = rows_per_example // tm
    n_steps = rows // tm

    def body(duc_ref, dqk_ref, dvg_ref, dlr_ref, dh1_ref, h_ref, w_ref, g_ref, token_ref, gx_ref, part_ref, dmeta_ref,
             buf_ref, sems):
        dn = (_dot(duc_ref[...], w_ref[:, 0:1024], _NT) + _dot(dqk_ref[...], w_ref[:, 1024:1536], _NT)
              + _dot(dvg_ref[...], w_ref[:, 1536:2560], _NT) + _dot(dlr_ref[...], w_ref[:, 2560:2688], _NT))
        h = h_ref[...]
        r = lax.rsqrt(jnp.mean(h * h, axis=-1, keepdims=True) + RMS_EPS)
        xh = h * r
        dg = jnp.sum(dn * xh, axis=0, keepdims=True)
        dxh = dn * g_ref[...]
        dh0 = dh1_ref[...] + r * (dxh - xh * jnp.mean(dxh * xh, axis=-1, keepdims=True))
        i = pl.program_id(0)

        def copies(step):
            slot, b, j = step % 2, step // tiles_per_example, step % tiles_per_example
            out = [(j == 0, pltpu.make_async_copy(buf_ref.at[slot, pl.ds(LEAD, tm - LEAD)],
                                                   gx_ref.at[b, pl.ds(0, tm - LEAD)], sems.at[slot]))]
            if tiles_per_example > 1:
                out.append((j != 0, pltpu.make_async_copy(
                    buf_ref.at[slot], gx_ref.at[b, pl.ds(pl.multiple_of(jnp.maximum(j * tm - LEAD, 0), 8), tm)],
                    sems.at[slot])))
            return out

        def each(step, act):
            for cond, cp in copies(step):
                pl.when(cond)(functools.partial(act, cp))

        @pl.when(i >= 2)
        def _():
            each(i - 2, lambda cp: cp.wait())

        buf_ref[i % 2] = dh0
        each(i, lambda cp: cp.start())

        @pl.when(i == n_steps - 1)
        def _():
            each(i, lambda cp: cp.wait())
            if n_steps > 1:
                each(i - 1, lambda cp: cp.wait())

        @pl.when(i == 0)
        def _():
            part_ref[...] = jnp.zeros_like(part_ref)
            dmeta_ref[...] = jnp.zeros_like(dmeta_ref)

        part_ref[0:1, :] += dg

        @pl.when(i % tiles_per_example == 0)
        def _():
            dmeta_ref[...] += dh0[ZROWS:LEAD, :]

    return pl.pallas_call(
        body, name="bwd_inproj", grid=(n_steps,),
        in_specs=[_rows(tm, 1024), _rows(tm, 512), _rows(tm, 1024), _rows(tm, RANK_P), _rows(tm, D), _rows(tm, D),
                  _whole_vmem(), _fixed((1, D)), _fixed((8, 128))],
        out_specs=[_any(), _fixed((8, D)), _fixed((N_META, D))],
        out_shape=[jax.ShapeDtypeStruct((n_ex, rows_per_example - LEAD, D), F32), jax.ShapeDtypeStruct((8, D), F32),
                   jax.ShapeDtypeStruct((N_META, D), F32)],
        scratch_shapes=[pltpu.VMEM((2, tm, D), F32), pltpu.SemaphoreType.DMA((2,))],
        compiler_params=_params(("arbitrary",)),
    )(duc, dqk, dvg, dlr, dh1, h0, w_in, g1, token)


def _dw_blocked(a, bs, width, name):
    rows, m = a.shape
    ws = [b.shape[1] for b in bs]
    assert sum(ws) >= N_DEV * width
    tk = _row_tile(rows, DW_ROW_TILE)
    nk = rows // tk

    def body(a_ref, *refs):
        b_refs, o_ref, acc_ref = refs[:len(bs)], refs[len(bs)], refs[len(bs) + 1]
        k = pl.program_id(0)

        @pl.when(k == 0)
        def _():
            acc_ref[...] = jnp.zeros_like(acc_ref)

        at = a_ref[...].T
        off = 0
        for b_ref, w in zip(b_refs, ws):
            acc_ref[:, off:off + w] += _dot(at, b_ref[...])
            off += w

        @pl.when(k == nk - 1)
        def _():
            for d in range(N_DEV):
                o_ref[d] = acc_ref[:, d * width:(d + 1) * width].astype(BF16)

    return pl.pallas_call(
        body, name=name, grid=(nk,),
        in_specs=[_rows(tk, m)] + [_rows(tk, w) for w in ws],
        out_specs=_fixed((N_DEV, m, width)),
        out_shape=jax.ShapeDtypeStruct((N_DEV, m, width), BF16),
        scratch_shapes=[pltpu.VMEM((m, sum(ws)), F32)],
        compiler_params=_params(("arbitrary",)),
    )(a, *bs)


def _dw_out(yc, yg, dh1b):
    rows = yc.shape[0]
    tk = _row_tile(rows, DW_ROW_TILE)
    nk = rows // tk

    def body(yc_ref, yg_ref, d_ref, o_ref, acc_ref):
        k = pl.program_id(0)

        @pl.when(k == 0)
        def _():
            acc_ref[...] = jnp.zeros_like(acc_ref)

        d = d_ref[...]
        acc_ref[0:C_CONV, :] += _dot(yc_ref[...], d, _TN)
        acc_ref[C_CONV:D, :] += _dot(yg_ref[...], d, _TN)

        @pl.when(k == nk - 1)
        def _():
            o_ref[...] = acc_ref[...].astype(BF16)

    return pl.pallas_call(
        body, name="dw_out", grid=(nk,),
        in_specs=[_rows(tk, C_CONV), _rows(tk, GLA_V), _rows(tk, D)],
        out_specs=_fixed((D, D)), out_shape=jax.ShapeDtypeStruct((D, D), BF16),
        scratch_shapes=[pltpu.VMEM((D, D), F32)],
        compiler_params=_params(("arbitrary",)),
    )(yc, yg, dh1b)


def _matmul_tn(a, b, name):
    rows, m = a.shape
    n = b.shape[1]
    tk = _row_tile(rows, DW_ROW_TILE)
    tn = n if n <= 1024 else FF_CHUNK
    tm_ = m if m <= 1024 else FF_CHUNK
    assert n % tn == 0 and m % tm_ == 0
    nk = rows // tk

    def body(a_ref, b_ref, o_ref, acc_ref):
        k = pl.program_id(2)

        @pl.when(k == 0)
        def _():
            acc_ref[...] = jnp.zeros_like(acc_ref)

        acc_ref[...] += _dot(a_ref[...], b_ref[...], _TN)

        @pl.when(k == nk - 1)
        def _():
            o_ref[...] = acc_ref[...].astype(BF16)

    return pl.pallas_call(
        body, name=name, grid=(m // tm_, n // tn, nk),
        in_specs=[pl.BlockSpec((tk, tm_), lambda i, j, k: (k, i)), pl.BlockSpec((tk, tn), lambda i, j, k: (k, j))],
        out_specs=pl.BlockSpec((tm_, tn), lambda i, j, k: (i, j)),
        out_shape=jax.ShapeDtypeStruct((m, n), BF16),
        scratch_shapes=[pltpu.VMEM((tm_, tn), F32)],
        compiler_params=_params(("parallel", "parallel", "arbitrary")),
    )(a, b)


HALO = 32
LANES = 128


def _shifted(win, offsets):
    for r in range(8):
        js = [j for j, k in enumerate(offsets) if k % 8 == r]
        if js:
            rolled = win if r == 0 else pltpu.roll(win, CHUNK + HALO - r, 0)
            for j in js:
                yield j, rolled[offsets[j] - r:offsets[j] - r + CHUNK]


def _glu_into(uc_ref, vs_ref, n_chunk):
    vs_ref[0:CHUNK, :] = jnp.zeros((CHUNK, C_CONV), F32)

    def glu(i, carry):
        base = pl.multiple_of(i * CHUNK, CHUNK)
        val = uc_ref[pl.ds(base, CHUNK), 0:C_CONV]
        gate = uc_ref[pl.ds(base, CHUNK), C_CONV:2 * C_CONV]
        vs_ref[pl.ds(base + CHUNK, CHUNK), :] = val * _sigmoid(gate)
        return carry

    lax.fori_loop(0, n_chunk, glu, 0)


def _fwd_conv(uc, conv_w, conv_b, ln_g, ln_b, token, n_ex):
    rows = uc.shape[0]
    lp = rows // n_ex
    n_chunk = lp // CHUNK

    def body(uc_ref, w_ref, b_ref, lg_ref, lb_ref, token_ref, ypre_ref, yc_ref, vs_ref):
        _glu_into(uc_ref, vs_ref, n_chunk)

        def conv(i, carry):
            base = pl.multiple_of(i * CHUNK, CHUNK)
            for lb in range(C_CONV // LANES):
                ls = slice(lb * LANES, (lb + 1) * LANES)
                win = vs_ref[pl.ds(base + CHUNK - HALO, CHUNK + HALO), ls]
                acc = jnp.broadcast_to(b_ref[:, ls], (CHUNK, LANES))
                for j, rows_j in _shifted(win, [HALO - (CONV_W - 1) + j for j in range(CONV_W)]):
                    acc = acc + w_ref[j:j + 1, ls] * rows_j
                ypre_ref[pl.ds(base, CHUNK), ls] = acc
            y = ypre_ref[pl.ds(base, CHUNK), :]
            mu = jnp.mean(y, axis=-1, keepdims=True)
            yc_ = y - mu
            rstd = lax.rsqrt(jnp.mean(yc_ * yc_, axis=-1, keepdims=True) + LN_EPS)
            s = yc_ * rstd * lg_ref[...] + lb_ref[...]
            yc_ref[pl.ds(base, CHUNK), :] = (s * _sigmoid(s)).astype(BF16)
            return carry

        lax.fori_loop(0, n_chunk, conv, 0)

    ex = lambda w: pl.BlockSpec((lp, w), lambda b: (b, 0))
    return pl.pallas_call(
        body, name="fwd_conv", grid=(n_ex,),
        in_specs=[ex(2 * C_CONV), _fixed((32, C_CONV)), _fixed((1, C_CONV)), _fixed((1, C_CONV)), _fixed((1, C_CONV)),
                  _fixed((8, 128))],
        out_specs=[ex(C_CONV), ex(C_CONV)],
        out_shape=[jax.ShapeDtypeStruct((rows, C_CONV), F32), jax.ShapeDtypeStruct((rows, C_CONV), BF16)],
        scratch_shapes=[pltpu.VMEM((lp + CHUNK, C_CONV), F32)],
        compiler_params=_params(("parallel",)),
    )(uc, conv_w, conv_b, ln_g, ln_b, token)


def _bwd_conv(uc, ypre, dyc, conv_w, ln_g, ln_b, token, n_ex):
    rows = uc.shape[0]
    lp = rows // n_ex
    n_chunk = lp // CHUNK

    def body(uc_ref, ypre_ref, dyc_ref, w_ref, lg_ref, lb_ref, token_ref, duc_ref, dw_ref, dvec_ref, vs_ref, dys_ref,
             dwacc_ref):
        _glu_into(uc_ref, vs_ref, n_chunk)
        dys_ref[pl.ds(lp, CHUNK), :] = jnp.zeros((CHUNK, C_CONV), F32)
        dwacc_ref[...] = jnp.zeros_like(dwacc_ref)

        def ln_bwd(i, carry):
            dcb, dlg, dlb = carry
            base = pl.multiple_of(i * CHUNK, CHUNK)
            y = ypre_ref[pl.ds(base, CHUNK), :]
            mu = jnp.mean(y, axis=-1, keepdims=True)
            yc_ = y - mu
            rstd = lax.rsqrt(jnp.mean(yc_ * yc_, axis=-1, keepdims=True) + LN_EPS)
            xh = yc_ * rstd
            s = xh * lg_ref[...] + lb_ref[...]
            sg = _sigmoid(s)
            ds = dyc_ref[pl.ds(base, CHUNK), :] * (sg * (1.0 + s * (1.0 - sg)))
            dxh = ds * lg_ref[...]
            dy = rstd * (dxh - jnp.mean(dxh, axis=-1, keepdims=True) - xh * jnp.mean(dxh * xh, axis=-1, keepdims=True))
            dys_ref[pl.ds(base, CHUNK), :] = dy
            return (dcb + jnp.sum(dy, axis=0, keepdims=True), dlg + jnp.sum(ds * xh, axis=0, keepdims=True),
                    dlb + jnp.sum(ds, axis=0, keepdims=True))

        zero = jnp.zeros((1, C_CONV), F32)
        dcb, dlg, dlb = lax.fori_loop(0, n_chunk, ln_bwd, (zero, zero, zero))

        @pl.when(pl.program_id(0) == 0)
        def _():
            dvec_ref[...] = jnp.zeros_like(dvec_ref)
            dw_ref[...] = jnp.zeros_like(dw_ref)

        dvec_ref[0:1, :] += dcb
        dvec_ref[1:2, :] += dlg
        dvec_ref[2:3, :] += dlb

        def taps(i, carry):
            base = pl.multiple_of(i * CHUNK, CHUNK)
            for lb in range(C_CONV // LANES):
                ls = slice(lb * LANES, (lb + 1) * LANES)
                dwin = dys_ref[pl.ds(base, CHUNK + HALO), ls]
                vwin = vs_ref[pl.ds(base + CHUNK - HALO, CHUNK + HALO), ls]
                dy = dwin[0:CHUNK]
                acc = jnp.zeros((CHUNK, LANES), F32)
                for j, rows_j in _shifted(dwin, [CONV_W - 1 - j for j in range(CONV_W)]):
                    acc = acc + w_ref[j:j + 1, ls] * rows_j
                for j, rows_j in _shifted(vwin, [HALO - (CONV_W - 1) + j for j in range(CONV_W)]):
                    dwacc_ref[8 * j:8 * j + 8, ls] += jnp.sum((dy * rows_j).reshape(CHUNK // 8, 8, LANES), axis=0)
                val = uc_ref[pl.ds(base, CHUNK), ls]
                gate = uc_ref[pl.ds(base, CHUNK), C_CONV + lb * LANES:C_CONV + (lb + 1) * LANES]
                sg = _sigmoid(gate)
                duc_ref[pl.ds(base, CHUNK), ls] = (acc * sg).astype(BF16)
                duc_ref[pl.ds(base, CHUNK), C_CONV + lb * LANES:C_CONV + (lb + 1) * LANES] = (
                    acc * val * sg * (1.0 - sg)).astype(BF16)
            return carry

        lax.fori_loop(0, n_chunk, taps, 0)
        for j in range(CONV_W):
            dw_ref[j:j + 1, :] += jnp.sum(dwacc_ref[8 * j:8 * j + 8, :], axis=0, keepdims=True)

    ex = lambda w: pl.BlockSpec((lp, w), lambda b: (b, 0))
    return pl.pallas_call(
        body, name="bwd_conv", grid=(n_ex,),
        in_specs=[ex(2 * C_CONV), ex(C_CONV), ex(C_CONV), _fixed((32, C_CONV)), _fixed((1, C_CONV)), _fixed((1, C_CONV)),
                  _fixed((8, 128))],
        out_specs=[ex(2 * C_CONV), _fixed((32, C_CONV)), _fixed((8, C_CONV))],
        out_shape=[jax.ShapeDtypeStruct((rows, 2 * C_CONV), BF16), jax.ShapeDtypeStruct((32, C_CONV), F32),
                   jax.ShapeDtypeStruct((8, C_CONV), F32)],
        scratch_shapes=[pltpu.VMEM((lp + CHUNK, C_CONV), F32), pltpu.VMEM((lp + CHUNK, C_CONV), F32),
                        pltpu.VMEM((8 * 32, C_CONV), F32)],
        compiler_params=_params(("arbitrary",)),
    )(uc, ypre, dyc, conv_w, ln_g, ln_b, token)


def _seg_chunks(n_chunk):
    return max(c for c in (11, 3, 1) if n_chunk % c == 0)


def _block_mask(shape, row_block, lane_block):
    return (lax.broadcasted_iota(jnp.int32, shape, 0) // row_block) == (lax.broadcasted_iota(jnp.int32, shape, 1) // lane_block)


def _per_head_rows(x, mask):
    return jnp.where(mask, jnp.concatenate([x] * GLA_H, axis=0), 0)


def _fold_heads(full, lane_block):
    lane = lax.broadcasted_iota(jnp.int32, (1, full.shape[1]), 1) // lane_block
    out = jnp.where(lane == 0, full[0:CHUNK], 0.0)
    for h in range(1, GLA_H):
        out = out + jnp.where(lane == h, full[h * CHUNK:(h + 1) * CHUNK], 0.0)
    return out


def _causal_heads():
    return (lax.broadcasted_iota(jnp.int32, (CHUNK, GLA_H * CHUNK), 1) % CHUNK) <= lax.broadcasted_iota(
        jnp.int32, (CHUNK, GLA_H * CHUNK), 0)


def _cumsum_rows(x):
    row = lax.broadcasted_iota(jnp.int32, x.shape, 0)
    s = 1
    while s < CHUNK:
        x = x + jnp.where(row >= s, pltpu.roll(x, s, 0), 0.0)
        s *= 2
    return x


def _rev_cumsum_rows(x):
    row = lax.broadcasted_iota(jnp.int32, x.shape, 0)
    s = 1
    while s < CHUNK:
        x = x + jnp.where(row < CHUNK - s, pltpu.roll(x, CHUNK - s, 0), 0.0)
        s *= 2
    return x


def _gate_terms(lr_ref, w2_ref, gb_ref, rs, first_pos):
    z = _dot(lr_ref[rs, :].astype(BF16), w2_ref[...]) + gb_ref[...]
    la = (jnp.minimum(z, 0.0) - jnp.log(1.0 + jnp.exp(-jnp.abs(z)))) * (1.0 / TAU)
    pos = first_pos + lax.broadcasted_iota(jnp.int32, (CHUNK, 1), 0)
    live = pos >= ZROWS
    la = jnp.where(live, la, 0.0)
    return z, live, _cumsum_rows(la)


def _fwd_gla(qk, vg, lr, w2p, gb, ng, n_ex):
    rows = qk.shape[0]
    lp = rows // n_ex
    n_chunk = lp // CHUNK
    sc = _seg_chunks(n_chunk)
    n_seg = n_chunk // sc
    seg = sc * CHUNK

    def body(qk_ref, vg_ref, lr_ref, w2_ref, gb_ref, ng_ref, yg_ref, o_ref, st_ref, state_ref):
        sidx = pl.program_id(1)

        @pl.when(sidx == 0)
        def _():
            state_ref[...] = jnp.zeros_like(state_ref)

        causal = _causal_heads()
        k_mask = _block_mask((GLA_H * CHUNK, GLA_K), CHUNK, GLA_DK)
        v_mask = _block_mask((GLA_H * CHUNK, GLA_V), CHUNK, GLA_DV)
        s_mask = _block_mask((GLA_V, GLA_K), GLA_DV, GLA_DK)

        def chunk(ci, carry):
            base = pl.multiple_of(ci * CHUNK, CHUNK)
            rs = pl.ds(base, CHUNK)
            _, _, bcum = _gate_terms(lr_ref, w2_ref, gb_ref, rs, (sidx * sc + ci) * CHUNK)
            bl = bcum[CHUNK - 1:CHUNK, :]
            q = qk_ref[rs, 0:GLA_K]
            k = qk_ref[rs, GLA_K:2 * GLA_K]
            qt = (q * (GLA_DK ** -0.5) * jnp.exp(bcum)).astype(BF16)
            kt = (k * jnp.exp(-bcum)).astype(BF16)
            kh = (k * jnp.exp(bl - bcum)).astype(BF16)
            vb = vg_ref[rs, 0:GLA_V].astype(BF16)
            state = state_ref[...]
            st_ref[ci] = state
            a = jnp.where(causal, _dot(qt, _per_head_rows(kt, k_mask), _NT), 0.0)
            o = _dot(a.astype(BF16), _per_head_rows(vb, v_mask)) + _dot(qt, state.astype(BF16), _NT)
            o_ref[rs, :] = o
            for h in range(GLA_H):
                hs = slice(h * GLA_DV, (h + 1) * GLA_DV)
                oh = o[:, hs]
                ro = lax.rsqrt(jnp.mean(oh * oh, axis=-1, keepdims=True) + RMS_EPS)
                g = vg_ref[rs, GLA_V + h * GLA_DV:GLA_V + (h + 1) * GLA_DV]
                yg_ref[rs, hs] = (oh * ro * ng_ref[...] * g * _sigmoid(g)).astype(BF16)
            state_ref[...] = state * jnp.exp(bl) + jnp.where(s_mask, _dot(vb, kh, _TN), 0.0)
            return carry

        lax.fori_loop(0, sc, chunk, 0)

    sg = lambda w: pl.BlockSpec((seg, w), lambda b, s: (b * n_seg + s, 0))
    return pl.pallas_call(
        body, name="fwd_gla", grid=(n_ex, n_seg),
        in_specs=[sg(2 * GLA_K), sg(2 * GLA_V), sg(RANK_P), _fixed((RANK_P, GLA_K)), _fixed((1, GLA_K)), _fixed((1, GLA_DV))],
        out_specs=[sg(GLA_V), sg(GLA_V), pl.BlockSpec((sc, GLA_V, GLA_K), lambda b, s: (b * n_seg + s, 0, 0))],
        out_shape=[jax.ShapeDtypeStruct((rows, GLA_V), BF16), jax.ShapeDtypeStruct((rows, GLA_V), F32),
                   jax.ShapeDtypeStruct((n_ex * n_chunk, GLA_V, GLA_K), F32)],
        scratch_shapes=[pltpu.VMEM((GLA_V, GLA_K), F32)],
        compiler_params=_params(("parallel", "arbitrary")),
    )(qk, vg, lr, w2p, gb, ng)


def _bwd_gla(qk, vg, lr, o, st, dyg, w2p, gb, ng, n_ex):
    rows = qk.shape[0]
    lp = rows // n_ex
    n_chunk = lp // CHUNK
    sc = _seg_chunks(n_chunk)
    n_seg = n_chunk // sc
    seg = sc * CHUNK

    def body(qk_ref, vg_ref, lr_ref, o_ref, st_ref, dyg_ref, w2_ref, gb_ref, ng_ref,
             dqk_ref, dvg_ref, dlr_ref, dw2_ref, dvec_ref, gt_ref, dz_ref):
        step = pl.program_id(1)
        sidx = n_seg - 1 - step

        @pl.when(step == 0)
        def _():
            gt_ref[...] = jnp.zeros_like(gt_ref)

        @pl.when((step == 0) & (pl.program_id(0) == 0))
        def _():
            dw2_ref[...] = jnp.zeros_like(dw2_ref)
            dvec_ref[...] = jnp.zeros_like(dvec_ref)

        causal = _causal_heads()
        k_mask = _block_mask((GLA_H * CHUNK, GLA_K), CHUNK, GLA_DK)
        v_mask = _block_mask((GLA_H * CHUNK, GLA_V), CHUNK, GLA_DV)
        s_mask = _block_mask((GLA_V, GLA_K), GLA_DV, GLA_DK)
        last_row = lax.broadcasted_iota(jnp.int32, (CHUNK, 1), 0) == CHUNK - 1
        ng = ng_ref[...]

        def chunk(ii, dng):
            ci = sc - 1 - ii
            base = pl.multiple_of(ci * CHUNK, CHUNK)
            rs = pl.ds(base, CHUNK)
            z, live, bcum = _gate_terms(lr_ref, w2_ref, gb_ref, rs, (sidx * sc + ci) * CHUNK)
            bl = bcum[CHUNK - 1:CHUNK, :]
            ebl = jnp.exp(bl)
            q = qk_ref[rs, 0:GLA_K]
            k = qk_ref[rs, GLA_K:2 * GLA_K]
            eb = jnp.exp(bcum)
            enb = jnp.exp(-bcum)
            ehb = jnp.exp(bl - bcum)
            qt = q * (GLA_DK ** -0.5) * eb
            kt = k * enb
            kh = k * ehb
            qtb = qt.astype(BF16)
            vb = vg_ref[rs, 0:GLA_V].astype(BF16)
            k_rows = _per_head_rows(kt.astype(BF16), k_mask)
            v_rows = _per_head_rows(vb, v_mask)
            gt = gt_ref[...]
            gtb = gt.astype(BF16)
            s_in = st_ref[ci]
            dos = []
            for h in range(GLA_H):
                hs = slice(h * GLA_DV, (h + 1) * GLA_DV)
                gs = slice(GLA_V + h * GLA_DV, GLA_V + (h + 1) * GLA_DV)
                oh = o_ref[rs, hs]
                ro = lax.rsqrt(jnp.mean(oh * oh, axis=-1, keepdims=True) + RMS_EPS)
                on = oh * ro
                g = vg_ref[rs, gs]
                sg = _sigmoid(g)
                dout = dyg_ref[rs, hs]
                dvg_ref[rs, gs] = (dout * on * ng * (sg * (1.0 + g * (1.0 - sg)))).astype(BF16)
                dw = dout * g * sg
                dng = dng + jnp.sum(dw * on, axis=0, keepdims=True)
                don = dw * ng
                dos.append((ro * (don - on * jnp.mean(don * on, axis=-1, keepdims=True))).astype(BF16))
            dob = jnp.concatenate(dos, axis=1)
            a = jnp.where(causal, _dot(qtb, k_rows, _NT), 0.0).astype(BF16)
            da = jnp.where(causal, _dot(dob, v_rows, _NT), 0.0).astype(BF16)
            dv = _fold_heads(_dot(a, dob, _TN), GLA_DV) + _dot(kh.astype(BF16), gtb, _NT)
            dvg_ref[rs, 0:GLA_V] = dv.astype(BF16)
            dkh = _dot(vb, gtb)
            dqt = _dot(da, k_rows) + _dot(dob, s_in.astype(BF16))
            dkt = _fold_heads(_dot(da, qtb, _TN), GLA_DK)
            dbl = jnp.sum(gt * s_in, axis=0, keepdims=True) * ebl + jnp.sum(dkh * kh, axis=0, keepdims=True)
            dqk_ref[rs, 0:GLA_K] = (dqt * (GLA_DK ** -0.5) * eb).astype(BF16)
            dqk_ref[rs, GLA_K:2 * GLA_K] = (dkt * enb + dkh * ehb).astype(BF16)
            db = dqt * qt - dkt * kt - dkh * kh
            db = jnp.where(last_row, db + dbl, db)
            dla = jnp.where(live, _rev_cumsum_rows(db), 0.0)
            dz_ref[rs, :] = dla * (1.0 / TAU) * (1.0 - _sigmoid(z))
            gt_ref[...] = jnp.where(s_mask, _dot(dob, qtb, _TN), 0.0) + gt * ebl
            return dng

        dng = lax.fori_loop(0, sc, chunk, jnp.zeros((1, GLA_DV), F32))
        dz = dz_ref[...]
        dzb = dz.astype(BF16)
        dlr_ref[...] = _dot(dzb, w2_ref[...], _NT).astype(BF16)
        dw2_ref[...] += _dot(lr_ref[...].astype(BF16), dzb, _TN)
        dvec_ref[0:1, :] += jnp.sum(dz, axis=0, keepdims=True)
        dvec_ref[1:2, 0:GLA_DV] += dng

    sg_ = lambda w: pl.BlockSpec((seg, w), lambda b, s: (b * n_seg + n_seg - 1 - s, 0))
    return pl.pallas_call(
        body, name="bwd_gla", grid=(n_ex, n_seg),
        in_specs=[sg_(2 * GLA_K), sg_(2 * GLA_V), sg_(RANK_P), sg_(GLA_V),
                  pl.BlockSpec((sc, GLA_V, GLA_K), lambda b, s: (b * n_seg + n_seg - 1 - s, 0, 0)), sg_(GLA_V),
                  _fixed((RANK_P, GLA_K)), _fixed((1, GLA_K)), _fixed((1, GLA_DV))],
        out_specs=[sg_(2 * GLA_K), sg_(2 * GLA_V), sg_(RANK_P), _fixed((RANK_P, GLA_K)), _fixed((8, GLA_K))],
        out_shape=[jax.ShapeDtypeStruct((rows, 2 * GLA_K), BF16), jax.ShapeDtypeStruct((rows, 2 * GLA_V), BF16),
                   jax.ShapeDtypeStruct((rows, RANK_P), BF16), jax.ShapeDtypeStruct((RANK_P, GLA_K), F32),
                   jax.ShapeDtypeStruct((8, GLA_K), F32)],
        scratch_shapes=[pltpu.VMEM((GLA_V, GLA_K), F32), pltpu.VMEM((seg, GLA_K), F32)],
        compiler_params=_params(("arbitrary", "arbitrary")),
    )(qk, vg, lr, o, st, dyg, w2p, gb, ng)


def _pad_rows(x, tgt):
    return jnp.pad(x, ((0, 0), (LEAD, 0), (0, 0))), jnp.pad(tgt, ((0, 0), (LEAD, 0), (0, 0)))


def _local_step(h0, tgt_p, p, pass_on, late_weights, send_early):
    n_ex, lp, _ = h0.shape
    rows = n_ex * lp
    meta = jnp.broadcast_to(p["meta"][None], (n_ex, N_META, D))
    h0 = lax.dynamic_update_slice(h0, meta, (0, ZROWS, 0)).reshape(rows, D)
    tgt_p = tgt_p.reshape(rows, D)

    uc, qk, vg, lr, n1 = _fwd_inproj(h0, p["g1"], p["w_in"])
    yg, o, st = _fwd_gla(qk, vg, lr, p["w2"], p["gb"], p["ng"], n_ex)
    token = pass_on(yg)
    ypre, yc = _fwd_conv(uc, p["conv_w"], p["conv_b"], p["ln_g"], p["ln_b"], token, n_ex)
    w_out, wg, wu, wd = late_weights(yc)
    h1, n2 = _fwd_outproj(yc, yg, h0, w_out, p["g2"], token)
    f, da, db, dh2, dh1, dh1b, part = _ffn_rows(h1, n2, tgt_p, wg, wu, wd, p["g2"], p["g3"], lp)
    g = {}
    token = send_early("ffn", [_matmul_tn(a_, b_, name).reshape(N_DEV, FF_S, D) for a_, b_, name in (
        (da, n2, "dw_gate"), (db, n2, "dw_up"), (f, dh2, "dw_down"))])
    dyc, dyg = _bwd_outproj(dh1b, w_out, token)
    token = send_early("out", [_dw_out(yc, yg, dh1b).reshape(N_DEV, W_OUT_S, D)])
    duc, g["conv_w"], g["conv_vec"] = _bwd_conv(uc, ypre, dyc, p["conv_w"], p["ln_g"], p["ln_b"], token, n_ex)
    dqk, dvg, dlr, g["w2"], g["gla_vec"] = _bwd_gla(qk, vg, lr, o, st, dyg, p["w2"], p["gb"], p["ng"], n_ex)
    token = send_early("in", [_dw_blocked(n1, [duc, dqk, dvg, dlr], W_IN_S, "dw_in")])
    grad_x, g["in_vec"], g["meta"] = _bwd_inproj(duc, dqk, dvg, dlr, dh1, h0, p["w_in"], p["g1"], token, lp)
    g["ffn_vec"] = part
    return grad_x, g


W_IN_S = D_IN // N_DEV
W_OUT_S = D // N_DEV
FF_S = D_FF // N_DEV
CONV_S = C_CONV // N_DEV
GATE_S = GLA_K // N_DEV
SMALL_PACK = 64
CONV_ROW = 16
GATE_ROW = 48
VEC_ROWS = 16
_VEC_ROWS = (("norm_mix_g", D), ("conv_b", C_CONV), ("conv_ln_g", C_CONV), ("conv_ln_b", C_CONV), ("gla_gate_b", GLA_K),
             ("gla_norm_g", GLA_DV), ("norm_ffn_g", D), ("norm_final_g", D))
LOSS_ROW = len(_VEC_ROWS)


def _position():
    return lax.axis_index("x"), lax.axis_index("y"), lax.axis_index("c")


def _any():
    return pl.BlockSpec(memory_space=pl.ANY)


def _stage(mats, meta, conv_w, w2):
    n_t = len(mats) + 1

    def body(*refs):
        ins = refs[0:n_t - 1]
        meta_ref, cw_ref, w2_ref = refs[n_t - 1:n_t + 2]
        lands = refs[n_t + 2:2 * n_t + 2]
        shards = refs[2 * n_t + 2:3 * n_t + 2]
        sems = refs[3 * n_t + 2]
        for s_ref, w_ref in zip(shards, ins):
            s_ref[...] = w_ref[...].astype(BF16)
        sp = shards[n_t - 1]
        sp[...] = jnp.zeros_like(sp)
        sp[0:N_META, :] = meta_ref[...]
        sp[CONV_ROW:CONV_ROW + CONV_W, 0:CONV_S] = cw_ref[...]
        sp[GATE_ROW:GATE_ROW + RANK, 0:GATE_S] = w2_ref[...]
        x, y, c = _position()
        mine = [pltpu.make_async_copy(shards[t], lands[t].at[4 * x + 2 * y + c], sems.at[t]) for t in range(n_t)]
        for cp in mine:
            cp.start()
        for cp in mine:
            cp.wait()

    shard_shapes = [jax.ShapeDtypeStruct(m.shape, BF16) for m in mats] + [jax.ShapeDtypeStruct((SMALL_PACK, 128), F32)]
    res = pl.pallas_call(
        body, name="stage",
        out_shape=[jax.ShapeDtypeStruct((N_DEV,) + s.shape, s.dtype) for s in shard_shapes] + shard_shapes,
        in_specs=[_whole_vmem()] * (n_t + 2), out_specs=[_any()] * n_t + [_whole_vmem()] * n_t,
        scratch_shapes=[pltpu.SemaphoreType.DMA((n_t,))],
        compiler_params=pltpu.CompilerParams(vmem_limit_bytes=VMEM_LIMIT),
    )(*mats, meta, conv_w, w2)
    return res[0:n_t], res[n_t:]


_HBM = pl.BlockSpec(memory_space=pltpu.HBM)
_SEM = pl.BlockSpec(memory_space=pltpu.SEMAPHORE)
_EFFECT = pltpu.SideEffectType.DATAFLOW_SIDE_EFFECTING


_N_ROUTES = {"scatter": 7, "first": 4, "forward": 3}


def _routes(mode):
    x, y, c = _position()
    me = 4 * x + 2 * y + c
    if mode == "scatter":
        out = []
        for k in range(1, N_DEV):
            px = 1 - x if k & 4 else x
            py = 1 - y if k & 2 else y
            pc = 1 - c if k & 1 else c
            out.append(((px, py, pc), 4 * px + 2 * py + pc, me))
        return out
    if mode == "first":
        return [(pos, None, me) for pos in ((x, y, 1 - c), (1 - x, y, c), (x, 1 - y, c), (1 - x, 1 - y, c))]
    assert mode == "forward"
    return [((x, y, 1 - c), 4 * px + 2 * py + c, 4 * px + 2 * py + c) for px, py in ((1 - x, y), (x, 1 - y), (1 - x, 1 - y))]


def _route_copies(mode, n, src_refs, land_refs, send_sems, recv_sems):
    nr = _N_ROUTES[mode]
    for i, (pos, src_blk, dst_blk) in enumerate(_routes(mode)):
        for t in range(n):
            src = land_refs[t] if mode == "forward" else src_refs[t]
            yield pltpu.make_async_remote_copy(
                src_ref=src if src_blk is None else src.at[src_blk], dst_ref=land_refs[t].at[dst_blk],
                send_sem=send_sems.at[nr * t + i], recv_sem=recv_sems.at[nr * t + i], device_id=pos, device_id_type=MESH)


def _in_hbm(a):
    return pltpu.with_memory_space_constraint(a, pltpu.HBM)


def _send_start(name, srcs, lands, mode, after):
    n, ns = len(lands), len(srcs)
    nsem = _N_ROUTES[mode] * n

    def body(*refs):
        src_refs, land_refs = refs[0:ns], refs[ns:ns + n]
        send_sems, recv_sems = refs[ns + n + 1:ns + n + 3]
        token = refs[2 * (ns + n) + 3]
        for cp in _route_copies(mode, n, src_refs, land_refs, send_sems, recv_sems):
            cp.start()
        token[...] = jnp.zeros_like(token)

    bufs = list(srcs) + list(lands)
    res = pl.pallas_call(
        body, name=name,
        out_shape=(pltpu.SemaphoreType.DMA((nsem,)), pltpu.SemaphoreType.DMA((nsem,)),
                   *[pltpu.HBM(b.shape, b.dtype) for b in bufs], jax.ShapeDtypeStruct((8, 128), F32)),
        in_specs=[_HBM] * len(bufs) + [_any()], out_specs=(_SEM, _SEM, *[_HBM] * len(bufs), _whole_vmem()),
        input_output_aliases={i: 2 + i for i in range(len(bufs))},
        compiler_params=pltpu.CompilerParams(has_side_effects=_EFFECT),
    )(*[_in_hbm(b) for b in bufs], after)
    return res[0], res[1], res[2:2 + ns], res[2 + ns:2 + ns + n], res[2 + ns + n]


def _send_wait(name, send_sems, recv_sems, srcs, lands, mode, after):
    n, ns = len(lands), len(srcs)
    after = after if isinstance(after, tuple) else (after,)

    def body(*refs):
        src_refs, land_refs = refs[0:ns], refs[ns:ns + n]
        send_sems, recv_sems = refs[ns + n:ns + n + 2]
        for cp in _route_copies(mode, n, src_refs, land_refs, send_sems, recv_sems):
            cp.wait_send()
            cp.wait_recv()

    bufs = list(srcs) + list(lands)
    res = pl.pallas_call(
        body, name=name,
        out_shape=tuple(pltpu.HBM(b.shape, b.dtype) for b in bufs),
        in_specs=[_HBM] * len(bufs) + [_SEM, _SEM] + [_any()] * len(after), out_specs=tuple([_HBM] * len(bufs)),
        input_output_aliases={i: i for i in range(len(bufs))},
        compiler_params=pltpu.CompilerParams(has_side_effects=_EFFECT),
    )(*bufs, send_sems, recv_sems, *after)
    return res[0:ns], res[ns:ns + n]


def _unshard_in(a_in, a_small, token):
    def body(a_ref, s_ref, token_ref, w_ref, meta_ref, cw_ref, w2_ref):
        w_ref[:, D_IN:D_INP] = jnp.zeros((D, D_INP - D_IN), BF16)
        w2_ref[...] = jnp.zeros_like(w2_ref)
        for d in range(N_DEV):
            w_ref[:, d * W_IN_S:(d + 1) * W_IN_S] = a_ref[d]
            meta_ref[:, d * 128:(d + 1) * 128] = s_ref[d, 0:N_META, :]
            cw_ref[:, d * CONV_S:(d + 1) * CONV_S] = s_ref[d, CONV_ROW:CONV_ROW + 32, 0:CONV_S]
            w2_ref[0:RANK, d * GATE_S:(d + 1) * GATE_S] = s_ref[d, GATE_ROW:GATE_ROW + RANK, 0:GATE_S].astype(BF16)

    return pl.pallas_call(
        body, name="unshard_in",
        out_shape=[jax.ShapeDtypeStruct((D, D_INP), BF16), jax.ShapeDtypeStruct((N_META, D), F32),
                   jax.ShapeDtypeStruct((32, C_CONV), F32), jax.ShapeDtypeStruct((RANK_P, GLA_K), BF16)],
        compiler_params=pltpu.CompilerParams(vmem_limit_bytes=VMEM_LIMIT),
    )(a_in, a_small, token)


def _pack_small(g):
    def body(meta_ref, cw_ref, w2_ref, in_vec, ffn_vec, conv_vec, gla_vec, sp, vp):
        sp[...] = jnp.zeros_like(sp)
        vp[...] = jnp.zeros_like(vp)
        for d in range(N_DEV):
            sp[d, 0:N_META, :] = meta_ref[:, d * 128:(d + 1) * 128]
            sp[d, CONV_ROW:CONV_ROW + 32, 0:CONV_S] = cw_ref[:, d * CONV_S:(d + 1) * CONV_S]
            sp[d, GATE_ROW:GATE_ROW + RANK, 0:GATE_S] = w2_ref[0:RANK, d * GATE_S:(d + 1) * GATE_S]
            vp[d, 0:1, :] = in_vec[0:1, :]
            vp[d, 1:4, 0:C_CONV] = conv_vec[0:3, :]
            vp[d, 4:5, 0:GLA_K] = gla_vec[0:1, :]
            vp[d, 5:6, 0:GLA_DV] = gla_vec[1:2, 0:GLA_DV]
            vp[d, 6:7, :] = ffn_vec[1:2, :]
            vp[d, 7:8, :] = ffn_vec[0:1, :]
            vp[d, LOSS_ROW:LOSS_ROW + 1, :] = ffn_vec[2:3, :]

    return pl.pallas_call(
        body, name="pack_small",
        out_shape=[jax.ShapeDtypeStruct((N_DEV, SMALL_PACK, 128), F32), jax.ShapeDtypeStruct((N_DEV, VEC_ROWS, D), F32)],
    )(g["meta"], g["conv_w"], g["w2"], g["in_vec"], g["ffn_vec"], g["conv_vec"], g["gla_vec"])


def _adamw(w, g, m, v):
    m = ADAM_B1 * m + (1.0 - ADAM_B1) * g
    v = ADAM_B2 * v + (1.0 - ADAM_B2) * (g * g)
    m_hat = m / (1.0 - ADAM_B1 ** ADAM_STEP)
    v_hat = v / (1.0 - ADAM_B2 ** ADAM_STEP)
    return -ADAM_LR * (m_hat / (jnp.sqrt(v_hat) + ADAM_EPS) + ADAM_WD * w), m, v


def _update_matrix(recv, own, me, w, m, v, name):
    _, r, c = recv.shape
    tr = _row_tile(r, 256)

    def body(me_ref, recv_ref, own_ref, w_ref, m_ref, v_ref, g_ref, d_ref, nm_ref, nv_ref):
        g = jnp.zeros((tr, c), F32)
        for s in range(N_DEV):
            g = g + jnp.where(me_ref[0] == s, own_ref[...], recv_ref[s]).astype(F32)
        g_ref[...] = g
        d_ref[...], nm_ref[...], nv_ref[...] = _adamw(w_ref[...], g, m_ref[...], v_ref[...])

    one = pl.BlockSpec((None, tr, c), lambda i, me_ref: (0, i, 0))
    return pl.pallas_call(
        body, name=name,
        grid_spec=pltpu.PrefetchScalarGridSpec(
            num_scalar_prefetch=1, grid=(r // tr,),
            in_specs=[pl.BlockSpec((N_DEV, tr, c), lambda i, me_ref: (0, i, 0)),
                      pl.BlockSpec((None, tr, c), lambda i, me_ref: (me_ref[0], i, 0)), one, one, one],
            out_specs=[one] * 4),
        out_shape=[jax.ShapeDtypeStruct((1, r, c), F32)] * 4,
        compiler_params=_params(("parallel",)),
    )(me, recv, own, w, m, v)


_SMALL = ("meta_tokens", "conv_w", "gla_w_gate2") + tuple(n for n, _ in _VEC_ROWS)


def _update_small(me, srecv, vrecv, sown, vown, w, m, v):
    n = len(_SMALL)

    def body(*refs):
        me_ref, s_ref, v_ref, so_ref, vo_ref = refs[0:5]
        w_refs, m_refs, v_refs = refs[5:5 + n], refs[5 + n:5 + 2 * n], refs[5 + 2 * n:5 + 3 * n]
        outs = refs[5 + 3 * n:]
        ssum = jnp.zeros((SMALL_PACK, 128), F32)
        vsum = jnp.zeros((VEC_ROWS, D), F32)
        for s in range(N_DEV):
            ssum = ssum + jnp.where(me_ref[0] == s, so_ref[s], s_ref[s])
            vsum = vsum + jnp.where(me_ref[0] == s, vo_ref[s], v_ref[s])
        grads = [ssum[0:N_META, :], ssum[CONV_ROW:CONV_ROW + CONV_W, 0:CONV_S], ssum[GATE_ROW:GATE_ROW + RANK, 0:GATE_S]]
        grads += [vsum[i:i + 1, 0:width] for i, (_, width) in enumerate(_VEC_ROWS)]
        for i, g in enumerate(grads):
            d, nm, nv = _adamw(w_refs[i][...], g, m_refs[i][...], v_refs[i][...])
            outs[i][...] = g
            outs[n + i][...] = d
            outs[2 * n + i][...] = nm
            outs[3 * n + i][...] = nv
        outs[4 * n][...] = vsum[LOSS_ROW:LOSS_ROW + 1, 0:128]

    shapes = [jax.ShapeDtypeStruct(t.shape, F32) for t in w]
    res = pl.pallas_call(
        body, name="update_small", out_shape=shapes * 4 + [jax.ShapeDtypeStruct((1, 128), F32)],
        in_specs=[pl.BlockSpec(memory_space=pltpu.SMEM)] + [_whole_vmem()] * (4 + 3 * n),
    )(me, srecv, vrecv, sown, vown, *w, *m, *v)
    return res[0:n], res[n:2 * n], res[2 * n:3 * n], res[3 * n:4 * n], res[4 * n]


_WEIGHTS = ("meta_tokens", "norm_mix_g", "w_in", "conv_w", "conv_b", "conv_ln_g", "conv_ln_b", "gla_w_gate2", "gla_gate_b",
            "gla_norm_g", "w_out", "norm_ffn_g", "w_ffn_gate", "w_ffn_up", "w_ffn_down", "norm_final_g")
_MATRICES = ("w_in", "w_out", "w_ffn_gate", "w_ffn_up", "w_ffn_down")
_TRANSPOSED = ("w_ffn_gate", "w_ffn_up")


def kernel(x, meta_tokens, norm_mix_g, w_in, conv_w, conv_b, conv_ln_g, conv_ln_b, gla_w_gate2, gla_gate_b, gla_norm_g, w_out, norm_ffn_g, w_ffn_gate, w_ffn_up, w_ffn_down, norm_final_g, loss_target, m_meta_tokens, m_norm_mix_g, m_w_in, m_conv_w, m_conv_b, m_conv_ln_g, m_conv_ln_b, m_gla_w_gate2, m_gla_gate_b, m_gla_norm_g, m_w_out, m_norm_ffn_g, m_w_ffn_gate, m_w_ffn_up, m_w_ffn_down, m_norm_final_g, v_meta_tokens, v_norm_mix_g, v_w_in, v_conv_w, v_conv_b, v_conv_ln_g, v_conv_ln_b, v_gla_w_gate2, v_gla_gate_b, v_gla_norm_g, v_w_out, v_norm_ffn_g, v_w_ffn_gate, v_w_ffn_up, v_w_ffn_down, v_norm_final_g):
    given = dict(locals())
    two_d = lambda a: a.reshape(1, -1) if a.ndim == 1 else a.reshape(a.shape[-2:])
    fams = [{n: given[pre + n] for n in _WEIGHTS} for pre in ("", "m_", "v_")]
    for f in fams:
        for n in _TRANSPOSED:
            f[n] = f[n].transpose(0, 2, 1)
    w = fams[0]

    lands, shards = _stage([two_d(w[n]) for n in _MATRICES], w["meta_tokens"], two_d(w["conv_w"]), two_d(w["gla_w_gate2"]))
    soon, later = (0, 5), (1, 2, 3, 4)
    pick = lambda seq, idx: [seq[i] for i in idx]
    first = _send_start("gather_first_start", pick(shards, soon), pick(lands, soon), "first", norm_mix_g)
    ffn_first = _send_start("gather_ffn_first_start", pick(shards, later), pick(lands, later), "first", first[4])
    h0, tgt_p = _pad_rows(x, loss_target)
    _, arrived = _send_wait("gather_first_wait", *first[0:4], "first", (h0, tgt_p))
    forward = _send_start("gather_forward_start", [], arrived, "forward", ffn_first[4])
    _, (a_in, a_small) = _send_wait("gather_forward_wait", *forward[0:4], "forward", forward[4])
    w_in, meta, conv_taps, w2 = _unshard_in(a_in, a_small, forward[4])
    p = dict(meta=meta, conv_w=conv_taps, w2=w2, w_in=w_in, g1=norm_mix_g, conv_b=conv_b, ln_g=conv_ln_g, ln_b=conv_ln_b,
             gb=gla_gate_b, ng=gla_norm_g, g2=norm_ffn_g, g3=two_d(norm_final_g))
    passed = {}

    def pass_on(after):
        _, arrived_ffn = _send_wait("gather_ffn_first_wait", *ffn_first[0:4], "first", after)
        passed["sent"] = _send_start("gather_ffn_forward_start", [], arrived_ffn, "forward", after)
        return passed["sent"][4]

    def late_weights(after):
        _, (a_out, a_g, a_u, a_d) = _send_wait("gather_ffn_forward_wait", *passed["sent"][0:4], "forward", after)
        return a_out.reshape(D, D), a_g.reshape(D_FF, D), a_u.reshape(D_FF, D), a_d.reshape(D_FF, D)

    sent = {}

    def send_early(tag, mats):
        landing = [_in_hbm(lax.empty(m_.shape, m_.dtype)) for m_ in mats]
        sent[tag] = _send_start("scatter_" + tag + "_start", mats, landing, "scatter", norm_mix_g)
        return sent[tag][4]

    grad_x, g = _local_step(h0, tgt_p, p, pass_on, late_weights, send_early)

    token = send_early("small", list(_pack_small(g)))
    x_, y_, c_ = _position()
    me = (4 * x_ + 2 * y_ + c_).astype(jnp.int32).reshape(1)
    res = {}
    for tag, names in (("ffn", ("w_ffn_gate", "w_ffn_up", "w_ffn_down")), ("out", ("w_out",)), ("in", ("w_in",))):
        own, recv = _send_wait("scatter_" + tag + "_wait", *sent[tag][0:4], "scatter", token)
        for n, o_, r_ in zip(names, own, recv):
            res[n] = _update_matrix(r_, o_, me, *[f[n] for f in fams], "update_" + n)
            token = res[n][1]
    (sown, vown), (srecv, vrecv) = _send_wait("scatter_small_wait", *sent["small"][0:4], "scatter", token)
    small = _update_small(me, srecv, vrecv, sown, vown, *[[two_d(f[n]) for n in _SMALL] for f in fams])
    for i, n in enumerate(_SMALL):
        res[n] = [fam[i].reshape(w[n].shape) for fam in small[0:4]]
    for n in _TRANSPOSED:
        res[n] = [t.transpose(0, 2, 1) for t in res[n]]
    outs = [small[4][0, 0], grad_x]
    for k in range(4):
        outs += [res[n][k] for n in _WEIGHTS]
    return tuple(outs)
```

```python
import functools

import jax
import jax.numpy as jnp
from jax import lax
from jax.experimental import pallas as pl
from jax.experimental.pallas import tpu as pltpu

F32 = jnp.float32
BF16 = jnp.bfloat16

D = 1024
N_META = 16
C_CONV = 512
CONV_W = 31
GLA_H = 4
GLA_DK = 64
GLA_DV = 128
GLA_K = GLA_H * GLA_DK
GLA_V = GLA_H * GLA_DV
RANK = 16
RANK_P = 128
TAU = 16.0
CHUNK = 64
LEAD = CHUNK
ZROWS = LEAD - N_META
D_IN = 2 * C_CONV + 2 * GLA_K + 2 * GLA_V + RANK
D_INP = D_IN - RANK + RANK_P
D_FF = 2816
FF_CHUNK = 1408
FF_SPLIT = (0, 1536, D_FF)
RMS_EPS = 1e-6
LN_EPS = 1e-5
N_DEV = 8

ADAM_LR = 0.001
ADAM_B1 = 0.9
ADAM_B2 = 0.999
ADAM_EPS = 1e-08
ADAM_WD = 0.01
ADAM_STEP = 10

VMEM_LIMIT = 60 * 1024 * 1024
ROW_TILE = 1056
FFN_ROW_TILE = 352
DW_ROW_TILE = 1408
MESH = pl.DeviceIdType.MESH

_NN = (((1,), (0,)), ((), ()))
_NT = (((1,), (1,)), ((), ()))
_TN = (((0,), (0,)), ((), ()))


def _dot(a, b, dims=_NN):
    return lax.dot_general(a, b, dims, preferred_element_type=F32)


def _sigmoid(x):
    return 1.0 / (1.0 + jnp.exp(-x))


def _row_tile(rows, target):
    best = None
    for t in range(16, min(rows, target) + 1, 16):
        if rows % t == 0:
            best = t
    assert best is not None, rows
    return best


def _params(sem=None):
    return pltpu.CompilerParams(dimension_semantics=sem, vmem_limit_bytes=VMEM_LIMIT)


def _whole_vmem():
    return pl.BlockSpec(memory_space=pltpu.VMEM)


def _rows(tm, width):
    return pl.BlockSpec((tm, width), lambda i: (i, 0))


def _fixed(shape):
    return pl.BlockSpec(shape, lambda *_: (0,) * len(shape))


def _fwd_inproj(h0, g1, w_in):
    rows = h0.shape[0]
    tm = _row_tile(rows, ROW_TILE)

    def body(h_ref, g_ref, w_ref, uc_ref, qk_ref, vg_ref, lr_ref, n1_ref):
        h = h_ref[...]
        r = lax.rsqrt(jnp.mean(h * h, axis=-1, keepdims=True) + RMS_EPS)
        n = (h * r * g_ref[...]).astype(BF16)
        n1_ref[...] = n
        uc_ref[...] = _dot(n, w_ref[:, 0:1024])
        qk_ref[...] = _dot(n, w_ref[:, 1024:1536])
        vg_ref[...] = _dot(n, w_ref[:, 1536:2560])
        lr_ref[...] = _dot(n, w_ref[:, 2560:2688])

    return pl.pallas_call(
        body, name="fwd_inproj", grid=(rows // tm,),
        in_specs=[_rows(tm, D), _fixed((1, D)), _whole_vmem()],
        out_specs=[_rows(tm, 1024), _rows(tm, 512), _rows(tm, 1024), _rows(tm, RANK_P), _rows(tm, D)],
        out_shape=[jax.ShapeDtypeStruct((rows, 1024), F32), jax.ShapeDtypeStruct((rows, 512), F32),
                   jax.ShapeDtypeStruct((rows, 1024), F32), jax.ShapeDtypeStruct((rows, RANK_P), F32),
                   jax.ShapeDtypeStruct((rows, D), BF16)],
        compiler_params=_params(("parallel",)),
    )(h0, g1, w_in)


def _fwd_outproj(yc, yg, h0, w_out, g2, token):
    rows = h0.shape[0]
    tm = _row_tile(rows, ROW_TILE)

    def body(yc_ref, yg_ref, h_ref, w_ref, g_ref, token_ref, h1_ref, n2_ref):
        h1 = h_ref[...] + _dot(yc_ref[...], w_ref[0:C_CONV, :]) + _dot(yg_ref[...], w_ref[C_CONV:D, :])
        h1_ref[...] = h1
        r = lax.rsqrt(jnp.mean(h1 * h1, axis=-1, keepdims=True) + RMS_EPS)
        n2_ref[...] = (h1 * r * g_ref[...]).astype(BF16)

    return pl.pallas_call(
        body, name="fwd_outproj", grid=(rows // tm,),
        in_specs=[_rows(tm, C_CONV), _rows(tm, GLA_V), _rows(tm, D), _whole_vmem(), _fixed((1, D)), _fixed((8, 128))],
        out_specs=[_rows(tm, D), _rows(tm, D)],
        out_shape=[jax.ShapeDtypeStruct((rows, D), F32), jax.ShapeDtypeStruct((rows, D), BF16)],
        compiler_params=_params(("parallel",)),
    )(yc, yg, h0, w_out, g2, token)


def _ffn_rows(h1, n2, tgt, wg, wu, wd, g2, g3, rows_per_example):
    rows = h1.shape[0]
    tm = _row_tile(rows, FFN_ROW_TILE)
    ff_blocks = [slice(lo, hi) for lo, hi in zip(FF_SPLIT[:-1], FF_SPLIT[1:])]

    def body(h1_ref, n2_ref, t_ref, wg_ref, wu_ref, wd_ref, g2_ref, g3_ref,
             f_ref, da_ref, db_ref, dh2_ref, dh1_ref, dh1b_ref, part_ref):
        i = pl.program_id(0)
        n2 = n2_ref[...]
        y2 = jnp.zeros((tm, D), F32)
        for cs in ff_blocks:
            a = _dot(n2, wg_ref[cs, :], _NT)
            b = _dot(n2, wu_ref[cs, :], _NT)
            f = (a * _sigmoid(a) * b).astype(BF16)
            f_ref[:, cs] = f
            da_ref[:, cs] = a.astype(BF16)
            db_ref[:, cs] = b.astype(BF16)
            y2 = y2 + _dot(f, wd_ref[cs, :])
        h1 = h1_ref[...]
        h2 = h1 + y2
        r3 = lax.rsqrt(jnp.mean(h2 * h2, axis=-1, keepdims=True) + RMS_EPS)
        xh3 = h2 * r3
        g3 = g3_ref[...]
        pos = (i * tm + lax.broadcasted_iota(jnp.int32, (tm, 1), 0)) % rows_per_example
        valid = pos >= LEAD
        err = jnp.where(valid, xh3 * g3 - t_ref[...], 0.0)
        loss = 0.5 / D * jnp.sum(jnp.sum(err * err, axis=-1, keepdims=True), axis=0, keepdims=True)
        dy = err * (1.0 / D)
        dg3 = jnp.sum(dy * xh3, axis=0, keepdims=True)
        dxh = dy * g3
        dh2 = r3 * (dxh - xh3 * jnp.mean(dxh * xh3, axis=-1, keepdims=True))
        dh2b = dh2.astype(BF16)
        dh2_ref[...] = dh2b
        dn2 = jnp.zeros((tm, D), F32)
        for cs in ff_blocks:
            df = _dot(dh2b, wd_ref[cs, :], _NT)
            a = da_ref[:, cs].astype(F32)
            b = db_ref[:, cs].astype(F32)
            sg = _sigmoid(a)
            da = (df * b * sg * (1.0 + a * (1.0 - sg))).astype(BF16)
            db = (df * a * sg).astype(BF16)
            da_ref[:, cs] = da
            db_ref[:, cs] = db
            dn2 = dn2 + _dot(da, wg_ref[cs, :]) + _dot(db, wu_ref[cs, :])
        r2 = lax.rsqrt(jnp.mean(h1 * h1, axis=-1, keepdims=True) + RMS_EPS)
        xh2 = h1 * r2
        dg2 = jnp.sum(dn2 * xh2, axis=0, keepdims=True)
        dxh2 = dn2 * g2_ref[...]
        dh1 = dh2 + r2 * (dxh2 - xh2 * jnp.mean(dxh2 * xh2, axis=-1, keepdims=True))
        dh1_ref[...] = dh1
        dh1b_ref[...] = dh1.astype(BF16)

        @pl.when(i == 0)
        def _():
            part_ref[...] = jnp.zeros_like(part_ref)

        part_ref[0:1, :] += dg3
        part_ref[1:2, :] += dg2
        part_ref[2:3, :] += jnp.broadcast_to(loss, (1, D))

    return pl.pallas_call(
        body, name="ffn_rows", grid=(rows // tm,),
        in_specs=[_rows(tm, D), _rows(tm, D), _rows(tm, D), _whole_vmem(), _whole_vmem(), _whole_vmem(),
                  _fixed((1, D)), _fixed((1, D))],
        out_specs=[_rows(tm, D_FF), _rows(tm, D_FF), _rows(tm, D_FF), _rows(tm, D), _rows(tm, D), _rows(tm, D),
                   _fixed((8, D))],
        out_shape=[jax.ShapeDtypeStruct((rows, D_FF), BF16)] * 3
        + [jax.ShapeDtypeStruct((rows, D), BF16), jax.ShapeDtypeStruct((rows, D), F32),
           jax.ShapeDtypeStruct((rows, D), BF16), jax.ShapeDtypeStruct((8, D), F32)],
        compiler_params=_params(("arbitrary",)),
    )(h1, n2, tgt, wg, wu, wd, g2, g3)


def _bwd_outproj(dh1b, w_out, token):
    rows = dh1b.shape[0]
    tm = _row_tile(rows, ROW_TILE)

    def body(d_ref, w_ref, token_ref, dyc_ref, dyg_ref):
        d = d_ref[...]
        dyc_ref[...] = _dot(d, w_ref[0:C_CONV, :], _NT)
        dyg_ref[...] = _dot(d, w_ref[C_CONV:D, :], _NT)

    return pl.pallas_call(
        body, name="bwd_outproj", grid=(rows // tm,),
        in_specs=[_rows(tm, D), _whole_vmem(), _fixed((8, 128))],
        out_specs=[_rows(tm, C_CONV), _rows(tm, GLA_V)],
        out_shape=[jax.ShapeDtypeStruct((rows, C_CONV), F32), jax.ShapeDtypeStruct((rows, GLA_V), F32)],
        compiler_params=_params(("parallel",)),
    )(dh1b, w_out, token)


def _bwd_inproj(duc, dqk, dvg, dlr, dh1, h0, w_in, g1, token, rows_per_example):
    rows = h0.shape[0]
    n_ex = rows // rows_per_example
    tm = _row_tile(rows_per_example, ROW_TILE)
    tiles_per_example = rows_per_example // tm
    n_steps = rows // tm

    def body(duc_ref, dqk_ref, dvg_ref, dlr_ref, dh1_ref, h_ref, w_ref, g_ref, token_ref, gx_ref, part_ref, dmeta_ref,
             buf_ref, sems):
        dn = (_dot(duc_ref[...], w_ref[:, 0:1024], _NT) + _dot(dqk_ref[...], w_ref[:, 1024:1536], _NT)
              + _dot(dvg_ref[...], w_ref[:, 1536:2560], _NT) + _dot(dlr_ref[...], w_ref[:, 2560:2688], _NT))
        h = h_ref[...]
        r = lax.rsqrt(jnp.mean(h * h, axis=-1, keepdims=True) + RMS_EPS)
        xh = h * r
        dg = jnp.sum(dn * xh, axis=0, keepdims=True)
        dxh = dn * g_ref[...]
        dh0 = dh1_ref[...] + r * (dxh - xh * jnp.mean(dxh * xh, axis=-1, keepdims=True))
        i = pl.program_id(0)

        def copies(step):
            slot, b, j = step % 2, step // tiles_per_example, step % tiles_per_example
            out = [(j == 0, pltpu.make_async_copy(buf_ref.at[slot, pl.ds(LEAD, tm - LEAD)],
                                                   gx_ref.at[b, pl.ds(0, tm - LEAD)], sems.at[slot]))]
            if tiles_per_example > 1:
                out.append((j != 0, pltpu.make_async_copy(
                    buf_ref.at[slot], gx_ref.at[b, pl.ds(pl.multiple_of(jnp.maximum(j * tm - LEAD, 0), 8), tm)],
                    sems.at[slot])))
            return out

        def each(step, act):
            for cond, cp in copies(step):
                pl.when(cond)(functools.partial(act, cp))

        @pl.when(i >= 2)
        def _():
            each(i - 2, lambda cp: cp.wait())

        buf_ref[i % 2] = dh0
        each(i, lambda cp: cp.start())

        @pl.when(i == n_steps - 1)
        def _():
            each(i, lambda cp: cp.wait())
            if n_steps > 1:
                each(i - 1, lambda cp: cp.wait())

        @pl.when(i == 0)
        def _():
            part_ref[...] = jnp.zeros_like(part_ref)
            dmeta_ref[...] = jnp.zeros_like(dmeta_ref)

        part_ref[0:1, :] += dg

        @pl.when(i % tiles_per_example == 0)
        def _():
            dmeta_ref[...] += dh0[ZROWS:LEAD, :]

    return pl.pallas_call(
        body, name="bwd_inproj", grid=(n_steps,),
        in_specs=[_rows(tm, 1024), _rows(tm, 512), _rows(tm, 1024), _rows(tm, RANK_P), _rows(tm, D), _rows(tm, D),
                  _whole_vmem(), _fixed((1, D)), _fixed((8, 128))],
        out_specs=[_any(), _fixed((8, D)), _fixed((N_META, D))],
        out_shape=[jax.ShapeDtypeStruct((n_ex, rows_per_example - LEAD, D), F32), jax.ShapeDtypeStruct((8, D), F32),
                   jax.ShapeDtypeStruct((N_META, D), F32)],
        scratch_shapes=[pltpu.VMEM((2, tm, D), F32), pltpu.SemaphoreType.DMA((2,))],
        compiler_params=_params(("arbitrary",)),
    )(duc, dqk, dvg, dlr, dh1, h0, w_in, g1, token)


def _dw_blocked(a, bs, width, name):
    rows, m = a.shape
    ws = [b.shape[1] for b in bs]
    assert sum(ws) >= N_DEV * width
    tk = _row_tile(rows, DW_ROW_TILE)
    nk = rows // tk

    def body(a_ref, *refs):
        b_refs, o_ref, acc_ref = refs[:len(bs)], refs[len(bs)], refs[len(bs) + 1]
        k = pl.program_id(0)

        @pl.when(k == 0)
        def _():
            acc_ref[...] = jnp.zeros_like(acc_ref)

        at = a_ref[...].T
        off = 0
        for b_ref, w in zip(b_refs, ws):
            acc_ref[:, off:off + w] += _dot(at, b_ref[...])
            off += w

        @pl.when(k == nk - 1)
        def _():
            for d in range(N_DEV):
                o_ref[d] = acc_ref[:, d * width:(d + 1) * width].astype(BF16)

    return pl.pallas_call(
        body, name=name, grid=(nk,),
        in_specs=[_rows(tk, m)] + [_rows(tk, w) for w in ws],
        out_specs=_fixed((N_DEV, m, width)),
        out_shape=jax.ShapeDtypeStruct((N_DEV, m, width), BF16),
        scratch_shapes=[pltpu.VMEM((m, sum(ws)), F32)],
        compiler_params=_params(("arbitrary",)),
    )(a, *bs)


def _dw_out(yc, yg, dh1b):
    rows = yc.shape[0]
    tk = _row_tile(rows, DW_ROW_TILE)
    nk = rows // tk

    def body(yc_ref, yg_ref, d_ref, o_ref, acc_ref):
        k = pl.program_id(0)

        @pl.when(k == 0)
        def _():
            acc_ref[...] = jnp.zeros_like(acc_ref)

        d = d_ref[...]
        acc_ref[0:C_CONV, :] += _dot(yc_ref[...], d, _TN)
        acc_ref[C_CONV:D, :] += _dot(yg_ref[...], d, _TN)

        @pl.when(k == nk - 1)
        def _():
            o_ref[...] = acc_ref[...].astype(BF16)

    return pl.pallas_call(
        body, name="dw_out", grid=(nk,),
        in_specs=[_rows(tk, C_CONV), _rows(tk, GLA_V), _rows(tk, D)],
        out_specs=_fixed((D, D)), out_shape=jax.ShapeDtypeStruct((D, D), BF16),
        scratch_shapes=[pltpu.VMEM((D, D), F32)],
        compiler_params=_params(("arbitrary",)),
    )(yc, yg, dh1b)


def _matmul_tn(a, b, name):
    rows, m = a.shape
    n = b.shape[1]
    tk = _row_tile(rows, DW_ROW_TILE)
    tn = n if n <= 1024 else FF_CHUNK
    tm_ = m if m <= 1024 else FF_CHUNK
    assert n % tn == 0 and m % tm_ == 0
    nk = rows // tk

    def body(a_ref, b_ref, o_ref, acc_ref):
        k = pl.program_id(2)

        @pl.when(k == 0)
        def _():
            acc_ref[...] = jnp.zeros_like(acc_ref)

        acc_ref[...] += _dot(a_ref[...], b_ref[...], _TN)

        @pl.when(k == nk - 1)
        def _():
            o_ref[...] = acc_ref[...].astype(BF16)

    return pl.pallas_call(
        body, name=name, grid=(m // tm_, n // tn, nk),
        in_specs=[pl.BlockSpec((tk, tm_), lambda i, j, k: (k, i)), pl.BlockSpec((tk, tn), lambda i, j, k: (k, j))],
        out_specs=pl.BlockSpec((tm_, tn), lambda i, j, k: (i, j)),
        out_shape=jax.ShapeDtypeStruct((m, n), BF16),
        scratch_shapes=[pltpu.VMEM((tm_, tn), F32)],
        compiler_params=_params(("parallel", "parallel", "arbitrary")),
    )(a, b)


HALO = 32
LANES = 128


def _shifted(win, offsets):
    for r in range(8):
        js = [j for j, k in enumerate(offsets) if k % 8 == r]
        if js:
            rolled = win if r == 0 else pltpu.roll(win, CHUNK + HALO - r, 0)
            for j in js:
                yield j, rolled[offsets[j] - r:offsets[j] - r + CHUNK]


def _glu_into(uc_ref, vs_ref, n_chunk):
    vs_ref[0:CHUNK, :] = jnp.zeros((CHUNK, C_CONV), F32)

    def glu(i, carry):
        base = pl.multiple_of(i * CHUNK, CHUNK)
        val = uc_ref[pl.ds(base, CHUNK), 0:C_CONV]
        gate = uc_ref[pl.ds(base, CHUNK), C_CONV:2 * C_CONV]
        vs_ref[pl.ds(base + CHUNK, CHUNK), :] = val * _sigmoid(gate)
        return carry

    lax.fori_loop(0, n_chunk, glu, 0)


def _fwd_conv(uc, conv_w, conv_b, ln_g, ln_b, token, n_ex):
    rows = uc.shape[0]
    lp = rows // n_ex
    n_chunk = lp // CHUNK

    def body(uc_ref, w_ref, b_ref, lg_ref, lb_ref, token_ref, ypre_ref, yc_ref, vs_ref):
        _glu_into(uc_ref, vs_ref, n_chunk)

        def conv(i, carry):
            base = pl.multiple_of(i * CHUNK, CHUNK)
            for lb in range(C_CONV // LANES):
                ls = slice(lb * LANES, (lb + 1) * LANES)
                win = vs_ref[pl.ds(base + CHUNK - HALO, CHUNK + HALO), ls]
                acc = jnp.broadcast_to(b_ref[:, ls], (CHUNK, LANES))
                for j, rows_j in _shifted(win, [HALO - (CONV_W - 1) + j for j in range(CONV_W)]):
                    acc = acc + w_ref[j:j + 1, ls] * rows_j
                ypre_ref[pl.ds(base, CHUNK), ls] = acc
            y = ypre_ref[pl.ds(base, CHUNK), :]
            mu = jnp.mean(y, axis=-1, keepdims=True)
            yc_ = y - mu
            rstd = lax.rsqrt(jnp.mean(yc_ * yc_, axis=-1, keepdims=True) + LN_EPS)
            s = yc_ * rstd * lg_ref[...] + lb_ref[...]
            yc_ref[pl.ds(base, CHUNK), :] = (s * _sigmoid(s)).astype(BF16)
            return carry

        lax.fori_loop(0, n_chunk, conv, 0)

    ex = lambda w: pl.BlockSpec((lp, w), lambda b: (b, 0))
    return pl.pallas_call(
        body, name="fwd_conv", grid=(n_ex,),
        in_specs=[ex(2 * C_CONV), _fixed((32, C_CONV)), _fixed((1, C_CONV)), _fixed((1, C_CONV)), _fixed((1, C_CONV)),
                  _fixed((8, 128))],
        out_specs=[ex(C_CONV), ex(C_CONV)],
        out_shape=[jax.ShapeDtypeStruct((rows, C_CONV), F32), jax.ShapeDtypeStruct((rows, C_CONV), BF16)],
        scratch_shapes=[pltpu.VMEM((lp + CHUNK, C_CONV), F32)],
        compiler_params=_params(("parallel",)),
    )(uc, conv_w, conv_b, ln_g, ln_b, token)


def _bwd_conv(uc, ypre, dyc, conv_w, ln_g, ln_b, token, n_ex):
    rows = uc.shape[0]
    lp = rows // n_ex
    n_chunk = lp // CHUNK

    def body(uc_ref, ypre_ref, dyc_ref, w_ref, lg_ref, lb_ref, token_ref, duc_ref, dw_ref, dvec_ref, vs_ref, dys_ref,
             dwacc_ref):
        _glu_into(uc_ref, vs_ref, n_chunk)
        dys_ref[pl.ds(lp, CHUNK), :] = jnp.zeros((CHUNK, C_CONV), F32)
        dwacc_ref[...] = jnp.zeros_like(dwacc_ref)

        def ln_bwd(i, carry):
            dcb, dlg, dlb = carry
            base = pl.multiple_of(i * CHUNK, CHUNK)
            y = ypre_ref[pl.ds(base, CHUNK), :]
            mu = jnp.mean(y, axis=-1, keepdims=True)
            yc_ = y - mu
            rstd = lax.rsqrt(jnp.mean(yc_ * yc_, axis=-1, keepdims=True) + LN_EPS)
            xh = yc_ * rstd
            s = xh * lg_ref[...] + lb_ref[...]
            sg = _sigmoid(s)
            ds = dyc_ref[pl.ds(base, CHUNK), :] * (sg * (1.0 + s * (1.0 - sg)))
            dxh = ds * lg_ref[...]
            dy = rstd * (dxh - jnp.mean(dxh, axis=-1, keepdims=True) - xh * jnp.mean(dxh * xh, axis=-1, keepdims=True))
            dys_ref[pl.ds(base, CHUNK), :] = dy
            return (dcb + jnp.sum(dy, axis=0, keepdims=True), dlg + jnp.sum(ds * xh, axis=0, keepdims=True),
                    dlb + jnp.sum(ds, axis=0, keepdims=True))

        zero = jnp.zeros((1, C_CONV), F32)
        dcb, dlg, dlb = lax.fori_loop(0, n_chunk, ln_bwd, (zero, zero, zero))

        @pl.when(pl.program_id(0) == 0)
        def _():
            dvec_ref[...] = jnp.zeros_like(dvec_ref)
            dw_ref[...] = jnp.zeros_like(dw_ref)

        dvec_ref[0:1, :] += dcb
        dvec_ref[1:2, :] += dlg
        dvec_ref[2:3, :] += dlb

        def taps(i, carry):
            base = pl.multiple_of(i * CHUNK, CHUNK)
            for lb in range(C_CONV // LANES):
                ls = slice(lb * LANES, (lb + 1) * LANES)
                dwin = dys_ref[pl.ds(base, CHUNK + HALO), ls]
                vwin = vs_ref[pl.ds(base + CHUNK - HALO, CHUNK + HALO), ls]
                dy = dwin[0:CHUNK]
                acc = jnp.zeros((CHUNK, LANES), F32)
                for j, rows_j in _shifted(dwin, [CONV_W - 1 - j for j in range(CONV_W)]):
                    acc = acc + w_ref[j:j + 1, ls] * rows_j
                for j, rows_j in _shifted(vwin, [HALO - (CONV_W - 1) + j for j in range(CONV_W)]):
                    dwacc_ref[8 * j:8 * j + 8, ls] += jnp.sum((dy * rows_j).reshape(CHUNK // 8, 8, LANES), axis=0)
                val = uc_ref[pl.ds(base, CHUNK), ls]
                gate = uc_ref[pl.ds(base, CHUNK), C_CONV + lb * LANES:C_CONV + (lb + 1) * LANES]
                sg = _sigmoid(gate)
                duc_ref[pl.ds(base, CHUNK), ls] = (acc * sg).astype(BF16)
                duc_ref[pl.ds(base, CHUNK), C_CONV + lb * LANES:C_CONV + (lb + 1) * LANES] = (
                    acc * val * sg * (1.0 - sg)).astype(BF16)
            return carry

        lax.fori_loop(0, n_chunk, taps, 0)
        for j in range(CONV_W):
            dw_ref[j:j + 1, :] += jnp.sum(dwacc_ref[8 * j:8 * j + 8, :], axis=0, keepdims=True)

    ex = lambda w: pl.BlockSpec((lp, w), lambda b: (b, 0))
    return pl.pallas_call(
        body, name="bwd_conv", grid=(n_ex,),
        in_specs=[ex(2 * C_CONV), ex(C_CONV), ex(C_CONV), _fixed((32, C_CONV)), _fixed((1, C_CONV)), _fixed((1, C_CONV)),
                  _fixed((8, 128))],
        out_specs=[ex(2 * C_CONV), _fixed((32, C_CONV)), _fixed((8, C_CONV))],
        out_shape=[jax.ShapeDtypeStruct((rows, 2 * C_CONV), BF16), jax.ShapeDtypeStruct((32, C_CONV), F32),
                   jax.ShapeDtypeStruct((8, C_CONV), F32)],
        scratch_shapes=[pltpu.VMEM((lp + CHUNK, C_CONV), F32), pltpu.VMEM((lp + CHUNK, C_CONV), F32),
                        pltpu.VMEM((8 * 32, C_CONV), F32)],
        compiler_params=_params(("arbitrary",)),
    )(uc, ypre, dyc, conv_w, ln_g, ln_b, token)


def _seg_chunks(n_chunk):
    return max(c for c in (11, 3, 1) if n_chunk % c == 0)


def _block_mask(shape, row_block, lane_block):
    return (lax.broadcasted_iota(jnp.int32, shape, 0) // row_block) == (lax.broadcasted_iota(jnp.int32, shape, 1) // lane_block)


def _per_head_rows(x, mask):
    return jnp.where(mask, jnp.concatenate([x] * GLA_H, axis=0), 0)


def _fold_heads(full, lane_block):
    lane = lax.broadcasted_iota(jnp.int32, (1, full.shape[1]), 1) // lane_block
    out = jnp.where(lane == 0, full[0:CHUNK], 0.0)
    for h in range(1, GLA_H):
        out = out + jnp.where(lane == h, full[h * CHUNK:(h + 1) * CHUNK], 0.0)
    return out


def _causal_heads():
    return (lax.broadcasted_iota(jnp.int32, (CHUNK, GLA_H * CHUNK), 1) % CHUNK) <= lax.broadcasted_iota(
        jnp.int32, (CHUNK, GLA_H * CHUNK), 0)


def _cumsum_rows(x):
    row = lax.broadcasted_iota(jnp.int32, x.shape, 0)
    s = 1
    while s < CHUNK:
        x = x + jnp.where(row >= s, pltpu.roll(x, s, 0), 0.0)
        s *= 2
    return x


def _rev_cumsum_rows(x):
    row = lax.broadcasted_iota(jnp.int32, x.shape, 0)
    s = 1
    while s < CHUNK:
        x = x + jnp.where(row < CHUNK - s, pltpu.roll(x, CHUNK - s, 0), 0.0)
        s *= 2
    return x


def _gate_terms(lr_ref, w2_ref, gb_ref, rs, first_pos):
    z = _dot(lr_ref[rs, :].astype(BF16), w2_ref[...]) + gb_ref[...]
    la = (jnp.minimum(z, 0.0) - jnp.log(1.0 + jnp.exp(-jnp.abs(z)))) * (1.0 / TAU)
    pos = first_pos + lax.broadcasted_iota(jnp.int32, (CHUNK, 1), 0)
    live = pos >= ZROWS
    la = jnp.where(live, la, 0.0)
    return z, live, _cumsum_rows(la)


def _fwd_gla(qk, vg, lr, w2p, gb, ng, n_ex):
    rows = qk.shape[0]
    lp = rows // n_ex
    n_chunk = lp // CHUNK
    sc = _seg_chunks(n_chunk)
    n_seg = n_chunk // sc
    seg = sc * CHUNK

    def body(qk_ref, vg_ref, lr_ref, w2_ref, gb_ref, ng_ref, yg_ref, o_ref, st_ref, state_ref):
        sidx = pl.program_id(1)

        @pl.when(sidx == 0)
        def _():
            state_ref[...] = jnp.zeros_like(state_ref)

        causal = _causal_heads()
        k_mask = _block_mask((GLA_H * CHUNK, GLA_K), CHUNK, GLA_DK)
        v_mask = _block_mask((GLA_H * CHUNK, GLA_V), CHUNK, GLA_DV)
        s_mask = _block_mask((GLA_V, GLA_K), GLA_DV, GLA_DK)

        def chunk(ci, carry):
            base = pl.multiple_of(ci * CHUNK, CHUNK)
            rs = pl.ds(base, CHUNK)
            _, _, bcum = _gate_terms(lr_ref, w2_ref, gb_ref, rs, (sidx * sc + ci) * CHUNK)
            bl = bcum[CHUNK - 1:CHUNK, :]
            q = qk_ref[rs, 0:GLA_K]
            k = qk_ref[rs, GLA_K:2 * GLA_K]
            qt = (q * (GLA_DK ** -0.5) * jnp.exp(bcum)).astype(BF16)
            kt = (k * jnp.exp(-bcum)).astype(BF16)
            kh = (k * jnp.exp(bl - bcum)).astype(BF16)
            vb = vg_ref[rs, 0:GLA_V].astype(BF16)
            state = state_ref[...]
            st_ref[ci] = state
            a = jnp.where(causal, _dot(qt, _per_head_rows(kt, k_mask), _NT), 0.0)
            o = _dot(a.astype(BF16), _per_head_rows(vb, v_mask)) + _dot(qt, state.astype(BF16), _NT)
            o_ref[rs, :] = o
            for h in range(GLA_H):
                hs = slice(h * GLA_DV, (h + 1) * GLA_DV)
                oh = o[:, hs]
                ro = lax.rsqrt(jnp.mean(oh * oh, axis=-1, keepdims=True) + RMS_EPS)
                g = vg_ref[rs, GLA_V + h * GLA_DV:GLA_V + (h + 1) * GLA_DV]
                yg_ref[rs, hs] = (oh * ro * ng_ref[...] * g * _sigmoid(g)).astype(BF16)
            state_ref[...] = state * jnp.exp(bl) + jnp.where(s_mask, _dot(vb, kh, _TN), 0.0)
            return carry

        lax.fori_loop(0, sc, chunk, 0, unroll=True)

    sg = lambda w: pl.BlockSpec((seg, w), lambda b, s: (b * n_seg + s, 0))
    return pl.pallas_call(
        body, name="fwd_gla", grid=(n_ex, n_seg),
        in_specs=[sg(2 * GLA_K), sg(2 * GLA_V), sg(RANK_P), _fixed((RANK_P, GLA_K)), _fixed((1, GLA_K)), _fixed((1, GLA_DV))],
        out_specs=[sg(GLA_V), sg(GLA_V), pl.BlockSpec((sc, GLA_V, GLA_K), lambda b, s: (b * n_seg + s, 0, 0))],
        out_shape=[jax.ShapeDtypeStruct((rows, GLA_V), BF16), jax.ShapeDtypeStruct((rows, GLA_V), F32),
                   jax.ShapeDtypeStruct((n_ex * n_chunk, GLA_V, GLA_K), F32)],
        scratch_shapes=[pltpu.VMEM((GLA_V, GLA_K), F32)],
        compiler_params=_params(("parallel", "arbitrary")),
    )(qk, vg, lr, w2p, gb, ng)


def _bwd_gla(qk, vg, lr, o, st, dyg, w2p, gb, ng, n_ex):
    rows = qk.shape[0]
    lp = rows // n_ex
    n_chunk = lp // CHUNK
    sc = _seg_chunks(n_chunk)
    n_seg = n_chunk // sc
    seg = sc * CHUNK

    def body(qk_ref, vg_ref, lr_ref, o_ref, st_ref, dyg_ref, w2_ref, gb_ref, ng_ref,
             dqk_ref, dvg_ref, dlr_ref, dw2_ref, dvec_ref, gt_ref, dz_ref):
        step = pl.program_id(1)
        sidx = n_seg - 1 - step

        @pl.when(step == 0)
        def _():
            gt_ref[...] = jnp.zeros_like(gt_ref)

        @pl.when((step == 0) & (pl.program_id(0) == 0))
        def _():
            dw2_ref[...] = jnp.zeros_like(dw2_ref)
            dvec_ref[...] = jnp.zeros_like(dvec_ref)

        causal = _causal_heads()
        k_mask = _block_mask((GLA_H * CHUNK, GLA_K), CHUNK, GLA_DK)
        v_mask = _block_mask((GLA_H * CHUNK, GLA_V), CHUNK, GLA_DV)
        s_mask = _block_mask((GLA_V, GLA_K), GLA_DV, GLA_DK)
        last_row = lax.broadcasted_iota(jnp.int32, (CHUNK, 1), 0) == CHUNK - 1
        ng = ng_ref[...]

        def chunk(ii, dng):
            ci = sc - 1 - ii
            base = pl.multiple_of(ci * CHUNK, CHUNK)
            rs = pl.ds(base, CHUNK)
            z, live, bcum = _gate_terms(lr_ref, w2_ref, gb_ref, rs, (sidx * sc + ci) * CHUNK)
            bl = bcum[CHUNK - 1:CHUNK, :]
            ebl = jnp.exp(bl)
            q = qk_ref[rs, 0:GLA_K]
            k = qk_ref[rs, GLA_K:2 * GLA_K]
            eb = jnp.exp(bcum)
            enb = jnp.exp(-bcum)
            ehb = jnp.exp(bl - bcum)
            qt = q * (GLA_DK ** -0.5) * eb
            kt = k * enb
            kh = k * ehb
            qtb = qt.astype(BF16)
            vb = vg_ref[rs, 0:GLA_V].astype(BF16)
            k_rows = _per_head_rows(kt.astype(BF16), k_mask)
            v_rows = _per_head_rows(vb, v_mask)
            gt = gt_ref[...]
            gtb = gt.astype(BF16)
            s_in = st_ref[ci]
            dos = []
            for h in range(GLA_H):
                hs = slice(h * GLA_DV, (h + 1) * GLA_DV)
                gs = slice(GLA_V + h * GLA_DV, GLA_V + (h + 1) * GLA_DV)
                oh = o_ref[rs, hs]
                ro = lax.rsqrt(jnp.mean(oh * oh, axis=-1, keepdims=True) + RMS_EPS)
                on = oh * ro
                g = vg_ref[rs, gs]
                sg = _sigmoid(g)
                dout = dyg_ref[rs, hs]
                dvg_ref[rs, gs] = (dout * on * ng * (sg * (1.0 + g * (1.0 - sg)))).astype(BF16)
                dw = dout * g * sg
                dng = dng + jnp.sum(dw * on, axis=0, keepdims=True)
                don = dw * ng
                dos.append((ro * (don - on * jnp.mean(don * on, axis=-1, keepdims=True))).astype(BF16))
            dob = jnp.concatenate(dos, axis=1)
            a = jnp.where(causal, _dot(qtb, k_rows, _NT), 0.0).astype(BF16)
            da = jnp.where(causal, _dot(dob, v_rows, _NT), 0.0).astype(BF16)
            dv = _fold_heads(_dot(a, dob, _TN), GLA_DV) + _dot(kh.astype(BF16), gtb, _NT)
            dvg_ref[rs, 0:GLA_V] = dv.astype(BF16)
            dkh = _dot(vb, gtb)
            dqt = _dot(da, k_rows) + _dot(dob, s_in.astype(BF16))
            dkt = _fold_heads(_dot(da, qtb, _TN), GLA_DK)
            dbl = jnp.sum(gt * s_in, axis=0, keepdims=True) * ebl + jnp.sum(dkh * kh, axis=0, keepdims=True)
            dqk_ref[rs, 0:GLA_K] = (dqt * (GLA_DK ** -0.5) * eb).astype(BF16)
            dqk_ref[rs, GLA_K:2 * GLA_K] = (dkt * enb + dkh * ehb).astype(BF16)
            db = dqt * qt - dkt * kt - dkh * kh
            db = jnp.where(last_row, db + dbl, db)
            dla = jnp.where(live, _rev_cumsum_rows(db), 0.0)
            dz_ref[rs, :] = dla * (1.0 / TAU) * (1.0 - _sigmoid(z))
            gt_ref[...] = jnp.where(s_mask, _dot(dob, qtb, _TN), 0.0) + gt * ebl
            return dng

        dng = lax.fori_loop(0, sc, chunk, jnp.zeros((1, GLA_DV), F32), unroll=True)
        dz = dz_ref[...]
        dzb = dz.astype(BF16)
        dlr_ref[...] = _dot(dzb, w2_ref[...], _NT).astype(BF16)
        dw2_ref[...] += _dot(lr_ref[...].astype(BF16), dzb, _TN)
        dvec_ref[0:1, :] += jnp.sum(dz, axis=0, keepdims=True)
        dvec_ref[1:2, 0:GLA_DV] += dng

    sg_ = lambda w: pl.BlockSpec((seg, w), lambda b, s: (b * n_seg + n_seg - 1 - s, 0))
    return pl.pallas_call(
        body, name="bwd_gla", grid=(n_ex, n_seg),
        in_specs=[sg_(2 * GLA_K), sg_(2 * GLA_V), sg_(RANK_P), sg_(GLA_V),
                  pl.BlockSpec((sc, GLA_V, GLA_K), lambda b, s: (b * n_seg + n_seg - 1 - s, 0, 0)), sg_(GLA_V),
                  _fixed((RANK_P, GLA_K)), _fixed((1, GLA_K)), _fixed((1, GLA_DV))],
        out_specs=[sg_(2 * GLA_K), sg_(2 * GLA_V), sg_(RANK_P), _fixed((RANK_P, GLA_K)), _fixed((8, GLA_K))],
        out_shape=[jax.ShapeDtypeStruct((rows, 2 * GLA_K), BF16), jax.ShapeDtypeStruct((rows, 2 * GLA_V), BF16),
                   jax.ShapeDtypeStruct((rows, RANK_P), BF16), jax.ShapeDtypeStruct((RANK_P, GLA_K), F32),
                   jax.ShapeDtypeStruct((8, GLA_K), F32)],
        scratch_shapes=[pltpu.VMEM((GLA_V, GLA_K), F32), pltpu.VMEM((seg, GLA_K), F32)],
        compiler_params=_params(("arbitrary", "arbitrary")),
    )(qk, vg, lr, o, st, dyg, w2p, gb, ng)


def _pad_rows(x, tgt):
    return jnp.pad(x, ((0, 0), (LEAD, 0), (0, 0))), jnp.pad(tgt, ((0, 0), (LEAD, 0), (0, 0)))


def _local_step(h0, tgt_p, p, pass_on, late_weights, send_early):
    n_ex, lp, _ = h0.shape
    rows = n_ex * lp
    meta = jnp.broadcast_to(p["meta"][None], (n_ex, N_META, D))
    h0 = lax.dynamic_update_slice(h0, meta, (0, ZROWS, 0)).reshape(rows, D)
    tgt_p = tgt_p.reshape(rows, D)

    uc, qk, vg, lr, n1 = _fwd_inproj(h0, p["g1"], p["w_in"])
    yg, o, st = _fwd_gla(qk, vg, lr, p["w2"], p["gb"], p["ng"], n_ex)
    token = pass_on(yg)
    ypre, yc = _fwd_conv(uc, p["conv_w"], p["conv_b"], p["ln_g"], p["ln_b"], token, n_ex)
    w_out, wg, wu, wd = late_weights(yc)
    h1, n2 = _fwd_outproj(yc, yg, h0, w_out, p["g2"], token)
    f, da, db, dh2, dh1, dh1b, part = _ffn_rows(h1, n2, tgt_p, wg, wu, wd, p["g2"], p["g3"], lp)
    g = {}
    token = send_early("ffn", [_matmul_tn(a_, b_, name).reshape(N_DEV, FF_S, D) for a_, b_, name in (
        (da, n2, "dw_gate"), (db, n2, "dw_up"), (f, dh2, "dw_down"))])
    dyc, dyg = _bwd_outproj(dh1b, w_out, token)
    token = send_early("out", [_dw_out(yc, yg, dh1b).reshape(N_DEV, W_OUT_S, D)])
    duc, g["conv_w"], g["conv_vec"] = _bwd_conv(uc, ypre, dyc, p["conv_w"], p["ln_g"], p["ln_b"], token, n_ex)
    dqk, dvg, dlr, g["w2"], g["gla_vec"] = _bwd_gla(qk, vg, lr, o, st, dyg, p["w2"], p["gb"], p["ng"], n_ex)
    token = send_early("in", [_dw_blocked(n1, [duc, dqk, dvg, dlr], W_IN_S, "dw_in")])
    grad_x, g["in_vec"], g["meta"] = _bwd_inproj(duc, dqk, dvg, dlr, dh1, h0, p["w_in"], p["g1"], token, lp)
    g["ffn_vec"] = part
    return grad_x, g


W_IN_S = D_IN // N_DEV
W_OUT_S = D // N_DEV
FF_S = D_FF // N_DEV
CONV_S = C_CONV // N_DEV
GATE_S = GLA_K // N_DEV
SMALL_PACK = 64
CONV_ROW = 16
GATE_ROW = 48
VEC_ROWS = 16
_VEC_ROWS = (("norm_mix_g", D), ("conv_b", C_CONV), ("conv_ln_g", C_CONV), ("conv_ln_b", C_CONV), ("gla_gate_b", GLA_K),
             ("gla_norm_g", GLA_DV), ("norm_ffn_g", D), ("norm_final_g", D))
LOSS_ROW = len(_VEC_ROWS)


def _position():
    return lax.axis_index("x"), lax.axis_index("y"), lax.axis_index("c")


def _any():
    return pl.BlockSpec(memory_space=pl.ANY)


def _stage(mats, meta, conv_w, w2):
    n_t = len(mats) + 1

    def body(*refs):
        ins = refs[0:n_t - 1]
        meta_ref, cw_ref, w2_ref = refs[n_t - 1:n_t + 2]
        lands = refs[n_t + 2:2 * n_t + 2]
        shards = refs[2 * n_t + 2:3 * n_t + 2]
        sems = refs[3 * n_t + 2]
        for s_ref, w_ref in zip(shards, ins):
            s_ref[...] = w_ref[...].astype(BF16)
        sp = shards[n_t - 1]
        sp[...] = jnp.zeros_like(sp)
        sp[0:N_META, :] = meta_ref[...]
        sp[CONV_ROW:CONV_ROW + CONV_W, 0:CONV_S] = cw_ref[...]
        sp[GATE_ROW:GATE_ROW + RANK, 0:GATE_S] = w2_ref[...]
        x, y, c = _position()
        mine = [pltpu.make_async_copy(shards[t], lands[t].at[4 * x + 2 * y + c], sems.at[t]) for t in range(n_t)]
        for cp in mine:
            cp.start()
        for cp in mine:
            cp.wait()

    shard_shapes = [jax.ShapeDtypeStruct(m.shape, BF16) for m in mats] + [jax.ShapeDtypeStruct((SMALL_PACK, 128), F32)]
    res = pl.pallas_call(
        body, name="stage",
        out_shape=[jax.ShapeDtypeStruct((N_DEV,) + s.shape, s.dtype) for s in shard_shapes] + shard_shapes,
        in_specs=[_whole_vmem()] * (n_t + 2), out_specs=[_any()] * n_t + [_whole_vmem()] * n_t,
        scratch_shapes=[pltpu.SemaphoreType.DMA((n_t,))],
        compiler_params=pltpu.CompilerParams(vmem_limit_bytes=VMEM_LIMIT),
    )(*mats, meta, conv_w, w2)
    return res[0:n_t], res[n_t:]


_HBM = pl.BlockSpec(memory_space=pltpu.HBM)
_SEM = pl.BlockSpec(memory_space=pltpu.SEMAPHORE)
_EFFECT = pltpu.SideEffectType.DATAFLOW_SIDE_EFFECTING


_N_ROUTES = {"scatter": 7, "first": 4, "forward": 3}


def _routes(mode):
    x, y, c = _position()
    me = 4 * x + 2 * y + c
    if mode == "scatter":
        out = []
        for k in range(1, N_DEV):
            px = 1 - x if k & 4 else x
            py = 1 - y if k & 2 else y
            pc = 1 - c if k & 1 else c
            out.append(((px, py, pc), 4 * px + 2 * py + pc, me))
        return out
    if mode == "first":
        return [(pos, None, me) for pos in ((x, y, 1 - c), (1 - x, y, c), (x, 1 - y, c), (1 - x, 1 - y, c))]
    assert mode == "forward"
    return [((x, y, 1 - c), 4 * px + 2 * py + c, 4 * px + 2 * py + c) for px, py in ((1 - x, y), (x, 1 - y), (1 - x, 1 - y))]


def _route_copies(mode, n, src_refs, land_refs, send_sems, recv_sems):
    nr = _N_ROUTES[mode]
    for i, (pos, src_blk, dst_blk) in enumerate(_routes(mode)):
        for t in range(n):
            src = land_refs[t] if mode == "forward" else src_refs[t]
            yield pltpu.make_async_remote_copy(
                src_ref=src if src_blk is None else src.at[src_blk], dst_ref=land_refs[t].at[dst_blk],
                send_sem=send_sems.at[nr * t + i], recv_sem=recv_sems.at[nr * t + i], device_id=pos, device_id_type=MESH)


def _in_hbm(a):
    return pltpu.with_memory_space_constraint(a, pltpu.HBM)


def _send_start(name, srcs, lands, mode, after):
    n, ns = len(lands), len(srcs)
    nsem = _N_ROUTES[mode] * n

    def body(*refs):
        src_refs, land_refs = refs[0:ns], refs[ns:ns + n]
        send_sems, recv_sems = refs[ns + n + 1:ns + n + 3]
        token = refs[2 * (ns + n) + 3]
        for cp in _route_copies(mode, n, src_refs, land_refs, send_sems, recv_sems):
            cp.start()
        token[...] = jnp.zeros_like(token)

    bufs = list(srcs) + list(lands)
    res = pl.pallas_call(
        body, name=name,
        out_shape=(pltpu.SemaphoreType.DMA((nsem,)), pltpu.SemaphoreType.DMA((nsem,)),
                   *[pltpu.HBM(b.shape, b.dtype) for b in bufs], jax.ShapeDtypeStruct((8, 128), F32)),
        in_specs=[_HBM] * len(bufs) + [_any()], out_specs=(_SEM, _SEM, *[_HBM] * len(bufs), _whole_vmem()),
        input_output_aliases={i: 2 + i for i in range(len(bufs))},
        compiler_params=pltpu.CompilerParams(has_side_effects=_EFFECT),
    )(*[_in_hbm(b) for b in bufs], after)
    return res[0], res[1], res[2:2 + ns], res[2 + ns:2 + ns + n], res[2 + ns + n]


def _send_wait(name, send_sems, recv_sems, srcs, lands, mode, after):
    n, ns = len(lands), len(srcs)
    after = after if isinstance(after, tuple) else (after,)

    def body(*refs):
        src_refs, land_refs = refs[0:ns], refs[ns:ns + n]
        send_sems, recv_sems = refs[ns + n:ns + n + 2]
        for cp in _route_copies(mode, n, src_refs, land_refs, send_sems, recv_sems):
            cp.wait_send()
            cp.wait_recv()

    bufs = list(srcs) + list(lands)
    res = pl.pallas_call(
        body, name=name,
        out_shape=tuple(pltpu.HBM(b.shape, b.dtype) for b in bufs),
        in_specs=[_HBM] * len(bufs) + [_SEM, _SEM] + [_any()] * len(after), out_specs=tuple([_HBM] * len(bufs)),
        input_output_aliases={i: i for i in range(len(bufs))},
        compiler_params=pltpu.CompilerParams(has_side_effects=_EFFECT),
    )(*bufs, send_sems, recv_sems, *after)
    return res[0:ns], res[ns:ns + n]


def _unshard_in(a_in, a_small, token):
    def body(a_ref, s_ref, token_ref, w_ref, meta_ref, cw_ref, w2_ref):
        w_ref[:, D_IN:D_INP] = jnp.zeros((D, D_INP - D_IN), BF16)
        w2_ref[...] = jnp.zeros_like(w2_ref)
        for d in range(N_DEV):
            w_ref[:, d * W_IN_S:(d + 1) * W_IN_S] = a_ref[d]
            meta_ref[:, d * 128:(d + 1) * 128] = s_ref[d, 0:N_META, :]
            cw_ref[:, d * CONV_S:(d + 1) * CONV_S] = s_ref[d, CONV_ROW:CONV_ROW + 32, 0:CONV_S]
            w2_ref[0:RANK, d * GATE_S:(d + 1) * GATE_S] = s_ref[d, GATE_ROW:GATE_ROW + RANK, 0:GATE_S].astype(BF16)

    return pl.pallas_call(
        body, name="unshard_in",
        out_shape=[jax.ShapeDtypeStruct((D, D_INP), BF16), jax.ShapeDtypeStruct((N_META, D), F32),
                   jax.ShapeDtypeStruct((32, C_CONV), F32), jax.ShapeDtypeStruct((RANK_P, GLA_K), BF16)],
        compiler_params=pltpu.CompilerParams(vmem_limit_bytes=VMEM_LIMIT),
    )(a_in, a_small, token)


def _pack_small(g):
    def body(meta_ref, cw_ref, w2_ref, in_vec, ffn_vec, conv_vec, gla_vec, sp, vp):
        sp[...] = jnp.zeros_like(sp)
        vp[...] = jnp.zeros_like(vp)
        for d in range(N_DEV):
            sp[d, 0:N_META, :] = meta_ref[:, d * 128:(d + 1) * 128]
            sp[d, CONV_ROW:CONV_ROW + 32, 0:CONV_S] = cw_ref[:, d * CONV_S:(d + 1) * CONV_S]
            sp[d, GATE_ROW:GATE_ROW + RANK, 0:GATE_S] = w2_ref[0:RANK, d * GATE_S:(d + 1) * GATE_S]
            vp[d, 0:1, :] = in_vec[0:1, :]
            vp[d, 1:4, 0:C_CONV] = conv_vec[0:3, :]
            vp[d, 4:5, 0:GLA_K] = gla_vec[0:1, :]
            vp[d, 5:6, 0:GLA_DV] = gla_vec[1:2, 0:GLA_DV]
            vp[d, 6:7, :] = ffn_vec[1:2, :]
            vp[d, 7:8, :] = ffn_vec[0:1, :]
            vp[d, LOSS_ROW:LOSS_ROW + 1, :] = ffn_vec[2:3, :]

    return pl.pallas_call(
        body, name="pack_small",
        out_shape=[jax.ShapeDtypeStruct((N_DEV, SMALL_PACK, 128), F32), jax.ShapeDtypeStruct((N_DEV, VEC_ROWS, D), F32)],
    )(g["meta"], g["conv_w"], g["w2"], g["in_vec"], g["ffn_vec"], g["conv_vec"], g["gla_vec"])


def _adamw(w, g, m, v):
    m = ADAM_B1 * m + (1.0 - ADAM_B1) * g
    v = ADAM_B2 * v + (1.0 - ADAM_B2) * (g * g)
    m_hat = m / (1.0 - ADAM_B1 ** ADAM_STEP)
    v_hat = v / (1.0 - ADAM_B2 ** ADAM_STEP)
    return -ADAM_LR * (m_hat / (jnp.sqrt(v_hat) + ADAM_EPS) + ADAM_WD * w), m, v


def _update_matrix(recv, own, me, w, m, v, name):
    _, r, c = recv.shape
    tr = _row_tile(r, 256)

    def body(me_ref, recv_ref, own_ref, w_ref, m_ref, v_ref, g_ref, d_ref, nm_ref, nv_ref):
        g = jnp.zeros((tr, c), F32)
        for s in range(N_DEV):
            g = g + jnp.where(me_ref[0] == s, own_ref[...], recv_ref[s]).astype(F32)
        g_ref[...] = g
        d_ref[...], nm_ref[...], nv_ref[...] = _adamw(w_ref[...], g, m_ref[...], v_ref[...])

    one = pl.BlockSpec((None, tr, c), lambda i, me_ref: (0, i, 0))
    return pl.pallas_call(
        body, name=name,
        grid_spec=pltpu.PrefetchScalarGridSpec(
            num_scalar_prefetch=1, grid=(r // tr,),
            in_specs=[pl.BlockSpec((N_DEV, tr, c), lambda i, me_ref: (0, i, 0)),
                      pl.BlockSpec((None, tr, c), lambda i, me_ref: (me_ref[0], i, 0)), one, one, one],
            out_specs=[one] * 4),
        out_shape=[jax.ShapeDtypeStruct((1, r, c), F32)] * 4,
        compiler_params=_params(("parallel",)),
    )(me, recv, own, w, m, v)


_SMALL = ("meta_tokens", "conv_w", "gla_w_gate2") + tuple(n for n, _ in _VEC_ROWS)


def _update_small(me, srecv, vrecv, sown, vown, w, m, v):
    n = len(_SMALL)

    def body(*refs):
        me_ref, s_ref, v_ref, so_ref, vo_ref = refs[0:5]
        w_refs, m_refs, v_refs = refs[5:5 + n], refs[5 + n:5 + 2 * n], refs[5 + 2 * n:5 + 3 * n]
        outs = refs[5 + 3 * n:]
        ssum = jnp.zeros((SMALL_PACK, 128), F32)
        vsum = jnp.zeros((VEC_ROWS, D), F32)
        for s in range(N_DEV):
            ssum = ssum + jnp.where(me_ref[0] == s, so_ref[s], s_ref[s])
            vsum = vsum + jnp.where(me_ref[0] == s, vo_ref[s], v_ref[s])
        grads = [ssum[0:N_META, :], ssum[CONV_ROW:CONV_ROW + CONV_W, 0:CONV_S], ssum[GATE_ROW:GATE_ROW + RANK, 0:GATE_S]]
        grads += [vsum[i:i + 1, 0:width] for i, (_, width) in enumerate(_VEC_ROWS)]
        for i, g in enumerate(grads):
            d, nm, nv = _adamw(w_refs[i][...], g, m_refs[i][...], v_refs[i][...])
            outs[i][...] = g
            outs[n + i][...] = d
            outs[2 * n + i][...] = nm
            outs[3 * n + i][...] = nv
        outs[4 * n][...] = vsum[LOSS_ROW:LOSS_ROW + 1, 0:128]

    shapes = [jax.ShapeDtypeStruct(t.shape, F32) for t in w]
    res = pl.pallas_call(
        body, name="update_small", out_shape=shapes * 4 + [jax.ShapeDtypeStruct((1, 128), F32)],
        in_specs=[pl.BlockSpec(memory_space=pltpu.SMEM)] + [_whole_vmem()] * (4 + 3 * n),
    )(me, srecv, vrecv, sown, vown, *w, *m, *v)
    return res[0:n], res[n:2 * n], res[2 * n:3 * n], res[3 * n:4 * n], res[4 * n]


_WEIGHTS = ("meta_tokens", "norm_mix_g", "w_in", "conv_w", "conv_b", "conv_ln_g", "conv_ln_b", "gla_w_gate2", "gla_gate_b",
            "gla_norm_g", "w_out", "norm_ffn_g", "w_ffn_gate", "w_ffn_up", "w_ffn_down", "norm_final_g")
_MATRICES = ("w_in", "w_out", "w_ffn_gate", "w_ffn_up", "w_ffn_down")
_TRANSPOSED = ("w_ffn_gate", "w_ffn_up")


def kernel(x, meta_tokens, norm_mix_g, w_in, conv_w, conv_b, conv_ln_g, conv_ln_b, gla_w_gate2, gla_gate_b, gla_norm_g, w_out, norm_ffn_g, w_ffn_gate, w_ffn_up, w_ffn_down, norm_final_g, loss_target, m_meta_tokens, m_norm_mix_g, m_w_in, m_conv_w, m_conv_b, m_conv_ln_g, m_conv_ln_b, m_gla_w_gate2, m_gla_gate_b, m_gla_norm_g, m_w_out, m_norm_ffn_g, m_w_ffn_gate, m_w_ffn_up, m_w_ffn_down, m_norm_final_g, v_meta_tokens, v_norm_mix_g, v_w_in, v_conv_w, v_conv_b, v_conv_ln_g, v_conv_ln_b, v_gla_w_gate2, v_gla_gate_b, v_gla_norm_g, v_w_out, v_norm_ffn_g, v_w_ffn_gate, v_w_ffn_up, v_w_ffn_down, v_norm_final_g):
    given = dict(locals())
    two_d = lambda a: a.reshape(1, -1) if a.ndim == 1 else a.reshape(a.shape[-2:])
    fams = [{n: given[pre + n] for n in _WEIGHTS} for pre in ("", "m_", "v_")]
    for f in fams:
        for n in _TRANSPOSED:
            f[n] = f[n].transpose(0, 2, 1)
    w = fams[0]

    lands, shards = _stage([two_d(w[n]) for n in _MATRICES], w["meta_tokens"], two_d(w["conv_w"]), two_d(w["gla_w_gate2"]))
    soon, later = (0, 5), (1, 2, 3, 4)
    pick = lambda seq, idx: [seq[i] for i in idx]
    first = _send_start("gather_first_start", pick(shards, soon), pick(lands, soon), "first", norm_mix_g)
    ffn_first = _send_start("gather_ffn_first_start", pick(shards, later), pick(lands, later), "first", first[4])
    h0, tgt_p = _pad_rows(x, loss_target)
    _, arrived = _send_wait("gather_first_wait", *first[0:4], "first", (h0, tgt_p))
    forward = _send_start("gather_forward_start", [], arrived, "forward", ffn_first[4])
    _, (a_in, a_small) = _send_wait("gather_forward_wait", *forward[0:4], "forward", forward[4])
    w_in, meta, conv_taps, w2 = _unshard_in(a_in, a_small, forward[4])
    p = dict(meta=meta, conv_w=conv_taps, w2=w2, w_in=w_in, g1=norm_mix_g, conv_b=conv_b, ln_g=conv_ln_g, ln_b=conv_ln_b,
             gb=gla_gate_b, ng=gla_norm_g, g2=norm_ffn_g, g3=two_d(norm_final_g))
    passed = {}

    def pass_on(after):
        _, arrived_ffn = _send_wait("gather_ffn_first_wait", *ffn_first[0:4], "first", after)
        passed["sent"] = _send_start("gather_ffn_forward_start", [], arrived_ffn, "forward", after)
        return passed["sent"][4]

    def late_weights(after):
        _, (a_out, a_g, a_u, a_d) = _send_wait("gather_ffn_forward_wait", *passed["sent"][0:4], "forward", after)
        return a_out.reshape(D, D), a_g.reshape(D_FF, D), a_u.reshape(D_FF, D), a_d.reshape(D_FF, D)

    sent = {}

    def send_early(tag, mats):
        landing = [_in_hbm(lax.empty(m_.shape, m_.dtype)) for m_ in mats]
        sent[tag] = _send_start("scatter_" + tag + "_start", mats, landing, "scatter", norm_mix_g)
        return sent[tag][4]

    grad_x, g = _local_step(h0, tgt_p, p, pass_on, late_weights, send_early)

    token = send_early("small", list(_pack_small(g)))
    x_, y_, c_ = _position()
    me = (4 * x_ + 2 * y_ + c_).astype(jnp.int32).reshape(1)
    res = {}
    for tag, names in (("ffn", ("w_ffn_gate", "w_ffn_up", "w_ffn_down")), ("out", ("w_out",)), ("in", ("w_in",))):
        own, recv = _send_wait("scatter_" + tag + "_wait", *sent[tag][0:4], "scatter", token)
        for n, o_, r_ in zip(names, own, recv):
            res[n] = _update_matrix(r_, o_, me, *[f[n] for f in fams], "update_" + n)
            token = res[n][1]
    (sown, vown), (srecv, vrecv) = _send_wait("scatter_small_wait", *sent["small"][0:4], "scatter", token)
    small = _update_small(me, srecv, vrecv, sown, vown, *[[two_d(f[n]) for n in _SMALL] for f in fams])
    for i, n in enumerate(_SMALL):
        res[n] = [fam[i].reshape(w[n].shape) for fam in small[0:4]]
    for n in _TRANSPOSED:
        res[n] = [t.transpose(0, 2, 1) for t in res[n]]
    outs = [small[4][0, 0], grad_x]
    for k in range(4):
        outs += [res[n][k] for n in _WEIGHTS]
    return tuple(outs)
```

```python
import functools

import jax
import jax.numpy as jnp
from jax import lax
from jax.experimental import pallas as pl
from jax.experimental.pallas import tpu as pltpu

F32 = jnp.float32
BF16 = jnp.bfloat16

D = 1024
N_META = 16
C_CONV = 512
CONV_W = 31
GLA_H = 4
GLA_DK = 64
GLA_DV = 128
GLA_K = GLA_H * GLA_DK
GLA_V = GLA_H * GLA_DV
RANK = 16
RANK_P = 128
TAU = 16.0
CHUNK = 64
LEAD = CHUNK
ZROWS = LEAD - N_META
D_IN = 2 * C_CONV + 2 * GLA_K + 2 * GLA_V + RANK
D_INP = D_IN - RANK + RANK_P
D_FF = 2816
FF_CHUNK = 1408
FF_SPLIT = (0, 1536, D_FF)
RMS_EPS = 1e-6
LN_EPS = 1e-5
N_DEV = 8

ADAM_LR = 0.001
ADAM_B1 = 0.9
ADAM_B2 = 0.999
ADAM_EPS = 1e-08
ADAM_WD = 0.01
ADAM_STEP = 10

VMEM_LIMIT = 60 * 1024 * 1024
ROW_TILE = 1056
FFN_ROW_TILE = 352
DW_ROW_TILE = 1408
MESH = pl.DeviceIdType.MESH

_NN = (((1,), (0,)), ((), ()))
_NT = (((1,), (1,)), ((), ()))
_TN = (((0,), (0,)), ((), ()))


def _dot(a, b, dims=_NN):
    return lax.dot_general(a, b, dims, preferred_element_type=F32)


def _sigmoid(x):
    return 1.0 / (1.0 + jnp.exp(-x))


def _row_tile(rows, target):
    best = None
    for t in range(16, min(rows, target) + 1, 16):
        if rows % t == 0:
            best = t
    assert best is not None, rows
    return best


def _params(sem=None):
    return pltpu.CompilerParams(dimension_semantics=sem, vmem_limit_bytes=VMEM_LIMIT)


def _whole_vmem():
    return pl.BlockSpec(memory_space=pltpu.VMEM)


def _rows(tm, width):
    return pl.BlockSpec((tm, width), lambda i: (i, 0))


def _fixed(shape):
    return pl.BlockSpec(shape, lambda *_: (0,) * len(shape))


def _fwd_inproj(h0, g1, w_in):
    rows = h0.shape[0]
    tm = _row_tile(rows, ROW_TILE)

    def body(h_ref, g_ref, w_ref, uc_ref, qk_ref, vg_ref, lr_ref, n1_ref):
        h = h_ref[...]
        r = lax.rsqrt(jnp.mean(h * h, axis=-1, keepdims=True) + RMS_EPS)
        n = (h * r * g_ref[...]).astype(BF16)
        n1_ref[...] = n
        uc_ref[...] = _dot(n, w_ref[:, 0:1024])
        qk_ref[...] = _dot(n, w_ref[:, 1024:1536])
        vg_ref[...] = _dot(n, w_ref[:, 1536:2560])
        lr_ref[...] = _dot(n, w_ref[:, 2560:2688])

    return pl.pallas_call(
        body, name="fwd_inproj", grid=(rows // tm,),
        in_specs=[_rows(tm, D), _fixed((1, D)), _whole_vmem()],
        out_specs=[_rows(tm, 1024), _rows(tm, 512), _rows(tm, 1024), _rows(tm, RANK_P), _rows(tm, D)],
        out_shape=[jax.ShapeDtypeStruct((rows, 1024), F32), jax.ShapeDtypeStruct((rows, 512), F32),
                   jax.ShapeDtypeStruct((rows, 1024), F32), jax.ShapeDtypeStruct((rows, RANK_P), F32),
                   jax.ShapeDtypeStruct((rows, D), BF16)],
        compiler_params=_params(("parallel",)),
    )(h0, g1, w_in)


def _fwd_outproj(yc, yg, h0, w_out, g2, token):
    rows = h0.shape[0]
    tm = _row_tile(rows, ROW_TILE)

    def body(yc_ref, yg_ref, h_ref, w_ref, g_ref, token_ref, h1_ref, n2_ref):
        h1 = h_ref[...] + _dot(yc_ref[...], w_ref[0:C_CONV, :]) + _dot(yg_ref[...], w_ref[C_CONV:D, :])
        h1_ref[...] = h1
        r = lax.rsqrt(jnp.mean(h1 * h1, axis=-1, keepdims=True) + RMS_EPS)
        n2_ref[...] = (h1 * r * g_ref[...]).astype(BF16)

    return pl.pallas_call(
        body, name="fwd_outproj", grid=(rows // tm,),
        in_specs=[_rows(tm, C_CONV), _rows(tm, GLA_V), _rows(tm, D), _whole_vmem(), _fixed((1, D)), _fixed((8, 128))],
        out_specs=[_rows(tm, D), _rows(tm, D)],
        out_shape=[jax.ShapeDtypeStruct((rows, D), F32), jax.ShapeDtypeStruct((rows, D), BF16)],
        compiler_params=_params(("parallel",)),
    )(yc, yg, h0, w_out, g2, token)


def _ffn_rows(h1, n2, tgt, wg, wu, wd, g2, g3, rows_per_example):
    rows = h1.shape[0]
    tm = _row_tile(rows, FFN_ROW_TILE)
    ff_blocks = [slice(lo, hi) for lo, hi in zip(FF_SPLIT[:-1], FF_SPLIT[1:])]

    def body(h1_ref, n2_ref, t_ref, wg_ref, wu_ref, wd_ref, g2_ref, g3_ref,
             f_ref, da_ref, db_ref, dh2_ref, dh1_ref, dh1b_ref, part_ref):
        i = pl.program_id(0)
        n2 = n2_ref[...]
        y2 = jnp.zeros((tm, D), F32)
        for cs in ff_blocks:
            a = _dot(n2, wg_ref[cs, :], _NT)
            b = _dot(n2, wu_ref[cs, :], _NT)
            f = (a * _sigmoid(a) * b).astype(BF16)
            f_ref[:, cs] = f
            da_ref[:, cs] = a.astype(BF16)
            db_ref[:, cs] = b.astype(BF16)
            y2 = y2 + _dot(f, wd_ref[cs, :])
        h1 = h1_ref[...]
        h2 = h1 + y2
        r3 = lax.rsqrt(jnp.mean(h2 * h2, axis=-1, keepdims=True) + RMS_EPS)
        xh3 = h2 * r3
        g3 = g3_ref[...]
        pos = (i * tm + lax.broadcasted_iota(jnp.int32, (tm, 1), 0)) % rows_per_example
        valid = pos >= LEAD
        err = jnp.where(valid, xh3 * g3 - t_ref[...], 0.0)
        loss = 0.5 / D * jnp.sum(jnp.sum(err * err, axis=-1, keepdims=True), axis=0, keepdims=True)
        dy = err * (1.0 / D)
        dg3 = jnp.sum(dy * xh3, axis=0, keepdims=True)
        dxh = dy * g3
        dh2 = r3 * (dxh - xh3 * jnp.mean(dxh * xh3, axis=-1, keepdims=True))
        dh2b = dh2.astype(BF16)
        dh2_ref[...] = dh2b
        dn2 = jnp.zeros((tm, D), F32)
        for cs in ff_blocks:
            df = _dot(dh2b, wd_ref[cs, :], _NT)
            a = da_ref[:, cs].astype(F32)
            b = db_ref[:, cs].astype(F32)
            sg = _sigmoid(a)
            da = (df * b * sg * (1.0 + a * (1.0 - sg))).astype(BF16)
            db = (df * a * sg).astype(BF16)
            da_ref[:, cs] = da
            db_ref[:, cs] = db
            dn2 = dn2 + _dot(da, wg_ref[cs, :]) + _dot(db, wu_ref[cs, :])
        r2 = lax.rsqrt(jnp.mean(h1 * h1, axis=-1, keepdims=True) + RMS_EPS)
        xh2 = h1 * r2
        dg2 = jnp.sum(dn2 * xh2, axis=0, keepdims=True)
        dxh2 = dn2 * g2_ref[...]
        dh1 = dh2 + r2 * (dxh2 - xh2 * jnp.mean(dxh2 * xh2, axis=-1, keepdims=True))
        dh1_ref[...] = dh1
        dh1b_ref[...] = dh1.astype(BF16)

        @pl.when(i == 0)
        def _():
            part_ref[...] = jnp.zeros_like(part_ref)

        part_ref[0:1, :] += dg3
        part_ref[1:2, :] += dg2
        part_ref[2:3, :] += jnp.broadcast_to(loss, (1, D))

    return pl.pallas_call(
        body, name="ffn_rows", grid=(rows // tm,),
        in_specs=[_rows(tm, D), _rows(tm, D), _rows(tm, D), _whole_vmem(), _whole_vmem(), _whole_vmem(),
                  _fixed((1, D)), _fixed((1, D))],
        out_specs=[_rows(tm, D_FF), _rows(tm, D_FF), _rows(tm, D_FF), _rows(tm, D), _rows(tm, D), _rows(tm, D),
                   _fixed((8, D))],
        out_shape=[jax.ShapeDtypeStruct((rows, D_FF), BF16)] * 3
        + [jax.ShapeDtypeStruct((rows, D), BF16), jax.ShapeDtypeStruct((rows, D), F32),
           jax.ShapeDtypeStruct((rows, D), BF16), jax.ShapeDtypeStruct((8, D), F32)],
        compiler_params=_params(("arbitrary",)),
    )(h1, n2, tgt, wg, wu, wd, g2, g3)


def _bwd_outproj(dh1b, w_out, token):
    rows = dh1b.shape[0]
    tm = _row_tile(rows, ROW_TILE)

    def body(d_ref, w_ref, token_ref, dyc_ref, dyg_ref):
        d = d_ref[...]
        dyc_ref[...] = _dot(d, w_ref[0:C_CONV, :], _NT)
        dyg_ref[...] = _dot(d, w_ref[C_CONV:D, :], _NT)

    return pl.pallas_call(
        body, name="bwd_outproj", grid=(rows // tm,),
        in_specs=[_rows(tm, D), _whole_vmem(), _fixed((8, 128))],
        out_specs=[_rows(tm, C_CONV), _rows(tm, GLA_V)],
        out_shape=[jax.ShapeDtypeStruct((rows, C_CONV), F32), jax.ShapeDtypeStruct((rows, GLA_V), F32)],
        compiler_params=_params(("parallel",)),
    )(dh1b, w_out, token)


def _bwd_inproj(duc, dqk, dvg, dlr, dh1, h0, w_in, g1, token, rows_per_example):
    rows = h0.shape[0]
    n_ex = rows // rows_per_example
    tm = _row_tile(rows_per_example, ROW_TILE)
    tiles_per_example = rows_per_example // tm
    n_steps = rows // tm

    def body(duc_ref, dqk_ref, dvg_ref, dlr_ref, dh1_ref, h_ref, w_ref, g_ref, token_ref, gx_ref, part_ref, dmeta_ref,
             buf_ref, sems):
        dn = (_dot(duc_ref[...], w_ref[:, 0:1024], _NT) + _dot(dqk_ref[...], w_ref[:, 1024:1536], _NT)
              + _dot(dvg_ref[...], w_ref[:, 1536:2560], _NT) + _dot(dlr_ref[...], w_ref[:, 2560:2688], _NT))
        h = h_ref[...]
        r = lax.rsqrt(jnp.mean(h * h, axis=-1, keepdims=True) + RMS_EPS)
        xh = h * r
        dg = jnp.sum(dn * xh, axis=0, keepdims=True)
        dxh = dn * g_ref[...]
        dh0 = dh1_ref[...] + r * (dxh - xh * jnp.mean(dxh * xh, axis=-1, keepdims=True))
        i = pl.program_id(0)

        def copies(step):
            slot, b, j = step % 2, step // tiles_per_example, step % tiles_per_example
            out = [(j == 0, pltpu.make_async_copy(buf_ref.at[slot, pl.ds(LEAD, tm - LEAD)],
                                                   gx_ref.at[b, pl.ds(0, tm - LEAD)], sems.at[slot]))]
            if tiles_per_example > 1:
                out.append((j != 0, pltpu.make_async_copy(
                    buf_ref.at[slot], gx_ref.at[b, pl.ds(pl.multiple_of(jnp.maximum(j * tm - LEAD, 0), 8), tm)],
                    sems.at[slot])))
            return out

        def each(step, act):
            for cond, cp in copies(step):
                pl.when(cond)(functools.partial(act, cp))

        @pl.when(i >= 2)
        def _():
            each(i - 2, lambda cp: cp.wait())

        buf_ref[i % 2] = dh0
        each(i, lambda cp: cp.start())

        @pl.when(i == n_steps - 1)
        def _():
            each(i, lambda cp: cp.wait())
            if n_steps > 1:
                each(i - 1, lambda cp: cp.wait())

        @pl.when(i == 0)
        def _():
            part_ref[...] = jnp.zeros_like(part_ref)
            dmeta_ref[...] = jnp.zeros_like(dmeta_ref)

        part_ref[0:1, :] += dg

        @pl.when(i % tiles_per_example == 0)
        def _():
            dmeta_ref[...] += dh0[ZROWS:LEAD, :]

    return pl.pallas_call(
        body, name="bwd_inproj", grid=(n_steps,),
        in_specs=[_rows(tm, 1024), _rows(tm, 512), _rows(tm, 1024), _rows(tm, RANK_P), _rows(tm, D), _rows(tm, D),
                  _whole_vmem(), _fixed((1, D)), _fixed((8, 128))],
        out_specs=[_any(), _fixed((8, D)), _fixed((N_META, D))],
        out_shape=[jax.ShapeDtypeStruct((n_ex, rows_per_example - LEAD, D), F32), jax.ShapeDtypeStruct((8, D), F32),
                   jax.ShapeDtypeStruct((N_META, D), F32)],
        scratch_shapes=[pltpu.VMEM((2, tm, D), F32), pltpu.SemaphoreType.DMA((2,))],
        compiler_params=_params(("arbitrary",)),
    )(duc, dqk, dvg, dlr, dh1, h0, w_in, g1, token)


def _dw_blocked(a, bs, width, name):
    rows, m = a.shape
    ws = [b.shape[1] for b in bs]
    assert sum(ws) >= N_DEV * width
    tk = _row_tile(rows, DW_ROW_TILE)
    nk = rows // tk

    def body(a_ref, *refs):
        b_refs, o_ref, acc_ref = refs[:len(bs)], refs[len(bs)], refs[len(bs) + 1]
        k = pl.program_id(0)

        @pl.when(k == 0)
        def _():
            acc_ref[...] = jnp.zeros_like(acc_ref)

        at = a_ref[...].T
        off = 0
        for b_ref, w in zip(b_refs, ws):
            acc_ref[:, off:off + w] += _dot(at, b_ref[...])
            off += w

        @pl.when(k == nk - 1)
        def _():
            for d in range(N_DEV):
                o_ref[d] = acc_ref[:, d * width:(d + 1) * width].astype(BF16)

    return pl.pallas_call(
        body, name=name, grid=(nk,),
        in_specs=[_rows(tk, m)] + [_rows(tk, w) for w in ws],
        out_specs=_fixed((N_DEV, m, width)),
        out_shape=jax.ShapeDtypeStruct((N_DEV, m, width), BF16),
        scratch_shapes=[pltpu.VMEM((m, sum(ws)), F32)],
        compiler_params=_params(("arbitrary",)),
    )(a, *bs)


def _dw_out(yc, yg, dh1b):
    rows = yc.shape[0]
    tk = _row_tile(rows, DW_ROW_TILE)
    nk = rows // tk

    def body(yc_ref, yg_ref, d_ref, o_ref, acc_ref):
        k = pl.program_id(0)

        @pl.when(k == 0)
        def _():
            acc_ref[...] = jnp.zeros_like(acc_ref)

        d = d_ref[...]
        acc_ref[0:C_CONV, :] += _dot(yc_ref[...], d, _TN)
        acc_ref[C_CONV:D, :] += _dot(yg_ref[...], d, _TN)

        @pl.when(k == nk - 1)
        def _():
            o_ref[...] = acc_ref[...].astype(BF16)

    return pl.pallas_call(
        body, name="dw_out", grid=(nk,),
        in_specs=[_rows(tk, C_CONV), _rows(tk, GLA_V), _rows(tk, D)],
        out_specs=_fixed((D, D)), out_shape=jax.ShapeDtypeStruct((D, D), BF16),
        scratch_shapes=[pltpu.VMEM((D, D), F32)],
        compiler_params=_params(("arbitrary",)),
    )(yc, yg, dh1b)


def _matmul_tn(a, b, name):
    rows, m = a.shape
    n = b.shape[1]
    tk = _row_tile(rows, DW_ROW_TILE)
    tn = n if n <= 1024 else FF_CHUNK
    tm_ = m if m <= 1024 else FF_CHUNK
    assert n % tn == 0 and m % tm_ == 0
    nk = rows // tk

    def body(a_ref, b_ref, o_ref, acc_ref):
        k = pl.program_id(2)

        @pl.when(k == 0)
        def _():
            acc_ref[...] = jnp.zeros_like(acc_ref)

        acc_ref[...] += _dot(a_ref[...], b_ref[...], _TN)

        @pl.when(k == nk - 1)
        def _():
            o_ref[...] = acc_ref[...].astype(BF16)

    return pl.pallas_call(
        body, name=name, grid=(m // tm_, n // tn, nk),
        in_specs=[pl.BlockSpec((tk, tm_), lambda i, j, k: (k, i)), pl.BlockSpec((tk, tn), lambda i, j, k: (k, j))],
        out_specs=pl.BlockSpec((tm_, tn), lambda i, j, k: (i, j)),
        out_shape=jax.ShapeDtypeStruct((m, n), BF16),
        scratch_shapes=[pltpu.VMEM((tm_, tn), F32)],
        compiler_params=_params(("parallel", "parallel", "arbitrary")),
    )(a, b)


HALO = 32
LANES = 128


def _shifted(win, offsets):
    for r in range(8):
        js = [j for j, k in enumerate(offsets) if k % 8 == r]
        if js:
            rolled = win if r == 0 else pltpu.roll(win, CHUNK + HALO - r, 0)
            for j in js:
                yield j, rolled[offsets[j] - r:offsets[j] - r + CHUNK]


def _glu_into(uc_ref, vs_ref, n_chunk):
    vs_ref[0:CHUNK, :] = jnp.zeros((CHUNK, C_CONV), F32)

    def glu(i, carry):
        base = pl.multiple_of(i * CHUNK, CHUNK)
        val = uc_ref[pl.ds(base, CHUNK), 0:C_CONV]
        gate = uc_ref[pl.ds(base, CHUNK), C_CONV:2 * C_CONV]
        vs_ref[pl.ds(base + CHUNK, CHUNK), :] = val * _sigmoid(gate)
        return carry

    lax.fori_loop(0, n_chunk, glu, 0)


def _fwd_conv(uc, conv_w, conv_b, ln_g, ln_b, token, n_ex):
    rows = uc.shape[0]
    lp = rows // n_ex
    n_chunk = lp // CHUNK

    def body(uc_ref, w_ref, b_ref, lg_ref, lb_ref, token_ref, ypre_ref, yc_ref, vs_ref):
        _glu_into(uc_ref, vs_ref, n_chunk)

        def conv(i, carry):
            base = pl.multiple_of(i * CHUNK, CHUNK)
            for lb in range(C_CONV // LANES):
                ls = slice(lb * LANES, (lb + 1) * LANES)
                win = vs_ref[pl.ds(base + CHUNK - HALO, CHUNK + HALO), ls]
                acc = jnp.broadcast_to(b_ref[:, ls], (CHUNK, LANES))
                for j, rows_j in _shifted(win, [HALO - (CONV_W - 1) + j for j in range(CONV_W)]):
                    acc = acc + w_ref[j:j + 1, ls] * rows_j
                ypre_ref[pl.ds(base, CHUNK), ls] = acc
            y = ypre_ref[pl.ds(base, CHUNK), :]
            mu = jnp.mean(y, axis=-1, keepdims=True)
            yc_ = y - mu
            rstd = lax.rsqrt(jnp.mean(yc_ * yc_, axis=-1, keepdims=True) + LN_EPS)
            s = yc_ * rstd * lg_ref[...] + lb_ref[...]
            yc_ref[pl.ds(base, CHUNK), :] = (s * _sigmoid(s)).astype(BF16)
            return carry

        lax.fori_loop(0, n_chunk, conv, 0, unroll=3)

    ex = lambda w: pl.BlockSpec((lp, w), lambda b: (b, 0))
    return pl.pallas_call(
        body, name="fwd_conv", grid=(n_ex,),
        in_specs=[ex(2 * C_CONV), _fixed((32, C_CONV)), _fixed((1, C_CONV)), _fixed((1, C_CONV)), _fixed((1, C_CONV)),
                  _fixed((8, 128))],
        out_specs=[ex(C_CONV), ex(C_CONV)],
        out_shape=[jax.ShapeDtypeStruct((rows, C_CONV), F32), jax.ShapeDtypeStruct((rows, C_CONV), BF16)],
        scratch_shapes=[pltpu.VMEM((lp + CHUNK, C_CONV), F32)],
        compiler_params=_params(("parallel",)),
    )(uc, conv_w, conv_b, ln_g, ln_b, token)


def _bwd_conv(uc, ypre, dyc, conv_w, ln_g, ln_b, token, n_ex):
    rows = uc.shape[0]
    lp = rows // n_ex
    n_chunk = lp // CHUNK

    def body(uc_ref, ypre_ref, dyc_ref, w_ref, lg_ref, lb_ref, token_ref, duc_ref, dw_ref, dvec_ref, vs_ref, dys_ref,
             dwacc_ref):
        _glu_into(uc_ref, vs_ref, n_chunk)
        dys_ref[pl.ds(lp, CHUNK), :] = jnp.zeros((CHUNK, C_CONV), F32)
        dwacc_ref[...] = jnp.zeros_like(dwacc_ref)

        def ln_bwd(i, carry):
            dcb, dlg, dlb = carry
            base = pl.multiple_of(i * CHUNK, CHUNK)
            y = ypre_ref[pl.ds(base, CHUNK), :]
            mu = jnp.mean(y, axis=-1, keepdims=True)
            yc_ = y - mu
            rstd = lax.rsqrt(jnp.mean(yc_ * yc_, axis=-1, keepdims=True) + LN_EPS)
            xh = yc_ * rstd
            s = xh * lg_ref[...] + lb_ref[...]
            sg = _sigmoid(s)
            ds = dyc_ref[pl.ds(base, CHUNK), :] * (sg * (1.0 + s * (1.0 - sg)))
            dxh = ds * lg_ref[...]
            dy = rstd * (dxh - jnp.mean(dxh, axis=-1, keepdims=True) - xh * jnp.mean(dxh * xh, axis=-1, keepdims=True))
            dys_ref[pl.ds(base, CHUNK), :] = dy
            return (dcb + jnp.sum(dy, axis=0, keepdims=True), dlg + jnp.sum(ds * xh, axis=0, keepdims=True),
                    dlb + jnp.sum(ds, axis=0, keepdims=True))

        zero = jnp.zeros((1, C_CONV), F32)
        dcb, dlg, dlb = lax.fori_loop(0, n_chunk, ln_bwd, (zero, zero, zero))

        @pl.when(pl.program_id(0) == 0)
        def _():
            dvec_ref[...] = jnp.zeros_like(dvec_ref)
            dw_ref[...] = jnp.zeros_like(dw_ref)

        dvec_ref[0:1, :] += dcb
        dvec_ref[1:2, :] += dlg
        dvec_ref[2:3, :] += dlb

        def taps(i, carry):
            base = pl.multiple_of(i * CHUNK, CHUNK)
            for lb in range(C_CONV // LANES):
                ls = slice(lb * LANES, (lb + 1) * LANES)
                dwin = dys_ref[pl.ds(base, CHUNK + HALO), ls]
                vwin = vs_ref[pl.ds(base + CHUNK - HALO, CHUNK + HALO), ls]
                dy = dwin[0:CHUNK]
                acc = jnp.zeros((CHUNK, LANES), F32)
                for j, rows_j in _shifted(dwin, [CONV_W - 1 - j for j in range(CONV_W)]):
                    acc = acc + w_ref[j:j + 1, ls] * rows_j
                for j, rows_j in _shifted(vwin, [HALO - (CONV_W - 1) + j for j in range(CONV_W)]):
                    dwacc_ref[8 * j:8 * j + 8, ls] += jnp.sum((dy * rows_j).reshape(CHUNK // 8, 8, LANES), axis=0)
                val = uc_ref[pl.ds(base, CHUNK), ls]
                gate = uc_ref[pl.ds(base, CHUNK), C_CONV + lb * LANES:C_CONV + (lb + 1) * LANES]
                sg = _sigmoid(gate)
                duc_ref[pl.ds(base, CHUNK), ls] = (acc * sg).astype(BF16)
                duc_ref[pl.ds(base, CHUNK), C_CONV + lb * LANES:C_CONV + (lb + 1) * LANES] = (
                    acc * val * sg * (1.0 - sg)).astype(BF16)
            return carry

        lax.fori_loop(0, n_chunk, taps, 0, unroll=3)
        for j in range(CONV_W):
            dw_ref[j:j + 1, :] += jnp.sum(dwacc_ref[8 * j:8 * j + 8, :], axis=0, keepdims=True)

    ex = lambda w: pl.BlockSpec((lp, w), lambda b: (b, 0))
    return pl.pallas_call(
        body, name="bwd_conv", grid=(n_ex,),
        in_specs=[ex(2 * C_CONV), ex(C_CONV), ex(C_CONV), _fixed((32, C_CONV)), _fixed((1, C_CONV)), _fixed((1, C_CONV)),
                  _fixed((8, 128))],
        out_specs=[ex(2 * C_CONV), _fixed((32, C_CONV)), _fixed((8, C_CONV))],
        out_shape=[jax.ShapeDtypeStruct((rows, 2 * C_CONV), BF16), jax.ShapeDtypeStruct((32, C_CONV), F32),
                   jax.ShapeDtypeStruct((8, C_CONV), F32)],
        scratch_shapes=[pltpu.VMEM((lp + CHUNK, C_CONV), F32), pltpu.VMEM((lp + CHUNK, C_CONV), F32),
                        pltpu.VMEM((8 * 32, C_CONV), F32)],
        compiler_params=_params(("arbitrary",)),
    )(uc, ypre, dyc, conv_w, ln_g, ln_b, token)


def _seg_chunks(n_chunk):
    return max(c for c in (11, 3, 1) if n_chunk % c == 0)


def _block_mask(shape, row_block, lane_block):
    return (lax.broadcasted_iota(jnp.int32, shape, 0) // row_block) == (lax.broadcasted_iota(jnp.int32, shape, 1) // lane_block)


def _per_head_rows(x, mask):
    return jnp.where(mask, jnp.concatenate([x] * GLA_H, axis=0), 0)


def _fold_heads(full, lane_block):
    lane = lax.broadcasted_iota(jnp.int32, (1, full.shape[1]), 1) // lane_block
    out = jnp.where(lane == 0, full[0:CHUNK], 0.0)
    for h in range(1, GLA_H):
        out = out + jnp.where(lane == h, full[h * CHUNK:(h + 1) * CHUNK], 0.0)
    return out


def _causal_heads():
    return (lax.broadcasted_iota(jnp.int32, (CHUNK, GLA_H * CHUNK), 1) % CHUNK) <= lax.broadcasted_iota(
        jnp.int32, (CHUNK, GLA_H * CHUNK), 0)


def _cumsum_rows(x):
    row = lax.broadcasted_iota(jnp.int32, x.shape, 0)
    s = 1
    while s < CHUNK:
        x = x + jnp.where(row >= s, pltpu.roll(x, s, 0), 0.0)
        s *= 2
    return x


def _rev_cumsum_rows(x):
    row = lax.broadcasted_iota(jnp.int32, x.shape, 0)
    s = 1
    while s < CHUNK:
        x = x + jnp.where(row < CHUNK - s, pltpu.roll(x, CHUNK - s, 0), 0.0)
        s *= 2
    return x


def _gate_terms(lr_ref, w2_ref, gb_ref, rs, first_pos):
    z = _dot(lr_ref[rs, :].astype(BF16), w2_ref[...]) + gb_ref[...]
    la = (jnp.minimum(z, 0.0) - jnp.log(1.0 + jnp.exp(-jnp.abs(z)))) * (1.0 / TAU)
    pos = first_pos + lax.broadcasted_iota(jnp.int32, (CHUNK, 1), 0)
    live = pos >= ZROWS
    la = jnp.where(live, la, 0.0)
    return z, live, _cumsum_rows(la)


def _fwd_gla(qk, vg, lr, w2p, gb, ng, token, n_ex):
    rows = qk.shape[0]
    lp = rows // n_ex
    n_chunk = lp // CHUNK
    sc = _seg_chunks(n_chunk)
    n_seg = n_chunk // sc
    seg = sc * CHUNK

    def body(qk_ref, vg_ref, lr_ref, w2_ref, gb_ref, ng_ref, token_ref, yg_ref, o_ref, st_ref, state_ref):
        sidx = pl.program_id(1)

        @pl.when(sidx == 0)
        def _():
            state_ref[...] = jnp.zeros_like(state_ref)

        causal = _causal_heads()
        k_mask = _block_mask((GLA_H * CHUNK, GLA_K), CHUNK, GLA_DK)
        v_mask = _block_mask((GLA_H * CHUNK, GLA_V), CHUNK, GLA_DV)
        s_mask = _block_mask((GLA_V, GLA_K), GLA_DV, GLA_DK)

        def chunk(ci, carry):
            base = pl.multiple_of(ci * CHUNK, CHUNK)
            rs = pl.ds(base, CHUNK)
            _, _, bcum = _gate_terms(lr_ref, w2_ref, gb_ref, rs, (sidx * sc + ci) * CHUNK)
            bl = bcum[CHUNK - 1:CHUNK, :]
            q = qk_ref[rs, 0:GLA_K]
            k = qk_ref[rs, GLA_K:2 * GLA_K]
            qt = (q * (GLA_DK ** -0.5) * jnp.exp(bcum)).astype(BF16)
            kt = (k * jnp.exp(-bcum)).astype(BF16)
            kh = (k * jnp.exp(bl - bcum)).astype(BF16)
            vb = vg_ref[rs, 0:GLA_V].astype(BF16)
            state = state_ref[...]
            st_ref[ci] = state
            a = jnp.where(causal, _dot(qt, _per_head_rows(kt, k_mask), _NT), 0.0)
            o = _dot(a.astype(BF16), _per_head_rows(vb, v_mask)) + _dot(qt, state.astype(BF16), _NT)
            o_ref[rs, :] = o
            for h in range(GLA_H):
                hs = slice(h * GLA_DV, (h + 1) * GLA_DV)
                oh = o[:, hs]
                ro = lax.rsqrt(jnp.mean(oh * oh, axis=-1, keepdims=True) + RMS_EPS)
                g = vg_ref[rs, GLA_V + h * GLA_DV:GLA_V + (h + 1) * GLA_DV]
                yg_ref[rs, hs] = (oh * ro * ng_ref[...] * g * _sigmoid(g)).astype(BF16)
            state_ref[...] = state * jnp.exp(bl) + jnp.where(s_mask, _dot(vb, kh, _TN), 0.0)
            return carry

        lax.fori_loop(0, sc, chunk, 0, unroll=True)

    sg = lambda w: pl.BlockSpec((seg, w), lambda b, s: (b * n_seg + s, 0))
    return pl.pallas_call(
        body, name="fwd_gla", grid=(n_ex, n_seg),
        in_specs=[sg(2 * GLA_K), sg(2 * GLA_V), sg(RANK_P), _fixed((RANK_P, GLA_K)), _fixed((1, GLA_K)), _fixed((1, GLA_DV)),
                  _fixed((8, 128))],
        out_specs=[sg(GLA_V), sg(GLA_V), pl.BlockSpec((sc, GLA_V, GLA_K), lambda b, s: (b * n_seg + s, 0, 0))],
        out_shape=[jax.ShapeDtypeStruct((rows, GLA_V), BF16), jax.ShapeDtypeStruct((rows, GLA_V), F32),
                   jax.ShapeDtypeStruct((n_ex * n_chunk, GLA_V, GLA_K), F32)],
        scratch_shapes=[pltpu.VMEM((GLA_V, GLA_K), F32)],
        compiler_params=_params(("parallel", "arbitrary")),
    )(qk, vg, lr, w2p, gb, ng, token)


def _bwd_gla(qk, vg, lr, o, st, dyg, w2p, gb, ng, n_ex):
    rows = qk.shape[0]
    lp = rows // n_ex
    n_chunk = lp // CHUNK
    sc = _seg_chunks(n_chunk)
    n_seg = n_chunk // sc
    seg = sc * CHUNK

    def body(qk_ref, vg_ref, lr_ref, o_ref, st_ref, dyg_ref, w2_ref, gb_ref, ng_ref,
             dqk_ref, dvg_ref, dlr_ref, dw2_ref, dvec_ref, gt_ref, dz_ref):
        step = pl.program_id(1)
        sidx = n_seg - 1 - step

        @pl.when(step == 0)
        def _():
            gt_ref[...] = jnp.zeros_like(gt_ref)

        @pl.when((step == 0) & (pl.program_id(0) == 0))
        def _():
            dw2_ref[...] = jnp.zeros_like(dw2_ref)
            dvec_ref[...] = jnp.zeros_like(dvec_ref)

        causal = _causal_heads()
        k_mask = _block_mask((GLA_H * CHUNK, GLA_K), CHUNK, GLA_DK)
        v_mask = _block_mask((GLA_H * CHUNK, GLA_V), CHUNK, GLA_DV)
        s_mask = _block_mask((GLA_V, GLA_K), GLA_DV, GLA_DK)
        last_row = lax.broadcasted_iota(jnp.int32, (CHUNK, 1), 0) == CHUNK - 1
        ng = ng_ref[...]

        def chunk(ii, dng):
            ci = sc - 1 - ii
            base = pl.multiple_of(ci * CHUNK, CHUNK)
            rs = pl.ds(base, CHUNK)
            z, live, bcum = _gate_terms(lr_ref, w2_ref, gb_ref, rs, (sidx * sc + ci) * CHUNK)
            bl = bcum[CHUNK - 1:CHUNK, :]
            ebl = jnp.exp(bl)
            q = qk_ref[rs, 0:GLA_K]
            k = qk_ref[rs, GLA_K:2 * GLA_K]
            eb = jnp.exp(bcum)
            enb = jnp.exp(-bcum)
            ehb = jnp.exp(bl - bcum)
            qt = q * (GLA_DK ** -0.5) * eb
            kt = k * enb
            kh = k * ehb
            qtb = qt.astype(BF16)
            vb = vg_ref[rs, 0:GLA_V].astype(BF16)
            k_rows = _per_head_rows(kt.astype(BF16), k_mask)
            v_rows = _per_head_rows(vb, v_mask)
            gt = gt_ref[...]
            gtb = gt.astype(BF16)
            s_in = st_ref[ci]
            dos = []
            for h in range(GLA_H):
                hs = slice(h * GLA_DV, (h + 1) * GLA_DV)
                gs = slice(GLA_V + h * GLA_DV, GLA_V + (h + 1) * GLA_DV)
                oh = o_ref[rs, hs]
                ro = lax.rsqrt(jnp.mean(oh * oh, axis=-1, keepdims=True) + RMS_EPS)
                on = oh * ro
                g = vg_ref[rs, gs]
                sg = _sigmoid(g)
                dout = dyg_ref[rs, hs]
                dvg_ref[rs, gs] = (dout * on * ng * (sg * (1.0 + g * (1.0 - sg)))).astype(BF16)
                dw = dout * g * sg
                dng = dng + jnp.sum(dw * on, axis=0, keepdims=True)
                don = dw * ng
                dos.append((ro * (don - on * jnp.mean(don * on, axis=-1, keepdims=True))).astype(BF16))
            dob = jnp.concatenate(dos, axis=1)
            a = jnp.where(causal, _dot(qtb, k_rows, _NT), 0.0).astype(BF16)
            da = jnp.where(causal, _dot(dob, v_rows, _NT), 0.0).astype(BF16)
            dv = _fold_heads(_dot(a, dob, _TN), GLA_DV) + _dot(kh.astype(BF16), gtb, _NT)
            dvg_ref[rs, 0:GLA_V] = dv.astype(BF16)
            dkh = _dot(vb, gtb)
            dqt = _dot(da, k_rows) + _dot(dob, s_in.astype(BF16))
            dkt = _fold_heads(_dot(da, qtb, _TN), GLA_DK)
            dbl = jnp.sum(gt * s_in, axis=0, keepdims=True) * ebl + jnp.sum(dkh * kh, axis=0, keepdims=True)
            dqk_ref[rs, 0:GLA_K] = (dqt * (GLA_DK ** -0.5) * eb).astype(BF16)
            dqk_ref[rs, GLA_K:2 * GLA_K] = (dkt * enb + dkh * ehb).astype(BF16)
            db = dqt * qt - dkt * kt - dkh * kh
            db = jnp.where(last_row, db + dbl, db)
            dla = jnp.where(live, _rev_cumsum_rows(db), 0.0)
            dz_ref[rs, :] = dla * (1.0 / TAU) * (1.0 - _sigmoid(z))
            gt_ref[...] = jnp.where(s_mask, _dot(dob, qtb, _TN), 0.0) + gt * ebl
            return dng

        dng = lax.fori_loop(0, sc, chunk, jnp.zeros((1, GLA_DV), F32), unroll=True)
        dz = dz_ref[...]
        dzb = dz.astype(BF16)
        dlr_ref[...] = _dot(dzb, w2_ref[...], _NT).astype(BF16)
        dw2_ref[...] += _dot(lr_ref[...].astype(BF16), dzb, _TN)
        dvec_ref[0:1, :] += jnp.sum(dz, axis=0, keepdims=True)
        dvec_ref[1:2, 0:GLA_DV] += dng

    sg_ = lambda w: pl.BlockSpec((seg, w), lambda b, s: (b * n_seg + n_seg - 1 - s, 0))
    return pl.pallas_call(
        body, name="bwd_gla", grid=(n_ex, n_seg),
        in_specs=[sg_(2 * GLA_K), sg_(2 * GLA_V), sg_(RANK_P), sg_(GLA_V),
                  pl.BlockSpec((sc, GLA_V, GLA_K), lambda b, s: (b * n_seg + n_seg - 1 - s, 0, 0)), sg_(GLA_V),
                  _fixed((RANK_P, GLA_K)), _fixed((1, GLA_K)), _fixed((1, GLA_DV))],
        out_specs=[sg_(2 * GLA_K), sg_(2 * GLA_V), sg_(RANK_P), _fixed((RANK_P, GLA_K)), _fixed((8, GLA_K))],
        out_shape=[jax.ShapeDtypeStruct((rows, 2 * GLA_K), BF16), jax.ShapeDtypeStruct((rows, 2 * GLA_V), BF16),
                   jax.ShapeDtypeStruct((rows, RANK_P), BF16), jax.ShapeDtypeStruct((RANK_P, GLA_K), F32),
                   jax.ShapeDtypeStruct((8, GLA_K), F32)],
        scratch_shapes=[pltpu.VMEM((GLA_V, GLA_K), F32), pltpu.VMEM((seg, GLA_K), F32)],
        compiler_params=_params(("arbitrary", "arbitrary")),
    )(qk, vg, lr, o, st, dyg, w2p, gb, ng)


def _pad_rows(x, tgt):
    return jnp.pad(x, ((0, 0), (LEAD, 0), (0, 0))), jnp.pad(tgt, ((0, 0), (LEAD, 0), (0, 0)))


def _local_step(h0, tgt_p, p, pass_on, late_weights, send_early):
    n_ex, lp, _ = h0.shape
    rows = n_ex * lp
    meta = jnp.broadcast_to(p["meta"][None], (n_ex, N_META, D))
    h0 = lax.dynamic_update_slice(h0, meta, (0, ZROWS, 0)).reshape(rows, D)
    tgt_p = tgt_p.reshape(rows, D)

    uc, qk, vg, lr, n1 = _fwd_inproj(h0, p["g1"], p["w_in"])
    ypre, yc = _fwd_conv(uc, p["conv_w"], p["conv_b"], p["ln_g"], p["ln_b"], p["token"], n_ex)
    token = pass_on(yc)
    yg, o, st = _fwd_gla(qk, vg, lr, p["w2"], p["gb"], p["ng"], token, n_ex)
    w_out, wg, wu, wd = late_weights(yg)
    h1, n2 = _fwd_outproj(yc, yg, h0, w_out, p["g2"], token)
    f, da, db, dh2, dh1, dh1b, part = _ffn_rows(h1, n2, tgt_p, wg, wu, wd, p["g2"], p["g3"], lp)
    g = {}
    token = send_early("ffn", [_matmul_tn(a_, b_, name).reshape(N_DEV, FF_S, D) for a_, b_, name in (
        (da, n2, "dw_gate"), (db, n2, "dw_up"), (f, dh2, "dw_down"))])
    dyc, dyg = _bwd_outproj(dh1b, w_out, token)
    token = send_early("out", [_dw_out(yc, yg, dh1b).reshape(N_DEV, W_OUT_S, D)])
    duc, g["conv_w"], g["conv_vec"] = _bwd_conv(uc, ypre, dyc, p["conv_w"], p["ln_g"], p["ln_b"], token, n_ex)
    dqk, dvg, dlr, g["w2"], g["gla_vec"] = _bwd_gla(qk, vg, lr, o, st, dyg, p["w2"], p["gb"], p["ng"], n_ex)
    token = send_early("in", [_dw_blocked(n1, [duc, dqk, dvg, dlr], W_IN_S, "dw_in")])
    grad_x, g["in_vec"], g["meta"] = _bwd_inproj(duc, dqk, dvg, dlr, dh1, h0, p["w_in"], p["g1"], token, lp)
    g["ffn_vec"] = part
    return grad_x, g


W_IN_S = D_IN // N_DEV
W_OUT_S = D // N_DEV
FF_S = D_FF // N_DEV
CONV_S = C_CONV // N_DEV
GATE_S = GLA_K // N_DEV
SMALL_PACK = 64
CONV_ROW = 16
GATE_ROW = 48
VEC_ROWS = 16
_VEC_ROWS = (("norm_mix_g", D), ("conv_b", C_CONV), ("conv_ln_g", C_CONV), ("conv_ln_b", C_CONV), ("gla_gate_b", GLA_K),
             ("gla_norm_g", GLA_DV), ("norm_ffn_g", D), ("norm_final_g", D))
LOSS_ROW = len(_VEC_ROWS)


def _position():
    return lax.axis_index("x"), lax.axis_index("y"), lax.axis_index("c")


def _any():
    return pl.BlockSpec(memory_space=pl.ANY)


def _stage(mats, meta, conv_w, w2):
    n_t = len(mats) + 1

    def body(*refs):
        ins = refs[0:n_t - 1]
        meta_ref, cw_ref, w2_ref = refs[n_t - 1:n_t + 2]
        lands = refs[n_t + 2:2 * n_t + 2]
        shards = refs[2 * n_t + 2:3 * n_t + 2]
        sems = refs[3 * n_t + 2]
        for s_ref, w_ref in zip(shards, ins):
            s_ref[...] = w_ref[...].astype(BF16)
        sp = shards[n_t - 1]
        sp[...] = jnp.zeros_like(sp)
        sp[0:N_META, :] = meta_ref[...]
        sp[CONV_ROW:CONV_ROW + CONV_W, 0:CONV_S] = cw_ref[...]
        sp[GATE_ROW:GATE_ROW + RANK, 0:GATE_S] = w2_ref[...]
        x, y, c = _position()
        mine = [pltpu.make_async_copy(shards[t], lands[t].at[4 * x + 2 * y + c], sems.at[t]) for t in range(n_t)]
        for cp in mine:
            cp.start()
        for cp in mine:
            cp.wait()

    shard_shapes = [jax.ShapeDtypeStruct(m.shape, BF16) for m in mats] + [jax.ShapeDtypeStruct((SMALL_PACK, 128), F32)]
    res = pl.pallas_call(
        body, name="stage",
        out_shape=[jax.ShapeDtypeStruct((N_DEV,) + s.shape, s.dtype) for s in shard_shapes] + shard_shapes,
        in_specs=[_whole_vmem()] * (n_t + 2), out_specs=[_any()] * n_t + [_whole_vmem()] * n_t,
        scratch_shapes=[pltpu.SemaphoreType.DMA((n_t,))],
        compiler_params=pltpu.CompilerParams(vmem_limit_bytes=VMEM_LIMIT),
    )(*mats, meta, conv_w, w2)
    return res[0:n_t], res[n_t:]


_HBM = pl.BlockSpec(memory_space=pltpu.HBM)
_SEM = pl.BlockSpec(memory_space=pltpu.SEMAPHORE)
_EFFECT = pltpu.SideEffectType.DATAFLOW_SIDE_EFFECTING


_N_ROUTES = {"scatter": 7, "first": 4, "forward": 3}


def _routes(mode):
    x, y, c = _position()
    me = 4 * x + 2 * y + c
    if mode == "scatter":
        out = []
        for k in range(1, N_DEV):
            px = 1 - x if k & 4 else x
            py = 1 - y if k & 2 else y
            pc = 1 - c if k & 1 else c
            out.append(((px, py, pc), 4 * px + 2 * py + pc, me))
        return out
    if mode == "first":
        return [(pos, None, me) for pos in ((x, y, 1 - c), (1 - x, y, c), (x, 1 - y, c), (1 - x, 1 - y, c))]
    assert mode == "forward"
    return [((x, y, 1 - c), 4 * px + 2 * py + c, 4 * px + 2 * py + c) for px, py in ((1 - x, y), (x, 1 - y), (1 - x, 1 - y))]


def _route_copies(mode, n, src_refs, land_refs, send_sems, recv_sems):
    nr = _N_ROUTES[mode]
    for i, (pos, src_blk, dst_blk) in enumerate(_routes(mode)):
        for t in range(n):
            src = land_refs[t] if mode == "forward" else src_refs[t]
            yield pltpu.make_async_remote_copy(
                src_ref=src if src_blk is None else src.at[src_blk], dst_ref=land_refs[t].at[dst_blk],
                send_sem=send_sems.at[nr * t + i], recv_sem=recv_sems.at[nr * t + i], device_id=pos, device_id_type=MESH)


def _in_hbm(a):
    return pltpu.with_memory_space_constraint(a, pltpu.HBM)


def _send_start(name, srcs, lands, mode, after):
    n, ns = len(lands), len(srcs)
    nsem = _N_ROUTES[mode] * n

    def body(*refs):
        src_refs, land_refs = refs[0:ns], refs[ns:ns + n]
        send_sems, recv_sems = refs[ns + n + 1:ns + n + 3]
        token = refs[2 * (ns + n) + 3]
        for cp in _route_copies(mode, n, src_refs, land_refs, send_sems, recv_sems):
            cp.start()
        token[...] = jnp.zeros_like(token)

    bufs = list(srcs) + list(lands)
    res = pl.pallas_call(
        body, name=name,
        out_shape=(pltpu.SemaphoreType.DMA((nsem,)), pltpu.SemaphoreType.DMA((nsem,)),
                   *[pltpu.HBM(b.shape, b.dtype) for b in bufs], jax.ShapeDtypeStruct((8, 128), F32)),
        in_specs=[_HBM] * len(bufs) + [_any()], out_specs=(_SEM, _SEM, *[_HBM] * len(bufs), _whole_vmem()),
        input_output_aliases={i: 2 + i for i in range(len(bufs))},
        compiler_params=pltpu.CompilerParams(has_side_effects=_EFFECT),
    )(*[_in_hbm(b) for b in bufs], after)
    return res[0], res[1], res[2:2 + ns], res[2 + ns:2 + ns + n], res[2 + ns + n]


def _send_wait(name, send_sems, recv_sems, srcs, lands, mode, after):
    n, ns = len(lands), len(srcs)
    after = after if isinstance(after, tuple) else (after,)

    def body(*refs):
        src_refs, land_refs = refs[0:ns], refs[ns:ns + n]
        send_sems, recv_sems = refs[ns + n:ns + n + 2]
        for cp in _route_copies(mode, n, src_refs, land_refs, send_sems, recv_sems):
            cp.wait_send()
            cp.wait_recv()

    bufs = list(srcs) + list(lands)
    res = pl.pallas_call(
        body, name=name,
        out_shape=tuple(pltpu.HBM(b.shape, b.dtype) for b in bufs),
        in_specs=[_HBM] * len(bufs) + [_SEM, _SEM] + [_any()] * len(after), out_specs=tuple([_HBM] * len(bufs)),
        input_output_aliases={i: i for i in range(len(bufs))},
        compiler_params=pltpu.CompilerParams(has_side_effects=_EFFECT),
    )(*bufs, send_sems, recv_sems, *after)
    return res[0:ns], res[ns:ns + n]


def _unshard_in(a_in, a_small, token):
    def body(a_ref, s_ref, token_ref, w_ref, meta_ref, cw_ref, w2_ref):
        w_ref[:, D_IN:D_INP] = jnp.zeros((D, D_INP - D_IN), BF16)
        w2_ref[...] = jnp.zeros_like(w2_ref)
        for d in range(N_DEV):
            w_ref[:, d * W_IN_S:(d + 1) * W_IN_S] = a_ref[d]
            meta_ref[:, d * 128:(d + 1) * 128] = s_ref[d, 0:N_META, :]
            cw_ref[:, d * CONV_S:(d + 1) * CONV_S] = s_ref[d, CONV_ROW:CONV_ROW + 32, 0:CONV_S]
            w2_ref[0:RANK, d * GATE_S:(d + 1) * GATE_S] = s_ref[d, GATE_ROW:GATE_ROW + RANK, 0:GATE_S].astype(BF16)

    return pl.pallas_call(
        body, name="unshard_in",
        out_shape=[jax.ShapeDtypeStruct((D, D_INP), BF16), jax.ShapeDtypeStruct((N_META, D), F32),
                   jax.ShapeDtypeStruct((32, C_CONV), F32), jax.ShapeDtypeStruct((RANK_P, GLA_K), BF16)],
        compiler_params=pltpu.CompilerParams(vmem_limit_bytes=VMEM_LIMIT),
    )(a_in, a_small, token)


def _pack_small(g):
    def body(meta_ref, cw_ref, w2_ref, in_vec, ffn_vec, conv_vec, gla_vec, sp, vp):
        sp[...] = jnp.zeros_like(sp)
        vp[...] = jnp.zeros_like(vp)
        for d in range(N_DEV):
            sp[d, 0:N_META, :] = meta_ref[:, d * 128:(d + 1) * 128]
            sp[d, CONV_ROW:CONV_ROW + 32, 0:CONV_S] = cw_ref[:, d * CONV_S:(d + 1) * CONV_S]
            sp[d, GATE_ROW:GATE_ROW + RANK, 0:GATE_S] = w2_ref[0:RANK, d * GATE_S:(d + 1) * GATE_S]
            vp[d, 0:1, :] = in_vec[0:1, :]
            vp[d, 1:4, 0:C_CONV] = conv_vec[0:3, :]
            vp[d, 4:5, 0:GLA_K] = gla_vec[0:1, :]
            vp[d, 5:6, 0:GLA_DV] = gla_vec[1:2, 0:GLA_DV]
            vp[d, 6:7, :] = ffn_vec[1:2, :]
            vp[d, 7:8, :] = ffn_vec[0:1, :]
            vp[d, LOSS_ROW:LOSS_ROW + 1, :] = ffn_vec[2:3, :]

    return pl.pallas_call(
        body, name="pack_small",
        out_shape=[jax.ShapeDtypeStruct((N_DEV, SMALL_PACK, 128), F32), jax.ShapeDtypeStruct((N_DEV, VEC_ROWS, D), F32)],
    )(g["meta"], g["conv_w"], g["w2"], g["in_vec"], g["ffn_vec"], g["conv_vec"], g["gla_vec"])


def _adamw(w, g, m, v):
    m = ADAM_B1 * m + (1.0 - ADAM_B1) * g
    v = ADAM_B2 * v + (1.0 - ADAM_B2) * (g * g)
    m_hat = m / (1.0 - ADAM_B1 ** ADAM_STEP)
    v_hat = v / (1.0 - ADAM_B2 ** ADAM_STEP)
    return -ADAM_LR * (m_hat / (jnp.sqrt(v_hat) + ADAM_EPS) + ADAM_WD * w), m, v


def _update_matrix(recv, own, me, w, m, v, name):
    _, r, c = recv.shape
    tr = _row_tile(r, 256)

    def body(me_ref, recv_ref, own_ref, w_ref, m_ref, v_ref, g_ref, d_ref, nm_ref, nv_ref):
        g = jnp.zeros((tr, c), F32)
        for s in range(N_DEV):
            g = g + jnp.where(me_ref[0] == s, own_ref[...], recv_ref[s]).astype(F32)
        g_ref[...] = g
        d_ref[...], nm_ref[...], nv_ref[...] = _adamw(w_ref[...], g, m_ref[...], v_ref[...])

    one = pl.BlockSpec((None, tr, c), lambda i, me_ref: (0, i, 0))
    return pl.pallas_call(
        body, name=name,
        grid_spec=pltpu.PrefetchScalarGridSpec(
            num_scalar_prefetch=1, grid=(r // tr,),
            in_specs=[pl.BlockSpec((N_DEV, tr, c), lambda i, me_ref: (0, i, 0)),
                      pl.BlockSpec((None, tr, c), lambda i, me_ref: (me_ref[0], i, 0)), one, one, one],
            out_specs=[one] * 4),
        out_shape=[jax.ShapeDtypeStruct((1, r, c), F32)] * 4,
        compiler_params=_params(("parallel",)),
    )(me, recv, own, w, m, v)


_SMALL = ("meta_tokens", "conv_w", "gla_w_gate2") + tuple(n for n, _ in _VEC_ROWS)


def _update_small(me, srecv, vrecv, sown, vown, w, m, v):
    n = len(_SMALL)

    def body(*refs):
        me_ref, s_ref, v_ref, so_ref, vo_ref = refs[0:5]
        w_refs, m_refs, v_refs = refs[5:5 + n], refs[5 + n:5 + 2 * n], refs[5 + 2 * n:5 + 3 * n]
        outs = refs[5 + 3 * n:]
        ssum = jnp.zeros((SMALL_PACK, 128), F32)
        vsum = jnp.zeros((VEC_ROWS, D), F32)
        for s in range(N_DEV):
            ssum = ssum + jnp.where(me_ref[0] == s, so_ref[s], s_ref[s])
            vsum = vsum + jnp.where(me_ref[0] == s, vo_ref[s], v_ref[s])
        grads = [ssum[0:N_META, :], ssum[CONV_ROW:CONV_ROW + CONV_W, 0:CONV_S], ssum[GATE_ROW:GATE_ROW + RANK, 0:GATE_S]]
        grads += [vsum[i:i + 1, 0:width] for i, (_, width) in enumerate(_VEC_ROWS)]
        for i, g in enumerate(grads):
            d, nm, nv = _adamw(w_refs[i][...], g, m_refs[i][...], v_refs[i][...])
            outs[i][...] = g
            outs[n + i][...] = d
            outs[2 * n + i][...] = nm
            outs[3 * n + i][...] = nv
        outs[4 * n][...] = vsum[LOSS_ROW:LOSS_ROW + 1, 0:128]

    shapes = [jax.ShapeDtypeStruct(t.shape, F32) for t in w]
    res = pl.pallas_call(
        body, name="update_small", out_shape=shapes * 4 + [jax.ShapeDtypeStruct((1, 128), F32)],
        in_specs=[pl.BlockSpec(memory_space=pltpu.SMEM)] + [_whole_vmem()] * (4 + 3 * n),
    )(me, srecv, vrecv, sown, vown, *w, *m, *v)
    return res[0:n], res[n:2 * n], res[2 * n:3 * n], res[3 * n:4 * n], res[4 * n]


_WEIGHTS = ("meta_tokens", "norm_mix_g", "w_in", "conv_w", "conv_b", "conv_ln_g", "conv_ln_b", "gla_w_gate2", "gla_gate_b",
            "gla_norm_g", "w_out", "norm_ffn_g", "w_ffn_gate", "w_ffn_up", "w_ffn_down", "norm_final_g")
_MATRICES = ("w_in", "w_out", "w_ffn_gate", "w_ffn_up", "w_ffn_down")
_TRANSPOSED = ("w_ffn_gate", "w_ffn_up")


def kernel(x, meta_tokens, norm_mix_g, w_in, conv_w, conv_b, conv_ln_g, conv_ln_b, gla_w_gate2, gla_gate_b, gla_norm_g, w_out, norm_ffn_g, w_ffn_gate, w_ffn_up, w_ffn_down, norm_final_g, loss_target, m_meta_tokens, m_norm_mix_g, m_w_in, m_conv_w, m_conv_b, m_conv_ln_g, m_conv_ln_b, m_gla_w_gate2, m_gla_gate_b, m_gla_norm_g, m_w_out, m_norm_ffn_g, m_w_ffn_gate, m_w_ffn_up, m_w_ffn_down, m_norm_final_g, v_meta_tokens, v_norm_mix_g, v_w_in, v_conv_w, v_conv_b, v_conv_ln_g, v_conv_ln_b, v_gla_w_gate2, v_gla_gate_b, v_gla_norm_g, v_w_out, v_norm_ffn_g, v_w_ffn_gate, v_w_ffn_up, v_w_ffn_down, v_norm_final_g):
    given = dict(locals())
    two_d = lambda a: a.reshape(1, -1) if a.ndim == 1 else a.reshape(a.shape[-2:])
    fams = [{n: given[pre + n] for n in _WEIGHTS} for pre in ("", "m_", "v_")]
    for f in fams:
        for n in _TRANSPOSED:
            f[n] = f[n].transpose(0, 2, 1)
    w = fams[0]

    lands, shards = _stage([two_d(w[n]) for n in _MATRICES], w["meta_tokens"], two_d(w["conv_w"]), two_d(w["gla_w_gate2"]))
    soon, later = (0, 5), (1, 2, 3, 4)
    pick = lambda seq, idx: [seq[i] for i in idx]
    first = _send_start("gather_first_start", pick(shards, soon), pick(lands, soon), "first", norm_mix_g)
    ffn_first = _send_start("gather_ffn_first_start", pick(shards, later), pick(lands, later), "first", first[4])
    h0, tgt_p = _pad_rows(x, loss_target)
    _, arrived = _send_wait("gather_first_wait", *first[0:4], "first", (h0, tgt_p, ffn_first[4]))
    forward = _send_start("gather_forward_start", [], arrived, "forward", ffn_first[4])
    _, (a_in, a_small) = _send_wait("gather_forward_wait", *forward[0:4], "forward", forward[4])
    w_in, meta, conv_taps, w2 = _unshard_in(a_in, a_small, forward[4])
    p = dict(meta=meta, conv_w=conv_taps, w2=w2, w_in=w_in, g1=norm_mix_g, conv_b=conv_b, ln_g=conv_ln_g, ln_b=conv_ln_b,
             gb=gla_gate_b, ng=gla_norm_g, g2=norm_ffn_g, g3=two_d(norm_final_g), token=forward[4])
    passed = {}

    def pass_on(after):
        _, arrived_ffn = _send_wait("gather_ffn_first_wait", *ffn_first[0:4], "first", after)
        passed["sent"] = _send_start("gather_ffn_forward_start", [], arrived_ffn, "forward", after)
        return passed["sent"][4]

    def late_weights(after):
        _, (a_out, a_g, a_u, a_d) = _send_wait("gather_ffn_forward_wait", *passed["sent"][0:4], "forward", after)
        return a_out.reshape(D, D), a_g.reshape(D_FF, D), a_u.reshape(D_FF, D), a_d.reshape(D_FF, D)

    sent = {}

    def send_early(tag, mats):
        landing = [_in_hbm(lax.empty(m_.shape, m_.dtype)) for m_ in mats]
        sent[tag] = _send_start("scatter_" + tag + "_start", mats, landing, "scatter", norm_mix_g)
        return sent[tag][4]

    grad_x, g = _local_step(h0, tgt_p, p, pass_on, late_weights, send_early)

    token = send_early("small", list(_pack_small(g)))
    x_, y_, c_ = _position()
    me = (4 * x_ + 2 * y_ + c_).astype(jnp.int32).reshape(1)
    res = {}
    for tag, names in (("ffn", ("w_ffn_gate", "w_ffn_up", "w_ffn_down")), ("out", ("w_out",)), ("in", ("w_in",))):
        own, recv = _send_wait("scatter_" + tag + "_wait", *sent[tag][0:4], "scatter", token)
        for n, o_, r_ in zip(names, own, recv):
            res[n] = _update_matrix(r_, o_, me, *[f[n] for f in fams], "update_" + n)
            token = res[n][1]
    (sown, vown), (srecv, vrecv) = _send_wait("scatter_small_wait", *sent["small"][0:4], "scatter", token)
    small = _update_small(me, srecv, vrecv, sown, vown, *[[two_d(f[n]) for n in _SMALL] for f in fams])
    for i, n in enumerate(_SMALL):
        res[n] = [fam[i].reshape(w[n].shape) for fam in small[0:4]]
    for n in _TRANSPOSED:
        res[n] = [t.transpose(0, 2, 1) for t in res[n]]
    outs = [small[4][0, 0], grad_x]
    for k in range(4):
        outs += [res[n][k] for n in _WEIGHTS]
    return tuple(outs)
```

```python
import functools

import jax
import jax.numpy as jnp
from jax import lax
from jax.experimental import pallas as pl
from jax.experimental.pallas import tpu as pltpu

F32 = jnp.float32
BF16 = jnp.bfloat16

D = 1024
N_META = 16
C_CONV = 512
CONV_W = 31
GLA_H = 4
GLA_DK = 64
GLA_DV = 128
GLA_K = GLA_H * GLA_DK
GLA_V = GLA_H * GLA_DV
RANK = 16
RANK_P = 128
TAU = 16.0
CHUNK = 64
LEAD = CHUNK
ZROWS = LEAD - N_META
D_IN = 2 * C_CONV + 2 * GLA_K + 2 * GLA_V + RANK
D_INP = D_IN - RANK + RANK_P
D_FF = 2816
FF_CHUNK = 1408
FF_SPLIT = (0, 1536, D_FF)
RMS_EPS = 1e-6
LN_EPS = 1e-5
N_DEV = 8

ADAM_LR = 0.001
ADAM_B1 = 0.9
ADAM_B2 = 0.999
ADAM_EPS = 1e-08
ADAM_WD = 0.01
ADAM_STEP = 10

VMEM_LIMIT = 60 * 1024 * 1024
ROW_TILE = 1056
FFN_ROW_TILE = 352
DW_ROW_TILE = 1408
MESH = pl.DeviceIdType.MESH

_NN = (((1,), (0,)), ((), ()))
_NT = (((1,), (1,)), ((), ()))
_TN = (((0,), (0,)), ((), ()))


def _dot(a, b, dims=_NN):
    return lax.dot_general(a, b, dims, preferred_element_type=F32)


def _sigmoid(x):
    return 1.0 / (1.0 + jnp.exp(-x))


def _row_tile(rows, target):
    best = None
    for t in range(16, min(rows, target) + 1, 16):
        if rows % t == 0:
            best = t
    assert best is not None, rows
    return best


def _params(sem=None):
    return pltpu.CompilerParams(dimension_semantics=sem, vmem_limit_bytes=VMEM_LIMIT)


def _whole_vmem():
    return pl.BlockSpec(memory_space=pltpu.VMEM)


def _rows(tm, width):
    return pl.BlockSpec((tm, width), lambda i: (i, 0))


def _fixed(shape):
    return pl.BlockSpec(shape, lambda *_: (0,) * len(shape))


def _fwd_inproj(h0, g1, w_in):
    rows = h0.shape[0]
    tm = _row_tile(rows, ROW_TILE)

    def body(h_ref, g_ref, w_ref, uc_ref, qk_ref, vg_ref, lr_ref, n1_ref):
        h = h_ref[...]
        r = lax.rsqrt(jnp.mean(h * h, axis=-1, keepdims=True) + RMS_EPS)
        n = (h * r * g_ref[...]).astype(BF16)
        n1_ref[...] = n
        uc_ref[...] = _dot(n, w_ref[:, 0:1024])
        qk_ref[...] = _dot(n, w_ref[:, 1024:1536])
        vg_ref[...] = _dot(n, w_ref[:, 1536:2560])
        lr_ref[...] = _dot(n, w_ref[:, 2560:2688])

    return pl.pallas_call(
        body, name="fwd_inproj", grid=(rows // tm,),
        in_specs=[_rows(tm, D), _fixed((1, D)), _whole_vmem()],
        out_specs=[_rows(tm, 1024), _rows(tm, 512), _rows(tm, 1024), _rows(tm, RANK_P), _rows(tm, D)],
        out_shape=[jax.ShapeDtypeStruct((rows, 1024), F32), jax.ShapeDtypeStruct((rows, 512), F32),
                   jax.ShapeDtypeStruct((rows, 1024), F32), jax.ShapeDtypeStruct((rows, RANK_P), F32),
                   jax.ShapeDtypeStruct((rows, D), BF16)],
        compiler_params=_params(("parallel",)),
    )(h0, g1, w_in)


def _mid_rows(yc, yg, h0, tgt, w_out, wg, wu, wd, g2, g3, token, rows_per_example):
    rows = h0.shape[0]
    tm = _row_tile(rows, FFN_ROW_TILE)
    ff_blocks = [slice(lo, hi) for lo, hi in zip(FF_SPLIT[:-1], FF_SPLIT[1:])]

    def body(yc_ref, yg_ref, h0_ref, t_ref, wo_ref, wg_ref, wu_ref, wd_ref, g2_ref, g3_ref, token_ref,
             n2_ref, f_ref, da_ref, db_ref, dh2_ref, dh1_ref, dh1b_ref, dyc_ref, dyg_ref, part_ref):
        i = pl.program_id(0)
        h1 = h0_ref[...] + _dot(yc_ref[...], wo_ref[0:C_CONV, :]) + _dot(yg_ref[...], wo_ref[C_CONV:D, :])
        r2 = lax.rsqrt(jnp.mean(h1 * h1, axis=-1, keepdims=True) + RMS_EPS)
        xh2 = h1 * r2
        n2 = (xh2 * g2_ref[...]).astype(BF16)
        n2_ref[...] = n2
        y2 = jnp.zeros((tm, D), F32)
        for cs in ff_blocks:
            a = _dot(n2, wg_ref[cs, :], _NT)
            b = _dot(n2, wu_ref[cs, :], _NT)
            f = (a * _sigmoid(a) * b).astype(BF16)
            f_ref[:, cs] = f
            da_ref[:, cs] = a.astype(BF16)
            db_ref[:, cs] = b.astype(BF16)
            y2 = y2 + _dot(f, wd_ref[cs, :])
        h2 = h1 + y2
        r3 = lax.rsqrt(jnp.mean(h2 * h2, axis=-1, keepdims=True) + RMS_EPS)
        xh3 = h2 * r3
        g3 = g3_ref[...]
        pos = (i * tm + lax.broadcasted_iota(jnp.int32, (tm, 1), 0)) % rows_per_example
        valid = pos >= LEAD
        err = jnp.where(valid, xh3 * g3 - t_ref[...], 0.0)
        loss = 0.5 / D * jnp.sum(jnp.sum(err * err, axis=-1, keepdims=True), axis=0, keepdims=True)
        dy = err * (1.0 / D)
        dg3 = jnp.sum(dy * xh3, axis=0, keepdims=True)
        dxh = dy * g3
        dh2 = r3 * (dxh - xh3 * jnp.mean(dxh * xh3, axis=-1, keepdims=True))
        dh2b = dh2.astype(BF16)
        dh2_ref[...] = dh2b
        dn2 = jnp.zeros((tm, D), F32)
        for cs in ff_blocks:
            df = _dot(dh2b, wd_ref[cs, :], _NT)
            a = da_ref[:, cs].astype(F32)
            b = db_ref[:, cs].astype(F32)
            sg = _sigmoid(a)
            da = (df * b * sg * (1.0 + a * (1.0 - sg))).astype(BF16)
            db = (df * a * sg).astype(BF16)
            da_ref[:, cs] = da
            db_ref[:, cs] = db
            dn2 = dn2 + _dot(da, wg_ref[cs, :]) + _dot(db, wu_ref[cs, :])
        dg2 = jnp.sum(dn2 * xh2, axis=0, keepdims=True)
        dxh2 = dn2 * g2_ref[...]
        dh1 = dh2 + r2 * (dxh2 - xh2 * jnp.mean(dxh2 * xh2, axis=-1, keepdims=True))
        dh1_ref[...] = dh1
        dh1b = dh1.astype(BF16)
        dh1b_ref[...] = dh1b
        dyc_ref[...] = _dot(dh1b, wo_ref[0:C_CONV, :], _NT)
        dyg_ref[...] = _dot(dh1b, wo_ref[C_CONV:D, :], _NT)

        @pl.when(i == 0)
        def _():
            part_ref[...] = jnp.zeros_like(part_ref)

        part_ref[0:1, :] += dg3
        part_ref[1:2, :] += dg2
        part_ref[2:3, :] += jnp.broadcast_to(loss, (1, D))

    return pl.pallas_call(
        body, name="mid_rows", grid=(rows // tm,),
        in_specs=[_rows(tm, C_CONV), _rows(tm, GLA_V), _rows(tm, D), _rows(tm, D), _whole_vmem(), _whole_vmem(),
                  _whole_vmem(), _whole_vmem(), _fixed((1, D)), _fixed((1, D)), _fixed((8, 128))],
        out_specs=[_rows(tm, D), _rows(tm, D_FF), _rows(tm, D_FF), _rows(tm, D_FF), _rows(tm, D), _rows(tm, D),
                   _rows(tm, D), _rows(tm, C_CONV), _rows(tm, GLA_V), _fixed((8, D))],
        out_shape=[jax.ShapeDtypeStruct((rows, D), BF16)] + [jax.ShapeDtypeStruct((rows, D_FF), BF16)] * 3
        + [jax.ShapeDtypeStruct((rows, D), BF16), jax.ShapeDtypeStruct((rows, D), F32),
           jax.ShapeDtypeStruct((rows, D), BF16), jax.ShapeDtypeStruct((rows, C_CONV), F32),
           jax.ShapeDtypeStruct((rows, GLA_V), F32), jax.ShapeDtypeStruct((8, D), F32)],
        compiler_params=_params(("arbitrary",)),
    )(yc, yg, h0, tgt, w_out, wg, wu, wd, g2, g3, token)


def _bwd_inproj(duc, dqk, dvg, dlr, dh1, h0, w_in, g1, token, rows_per_example):
    rows = h0.shape[0]
    n_ex = rows // rows_per_example
    tm = _row_tile(rows_per_example, ROW_TILE)
    tiles_per_example = rows_per_example // tm
    n_steps = rows // tm

    def body(duc_ref, dqk_ref, dvg_ref, dlr_ref, dh1_ref, h_ref, w_ref, g_ref, token_ref, gx_ref, part_ref, dmeta_ref,
             buf_ref, sems):
        dn = (_dot(duc_ref[...], w_ref[:, 0:1024], _NT) + _dot(dqk_ref[...], w_ref[:, 1024:1536], _NT)
              + _dot(dvg_ref[...], w_ref[:, 1536:2560], _NT) + _dot(dlr_ref[...], w_ref[:, 2560:2688], _NT))
        h = h_ref[...]
        r = lax.rsqrt(jnp.mean(h * h, axis=-1, keepdims=True) + RMS_EPS)
        xh = h * r
        dg = jnp.sum(dn * xh, axis=0, keepdims=True)
        dxh = dn * g_ref[...]
        dh0 = dh1_ref[...] + r * (dxh - xh * jnp.mean(dxh * xh, axis=-1, keepdims=True))
        i = pl.program_id(0)

        def copies(step):
            slot, b, j = step % 2, step // tiles_per_example, step % tiles_per_example
            out = [(j == 0, pltpu.make_async_copy(buf_ref.at[slot, pl.ds(LEAD, tm - LEAD)],
                                                   gx_ref.at[b, pl.ds(0, tm - LEAD)], sems.at[slot]))]
            if tiles_per_example > 1:
                out.append((j != 0, pltpu.make_async_copy(
                    buf_ref.at[slot], gx_ref.at[b, pl.ds(pl.multiple_of(jnp.maximum(j * tm - LEAD, 0), 8), tm)],
                    sems.at[slot])))
            return out

        def each(step, act):
            for cond, cp in copies(step):
                pl.when(cond)(functools.partial(act, cp))

        @pl.when(i >= 2)
        def _():
            each(i - 2, lambda cp: cp.wait())

        buf_ref[i % 2] = dh0
        each(i, lambda cp: cp.start())

        @pl.when(i == n_steps - 1)
        def _():
            each(i, lambda cp: cp.wait())
            if n_steps > 1:
                each(i - 1, lambda cp: cp.wait())

        @pl.when(i == 0)
        def _():
            part_ref[...] = jnp.zeros_like(part_ref)
            dmeta_ref[...] = jnp.zeros_like(dmeta_ref)

        part_ref[0:1, :] += dg

        @pl.when(i % tiles_per_example == 0)
        def _():
            dmeta_ref[...] += dh0[ZROWS:LEAD, :]

    return pl.pallas_call(
        body, name="bwd_inproj", grid=(n_steps,),
        in_specs=[_rows(tm, 1024), _rows(tm, 512), _rows(tm, 1024), _rows(tm, RANK_P), _rows(tm, D), _rows(tm, D),
                  _whole_vmem(), _fixed((1, D)), _fixed((8, 128))],
        out_specs=[_any(), _fixed((8, D)), _fixed((N_META, D))],
        out_shape=[jax.ShapeDtypeStruct((n_ex, rows_per_example - LEAD, D), F32), jax.ShapeDtypeStruct((8, D), F32),
                   jax.ShapeDtypeStruct((N_META, D), F32)],
        scratch_shapes=[pltpu.VMEM((2, tm, D), F32), pltpu.SemaphoreType.DMA((2,))],
        compiler_params=_params(("arbitrary",)),
    )(duc, dqk, dvg, dlr, dh1, h0, w_in, g1, token)


def _dw_blocked(a, bs, width, name):
    rows, m = a.shape
    ws = [b.shape[1] for b in bs]
    assert sum(ws) >= N_DEV * width
    tk = _row_tile(rows, DW_ROW_TILE)
    nk = rows // tk

    def body(a_ref, *refs):
        b_refs, o_ref, acc_ref = refs[:len(bs)], refs[len(bs)], refs[len(bs) + 1]
        k = pl.program_id(0)

        @pl.when(k == 0)
        def _():
            acc_ref[...] = jnp.zeros_like(acc_ref)

        at = a_ref[...].T
        off = 0
        for b_ref, w in zip(b_refs, ws):
            acc_ref[:, off:off + w] += _dot(at, b_ref[...])
            off += w

        @pl.when(k == nk - 1)
        def _():
            for d in range(N_DEV):
                o_ref[d] = acc_ref[:, d * width:(d + 1) * width].astype(BF16)

    return pl.pallas_call(
        body, name=name, grid=(nk,),
        in_specs=[_rows(tk, m)] + [_rows(tk, w) for w in ws],
        out_specs=_fixed((N_DEV, m, width)),
        out_shape=jax.ShapeDtypeStruct((N_DEV, m, width), BF16),
        scratch_shapes=[pltpu.VMEM((m, sum(ws)), F32)],
        compiler_params=_params(("arbitrary",)),
    )(a, *bs)


def _dw_out(yc, yg, dh1b, token):
    rows = yc.shape[0]
    tk = _row_tile(rows, DW_ROW_TILE)
    nk = rows // tk

    def body(yc_ref, yg_ref, d_ref, token_ref, o_ref, acc_ref):
        k = pl.program_id(0)

        @pl.when(k == 0)
        def _():
            acc_ref[...] = jnp.zeros_like(acc_ref)

        d = d_ref[...]
        acc_ref[0:C_CONV, :] += _dot(yc_ref[...], d, _TN)
        acc_ref[C_CONV:D, :] += _dot(yg_ref[...], d, _TN)

        @pl.when(k == nk - 1)
        def _():
            o_ref[...] = acc_ref[...].astype(BF16)

    return pl.pallas_call(
        body, name="dw_out", grid=(nk,),
        in_specs=[_rows(tk, C_CONV), _rows(tk, GLA_V), _rows(tk, D), _fixed((8, 128))],
        out_specs=_fixed((D, D)), out_shape=jax.ShapeDtypeStruct((D, D), BF16),
        scratch_shapes=[pltpu.VMEM((D, D), F32)],
        compiler_params=_params(("arbitrary",)),
    )(yc, yg, dh1b, token)


def _matmul_tn(a, b, name):
    rows, m = a.shape
    n = b.shape[1]
    tk = _row_tile(rows, DW_ROW_TILE)
    tn = n if n <= 1024 else FF_CHUNK
    tm_ = m if m <= 1024 else FF_CHUNK
    assert n % tn == 0 and m % tm_ == 0
    nk = rows // tk

    def body(a_ref, b_ref, o_ref, acc_ref):
        k = pl.program_id(2)

        @pl.when(k == 0)
        def _():
            acc_ref[...] = jnp.zeros_like(acc_ref)

        acc_ref[...] += _dot(a_ref[...], b_ref[...], _TN)

        @pl.when(k == nk - 1)
        def _():
            o_ref[...] = acc_ref[...].astype(BF16)

    return pl.pallas_call(
        body, name=name, grid=(m // tm_, n // tn, nk),
        in_specs=[pl.BlockSpec((tk, tm_), lambda i, j, k: (k, i)), pl.BlockSpec((tk, tn), lambda i, j, k: (k, j))],
        out_specs=pl.BlockSpec((tm_, tn), lambda i, j, k: (i, j)),
        out_shape=jax.ShapeDtypeStruct((m, n), BF16),
        scratch_shapes=[pltpu.VMEM((tm_, tn), F32)],
        compiler_params=_params(("parallel", "parallel", "arbitrary")),
    )(a, b)


HALO = 32
LANES = 128


def _shifted(win, offsets):
    for r in range(8):
        js = [j for j, k in enumerate(offsets) if k % 8 == r]
        if js:
            rolled = win if r == 0 else pltpu.roll(win, CHUNK + HALO - r, 0)
            for j in js:
                yield j, rolled[offsets[j] - r:offsets[j] - r + CHUNK]


def _glu_into(uc_ref, vs_ref, n_chunk):
    vs_ref[0:CHUNK, :] = jnp.zeros((CHUNK, C_CONV), F32)

    def glu(i, carry):
        base = pl.multiple_of(i * CHUNK, CHUNK)
        val = uc_ref[pl.ds(base, CHUNK), 0:C_CONV]
        gate = uc_ref[pl.ds(base, CHUNK), C_CONV:2 * C_CONV]
        vs_ref[pl.ds(base + CHUNK, CHUNK), :] = val * _sigmoid(gate)
        return carry

    lax.fori_loop(0, n_chunk, glu, 0)


def _fwd_conv(uc, conv_w, conv_b, ln_g, ln_b, token, n_ex):
    rows = uc.shape[0]
    lp = rows // n_ex
    n_chunk = lp // CHUNK

    def body(uc_ref, w_ref, b_ref, lg_ref, lb_ref, token_ref, ypre_ref, yc_ref, vs_ref):
        _glu_into(uc_ref, vs_ref, n_chunk)

        def conv(i, carry):
            base = pl.multiple_of(i * CHUNK, CHUNK)
            for lb in range(C_CONV // LANES):
                ls = slice(lb * LANES, (lb + 1) * LANES)
                win = vs_ref[pl.ds(base + CHUNK - HALO, CHUNK + HALO), ls]
                acc = jnp.broadcast_to(b_ref[:, ls], (CHUNK, LANES))
                for j, rows_j in _shifted(win, [HALO - (CONV_W - 1) + j for j in range(CONV_W)]):
                    acc = acc + w_ref[j:j + 1, ls] * rows_j
                ypre_ref[pl.ds(base, CHUNK), ls] = acc
            y = ypre_ref[pl.ds(base, CHUNK), :]
            mu = jnp.mean(y, axis=-1, keepdims=True)
            yc_ = y - mu
            rstd = lax.rsqrt(jnp.mean(yc_ * yc_, axis=-1, keepdims=True) + LN_EPS)
            s = yc_ * rstd * lg_ref[...] + lb_ref[...]
            yc_ref[pl.ds(base, CHUNK), :] = (s * _sigmoid(s)).astype(BF16)
            return carry

        lax.fori_loop(0, n_chunk, conv, 0, unroll=3)

    ex = lambda w: pl.BlockSpec((lp, w), lambda b: (b, 0))
    return pl.pallas_call(
        body, name="fwd_conv", grid=(n_ex,),
        in_specs=[ex(2 * C_CONV), _fixed((32, C_CONV)), _fixed((1, C_CONV)), _fixed((1, C_CONV)), _fixed((1, C_CONV)),
                  _fixed((8, 128))],
        out_specs=[ex(C_CONV), ex(C_CONV)],
        out_shape=[jax.ShapeDtypeStruct((rows, C_CONV), F32), jax.ShapeDtypeStruct((rows, C_CONV), BF16)],
        scratch_shapes=[pltpu.VMEM((lp + CHUNK, C_CONV), F32)],
        compiler_params=_params(("parallel",)),
    )(uc, conv_w, conv_b, ln_g, ln_b, token)


def _bwd_conv(uc, ypre, dyc, conv_w, ln_g, ln_b, token, n_ex):
    rows = uc.shape[0]
    lp = rows // n_ex
    n_chunk = lp // CHUNK

    def body(uc_ref, ypre_ref, dyc_ref, w_ref, lg_ref, lb_ref, token_ref, duc_ref, dw_ref, dvec_ref, vs_ref, dys_ref,
             dwacc_ref):
        _glu_into(uc_ref, vs_ref, n_chunk)
        dys_ref[pl.ds(lp, CHUNK), :] = jnp.zeros((CHUNK, C_CONV), F32)
        dwacc_ref[...] = jnp.zeros_like(dwacc_ref)

        def ln_bwd(i, carry):
            dcb, dlg, dlb = carry
            base = pl.multiple_of(i * CHUNK, CHUNK)
            y = ypre_ref[pl.ds(base, CHUNK), :]
            mu = jnp.mean(y, axis=-1, keepdims=True)
            yc_ = y - mu
            rstd = lax.rsqrt(jnp.mean(yc_ * yc_, axis=-1, keepdims=True) + LN_EPS)
            xh = yc_ * rstd
            s = xh * lg_ref[...] + lb_ref[...]
            sg = _sigmoid(s)
            ds = dyc_ref[pl.ds(base, CHUNK), :] * (sg * (1.0 + s * (1.0 - sg)))
            dxh = ds * lg_ref[...]
            dy = rstd * (dxh - jnp.mean(dxh, axis=-1, keepdims=True) - xh * jnp.mean(dxh * xh, axis=-1, keepdims=True))
            dys_ref[pl.ds(base, CHUNK), :] = dy
            return (dcb + jnp.sum(dy, axis=0, keepdims=True), dlg + jnp.sum(ds * xh, axis=0, keepdims=True),
                    dlb + jnp.sum(ds, axis=0, keepdims=True))

        zero = jnp.zeros((1, C_CONV), F32)
        dcb, dlg, dlb = lax.fori_loop(0, n_chunk, ln_bwd, (zero, zero, zero))

        @pl.when(pl.program_id(0) == 0)
        def _():
            dvec_ref[...] = jnp.zeros_like(dvec_ref)
            dw_ref[...] = jnp.zeros_like(dw_ref)

        dvec_ref[0:1, :] += dcb
        dvec_ref[1:2, :] += dlg
        dvec_ref[2:3, :] += dlb

        def taps(i, carry):
            base = pl.multiple_of(i * CHUNK, CHUNK)
            for lb in range(C_CONV // LANES):
                ls = slice(lb * LANES, (lb + 1) * LANES)
                dwin = dys_ref[pl.ds(base, CHUNK + HALO), ls]
                vwin = vs_ref[pl.ds(base + CHUNK - HALO, CHUNK + HALO), ls]
                dy = dwin[0:CHUNK]
                acc = jnp.zeros((CHUNK, LANES), F32)
                for j, rows_j in _shifted(dwin, [CONV_W - 1 - j for j in range(CONV_W)]):
                    acc = acc + w_ref[j:j + 1, ls] * rows_j
                for j, rows_j in _shifted(vwin, [HALO - (CONV_W - 1) + j for j in range(CONV_W)]):
                    dwacc_ref[8 * j:8 * j + 8, ls] += jnp.sum((dy * rows_j).reshape(CHUNK // 8, 8, LANES), axis=0)
                val = uc_ref[pl.ds(base, CHUNK), ls]
                gate = uc_ref[pl.ds(base, CHUNK), C_CONV + lb * LANES:C_CONV + (lb + 1) * LANES]
                sg = _sigmoid(gate)
                duc_ref[pl.ds(base, CHUNK), ls] = (acc * sg).astype(BF16)
                duc_ref[pl.ds(base, CHUNK), C_CONV + lb * LANES:C_CONV + (lb + 1) * LANES] = (
                    acc * val * sg * (1.0 - sg)).astype(BF16)
            return carry

        lax.fori_loop(0, n_chunk, taps, 0, unroll=3)
        for j in range(CONV_W):
            dw_ref[j:j + 1, :] += jnp.sum(dwacc_ref[8 * j:8 * j + 8, :], axis=0, keepdims=True)

    ex = lambda w: pl.BlockSpec((lp, w), lambda b: (b, 0))
    return pl.pallas_call(
        body, name="bwd_conv", grid=(n_ex,),
        in_specs=[ex(2 * C_CONV), ex(C_CONV), ex(C_CONV), _fixed((32, C_CONV)), _fixed((1, C_CONV)), _fixed((1, C_CONV)),
                  _fixed((8, 128))],
        out_specs=[ex(2 * C_CONV), _fixed((32, C_CONV)), _fixed((8, C_CONV))],
        out_shape=[jax.ShapeDtypeStruct((rows, 2 * C_CONV), BF16), jax.ShapeDtypeStruct((32, C_CONV), F32),
                   jax.ShapeDtypeStruct((8, C_CONV), F32)],
        scratch_shapes=[pltpu.VMEM((lp + CHUNK, C_CONV), F32), pltpu.VMEM((lp + CHUNK, C_CONV), F32),
                        pltpu.VMEM((8 * 32, C_CONV), F32)],
        compiler_params=_params(("arbitrary",)),
    )(uc, ypre, dyc, conv_w, ln_g, ln_b, token)


def _seg_chunks(n_chunk):
    return max(c for c in (11, 3, 1) if n_chunk % c == 0)


def _block_mask(shape, row_block, lane_block):
    return (lax.broadcasted_iota(jnp.int32, shape, 0) // row_block) == (lax.broadcasted_iota(jnp.int32, shape, 1) // lane_block)


def _per_head_rows(x, mask):
    return jnp.where(mask, jnp.concatenate([x] * GLA_H, axis=0), 0)


def _fold_heads(full, lane_block):
    lane = lax.broadcasted_iota(jnp.int32, (1, full.shape[1]), 1) // lane_block
    out = jnp.where(lane == 0, full[0:CHUNK], 0.0)
    for h in range(1, GLA_H):
        out = out + jnp.where(lane == h, full[h * CHUNK:(h + 1) * CHUNK], 0.0)
    return out


def _causal_heads():
    return (lax.broadcasted_iota(jnp.int32, (CHUNK, GLA_H * CHUNK), 1) % CHUNK) <= lax.broadcasted_iota(
        jnp.int32, (CHUNK, GLA_H * CHUNK), 0)


def _cumsum_rows(x):
    row = lax.broadcasted_iota(jnp.int32, x.shape, 0)
    s = 1
    while s < CHUNK:
        x = x + jnp.where(row >= s, pltpu.roll(x, s, 0), 0.0)
        s *= 2
    return x


def _rev_cumsum_rows(x):
    row = lax.broadcasted_iota(jnp.int32, x.shape, 0)
    s = 1
    while s < CHUNK:
        x = x + jnp.where(row < CHUNK - s, pltpu.roll(x, CHUNK - s, 0), 0.0)
        s *= 2
    return x


def _gate_terms(lr_ref, w2_ref, gb_ref, rs, first_pos):
    z = _dot(lr_ref[rs, :].astype(BF16), w2_ref[...]) + gb_ref[...]
    la = (jnp.minimum(z, 0.0) - jnp.log(1.0 + jnp.exp(-jnp.abs(z)))) * (1.0 / TAU)
    pos = first_pos + lax.broadcasted_iota(jnp.int32, (CHUNK, 1), 0)
    live = pos >= ZROWS
    la = jnp.where(live, la, 0.0)
    return z, live, _cumsum_rows(la)


def _fwd_gla(qk, vg, lr, w2p, gb, ng, token, n_ex):
    rows = qk.shape[0]
    lp = rows // n_ex
    n_chunk = lp // CHUNK
    sc = _seg_chunks(n_chunk)
    n_seg = n_chunk // sc
    seg = sc * CHUNK

    def body(qk_ref, vg_ref, lr_ref, w2_ref, gb_ref, ng_ref, token_ref, yg_ref, o_ref, st_ref, state_ref):
        sidx = pl.program_id(1)

        @pl.when(sidx == 0)
        def _():
            state_ref[...] = jnp.zeros_like(state_ref)

        causal = _causal_heads()
        k_mask = _block_mask((GLA_H * CHUNK, GLA_K), CHUNK, GLA_DK)
        v_mask = _block_mask((GLA_H * CHUNK, GLA_V), CHUNK, GLA_DV)
        s_mask = _block_mask((GLA_V, GLA_K), GLA_DV, GLA_DK)

        def chunk(ci, carry):
            base = pl.multiple_of(ci * CHUNK, CHUNK)
            rs = pl.ds(base, CHUNK)
            _, _, bcum = _gate_terms(lr_ref, w2_ref, gb_ref, rs, (sidx * sc + ci) * CHUNK)
            bl = bcum[CHUNK - 1:CHUNK, :]
            q = qk_ref[rs, 0:GLA_K]
            k = qk_ref[rs, GLA_K:2 * GLA_K]
            qt = (q * (GLA_DK ** -0.5) * jnp.exp(bcum)).astype(BF16)
            kt = (k * jnp.exp(-bcum)).astype(BF16)
            kh = (k * jnp.exp(bl - bcum)).astype(BF16)
            vb = vg_ref[rs, 0:GLA_V].astype(BF16)
            state = state_ref[...]
            st_ref[ci] = state
            a = jnp.where(causal, _dot(qt, _per_head_rows(kt, k_mask), _NT), 0.0)
            o = _dot(a.astype(BF16), _per_head_rows(vb, v_mask)) + _dot(qt, state.astype(BF16), _NT)
            o_ref[rs, :] = o
            for h in range(GLA_H):
                hs = slice(h * GLA_DV, (h + 1) * GLA_DV)
                oh = o[:, hs]
                ro = lax.rsqrt(jnp.mean(oh * oh, axis=-1, keepdims=True) + RMS_EPS)
                g = vg_ref[rs, GLA_V + h * GLA_DV:GLA_V + (h + 1) * GLA_DV]
                yg_ref[rs, hs] = (oh * ro * ng_ref[...] * g * _sigmoid(g)).astype(BF16)
            state_ref[...] = state * jnp.exp(bl) + jnp.where(s_mask, _dot(vb, kh, _TN), 0.0)
            return carry

        lax.fori_loop(0, sc, chunk, 0, unroll=True)

    sg = lambda w: pl.BlockSpec((seg, w), lambda b, s: (b * n_seg + s, 0))
    return pl.pallas_call(
        body, name="fwd_gla", grid=(n_ex, n_seg),
        in_specs=[sg(2 * GLA_K), sg(2 * GLA_V), sg(RANK_P), _fixed((RANK_P, GLA_K)), _fixed((1, GLA_K)), _fixed((1, GLA_DV)),
                  _fixed((8, 128))],
        out_specs=[sg(GLA_V), sg(GLA_V), pl.BlockSpec((sc, GLA_V, GLA_K), lambda b, s: (b * n_seg + s, 0, 0))],
        out_shape=[jax.ShapeDtypeStruct((rows, GLA_V), BF16), jax.ShapeDtypeStruct((rows, GLA_V), F32),
                   jax.ShapeDtypeStruct((n_ex * n_chunk, GLA_V, GLA_K), F32)],
        scratch_shapes=[pltpu.VMEM((GLA_V, GLA_K), F32)],
        compiler_params=_params(("parallel", "arbitrary")),
    )(qk, vg, lr, w2p, gb, ng, token)


def _bwd_gla(qk, vg, lr, o, st, dyg, w2p, gb, ng, n_ex):
    rows = qk.shape[0]
    lp = rows // n_ex
    n_chunk = lp // CHUNK
    sc = _seg_chunks(n_chunk)
    n_seg = n_chunk // sc
    seg = sc * CHUNK

    def body(qk_ref, vg_ref, lr_ref, o_ref, st_ref, dyg_ref, w2_ref, gb_ref, ng_ref,
             dqk_ref, dvg_ref, dlr_ref, dw2_ref, dvec_ref, gt_ref, dz_ref):
        step = pl.program_id(1)
        sidx = n_seg - 1 - step

        @pl.when(step == 0)
        def _():
            gt_ref[...] = jnp.zeros_like(gt_ref)

        @pl.when((step == 0) & (pl.program_id(0) == 0))
        def _():
            dw2_ref[...] = jnp.zeros_like(dw2_ref)
            dvec_ref[...] = jnp.zeros_like(dvec_ref)

        causal = _causal_heads()
        k_mask = _block_mask((GLA_H * CHUNK, GLA_K), CHUNK, GLA_DK)
        v_mask = _block_mask((GLA_H * CHUNK, GLA_V), CHUNK, GLA_DV)
        s_mask = _block_mask((GLA_V, GLA_K), GLA_DV, GLA_DK)
        last_row = lax.broadcasted_iota(jnp.int32, (CHUNK, 1), 0) == CHUNK - 1
        ng = ng_ref[...]

        def chunk(ii, dng):
            ci = sc - 1 - ii
            base = pl.multiple_of(ci * CHUNK, CHUNK)
            rs = pl.ds(base, CHUNK)
            z, live, bcum = _gate_terms(lr_ref, w2_ref, gb_ref, rs, (sidx * sc + ci) * CHUNK)
            bl = bcum[CHUNK - 1:CHUNK, :]
            ebl = jnp.exp(bl)
            q = qk_ref[rs, 0:GLA_K]
            k = qk_ref[rs, GLA_K:2 * GLA_K]
            eb = jnp.exp(bcum)
            enb = jnp.exp(-bcum)
            ehb = jnp.exp(bl - bcum)
            qt = q * (GLA_DK ** -0.5) * eb
            kt = k * enb
            kh = k * ehb
            qtb = qt.astype(BF16)
            vb = vg_ref[rs, 0:GLA_V].astype(BF16)
            k_rows = _per_head_rows(kt.astype(BF16), k_mask)
            v_rows = _per_head_rows(vb, v_mask)
            gt = gt_ref[...]
            gtb = gt.astype(BF16)
            s_in = st_ref[ci]
            dos = []
            for h in range(GLA_H):
                hs = slice(h * GLA_DV, (h + 1) * GLA_DV)
                gs = slice(GLA_V + h * GLA_DV, GLA_V + (h + 1) * GLA_DV)
                oh = o_ref[rs, hs]
                ro = lax.rsqrt(jnp.mean(oh * oh, axis=-1, keepdims=True) + RMS_EPS)
                on = oh * ro
                g = vg_ref[rs, gs]
                sg = _sigmoid(g)
                dout = dyg_ref[rs, hs]
                dvg_ref[rs, gs] = (dout * on * ng * (sg * (1.0 + g * (1.0 - sg)))).astype(BF16)
                dw = dout * g * sg
                dng = dng + jnp.sum(dw * on, axis=0, keepdims=True)
                don = dw * ng
                dos.append((ro * (don - on * jnp.mean(don * on, axis=-1, keepdims=True))).astype(BF16))
            dob = jnp.concatenate(dos, axis=1)
            a = jnp.where(causal, _dot(qtb, k_rows, _NT), 0.0).astype(BF16)
            da = jnp.where(causal, _dot(dob, v_rows, _NT), 0.0).astype(BF16)
            dv = _fold_heads(_dot(a, dob, _TN), GLA_DV) + _dot(kh.astype(BF16), gtb, _NT)
            dvg_ref[rs, 0:GLA_V] = dv.astype(BF16)
            dkh = _dot(vb, gtb)
            dqt = _dot(da, k_rows) + _dot(dob, s_in.astype(BF16))
            dkt = _fold_heads(_dot(da, qtb, _TN), GLA_DK)
            dbl = jnp.sum(gt * s_in, axis=0, keepdims=True) * ebl + jnp.sum(dkh * kh, axis=0, keepdims=True)
            dqk_ref[rs, 0:GLA_K] = (dqt * (GLA_DK ** -0.5) * eb).astype(BF16)
            dqk_ref[rs, GLA_K:2 * GLA_K] = (dkt * enb + dkh * ehb).astype(BF16)
            db = dqt * qt - dkt * kt - dkh * kh
            db = jnp.where(last_row, db + dbl, db)
            dla = jnp.where(live, _rev_cumsum_rows(db), 0.0)
            dz_ref[rs, :] = dla * (1.0 / TAU) * (1.0 - _sigmoid(z))
            gt_ref[...] = jnp.where(s_mask, _dot(dob, qtb, _TN), 0.0) + gt * ebl
            return dng

        dng = lax.fori_loop(0, sc, chunk, jnp.zeros((1, GLA_DV), F32), unroll=True)
        dz = dz_ref[...]
        dzb = dz.astype(BF16)
        dlr_ref[...] = _dot(dzb, w2_ref[...], _NT).astype(BF16)
        dw2_ref[...] += _dot(lr_ref[...].astype(BF16), dzb, _TN)
        dvec_ref[0:1, :] += jnp.sum(dz, axis=0, keepdims=True)
        dvec_ref[1:2, 0:GLA_DV] += dng

    sg_ = lambda w: pl.BlockSpec((seg, w), lambda b, s: (b * n_seg + n_seg - 1 - s, 0))
    return pl.pallas_call(
        body, name="bwd_gla", grid=(n_ex, n_seg),
        in_specs=[sg_(2 * GLA_K), sg_(2 * GLA_V), sg_(RANK_P), sg_(GLA_V),
                  pl.BlockSpec((sc, GLA_V, GLA_K), lambda b, s: (b * n_seg + n_seg - 1 - s, 0, 0)), sg_(GLA_V),
                  _fixed((RANK_P, GLA_K)), _fixed((1, GLA_K)), _fixed((1, GLA_DV))],
        out_specs=[sg_(2 * GLA_K), sg_(2 * GLA_V), sg_(RANK_P), _fixed((RANK_P, GLA_K)), _fixed((8, GLA_K))],
        out_shape=[jax.ShapeDtypeStruct((rows, 2 * GLA_K), BF16), jax.ShapeDtypeStruct((rows, 2 * GLA_V), BF16),
                   jax.ShapeDtypeStruct((rows, RANK_P), BF16), jax.ShapeDtypeStruct((RANK_P, GLA_K), F32),
                   jax.ShapeDtypeStruct((8, GLA_K), F32)],
        scratch_shapes=[pltpu.VMEM((GLA_V, GLA_K), F32), pltpu.VMEM((seg, GLA_K), F32)],
        compiler_params=_params(("arbitrary", "arbitrary")),
    )(qk, vg, lr, o, st, dyg, w2p, gb, ng)


def _pad_rows(x, tgt):
    return jnp.pad(x, ((0, 0), (LEAD, 0), (0, 0))), jnp.pad(tgt, ((0, 0), (LEAD, 0), (0, 0)))


def _local_step(h0, tgt_p, p, pass_on, late_weights, send_early):
    n_ex, lp, _ = h0.shape
    rows = n_ex * lp
    meta = jnp.broadcast_to(p["meta"][None], (n_ex, N_META, D))
    h0 = lax.dynamic_update_slice(h0, meta, (0, ZROWS, 0)).reshape(rows, D)
    tgt_p = tgt_p.reshape(rows, D)

    uc, qk, vg, lr, n1 = _fwd_inproj(h0, p["g1"], p["w_in"])
    ypre, yc = _fwd_conv(uc, p["conv_w"], p["conv_b"], p["ln_g"], p["ln_b"], p["token"], n_ex)
    token = pass_on(yc)
    yg, o, st = _fwd_gla(qk, vg, lr, p["w2"], p["gb"], p["ng"], token, n_ex)
    w_out, wg, wu, wd = late_weights(yg)
    n2, f, da, db, dh2, dh1, dh1b, dyc, dyg, part = _mid_rows(
        yc, yg, h0, tgt_p, w_out, wg, wu, wd, p["g2"], p["g3"], token, lp)
    g = {}
    token = send_early("ffn", [_matmul_tn(a_, b_, name).reshape(N_DEV, FF_S, D) for a_, b_, name in (
        (da, n2, "dw_gate"), (db, n2, "dw_up"), (f, dh2, "dw_down"))])
    token = send_early("out", [_dw_out(yc, yg, dh1b, token).reshape(N_DEV, W_OUT_S, D)])
    duc, g["conv_w"], g["conv_vec"] = _bwd_conv(uc, ypre, dyc, p["conv_w"], p["ln_g"], p["ln_b"], token, n_ex)
    dqk, dvg, dlr, g["w2"], g["gla_vec"] = _bwd_gla(qk, vg, lr, o, st, dyg, p["w2"], p["gb"], p["ng"], n_ex)
    token = send_early("in", [_dw_blocked(n1, [duc, dqk, dvg, dlr], W_IN_S, "dw_in")])
    grad_x, g["in_vec"], g["meta"] = _bwd_inproj(duc, dqk, dvg, dlr, dh1, h0, p["w_in"], p["g1"], token, lp)
    g["ffn_vec"] = part
    return grad_x, g


W_IN_S = D_IN // N_DEV
W_OUT_S = D // N_DEV
FF_S = D_FF // N_DEV
CONV_S = C_CONV // N_DEV
GATE_S = GLA_K // N_DEV
SMALL_PACK = 64
CONV_ROW = 16
GATE_ROW = 48
VEC_ROWS = 16
_VEC_ROWS = (("norm_mix_g", D), ("conv_b", C_CONV), ("conv_ln_g", C_CONV), ("conv_ln_b", C_CONV), ("gla_gate_b", GLA_K),
             ("gla_norm_g", GLA_DV), ("norm_ffn_g", D), ("norm_final_g", D))
LOSS_ROW = len(_VEC_ROWS)


def _position():
    return lax.axis_index("x"), lax.axis_index("y"), lax.axis_index("c")


def _any():
    return pl.BlockSpec(memory_space=pl.ANY)


def _stage(mats, meta, conv_w, w2):
    n_t = len(mats) + 1

    def body(*refs):
        ins = refs[0:n_t - 1]
        meta_ref, cw_ref, w2_ref = refs[n_t - 1:n_t + 2]
        lands = refs[n_t + 2:2 * n_t + 2]
        shards = refs[2 * n_t + 2:3 * n_t + 2]
        sems = refs[3 * n_t + 2]
        for s_ref, w_ref in zip(shards, ins):
            s_ref[...] = w_ref[...].astype(BF16)
        sp = shards[n_t - 1]
        sp[...] = jnp.zeros_like(sp)
        sp[0:N_META, :] = meta_ref[...]
        sp[CONV_ROW:CONV_ROW + CONV_W, 0:CONV_S] = cw_ref[...]
        sp[GATE_ROW:GATE_ROW + RANK, 0:GATE_S] = w2_ref[...]
        x, y, c = _position()
        mine = [pltpu.make_async_copy(shards[t], lands[t].at[4 * x + 2 * y + c], sems.at[t]) for t in range(n_t)]
        for cp in mine:
            cp.start()
        for cp in mine:
            cp.wait()

    shard_shapes = [jax.ShapeDtypeStruct(m.shape, BF16) for m in mats] + [jax.ShapeDtypeStruct((SMALL_PACK, 128), F32)]
    res = pl.pallas_call(
        body, name="stage",
        out_shape=[jax.ShapeDtypeStruct((N_DEV,) + s.shape, s.dtype) for s in shard_shapes] + shard_shapes,
        in_specs=[_whole_vmem()] * (n_t + 2), out_specs=[_any()] * n_t + [_whole_vmem()] * n_t,
        scratch_shapes=[pltpu.SemaphoreType.DMA((n_t,))],
        compiler_params=pltpu.CompilerParams(vmem_limit_bytes=VMEM_LIMIT),
    )(*mats, meta, conv_w, w2)
    return res[0:n_t], res[n_t:]


_HBM = pl.BlockSpec(memory_space=pltpu.HBM)
_SEM = pl.BlockSpec(memory_space=pltpu.SEMAPHORE)
_EFFECT = pltpu.SideEffectType.DATAFLOW_SIDE_EFFECTING


_N_ROUTES = {"scatter": 7, "first": 4, "forward": 3}


def _routes(mode):
    x, y, c = _position()
    me = 4 * x + 2 * y + c
    if mode == "scatter":
        out = []
        for k in range(1, N_DEV):
            px = 1 - x if k & 4 else x
            py = 1 - y if k & 2 else y
            pc = 1 - c if k & 1 else c
            out.append(((px, py, pc), 4 * px + 2 * py + pc, me))
        return out
    if mode == "first":
        return [(pos, None, me) for pos in ((x, y, 1 - c), (1 - x, y, c), (x, 1 - y, c), (1 - x, 1 - y, c))]
    assert mode == "forward"
    return [((x, y, 1 - c), 4 * px + 2 * py + c, 4 * px + 2 * py + c) for px, py in ((1 - x, y), (x, 1 - y), (1 - x, 1 - y))]


def _route_copies(mode, n, src_refs, land_refs, send_sems, recv_sems):
    nr = _N_ROUTES[mode]
    for i, (pos, src_blk, dst_blk) in enumerate(_routes(mode)):
        for t in range(n):
            src = land_refs[t] if mode == "forward" else src_refs[t]
            yield pltpu.make_async_remote_copy(
                src_ref=src if src_blk is None else src.at[src_blk], dst_ref=land_refs[t].at[dst_blk],
                send_sem=send_sems.at[nr * t + i], recv_sem=recv_sems.at[nr * t + i], device_id=pos, device_id_type=MESH)


def _in_hbm(a):
    return pltpu.with_memory_space_constraint(a, pltpu.HBM)


def _send_start(name, srcs, lands, mode, after):
    n, ns = len(lands), len(srcs)
    nsem = _N_ROUTES[mode] * n

    def body(*refs):
        src_refs, land_refs = refs[0:ns], refs[ns:ns + n]
        send_sems, recv_sems = refs[ns + n + 1:ns + n + 3]
        token = refs[2 * (ns + n) + 3]
        for cp in _route_copies(mode, n, src_refs, land_refs, send_sems, recv_sems):
            cp.start()
        token[...] = jnp.zeros_like(token)

    bufs = list(srcs) + list(lands)
    res = pl.pallas_call(
        body, name=name,
        out_shape=(pltpu.SemaphoreType.DMA((nsem,)), pltpu.SemaphoreType.DMA((nsem,)),
                   *[pltpu.HBM(b.shape, b.dtype) for b in bufs], jax.ShapeDtypeStruct((8, 128), F32)),
        in_specs=[_HBM] * len(bufs) + [_any()], out_specs=(_SEM, _SEM, *[_HBM] * len(bufs), _whole_vmem()),
        input_output_aliases={i: 2 + i for i in range(len(bufs))},
        compiler_params=pltpu.CompilerParams(has_side_effects=_EFFECT),
    )(*[_in_hbm(b) for b in bufs], after)
    return res[0], res[1], res[2:2 + ns], res[2 + ns:2 + ns + n], res[2 + ns + n]


def _send_wait(name, send_sems, recv_sems, srcs, lands, mode, after):
    n, ns = len(lands), len(srcs)
    after = after if isinstance(after, tuple) else (after,)

    def body(*refs):
        src_refs, land_refs = refs[0:ns], refs[ns:ns + n]
        send_sems, recv_sems = refs[ns + n:ns + n + 2]
        for cp in _route_copies(mode, n, src_refs, land_refs, send_sems, recv_sems):
            cp.wait_send()
            cp.wait_recv()

    bufs = list(srcs) + list(lands)
    res = pl.pallas_call(
        body, name=name,
        out_shape=tuple(pltpu.HBM(b.shape, b.dtype) for b in bufs),
        in_specs=[_HBM] * len(bufs) + [_SEM, _SEM] + [_any()] * len(after), out_specs=tuple([_HBM] * len(bufs)),
        input_output_aliases={i: i for i in range(len(bufs))},
        compiler_params=pltpu.CompilerParams(has_side_effects=_EFFECT),
    )(*bufs, send_sems, recv_sems, *after)
    return res[0:ns], res[ns:ns + n]


def _unshard_in(a_in, a_small, token):
    def body(a_ref, s_ref, token_ref, w_ref, meta_ref, cw_ref, w2_ref):
        w_ref[:, D_IN:D_INP] = jnp.zeros((D, D_INP - D_IN), BF16)
        w2_ref[...] = jnp.zeros_like(w2_ref)
        for d in range(N_DEV):
            w_ref[:, d * W_IN_S:(d + 1) * W_IN_S] = a_ref[d]
            meta_ref[:, d * 128:(d + 1) * 128] = s_ref[d, 0:N_META, :]
            cw_ref[:, d * CONV_S:(d + 1) * CONV_S] = s_ref[d, CONV_ROW:CONV_ROW + 32, 0:CONV_S]
            w2_ref[0:RANK, d * GATE_S:(d + 1) * GATE_S] = s_ref[d, GATE_ROW:GATE_ROW + RANK, 0:GATE_S].astype(BF16)

    return pl.pallas_call(
        body, name="unshard_in",
        out_shape=[jax.ShapeDtypeStruct((D, D_INP), BF16), jax.ShapeDtypeStruct((N_META, D), F32),
                   jax.ShapeDtypeStruct((32, C_CONV), F32), jax.ShapeDtypeStruct((RANK_P, GLA_K), BF16)],
        compiler_params=pltpu.CompilerParams(vmem_limit_bytes=VMEM_LIMIT),
    )(a_in, a_small, token)


def _pack_small(g):
    def body(meta_ref, cw_ref, w2_ref, in_vec, ffn_vec, conv_vec, gla_vec, sp, vp):
        sp[...] = jnp.zeros_like(sp)
        vp[...] = jnp.zeros_like(vp)
        for d in range(N_DEV):
            sp[d, 0:N_META, :] = meta_ref[:, d * 128:(d + 1) * 128]
            sp[d, CONV_ROW:CONV_ROW + 32, 0:CONV_S] = cw_ref[:, d * CONV_S:(d + 1) * CONV_S]
            sp[d, GATE_ROW:GATE_ROW + RANK, 0:GATE_S] = w2_ref[0:RANK, d * GATE_S:(d + 1) * GATE_S]
            vp[d, 0:1, :] = in_vec[0:1, :]
            vp[d, 1:4, 0:C_CONV] = conv_vec[0:3, :]
            vp[d, 4:5, 0:GLA_K] = gla_vec[0:1, :]
            vp[d, 5:6, 0:GLA_DV] = gla_vec[1:2, 0:GLA_DV]
            vp[d, 6:7, :] = ffn_vec[1:2, :]
            vp[d, 7:8, :] = ffn_vec[0:1, :]
            vp[d, LOSS_ROW:LOSS_ROW + 1, :] = ffn_vec[2:3, :]

    return pl.pallas_call(
        body, name="pack_small",
        out_shape=[jax.ShapeDtypeStruct((N_DEV, SMALL_PACK, 128), F32), jax.ShapeDtypeStruct((N_DEV, VEC_ROWS, D), F32)],
    )(g["meta"], g["conv_w"], g["w2"], g["in_vec"], g["ffn_vec"], g["conv_vec"], g["gla_vec"])


def _adamw(w, g, m, v):
    m = ADAM_B1 * m + (1.0 - ADAM_B1) * g
    v = ADAM_B2 * v + (1.0 - ADAM_B2) * (g * g)
    m_hat = m / (1.0 - ADAM_B1 ** ADAM_STEP)
    v_hat = v / (1.0 - ADAM_B2 ** ADAM_STEP)
    return -ADAM_LR * (m_hat / (jnp.sqrt(v_hat) + ADAM_EPS) + ADAM_WD * w), m, v


def _update_matrix(recv, own, me, w, m, v, name):
    _, r, c = recv.shape
    tr = _row_tile(r, 256)

    def body(me_ref, recv_ref, own_ref, w_ref, m_ref, v_ref, g_ref, d_ref, nm_ref, nv_ref):
        g = jnp.zeros((tr, c), F32)
        for s in range(N_DEV):
            g = g + jnp.where(me_ref[0] == s, own_ref[...], recv_ref[s]).astype(F32)
        g_ref[...] = g
        d_ref[...], nm_ref[...], nv_ref[...] = _adamw(w_ref[...], g, m_ref[...], v_ref[...])

    one = pl.BlockSpec((None, tr, c), lambda i, me_ref: (0, i, 0))
    return pl.pallas_call(
        body, name=name,
        grid_spec=pltpu.PrefetchScalarGridSpec(
            num_scalar_prefetch=1, grid=(r // tr,),
            in_specs=[pl.BlockSpec((N_DEV, tr, c), lambda i, me_ref: (0, i, 0)),
                      pl.BlockSpec((None, tr, c), lambda i, me_ref: (me_ref[0], i, 0)), one, one, one],
            out_specs=[one] * 4),
        out_shape=[jax.ShapeDtypeStruct((1, r, c), F32)] * 4,
        compiler_params=_params(("parallel",)),
    )(me, recv, own, w, m, v)


_SMALL = ("meta_tokens", "conv_w", "gla_w_gate2") + tuple(n for n, _ in _VEC_ROWS)


def _update_small(me, srecv, vrecv, sown, vown, w, m, v):
    n = len(_SMALL)

    def body(*refs):
        me_ref, s_ref, v_ref, so_ref, vo_ref = refs[0:5]
        w_refs, m_refs, v_refs = refs[5:5 + n], refs[5 + n:5 + 2 * n], refs[5 + 2 * n:5 + 3 * n]
        outs = refs[5 + 3 * n:]
        ssum = jnp.zeros((SMALL_PACK, 128), F32)
        vsum = jnp.zeros((VEC_ROWS, D), F32)
        for s in range(N_DEV):
            ssum = ssum + jnp.where(me_ref[0] == s, so_ref[s], s_ref[s])
            vsum = vsum + jnp.where(me_ref[0] == s, vo_ref[s], v_ref[s])
        grads = [ssum[0:N_META, :], ssum[CONV_ROW:CONV_ROW + CONV_W, 0:CONV_S], ssum[GATE_ROW:GATE_ROW + RANK, 0:GATE_S]]
        grads += [vsum[i:i + 1, 0:width] for i, (_, width) in enumerate(_VEC_ROWS)]
        for i, g in enumerate(grads):
            d, nm, nv = _adamw(w_refs[i][...], g, m_refs[i][...], v_refs[i][...])
            outs[i][...] = g
            outs[n + i][...] = d
            outs[2 * n + i][...] = nm
            outs[3 * n + i][...] = nv
        outs[4 * n][...] = vsum[LOSS_ROW:LOSS_ROW + 1, 0:128]

    shapes = [jax.ShapeDtypeStruct(t.shape, F32) for t in w]
    res = pl.pallas_call(
        body, name="update_small", out_shape=shapes * 4 + [jax.ShapeDtypeStruct((1, 128), F32)],
        in_specs=[pl.BlockSpec(memory_space=pltpu.SMEM)] + [_whole_vmem()] * (4 + 3 * n),
    )(me, srecv, vrecv, sown, vown, *w, *m, *v)
    return res[0:n], res[n:2 * n], res[2 * n:3 * n], res[3 * n:4 * n], res[4 * n]


_WEIGHTS = ("meta_tokens", "norm_mix_g", "w_in", "conv_w", "conv_b", "conv_ln_g", "conv_ln_b", "gla_w_gate2", "gla_gate_b",
            "gla_norm_g", "w_out", "norm_ffn_g", "w_ffn_gate", "w_ffn_up", "w_ffn_down", "norm_final_g")
_MATRICES = ("w_in", "w_out", "w_ffn_gate", "w_ffn_up", "w_ffn_down")
_TRANSPOSED = ("w_ffn_gate", "w_ffn_up")


def kernel(x, meta_tokens, norm_mix_g, w_in, conv_w, conv_b, conv_ln_g, conv_ln_b, gla_w_gate2, gla_gate_b, gla_norm_g, w_out, norm_ffn_g, w_ffn_gate, w_ffn_up, w_ffn_down, norm_final_g, loss_target, m_meta_tokens, m_norm_mix_g, m_w_in, m_conv_w, m_conv_b, m_conv_ln_g, m_conv_ln_b, m_gla_w_gate2, m_gla_gate_b, m_gla_norm_g, m_w_out, m_norm_ffn_g, m_w_ffn_gate, m_w_ffn_up, m_w_ffn_down, m_norm_final_g, v_meta_tokens, v_norm_mix_g, v_w_in, v_conv_w, v_conv_b, v_conv_ln_g, v_conv_ln_b, v_gla_w_gate2, v_gla_gate_b, v_gla_norm_g, v_w_out, v_norm_ffn_g, v_w_ffn_gate, v_w_ffn_up, v_w_ffn_down, v_norm_final_g):
    given = dict(locals())
    two_d = lambda a: a.reshape(1, -1) if a.ndim == 1 else a.reshape(a.shape[-2:])
    fams = [{n: given[pre + n] for n in _WEIGHTS} for pre in ("", "m_", "v_")]
    for f in fams:
        for n in _TRANSPOSED:
            f[n] = f[n].transpose(0, 2, 1)
    w = fams[0]

    lands, shards = _stage([two_d(w[n]) for n in _MATRICES], w["meta_tokens"], two_d(w["conv_w"]), two_d(w["gla_w_gate2"]))
    soon, later = (0, 5), (1, 2, 3, 4)
    pick = lambda seq, idx: [seq[i] for i in idx]
    first = _send_start("gather_first_start", pick(shards, soon), pick(lands, soon), "first", norm_mix_g)
    ffn_first = _send_start("gather_ffn_first_start", pick(shards, later), pick(lands, later), "first", first[4])
    h0, tgt_p = _pad_rows(x, loss_target)
    _, arrived = _send_wait("gather_first_wait", *first[0:4], "first", (h0, tgt_p, ffn_first[4]))
    forward = _send_start("gather_forward_start", [], arrived, "forward", ffn_first[4])
    _, (a_in, a_small) = _send_wait("gather_forward_wait", *forward[0:4], "forward", forward[4])
    w_in, meta, conv_taps, w2 = _unshard_in(a_in, a_small, forward[4])
    p = dict(meta=meta, conv_w=conv_taps, w2=w2, w_in=w_in, g1=norm_mix_g, conv_b=conv_b, ln_g=conv_ln_g, ln_b=conv_ln_b,
             gb=gla_gate_b, ng=gla_norm_g, g2=norm_ffn_g, g3=two_d(norm_final_g), token=forward[4])
    passed = {}

    def pass_on(after):
        _, arrived_ffn = _send_wait("gather_ffn_first_wait", *ffn_first[0:4], "first", after)
        passed["sent"] = _send_start("gather_ffn_forward_start", [], arrived_ffn, "forward", after)
        return passed["sent"][4]

    def late_weights(after):
        _, (a_out, a_g, a_u, a_d) = _send_wait("gather_ffn_forward_wait", *passed["sent"][0:4], "forward", after)
        return a_out.reshape(D, D), a_g.reshape(D_FF, D), a_u.reshape(D_FF, D), a_d.reshape(D_FF, D)

    sent = {}

    def send_early(tag, mats):
        landing = [_in_hbm(lax.empty(m_.shape, m_.dtype)) for m_ in mats]
        sent[tag] = _send_start("scatter_" + tag + "_start", mats, landing, "scatter", norm_mix_g)
        return sent[tag][4]

    grad_x, g = _local_step(h0, tgt_p, p, pass_on, late_weights, send_early)

    token = send_early("small", list(_pack_small(g)))
    x_, y_, c_ = _position()
    me = (4 * x_ + 2 * y_ + c_).astype(jnp.int32).reshape(1)
    res = {}
    for tag, names in (("ffn", ("w_ffn_gate", "w_ffn_up", "w_ffn_down")), ("out", ("w_out",)), ("in", ("w_in",))):
        own, recv = _send_wait("scatter_" + tag + "_wait", *sent[tag][0:4], "scatter", token)
        for n, o_, r_ in zip(names, own, recv):
            res[n] = _update_matrix(r_, o_, me, *[f[n] for f in fams], "update_" + n)
            token = res[n][1]
    (sown, vown), (srecv, vrecv) = _send_wait("scatter_small_wait", *sent["small"][0:4], "scatter", token)
    small = _update_small(me, srecv, vrecv, sown, vown, *[[two_d(f[n]) for n in _SMALL] for f in fams])
    for i, n in enumerate(_SMALL):
        res[n] = [fam[i].reshape(w[n].shape) for fam in small[0:4]]
    for n in _TRANSPOSED:
        res[n] = [t.transpose(0, 2, 1) for t in res[n]]
    outs = [small[4][0, 0], grad_x]
    for k in range(4):
        outs += [res[n][k] for n in _WEIGHTS]
    return tuple(outs)
```

```python
import functools

import jax
import jax.numpy as jnp
from jax import lax
from jax.experimental import pallas as pl
from jax.experimental.pallas import tpu as pltpu

F32 = jnp.float32
BF16 = jnp.bfloat16

D = 1024
N_META = 16
C_CONV = 512
CONV_W = 31
GLA_H = 4
GLA_DK = 64
GLA_DV = 128
GLA_K = GLA_H * GLA_DK
GLA_V = GLA_H * GLA_DV
RANK = 16
RANK_P = 128
TAU = 16.0
CHUNK = 64
LEAD = CHUNK
ZROWS = LEAD - N_META
D_IN = 2 * C_CONV + 2 * GLA_K + 2 * GLA_V + RANK
D_INP = D_IN - RANK + RANK_P
D_FF = 2816
FF_CHUNK = 1408
FF_SPLIT = (0, 1536, D_FF)
RMS_EPS = 1e-6
LN_EPS = 1e-5
N_DEV = 8

ADAM_LR = 0.001
ADAM_B1 = 0.9
ADAM_B2 = 0.999
ADAM_EPS = 1e-08
ADAM_WD = 0.01
ADAM_STEP = 10

VMEM_LIMIT = 60 * 1024 * 1024
ROW_TILE = 1056
FFN_ROW_TILE = 352
DW_ROW_TILE = 1408
MESH = pl.DeviceIdType.MESH

_NN = (((1,), (0,)), ((), ()))
_NT = (((1,), (1,)), ((), ()))
_TN = (((0,), (0,)), ((), ()))


def _dot(a, b, dims=_NN):
    return lax.dot_general(a, b, dims, preferred_element_type=F32)


def _sigmoid(x):
    return 1.0 / (1.0 + jnp.exp(-x))


def _row_tile(rows, target):
    best = None
    for t in range(16, min(rows, target) + 1, 16):
        if rows % t == 0:
            best = t
    assert best is not None, rows
    return best


def _params(sem=None):
    return pltpu.CompilerParams(dimension_semantics=sem, vmem_limit_bytes=VMEM_LIMIT)


def _whole_vmem():
    return pl.BlockSpec(memory_space=pltpu.VMEM)


def _rows(tm, width):
    return pl.BlockSpec((tm, width), lambda i: (i, 0))


def _fixed(shape):
    return pl.BlockSpec(shape, lambda *_: (0,) * len(shape))


def _fwd_inproj(h0, g1, w_in):
    rows = h0.shape[0]
    tm = _row_tile(rows, ROW_TILE)

    def body(h_ref, g_ref, w_ref, uc_ref, qk_ref, vg_ref, lr_ref, n1_ref):
        h = h_ref[...]
        r = lax.rsqrt(jnp.mean(h * h, axis=-1, keepdims=True) + RMS_EPS)
        n = (h * r * g_ref[...]).astype(BF16)
        n1_ref[...] = n
        uc_ref[...] = _dot(n, w_ref[:, 0:1024])
        qk_ref[...] = _dot(n, w_ref[:, 1024:1536])
        vg_ref[...] = _dot(n, w_ref[:, 1536:2560])
        lr_ref[...] = _dot(n, w_ref[:, 2560:2688])

    return pl.pallas_call(
        body, name="fwd_inproj", grid=(rows // tm,),
        in_specs=[_rows(tm, D), _fixed((1, D)), _whole_vmem()],
        out_specs=[_rows(tm, 1024), _rows(tm, 512), _rows(tm, 1024), _rows(tm, RANK_P), _rows(tm, D)],
        out_shape=[jax.ShapeDtypeStruct((rows, 1024), F32), jax.ShapeDtypeStruct((rows, 512), F32),
                   jax.ShapeDtypeStruct((rows, 1024), F32), jax.ShapeDtypeStruct((rows, RANK_P), F32),
                   jax.ShapeDtypeStruct((rows, D), BF16)],
        compiler_params=_params(("parallel",)),
    )(h0, g1, w_in)


def _mid_rows(yc, yg, h0, tgt, w_out, wg, wu, wd, g2, g3, token, rows_per_example):
    rows = h0.shape[0]
    tm = _row_tile(rows, FFN_ROW_TILE)
    ff_blocks = [slice(lo, hi) for lo, hi in zip(FF_SPLIT[:-1], FF_SPLIT[1:])]

    def body(yc_ref, yg_ref, h0_ref, t_ref, wo_ref, wg_ref, wu_ref, wd_ref, g2_ref, g3_ref, token_ref,
             n2_ref, f_ref, da_ref, db_ref, dh2_ref, dh1_ref, dh1b_ref, dyc_ref, dyg_ref, part_ref):
        i = pl.program_id(0)
        h1 = h0_ref[...] + _dot(yc_ref[...], wo_ref[0:C_CONV, :]) + _dot(yg_ref[...], wo_ref[C_CONV:D, :])
        r2 = lax.rsqrt(jnp.mean(h1 * h1, axis=-1, keepdims=True) + RMS_EPS)
        xh2 = h1 * r2
        n2 = (xh2 * g2_ref[...]).astype(BF16)
        n2_ref[...] = n2
        y2 = jnp.zeros((tm, D), F32)
        for cs in ff_blocks:
            a = _dot(n2, wg_ref[cs, :], _NT)
            b = _dot(n2, wu_ref[cs, :], _NT)
            f = (a * _sigmoid(a) * b).astype(BF16)
            f_ref[:, cs] = f
            da_ref[:, cs] = a.astype(BF16)
            db_ref[:, cs] = b.astype(BF16)
            y2 = y2 + _dot(f, wd_ref[cs, :])
        h2 = h1 + y2
        r3 = lax.rsqrt(jnp.mean(h2 * h2, axis=-1, keepdims=True) + RMS_EPS)
        xh3 = h2 * r3
        g3 = g3_ref[...]
        pos = (i * tm + lax.broadcasted_iota(jnp.int32, (tm, 1), 0)) % rows_per_example
        valid = pos >= LEAD
        err = jnp.where(valid, xh3 * g3 - t_ref[...], 0.0)
        loss = 0.5 / D * jnp.sum(jnp.sum(err * err, axis=-1, keepdims=True), axis=0, keepdims=True)
        dy = err * (1.0 / D)
        dg3 = jnp.sum(dy * xh3, axis=0, keepdims=True)
        dxh = dy * g3
        dh2 = r3 * (dxh - xh3 * jnp.mean(dxh * xh3, axis=-1, keepdims=True))
        dh2b = dh2.astype(BF16)
        dh2_ref[...] = dh2b
        dn2 = jnp.zeros((tm, D), F32)
        for cs in ff_blocks:
            df = _dot(dh2b, wd_ref[cs, :], _NT)
            a = da_ref[:, cs].astype(F32)
            b = db_ref[:, cs].astype(F32)
            sg = _sigmoid(a)
            da = (df * b * sg * (1.0 + a * (1.0 - sg))).astype(BF16)
            db = (df * a * sg).astype(BF16)
            da_ref[:, cs] = da
            db_ref[:, cs] = db
            dn2 = dn2 + _dot(da, wg_ref[cs, :]) + _dot(db, wu_ref[cs, :])
        dg2 = jnp.sum(dn2 * xh2, axis=0, keepdims=True)
        dxh2 = dn2 * g2_ref[...]
        dh1 = dh2 + r2 * (dxh2 - xh2 * jnp.mean(dxh2 * xh2, axis=-1, keepdims=True))
        dh1_ref[...] = dh1
        dh1b = dh1.astype(BF16)
        dh1b_ref[...] = dh1b
        dyc_ref[...] = _dot(dh1b, wo_ref[0:C_CONV, :], _NT)
        dyg_ref[...] = _dot(dh1b, wo_ref[C_CONV:D, :], _NT)

        @pl.when(i == 0)
        def _():
            part_ref[...] = jnp.zeros_like(part_ref)

        part_ref[0:1, :] += dg3
        part_ref[1:2, :] += dg2
        part_ref[2:3, :] += jnp.broadcast_to(loss, (1, D))

    return pl.pallas_call(
        body, name="mid_rows", grid=(rows // tm,),
        in_specs=[_rows(tm, C_CONV), _rows(tm, GLA_V), _rows(tm, D), _rows(tm, D), _whole_vmem(), _whole_vmem(),
                  _whole_vmem(), _whole_vmem(), _fixed((1, D)), _fixed((1, D)), _fixed((8, 128))],
        out_specs=[_rows(tm, D), _rows(tm, D_FF), _rows(tm, D_FF), _rows(tm, D_FF), _rows(tm, D), _rows(tm, D),
                   _rows(tm, D), _rows(tm, C_CONV), _rows(tm, GLA_V), _fixed((8, D))],
        out_shape=[jax.ShapeDtypeStruct((rows, D), BF16)] + [jax.ShapeDtypeStruct((rows, D_FF), BF16)] * 3
        + [jax.ShapeDtypeStruct((rows, D), BF16), jax.ShapeDtypeStruct((rows, D), F32),
           jax.ShapeDtypeStruct((rows, D), BF16), jax.ShapeDtypeStruct((rows, C_CONV), F32),
           jax.ShapeDtypeStruct((rows, GLA_V), F32), jax.ShapeDtypeStruct((8, D), F32)],
        compiler_params=_params(("arbitrary",)),
    )(yc, yg, h0, tgt, w_out, wg, wu, wd, g2, g3, token)


def _bwd_inproj(duc, dqk, dvg, dlr, dh1, h0, w_in, g1, token, rows_per_example):
    rows = h0.shape[0]
    n_ex = rows // rows_per_example
    tm = _row_tile(rows_per_example, ROW_TILE)
    tiles_per_example = rows_per_example // tm
    n_steps = rows // tm

    def body(duc_ref, dqk_ref, dvg_ref, dlr_ref, dh1_ref, h_ref, w_ref, g_ref, token_ref, gx_ref, part_ref, dmeta_ref,
             buf_ref, sems):
        dn = (_dot(duc_ref[...], w_ref[:, 0:1024], _NT) + _dot(dqk_ref[...], w_ref[:, 1024:1536], _NT)
              + _dot(dvg_ref[...], w_ref[:, 1536:2560], _NT) + _dot(dlr_ref[...], w_ref[:, 2560:2688], _NT))
        h = h_ref[...]
        r = lax.rsqrt(jnp.mean(h * h, axis=-1, keepdims=True) + RMS_EPS)
        xh = h * r
        dg = jnp.sum(dn * xh, axis=0, keepdims=True)
        dxh = dn * g_ref[...]
        dh0 = dh1_ref[...] + r * (dxh - xh * jnp.mean(dxh * xh, axis=-1, keepdims=True))
        i = pl.program_id(0)

        def copies(step):
            slot, b, j = step % 2, step // tiles_per_example, step % tiles_per_example
            out = [(j == 0, pltpu.make_async_copy(buf_ref.at[slot, pl.ds(LEAD, tm - LEAD)],
                                                   gx_ref.at[b, pl.ds(0, tm - LEAD)], sems.at[slot]))]
            if tiles_per_example > 1:
                out.append((j != 0, pltpu.make_async_copy(
                    buf_ref.at[slot], gx_ref.at[b, pl.ds(pl.multiple_of(jnp.maximum(j * tm - LEAD, 0), 8), tm)],
                    sems.at[slot])))
            return out

        def each(step, act):
            for cond, cp in copies(step):
                pl.when(cond)(functools.partial(act, cp))

        @pl.when(i >= 2)
        def _():
            each(i - 2, lambda cp: cp.wait())

        buf_ref[i % 2] = dh0
        each(i, lambda cp: cp.start())

        @pl.when(i == n_steps - 1)
        def _():
            each(i, lambda cp: cp.wait())
            if n_steps > 1:
                each(i - 1, lambda cp: cp.wait())

        @pl.when(i == 0)
        def _():
            part_ref[...] = jnp.zeros_like(part_ref)
            dmeta_ref[...] = jnp.zeros_like(dmeta_ref)

        part_ref[0:1, :] += dg

        @pl.when(i % tiles_per_example == 0)
        def _():
            dmeta_ref[...] += dh0[ZROWS:LEAD, :]

    return pl.pallas_call(
        body, name="bwd_inproj", grid=(n_steps,),
        in_specs=[_rows(tm, 1024), _rows(tm, 512), _rows(tm, 1024), _rows(tm, RANK_P), _rows(tm, D), _rows(tm, D),
                  _whole_vmem(), _fixed((1, D)), _fixed((8, 128))],
        out_specs=[_any(), _fixed((8, D)), _fixed((N_META, D))],
        out_shape=[jax.ShapeDtypeStruct((n_ex, rows_per_example - LEAD, D), F32), jax.ShapeDtypeStruct((8, D), F32),
                   jax.ShapeDtypeStruct((N_META, D), F32)],
        scratch_shapes=[pltpu.VMEM((2, tm, D), F32), pltpu.SemaphoreType.DMA((2,))],
        compiler_params=_params(("arbitrary",)),
    )(duc, dqk, dvg, dlr, dh1, h0, w_in, g1, token)


def _dw_blocked(a, bs, width, name):
    rows, m = a.shape
    ws = [b.shape[1] for b in bs]
    assert sum(ws) >= N_DEV * width
    tk = _row_tile(rows, DW_ROW_TILE)
    nk = rows // tk

    def body(a_ref, *refs):
        b_refs, o_ref, acc_ref = refs[:len(bs)], refs[len(bs)], refs[len(bs) + 1]
        k = pl.program_id(0)

        @pl.when(k == 0)
        def _():
            acc_ref[...] = jnp.zeros_like(acc_ref)

        at = a_ref[...].T
        off = 0
        for b_ref, w in zip(b_refs, ws):
            acc_ref[:, off:off + w] += _dot(at, b_ref[...])
            off += w

        @pl.when(k == nk - 1)
        def _():
            for d in range(N_DEV):
                o_ref[d] = acc_ref[:, d * width:(d + 1) * width].astype(BF16)

    return pl.pallas_call(
        body, name=name, grid=(nk,),
        in_specs=[_rows(tk, m)] + [_rows(tk, w) for w in ws],
        out_specs=_fixed((N_DEV, m, width)),
        out_shape=jax.ShapeDtypeStruct((N_DEV, m, width), BF16),
        scratch_shapes=[pltpu.VMEM((m, sum(ws)), F32)],
        compiler_params=_params(("arbitrary",)),
    )(a, *bs)


def _dw_out(yc, yg, dh1b, token):
    rows = yc.shape[0]
    tk = _row_tile(rows, DW_ROW_TILE)
    nk = rows // tk

    def body(yc_ref, yg_ref, d_ref, token_ref, o_ref, acc_ref):
        k = pl.program_id(0)

        @pl.when(k == 0)
        def _():
            acc_ref[...] = jnp.zeros_like(acc_ref)

        d = d_ref[...]
        acc_ref[0:C_CONV, :] += _dot(yc_ref[...], d, _TN)
        acc_ref[C_CONV:D, :] += _dot(yg_ref[...], d, _TN)

        @pl.when(k == nk - 1)
        def _():
            o_ref[...] = acc_ref[...].astype(BF16)

    return pl.pallas_call(
        body, name="dw_out", grid=(nk,),
        in_specs=[_rows(tk, C_CONV), _rows(tk, GLA_V), _rows(tk, D), _fixed((8, 128))],
        out_specs=_fixed((D, D)), out_shape=jax.ShapeDtypeStruct((D, D), BF16),
        scratch_shapes=[pltpu.VMEM((D, D), F32)],
        compiler_params=_params(("arbitrary",)),
    )(yc, yg, dh1b, token)


def _matmul_tn(a, b, name):
    rows, m = a.shape
    n = b.shape[1]
    tk = _row_tile(rows, DW_ROW_TILE)
    tn = n if n <= 1024 else FF_CHUNK
    tm_ = m if m <= 1024 else FF_CHUNK
    assert n % tn == 0 and m % tm_ == 0
    nk = rows // tk

    def body(a_ref, b_ref, o_ref, acc_ref):
        k = pl.program_id(2)

        @pl.when(k == 0)
        def _():
            acc_ref[...] = jnp.zeros_like(acc_ref)

        acc_ref[...] += _dot(a_ref[...], b_ref[...], _TN)

        @pl.when(k == nk - 1)
        def _():
            o_ref[...] = acc_ref[...].astype(BF16)

    return pl.pallas_call(
        body, name=name, grid=(m // tm_, n // tn, nk),
        in_specs=[pl.BlockSpec((tk, tm_), lambda i, j, k: (k, i)), pl.BlockSpec((tk, tn), lambda i, j, k: (k, j))],
        out_specs=pl.BlockSpec((tm_, tn), lambda i, j, k: (i, j)),
        out_shape=jax.ShapeDtypeStruct((m, n), BF16),
        scratch_shapes=[pltpu.VMEM((tm_, tn), F32)],
        compiler_params=_params(("parallel", "parallel", "arbitrary")),
    )(a, b)


HALO = 32
LANES = 128


def _shifted(win, offsets):
    for r in range(8):
        js = [j for j, k in enumerate(offsets) if k % 8 == r]
        if js:
            rolled = win if r == 0 else pltpu.roll(win, CHUNK + HALO - r, 0)
            for j in js:
                yield j, rolled[offsets[j] - r:offsets[j] - r + CHUNK]


def _glu_into(uc_ref, vs_ref, n_chunk):
    vs_ref[0:CHUNK, :] = jnp.zeros((CHUNK, C_CONV), F32)

    def glu(i, carry):
        base = pl.multiple_of(i * CHUNK, CHUNK)
        val = uc_ref[pl.ds(base, CHUNK), 0:C_CONV]
        gate = uc_ref[pl.ds(base, CHUNK), C_CONV:2 * C_CONV]
        vs_ref[pl.ds(base + CHUNK, CHUNK), :] = val * _sigmoid(gate)
        return carry

    lax.fori_loop(0, n_chunk, glu, 0)


def _fwd_conv(uc, conv_w, conv_b, ln_g, ln_b, token, n_ex):
    rows = uc.shape[0]
    lp = rows // n_ex
    n_chunk = lp // CHUNK

    def body(uc_ref, w_ref, b_ref, lg_ref, lb_ref, token_ref, ypre_ref, yc_ref, vs_ref):
        _glu_into(uc_ref, vs_ref, n_chunk)

        def conv(i, carry):
            base = pl.multiple_of(i * CHUNK, CHUNK)
            for lb in range(C_CONV // LANES):
                ls = slice(lb * LANES, (lb + 1) * LANES)
                win = vs_ref[pl.ds(base + CHUNK - HALO, CHUNK + HALO), ls]
                acc = jnp.broadcast_to(b_ref[:, ls], (CHUNK, LANES))
                for j, rows_j in _shifted(win, [HALO - (CONV_W - 1) + j for j in range(CONV_W)]):
                    acc = acc + w_ref[j:j + 1, ls] * rows_j
                ypre_ref[pl.ds(base, CHUNK), ls] = acc
            y = ypre_ref[pl.ds(base, CHUNK), :]
            mu = jnp.mean(y, axis=-1, keepdims=True)
            yc_ = y - mu
            rstd = lax.rsqrt(jnp.mean(yc_ * yc_, axis=-1, keepdims=True) + LN_EPS)
            s = yc_ * rstd * lg_ref[...] + lb_ref[...]
            yc_ref[pl.ds(base, CHUNK), :] = (s * _sigmoid(s)).astype(BF16)
            return carry

        lax.fori_loop(0, n_chunk, conv, 0, unroll=3)

    ex = lambda w: pl.BlockSpec((lp, w), lambda b: (b, 0))
    return pl.pallas_call(
        body, name="fwd_conv", grid=(n_ex,),
        in_specs=[ex(2 * C_CONV), _fixed((32, C_CONV)), _fixed((1, C_CONV)), _fixed((1, C_CONV)), _fixed((1, C_CONV)),
                  _fixed((8, 128))],
        out_specs=[ex(C_CONV), ex(C_CONV)],
        out_shape=[jax.ShapeDtypeStruct((rows, C_CONV), F32), jax.ShapeDtypeStruct((rows, C_CONV), BF16)],
        scratch_shapes=[pltpu.VMEM((lp + CHUNK, C_CONV), F32)],
        compiler_params=_params(("parallel",)),
    )(uc, conv_w, conv_b, ln_g, ln_b, token)


def _bwd_conv(uc, ypre, dyc, conv_w, ln_g, ln_b, token, n_ex):
    rows = uc.shape[0]
    lp = rows // n_ex
    n_chunk = lp // CHUNK

    def body(uc_ref, ypre_ref, dyc_ref, w_ref, lg_ref, lb_ref, token_ref, duc_ref, dw_ref, dvec_ref, vs_ref, dys_ref,
             dwacc_ref):
        _glu_into(uc_ref, vs_ref, n_chunk)
        dys_ref[pl.ds(lp, CHUNK), :] = jnp.zeros((CHUNK, C_CONV), F32)
        dwacc_ref[...] = jnp.zeros_like(dwacc_ref)

        def ln_bwd(i, carry):
            dcb, dlg, dlb = carry
            base = pl.multiple_of(i * CHUNK, CHUNK)
            y = ypre_ref[pl.ds(base, CHUNK), :]
            mu = jnp.mean(y, axis=-1, keepdims=True)
            yc_ = y - mu
            rstd = lax.rsqrt(jnp.mean(yc_ * yc_, axis=-1, keepdims=True) + LN_EPS)
            xh = yc_ * rstd
            s = xh * lg_ref[...] + lb_ref[...]
            sg = _sigmoid(s)
            ds = dyc_ref[pl.ds(base, CHUNK), :] * (sg * (1.0 + s * (1.0 - sg)))
            dxh = ds * lg_ref[...]
            dy = rstd * (dxh - jnp.mean(dxh, axis=-1, keepdims=True) - xh * jnp.mean(dxh * xh, axis=-1, keepdims=True))
            dys_ref[pl.ds(base, CHUNK), :] = dy
            return (dcb + jnp.sum(dy, axis=0, keepdims=True), dlg + jnp.sum(ds * xh, axis=0, keepdims=True),
                    dlb + jnp.sum(ds, axis=0, keepdims=True))

        zero = jnp.zeros((1, C_CONV), F32)
        dcb, dlg, dlb = lax.fori_loop(0, n_chunk, ln_bwd, (zero, zero, zero))

        @pl.when(pl.program_id(0) == 0)
        def _():
            dvec_ref[...] = jnp.zeros_like(dvec_ref)
            dw_ref[...] = jnp.zeros_like(dw_ref)

        dvec_ref[0:1, :] += dcb
        dvec_ref[1:2, :] += dlg
        dvec_ref[2:3, :] += dlb

        def taps(i, carry):
            base = pl.multiple_of(i * CHUNK, CHUNK)
            for lb in range(C_CONV // LANES):
                ls = slice(lb * LANES, (lb + 1) * LANES)
                dwin = dys_ref[pl.ds(base, CHUNK + HALO), ls]
                vwin = vs_ref[pl.ds(base + CHUNK - HALO, CHUNK + HALO), ls]
                dy = dwin[0:CHUNK]
                acc = jnp.zeros((CHUNK, LANES), F32)
                for j, rows_j in _shifted(dwin, [CONV_W - 1 - j for j in range(CONV_W)]):
                    acc = acc + w_ref[j:j + 1, ls] * rows_j
                for j, rows_j in _shifted(vwin, [HALO - (CONV_W - 1) + j for j in range(CONV_W)]):
                    dwacc_ref[8 * j:8 * j + 8, ls] += jnp.sum((dy * rows_j).reshape(CHUNK // 8, 8, LANES), axis=0)
                val = uc_ref[pl.ds(base, CHUNK), ls]
                gate = uc_ref[pl.ds(base, CHUNK), C_CONV + lb * LANES:C_CONV + (lb + 1) * LANES]
                sg = _sigmoid(gate)
                duc_ref[pl.ds(base, CHUNK), ls] = (acc * sg).astype(BF16)
                duc_ref[pl.ds(base, CHUNK), C_CONV + lb * LANES:C_CONV + (lb + 1) * LANES] = (
                    acc * val * sg * (1.0 - sg)).astype(BF16)
            return carry

        lax.fori_loop(0, n_chunk, taps, 0, unroll=3)
        for j in range(CONV_W):
            dw_ref[j:j + 1, :] += jnp.sum(dwacc_ref[8 * j:8 * j + 8, :], axis=0, keepdims=True)

    ex = lambda w: pl.BlockSpec((lp, w), lambda b: (b, 0))
    return pl.pallas_call(
        body, name="bwd_conv", grid=(n_ex,),
        in_specs=[ex(2 * C_CONV), ex(C_CONV), ex(C_CONV), _fixed((32, C_CONV)), _fixed((1, C_CONV)), _fixed((1, C_CONV)),
                  _fixed((8, 128))],
        out_specs=[ex(2 * C_CONV), _fixed((32, C_CONV)), _fixed((8, C_CONV))],
        out_shape=[jax.ShapeDtypeStruct((rows, 2 * C_CONV), BF16), jax.ShapeDtypeStruct((32, C_CONV), F32),
                   jax.ShapeDtypeStruct((8, C_CONV), F32)],
        scratch_shapes=[pltpu.VMEM((lp + CHUNK, C_CONV), F32), pltpu.VMEM((lp + CHUNK, C_CONV), F32),
                        pltpu.VMEM((8 * 32, C_CONV), F32)],
        compiler_params=_params(("arbitrary",)),
    )(uc, ypre, dyc, conv_w, ln_g, ln_b, token)


def _seg_chunks(n_chunk):
    return max(c for c in (11, 3, 1) if n_chunk % c == 0)


def _block_mask(shape, row_block, lane_block):
    return (lax.broadcasted_iota(jnp.int32, shape, 0) // row_block) == (lax.broadcasted_iota(jnp.int32, shape, 1) // lane_block)


def _per_head_rows(x, mask):
    return jnp.where(mask, jnp.concatenate([x] * GLA_H, axis=0), 0)


def _fold_heads(full, lane_block):
    lane = lax.broadcasted_iota(jnp.int32, (1, full.shape[1]), 1) // lane_block
    out = jnp.where(lane == 0, full[0:CHUNK], 0.0)
    for h in range(1, GLA_H):
        out = out + jnp.where(lane == h, full[h * CHUNK:(h + 1) * CHUNK], 0.0)
    return out


def _causal_heads():
    return (lax.broadcasted_iota(jnp.int32, (CHUNK, GLA_H * CHUNK), 1) % CHUNK) <= lax.broadcasted_iota(
        jnp.int32, (CHUNK, GLA_H * CHUNK), 0)


def _cumsum_rows(x):
    row = lax.broadcasted_iota(jnp.int32, x.shape, 0)
    s = 1
    while s < CHUNK:
        x = x + jnp.where(row >= s, pltpu.roll(x, s, 0), 0.0)
        s *= 2
    return x


def _rev_cumsum_rows(x):
    row = lax.broadcasted_iota(jnp.int32, x.shape, 0)
    s = 1
    while s < CHUNK:
        x = x + jnp.where(row < CHUNK - s, pltpu.roll(x, CHUNK - s, 0), 0.0)
        s *= 2
    return x


def _gate_terms(lr_ref, w2_ref, gb_ref, rs, first_pos):
    z = _dot(lr_ref[rs, :].astype(BF16), w2_ref[...]) + gb_ref[...]
    la = (jnp.minimum(z, 0.0) - jnp.log(1.0 + jnp.exp(-jnp.abs(z)))) * (1.0 / TAU)
    pos = first_pos + lax.broadcasted_iota(jnp.int32, (CHUNK, 1), 0)
    live = pos >= ZROWS
    la = jnp.where(live, la, 0.0)
    return z, live, _cumsum_rows(la)


def _fwd_gla(qk, vg, lr, w2p, gb, ng, token, n_ex):
    rows = qk.shape[0]
    lp = rows // n_ex
    n_chunk = lp // CHUNK
    sc = _seg_chunks(n_chunk)
    n_seg = n_chunk // sc
    seg = sc * CHUNK

    def body(qk_ref, vg_ref, lr_ref, w2_ref, gb_ref, ng_ref, token_ref, yg_ref, o_ref, st_ref, state_ref):
        sidx = pl.program_id(1)

        @pl.when(sidx == 0)
        def _():
            state_ref[...] = jnp.zeros_like(state_ref)

        causal = _causal_heads()
        k_mask = _block_mask((GLA_H * CHUNK, GLA_K), CHUNK, GLA_DK)
        v_mask = _block_mask((GLA_H * CHUNK, GLA_V), CHUNK, GLA_DV)
        s_mask = _block_mask((GLA_V, GLA_K), GLA_DV, GLA_DK)

        def chunk(ci, carry):
            base = pl.multiple_of(ci * CHUNK, CHUNK)
            rs = pl.ds(base, CHUNK)
            _, _, bcum = _gate_terms(lr_ref, w2_ref, gb_ref, rs, (sidx * sc + ci) * CHUNK)
            bl = bcum[CHUNK - 1:CHUNK, :]
            q = qk_ref[rs, 0:GLA_K]
            k = qk_ref[rs, GLA_K:2 * GLA_K]
            qt = (q * (GLA_DK ** -0.5) * jnp.exp(bcum)).astype(BF16)
            kt = (k * jnp.exp(-bcum)).astype(BF16)
            kh = (k * jnp.exp(bl - bcum)).astype(BF16)
            vb = vg_ref[rs, 0:GLA_V].astype(BF16)
            state = state_ref[...]
            st_ref[ci] = state
            a = jnp.where(causal, _dot(qt, _per_head_rows(kt, k_mask), _NT), 0.0)
            o = _dot(a.astype(BF16), _per_head_rows(vb, v_mask)) + _dot(qt, state.astype(BF16), _NT)
            o_ref[rs, :] = o
            for h in range(GLA_H):
                hs = slice(h * GLA_DV, (h + 1) * GLA_DV)
                oh = o[:, hs]
                ro = lax.rsqrt(jnp.mean(oh * oh, axis=-1, keepdims=True) + RMS_EPS)
                g = vg_ref[rs, GLA_V + h * GLA_DV:GLA_V + (h + 1) * GLA_DV]
                yg_ref[rs, hs] = (oh * ro * ng_ref[...] * g * _sigmoid(g)).astype(BF16)
            state_ref[...] = state * jnp.exp(bl) + jnp.where(s_mask, _dot(vb, kh, _TN), 0.0)
            return carry

        lax.fori_loop(0, sc, chunk, 0, unroll=True)

    sg = lambda w: pl.BlockSpec((seg, w), lambda b, s: (b * n_seg + s, 0))
    return pl.pallas_call(
        body, name="fwd_gla", grid=(n_ex, n_seg),
        in_specs=[sg(2 * GLA_K), sg(2 * GLA_V), sg(RANK_P), _fixed((RANK_P, GLA_K)), _fixed((1, GLA_K)), _fixed((1, GLA_DV)),
                  _fixed((8, 128))],
        out_specs=[sg(GLA_V), sg(GLA_V), pl.BlockSpec((sc, GLA_V, GLA_K), lambda b, s: (b * n_seg + s, 0, 0))],
        out_shape=[jax.ShapeDtypeStruct((rows, GLA_V), BF16), jax.ShapeDtypeStruct((rows, GLA_V), F32),
                   jax.ShapeDtypeStruct((n_ex * n_chunk, GLA_V, GLA_K), F32)],
        scratch_shapes=[pltpu.VMEM((GLA_V, GLA_K), F32)],
        compiler_params=_params(("parallel", "arbitrary")),
    )(qk, vg, lr, w2p, gb, ng, token)


def _bwd_gla(qk, vg, lr, o, st, dyg, w2p, gb, ng, token, n_ex):
    rows = qk.shape[0]
    lp = rows // n_ex
    n_chunk = lp // CHUNK
    sc = _seg_chunks(n_chunk)
    n_seg = n_chunk // sc
    seg = sc * CHUNK

    def body(qk_ref, vg_ref, lr_ref, o_ref, st_ref, dyg_ref, w2_ref, gb_ref, ng_ref, token_ref,
             dqk_ref, dvg_ref, dlr_ref, dw2_ref, dvec_ref, gt_ref, dz_ref):
        step = pl.program_id(1)
        sidx = n_seg - 1 - step

        @pl.when(step == 0)
        def _():
            gt_ref[...] = jnp.zeros_like(gt_ref)

        @pl.when((step == 0) & (pl.program_id(0) == 0))
        def _():
            dw2_ref[...] = jnp.zeros_like(dw2_ref)
            dvec_ref[...] = jnp.zeros_like(dvec_ref)

        causal = _causal_heads()
        k_mask = _block_mask((GLA_H * CHUNK, GLA_K), CHUNK, GLA_DK)
        v_mask = _block_mask((GLA_H * CHUNK, GLA_V), CHUNK, GLA_DV)
        s_mask = _block_mask((GLA_V, GLA_K), GLA_DV, GLA_DK)
        last_row = lax.broadcasted_iota(jnp.int32, (CHUNK, 1), 0) == CHUNK - 1
        ng = ng_ref[...]

        def chunk(ii, dng):
            ci = sc - 1 - ii
            base = pl.multiple_of(ci * CHUNK, CHUNK)
            rs = pl.ds(base, CHUNK)
            z, live, bcum = _gate_terms(lr_ref, w2_ref, gb_ref, rs, (sidx * sc + ci) * CHUNK)
            bl = bcum[CHUNK - 1:CHUNK, :]
            ebl = jnp.exp(bl)
            q = qk_ref[rs, 0:GLA_K]
            k = qk_ref[rs, GLA_K:2 * GLA_K]
            eb = jnp.exp(bcum)
            enb = jnp.exp(-bcum)
            ehb = jnp.exp(bl - bcum)
            qt = q * (GLA_DK ** -0.5) * eb
            kt = k * enb
            kh = k * ehb
            qtb = qt.astype(BF16)
            vb = vg_ref[rs, 0:GLA_V].astype(BF16)
            k_rows = _per_head_rows(kt.astype(BF16), k_mask)
            v_rows = _per_head_rows(vb, v_mask)
            gt = gt_ref[...]
            gtb = gt.astype(BF16)
            s_in = st_ref[ci]
            dos = []
            for h in range(GLA_H):
                hs = slice(h * GLA_DV, (h + 1) * GLA_DV)
                gs = slice(GLA_V + h * GLA_DV, GLA_V + (h + 1) * GLA_DV)
                oh = o_ref[rs, hs]
                ro = lax.rsqrt(jnp.mean(oh * oh, axis=-1, keepdims=True) + RMS_EPS)
                on = oh * ro
                g = vg_ref[rs, gs]
                sg = _sigmoid(g)
                dout = dyg_ref[rs, hs]
                dvg_ref[rs, gs] = (dout * on * ng * (sg * (1.0 + g * (1.0 - sg)))).astype(BF16)
                dw = dout * g * sg
                dng = dng + jnp.sum(dw * on, axis=0, keepdims=True)
                don = dw * ng
                dos.append((ro * (don - on * jnp.mean(don * on, axis=-1, keepdims=True))).astype(BF16))
            dob = jnp.concatenate(dos, axis=1)
            a = jnp.where(causal, _dot(qtb, k_rows, _NT), 0.0).astype(BF16)
            da = jnp.where(causal, _dot(dob, v_rows, _NT), 0.0).astype(BF16)
            dv = _fold_heads(_dot(a, dob, _TN), GLA_DV) + _dot(kh.astype(BF16), gtb, _NT)
            dvg_ref[rs, 0:GLA_V] = dv.astype(BF16)
            dkh = _dot(vb, gtb)
            dqt = _dot(da, k_rows) + _dot(dob, s_in.astype(BF16))
            dkt = _fold_heads(_dot(da, qtb, _TN), GLA_DK)
            dbl = jnp.sum(gt * s_in, axis=0, keepdims=True) * ebl + jnp.sum(dkh * kh, axis=0, keepdims=True)
            dqk_ref[rs, 0:GLA_K] = (dqt * (GLA_DK ** -0.5) * eb).astype(BF16)
            dqk_ref[rs, GLA_K:2 * GLA_K] = (dkt * enb + dkh * ehb).astype(BF16)
            db = dqt * qt - dkt * kt - dkh * kh
            db = jnp.where(last_row, db + dbl, db)
            dla = jnp.where(live, _rev_cumsum_rows(db), 0.0)
            dz_ref[rs, :] = dla * (1.0 / TAU) * (1.0 - _sigmoid(z))
            gt_ref[...] = jnp.where(s_mask, _dot(dob, qtb, _TN), 0.0) + gt * ebl
            return dng

        dng = lax.fori_loop(0, sc, chunk, jnp.zeros((1, GLA_DV), F32), unroll=True)
        dz = dz_ref[...]
        dzb = dz.astype(BF16)
        dlr_ref[...] = _dot(dzb, w2_ref[...], _NT).astype(BF16)
        dw2_ref[...] += _dot(lr_ref[...].astype(BF16), dzb, _TN)
        dvec_ref[0:1, :] += jnp.sum(dz, axis=0, keepdims=True)
        dvec_ref[1:2, 0:GLA_DV] += dng

    sg_ = lambda w: pl.BlockSpec((seg, w), lambda b, s: (b * n_seg + n_seg - 1 - s, 0))
    return pl.pallas_call(
        body, name="bwd_gla", grid=(n_ex, n_seg),
        in_specs=[sg_(2 * GLA_K), sg_(2 * GLA_V), sg_(RANK_P), sg_(GLA_V),
                  pl.BlockSpec((sc, GLA_V, GLA_K), lambda b, s: (b * n_seg + n_seg - 1 - s, 0, 0)), sg_(GLA_V),
                  _fixed((RANK_P, GLA_K)), _fixed((1, GLA_K)), _fixed((1, GLA_DV)), _fixed((8, 128))],
        out_specs=[sg_(2 * GLA_K), sg_(2 * GLA_V), sg_(RANK_P), _fixed((RANK_P, GLA_K)), _fixed((8, GLA_K))],
        out_shape=[jax.ShapeDtypeStruct((rows, 2 * GLA_K), BF16), jax.ShapeDtypeStruct((rows, 2 * GLA_V), BF16),
                   jax.ShapeDtypeStruct((rows, RANK_P), BF16), jax.ShapeDtypeStruct((RANK_P, GLA_K), F32),
                   jax.ShapeDtypeStruct((8, GLA_K), F32)],
        scratch_shapes=[pltpu.VMEM((GLA_V, GLA_K), F32), pltpu.VMEM((seg, GLA_K), F32)],
        compiler_params=_params(("arbitrary", "arbitrary")),
    )(qk, vg, lr, o, st, dyg, w2p, gb, ng, token)


def _pad_rows(x, tgt):
    return jnp.pad(x, ((0, 0), (LEAD, 0), (0, 0))), jnp.pad(tgt, ((0, 0), (LEAD, 0), (0, 0)))


def _local_step(h0, tgt_p, p, pass_on, late_weights, send_early):
    n_ex, lp, _ = h0.shape
    rows = n_ex * lp
    meta = jnp.broadcast_to(p["meta"][None], (n_ex, N_META, D))
    h0 = lax.dynamic_update_slice(h0, meta, (0, ZROWS, 0)).reshape(rows, D)
    tgt_p = tgt_p.reshape(rows, D)

    uc, qk, vg, lr, n1 = _fwd_inproj(h0, p["g1"], p["w_in"])
    ypre, yc = _fwd_conv(uc, p["conv_w"], p["conv_b"], p["ln_g"], p["ln_b"], p["token"], n_ex)
    token = pass_on(yc)
    yg, o, st = _fwd_gla(qk, vg, lr, p["w2"], p["gb"], p["ng"], token, n_ex)
    w_out, wg, wu, wd = late_weights(yg)
    n2, f, da, db, dh2, dh1, dh1b, dyc, dyg, part = _mid_rows(
        yc, yg, h0, tgt_p, w_out, wg, wu, wd, p["g2"], p["g3"], token, lp)
    g = {}
    token = send_early("ffn", [_matmul_tn(a_, b_, name).reshape(N_DEV, FF_S, D) for a_, b_, name in (
        (da, n2, "dw_gate"), (db, n2, "dw_up"), (f, dh2, "dw_down"))])
    token = send_early("out", [_dw_out(yc, yg, dh1b, token).reshape(N_DEV, W_OUT_S, D)])
    duc, g["conv_w"], g["conv_vec"] = _bwd_conv(uc, ypre, dyc, p["conv_w"], p["ln_g"], p["ln_b"], token, n_ex)
    dqk, dvg, dlr, g["w2"], g["gla_vec"] = _bwd_gla(qk, vg, lr, o, st, dyg, p["w2"], p["gb"], p["ng"], token, n_ex)
    token = send_early("in", [_dw_blocked(n1, [duc, dqk, dvg, dlr], W_IN_S, "dw_in")])
    grad_x, g["in_vec"], g["meta"] = _bwd_inproj(duc, dqk, dvg, dlr, dh1, h0, p["w_in"], p["g1"], token, lp)
    g["ffn_vec"] = part
    return grad_x, g


W_IN_S = D_IN // N_DEV
W_OUT_S = D // N_DEV
FF_S = D_FF // N_DEV
CONV_S = C_CONV // N_DEV
GATE_S = GLA_K // N_DEV
SMALL_PACK = 64
CONV_ROW = 16
GATE_ROW = 48
VEC_ROWS = 16
_VEC_ROWS = (("norm_mix_g", D), ("conv_b", C_CONV), ("conv_ln_g", C_CONV), ("conv_ln_b", C_CONV), ("gla_gate_b", GLA_K),
             ("gla_norm_g", GLA_DV), ("norm_ffn_g", D), ("norm_final_g", D))
LOSS_ROW = len(_VEC_ROWS)


def _position():
    return lax.axis_index("x"), lax.axis_index("y"), lax.axis_index("c")


def _any():
    return pl.BlockSpec(memory_space=pl.ANY)


def _stage(mats, meta, conv_w, w2):
    n_t = len(mats) + 1

    def body(*refs):
        ins = refs[0:n_t - 1]
        meta_ref, cw_ref, w2_ref = refs[n_t - 1:n_t + 2]
        lands = refs[n_t + 2:2 * n_t + 2]
        shards = refs[2 * n_t + 2:3 * n_t + 2]
        sems = refs[3 * n_t + 2]
        for s_ref, w_ref in zip(shards, ins):
            s_ref[...] = w_ref[...].astype(BF16)
        sp = shards[n_t - 1]
        sp[...] = jnp.zeros_like(sp)
        sp[0:N_META, :] = meta_ref[...]
        sp[CONV_ROW:CONV_ROW + CONV_W, 0:CONV_S] = cw_ref[...]
        sp[GATE_ROW:GATE_ROW + RANK, 0:GATE_S] = w2_ref[...]
        x, y, c = _position()
        mine = [pltpu.make_async_copy(shards[t], lands[t].at[4 * x + 2 * y + c], sems.at[t]) for t in range(n_t)]
        for cp in mine:
            cp.start()
        for cp in mine:
            cp.wait()

    shard_shapes = [jax.ShapeDtypeStruct(m.shape, BF16) for m in mats] + [jax.ShapeDtypeStruct((SMALL_PACK, 128), F32)]
    res = pl.pallas_call(
        body, name="stage",
        out_shape=[jax.ShapeDtypeStruct((N_DEV,) + s.shape, s.dtype) for s in shard_shapes] + shard_shapes,
        in_specs=[_whole_vmem()] * (n_t + 2), out_specs=[_any()] * n_t + [_whole_vmem()] * n_t,
        scratch_shapes=[pltpu.SemaphoreType.DMA((n_t,))],
        compiler_params=pltpu.CompilerParams(vmem_limit_bytes=VMEM_LIMIT),
    )(*mats, meta, conv_w, w2)
    return res[0:n_t], res[n_t:]


_HBM = pl.BlockSpec(memory_space=pltpu.HBM)
_SEM = pl.BlockSpec(memory_space=pltpu.SEMAPHORE)
_EFFECT = pltpu.SideEffectType.DATAFLOW_SIDE_EFFECTING


_N_ROUTES = {"scatter": 7, "first": 4, "forward": 3}


def _routes(mode):
    x, y, c = _position()
    me = 4 * x + 2 * y + c
    if mode == "scatter":
        out = []
        for k in range(1, N_DEV):
            px = 1 - x if k & 4 else x
            py = 1 - y if k & 2 else y
            pc = 1 - c if k & 1 else c
            out.append(((px, py, pc), 4 * px + 2 * py + pc, me))
        return out
    if mode == "first":
        return [(pos, None, me) for pos in ((x, y, 1 - c), (1 - x, y, c), (x, 1 - y, c), (1 - x, 1 - y, c))]
    assert mode == "forward"
    return [((x, y, 1 - c), 4 * px + 2 * py + c, 4 * px + 2 * py + c) for px, py in ((1 - x, y), (x, 1 - y), (1 - x, 1 - y))]


def _route_copies(mode, n, src_refs, land_refs, send_sems, recv_sems):
    nr = _N_ROUTES[mode]
    for i, (pos, src_blk, dst_blk) in enumerate(_routes(mode)):
        for t in range(n):
            src = land_refs[t] if mode == "forward" else src_refs[t]
            yield pltpu.make_async_remote_copy(
                src_ref=src if src_blk is None else src.at[src_blk], dst_ref=land_refs[t].at[dst_blk],
                send_sem=send_sems.at[nr * t + i], recv_sem=recv_sems.at[nr * t + i], device_id=pos, device_id_type=MESH)


def _in_hbm(a):
    return pltpu.with_memory_space_constraint(a, pltpu.HBM)


def _send_start(name, srcs, lands, mode, after):
    n, ns = len(lands), len(srcs)
    nsem = _N_ROUTES[mode] * n

    def body(*refs):
        src_refs, land_refs = refs[0:ns], refs[ns:ns + n]
        send_sems, recv_sems = refs[ns + n + 1:ns + n + 3]
        token = refs[2 * (ns + n) + 3]
        for cp in _route_copies(mode, n, src_refs, land_refs, send_sems, recv_sems):
            cp.start()
        token[...] = jnp.zeros_like(token)

    bufs = list(srcs) + list(lands)
    res = pl.pallas_call(
        body, name=name,
        out_shape=(pltpu.SemaphoreType.DMA((nsem,)), pltpu.SemaphoreType.DMA((nsem,)),
                   *[pltpu.HBM(b.shape, b.dtype) for b in bufs], jax.ShapeDtypeStruct((8, 128), F32)),
        in_specs=[_HBM] * len(bufs) + [_any()], out_specs=(_SEM, _SEM, *[_HBM] * len(bufs), _whole_vmem()),
        input_output_aliases={i: 2 + i for i in range(len(bufs))},
        compiler_params=pltpu.CompilerParams(has_side_effects=_EFFECT),
    )(*[_in_hbm(b) for b in bufs], after)
    return res[0], res[1], res[2:2 + ns], res[2 + ns:2 + ns + n], res[2 + ns + n]


def _send_wait(name, send_sems, recv_sems, srcs, lands, mode, after):
    n, ns = len(lands), len(srcs)
    after = after if isinstance(after, tuple) else (after,)

    def body(*refs):
        src_refs, land_refs = refs[0:ns], refs[ns:ns + n]
        send_sems, recv_sems = refs[ns + n:ns + n + 2]
        for cp in _route_copies(mode, n, src_refs, land_refs, send_sems, recv_sems):
            cp.wait_send()
            cp.wait_recv()

    bufs = list(srcs) + list(lands)
    res = pl.pallas_call(
        body, name=name,
        out_shape=tuple(pltpu.HBM(b.shape, b.dtype) for b in bufs),
        in_specs=[_HBM] * len(bufs) + [_SEM, _SEM] + [_any()] * len(after), out_specs=tuple([_HBM] * len(bufs)),
        input_output_aliases={i: i for i in range(len(bufs))},
        compiler_params=pltpu.CompilerParams(has_side_effects=_EFFECT),
    )(*bufs, send_sems, recv_sems, *after)
    return res[0:ns], res[ns:ns + n]


def _unshard_in(a_in, a_small, token):
    def body(a_ref, s_ref, token_ref, w_ref, meta_ref, cw_ref, w2_ref):
        w_ref[:, D_IN:D_INP] = jnp.zeros((D, D_INP - D_IN), BF16)
        w2_ref[...] = jnp.zeros_like(w2_ref)
        for d in range(N_DEV):
            w_ref[:, d * W_IN_S:(d + 1) * W_IN_S] = a_ref[d]
            meta_ref[:, d * 128:(d + 1) * 128] = s_ref[d, 0:N_META, :]
            cw_ref[:, d * CONV_S:(d + 1) * CONV_S] = s_ref[d, CONV_ROW:CONV_ROW + 32, 0:CONV_S]
            w2_ref[0:RANK, d * GATE_S:(d + 1) * GATE_S] = s_ref[d, GATE_ROW:GATE_ROW + RANK, 0:GATE_S].astype(BF16)

    return pl.pallas_call(
        body, name="unshard_in",
        out_shape=[jax.ShapeDtypeStruct((D, D_INP), BF16), jax.ShapeDtypeStruct((N_META, D), F32),
                   jax.ShapeDtypeStruct((32, C_CONV), F32), jax.ShapeDtypeStruct((RANK_P, GLA_K), BF16)],
        compiler_params=pltpu.CompilerParams(vmem_limit_bytes=VMEM_LIMIT),
    )(a_in, a_small, token)


def _pack_small(g):
    def body(meta_ref, cw_ref, w2_ref, in_vec, ffn_vec, conv_vec, gla_vec, sp, vp):
        sp[...] = jnp.zeros_like(sp)
        vp[...] = jnp.zeros_like(vp)
        for d in range(N_DEV):
            sp[d, 0:N_META, :] = meta_ref[:, d * 128:(d + 1) * 128]
            sp[d, CONV_ROW:CONV_ROW + 32, 0:CONV_S] = cw_ref[:, d * CONV_S:(d + 1) * CONV_S]
            sp[d, GATE_ROW:GATE_ROW + RANK, 0:GATE_S] = w2_ref[0:RANK, d * GATE_S:(d + 1) * GATE_S]
            vp[d, 0:1, :] = in_vec[0:1, :]
            vp[d, 1:4, 0:C_CONV] = conv_vec[0:3, :]
            vp[d, 4:5, 0:GLA_K] = gla_vec[0:1, :]
            vp[d, 5:6, 0:GLA_DV] = gla_vec[1:2, 0:GLA_DV]
            vp[d, 6:7, :] = ffn_vec[1:2, :]
            vp[d, 7:8, :] = ffn_vec[0:1, :]
            vp[d, LOSS_ROW:LOSS_ROW + 1, :] = ffn_vec[2:3, :]

    return pl.pallas_call(
        body, name="pack_small",
        out_shape=[jax.ShapeDtypeStruct((N_DEV, SMALL_PACK, 128), F32), jax.ShapeDtypeStruct((N_DEV, VEC_ROWS, D), F32)],
    )(g["meta"], g["conv_w"], g["w2"], g["in_vec"], g["ffn_vec"], g["conv_vec"], g["gla_vec"])


def _adamw(w, g, m, v):
    m = ADAM_B1 * m + (1.0 - ADAM_B1) * g
    v = ADAM_B2 * v + (1.0 - ADAM_B2) * (g * g)
    m_hat = m / (1.0 - ADAM_B1 ** ADAM_STEP)
    v_hat = v / (1.0 - ADAM_B2 ** ADAM_STEP)
    return -ADAM_LR * (m_hat / (jnp.sqrt(v_hat) + ADAM_EPS) + ADAM_WD * w), m, v


def _update_matrix(recv, own, me, w, m, v, name):
    _, r, c = recv.shape
    tr = _row_tile(r, 256)

    def body(me_ref, recv_ref, own_ref, w_ref, m_ref, v_ref, g_ref, d_ref, nm_ref, nv_ref):
        g = jnp.zeros((tr, c), F32)
        for s in range(N_DEV):
            g = g + jnp.where(me_ref[0] == s, own_ref[...], recv_ref[s]).astype(F32)
        g_ref[...] = g
        d_ref[...], nm_ref[...], nv_ref[...] = _adamw(w_ref[...], g, m_ref[...], v_ref[...])

    one = pl.BlockSpec((None, tr, c), lambda i, me_ref: (0, i, 0))
    return pl.pallas_call(
        body, name=name,
        grid_spec=pltpu.PrefetchScalarGridSpec(
            num_scalar_prefetch=1, grid=(r // tr,),
            in_specs=[pl.BlockSpec((N_DEV, tr, c), lambda i, me_ref: (0, i, 0)),
                      pl.BlockSpec((None, tr, c), lambda i, me_ref: (me_ref[0], i, 0)), one, one, one],
            out_specs=[one] * 4),
        out_shape=[jax.ShapeDtypeStruct((1, r, c), F32)] * 4,
        compiler_params=_params(("parallel",)),
    )(me, recv, own, w, m, v)


_SMALL = ("meta_tokens", "conv_w", "gla_w_gate2") + tuple(n for n, _ in _VEC_ROWS)


def _update_small(me, srecv, vrecv, sown, vown, w, m, v):
    n = len(_SMALL)

    def body(*refs):
        me_ref, s_ref, v_ref, so_ref, vo_ref = refs[0:5]
        w_refs, m_refs, v_refs = refs[5:5 + n], refs[5 + n:5 + 2 * n], refs[5 + 2 * n:5 + 3 * n]
        outs = refs[5 + 3 * n:]
        ssum = jnp.zeros((SMALL_PACK, 128), F32)
        vsum = jnp.zeros((VEC_ROWS, D), F32)
        for s in range(N_DEV):
            ssum = ssum + jnp.where(me_ref[0] == s, so_ref[s], s_ref[s])
            vsum = vsum + jnp.where(me_ref[0] == s, vo_ref[s], v_ref[s])
        grads = [ssum[0:N_META, :], ssum[CONV_ROW:CONV_ROW + CONV_W, 0:CONV_S], ssum[GATE_ROW:GATE_ROW + RANK, 0:GATE_S]]
        grads += [vsum[i:i + 1, 0:width] for i, (_, width) in enumerate(_VEC_ROWS)]
        for i, g in enumerate(grads):
            d, nm, nv = _adamw(w_refs[i][...], g, m_refs[i][...], v_refs[i][...])
            outs[i][...] = g
            outs[n + i][...] = d
            outs[2 * n + i][...] = nm
            outs[3 * n + i][...] = nv
        outs[4 * n][...] = vsum[LOSS_ROW:LOSS_ROW + 1, 0:128]

    shapes = [jax.ShapeDtypeStruct(t.shape, F32) for t in w]
    res = pl.pallas_call(
        body, name="update_small", out_shape=shapes * 4 + [jax.ShapeDtypeStruct((1, 128), F32)],
        in_specs=[pl.BlockSpec(memory_space=pltpu.SMEM)] + [_whole_vmem()] * (4 + 3 * n),
    )(me, srecv, vrecv, sown, vown, *w, *m, *v)
    return res[0:n], res[n:2 * n], res[2 * n:3 * n], res[3 * n:4 * n], res[4 * n]


_WEIGHTS = ("meta_tokens", "norm_mix_g", "w_in", "conv_w", "conv_b", "conv_ln_g", "conv_ln_b", "gla_w_gate2", "gla_gate_b",
            "gla_norm_g", "w_out", "norm_ffn_g", "w_ffn_gate", "w_ffn_up", "w_ffn_down", "norm_final_g")
_MATRICES = ("w_in", "w_out", "w_ffn_gate", "w_ffn_up", "w_ffn_down")
_TRANSPOSED = ("w_ffn_gate", "w_ffn_up")


def kernel(x, meta_tokens, norm_mix_g, w_in, conv_w, conv_b, conv_ln_g, conv_ln_b, gla_w_gate2, gla_gate_b, gla_norm_g, w_out, norm_ffn_g, w_ffn_gate, w_ffn_up, w_ffn_down, norm_final_g, loss_target, m_meta_tokens, m_norm_mix_g, m_w_in, m_conv_w, m_conv_b, m_conv_ln_g, m_conv_ln_b, m_gla_w_gate2, m_gla_gate_b, m_gla_norm_g, m_w_out, m_norm_ffn_g, m_w_ffn_gate, m_w_ffn_up, m_w_ffn_down, m_norm_final_g, v_meta_tokens, v_norm_mix_g, v_w_in, v_conv_w, v_conv_b, v_conv_ln_g, v_conv_ln_b, v_gla_w_gate2, v_gla_gate_b, v_gla_norm_g, v_w_out, v_norm_ffn_g, v_w_ffn_gate, v_w_ffn_up, v_w_ffn_down, v_norm_final_g):
    given = dict(locals())
    two_d = lambda a: a.reshape(1, -1) if a.ndim == 1 else a.reshape(a.shape[-2:])
    fams = [{n: given[pre + n] for n in _WEIGHTS} for pre in ("", "m_", "v_")]
    for f in fams:
        for n in _TRANSPOSED:
            f[n] = f[n].transpose(0, 2, 1)
    w = fams[0]

    lands, shards = _stage([two_d(w[n]) for n in _MATRICES], w["meta_tokens"], two_d(w["conv_w"]), two_d(w["gla_w_gate2"]))
    soon, later = (0, 5), (1, 2, 3, 4)
    pick = lambda seq, idx: [seq[i] for i in idx]
    first = _send_start("gather_first_start", pick(shards, soon), pick(lands, soon), "first", norm_mix_g)
    ffn_first = _send_start("gather_ffn_first_start", pick(shards, later), pick(lands, later), "first", first[4])
    h0, tgt_p = _pad_rows(x, loss_target)
    _, arrived = _send_wait("gather_first_wait", *first[0:4], "first", (h0, tgt_p, ffn_first[4]))
    forward = _send_start("gather_forward_start", [], arrived, "forward", ffn_first[4])
    _, (a_in, a_small) = _send_wait("gather_forward_wait", *forward[0:4], "forward", forward[4])
    w_in, meta, conv_taps, w2 = _unshard_in(a_in, a_small, forward[4])
    p = dict(meta=meta, conv_w=conv_taps, w2=w2, w_in=w_in, g1=norm_mix_g, conv_b=conv_b, ln_g=conv_ln_g, ln_b=conv_ln_b,
             gb=gla_gate_b, ng=gla_norm_g, g2=norm_ffn_g, g3=two_d(norm_final_g), token=forward[4])
    passed = {}

    def pass_on(after):
        _, arrived_ffn = _send_wait("gather_ffn_first_wait", *ffn_first[0:4], "first", after)
        passed["sent"] = _send_start("gather_ffn_forward_start", [], arrived_ffn, "forward", after)
        return passed["sent"][4]

    def late_weights(after):
        _, (a_out, a_g, a_u, a_d) = _send_wait("gather_ffn_forward_wait", *passed["sent"][0:4], "forward", after)
        return a_out.reshape(D, D), a_g.reshape(D_FF, D), a_u.reshape(D_FF, D), a_d.reshape(D_FF, D)

    sent = {}

    def send_early(tag, mats):
        landing = [_in_hbm(lax.empty(m_.shape, m_.dtype)) for m_ in mats]
        sent[tag] = _send_start("scatter_" + tag + "_start", mats, landing, "scatter", norm_mix_g)
        return sent[tag][4]

    grad_x, g = _local_step(h0, tgt_p, p, pass_on, late_weights, send_early)

    token = send_early("small", list(_pack_small(g)))
    x_, y_, c_ = _position()
    me = (4 * x_ + 2 * y_ + c_).astype(jnp.int32).reshape(1)
    res = {}
    for tag, names in (("ffn", ("w_ffn_gate", "w_ffn_up", "w_ffn_down")), ("out", ("w_out",)), ("in", ("w_in",))):
        own, recv = _send_wait("scatter_" + tag + "_wait", *sent[tag][0:4], "scatter", token)
        for n, o_, r_ in zip(names, own, recv):
            res[n] = _update_matrix(r_, o_, me, *[f[n] for f in fams], "update_" + n)
            token = res[n][1]
    (sown, vown), (srecv, vrecv) = _send_wait("scatter_small_wait", *sent["small"][0:4], "scatter", token)
    small = _update_small(me, srecv, vrecv, sown, vown, *[[two_d(f[n]) for n in _SMALL] for f in fams])
    for i, n in enumerate(_SMALL):
        res[n] = [fam[i].reshape(w[n].shape) for fam in small[0:4]]
    for n in _TRANSPOSED:
        res[n] = [t.transpose(0, 2, 1) for t in res[n]]
    outs = [small[4][0, 0], grad_x]
    for k in range(4):
        outs += [res[n][k] for n in _WEIGHTS]
    return tuple(outs)
```

```python
import functools

import jax
import jax.numpy as jnp
from jax import lax
from jax.experimental import pallas as pl
from jax.experimental.pallas import tpu as pltpu

F32 = jnp.float32
BF16 = jnp.bfloat16

D = 1024
N_META = 16
C_CONV = 512
CONV_W = 31
GLA_H = 4
GLA_DK = 64
GLA_DV = 128
GLA_K = GLA_H * GLA_DK
GLA_V = GLA_H * GLA_DV
RANK = 16
RANK_P = 128
TAU = 16.0
CHUNK = 64
LEAD = CHUNK
ZROWS = LEAD - N_META
D_IN = 2 * C_CONV + 2 * GLA_K + 2 * GLA_V + RANK
D_INP = D_IN - RANK + RANK_P
D_FF = 2816
FF_CHUNK = 1408
FF_SPLIT = (0, 1536, D_FF)
RMS_EPS = 1e-6
LN_EPS = 1e-5
N_DEV = 8

ADAM_LR = 0.001
ADAM_B1 = 0.9
ADAM_B2 = 0.999
ADAM_EPS = 1e-08
ADAM_WD = 0.01
ADAM_STEP = 10

VMEM_LIMIT = 60 * 1024 * 1024
ROW_TILE = 1056
FFN_ROW_TILE = 352
DW_ROW_TILE = 1408
MESH = pl.DeviceIdType.MESH

_NN = (((1,), (0,)), ((), ()))
_NT = (((1,), (1,)), ((), ()))
_TN = (((0,), (0,)), ((), ()))


def _dot(a, b, dims=_NN):
    return lax.dot_general(a, b, dims, preferred_element_type=F32)


def _sigmoid(x):
    return 1.0 / (1.0 + jnp.exp(-x))


def _row_tile(rows, target):
    best = None
    for t in range(16, min(rows, target) + 1, 16):
        if rows % t == 0:
            best = t
    assert best is not None, rows
    return best


def _params(sem=None):
    return pltpu.CompilerParams(dimension_semantics=sem, vmem_limit_bytes=VMEM_LIMIT)


def _whole_vmem():
    return pl.BlockSpec(memory_space=pltpu.VMEM)


def _rows(tm, width):
    return pl.BlockSpec((tm, width), lambda i: (i, 0))


def _fixed(shape):
    return pl.BlockSpec(shape, lambda *_: (0,) * len(shape))


def _fwd_inproj(h0, g1, w_in):
    rows = h0.shape[0]
    tm = _row_tile(rows, ROW_TILE)

    def body(h_ref, g_ref, w_ref, uc_ref, qk_ref, vg_ref, lr_ref, n1_ref):
        h = h_ref[...]
        r = lax.rsqrt(jnp.mean(h * h, axis=-1, keepdims=True) + RMS_EPS)
        n = (h * r * g_ref[...]).astype(BF16)
        n1_ref[...] = n
        uc_ref[...] = _dot(n, w_ref[:, 0:1024])
        qk_ref[...] = _dot(n, w_ref[:, 1024:1536])
        vg_ref[...] = _dot(n, w_ref[:, 1536:2560])
        lr_ref[...] = _dot(n, w_ref[:, 2560:2688])

    return pl.pallas_call(
        body, name="fwd_inproj", grid=(rows // tm,),
        in_specs=[_rows(tm, D), _fixed((1, D)), _whole_vmem()],
        out_specs=[_rows(tm, 1024), _rows(tm, 512), _rows(tm, 1024), _rows(tm, RANK_P), _rows(tm, D)],
        out_shape=[jax.ShapeDtypeStruct((rows, 1024), F32), jax.ShapeDtypeStruct((rows, 512), F32),
                   jax.ShapeDtypeStruct((rows, 1024), F32), jax.ShapeDtypeStruct((rows, RANK_P), F32),
                   jax.ShapeDtypeStruct((rows, D), BF16)],
        compiler_params=_params(("parallel",)),
    )(h0, g1, w_in)


def _mid_rows(yc, yg, h0, tgt, w_out, wg, wu, wd, g2, g3, token, rows_per_example):
    rows = h0.shape[0]
    tm = _row_tile(rows, FFN_ROW_TILE)
    ff_blocks = [slice(lo, hi) for lo, hi in zip(FF_SPLIT[:-1], FF_SPLIT[1:])]

    def body(yc_ref, yg_ref, h0_ref, t_ref, wo_ref, wg_ref, wu_ref, wd_ref, g2_ref, g3_ref, token_ref,
             n2_ref, f_ref, da_ref, db_ref, dh2_ref, dh1_ref, dh1b_ref, dyc_ref, dyg_ref, part_ref):
        i = pl.program_id(0)
        h1 = h0_ref[...] + _dot(yc_ref[...], wo_ref[0:C_CONV, :]) + _dot(yg_ref[...], wo_ref[C_CONV:D, :])
        r2 = lax.rsqrt(jnp.mean(h1 * h1, axis=-1, keepdims=True) + RMS_EPS)
        xh2 = h1 * r2
        n2 = (xh2 * g2_ref[...]).astype(BF16)
        n2_ref[...] = n2
        y2 = jnp.zeros((tm, D), F32)
        for cs in ff_blocks:
            a = _dot(n2, wg_ref[cs, :], _NT)
            b = _dot(n2, wu_ref[cs, :], _NT)
            f = (a * _sigmoid(a) * b).astype(BF16)
            f_ref[:, cs] = f
            da_ref[:, cs] = a.astype(BF16)
            db_ref[:, cs] = b.astype(BF16)
            y2 = y2 + _dot(f, wd_ref[cs, :])
        h2 = h1 + y2
        r3 = lax.rsqrt(jnp.mean(h2 * h2, axis=-1, keepdims=True) + RMS_EPS)
        xh3 = h2 * r3
        g3 = g3_ref[...]
        pos = (i * tm + lax.broadcasted_iota(jnp.int32, (tm, 1), 0)) % rows_per_example
        valid = pos >= LEAD
        err = jnp.where(valid, xh3 * g3 - t_ref[...], 0.0)
        loss = 0.5 / D * jnp.sum(jnp.sum(err * err, axis=-1, keepdims=True), axis=0, keepdims=True)
        dy = err * (1.0 / D)
        dg3 = jnp.sum(dy * xh3, axis=0, keepdims=True)
        dxh = dy * g3
        dh2 = r3 * (dxh - xh3 * jnp.mean(dxh * xh3, axis=-1, keepdims=True))
        dh2b = dh2.astype(BF16)
        dh2_ref[...] = dh2b
        dn2 = jnp.zeros((tm, D), F32)
        for cs in ff_blocks:
            df = _dot(dh2b, wd_ref[cs, :], _NT)
            a = da_ref[:, cs].astype(F32)
            b = db_ref[:, cs].astype(F32)
            sg = _sigmoid(a)
            da = (df * b * sg * (1.0 + a * (1.0 - sg))).astype(BF16)
            db = (df * a * sg).astype(BF16)
            da_ref[:, cs] = da
            db_ref[:, cs] = db
            dn2 = dn2 + _dot(da, wg_ref[cs, :]) + _dot(db, wu_ref[cs, :])
        dg2 = jnp.sum(dn2 * xh2, axis=0, keepdims=True)
        dxh2 = dn2 * g2_ref[...]
        dh1 = dh2 + r2 * (dxh2 - xh2 * jnp.mean(dxh2 * xh2, axis=-1, keepdims=True))
        dh1_ref[...] = dh1
        dh1b = dh1.astype(BF16)
        dh1b_ref[...] = dh1b
        dyc_ref[...] = _dot(dh1b, wo_ref[0:C_CONV, :], _NT)
        dyg_ref[...] = _dot(dh1b, wo_ref[C_CONV:D, :], _NT)

        @pl.when(i == 0)
        def _():
            part_ref[...] = jnp.zeros_like(part_ref)

        part_ref[0:1, :] += dg3
        part_ref[1:2, :] += dg2
        part_ref[2:3, :] += jnp.broadcast_to(loss, (1, D))

    return pl.pallas_call(
        body, name="mid_rows", grid=(rows // tm,),
        in_specs=[_rows(tm, C_CONV), _rows(tm, GLA_V), _rows(tm, D), _rows(tm, D), _whole_vmem(), _whole_vmem(),
                  _whole_vmem(), _whole_vmem(), _fixed((1, D)), _fixed((1, D)), _fixed((8, 128))],
        out_specs=[_rows(tm, D), _rows(tm, D_FF), _rows(tm, D_FF), _rows(tm, D_FF), _rows(tm, D), _rows(tm, D),
                   _rows(tm, D), _rows(tm, C_CONV), _rows(tm, GLA_V), _fixed((8, D))],
        out_shape=[jax.ShapeDtypeStruct((rows, D), BF16)] + [jax.ShapeDtypeStruct((rows, D_FF), BF16)] * 3
        + [jax.ShapeDtypeStruct((rows, D), BF16), jax.ShapeDtypeStruct((rows, D), F32),
           jax.ShapeDtypeStruct((rows, D), BF16), jax.ShapeDtypeStruct((rows, C_CONV), F32),
           jax.ShapeDtypeStruct((rows, GLA_V), F32), jax.ShapeDtypeStruct((8, D), F32)],
        compiler_params=_params(("arbitrary",)),
    )(yc, yg, h0, tgt, w_out, wg, wu, wd, g2, g3, token)


def _bwd_inproj(duc, dqk, dvg, dlr, dh1, h0, w_in, g1, token, rows_per_example):
    rows = h0.shape[0]
    n_ex = rows // rows_per_example
    tm = _row_tile(rows_per_example, ROW_TILE)
    tiles_per_example = rows_per_example // tm
    n_steps = rows // tm

    def body(duc_ref, dqk_ref, dvg_ref, dlr_ref, dh1_ref, h_ref, w_ref, g_ref, token_ref, gx_ref, part_ref, dmeta_ref,
             buf_ref, sems):
        dn = (_dot(duc_ref[...], w_ref[:, 0:1024], _NT) + _dot(dqk_ref[...], w_ref[:, 1024:1536], _NT)
              + _dot(dvg_ref[...], w_ref[:, 1536:2560], _NT) + _dot(dlr_ref[...], w_ref[:, 2560:2688], _NT))
        h = h_ref[...]
        r = lax.rsqrt(jnp.mean(h * h, axis=-1, keepdims=True) + RMS_EPS)
        xh = h * r
        dg = jnp.sum(dn * xh, axis=0, keepdims=True)
        dxh = dn * g_ref[...]
        dh0 = dh1_ref[...] + r * (dxh - xh * jnp.mean(dxh * xh, axis=-1, keepdims=True))
        i = pl.program_id(0)

        def copies(step):
            slot, b, j = step % 2, step // tiles_per_example, step % tiles_per_example
            out = [(j == 0, pltpu.make_async_copy(buf_ref.at[slot, pl.ds(LEAD, tm - LEAD)],
                                                   gx_ref.at[b, pl.ds(0, tm - LEAD)], sems.at[slot]))]
            if tiles_per_example > 1:
                out.append((j != 0, pltpu.make_async_copy(
                    buf_ref.at[slot], gx_ref.at[b, pl.ds(pl.multiple_of(jnp.maximum(j * tm - LEAD, 0), 8), tm)],
                    sems.at[slot])))
            return out

        def each(step, act):
            for cond, cp in copies(step):
                pl.when(cond)(functools.partial(act, cp))

        @pl.when(i >= 2)
        def _():
            each(i - 2, lambda cp: cp.wait())

        buf_ref[i % 2] = dh0
        each(i, lambda cp: cp.start())

        @pl.when(i == n_steps - 1)
        def _():
            each(i, lambda cp: cp.wait())
            if n_steps > 1:
                each(i - 1, lambda cp: cp.wait())

        @pl.when(i == 0)
        def _():
            part_ref[...] = jnp.zeros_like(part_ref)
            dmeta_ref[...] = jnp.zeros_like(dmeta_ref)

        part_ref[0:1, :] += dg

        @pl.when(i % tiles_per_example == 0)
        def _():
            dmeta_ref[...] += dh0[ZROWS:LEAD, :]

    return pl.pallas_call(
        body, name="bwd_inproj", grid=(n_steps,),
        in_specs=[_rows(tm, 1024), _rows(tm, 512), _rows(tm, 1024), _rows(tm, RANK_P), _rows(tm, D), _rows(tm, D),
                  _whole_vmem(), _fixed((1, D)), _fixed((8, 128))],
        out_specs=[_any(), _fixed((8, D)), _fixed((N_META, D))],
        out_shape=[jax.ShapeDtypeStruct((n_ex, rows_per_example - LEAD, D), F32), jax.ShapeDtypeStruct((8, D), F32),
                   jax.ShapeDtypeStruct((N_META, D), F32)],
        scratch_shapes=[pltpu.VMEM((2, tm, D), F32), pltpu.SemaphoreType.DMA((2,))],
        compiler_params=_params(("arbitrary",)),
    )(duc, dqk, dvg, dlr, dh1, h0, w_in, g1, token)


def _dw_blocked(a, bs, width, name):
    rows, m = a.shape
    ws = [b.shape[1] for b in bs]
    assert sum(ws) >= N_DEV * width
    tk = _row_tile(rows, DW_ROW_TILE)
    nk = rows // tk

    def body(a_ref, *refs):
        b_refs, o_ref, acc_ref = refs[:len(bs)], refs[len(bs)], refs[len(bs) + 1]
        k = pl.program_id(0)

        @pl.when(k == 0)
        def _():
            acc_ref[...] = jnp.zeros_like(acc_ref)

        at = a_ref[...].T
        off = 0
        for b_ref, w in zip(b_refs, ws):
            acc_ref[:, off:off + w] += _dot(at, b_ref[...])
            off += w

        @pl.when(k == nk - 1)
        def _():
            for d in range(N_DEV):
                o_ref[d] = acc_ref[:, d * width:(d + 1) * width].astype(BF16)

    return pl.pallas_call(
        body, name=name, grid=(nk,),
        in_specs=[_rows(tk, m)] + [_rows(tk, w) for w in ws],
        out_specs=_fixed((N_DEV, m, width)),
        out_shape=jax.ShapeDtypeStruct((N_DEV, m, width), BF16),
        scratch_shapes=[pltpu.VMEM((m, sum(ws)), F32)],
        compiler_params=_params(("arbitrary",)),
    )(a, *bs)


def _matmul_tn(a, b, name):
    rows, m = a.shape
    n = b.shape[1]
    tk = _row_tile(rows, DW_ROW_TILE)
    tn = n if n <= 1024 else FF_CHUNK
    tm_ = m if m <= 1024 else FF_CHUNK
    assert n % tn == 0 and m % tm_ == 0
    nk = rows // tk

    def body(a_ref, b_ref, o_ref, acc_ref):
        k = pl.program_id(2)

        @pl.when(k == 0)
        def _():
            acc_ref[...] = jnp.zeros_like(acc_ref)

        acc_ref[...] += _dot(a_ref[...], b_ref[...], _TN)

        @pl.when(k == nk - 1)
        def _():
            o_ref[...] = acc_ref[...].astype(BF16)

    return pl.pallas_call(
        body, name=name, grid=(m // tm_, n // tn, nk),
        in_specs=[pl.BlockSpec((tk, tm_), lambda i, j, k: (k, i)), pl.BlockSpec((tk, tn), lambda i, j, k: (k, j))],
        out_specs=pl.BlockSpec((tm_, tn), lambda i, j, k: (i, j)),
        out_shape=jax.ShapeDtypeStruct((m, n), BF16),
        scratch_shapes=[pltpu.VMEM((tm_, tn), F32)],
        compiler_params=_params(("parallel", "parallel", "arbitrary")),
    )(a, b)


HALO = 32
LANES = 128


def _shifted(win, offsets):
    for r in range(8):
        js = [j for j, k in enumerate(offsets) if k % 8 == r]
        if js:
            rolled = win if r == 0 else pltpu.roll(win, CHUNK + HALO - r, 0)
            for j in js:
                yield j, rolled[offsets[j] - r:offsets[j] - r + CHUNK]


def _glu_into(uc_ref, vs_ref, n_chunk):
    vs_ref[0:CHUNK, :] = jnp.zeros((CHUNK, C_CONV), F32)

    def glu(i, carry):
        base = pl.multiple_of(i * CHUNK, CHUNK)
        val = uc_ref[pl.ds(base, CHUNK), 0:C_CONV]
        gate = uc_ref[pl.ds(base, CHUNK), C_CONV:2 * C_CONV]
        vs_ref[pl.ds(base + CHUNK, CHUNK), :] = val * _sigmoid(gate)
        return carry

    lax.fori_loop(0, n_chunk, glu, 0)


def _fwd_conv(uc, conv_w, conv_b, ln_g, ln_b, token, n_ex):
    rows = uc.shape[0]
    lp = rows // n_ex
    n_chunk = lp // CHUNK

    def body(uc_ref, w_ref, b_ref, lg_ref, lb_ref, token_ref, ypre_ref, yc_ref, vs_ref):
        _glu_into(uc_ref, vs_ref, n_chunk)

        def conv(i, carry):
            base = pl.multiple_of(i * CHUNK, CHUNK)
            for lb in range(C_CONV // LANES):
                ls = slice(lb * LANES, (lb + 1) * LANES)
                win = vs_ref[pl.ds(base + CHUNK - HALO, CHUNK + HALO), ls]
                acc = jnp.broadcast_to(b_ref[:, ls], (CHUNK, LANES))
                for j, rows_j in _shifted(win, [HALO - (CONV_W - 1) + j for j in range(CONV_W)]):
                    acc = acc + w_ref[j:j + 1, ls] * rows_j
                ypre_ref[pl.ds(base, CHUNK), ls] = acc
            y = ypre_ref[pl.ds(base, CHUNK), :]
            mu = jnp.mean(y, axis=-1, keepdims=True)
            yc_ = y - mu
            rstd = lax.rsqrt(jnp.mean(yc_ * yc_, axis=-1, keepdims=True) + LN_EPS)
            s = yc_ * rstd * lg_ref[...] + lb_ref[...]
            yc_ref[pl.ds(base, CHUNK), :] = (s * _sigmoid(s)).astype(BF16)
            return carry

        lax.fori_loop(0, n_chunk, conv, 0, unroll=3)

    ex = lambda w: pl.BlockSpec((lp, w), lambda b: (b, 0))
    return pl.pallas_call(
        body, name="fwd_conv", grid=(n_ex,),
        in_specs=[ex(2 * C_CONV), _fixed((32, C_CONV)), _fixed((1, C_CONV)), _fixed((1, C_CONV)), _fixed((1, C_CONV)),
                  _fixed((8, 128))],
        out_specs=[ex(C_CONV), ex(C_CONV)],
        out_shape=[jax.ShapeDtypeStruct((rows, C_CONV), F32), jax.ShapeDtypeStruct((rows, C_CONV), BF16)],
        scratch_shapes=[pltpu.VMEM((lp + CHUNK, C_CONV), F32)],
        compiler_params=_params(("parallel",)),
    )(uc, conv_w, conv_b, ln_g, ln_b, token)


def _bwd_conv(uc, ypre, dyc, conv_w, ln_g, ln_b, token, n_ex):
    rows = uc.shape[0]
    lp = rows // n_ex
    n_chunk = lp // CHUNK

    def body(uc_ref, ypre_ref, dyc_ref, w_ref, lg_ref, lb_ref, token_ref, duc_ref, dw_ref, dvec_ref, vs_ref, dys_ref,
             dwacc_ref):
        _glu_into(uc_ref, vs_ref, n_chunk)
        dys_ref[pl.ds(lp, CHUNK), :] = jnp.zeros((CHUNK, C_CONV), F32)
        dwacc_ref[...] = jnp.zeros_like(dwacc_ref)

        def ln_bwd(i, carry):
            dcb, dlg, dlb = carry
            base = pl.multiple_of(i * CHUNK, CHUNK)
            y = ypre_ref[pl.ds(base, CHUNK), :]
            mu = jnp.mean(y, axis=-1, keepdims=True)
            yc_ = y - mu
            rstd = lax.rsqrt(jnp.mean(yc_ * yc_, axis=-1, keepdims=True) + LN_EPS)
            xh = yc_ * rstd
            s = xh * lg_ref[...] + lb_ref[...]
            sg = _sigmoid(s)
            ds = dyc_ref[pl.ds(base, CHUNK), :] * (sg * (1.0 + s * (1.0 - sg)))
            dxh = ds * lg_ref[...]
            dy = rstd * (dxh - jnp.mean(dxh, axis=-1, keepdims=True) - xh * jnp.mean(dxh * xh, axis=-1, keepdims=True))
            dys_ref[pl.ds(base, CHUNK), :] = dy
            return (dcb + jnp.sum(dy, axis=0, keepdims=True), dlg + jnp.sum(ds * xh, axis=0, keepdims=True),
                    dlb + jnp.sum(ds, axis=0, keepdims=True))

        zero = jnp.zeros((1, C_CONV), F32)
        dcb, dlg, dlb = lax.fori_loop(0, n_chunk, ln_bwd, (zero, zero, zero))

        @pl.when(pl.program_id(0) == 0)
        def _():
            dvec_ref[...] = jnp.zeros_like(dvec_ref)
            dw_ref[...] = jnp.zeros_like(dw_ref)

        dvec_ref[0:1, :] += dcb
        dvec_ref[1:2, :] += dlg
        dvec_ref[2:3, :] += dlb

        def taps(i, carry):
            base = pl.multiple_of(i * CHUNK, CHUNK)
            for lb in range(C_CONV // LANES):
                ls = slice(lb * LANES, (lb + 1) * LANES)
                dwin = dys_ref[pl.ds(base, CHUNK + HALO), ls]
                vwin = vs_ref[pl.ds(base + CHUNK - HALO, CHUNK + HALO), ls]
                dy = dwin[0:CHUNK]
                acc = jnp.zeros((CHUNK, LANES), F32)
                for j, rows_j in _shifted(dwin, [CONV_W - 1 - j for j in range(CONV_W)]):
                    acc = acc + w_ref[j:j + 1, ls] * rows_j
                for j, rows_j in _shifted(vwin, [HALO - (CONV_W - 1) + j for j in range(CONV_W)]):
                    dwacc_ref[8 * j:8 * j + 8, ls] += jnp.sum((dy * rows_j).reshape(CHUNK // 8, 8, LANES), axis=0)
                val = uc_ref[pl.ds(base, CHUNK), ls]
                gate = uc_ref[pl.ds(base, CHUNK), C_CONV + lb * LANES:C_CONV + (lb + 1) * LANES]
                sg = _sigmoid(gate)
                duc_ref[pl.ds(base, CHUNK), ls] = (acc * sg).astype(BF16)
                duc_ref[pl.ds(base, CHUNK), C_CONV + lb * LANES:C_CONV + (lb + 1) * LANES] = (
                    acc * val * sg * (1.0 - sg)).astype(BF16)
            return carry

        lax.fori_loop(0, n_chunk, taps, 0, unroll=3)
        for j in range(CONV_W):
            dw_ref[j:j + 1, :] += jnp.sum(dwacc_ref[8 * j:8 * j + 8, :], axis=0, keepdims=True)

    ex = lambda w: pl.BlockSpec((lp, w), lambda b: (b, 0))
    return pl.pallas_call(
        body, name="bwd_conv", grid=(n_ex,),
        in_specs=[ex(2 * C_CONV), ex(C_CONV), ex(C_CONV), _fixed((32, C_CONV)), _fixed((1, C_CONV)), _fixed((1, C_CONV)),
                  _fixed((8, 128))],
        out_specs=[ex(2 * C_CONV), _fixed((32, C_CONV)), _fixed((8, C_CONV))],
        out_shape=[jax.ShapeDtypeStruct((rows, 2 * C_CONV), BF16), jax.ShapeDtypeStruct((32, C_CONV), F32),
                   jax.ShapeDtypeStruct((8, C_CONV), F32)],
        scratch_shapes=[pltpu.VMEM((lp + CHUNK, C_CONV), F32), pltpu.VMEM((lp + CHUNK, C_CONV), F32),
                        pltpu.VMEM((8 * 32, C_CONV), F32)],
        compiler_params=_params(("arbitrary",)),
    )(uc, ypre, dyc, conv_w, ln_g, ln_b, token)


def _seg_chunks(n_chunk):
    return max(c for c in (11, 3, 1) if n_chunk % c == 0)


def _block_mask(shape, row_block, lane_block):
    return (lax.broadcasted_iota(jnp.int32, shape, 0) // row_block) == (lax.broadcasted_iota(jnp.int32, shape, 1) // lane_block)


def _per_head_rows(x, mask):
    return jnp.where(mask, jnp.concatenate([x] * GLA_H, axis=0), 0)


def _fold_heads(full, lane_block):
    lane = lax.broadcasted_iota(jnp.int32, (1, full.shape[1]), 1) // lane_block
    out = jnp.where(lane == 0, full[0:CHUNK], 0.0)
    for h in range(1, GLA_H):
        out = out + jnp.where(lane == h, full[h * CHUNK:(h + 1) * CHUNK], 0.0)
    return out


def _causal_heads():
    return (lax.broadcasted_iota(jnp.int32, (CHUNK, GLA_H * CHUNK), 1) % CHUNK) <= lax.broadcasted_iota(
        jnp.int32, (CHUNK, GLA_H * CHUNK), 0)


def _cumsum_rows(x):
    row = lax.broadcasted_iota(jnp.int32, x.shape, 0)
    s = 1
    while s < CHUNK:
        x = x + jnp.where(row >= s, pltpu.roll(x, s, 0), 0.0)
        s *= 2
    return x


def _rev_cumsum_rows(x):
    row = lax.broadcasted_iota(jnp.int32, x.shape, 0)
    s = 1
    while s < CHUNK:
        x = x + jnp.where(row < CHUNK - s, pltpu.roll(x, CHUNK - s, 0), 0.0)
        s *= 2
    return x


def _gate_terms(lr_ref, w2_ref, gb_ref, rs, first_pos):
    z = _dot(lr_ref[rs, :].astype(BF16), w2_ref[...]) + gb_ref[...]
    la = (jnp.minimum(z, 0.0) - jnp.log(1.0 + jnp.exp(-jnp.abs(z)))) * (1.0 / TAU)
    pos = first_pos + lax.broadcasted_iota(jnp.int32, (CHUNK, 1), 0)
    live = pos >= ZROWS
    la = jnp.where(live, la, 0.0)
    return z, live, _cumsum_rows(la)


def _fwd_gla(qk, vg, lr, w2p, gb, ng, token, n_ex):
    rows = qk.shape[0]
    lp = rows // n_ex
    n_chunk = lp // CHUNK
    sc = _seg_chunks(n_chunk)
    n_seg = n_chunk // sc
    seg = sc * CHUNK

    def body(qk_ref, vg_ref, lr_ref, w2_ref, gb_ref, ng_ref, token_ref, yg_ref, o_ref, st_ref, state_ref):
        sidx = pl.program_id(1)

        @pl.when(sidx == 0)
        def _():
            state_ref[...] = jnp.zeros_like(state_ref)

        causal = _causal_heads()
        k_mask = _block_mask((GLA_H * CHUNK, GLA_K), CHUNK, GLA_DK)
        v_mask = _block_mask((GLA_H * CHUNK, GLA_V), CHUNK, GLA_DV)
        s_mask = _block_mask((GLA_V, GLA_K), GLA_DV, GLA_DK)

        def chunk(ci, carry):
            base = pl.multiple_of(ci * CHUNK, CHUNK)
            rs = pl.ds(base, CHUNK)
            _, _, bcum = _gate_terms(lr_ref, w2_ref, gb_ref, rs, (sidx * sc + ci) * CHUNK)
            bl = bcum[CHUNK - 1:CHUNK, :]
            q = qk_ref[rs, 0:GLA_K]
            k = qk_ref[rs, GLA_K:2 * GLA_K]
            qt = (q * (GLA_DK ** -0.5) * jnp.exp(bcum)).astype(BF16)
            kt = (k * jnp.exp(-bcum)).astype(BF16)
            kh = (k * jnp.exp(bl - bcum)).astype(BF16)
            vb = vg_ref[rs, 0:GLA_V].astype(BF16)
            state = state_ref[...]
            st_ref[ci] = state
            a = jnp.where(causal, _dot(qt, _per_head_rows(kt, k_mask), _NT), 0.0)
            o = _dot(a.astype(BF16), _per_head_rows(vb, v_mask)) + _dot(qt, state.astype(BF16), _NT)
            o_ref[rs, :] = o
            for h in range(GLA_H):
                hs = slice(h * GLA_DV, (h + 1) * GLA_DV)
                oh = o[:, hs]
                ro = lax.rsqrt(jnp.mean(oh * oh, axis=-1, keepdims=True) + RMS_EPS)
                g = vg_ref[rs, GLA_V + h * GLA_DV:GLA_V + (h + 1) * GLA_DV]
                yg_ref[rs, hs] = (oh * ro * ng_ref[...] * g * _sigmoid(g)).astype(BF16)
            state_ref[...] = state * jnp.exp(bl) + jnp.where(s_mask, _dot(vb, kh, _TN), 0.0)
            return carry

        lax.fori_loop(0, sc, chunk, 0, unroll=True)

    sg = lambda w: pl.BlockSpec((seg, w), lambda b, s: (b * n_seg + s, 0))
    return pl.pallas_call(
        body, name="fwd_gla", grid=(n_ex, n_seg),
        in_specs=[sg(2 * GLA_K), sg(2 * GLA_V), sg(RANK_P), _fixed((RANK_P, GLA_K)), _fixed((1, GLA_K)), _fixed((1, GLA_DV)),
                  _fixed((8, 128))],
        out_specs=[sg(GLA_V), sg(GLA_V), pl.BlockSpec((sc, GLA_V, GLA_K), lambda b, s: (b * n_seg + s, 0, 0))],
        out_shape=[jax.ShapeDtypeStruct((rows, GLA_V), BF16), jax.ShapeDtypeStruct((rows, GLA_V), F32),
                   jax.ShapeDtypeStruct((n_ex * n_chunk, GLA_V, GLA_K), F32)],
        scratch_shapes=[pltpu.VMEM((GLA_V, GLA_K), F32)],
        compiler_params=_params(("parallel", "arbitrary")),
    )(qk, vg, lr, w2p, gb, ng, token)


def _bwd_gla(qk, vg, lr, o, st, dyg, w2p, gb, ng, yc, yg, dh1b, token, n_ex):
    rows = qk.shape[0]
    lp = rows // n_ex
    n_chunk = lp // CHUNK
    sc = _seg_chunks(n_chunk)
    n_seg = n_chunk // sc
    seg = sc * CHUNK

    def body(qk_ref, vg_ref, lr_ref, o_ref, st_ref, dyg_ref, w2_ref, gb_ref, ng_ref, yc_ref, yg_ref, dh1_ref, token_ref,
             dqk_ref, dvg_ref, dlr_ref, dw2_ref, dvec_ref, dwo_ref, gt_ref, dz_ref, dwo_acc):
        step = pl.program_id(1)
        sidx = n_seg - 1 - step
        first = (step == 0) & (pl.program_id(0) == 0)

        @pl.when(step == 0)
        def _():
            gt_ref[...] = jnp.zeros_like(gt_ref)

        @pl.when(first)
        def _():
            dw2_ref[...] = jnp.zeros_like(dw2_ref)
            dvec_ref[...] = jnp.zeros_like(dvec_ref)
            dwo_acc[...] = jnp.zeros_like(dwo_acc)

        d1 = dh1_ref[...]
        dwo_acc[0:C_CONV, :] += _dot(yc_ref[...], d1, _TN)
        dwo_acc[C_CONV:D, :] += _dot(yg_ref[...], d1, _TN)

        @pl.when((step == n_seg - 1) & (pl.program_id(0) == n_ex - 1))
        def _():
            dwo_ref[...] = dwo_acc[...].astype(BF16)

        causal = _causal_heads()
        k_mask = _block_mask((GLA_H * CHUNK, GLA_K), CHUNK, GLA_DK)
        v_mask = _block_mask((GLA_H * CHUNK, GLA_V), CHUNK, GLA_DV)
        s_mask = _block_mask((GLA_V, GLA_K), GLA_DV, GLA_DK)
        last_row = lax.broadcasted_iota(jnp.int32, (CHUNK, 1), 0) == CHUNK - 1
        ng = ng_ref[...]

        def chunk(ii, dng):
            ci = sc - 1 - ii
            base = pl.multiple_of(ci * CHUNK, CHUNK)
            rs = pl.ds(base, CHUNK)
            z, live, bcum = _gate_terms(lr_ref, w2_ref, gb_ref, rs, (sidx * sc + ci) * CHUNK)
            bl = bcum[CHUNK - 1:CHUNK, :]
            ebl = jnp.exp(bl)
            q = qk_ref[rs, 0:GLA_K]
            k = qk_ref[rs, GLA_K:2 * GLA_K]
            eb = jnp.exp(bcum)
            enb = jnp.exp(-bcum)
            ehb = jnp.exp(bl - bcum)
            qt = q * (GLA_DK ** -0.5) * eb
            kt = k * enb
            kh = k * ehb
            qtb = qt.astype(BF16)
            vb = vg_ref[rs, 0:GLA_V].astype(BF16)
            k_rows = _per_head_rows(kt.astype(BF16), k_mask)
            v_rows = _per_head_rows(vb, v_mask)
            gt = gt_ref[...]
            gtb = gt.astype(BF16)
            s_in = st_ref[ci]
            dos = []
            for h in range(GLA_H):
                hs = slice(h * GLA_DV, (h + 1) * GLA_DV)
                gs = slice(GLA_V + h * GLA_DV, GLA_V + (h + 1) * GLA_DV)
                oh = o_ref[rs, hs]
                ro = lax.rsqrt(jnp.mean(oh * oh, axis=-1, keepdims=True) + RMS_EPS)
                on = oh * ro
                g = vg_ref[rs, gs]
                sg = _sigmoid(g)
                dout = dyg_ref[rs, hs]
                dvg_ref[rs, gs] = (dout * on * ng * (sg * (1.0 + g * (1.0 - sg)))).astype(BF16)
                dw = dout * g * sg
                dng = dng + jnp.sum(dw * on, axis=0, keepdims=True)
                don = dw * ng
                dos.append((ro * (don - on * jnp.mean(don * on, axis=-1, keepdims=True))).astype(BF16))
            dob = jnp.concatenate(dos, axis=1)
            a = jnp.where(causal, _dot(qtb, k_rows, _NT), 0.0).astype(BF16)
            da = jnp.where(causal, _dot(dob, v_rows, _NT), 0.0).astype(BF16)
            dv = _fold_heads(_dot(a, dob, _TN), GLA_DV) + _dot(kh.astype(BF16), gtb, _NT)
            dvg_ref[rs, 0:GLA_V] = dv.astype(BF16)
            dkh = _dot(vb, gtb)
            dqt = _dot(da, k_rows) + _dot(dob, s_in.astype(BF16))
            dkt = _fold_heads(_dot(da, qtb, _TN), GLA_DK)
            dbl = jnp.sum(gt * s_in, axis=0, keepdims=True) * ebl + jnp.sum(dkh * kh, axis=0, keepdims=True)
            dqk_ref[rs, 0:GLA_K] = (dqt * (GLA_DK ** -0.5) * eb).astype(BF16)
            dqk_ref[rs, GLA_K:2 * GLA_K] = (dkt * enb + dkh * ehb).astype(BF16)
            db = dqt * qt - dkt * kt - dkh * kh
            db = jnp.where(last_row, db + dbl, db)
            dla = jnp.where(live, _rev_cumsum_rows(db), 0.0)
            dz_ref[rs, :] = dla * (1.0 / TAU) * (1.0 - _sigmoid(z))
            gt_ref[...] = jnp.where(s_mask, _dot(dob, qtb, _TN), 0.0) + gt * ebl
            return dng

        dng = lax.fori_loop(0, sc, chunk, jnp.zeros((1, GLA_DV), F32), unroll=True)
        dz = dz_ref[...]
        dzb = dz.astype(BF16)
        dlr_ref[...] = _dot(dzb, w2_ref[...], _NT).astype(BF16)
        dw2_ref[...] += _dot(lr_ref[...].astype(BF16), dzb, _TN)
        dvec_ref[0:1, :] += jnp.sum(dz, axis=0, keepdims=True)
        dvec_ref[1:2, 0:GLA_DV] += dng

    sg_ = lambda w: pl.BlockSpec((seg, w), lambda b, s: (b * n_seg + n_seg - 1 - s, 0))
    return pl.pallas_call(
        body, name="bwd_gla", grid=(n_ex, n_seg),
        in_specs=[sg_(2 * GLA_K), sg_(2 * GLA_V), sg_(RANK_P), sg_(GLA_V),
                  pl.BlockSpec((sc, GLA_V, GLA_K), lambda b, s: (b * n_seg + n_seg - 1 - s, 0, 0)), sg_(GLA_V),
                  _fixed((RANK_P, GLA_K)), _fixed((1, GLA_K)), _fixed((1, GLA_DV)), sg_(C_CONV), sg_(GLA_V), sg_(D),
                  _fixed((8, 128))],
        out_specs=[sg_(2 * GLA_K), sg_(2 * GLA_V), sg_(RANK_P), _fixed((RANK_P, GLA_K)), _fixed((8, GLA_K)),
                   _fixed((D, D))],
        out_shape=[jax.ShapeDtypeStruct((rows, 2 * GLA_K), BF16), jax.ShapeDtypeStruct((rows, 2 * GLA_V), BF16),
                   jax.ShapeDtypeStruct((rows, RANK_P), BF16), jax.ShapeDtypeStruct((RANK_P, GLA_K), F32),
                   jax.ShapeDtypeStruct((8, GLA_K), F32), jax.ShapeDtypeStruct((D, D), BF16)],
        scratch_shapes=[pltpu.VMEM((GLA_V, GLA_K), F32), pltpu.VMEM((seg, GLA_K), F32), pltpu.VMEM((D, D), F32)],
        compiler_params=_params(("arbitrary", "arbitrary")),
    )(qk, vg, lr, o, st, dyg, w2p, gb, ng, yc, yg, dh1b, token)


def _pad_rows(x, tgt):
    return jnp.pad(x, ((0, 0), (LEAD, 0), (0, 0))), jnp.pad(tgt, ((0, 0), (LEAD, 0), (0, 0)))


def _local_step(h0, tgt_p, p, pass_on, late_weights, send_early):
    n_ex, lp, _ = h0.shape
    rows = n_ex * lp
    meta = jnp.broadcast_to(p["meta"][None], (n_ex, N_META, D))
    h0 = lax.dynamic_update_slice(h0, meta, (0, ZROWS, 0)).reshape(rows, D)
    tgt_p = tgt_p.reshape(rows, D)

    uc, qk, vg, lr, n1 = _fwd_inproj(h0, p["g1"], p["w_in"])
    ypre, yc = _fwd_conv(uc, p["conv_w"], p["conv_b"], p["ln_g"], p["ln_b"], p["token"], n_ex)
    token = pass_on(yc)
    yg, o, st = _fwd_gla(qk, vg, lr, p["w2"], p["gb"], p["ng"], token, n_ex)
    w_out, wg, wu, wd = late_weights(yg)
    n2, f, da, db, dh2, dh1, dh1b, dyc, dyg, part = _mid_rows(
        yc, yg, h0, tgt_p, w_out, wg, wu, wd, p["g2"], p["g3"], token, lp)
    g = {}
    token = send_early("ffn", [_matmul_tn(a_, b_, name).reshape(N_DEV, FF_S, D) for a_, b_, name in (
        (da, n2, "dw_gate"), (db, n2, "dw_up"), (f, dh2, "dw_down"))])
    dqk, dvg, dlr, g["w2"], g["gla_vec"], dw_out = _bwd_gla(
        qk, vg, lr, o, st, dyg, p["w2"], p["gb"], p["ng"], yc, yg, dh1b, token, n_ex)
    token = send_early("out", [dw_out.reshape(N_DEV, W_OUT_S, D)])
    duc, g["conv_w"], g["conv_vec"] = _bwd_conv(uc, ypre, dyc, p["conv_w"], p["ln_g"], p["ln_b"], token, n_ex)
    token = send_early("in", [_dw_blocked(n1, [duc, dqk, dvg, dlr], W_IN_S, "dw_in")])
    grad_x, g["in_vec"], g["meta"] = _bwd_inproj(duc, dqk, dvg, dlr, dh1, h0, p["w_in"], p["g1"], token, lp)
    g["ffn_vec"] = part
    return grad_x, g


W_IN_S = D_IN // N_DEV
W_OUT_S = D // N_DEV
FF_S = D_FF // N_DEV
CONV_S = C_CONV // N_DEV
GATE_S = GLA_K // N_DEV
SMALL_PACK = 64
CONV_ROW = 16
GATE_ROW = 48
VEC_ROWS = 16
_VEC_ROWS = (("norm_mix_g", D), ("conv_b", C_CONV), ("conv_ln_g", C_CONV), ("conv_ln_b", C_CONV), ("gla_gate_b", GLA_K),
             ("gla_norm_g", GLA_DV), ("norm_ffn_g", D), ("norm_final_g", D))
LOSS_ROW = len(_VEC_ROWS)


def _position():
    return lax.axis_index("x"), lax.axis_index("y"), lax.axis_index("c")


def _any():
    return pl.BlockSpec(memory_space=pl.ANY)


def _stage(mats, meta, conv_w, w2):
    n_t = len(mats) + 1

    def body(*refs):
        ins = refs[0:n_t - 1]
        meta_ref, cw_ref, w2_ref = refs[n_t - 1:n_t + 2]
        lands = refs[n_t + 2:2 * n_t + 2]
        shards = refs[2 * n_t + 2:3 * n_t + 2]
        sems = refs[3 * n_t + 2]
        for s_ref, w_ref in zip(shards, ins):
            s_ref[...] = w_ref[...].astype(BF16)
        sp = shards[n_t - 1]
        sp[...] = jnp.zeros_like(sp)
        sp[0:N_META, :] = meta_ref[...]
        sp[CONV_ROW:CONV_ROW + CONV_W, 0:CONV_S] = cw_ref[...]
        sp[GATE_ROW:GATE_ROW + RANK, 0:GATE_S] = w2_ref[...]
        x, y, c = _position()
        mine = [pltpu.make_async_copy(shards[t], lands[t].at[4 * x + 2 * y + c], sems.at[t]) for t in range(n_t)]
        for cp in mine:
            cp.start()
        for cp in mine:
            cp.wait()

    shard_shapes = [jax.ShapeDtypeStruct(m.shape, BF16) for m in mats] + [jax.ShapeDtypeStruct((SMALL_PACK, 128), F32)]
    res = pl.pallas_call(
        body, name="stage",
        out_shape=[jax.ShapeDtypeStruct((N_DEV,) + s.shape, s.dtype) for s in shard_shapes] + shard_shapes,
        in_specs=[_whole_vmem()] * (n_t + 2), out_specs=[_any()] * n_t + [_whole_vmem()] * n_t,
        scratch_shapes=[pltpu.SemaphoreType.DMA((n_t,))],
        compiler_params=pltpu.CompilerParams(vmem_limit_bytes=VMEM_LIMIT),
    )(*mats, meta, conv_w, w2)
    return res[0:n_t], res[n_t:]


_HBM = pl.BlockSpec(memory_space=pltpu.HBM)
_SEM = pl.BlockSpec(memory_space=pltpu.SEMAPHORE)
_EFFECT = pltpu.SideEffectType.DATAFLOW_SIDE_EFFECTING


_N_ROUTES = {"scatter": 7, "first": 4, "forward": 3}


def _routes(mode):
    x, y, c = _position()
    me = 4 * x + 2 * y + c
    if mode == "scatter":
        out = []
        for k in range(1, N_DEV):
            px = 1 - x if k & 4 else x
            py = 1 - y if k & 2 else y
            pc = 1 - c if k & 1 else c
            out.append(((px, py, pc), 4 * px + 2 * py + pc, me))
        return out
    if mode == "first":
        return [(pos, None, me) for pos in ((x, y, 1 - c), (1 - x, y, c), (x, 1 - y, c), (1 - x, 1 - y, c))]
    assert mode == "forward"
    return [((x, y, 1 - c), 4 * px + 2 * py + c, 4 * px + 2 * py + c) for px, py in ((1 - x, y), (x, 1 - y), (1 - x, 1 - y))]


def _route_copies(mode, n, src_refs, land_refs, send_sems, recv_sems):
    nr = _N_ROUTES[mode]
    for i, (pos, src_blk, dst_blk) in enumerate(_routes(mode)):
        for t in range(n):
            src = land_refs[t] if mode == "forward" else src_refs[t]
            yield pltpu.make_async_remote_copy(
                src_ref=src if src_blk is None else src.at[src_blk], dst_ref=land_refs[t].at[dst_blk],
                send_sem=send_sems.at[nr * t + i], recv_sem=recv_sems.at[nr * t + i], device_id=pos, device_id_type=MESH)


def _in_hbm(a):
    return pltpu.with_memory_space_constraint(a, pltpu.HBM)


def _send_start(name, srcs, lands, mode, after):
    n, ns = len(lands), len(srcs)
    nsem = _N_ROUTES[mode] * n

    def body(*refs):
        src_refs, land_refs = refs[0:ns], refs[ns:ns + n]
        send_sems, recv_sems = refs[ns + n + 1:ns + n + 3]
        token = refs[2 * (ns + n) + 3]
        for cp in _route_copies(mode, n, src_refs, land_refs, send_sems, recv_sems):
            cp.start()
        token[...] = jnp.zeros_like(token)

    bufs = list(srcs) + list(lands)
    res = pl.pallas_call(
        body, name=name,
        out_shape=(pltpu.SemaphoreType.DMA((nsem,)), pltpu.SemaphoreType.DMA((nsem,)),
                   *[pltpu.HBM(b.shape, b.dtype) for b in bufs], jax.ShapeDtypeStruct((8, 128), F32)),
        in_specs=[_HBM] * len(bufs) + [_any()], out_specs=(_SEM, _SEM, *[_HBM] * len(bufs), _whole_vmem()),
        input_output_aliases={i: 2 + i for i in range(len(bufs))},
        compiler_params=pltpu.CompilerParams(has_side_effects=_EFFECT),
    )(*[_in_hbm(b) for b in bufs], after)
    return res[0], res[1], res[2:2 + ns], res[2 + ns:2 + ns + n], res[2 + ns + n]


def _send_wait(name, send_sems, recv_sems, srcs, lands, mode, after):
    n, ns = len(lands), len(srcs)
    after = after if isinstance(after, tuple) else (after,)

    def body(*refs):
        src_refs, land_refs = refs[0:ns], refs[ns:ns + n]
        send_sems, recv_sems = refs[ns + n:ns + n + 2]
        for cp in _route_copies(mode, n, src_refs, land_refs, send_sems, recv_sems):
            cp.wait_send()
            cp.wait_recv()

    bufs = list(srcs) + list(lands)
    res = pl.pallas_call(
        body, name=name,
        out_shape=tuple(pltpu.HBM(b.shape, b.dtype) for b in bufs),
        in_specs=[_HBM] * len(bufs) + [_SEM, _SEM] + [_any()] * len(after), out_specs=tuple([_HBM] * len(bufs)),
        input_output_aliases={i: i for i in range(len(bufs))},
        compiler_params=pltpu.CompilerParams(has_side_effects=_EFFECT),
    )(*bufs, send_sems, recv_sems, *after)
    return res[0:ns], res[ns:ns + n]


def _unshard_in(a_in, a_small, token):
    def body(a_ref, s_ref, token_ref, w_ref, meta_ref, cw_ref, w2_ref):
        w_ref[:, D_IN:D_INP] = jnp.zeros((D, D_INP - D_IN), BF16)
        w2_ref[...] = jnp.zeros_like(w2_ref)
        for d in range(N_DEV):
            w_ref[:, d * W_IN_S:(d + 1) * W_IN_S] = a_ref[d]
            meta_ref[:, d * 128:(d + 1) * 128] = s_ref[d, 0:N_META, :]
            cw_ref[:, d * CONV_S:(d + 1) * CONV_S] = s_ref[d, CONV_ROW:CONV_ROW + 32, 0:CONV_S]
            w2_ref[0:RANK, d * GATE_S:(d + 1) * GATE_S] = s_ref[d, GATE_ROW:GATE_ROW + RANK, 0:GATE_S].astype(BF16)

    return pl.pallas_call(
        body, name="unshard_in",
        out_shape=[jax.ShapeDtypeStruct((D, D_INP), BF16), jax.ShapeDtypeStruct((N_META, D), F32),
                   jax.ShapeDtypeStruct((32, C_CONV), F32), jax.ShapeDtypeStruct((RANK_P, GLA_K), BF16)],
        compiler_params=pltpu.CompilerParams(vmem_limit_bytes=VMEM_LIMIT),
    )(a_in, a_small, token)


def _pack_small(g):
    def body(meta_ref, cw_ref, w2_ref, in_vec, ffn_vec, conv_vec, gla_vec, sp, vp):
        sp[...] = jnp.zeros_like(sp)
        vp[...] = jnp.zeros_like(vp)
        for d in range(N_DEV):
            sp[d, 0:N_META, :] = meta_ref[:, d * 128:(d + 1) * 128]
            sp[d, CONV_ROW:CONV_ROW + 32, 0:CONV_S] = cw_ref[:, d * CONV_S:(d + 1) * CONV_S]
            sp[d, GATE_ROW:GATE_ROW + RANK, 0:GATE_S] = w2_ref[0:RANK, d * GATE_S:(d + 1) * GATE_S]
            vp[d, 0:1, :] = in_vec[0:1, :]
            vp[d, 1:4, 0:C_CONV] = conv_vec[0:3, :]
            vp[d, 4:5, 0:GLA_K] = gla_vec[0:1, :]
            vp[d, 5:6, 0:GLA_DV] = gla_vec[1:2, 0:GLA_DV]
            vp[d, 6:7, :] = ffn_vec[1:2, :]
            vp[d, 7:8, :] = ffn_vec[0:1, :]
            vp[d, LOSS_ROW:LOSS_ROW + 1, :] = ffn_vec[2:3, :]

    return pl.pallas_call(
        body, name="pack_small",
        out_shape=[jax.ShapeDtypeStruct((N_DEV, SMALL_PACK, 128), F32), jax.ShapeDtypeStruct((N_DEV, VEC_ROWS, D), F32)],
    )(g["meta"], g["conv_w"], g["w2"], g["in_vec"], g["ffn_vec"], g["conv_vec"], g["gla_vec"])


def _adamw(w, g, m, v):
    m = ADAM_B1 * m + (1.0 - ADAM_B1) * g
    v = ADAM_B2 * v + (1.0 - ADAM_B2) * (g * g)
    m_hat = m / (1.0 - ADAM_B1 ** ADAM_STEP)
    v_hat = v / (1.0 - ADAM_B2 ** ADAM_STEP)
    return -ADAM_LR * (m_hat / (jnp.sqrt(v_hat) + ADAM_EPS) + ADAM_WD * w), m, v


def _update_matrix(recv, own, me, w, m, v, name):
    _, r, c = recv.shape
    tr = _row_tile(r, 256)

    def body(me_ref, recv_ref, own_ref, w_ref, m_ref, v_ref, g_ref, d_ref, nm_ref, nv_ref):
        g = jnp.zeros((tr, c), F32)
        for s in range(N_DEV):
            g = g + jnp.where(me_ref[0] == s, own_ref[...], recv_ref[s]).astype(F32)
        g_ref[...] = g
        d_ref[...], nm_ref[...], nv_ref[...] = _adamw(w_ref[...], g, m_ref[...], v_ref[...])

    one = pl.BlockSpec((None, tr, c), lambda i, me_ref: (0, i, 0))
    return pl.pallas_call(
        body, name=name,
        grid_spec=pltpu.PrefetchScalarGridSpec(
            num_scalar_prefetch=1, grid=(r // tr,),
            in_specs=[pl.BlockSpec((N_DEV, tr, c), lambda i, me_ref: (0, i, 0)),
                      pl.BlockSpec((None, tr, c), lambda i, me_ref: (me_ref[0], i, 0)), one, one, one],
            out_specs=[one] * 4),
        out_shape=[jax.ShapeDtypeStruct((1, r, c), F32)] * 4,
        compiler_params=_params(("parallel",)),
    )(me, recv, own, w, m, v)


_SMALL = ("meta_tokens", "conv_w", "gla_w_gate2") + tuple(n for n, _ in _VEC_ROWS)


def _update_small(me, srecv, vrecv, sown, vown, w, m, v):
    n = len(_SMALL)

    def body(*refs):
        me_ref, s_ref, v_ref, so_ref, vo_ref = refs[0:5]
        w_refs, m_refs, v_refs = refs[5:5 + n], refs[5 + n:5 + 2 * n], refs[5 + 2 * n:5 + 3 * n]
        outs = refs[5 + 3 * n:]
        ssum = jnp.zeros((SMALL_PACK, 128), F32)
        vsum = jnp.zeros((VEC_ROWS, D), F32)
        for s in range(N_DEV):
            ssum = ssum + jnp.where(me_ref[0] == s, so_ref[s], s_ref[s])
            vsum = vsum + jnp.where(me_ref[0] == s, vo_ref[s], v_ref[s])
        grads = [ssum[0:N_META, :], ssum[CONV_ROW:CONV_ROW + CONV_W, 0:CONV_S], ssum[GATE_ROW:GATE_ROW + RANK, 0:GATE_S]]
        grads += [vsum[i:i + 1, 0:width] for i, (_, width) in enumerate(_VEC_ROWS)]
        for i, g in enumerate(grads):
            d, nm, nv = _adamw(w_refs[i][...], g, m_refs[i][...], v_refs[i][...])
            outs[i][...] = g
            outs[n + i][...] = d
            outs[2 * n + i][...] = nm
            outs[3 * n + i][...] = nv
        outs[4 * n][...] = vsum[LOSS_ROW:LOSS_ROW + 1, 0:128]

    shapes = [jax.ShapeDtypeStruct(t.shape, F32) for t in w]
    res = pl.pallas_call(
        body, name="update_small", out_shape=shapes * 4 + [jax.ShapeDtypeStruct((1, 128), F32)],
        in_specs=[pl.BlockSpec(memory_space=pltpu.SMEM)] + [_whole_vmem()] * (4 + 3 * n),
    )(me, srecv, vrecv, sown, vown, *w, *m, *v)
    return res[0:n], res[n:2 * n], res[2 * n:3 * n], res[3 * n:4 * n], res[4 * n]


_WEIGHTS = ("meta_tokens", "norm_mix_g", "w_in", "conv_w", "conv_b", "conv_ln_g", "conv_ln_b", "gla_w_gate2", "gla_gate_b",
            "gla_norm_g", "w_out", "norm_ffn_g", "w_ffn_gate", "w_ffn_up", "w_ffn_down", "norm_final_g")
_MATRICES = ("w_in", "w_out", "w_ffn_gate", "w_ffn_up", "w_ffn_down")
_TRANSPOSED = ("w_ffn_gate", "w_ffn_up")


def kernel(x, meta_tokens, norm_mix_g, w_in, conv_w, conv_b, conv_ln_g, conv_ln_b, gla_w_gate2, gla_gate_b, gla_norm_g, w_out, norm_ffn_g, w_ffn_gate, w_ffn_up, w_ffn_down, norm_final_g, loss_target, m_meta_tokens, m_norm_mix_g, m_w_in, m_conv_w, m_conv_b, m_conv_ln_g, m_conv_ln_b, m_gla_w_gate2, m_gla_gate_b, m_gla_norm_g, m_w_out, m_norm_ffn_g, m_w_ffn_gate, m_w_ffn_up, m_w_ffn_down, m_norm_final_g, v_meta_tokens, v_norm_mix_g, v_w_in, v_conv_w, v_conv_b, v_conv_ln_g, v_conv_ln_b, v_gla_w_gate2, v_gla_gate_b, v_gla_norm_g, v_w_out, v_norm_ffn_g, v_w_ffn_gate, v_w_ffn_up, v_w_ffn_down, v_norm_final_g):
    given = dict(locals())
    two_d = lambda a: a.reshape(1, -1) if a.ndim == 1 else a.reshape(a.shape[-2:])
    fams = [{n: given[pre + n] for n in _WEIGHTS} for pre in ("", "m_", "v_")]
    for f in fams:
        for n in _TRANSPOSED:
            f[n] = f[n].transpose(0, 2, 1)
    w = fams[0]

    lands, shards = _stage([two_d(w[n]) for n in _MATRICES], w["meta_tokens"], two_d(w["conv_w"]), two_d(w["gla_w_gate2"]))
    soon, later = (0, 5), (1, 2, 3, 4)
    pick = lambda seq, idx: [seq[i] for i in idx]
    first = _send_start("gather_first_start", pick(shards, soon), pick(lands, soon), "first", norm_mix_g)
    ffn_first = _send_start("gather_ffn_first_start", pick(shards, later), pick(lands, later), "first", first[4])
    h0, tgt_p = _pad_rows(x, loss_target)
    _, arrived = _send_wait("gather_first_wait", *first[0:4], "first", (h0, tgt_p, ffn_first[4]))
    forward = _send_start("gather_forward_start", [], arrived, "forward", ffn_first[4])
    _, (a_in, a_small) = _send_wait("gather_forward_wait", *forward[0:4], "forward", forward[4])
    w_in, meta, conv_taps, w2 = _unshard_in(a_in, a_small, forward[4])
    p = dict(meta=meta, conv_w=conv_taps, w2=w2, w_in=w_in, g1=norm_mix_g, conv_b=conv_b, ln_g=conv_ln_g, ln_b=conv_ln_b,
             gb=gla_gate_b, ng=gla_norm_g, g2=norm_ffn_g, g3=two_d(norm_final_g), token=forward[4])
    passed = {}

    def pass_on(after):
        _, arrived_ffn = _send_wait("gather_ffn_first_wait", *ffn_first[0:4], "first", after)
        passed["sent"] = _send_start("gather_ffn_forward_start", [], arrived_ffn, "forward", after)
        return passed["sent"][4]

    def late_weights(after):
        _, (a_out, a_g, a_u, a_d) = _send_wait("gather_ffn_forward_wait", *passed["sent"][0:4], "forward", after)
        return a_out.reshape(D, D), a_g.reshape(D_FF, D), a_u.reshape(D_FF, D), a_d.reshape(D_FF, D)

    sent = {}

    def send_early(tag, mats):
        landing = [_in_hbm(lax.empty(m_.shape, m_.dtype)) for m_ in mats]
        sent[tag] = _send_start("scatter_" + tag + "_start", mats, landing, "scatter", norm_mix_g)
        return sent[tag][4]

    grad_x, g = _local_step(h0, tgt_p, p, pass_on, late_weights, send_early)

    token = send_early("small", list(_pack_small(g)))
    x_, y_, c_ = _position()
    me = (4 * x_ + 2 * y_ + c_).astype(jnp.int32).reshape(1)
    res = {}
    for tag, names in (("ffn", ("w_ffn_gate", "w_ffn_up", "w_ffn_down")), ("out", ("w_out",)), ("in", ("w_in",))):
        own, recv = _send_wait("scatter_" + tag + "_wait", *sent[tag][0:4], "scatter", token)
        for n, o_, r_ in zip(names, own, recv):
            res[n] = _update_matrix(r_, o_, me, *[f[n] for f in fams], "update_" + n)
            token = res[n][1]
    (sown, vown), (srecv, vrecv) = _send_wait("scatter_small_wait", *sent["small"][0:4], "scatter", token)
    small = _update_small(me, srecv, vrecv, sown, vown, *[[two_d(f[n]) for n in _SMALL] for f in fams])
    for i, n in enumerate(_SMALL):
        res[n] = [fam[i].reshape(w[n].shape) for fam in small[0:4]]
    for n in _TRANSPOSED:
        res[n] = [t.transpose(0, 2, 1) for t in res[n]]
    outs = [small[4][0, 0], grad_x]
    for k in range(4):
        outs += [res[n][k] for n in _WEIGHTS]
    return tuple(outs)
```

```python
import functools

import jax
import jax.numpy as jnp
from jax import lax
from jax.experimental import pallas as pl
from jax.experimental.pallas import tpu as pltpu

F32 = jnp.float32
BF16 = jnp.bfloat16

D = 1024
N_META = 16
C_CONV = 512
CONV_W = 31
GLA_H = 4
GLA_DK = 64
GLA_DV = 128
GLA_K = GLA_H * GLA_DK
GLA_V = GLA_H * GLA_DV
RANK = 16
RANK_P = 128
TAU = 16.0
CHUNK = 64
LEAD = CHUNK
ZROWS = LEAD - N_META
D_IN = 2 * C_CONV + 2 * GLA_K + 2 * GLA_V + RANK
D_INP = D_IN - RANK + RANK_P
D_FF = 2816
FF_CHUNK = 1408
FF_SPLIT = (0, 1536, D_FF)
RMS_EPS = 1e-6
LN_EPS = 1e-5
N_DEV = 8

ADAM_LR = 0.001
ADAM_B1 = 0.9
ADAM_B2 = 0.999
ADAM_EPS = 1e-08
ADAM_WD = 0.01
ADAM_STEP = 10

VMEM_LIMIT = 60 * 1024 * 1024
ROW_TILE = 1056
FFN_ROW_TILE = 352
DW_ROW_TILE = 1408
MESH = pl.DeviceIdType.MESH

_NN = (((1,), (0,)), ((), ()))
_NT = (((1,), (1,)), ((), ()))
_TN = (((0,), (0,)), ((), ()))


def _dot(a, b, dims=_NN):
    return lax.dot_general(a, b, dims, preferred_element_type=F32)


def _sigmoid(x):
    return 1.0 / (1.0 + jnp.exp(-x))


def _row_tile(rows, target):
    best = None
    for t in range(16, min(rows, target) + 1, 16):
        if rows % t == 0:
            best = t
    assert best is not None, rows
    return best


def _params(sem=None):
    return pltpu.CompilerParams(dimension_semantics=sem, vmem_limit_bytes=VMEM_LIMIT)


def _whole_vmem():
    return pl.BlockSpec(memory_space=pltpu.VMEM)


def _rows(tm, width):
    return pl.BlockSpec((tm, width), lambda i: (i, 0))


def _fixed(shape):
    return pl.BlockSpec(shape, lambda *_: (0,) * len(shape))


def _fwd_inproj(h0, g1, w_in):
    rows = h0.shape[0]
    tm = _row_tile(rows, ROW_TILE)

    def body(h_ref, g_ref, w_ref, uc_ref, qk_ref, vg_ref, lr_ref, n1_ref):
        h = h_ref[...]
        r = lax.rsqrt(jnp.mean(h * h, axis=-1, keepdims=True) + RMS_EPS)
        n = (h * r * g_ref[...]).astype(BF16)
        n1_ref[...] = n
        uc_ref[...] = _dot(n, w_ref[:, 0:1024])
        qk_ref[...] = _dot(n, w_ref[:, 1024:1536])
        vg_ref[...] = _dot(n, w_ref[:, 1536:2560])
        lr_ref[...] = _dot(n, w_ref[:, 2560:2688])

    return pl.pallas_call(
        body, name="fwd_inproj", grid=(rows // tm,),
        in_specs=[_rows(tm, D), _fixed((1, D)), _whole_vmem()],
        out_specs=[_rows(tm, 1024), _rows(tm, 512), _rows(tm, 1024), _rows(tm, RANK_P), _rows(tm, D)],
        out_shape=[jax.ShapeDtypeStruct((rows, 1024), F32), jax.ShapeDtypeStruct((rows, 512), F32),
                   jax.ShapeDtypeStruct((rows, 1024), F32), jax.ShapeDtypeStruct((rows, RANK_P), F32),
                   jax.ShapeDtypeStruct((rows, D), BF16)],
        compiler_params=_params(("parallel",)),
    )(h0, g1, w_in)


def _mid_rows(yc, yg, h0, tgt, w_out, wg, wu, wd, g2, g3, token, rows_per_example):
    rows = h0.shape[0]
    tm = _row_tile(rows, FFN_ROW_TILE)
    ff_blocks = [slice(lo, hi) for lo, hi in zip(FF_SPLIT[:-1], FF_SPLIT[1:])]

    def body(yc_ref, yg_ref, h0_ref, t_ref, wo_ref, wg_ref, wu_ref, wd_ref, g2_ref, g3_ref, token_ref,
             n2_ref, f_ref, da_ref, db_ref, dh2_ref, dh1_ref, dh1b_ref, dyc_ref, dyg_ref, part_ref):
        i = pl.program_id(0)
        h1 = h0_ref[...] + _dot(yc_ref[...], wo_ref[0:C_CONV, :]) + _dot(yg_ref[...], wo_ref[C_CONV:D, :])
        r2 = lax.rsqrt(jnp.mean(h1 * h1, axis=-1, keepdims=True) + RMS_EPS)
        xh2 = h1 * r2
        n2 = (xh2 * g2_ref[...]).astype(BF16)
        n2_ref[...] = n2
        y2 = jnp.zeros((tm, D), F32)
        for cs in ff_blocks:
            a = _dot(n2, wg_ref[cs, :], _NT)
            b = _dot(n2, wu_ref[cs, :], _NT)
            f = (a * _sigmoid(a) * b).astype(BF16)
            f_ref[:, cs] = f
            da_ref[:, cs] = a.astype(BF16)
            db_ref[:, cs] = b.astype(BF16)
            y2 = y2 + _dot(f, wd_ref[cs, :])
        h2 = h1 + y2
        r3 = lax.rsqrt(jnp.mean(h2 * h2, axis=-1, keepdims=True) + RMS_EPS)
        xh3 = h2 * r3
        g3 = g3_ref[...]
        pos = (i * tm + lax.broadcasted_iota(jnp.int32, (tm, 1), 0)) % rows_per_example
        valid = pos >= LEAD
        err = jnp.where(valid, xh3 * g3 - t_ref[...], 0.0)
        loss = 0.5 / D * jnp.sum(jnp.sum(err * err, axis=-1, keepdims=True), axis=0, keepdims=True)
        dy = err * (1.0 / D)
        dg3 = jnp.sum(dy * xh3, axis=0, keepdims=True)
        dxh = dy * g3
        dh2 = r3 * (dxh - xh3 * jnp.mean(dxh * xh3, axis=-1, keepdims=True))
        dh2b = dh2.astype(BF16)
        dh2_ref[...] = dh2b
        dn2 = jnp.zeros((tm, D), F32)
        for cs in ff_blocks:
            df = _dot(dh2b, wd_ref[cs, :], _NT)
            a = da_ref[:, cs].astype(F32)
            b = db_ref[:, cs].astype(F32)
            sg = _sigmoid(a)
            da = (df * b * sg * (1.0 + a * (1.0 - sg))).astype(BF16)
            db = (df * a * sg).astype(BF16)
            da_ref[:, cs] = da
            db_ref[:, cs] = db
            dn2 = dn2 + _dot(da, wg_ref[cs, :]) + _dot(db, wu_ref[cs, :])
        dg2 = jnp.sum(dn2 * xh2, axis=0, keepdims=True)
        dxh2 = dn2 * g2_ref[...]
        dh1 = dh2 + r2 * (dxh2 - xh2 * jnp.mean(dxh2 * xh2, axis=-1, keepdims=True))
        dh1_ref[...] = dh1
        dh1b = dh1.astype(BF16)
        dh1b_ref[...] = dh1b
        dyc_ref[...] = _dot(dh1b, wo_ref[0:C_CONV, :], _NT)
        dyg_ref[...] = _dot(dh1b, wo_ref[C_CONV:D, :], _NT)

        @pl.when(i == 0)
        def _():
            part_ref[...] = jnp.zeros_like(part_ref)

        part_ref[0:1, :] += dg3
        part_ref[1:2, :] += dg2
        part_ref[2:3, :] += jnp.broadcast_to(loss, (1, D))

    return pl.pallas_call(
        body, name="mid_rows", grid=(rows // tm,),
        in_specs=[_rows(tm, C_CONV), _rows(tm, GLA_V), _rows(tm, D), _rows(tm, D), _whole_vmem(), _whole_vmem(),
                  _whole_vmem(), _whole_vmem(), _fixed((1, D)), _fixed((1, D)), _fixed((8, 128))],
        out_specs=[_rows(tm, D), _rows(tm, D_FF), _rows(tm, D_FF), _rows(tm, D_FF), _rows(tm, D), _rows(tm, D),
                   _rows(tm, D), _rows(tm, C_CONV), _rows(tm, GLA_V), _fixed((8, D))],
        out_shape=[jax.ShapeDtypeStruct((rows, D), BF16)] + [jax.ShapeDtypeStruct((rows, D_FF), BF16)] * 3
        + [jax.ShapeDtypeStruct((rows, D), BF16), jax.ShapeDtypeStruct((rows, D), F32),
           jax.ShapeDtypeStruct((rows, D), BF16), jax.ShapeDtypeStruct((rows, C_CONV), F32),
           jax.ShapeDtypeStruct((rows, GLA_V), F32), jax.ShapeDtypeStruct((8, D), F32)],
        compiler_params=_params(("arbitrary",)),
    )(yc, yg, h0, tgt, w_out, wg, wu, wd, g2, g3, token)


def _bwd_inproj(duc, dqk, dvg, dlr, dh1, h0, w_in, g1, token, rows_per_example):
    rows = h0.shape[0]
    n_ex = rows // rows_per_example
    tm = _row_tile(rows_per_example, ROW_TILE)
    tiles_per_example = rows_per_example // tm
    n_steps = rows // tm

    def body(duc_ref, dqk_ref, dvg_ref, dlr_ref, dh1_ref, h_ref, w_ref, g_ref, token_ref, gx_ref, part_ref, dmeta_ref,
             buf_ref, sems):
        dn = (_dot(duc_ref[...], w_ref[:, 0:1024], _NT) + _dot(dqk_ref[...], w_ref[:, 1024:1536], _NT)
              + _dot(dvg_ref[...], w_ref[:, 1536:2560], _NT) + _dot(dlr_ref[...], w_ref[:, 2560:2688], _NT))
        h = h_ref[...]
        r = lax.rsqrt(jnp.mean(h * h, axis=-1, keepdims=True) + RMS_EPS)
        xh = h * r
        dg = jnp.sum(dn * xh, axis=0, keepdims=True)
        dxh = dn * g_ref[...]
        dh0 = dh1_ref[...] + r * (dxh - xh * jnp.mean(dxh * xh, axis=-1, keepdims=True))
        i = pl.program_id(0)

        def copies(step):
            slot, b, j = step % 2, step // tiles_per_example, step % tiles_per_example
            out = [(j == 0, pltpu.make_async_copy(buf_ref.at[slot, pl.ds(LEAD, tm - LEAD)],
                                                   gx_ref.at[b, pl.ds(0, tm - LEAD)], sems.at[slot]))]
            if tiles_per_example > 1:
                out.append((j != 0, pltpu.make_async_copy(
                    buf_ref.at[slot], gx_ref.at[b, pl.ds(pl.multiple_of(jnp.maximum(j * tm - LEAD, 0), 8), tm)],
                    sems.at[slot])))
            return out

        def each(step, act):
            for cond, cp in copies(step):
                pl.when(cond)(functools.partial(act, cp))

        @pl.when(i >= 2)
        def _():
            each(i - 2, lambda cp: cp.wait())

        buf_ref[i % 2] = dh0
        each(i, lambda cp: cp.start())

        @pl.when(i == n_steps - 1)
        def _():
            each(i, lambda cp: cp.wait())
            if n_steps > 1:
                each(i - 1, lambda cp: cp.wait())

        @pl.when(i == 0)
        def _():
            part_ref[...] = jnp.zeros_like(part_ref)
            dmeta_ref[...] = jnp.zeros_like(dmeta_ref)

        part_ref[0:1, :] += dg

        @pl.when(i % tiles_per_example == 0)
        def _():
            dmeta_ref[...] += dh0[ZROWS:LEAD, :]

    return pl.pallas_call(
        body, name="bwd_inproj", grid=(n_steps,),
        in_specs=[_rows(tm, 1024), _rows(tm, 512), _rows(tm, 1024), _rows(tm, RANK_P), _rows(tm, D), _rows(tm, D),
                  _whole_vmem(), _fixed((1, D)), _fixed((8, 128))],
        out_specs=[_any(), _fixed((8, D)), _fixed((N_META, D))],
        out_shape=[jax.ShapeDtypeStruct((n_ex, rows_per_example - LEAD, D), F32), jax.ShapeDtypeStruct((8, D), F32),
                   jax.ShapeDtypeStruct((N_META, D), F32)],
        scratch_shapes=[pltpu.VMEM((2, tm, D), F32), pltpu.SemaphoreType.DMA((2,))],
        compiler_params=_params(("arbitrary",)),
    )(duc, dqk, dvg, dlr, dh1, h0, w_in, g1, token)


def _dw_blocked(a, bs, width, name):
    rows, m = a.shape
    ws = [b.shape[1] for b in bs]
    assert sum(ws) >= N_DEV * width
    tk = _row_tile(rows, DW_ROW_TILE)
    nk = rows // tk

    def body(a_ref, *refs):
        b_refs, o_ref, acc_ref = refs[:len(bs)], refs[len(bs)], refs[len(bs) + 1]
        k = pl.program_id(0)

        @pl.when(k == 0)
        def _():
            acc_ref[...] = jnp.zeros_like(acc_ref)

        at = a_ref[...].T
        off = 0
        for b_ref, w in zip(b_refs, ws):
            acc_ref[:, off:off + w] += _dot(at, b_ref[...])
            off += w

        @pl.when(k == nk - 1)
        def _():
            for d in range(N_DEV):
                o_ref[d] = acc_ref[:, d * width:(d + 1) * width].astype(BF16)

    return pl.pallas_call(
        body, name=name, grid=(nk,),
        in_specs=[_rows(tk, m)] + [_rows(tk, w) for w in ws],
        out_specs=_fixed((N_DEV, m, width)),
        out_shape=jax.ShapeDtypeStruct((N_DEV, m, width), BF16),
        scratch_shapes=[pltpu.VMEM((m, sum(ws)), F32)],
        compiler_params=_params(("arbitrary",)),
    )(a, *bs)


def _matmul_tn(a, b, name):
    rows, m = a.shape
    n = b.shape[1]
    tk = _row_tile(rows, DW_ROW_TILE)
    tn = n if n <= 1024 else FF_CHUNK
    tm_ = m if m <= 1024 else FF_CHUNK
    assert n % tn == 0 and m % tm_ == 0
    nk = rows // tk

    def body(a_ref, b_ref, o_ref, acc_ref):
        k = pl.program_id(2)

        @pl.when(k == 0)
        def _():
            acc_ref[...] = jnp.zeros_like(acc_ref)

        acc_ref[...] += _dot(a_ref[...], b_ref[...], _TN)

        @pl.when(k == nk - 1)
        def _():
            o_ref[...] = acc_ref[...].astype(BF16)

    return pl.pallas_call(
        body, name=name, grid=(m // tm_, n // tn, nk),
        in_specs=[pl.BlockSpec((tk, tm_), lambda i, j, k: (k, i)), pl.BlockSpec((tk, tn), lambda i, j, k: (k, j))],
        out_specs=pl.BlockSpec((tm_, tn), lambda i, j, k: (i, j)),
        out_shape=jax.ShapeDtypeStruct((m, n), BF16),
        scratch_shapes=[pltpu.VMEM((tm_, tn), F32)],
        compiler_params=_params(("parallel", "parallel", "arbitrary")),
    )(a, b)


HALO = 32
LN_ROWS = 3 * CHUNK
LANES = 128


def _shifted(win, offsets):
    for r in range(8):
        js = [j for j, k in enumerate(offsets) if k % 8 == r]
        if js:
            rolled = win if r == 0 else pltpu.roll(win, CHUNK + HALO - r, 0)
            for j in js:
                yield j, rolled[offsets[j] - r:offsets[j] - r + CHUNK]


def _glu_into(uc_ref, vs_ref, n_chunk):
    vs_ref[0:CHUNK, :] = jnp.zeros((CHUNK, C_CONV), F32)

    def glu(i, carry):
        base = pl.multiple_of(i * CHUNK, CHUNK)
        val = uc_ref[pl.ds(base, CHUNK), 0:C_CONV]
        gate = uc_ref[pl.ds(base, CHUNK), C_CONV:2 * C_CONV]
        vs_ref[pl.ds(base + CHUNK, CHUNK), :] = val * _sigmoid(gate)
        return carry

    lax.fori_loop(0, n_chunk, glu, 0, unroll=3)


def _fwd_conv(uc, conv_w, conv_b, ln_g, ln_b, token, n_ex):
    rows = uc.shape[0]
    lp = rows // n_ex
    n_chunk = lp // CHUNK

    def body(uc_ref, w_ref, b_ref, lg_ref, lb_ref, token_ref, ypre_ref, yc_ref, vs_ref):
        _glu_into(uc_ref, vs_ref, n_chunk)

        def conv(i, carry):
            base = pl.multiple_of(i * CHUNK, CHUNK)
            for lb in range(C_CONV // LANES):
                ls = slice(lb * LANES, (lb + 1) * LANES)
                win = vs_ref[pl.ds(base + CHUNK - HALO, CHUNK + HALO), ls]
                acc = jnp.broadcast_to(b_ref[:, ls], (CHUNK, LANES))
                for j, rows_j in _shifted(win, [HALO - (CONV_W - 1) + j for j in range(CONV_W)]):
                    acc = acc + w_ref[j:j + 1, ls] * rows_j
                ypre_ref[pl.ds(base, CHUNK), ls] = acc
            y = ypre_ref[pl.ds(base, CHUNK), :]
            mu = jnp.mean(y, axis=-1, keepdims=True)
            yc_ = y - mu
            rstd = lax.rsqrt(jnp.mean(yc_ * yc_, axis=-1, keepdims=True) + LN_EPS)
            s = yc_ * rstd * lg_ref[...] + lb_ref[...]
            yc_ref[pl.ds(base, CHUNK), :] = (s * _sigmoid(s)).astype(BF16)
            return carry

        lax.fori_loop(0, n_chunk, conv, 0, unroll=3)

    ex = lambda w: pl.BlockSpec((lp, w), lambda b: (b, 0))
    return pl.pallas_call(
        body, name="fwd_conv", grid=(n_ex,),
        in_specs=[ex(2 * C_CONV), _fixed((32, C_CONV)), _fixed((1, C_CONV)), _fixed((1, C_CONV)), _fixed((1, C_CONV)),
                  _fixed((8, 128))],
        out_specs=[ex(C_CONV), ex(C_CONV)],
        out_shape=[jax.ShapeDtypeStruct((rows, C_CONV), F32), jax.ShapeDtypeStruct((rows, C_CONV), BF16)],
        scratch_shapes=[pltpu.VMEM((lp + CHUNK, C_CONV), F32)],
        compiler_params=_params(("parallel",)),
    )(uc, conv_w, conv_b, ln_g, ln_b, token)


def _bwd_conv(uc, ypre, dyc, conv_w, ln_g, ln_b, token, n_ex):
    rows = uc.shape[0]
    lp = rows // n_ex
    n_chunk = lp // CHUNK

    def body(uc_ref, ypre_ref, dyc_ref, w_ref, lg_ref, lb_ref, token_ref, duc_ref, dw_ref, dvec_ref, vs_ref, dys_ref,
             dwacc_ref):
        _glu_into(uc_ref, vs_ref, n_chunk)
        dys_ref[pl.ds(lp, CHUNK), :] = jnp.zeros((CHUNK, C_CONV), F32)
        dwacc_ref[...] = jnp.zeros_like(dwacc_ref)

        def ln_bwd(i, carry):
            dcb, dlg, dlb = carry
            base = pl.multiple_of(i * LN_ROWS, LN_ROWS)
            y = ypre_ref[pl.ds(base, LN_ROWS), :]
            mu = jnp.mean(y, axis=-1, keepdims=True)
            yc_ = y - mu
            rstd = lax.rsqrt(jnp.mean(yc_ * yc_, axis=-1, keepdims=True) + LN_EPS)
            xh = yc_ * rstd
            s = xh * lg_ref[...] + lb_ref[...]
            sg = _sigmoid(s)
            ds = dyc_ref[pl.ds(base, LN_ROWS), :] * (sg * (1.0 + s * (1.0 - sg)))
            dxh = ds * lg_ref[...]
            dy = rstd * (dxh - jnp.mean(dxh, axis=-1, keepdims=True) - xh * jnp.mean(dxh * xh, axis=-1, keepdims=True))
            dys_ref[pl.ds(base, LN_ROWS), :] = dy
            return (dcb + jnp.sum(dy, axis=0, keepdims=True), dlg + jnp.sum(ds * xh, axis=0, keepdims=True),
                    dlb + jnp.sum(ds, axis=0, keepdims=True))

        zero = jnp.zeros((1, C_CONV), F32)
        dcb, dlg, dlb = lax.fori_loop(0, lp // LN_ROWS, ln_bwd, (zero, zero, zero))

        @pl.when(pl.program_id(0) == 0)
        def _():
            dvec_ref[...] = jnp.zeros_like(dvec_ref)
            dw_ref[...] = jnp.zeros_like(dw_ref)

        dvec_ref[0:1, :] += dcb
        dvec_ref[1:2, :] += dlg
        dvec_ref[2:3, :] += dlb

        def taps(i, carry):
            base = pl.multiple_of(i * CHUNK, CHUNK)
            for lb in range(C_CONV // LANES):
                ls = slice(lb * LANES, (lb + 1) * LANES)
                dwin = dys_ref[pl.ds(base, CHUNK + HALO), ls]
                vwin = vs_ref[pl.ds(base + CHUNK - HALO, CHUNK + HALO), ls]
                dy = dwin[0:CHUNK]
                acc = jnp.zeros((CHUNK, LANES), F32)
                for j, rows_j in _shifted(dwin, [CONV_W - 1 - j for j in range(CONV_W)]):
                    acc = acc + w_ref[j:j + 1, ls] * rows_j
                for j, rows_j in _shifted(vwin, [HALO - (CONV_W - 1) + j for j in range(CONV_W)]):
                    dwacc_ref[8 * j:8 * j + 8, ls] += jnp.sum((dy * rows_j).reshape(CHUNK // 8, 8, LANES), axis=0)
                val = uc_ref[pl.ds(base, CHUNK), ls]
                gate = uc_ref[pl.ds(base, CHUNK), C_CONV + lb * LANES:C_CONV + (lb + 1) * LANES]
                sg = _sigmoid(gate)
                duc_ref[pl.ds(base, CHUNK), ls] = (acc * sg).astype(BF16)
                duc_ref[pl.ds(base, CHUNK), C_CONV + lb * LANES:C_CONV + (lb + 1) * LANES] = (
                    acc * val * sg * (1.0 - sg)).astype(BF16)
            return carry

        lax.fori_loop(0, n_chunk, taps, 0, unroll=3)
        for j in range(CONV_W):
            dw_ref[j:j + 1, :] += jnp.sum(dwacc_ref[8 * j:8 * j + 8, :], axis=0, keepdims=True)

    ex = lambda w: pl.BlockSpec((lp, w), lambda b: (b, 0))
    return pl.pallas_call(
        body, name="bwd_conv", grid=(n_ex,),
        in_specs=[ex(2 * C_CONV), ex(C_CONV), ex(C_CONV), _fixed((32, C_CONV)), _fixed((1, C_CONV)), _fixed((1, C_CONV)),
                  _fixed((8, 128))],
        out_specs=[ex(2 * C_CONV), _fixed((32, C_CONV)), _fixed((8, C_CONV))],
        out_shape=[jax.ShapeDtypeStruct((rows, 2 * C_CONV), BF16), jax.ShapeDtypeStruct((32, C_CONV), F32),
                   jax.ShapeDtypeStruct((8, C_CONV), F32)],
        scratch_shapes=[pltpu.VMEM((lp + CHUNK, C_CONV), F32), pltpu.VMEM((lp + CHUNK, C_CONV), F32),
                        pltpu.VMEM((8 * 32, C_CONV), F32)],
        compiler_params=_params(("arbitrary",)),
    )(uc, ypre, dyc, conv_w, ln_g, ln_b, token)


def _seg_chunks(n_chunk):
    return max(c for c in (11, 3, 1) if n_chunk % c == 0)


def _block_mask(shape, row_block, lane_block):
    return (lax.broadcasted_iota(jnp.int32, shape, 0) // row_block) == (lax.broadcasted_iota(jnp.int32, shape, 1) // lane_block)


def _per_head_rows(x, mask):
    return jnp.where(mask, jnp.concatenate([x] * GLA_H, axis=0), 0)


def _fold_heads(full, lane_block):
    lane = lax.broadcasted_iota(jnp.int32, (1, full.shape[1]), 1) // lane_block
    out = jnp.where(lane == 0, full[0:CHUNK], 0.0)
    for h in range(1, GLA_H):
        out = out + jnp.where(lane == h, full[h * CHUNK:(h + 1) * CHUNK], 0.0)
    return out


def _causal_heads():
    return (lax.broadcasted_iota(jnp.int32, (CHUNK, GLA_H * CHUNK), 1) % CHUNK) <= lax.broadcasted_iota(
        jnp.int32, (CHUNK, GLA_H * CHUNK), 0)


def _cumsum_rows(x):
    row = lax.broadcasted_iota(jnp.int32, x.shape, 0)
    s = 1
    while s < CHUNK:
        x = x + jnp.where(row >= s, pltpu.roll(x, s, 0), 0.0)
        s *= 2
    return x


def _rev_cumsum_rows(x):
    row = lax.broadcasted_iota(jnp.int32, x.shape, 0)
    s = 1
    while s < CHUNK:
        x = x + jnp.where(row < CHUNK - s, pltpu.roll(x, CHUNK - s, 0), 0.0)
        s *= 2
    return x


def _gate_terms(lr_ref, w2_ref, gb_ref, rs, first_pos):
    z = _dot(lr_ref[rs, :].astype(BF16), w2_ref[...]) + gb_ref[...]
    la = (jnp.minimum(z, 0.0) - jnp.log(1.0 + jnp.exp(-jnp.abs(z)))) * (1.0 / TAU)
    pos = first_pos + lax.broadcasted_iota(jnp.int32, (CHUNK, 1), 0)
    live = pos >= ZROWS
    la = jnp.where(live, la, 0.0)
    return z, live, _cumsum_rows(la)


def _fwd_gla(qk, vg, lr, w2p, gb, ng, token, n_ex):
    rows = qk.shape[0]
    lp = rows // n_ex
    n_chunk = lp // CHUNK
    sc = _seg_chunks(n_chunk)
    n_seg = n_chunk // sc
    seg = sc * CHUNK

    def body(qk_ref, vg_ref, lr_ref, w2_ref, gb_ref, ng_ref, token_ref, yg_ref, o_ref, st_ref, state_ref):
        sidx = pl.program_id(1)

        @pl.when(sidx == 0)
        def _():
            state_ref[...] = jnp.zeros_like(state_ref)

        causal = _causal_heads()
        k_mask = _block_mask((GLA_H * CHUNK, GLA_K), CHUNK, GLA_DK)
        v_mask = _block_mask((GLA_H * CHUNK, GLA_V), CHUNK, GLA_DV)
        s_mask = _block_mask((GLA_V, GLA_K), GLA_DV, GLA_DK)

        def chunk(ci, carry):
            base = pl.multiple_of(ci * CHUNK, CHUNK)
            rs = pl.ds(base, CHUNK)
            _, _, bcum = _gate_terms(lr_ref, w2_ref, gb_ref, rs, (sidx * sc + ci) * CHUNK)
            bl = bcum[CHUNK - 1:CHUNK, :]
            q = qk_ref[rs, 0:GLA_K]
            k = qk_ref[rs, GLA_K:2 * GLA_K]
            qt = (q * (GLA_DK ** -0.5) * jnp.exp(bcum)).astype(BF16)
            kt = (k * jnp.exp(-bcum)).astype(BF16)
            kh = (k * jnp.exp(bl - bcum)).astype(BF16)
            vb = vg_ref[rs, 0:GLA_V].astype(BF16)
            state = state_ref[...]
            st_ref[ci] = state
            a = jnp.where(causal, _dot(qt, _per_head_rows(kt, k_mask), _NT), 0.0)
            o = _dot(a.astype(BF16), _per_head_rows(vb, v_mask)) + _dot(qt, state.astype(BF16), _NT)
            o_ref[rs, :] = o
            for h in range(GLA_H):
                hs = slice(h * GLA_DV, (h + 1) * GLA_DV)
                oh = o[:, hs]
                ro = lax.rsqrt(jnp.mean(oh * oh, axis=-1, keepdims=True) + RMS_EPS)
                g = vg_ref[rs, GLA_V + h * GLA_DV:GLA_V + (h + 1) * GLA_DV]
                yg_ref[rs, hs] = (oh * ro * ng_ref[...] * g * _sigmoid(g)).astype(BF16)
            state_ref[...] = state * jnp.exp(bl) + jnp.where(s_mask, _dot(vb, kh, _TN), 0.0)
            return carry

        lax.fori_loop(0, sc, chunk, 0, unroll=True)

    sg = lambda w: pl.BlockSpec((seg, w), lambda b, s: (b * n_seg + s, 0))
    return pl.pallas_call(
        body, name="fwd_gla", grid=(n_ex, n_seg),
        in_specs=[sg(2 * GLA_K), sg(2 * GLA_V), sg(RANK_P), _fixed((RANK_P, GLA_K)), _fixed((1, GLA_K)), _fixed((1, GLA_DV)),
                  _fixed((8, 128))],
        out_specs=[sg(GLA_V), sg(GLA_V), pl.BlockSpec((sc, GLA_V, GLA_K), lambda b, s: (b * n_seg + s, 0, 0))],
        out_shape=[jax.ShapeDtypeStruct((rows, GLA_V), BF16), jax.ShapeDtypeStruct((rows, GLA_V), F32),
                   jax.ShapeDtypeStruct((n_ex * n_chunk, GLA_V, GLA_K), F32)],
        scratch_shapes=[pltpu.VMEM((GLA_V, GLA_K), F32)],
        compiler_params=_params(("parallel", "arbitrary")),
    )(qk, vg, lr, w2p, gb, ng, token)


def _bwd_gla(qk, vg, lr, o, st, dyg, w2p, gb, ng, yc, yg, dh1b, token, n_ex):
    rows = qk.shape[0]
    lp = rows // n_ex
    n_chunk = lp // CHUNK
    sc = _seg_chunks(n_chunk)
    n_seg = n_chunk // sc
    seg = sc * CHUNK

    def body(qk_ref, vg_ref, lr_ref, o_ref, st_ref, dyg_ref, w2_ref, gb_ref, ng_ref, yc_ref, yg_ref, dh1_ref, token_ref,
             dqk_ref, dvg_ref, dlr_ref, dw2_ref, dvec_ref, dwo_ref, gt_ref, dz_ref, dwo_acc):
        step = pl.program_id(1)
        sidx = n_seg - 1 - step
        first = (step == 0) & (pl.program_id(0) == 0)

        @pl.when(step == 0)
        def _():
            gt_ref[...] = jnp.zeros_like(gt_ref)

        @pl.when(first)
        def _():
            dw2_ref[...] = jnp.zeros_like(dw2_ref)
            dvec_ref[...] = jnp.zeros_like(dvec_ref)
            dwo_acc[...] = jnp.zeros_like(dwo_acc)

        d1 = dh1_ref[...]
        dwo_acc[0:C_CONV, :] += _dot(yc_ref[...], d1, _TN)
        dwo_acc[C_CONV:D, :] += _dot(yg_ref[...], d1, _TN)

        @pl.when((step == n_seg - 1) & (pl.program_id(0) == n_ex - 1))
        def _():
            dwo_ref[...] = dwo_acc[...].astype(BF16)

        causal = _causal_heads()
        k_mask = _block_mask((GLA_H * CHUNK, GLA_K), CHUNK, GLA_DK)
        v_mask = _block_mask((GLA_H * CHUNK, GLA_V), CHUNK, GLA_DV)
        s_mask = _block_mask((GLA_V, GLA_K), GLA_DV, GLA_DK)
        last_row = lax.broadcasted_iota(jnp.int32, (CHUNK, 1), 0) == CHUNK - 1
        ng = ng_ref[...]

        def chunk(ii, dng):
            ci = sc - 1 - ii
            base = pl.multiple_of(ci * CHUNK, CHUNK)
            rs = pl.ds(base, CHUNK)
            z, live, bcum = _gate_terms(lr_ref, w2_ref, gb_ref, rs, (sidx * sc + ci) * CHUNK)
            bl = bcum[CHUNK - 1:CHUNK, :]
            ebl = jnp.exp(bl)
            q = qk_ref[rs, 0:GLA_K]
            k = qk_ref[rs, GLA_K:2 * GLA_K]
            eb = jnp.exp(bcum)
            enb = jnp.exp(-bcum)
            ehb = jnp.exp(bl - bcum)
            qt = q * (GLA_DK ** -0.5) * eb
            kt = k * enb
            kh = k * ehb
            qtb = qt.astype(BF16)
            vb = vg_ref[rs, 0:GLA_V].astype(BF16)
            k_rows = _per_head_rows(kt.astype(BF16), k_mask)
            v_rows = _per_head_rows(vb, v_mask)
            gt = gt_ref[...]
            gtb = gt.astype(BF16)
            s_in = st_ref[ci]
            dos = []
            for h in range(GLA_H):
                hs = slice(h * GLA_DV, (h + 1) * GLA_DV)
                gs = slice(GLA_V + h * GLA_DV, GLA_V + (h + 1) * GLA_DV)
                oh = o_ref[rs, hs]
                ro = lax.rsqrt(jnp.mean(oh * oh, axis=-1, keepdims=True) + RMS_EPS)
                on = oh * ro
                g = vg_ref[rs, gs]
                sg = _sigmoid(g)
                dout = dyg_ref[rs, hs]
                dvg_ref[rs, gs] = (dout * on * ng * (sg * (1.0 + g * (1.0 - sg)))).astype(BF16)
                dw = dout * g * sg
                dng = dng + jnp.sum(dw * on, axis=0, keepdims=True)
                don = dw * ng
                dos.append((ro * (don - on * jnp.mean(don * on, axis=-1, keepdims=True))).astype(BF16))
            dob = jnp.concatenate(dos, axis=1)
            a = jnp.where(causal, _dot(qtb, k_rows, _NT), 0.0).astype(BF16)
            da = jnp.where(causal, _dot(dob, v_rows, _NT), 0.0).astype(BF16)
            dv = _fold_heads(_dot(a, dob, _TN), GLA_DV) + _dot(kh.astype(BF16), gtb, _NT)
            dvg_ref[rs, 0:GLA_V] = dv.astype(BF16)
            dkh = _dot(vb, gtb)
            dqt = _dot(da, k_rows) + _dot(dob, s_in.astype(BF16))
            dkt = _fold_heads(_dot(da, qtb, _TN), GLA_DK)
            dbl = jnp.sum(gt * s_in, axis=0, keepdims=True) * ebl + jnp.sum(dkh * kh, axis=0, keepdims=True)
            dqk_ref[rs, 0:GLA_K] = (dqt * (GLA_DK ** -0.5) * eb).astype(BF16)
            dqk_ref[rs, GLA_K:2 * GLA_K] = (dkt * enb + dkh * ehb).astype(BF16)
            db = dqt * qt - dkt * kt - dkh * kh
            db = jnp.where(last_row, db + dbl, db)
            dla = jnp.where(live, _rev_cumsum_rows(db), 0.0)
            dz_ref[rs, :] = dla * (1.0 / TAU) * (1.0 - _sigmoid(z))
            gt_ref[...] = jnp.where(s_mask, _dot(dob, qtb, _TN), 0.0) + gt * ebl
            return dng

        dng = lax.fori_loop(0, sc, chunk, jnp.zeros((1, GLA_DV), F32), unroll=True)
        dz = dz_ref[...]
        dzb = dz.astype(BF16)
        dlr_ref[...] = _dot(dzb, w2_ref[...], _NT).astype(BF16)
        dw2_ref[...] += _dot(lr_ref[...].astype(BF16), dzb, _TN)
        dvec_ref[0:1, :] += jnp.sum(dz, axis=0, keepdims=True)
        dvec_ref[1:2, 0:GLA_DV] += dng

    sg_ = lambda w: pl.BlockSpec((seg, w), lambda b, s: (b * n_seg + n_seg - 1 - s, 0))
    return pl.pallas_call(
        body, name="bwd_gla", grid=(n_ex, n_seg),
        in_specs=[sg_(2 * GLA_K), sg_(2 * GLA_V), sg_(RANK_P), sg_(GLA_V),
                  pl.BlockSpec((sc, GLA_V, GLA_K), lambda b, s: (b * n_seg + n_seg - 1 - s, 0, 0)), sg_(GLA_V),
                  _fixed((RANK_P, GLA_K)), _fixed((1, GLA_K)), _fixed((1, GLA_DV)), sg_(C_CONV), sg_(GLA_V), sg_(D),
                  _fixed((8, 128))],
        out_specs=[sg_(2 * GLA_K), sg_(2 * GLA_V), sg_(RANK_P), _fixed((RANK_P, GLA_K)), _fixed((8, GLA_K)),
                   _fixed((D, D))],
        out_shape=[jax.ShapeDtypeStruct((rows, 2 * GLA_K), BF16), jax.ShapeDtypeStruct((rows, 2 * GLA_V), BF16),
                   jax.ShapeDtypeStruct((rows, RANK_P), BF16), jax.ShapeDtypeStruct((RANK_P, GLA_K), F32),
                   jax.ShapeDtypeStruct((8, GLA_K), F32), jax.ShapeDtypeStruct((D, D), BF16)],
        scratch_shapes=[pltpu.VMEM((GLA_V, GLA_K), F32), pltpu.VMEM((seg, GLA_K), F32), pltpu.VMEM((D, D), F32)],
        compiler_params=_params(("arbitrary", "arbitrary")),
    )(qk, vg, lr, o, st, dyg, w2p, gb, ng, yc, yg, dh1b, token)


def _pad_rows(x, tgt):
    return jnp.pad(x, ((0, 0), (LEAD, 0), (0, 0))), jnp.pad(tgt, ((0, 0), (LEAD, 0), (0, 0)))


def _local_step(h0, tgt_p, p, pass_on, late_weights, send_early):
    n_ex, lp, _ = h0.shape
    rows = n_ex * lp
    meta = jnp.broadcast_to(p["meta"][None], (n_ex, N_META, D))
    h0 = lax.dynamic_update_slice(h0, meta, (0, ZROWS, 0)).reshape(rows, D)
    tgt_p = tgt_p.reshape(rows, D)

    uc, qk, vg, lr, n1 = _fwd_inproj(h0, p["g1"], p["w_in"])
    ypre, yc = _fwd_conv(uc, p["conv_w"], p["conv_b"], p["ln_g"], p["ln_b"], p["token"], n_ex)
    token = pass_on(yc)
    yg, o, st = _fwd_gla(qk, vg, lr, p["w2"], p["gb"], p["ng"], token, n_ex)
    w_out, wg, wu, wd = late_weights(yg)
    n2, f, da, db, dh2, dh1, dh1b, dyc, dyg, part = _mid_rows(
        yc, yg, h0, tgt_p, w_out, wg, wu, wd, p["g2"], p["g3"], token, lp)
    g = {}
    token = send_early("ffn", [_matmul_tn(a_, b_, name).reshape(N_DEV, FF_S, D) for a_, b_, name in (
        (da, n2, "dw_gate"), (db, n2, "dw_up"), (f, dh2, "dw_down"))])
    dqk, dvg, dlr, g["w2"], g["gla_vec"], dw_out = _bwd_gla(
        qk, vg, lr, o, st, dyg, p["w2"], p["gb"], p["ng"], yc, yg, dh1b, token, n_ex)
    token = send_early("out", [dw_out.reshape(N_DEV, W_OUT_S, D)])
    duc, g["conv_w"], g["conv_vec"] = _bwd_conv(uc, ypre, dyc, p["conv_w"], p["ln_g"], p["ln_b"], token, n_ex)
    token = send_early("in", [_dw_blocked(n1, [duc, dqk, dvg, dlr], W_IN_S, "dw_in")])
    grad_x, g["in_vec"], g["meta"] = _bwd_inproj(duc, dqk, dvg, dlr, dh1, h0, p["w_in"], p["g1"], token, lp)
    g["ffn_vec"] = part
    return grad_x, g


W_IN_S = D_IN // N_DEV
W_OUT_S = D // N_DEV
FF_S = D_FF // N_DEV
CONV_S = C_CONV // N_DEV
GATE_S = GLA_K // N_DEV
SMALL_PACK = 64
CONV_ROW = 16
GATE_ROW = 48
VEC_ROWS = 16
_VEC_ROWS = (("norm_mix_g", D), ("conv_b", C_CONV), ("conv_ln_g", C_CONV), ("conv_ln_b", C_CONV), ("gla_gate_b", GLA_K),
             ("gla_norm_g", GLA_DV), ("norm_ffn_g", D), ("norm_final_g", D))
LOSS_ROW = len(_VEC_ROWS)


def _position():
    return lax.axis_index("x"), lax.axis_index("y"), lax.axis_index("c")


def _any():
    return pl.BlockSpec(memory_space=pl.ANY)


def _stage(mats, meta, conv_w, w2):
    n_t = len(mats) + 1

    def body(*refs):
        ins = refs[0:n_t - 1]
        meta_ref, cw_ref, w2_ref = refs[n_t - 1:n_t + 2]
        lands = refs[n_t + 2:2 * n_t + 2]
        shards = refs[2 * n_t + 2:3 * n_t + 2]
        sems = refs[3 * n_t + 2]
        for s_ref, w_ref in zip(shards, ins):
            s_ref[...] = w_ref[...].astype(BF16)
        sp = shards[n_t - 1]
        sp[...] = jnp.zeros_like(sp)
        sp[0:N_META, :] = meta_ref[...]
        sp[CONV_ROW:CONV_ROW + CONV_W, 0:CONV_S] = cw_ref[...]
        sp[GATE_ROW:GATE_ROW + RANK, 0:GATE_S] = w2_ref[...]
        x, y, c = _position()
        mine = [pltpu.make_async_copy(shards[t], lands[t].at[4 * x + 2 * y + c], sems.at[t]) for t in range(n_t)]
        for cp in mine:
            cp.start()
        for cp in mine:
            cp.wait()

    shard_shapes = [jax.ShapeDtypeStruct(m.shape, BF16) for m in mats] + [jax.ShapeDtypeStruct((SMALL_PACK, 128), F32)]
    res = pl.pallas_call(
        body, name="stage",
        out_shape=[jax.ShapeDtypeStruct((N_DEV,) + s.shape, s.dtype) for s in shard_shapes] + shard_shapes,
        in_specs=[_whole_vmem()] * (n_t + 2), out_specs=[_any()] * n_t + [_whole_vmem()] * n_t,
        scratch_shapes=[pltpu.SemaphoreType.DMA((n_t,))],
        compiler_params=pltpu.CompilerParams(vmem_limit_bytes=VMEM_LIMIT),
    )(*mats, meta, conv_w, w2)
    return res[0:n_t], res[n_t:]


_HBM = pl.BlockSpec(memory_space=pltpu.HBM)
_SEM = pl.BlockSpec(memory_space=pltpu.SEMAPHORE)
_EFFECT = pltpu.SideEffectType.DATAFLOW_SIDE_EFFECTING


_N_ROUTES = {"scatter": 7, "first": 4, "forward": 3}


def _routes(mode):
    x, y, c = _position()
    me = 4 * x + 2 * y + c
    if mode == "scatter":
        out = []
        for k in range(1, N_DEV):
            px = 1 - x if k & 4 else x
            py = 1 - y if k & 2 else y
            pc = 1 - c if k & 1 else c
            out.append(((px, py, pc), 4 * px + 2 * py + pc, me))
        return out
    if mode == "first":
        return [(pos, None, me) for pos in ((x, y, 1 - c), (1 - x, y, c), (x, 1 - y, c), (1 - x, 1 - y, c))]
    assert mode == "forward"
    return [((x, y, 1 - c), 4 * px + 2 * py + c, 4 * px + 2 * py + c) for px, py in ((1 - x, y), (x, 1 - y), (1 - x, 1 - y))]


def _route_copies(mode, n, src_refs, land_refs, send_sems, recv_sems):
    nr = _N_ROUTES[mode]
    for i, (pos, src_blk, dst_blk) in enumerate(_routes(mode)):
        for t in range(n):
            src = land_refs[t] if mode == "forward" else src_refs[t]
            yield pltpu.make_async_remote_copy(
                src_ref=src if src_blk is None else src.at[src_blk], dst_ref=land_refs[t].at[dst_blk],
                send_sem=send_sems.at[nr * t + i], recv_sem=recv_sems.at[nr * t + i], device_id=pos, device_id_type=MESH)


def _in_hbm(a):
    return pltpu.with_memory_space_constraint(a, pltpu.HBM)


def _send_start(name, srcs, lands, mode, after):
    n, ns = len(lands), len(srcs)
    nsem = _N_ROUTES[mode] * n

    def body(*refs):
        src_refs, land_refs = refs[0:ns], refs[ns:ns + n]
        send_sems, recv_sems = refs[ns + n + 1:ns + n + 3]
        token = refs[2 * (ns + n) + 3]
        for cp in _route_copies(mode, n, src_refs, land_refs, send_sems, recv_sems):
            cp.start()
        token[...] = jnp.zeros_like(token)

    bufs = list(srcs) + list(lands)
    res = pl.pallas_call(
        body, name=name,
        out_shape=(pltpu.SemaphoreType.DMA((nsem,)), pltpu.SemaphoreType.DMA((nsem,)),
                   *[pltpu.HBM(b.shape, b.dtype) for b in bufs], jax.ShapeDtypeStruct((8, 128), F32)),
        in_specs=[_HBM] * len(bufs) + [_any()], out_specs=(_SEM, _SEM, *[_HBM] * len(bufs), _whole_vmem()),
        input_output_aliases={i: 2 + i for i in range(len(bufs))},
        compiler_params=pltpu.CompilerParams(has_side_effects=_EFFECT),
    )(*[_in_hbm(b) for b in bufs], after)
    return res[0], res[1], res[2:2 + ns], res[2 + ns:2 + ns + n], res[2 + ns + n]


def _send_wait(name, send_sems, recv_sems, srcs, lands, mode, after):
    n, ns = len(lands), len(srcs)
    after = after if isinstance(after, tuple) else (after,)

    def body(*refs):
        src_refs, land_refs = refs[0:ns], refs[ns:ns + n]
        send_sems, recv_sems = refs[ns + n:ns + n + 2]
        for cp in _route_copies(mode, n, src_refs, land_refs, send_sems, recv_sems):
            cp.wait_send()
            cp.wait_recv()

    bufs = list(srcs) + list(lands)
    res = pl.pallas_call(
        body, name=name,
        out_shape=tuple(pltpu.HBM(b.shape, b.dtype) for b in bufs),
        in_specs=[_HBM] * len(bufs) + [_SEM, _SEM] + [_any()] * len(after), out_specs=tuple([_HBM] * len(bufs)),
        input_output_aliases={i: i for i in range(len(bufs))},
        compiler_params=pltpu.CompilerParams(has_side_effects=_EFFECT),
    )(*bufs, send_sems, recv_sems, *after)
    return res[0:ns], res[ns:ns + n]


def _unshard_in(a_in, a_small, token):
    def body(a_ref, s_ref, token_ref, w_ref, meta_ref, cw_ref, w2_ref):
        w_ref[:, D_IN:D_INP] = jnp.zeros((D, D_INP - D_IN), BF16)
        w2_ref[...] = jnp.zeros_like(w2_ref)
        for d in range(N_DEV):
            w_ref[:, d * W_IN_S:(d + 1) * W_IN_S] = a_ref[d]
            meta_ref[:, d * 128:(d + 1) * 128] = s_ref[d, 0:N_META, :]
            cw_ref[:, d * CONV_S:(d + 1) * CONV_S] = s_ref[d, CONV_ROW:CONV_ROW + 32, 0:CONV_S]
            w2_ref[0:RANK, d * GATE_S:(d + 1) * GATE_S] = s_ref[d, GATE_ROW:GATE_ROW + RANK, 0:GATE_S].astype(BF16)

    return pl.pallas_call(
        body, name="unshard_in",
        out_shape=[jax.ShapeDtypeStruct((D, D_INP), BF16), jax.ShapeDtypeStruct((N_META, D), F32),
                   jax.ShapeDtypeStruct((32, C_CONV), F32), jax.ShapeDtypeStruct((RANK_P, GLA_K), BF16)],
        compiler_params=pltpu.CompilerParams(vmem_limit_bytes=VMEM_LIMIT),
    )(a_in, a_small, token)


def _pack_small(g):
    def body(meta_ref, cw_ref, w2_ref, in_vec, ffn_vec, conv_vec, gla_vec, sp, vp):
        sp[...] = jnp.zeros_like(sp)
        vp[...] = jnp.zeros_like(vp)
        for d in range(N_DEV):
            sp[d, 0:N_META, :] = meta_ref[:, d * 128:(d + 1) * 128]
            sp[d, CONV_ROW:CONV_ROW + 32, 0:CONV_S] = cw_ref[:, d * CONV_S:(d + 1) * CONV_S]
            sp[d, GATE_ROW:GATE_ROW + RANK, 0:GATE_S] = w2_ref[0:RANK, d * GATE_S:(d + 1) * GATE_S]
            vp[d, 0:1, :] = in_vec[0:1, :]
            vp[d, 1:4, 0:C_CONV] = conv_vec[0:3, :]
            vp[d, 4:5, 0:GLA_K] = gla_vec[0:1, :]
            vp[d, 5:6, 0:GLA_DV] = gla_vec[1:2, 0:GLA_DV]
            vp[d, 6:7, :] = ffn_vec[1:2, :]
            vp[d, 7:8, :] = ffn_vec[0:1, :]
            vp[d, LOSS_ROW:LOSS_ROW + 1, :] = ffn_vec[2:3, :]

    return pl.pallas_call(
        body, name="pack_small",
        out_shape=[jax.ShapeDtypeStruct((N_DEV, SMALL_PACK, 128), F32), jax.ShapeDtypeStruct((N_DEV, VEC_ROWS, D), F32)],
    )(g["meta"], g["conv_w"], g["w2"], g["in_vec"], g["ffn_vec"], g["conv_vec"], g["gla_vec"])


def _adamw(w, g, m, v):
    m = ADAM_B1 * m + (1.0 - ADAM_B1) * g
    v = ADAM_B2 * v + (1.0 - ADAM_B2) * (g * g)
    m_hat = m / (1.0 - ADAM_B1 ** ADAM_STEP)
    v_hat = v / (1.0 - ADAM_B2 ** ADAM_STEP)
    return -ADAM_LR * (m_hat / (jnp.sqrt(v_hat) + ADAM_EPS) + ADAM_WD * w), m, v


def _update_matrix(recv, own, me, w, m, v, name):
    _, r, c = recv.shape
    tr = _row_tile(r, 256)

    def body(me_ref, recv_ref, own_ref, w_ref, m_ref, v_ref, g_ref, d_ref, nm_ref, nv_ref):
        g = jnp.zeros((tr, c), F32)
        for s in range(N_DEV):
            g = g + jnp.where(me_ref[0] == s, own_ref[...], recv_ref[s]).astype(F32)
        g_ref[...] = g
        d_ref[...], nm_ref[...], nv_ref[...] = _adamw(w_ref[...], g, m_ref[...], v_ref[...])

    one = pl.BlockSpec((None, tr, c), lambda i, me_ref: (0, i, 0))
    return pl.pallas_call(
        body, name=name,
        grid_spec=pltpu.PrefetchScalarGridSpec(
            num_scalar_prefetch=1, grid=(r // tr,),
            in_specs=[pl.BlockSpec((N_DEV, tr, c), lambda i, me_ref: (0, i, 0)),
                      pl.BlockSpec((None, tr, c), lambda i, me_ref: (me_ref[0], i, 0)), one, one, one],
            out_specs=[one] * 4),
        out_shape=[jax.ShapeDtypeStruct((1, r, c), F32)] * 4,
        compiler_params=_params(("parallel",)),
    )(me, recv, own, w, m, v)


_SMALL = ("meta_tokens", "conv_w", "gla_w_gate2") + tuple(n for n, _ in _VEC_ROWS)


def _update_small(me, srecv, vrecv, sown, vown, w, m, v):
    n = len(_SMALL)

    def body(*refs):
        me_ref, s_ref, v_ref, so_ref, vo_ref = refs[0:5]
        w_refs, m_refs, v_refs = refs[5:5 + n], refs[5 + n:5 + 2 * n], refs[5 + 2 * n:5 + 3 * n]
        outs = refs[5 + 3 * n:]
        ssum = jnp.zeros((SMALL_PACK, 128), F32)
        vsum = jnp.zeros((VEC_ROWS, D), F32)
        for s in range(N_DEV):
            ssum = ssum + jnp.where(me_ref[0] == s, so_ref[s], s_ref[s])
            vsum = vsum + jnp.where(me_ref[0] == s, vo_ref[s], v_ref[s])
        grads = [ssum[0:N_META, :], ssum[CONV_ROW:CONV_ROW + CONV_W, 0:CONV_S], ssum[GATE_ROW:GATE_ROW + RANK, 0:GATE_S]]
        grads += [vsum[i:i + 1, 0:width] for i, (_, width) in enumerate(_VEC_ROWS)]
        for i, g in enumerate(grads):
            d, nm, nv = _adamw(w_refs[i][...], g, m_refs[i][...], v_refs[i][...])
            outs[i][...] = g
            outs[n + i][...] = d
            outs[2 * n + i][...] = nm
            outs[3 * n + i][...] = nv
        outs[4 * n][...] = vsum[LOSS_ROW:LOSS_ROW + 1, 0:128]

    shapes = [jax.ShapeDtypeStruct(t.shape, F32) for t in w]
    res = pl.pallas_call(
        body, name="update_small", out_shape=shapes * 4 + [jax.ShapeDtypeStruct((1, 128), F32)],
        in_specs=[pl.BlockSpec(memory_space=pltpu.SMEM)] + [_whole_vmem()] * (4 + 3 * n),
    )(me, srecv, vrecv, sown, vown, *w, *m, *v)
    return res[0:n], res[n:2 * n], res[2 * n:3 * n], res[3 * n:4 * n], res[4 * n]


_WEIGHTS = ("meta_tokens", "norm_mix_g", "w_in", "conv_w", "conv_b", "conv_ln_g", "conv_ln_b", "gla_w_gate2", "gla_gate_b",
            "gla_norm_g", "w_out", "norm_ffn_g", "w_ffn_gate", "w_ffn_up", "w_ffn_down", "norm_final_g")
_MATRICES = ("w_in", "w_out", "w_ffn_gate", "w_ffn_up", "w_ffn_down")
_TRANSPOSED = ("w_ffn_gate", "w_ffn_up")


def kernel(x, meta_tokens, norm_mix_g, w_in, conv_w, conv_b, conv_ln_g, conv_ln_b, gla_w_gate2, gla_gate_b, gla_norm_g, w_out, norm_ffn_g, w_ffn_gate, w_ffn_up, w_ffn_down, norm_final_g, loss_target, m_meta_tokens, m_norm_mix_g, m_w_in, m_conv_w, m_conv_b, m_conv_ln_g, m_conv_ln_b, m_gla_w_gate2, m_gla_gate_b, m_gla_norm_g, m_w_out, m_norm_ffn_g, m_w_ffn_gate, m_w_ffn_up, m_w_ffn_down, m_norm_final_g, v_meta_tokens, v_norm_mix_g, v_w_in, v_conv_w, v_conv_b, v_conv_ln_g, v_conv_ln_b, v_gla_w_gate2, v_gla_gate_b, v_gla_norm_g, v_w_out, v_norm_ffn_g, v_w_ffn_gate, v_w_ffn_up, v_w_ffn_down, v_norm_final_g):
    given = dict(locals())
    two_d = lambda a: a.reshape(1, -1) if a.ndim == 1 else a.reshape(a.shape[-2:])
    fams = [{n: given[pre + n] for n in _WEIGHTS} for pre in ("", "m_", "v_")]
    for f in fams:
        for n in _TRANSPOSED:
            f[n] = f[n].transpose(0, 2, 1)
    w = fams[0]

    lands, shards = _stage([two_d(w[n]) for n in _MATRICES], w["meta_tokens"], two_d(w["conv_w"]), two_d(w["gla_w_gate2"]))
    soon, later = (0, 5), (1, 2, 3, 4)
    pick = lambda seq, idx: [seq[i] for i in idx]
    first = _send_start("gather_first_start", pick(shards, soon), pick(lands, soon), "first", norm_mix_g)
    ffn_first = _send_start("gather_ffn_first_start", pick(shards, later), pick(lands, later), "first", first[4])
    h0, tgt_p = _pad_rows(x, loss_target)
    _, arrived = _send_wait("gather_first_wait", *first[0:4], "first", (h0, tgt_p, ffn_first[4]))
    forward = _send_start("gather_forward_start", [], arrived, "forward", ffn_first[4])
    _, (a_in, a_small) = _send_wait("gather_forward_wait", *forward[0:4], "forward", forward[4])
    w_in, meta, conv_taps, w2 = _unshard_in(a_in, a_small, forward[4])
    p = dict(meta=meta, conv_w=conv_taps, w2=w2, w_in=w_in, g1=norm_mix_g, conv_b=conv_b, ln_g=conv_ln_g, ln_b=conv_ln_b,
             gb=gla_gate_b, ng=gla_norm_g, g2=norm_ffn_g, g3=two_d(norm_final_g), token=forward[4])
    passed = {}

    def pass_on(after):
        _, arrived_ffn = _send_wait("gather_ffn_first_wait", *ffn_first[0:4], "first", after)
        passed["sent"] = _send_start("gather_ffn_forward_start", [], arrived_ffn, "forward", after)
        return passed["sent"][4]

    def late_weights(after):
        _, (a_out, a_g, a_u, a_d) = _send_wait("gather_ffn_forward_wait", *passed["sent"][0:4], "forward", after)
        return a_out.reshape(D, D), a_g.reshape(D_FF, D), a_u.reshape(D_FF, D), a_d.reshape(D_FF, D)

    sent = {}

    def send_early(tag, mats):
        landing = [_in_hbm(lax.empty(m_.shape, m_.dtype)) for m_ in mats]
        sent[tag] = _send_start("scatter_" + tag + "_start", mats, landing, "scatter", norm_mix_g)
        return sent[tag][4]

    grad_x, g = _local_step(h0, tgt_p, p, pass_on, late_weights, send_early)

    token = send_early("small", list(_pack_small(g)))
    x_, y_, c_ = _position()
    me = (4 * x_ + 2 * y_ + c_).astype(jnp.int32).reshape(1)
    res = {}
    for tag, names in (("ffn", ("w_ffn_gate", "w_ffn_up", "w_ffn_down")), ("out", ("w_out",)), ("in", ("w_in",))):
        own, recv = _send_wait("scatter_" + tag + "_wait", *sent[tag][0:4], "scatter", token)
        for n, o_, r_ in zip(names, own, recv):
            res[n] = _update_matrix(r_, o_, me, *[f[n] for f in fams], "update_" + n)
            token = res[n][1]
    (sown, vown), (srecv, vrecv) = _send_wait("scatter_small_wait", *sent["small"][0:4], "scatter", token)
    small = _update_small(me, srecv, vrecv, sown, vown, *[[two_d(f[n]) for n in _SMALL] for f in fams])
    for i, n in enumerate(_SMALL):
        res[n] = [fam[i].reshape(w[n].shape) for fam in small[0:4]]
    for n in _TRANSPOSED:
        res[n] = [t.transpose(0, 2, 1) for t in res[n]]
    outs = [small[4][0, 0], grad_x]
    for k in range(4):
        outs += [res[n][k] for n in _WEIGHTS]
    return tuple(outs)
```

```python
import functools

import jax
import jax.numpy as jnp
from jax import lax
from jax.experimental import pallas as pl
from jax.experimental.pallas import tpu as pltpu

F32 = jnp.float32
BF16 = jnp.bfloat16

D = 1024
N_META = 16
C_CONV = 512
CONV_W = 31
GLA_H = 4
GLA_DK = 64
GLA_DV = 128
GLA_K = GLA_H * GLA_DK
GLA_V = GLA_H * GLA_DV
RANK = 16
RANK_P = 128
TAU = 16.0
CHUNK = 64
LEAD = CHUNK
ZROWS = LEAD - N_META
D_IN = 2 * C_CONV + 2 * GLA_K + 2 * GLA_V + RANK
D_INP = D_IN - RANK + RANK_P
D_FF = 2816
FF_CHUNK = 1408
FF_SPLIT = (0, 1536, D_FF)
RMS_EPS = 1e-6
LN_EPS = 1e-5
N_DEV = 8

ADAM_LR = 0.001
ADAM_B1 = 0.9
ADAM_B2 = 0.999
ADAM_EPS = 1e-08
ADAM_WD = 0.01
ADAM_STEP = 10

VMEM_LIMIT = 60 * 1024 * 1024
ROW_TILE = 1056
FFN_ROW_TILE = 352
DW_ROW_TILE = 1408
MESH = pl.DeviceIdType.MESH

_NN = (((1,), (0,)), ((), ()))
_NT = (((1,), (1,)), ((), ()))
_TN = (((0,), (0,)), ((), ()))


def _dot(a, b, dims=_NN):
    return lax.dot_general(a, b, dims, preferred_element_type=F32)


def _sigmoid(x):
    return 1.0 / (1.0 + jnp.exp(-x))


def _row_tile(rows, target):
    best = None
    for t in range(16, min(rows, target) + 1, 16):
        if rows % t == 0:
            best = t
    assert best is not None, rows
    return best


def _params(sem=None):
    return pltpu.CompilerParams(dimension_semantics=sem, vmem_limit_bytes=VMEM_LIMIT)


def _whole_vmem():
    return pl.BlockSpec(memory_space=pltpu.VMEM)


def _rows(tm, width):
    return pl.BlockSpec((tm, width), lambda i: (i, 0))


def _fixed(shape):
    return pl.BlockSpec(shape, lambda *_: (0,) * len(shape))


def _fwd_inproj(h0, g1, w_in):
    rows = h0.shape[0]
    tm = _row_tile(rows, ROW_TILE)

    def body(h_ref, g_ref, w_ref, uc_ref, qk_ref, vg_ref, lr_ref, n1_ref):
        h = h_ref[...]
        r = lax.rsqrt(jnp.mean(h * h, axis=-1, keepdims=True) + RMS_EPS)
        n = (h * r * g_ref[...]).astype(BF16)
        n1_ref[...] = n
        uc_ref[...] = _dot(n, w_ref[:, 0:1024])
        qk_ref[...] = _dot(n, w_ref[:, 1024:1536])
        vg_ref[...] = _dot(n, w_ref[:, 1536:2560])
        lr_ref[...] = _dot(n, w_ref[:, 2560:2688])

    return pl.pallas_call(
        body, name="fwd_inproj", grid=(rows // tm,),
        in_specs=[_rows(tm, D), _fixed((1, D)), _whole_vmem()],
        out_specs=[_rows(tm, 1024), _rows(tm, 512), _rows(tm, 1024), _rows(tm, RANK_P), _rows(tm, D)],
        out_shape=[jax.ShapeDtypeStruct((rows, 1024), F32), jax.ShapeDtypeStruct((rows, 512), F32),
                   jax.ShapeDtypeStruct((rows, 1024), F32), jax.ShapeDtypeStruct((rows, RANK_P), F32),
                   jax.ShapeDtypeStruct((rows, D), BF16)],
        compiler_params=_params(("parallel",)),
    )(h0, g1, w_in)


def _mid_rows(yc, yg, h0, tgt, w_out, wg, wu, wd, g2, g3, token, rows_per_example):
    rows = h0.shape[0]
    tm = _row_tile(rows, FFN_ROW_TILE)
    ff_blocks = [slice(lo, hi) for lo, hi in zip(FF_SPLIT[:-1], FF_SPLIT[1:])]

    def body(yc_ref, yg_ref, h0_ref, t_ref, wo_ref, wg_ref, wu_ref, wd_ref, g2_ref, g3_ref, token_ref,
             n2_ref, f_ref, da_ref, db_ref, dh2_ref, dh1_ref, dh1b_ref, dyc_ref, dyg_ref, part_ref):
        i = pl.program_id(0)
        h1 = h0_ref[...] + _dot(yc_ref[...], wo_ref[0:C_CONV, :]) + _dot(yg_ref[...], wo_ref[C_CONV:D, :])
        r2 = lax.rsqrt(jnp.mean(h1 * h1, axis=-1, keepdims=True) + RMS_EPS)
        xh2 = h1 * r2
        n2 = (xh2 * g2_ref[...]).astype(BF16)
        n2_ref[...] = n2
        y2 = jnp.zeros((tm, D), F32)
        for cs in ff_blocks:
            a = _dot(n2, wg_ref[cs, :], _NT)
            b = _dot(n2, wu_ref[cs, :], _NT)
            f = (a * _sigmoid(a) * b).astype(BF16)
            f_ref[:, cs] = f
            da_ref[:, cs] = a.astype(BF16)
            db_ref[:, cs] = b.astype(BF16)
            y2 = y2 + _dot(f, wd_ref[cs, :])
        h2 = h1 + y2
        r3 = lax.rsqrt(jnp.mean(h2 * h2, axis=-1, keepdims=True) + RMS_EPS)
        xh3 = h2 * r3
        g3 = g3_ref[...]
        pos = (i * tm + lax.broadcasted_iota(jnp.int32, (tm, 1), 0)) % rows_per_example
        valid = pos >= LEAD
        err = jnp.where(valid, xh3 * g3 - t_ref[...], 0.0)
        loss = 0.5 / D * jnp.sum(jnp.sum(err * err, axis=-1, keepdims=True), axis=0, keepdims=True)
        dy = err * (1.0 / D)
        dg3 = jnp.sum(dy * xh3, axis=0, keepdims=True)
        dxh = dy * g3
        dh2 = r3 * (dxh - xh3 * jnp.mean(dxh * xh3, axis=-1, keepdims=True))
        dh2b = dh2.astype(BF16)
        dh2_ref[...] = dh2b
        dn2 = jnp.zeros((tm, D), F32)
        for cs in ff_blocks:
            df = _dot(dh2b, wd_ref[cs, :], _NT)
            a = da_ref[:, cs].astype(F32)
            b = db_ref[:, cs].astype(F32)
            sg = _sigmoid(a)
            da = (df * b * sg * (1.0 + a * (1.0 - sg))).astype(BF16)
            db = (df * a * sg).astype(BF16)
            da_ref[:, cs] = da
            db_ref[:, cs] = db
            dn2 = dn2 + _dot(da, wg_ref[cs, :]) + _dot(db, wu_ref[cs, :])
        dg2 = jnp.sum(dn2 * xh2, axis=0, keepdims=True)
        dxh2 = dn2 * g2_ref[...]
        dh1 = dh2 + r2 * (dxh2 - xh2 * jnp.mean(dxh2 * xh2, axis=-1, keepdims=True))
        dh1_ref[...] = dh1
        dh1b = dh1.astype(BF16)
        dh1b_ref[...] = dh1b
        dyc_ref[...] = _dot(dh1b, wo_ref[0:C_CONV, :], _NT)
        dyg_ref[...] = _dot(dh1b, wo_ref[C_CONV:D, :], _NT)

        @pl.when(i == 0)
        def _():
            part_ref[...] = jnp.zeros_like(part_ref)

        part_ref[0:1, :] += dg3
        part_ref[1:2, :] += dg2
        part_ref[2:3, :] += jnp.broadcast_to(loss, (1, D))

    return pl.pallas_call(
        body, name="mid_rows", grid=(rows // tm,),
        in_specs=[_rows(tm, C_CONV), _rows(tm, GLA_V), _rows(tm, D), _rows(tm, D), _whole_vmem(), _whole_vmem(),
                  _whole_vmem(), _whole_vmem(), _fixed((1, D)), _fixed((1, D)), _fixed((8, 128))],
        out_specs=[_rows(tm, D), _rows(tm, D_FF), _rows(tm, D_FF), _rows(tm, D_FF), _rows(tm, D), _rows(tm, D),
                   _rows(tm, D), _rows(tm, C_CONV), _rows(tm, GLA_V), _fixed((8, D))],
        out_shape=[jax.ShapeDtypeStruct((rows, D), BF16)] + [jax.ShapeDtypeStruct((rows, D_FF), BF16)] * 3
        + [jax.ShapeDtypeStruct((rows, D), BF16), jax.ShapeDtypeStruct((rows, D), F32),
           jax.ShapeDtypeStruct((rows, D), BF16), jax.ShapeDtypeStruct((rows, C_CONV), F32),
           jax.ShapeDtypeStruct((rows, GLA_V), F32), jax.ShapeDtypeStruct((8, D), F32)],
        compiler_params=_params(("arbitrary",)),
    )(yc, yg, h0, tgt, w_out, wg, wu, wd, g2, g3, token)


def _bwd_inproj(duc, dqk, dvg, dlr, dh1, h0, w_in, g1, token, rows_per_example):
    rows = h0.shape[0]
    n_ex = rows // rows_per_example
    tm = _row_tile(rows_per_example, ROW_TILE)
    tiles_per_example = rows_per_example // tm
    n_steps = rows // tm

    def body(duc_ref, dqk_ref, dvg_ref, dlr_ref, dh1_ref, h_ref, w_ref, g_ref, token_ref, gx_ref, part_ref, dmeta_ref,
             buf_ref, sems):
        dn = (_dot(duc_ref[...], w_ref[:, 0:1024], _NT) + _dot(dqk_ref[...], w_ref[:, 1024:1536], _NT)
              + _dot(dvg_ref[...], w_ref[:, 1536:2560], _NT) + _dot(dlr_ref[...], w_ref[:, 2560:2688], _NT))
        h = h_ref[...]
        r = lax.rsqrt(jnp.mean(h * h, axis=-1, keepdims=True) + RMS_EPS)
        xh = h * r
        dg = jnp.sum(dn * xh, axis=0, keepdims=True)
        dxh = dn * g_ref[...]
        dh0 = dh1_ref[...] + r * (dxh - xh * jnp.mean(dxh * xh, axis=-1, keepdims=True))
        i = pl.program_id(0)

        def copies(step):
            slot, b, j = step % 2, step // tiles_per_example, step % tiles_per_example
            out = [(j == 0, pltpu.make_async_copy(buf_ref.at[slot, pl.ds(LEAD, tm - LEAD)],
                                                   gx_ref.at[b, pl.ds(0, tm - LEAD)], sems.at[slot]))]
            if tiles_per_example > 1:
                out.append((j != 0, pltpu.make_async_copy(
                    buf_ref.at[slot], gx_ref.at[b, pl.ds(pl.multiple_of(jnp.maximum(j * tm - LEAD, 0), 8), tm)],
                    sems.at[slot])))
            return out

        def each(step, act):
            for cond, cp in copies(step):
                pl.when(cond)(functools.partial(act, cp))

        @pl.when(i >= 2)
        def _():
            each(i - 2, lambda cp: cp.wait())

        buf_ref[i % 2] = dh0
        each(i, lambda cp: cp.start())

        @pl.when(i == n_steps - 1)
        def _():
            each(i, lambda cp: cp.wait())
            if n_steps > 1:
                each(i - 1, lambda cp: cp.wait())

        @pl.when(i == 0)
        def _():
            part_ref[...] = jnp.zeros_like(part_ref)
            dmeta_ref[...] = jnp.zeros_like(dmeta_ref)

        part_ref[0:1, :] += dg

        @pl.when(i % tiles_per_example == 0)
        def _():
            dmeta_ref[...] += dh0[ZROWS:LEAD, :]

    return pl.pallas_call(
        body, name="bwd_inproj", grid=(n_steps,),
        in_specs=[_rows(tm, 1024), _rows(tm, 512), _rows(tm, 1024), _rows(tm, RANK_P), _rows(tm, D), _rows(tm, D),
                  _whole_vmem(), _fixed((1, D)), _fixed((8, 128))],
        out_specs=[_any(), _fixed((8, D)), _fixed((N_META, D))],
        out_shape=[jax.ShapeDtypeStruct((n_ex, rows_per_example - LEAD, D), F32), jax.ShapeDtypeStruct((8, D), F32),
                   jax.ShapeDtypeStruct((N_META, D), F32)],
        scratch_shapes=[pltpu.VMEM((2, tm, D), F32), pltpu.SemaphoreType.DMA((2,))],
        compiler_params=_params(("arbitrary",)),
    )(duc, dqk, dvg, dlr, dh1, h0, w_in, g1, token)


def _dw_blocked(a, bs, width, name):
    rows, m = a.shape
    ws = [b.shape[1] for b in bs]
    assert sum(ws) >= N_DEV * width
    tk = _row_tile(rows, DW_ROW_TILE)
    nk = rows // tk

    def body(a_ref, *refs):
        b_refs, o_ref, acc_ref = refs[:len(bs)], refs[len(bs)], refs[len(bs) + 1]
        k = pl.program_id(0)

        @pl.when(k == 0)
        def _():
            acc_ref[...] = jnp.zeros_like(acc_ref)

        at = a_ref[...].T
        off = 0
        for b_ref, w in zip(b_refs, ws):
            acc_ref[:, off:off + w] += _dot(at, b_ref[...])
            off += w

        @pl.when(k == nk - 1)
        def _():
            for d in range(N_DEV):
                o_ref[d] = acc_ref[:, d * width:(d + 1) * width].astype(BF16)

    return pl.pallas_call(
        body, name=name, grid=(nk,),
        in_specs=[_rows(tk, m)] + [_rows(tk, w) for w in ws],
        out_specs=_fixed((N_DEV, m, width)),
        out_shape=jax.ShapeDtypeStruct((N_DEV, m, width), BF16),
        scratch_shapes=[pltpu.VMEM((m, sum(ws)), F32)],
        compiler_params=_params(("arbitrary",)),
    )(a, *bs)


def _matmul_tn(a, b, name):
    rows, m = a.shape
    n = b.shape[1]
    tk = _row_tile(rows, DW_ROW_TILE)
    tn = n if n <= 1024 else FF_CHUNK
    tm_ = m if m <= 1024 else FF_CHUNK
    assert n % tn == 0 and m % tm_ == 0
    nk = rows // tk

    def body(a_ref, b_ref, o_ref, acc_ref):
        k = pl.program_id(2)

        @pl.when(k == 0)
        def _():
            acc_ref[...] = jnp.zeros_like(acc_ref)

        acc_ref[...] += _dot(a_ref[...], b_ref[...], _TN)

        @pl.when(k == nk - 1)
        def _():
            o_ref[...] = acc_ref[...].astype(BF16)

    return pl.pallas_call(
        body, name=name, grid=(m // tm_, n // tn, nk),
        in_specs=[pl.BlockSpec((tk, tm_), lambda i, j, k: (k, i)), pl.BlockSpec((tk, tn), lambda i, j, k: (k, j))],
        out_specs=pl.BlockSpec((tm_, tn), lambda i, j, k: (i, j)),
        out_shape=jax.ShapeDtypeStruct((m, n), BF16),
        scratch_shapes=[pltpu.VMEM((tm_, tn), F32)],
        compiler_params=_params(("parallel", "parallel", "arbitrary")),
    )(a, b)


HALO = 32
LN_ROWS = 3 * CHUNK
LANES = 128


def _shifted(win, offsets):
    for r in range(8):
        js = [j for j, k in enumerate(offsets) if k % 8 == r]
        if js:
            rolled = win if r == 0 else pltpu.roll(win, CHUNK + HALO - r, 0)
            for j in js:
                yield j, rolled[offsets[j] - r:offsets[j] - r + CHUNK]


def _glu_into(uc_ref, vs_ref, n_chunk):
    vs_ref[0:CHUNK, :] = jnp.zeros((CHUNK, C_CONV), F32)

    def glu(i, carry):
        base = pl.multiple_of(i * CHUNK, CHUNK)
        val = uc_ref[pl.ds(base, CHUNK), 0:C_CONV]
        gate = uc_ref[pl.ds(base, CHUNK), C_CONV:2 * C_CONV]
        vs_ref[pl.ds(base + CHUNK, CHUNK), :] = val * _sigmoid(gate)
        return carry

    lax.fori_loop(0, n_chunk, glu, 0, unroll=3)


def _fwd_conv(uc, conv_w, conv_b, ln_g, ln_b, token, n_ex):
    rows = uc.shape[0]
    lp = rows // n_ex
    n_chunk = lp // CHUNK

    def body(uc_ref, w_ref, b_ref, lg_ref, lb_ref, token_ref, ypre_ref, yc_ref, vs_ref):
        _glu_into(uc_ref, vs_ref, n_chunk)

        def conv(i, carry):
            base = pl.multiple_of(i * CHUNK, CHUNK)
            for lb in range(C_CONV // LANES):
                ls = slice(lb * LANES, (lb + 1) * LANES)
                win = vs_ref[pl.ds(base + CHUNK - HALO, CHUNK + HALO), ls]
                acc = jnp.broadcast_to(b_ref[:, ls], (CHUNK, LANES))
                for j, rows_j in _shifted(win, [HALO - (CONV_W - 1) + j for j in range(CONV_W)]):
                    acc = acc + w_ref[j:j + 1, ls] * rows_j
                ypre_ref[pl.ds(base, CHUNK), ls] = acc
            y = ypre_ref[pl.ds(base, CHUNK), :]
            mu = jnp.mean(y, axis=-1, keepdims=True)
            yc_ = y - mu
            rstd = lax.rsqrt(jnp.mean(yc_ * yc_, axis=-1, keepdims=True) + LN_EPS)
            s = yc_ * rstd * lg_ref[...] + lb_ref[...]
            yc_ref[pl.ds(base, CHUNK), :] = (s * _sigmoid(s)).astype(BF16)
            return carry

        lax.fori_loop(0, n_chunk, conv, 0, unroll=3)

    ex = lambda w: pl.BlockSpec((lp, w), lambda b: (b, 0))
    return pl.pallas_call(
        body, name="fwd_conv", grid=(n_ex,),
        in_specs=[ex(2 * C_CONV), _fixed((32, C_CONV)), _fixed((1, C_CONV)), _fixed((1, C_CONV)), _fixed((1, C_CONV)),
                  _fixed((8, 128))],
        out_specs=[ex(C_CONV), ex(C_CONV)],
        out_shape=[jax.ShapeDtypeStruct((rows, C_CONV), F32), jax.ShapeDtypeStruct((rows, C_CONV), BF16)],
        scratch_shapes=[pltpu.VMEM((lp + CHUNK, C_CONV), F32)],
        compiler_params=_params(("parallel",)),
    )(uc, conv_w, conv_b, ln_g, ln_b, token)


def _bwd_conv(uc, ypre, dyc, conv_w, ln_g, ln_b, token, n_ex):
    rows = uc.shape[0]
    lp = rows // n_ex
    n_chunk = lp // CHUNK

    def body(uc_ref, ypre_ref, dyc_ref, w_ref, lg_ref, lb_ref, token_ref, duc_ref, dw_ref, dvec_ref, vs_ref, dys_ref,
             dwacc_ref):
        _glu_into(uc_ref, vs_ref, n_chunk)
        dys_ref[pl.ds(lp, CHUNK), :] = jnp.zeros((CHUNK, C_CONV), F32)
        dwacc_ref[...] = jnp.zeros_like(dwacc_ref)

        def ln_bwd(i, carry):
            dcb, dlg, dlb = carry
            base = pl.multiple_of(i * LN_ROWS, LN_ROWS)
            y = ypre_ref[pl.ds(base, LN_ROWS), :]
            mu = jnp.mean(y, axis=-1, keepdims=True)
            yc_ = y - mu
            rstd = lax.rsqrt(jnp.mean(yc_ * yc_, axis=-1, keepdims=True) + LN_EPS)
            xh = yc_ * rstd
            s = xh * lg_ref[...] + lb_ref[...]
            sg = _sigmoid(s)
            ds = dyc_ref[pl.ds(base, LN_ROWS), :] * (sg * (1.0 + s * (1.0 - sg)))
            dxh = ds * lg_ref[...]
            dy = rstd * (dxh - jnp.mean(dxh, axis=-1, keepdims=True) - xh * jnp.mean(dxh * xh, axis=-1, keepdims=True))
            dys_ref[pl.ds(base, LN_ROWS), :] = dy
            return (dcb + jnp.sum(dy, axis=0, keepdims=True), dlg + jnp.sum(ds * xh, axis=0, keepdims=True),
                    dlb + jnp.sum(ds, axis=0, keepdims=True))

        zero = jnp.zeros((1, C_CONV), F32)
        dcb, dlg, dlb = lax.fori_loop(0, lp // LN_ROWS, ln_bwd, (zero, zero, zero))

        @pl.when(pl.program_id(0) == 0)
        def _():
            dvec_ref[...] = jnp.zeros_like(dvec_ref)
            dw_ref[...] = jnp.zeros_like(dw_ref)

        dvec_ref[0:1, :] += dcb
        dvec_ref[1:2, :] += dlg
        dvec_ref[2:3, :] += dlb

        def taps(i, carry):
            base = pl.multiple_of(i * CHUNK, CHUNK)
            for lb in range(C_CONV // LANES):
                ls = slice(lb * LANES, (lb + 1) * LANES)
                dwin = dys_ref[pl.ds(base, CHUNK + HALO), ls]
                vwin = vs_ref[pl.ds(base + CHUNK - HALO, CHUNK + HALO), ls]
                dy = dwin[0:CHUNK]
                acc = jnp.zeros((CHUNK, LANES), F32)
                for j, rows_j in _shifted(dwin, [CONV_W - 1 - j for j in range(CONV_W)]):
                    acc = acc + w_ref[j:j + 1, ls] * rows_j
                for j, rows_j in _shifted(vwin, [HALO - (CONV_W - 1) + j for j in range(CONV_W)]):
                    dwacc_ref[8 * j:8 * j + 8, ls] += jnp.sum((dy * rows_j).reshape(CHUNK // 8, 8, LANES), axis=0)
                val = uc_ref[pl.ds(base, CHUNK), ls]
                gate = uc_ref[pl.ds(base, CHUNK), C_CONV + lb * LANES:C_CONV + (lb + 1) * LANES]
                sg = _sigmoid(gate)
                duc_ref[pl.ds(base, CHUNK), ls] = (acc * sg).astype(BF16)
                duc_ref[pl.ds(base, CHUNK), C_CONV + lb * LANES:C_CONV + (lb + 1) * LANES] = (
                    acc * val * sg * (1.0 - sg)).astype(BF16)
            return carry

        lax.fori_loop(0, n_chunk, taps, 0, unroll=3)
        for j in range(CONV_W):
            dw_ref[j:j + 1, :] += jnp.sum(dwacc_ref[8 * j:8 * j + 8, :], axis=0, keepdims=True)

    ex = lambda w: pl.BlockSpec((lp, w), lambda b: (b, 0))
    return pl.pallas_call(
        body, name="bwd_conv", grid=(n_ex,),
        in_specs=[ex(2 * C_CONV), ex(C_CONV), ex(C_CONV), _fixed((32, C_CONV)), _fixed((1, C_CONV)), _fixed((1, C_CONV)),
                  _fixed((8, 128))],
        out_specs=[ex(2 * C_CONV), _fixed((32, C_CONV)), _fixed((8, C_CONV))],
        out_shape=[jax.ShapeDtypeStruct((rows, 2 * C_CONV), BF16), jax.ShapeDtypeStruct((32, C_CONV), F32),
                   jax.ShapeDtypeStruct((8, C_CONV), F32)],
        scratch_shapes=[pltpu.VMEM((lp + CHUNK, C_CONV), F32), pltpu.VMEM((lp + CHUNK, C_CONV), F32),
                        pltpu.VMEM((8 * 32, C_CONV), F32)],
        compiler_params=_params(("arbitrary",)),
    )(uc, ypre, dyc, conv_w, ln_g, ln_b, token)


def _seg_chunks(n_chunk):
    return max(c for c in (11, 3, 1) if n_chunk % c == 0)


def _block_mask(shape, row_block, lane_block):
    return (lax.broadcasted_iota(jnp.int32, shape, 0) // row_block) == (lax.broadcasted_iota(jnp.int32, shape, 1) // lane_block)


def _per_head_rows(x, mask):
    return jnp.where(mask, jnp.concatenate([x] * GLA_H, axis=0), 0)


def _fold_heads(full, lane_block):
    lane = lax.broadcasted_iota(jnp.int32, (1, full.shape[1]), 1) // lane_block
    out = jnp.where(lane == 0, full[0:CHUNK], 0.0)
    for h in range(1, GLA_H):
        out = out + jnp.where(lane == h, full[h * CHUNK:(h + 1) * CHUNK], 0.0)
    return out


def _causal_heads():
    return (lax.broadcasted_iota(jnp.int32, (CHUNK, GLA_H * CHUNK), 1) % CHUNK) <= lax.broadcasted_iota(
        jnp.int32, (CHUNK, GLA_H * CHUNK), 0)


def _cumsum_rows(x):
    row = lax.broadcasted_iota(jnp.int32, x.shape, 0)
    s = 1
    while s < CHUNK:
        x = x + jnp.where(row >= s, pltpu.roll(x, s, 0), 0.0)
        s *= 2
    return x


def _rev_cumsum_rows(x):
    row = lax.broadcasted_iota(jnp.int32, x.shape, 0)
    s = 1
    while s < CHUNK:
        x = x + jnp.where(row < CHUNK - s, pltpu.roll(x, CHUNK - s, 0), 0.0)
        s *= 2
    return x


def _gate_terms(lr_ref, w2_ref, gb_ref, rs, first_pos):
    z = _dot(lr_ref[rs, :].astype(BF16), w2_ref[...]) + gb_ref[...]
    la = (jnp.minimum(z, 0.0) - jnp.log(1.0 + jnp.exp(-jnp.abs(z)))) * (1.0 / TAU)
    pos = first_pos + lax.broadcasted_iota(jnp.int32, (CHUNK, 1), 0)
    live = pos >= ZROWS
    la = jnp.where(live, la, 0.0)
    return z, live, _cumsum_rows(la)


def _fwd_gla(qk, vg, lr, w2p, gb, ng, token, n_ex):
    rows = qk.shape[0]
    lp = rows // n_ex
    n_chunk = lp // CHUNK
    sc = _seg_chunks(n_chunk)
    n_seg = n_chunk // sc
    seg = sc * CHUNK

    def body(qk_ref, vg_ref, lr_ref, w2_ref, gb_ref, ng_ref, token_ref, yg_ref, o_ref, st_ref, state_ref):
        sidx = pl.program_id(1)

        @pl.when(sidx == 0)
        def _():
            state_ref[...] = jnp.zeros_like(state_ref)

        causal = _causal_heads()
        k_mask = _block_mask((GLA_H * CHUNK, GLA_K), CHUNK, GLA_DK)
        v_mask = _block_mask((GLA_H * CHUNK, GLA_V), CHUNK, GLA_DV)
        s_mask = _block_mask((GLA_V, GLA_K), GLA_DV, GLA_DK)

        def chunk(ci, carry):
            base = pl.multiple_of(ci * CHUNK, CHUNK)
            rs = pl.ds(base, CHUNK)
            _, _, bcum = _gate_terms(lr_ref, w2_ref, gb_ref, rs, (sidx * sc + ci) * CHUNK)
            bl = bcum[CHUNK - 1:CHUNK, :]
            q = qk_ref[rs, 0:GLA_K]
            k = qk_ref[rs, GLA_K:2 * GLA_K]
            qt = (q * (GLA_DK ** -0.5) * jnp.exp(bcum)).astype(BF16)
            kt = (k * jnp.exp(-bcum)).astype(BF16)
            kh = (k * jnp.exp(bl - bcum)).astype(BF16)
            vb = vg_ref[rs, 0:GLA_V].astype(BF16)
            state = state_ref[...]
            state_b = state.astype(BF16)
            st_ref[ci] = state_b
            a = jnp.where(causal, _dot(qt, _per_head_rows(kt, k_mask), _NT), 0.0)
            o = _dot(a.astype(BF16), _per_head_rows(vb, v_mask)) + _dot(qt, state_b, _NT)
            o_ref[rs, :] = o
            for h in range(GLA_H):
                hs = slice(h * GLA_DV, (h + 1) * GLA_DV)
                oh = o[:, hs]
                ro = lax.rsqrt(jnp.mean(oh * oh, axis=-1, keepdims=True) + RMS_EPS)
                g = vg_ref[rs, GLA_V + h * GLA_DV:GLA_V + (h + 1) * GLA_DV]
                yg_ref[rs, hs] = (oh * ro * ng_ref[...] * g * _sigmoid(g)).astype(BF16)
            state_ref[...] = state * jnp.exp(bl) + jnp.where(s_mask, _dot(vb, kh, _TN), 0.0)
            return carry

        lax.fori_loop(0, sc, chunk, 0, unroll=True)

    sg = lambda w: pl.BlockSpec((seg, w), lambda b, s: (b * n_seg + s, 0))
    return pl.pallas_call(
        body, name="fwd_gla", grid=(n_ex, n_seg),
        in_specs=[sg(2 * GLA_K), sg(2 * GLA_V), sg(RANK_P), _fixed((RANK_P, GLA_K)), _fixed((1, GLA_K)), _fixed((1, GLA_DV)),
                  _fixed((8, 128))],
        out_specs=[sg(GLA_V), sg(GLA_V), pl.BlockSpec((sc, GLA_V, GLA_K), lambda b, s: (b * n_seg + s, 0, 0))],
        out_shape=[jax.ShapeDtypeStruct((rows, GLA_V), BF16), jax.ShapeDtypeStruct((rows, GLA_V), F32),
                   jax.ShapeDtypeStruct((n_ex * n_chunk, GLA_V, GLA_K), BF16)],
        scratch_shapes=[pltpu.VMEM((GLA_V, GLA_K), F32)],
        compiler_params=_params(("parallel", "arbitrary")),
    )(qk, vg, lr, w2p, gb, ng, token)


def _bwd_gla(qk, vg, lr, o, st, dyg, w2p, gb, ng, yc, yg, dh1b, token, n_ex):
    rows = qk.shape[0]
    lp = rows // n_ex
    n_chunk = lp // CHUNK
    sc = _seg_chunks(n_chunk)
    n_seg = n_chunk // sc
    seg = sc * CHUNK

    def body(qk_ref, vg_ref, lr_ref, o_ref, st_ref, dyg_ref, w2_ref, gb_ref, ng_ref, yc_ref, yg_ref, dh1_ref, token_ref,
             dqk_ref, dvg_ref, dlr_ref, dw2_ref, dvec_ref, dwo_ref, gt_ref, dz_ref, dwo_acc):
        step = pl.program_id(1)
        sidx = n_seg - 1 - step
        first = (step == 0) & (pl.program_id(0) == 0)

        @pl.when(step == 0)
        def _():
            gt_ref[...] = jnp.zeros_like(gt_ref)

        @pl.when(first)
        def _():
            dw2_ref[...] = jnp.zeros_like(dw2_ref)
            dvec_ref[...] = jnp.zeros_like(dvec_ref)
            dwo_acc[...] = jnp.zeros_like(dwo_acc)

        d1 = dh1_ref[...]
        dwo_acc[0:C_CONV, :] += _dot(yc_ref[...], d1, _TN)
        dwo_acc[C_CONV:D, :] += _dot(yg_ref[...], d1, _TN)

        @pl.when((step == n_seg - 1) & (pl.program_id(0) == n_ex - 1))
        def _():
            dwo_ref[...] = dwo_acc[...].astype(BF16)

        causal = _causal_heads()
        k_mask = _block_mask((GLA_H * CHUNK, GLA_K), CHUNK, GLA_DK)
        v_mask = _block_mask((GLA_H * CHUNK, GLA_V), CHUNK, GLA_DV)
        s_mask = _block_mask((GLA_V, GLA_K), GLA_DV, GLA_DK)
        last_row = lax.broadcasted_iota(jnp.int32, (CHUNK, 1), 0) == CHUNK - 1
        ng = ng_ref[...]

        def chunk(ii, dng):
            ci = sc - 1 - ii
            base = pl.multiple_of(ci * CHUNK, CHUNK)
            rs = pl.ds(base, CHUNK)
            z, live, bcum = _gate_terms(lr_ref, w2_ref, gb_ref, rs, (sidx * sc + ci) * CHUNK)
            bl = bcum[CHUNK - 1:CHUNK, :]
            ebl = jnp.exp(bl)
            q = qk_ref[rs, 0:GLA_K]
            k = qk_ref[rs, GLA_K:2 * GLA_K]
            eb = jnp.exp(bcum)
            enb = jnp.exp(-bcum)
            ehb = jnp.exp(bl - bcum)
            qt = q * (GLA_DK ** -0.5) * eb
            kt = k * enb
            kh = k * ehb
            qtb = qt.astype(BF16)
            vb = vg_ref[rs, 0:GLA_V].astype(BF16)
            k_rows = _per_head_rows(kt.astype(BF16), k_mask)
            v_rows = _per_head_rows(vb, v_mask)
            gt = gt_ref[...]
            gtb = gt.astype(BF16)
            s_in = st_ref[ci]
            dos = []
            for h in range(GLA_H):
                hs = slice(h * GLA_DV, (h + 1) * GLA_DV)
                gs = slice(GLA_V + h * GLA_DV, GLA_V + (h + 1) * GLA_DV)
                oh = o_ref[rs, hs]
                ro = lax.rsqrt(jnp.mean(oh * oh, axis=-1, keepdims=True) + RMS_EPS)
                on = oh * ro
                g = vg_ref[rs, gs]
                sg = _sigmoid(g)
                dout = dyg_ref[rs, hs]
                dvg_ref[rs, gs] = (dout * on * ng * (sg * (1.0 + g * (1.0 - sg)))).astype(BF16)
                dw = dout * g * sg
                dng = dng + jnp.sum(dw * on, axis=0, keepdims=True)
                don = dw * ng
                dos.append((ro * (don - on * jnp.mean(don * on, axis=-1, keepdims=True))).astype(BF16))
            dob = jnp.concatenate(dos, axis=1)
            a = jnp.where(causal, _dot(qtb, k_rows, _NT), 0.0).astype(BF16)
            da = jnp.where(causal, _dot(dob, v_rows, _NT), 0.0).astype(BF16)
            dv = _fold_heads(_dot(a, dob, _TN), GLA_DV) + _dot(kh.astype(BF16), gtb, _NT)
            dvg_ref[rs, 0:GLA_V] = dv.astype(BF16)
            dkh = _dot(vb, gtb)
            dqt = _dot(da, k_rows) + _dot(dob, s_in)
            dkt = _fold_heads(_dot(da, qtb, _TN), GLA_DK)
            dbl = jnp.sum(gt * s_in.astype(F32), axis=0, keepdims=True) * ebl + jnp.sum(dkh * kh, axis=0, keepdims=True)
            dqk_ref[rs, 0:GLA_K] = (dqt * (GLA_DK ** -0.5) * eb).astype(BF16)
            dqk_ref[rs, GLA_K:2 * GLA_K] = (dkt * enb + dkh * ehb).astype(BF16)
            db = dqt * qt - dkt * kt - dkh * kh
            db = jnp.where(last_row, db + dbl, db)
            dla = jnp.where(live, _rev_cumsum_rows(db), 0.0)
            dz_ref[rs, :] = dla * (1.0 / TAU) * (1.0 - _sigmoid(z))
            gt_ref[...] = jnp.where(s_mask, _dot(dob, qtb, _TN), 0.0) + gt * ebl
            return dng

        dng = lax.fori_loop(0, sc, chunk, jnp.zeros((1, GLA_DV), F32), unroll=True)
        dz = dz_ref[...]
        dzb = dz.astype(BF16)
        dlr_ref[...] = _dot(dzb, w2_ref[...], _NT).astype(BF16)
        dw2_ref[...] += _dot(lr_ref[...].astype(BF16), dzb, _TN)
        dvec_ref[0:1, :] += jnp.sum(dz, axis=0, keepdims=True)
        dvec_ref[1:2, 0:GLA_DV] += dng

    sg_ = lambda w: pl.BlockSpec((seg, w), lambda b, s: (b * n_seg + n_seg - 1 - s, 0))
    return pl.pallas_call(
        body, name="bwd_gla", grid=(n_ex, n_seg),
        in_specs=[sg_(2 * GLA_K), sg_(2 * GLA_V), sg_(RANK_P), sg_(GLA_V),
                  pl.BlockSpec((sc, GLA_V, GLA_K), lambda b, s: (b * n_seg + n_seg - 1 - s, 0, 0)), sg_(GLA_V),
                  _fixed((RANK_P, GLA_K)), _fixed((1, GLA_K)), _fixed((1, GLA_DV)), sg_(C_CONV), sg_(GLA_V), sg_(D),
                  _fixed((8, 128))],
        out_specs=[sg_(2 * GLA_K), sg_(2 * GLA_V), sg_(RANK_P), _fixed((RANK_P, GLA_K)), _fixed((8, GLA_K)),
                   _fixed((D, D))],
        out_shape=[jax.ShapeDtypeStruct((rows, 2 * GLA_K), BF16), jax.ShapeDtypeStruct((rows, 2 * GLA_V), BF16),
                   jax.ShapeDtypeStruct((rows, RANK_P), BF16), jax.ShapeDtypeStruct((RANK_P, GLA_K), F32),
                   jax.ShapeDtypeStruct((8, GLA_K), F32), jax.ShapeDtypeStruct((D, D), BF16)],
        scratch_shapes=[pltpu.VMEM((GLA_V, GLA_K), F32), pltpu.VMEM((seg, GLA_K), F32), pltpu.VMEM((D, D), F32)],
        compiler_params=_params(("arbitrary", "arbitrary")),
    )(qk, vg, lr, o, st, dyg, w2p, gb, ng, yc, yg, dh1b, token)


def _pad_rows(x, tgt):
    return jnp.pad(x, ((0, 0), (LEAD, 0), (0, 0))), jnp.pad(tgt, ((0, 0), (LEAD, 0), (0, 0)))


def _local_step(h0, tgt_p, p, pass_on, late_weights, send_early):
    n_ex, lp, _ = h0.shape
    rows = n_ex * lp
    meta = jnp.broadcast_to(p["meta"][None], (n_ex, N_META, D))
    h0 = lax.dynamic_update_slice(h0, meta, (0, ZROWS, 0)).reshape(rows, D)
    tgt_p = tgt_p.reshape(rows, D)

    uc, qk, vg, lr, n1 = _fwd_inproj(h0, p["g1"], p["w_in"])
    ypre, yc = _fwd_conv(uc, p["conv_w"], p["conv_b"], p["ln_g"], p["ln_b"], p["token"], n_ex)
    token = pass_on(yc)
    yg, o, st = _fwd_gla(qk, vg, lr, p["w2"], p["gb"], p["ng"], token, n_ex)
    w_out, wg, wu, wd = late_weights(yg)
    n2, f, da, db, dh2, dh1, dh1b, dyc, dyg, part = _mid_rows(
        yc, yg, h0, tgt_p, w_out, wg, wu, wd, p["g2"], p["g3"], token, lp)
    g = {}
    token = send_early("ffn", [_matmul_tn(a_, b_, name).reshape(N_DEV, FF_S, D) for a_, b_, name in (
        (da, n2, "dw_gate"), (db, n2, "dw_up"), (f, dh2, "dw_down"))])
    dqk, dvg, dlr, g["w2"], g["gla_vec"], dw_out = _bwd_gla(
        qk, vg, lr, o, st, dyg, p["w2"], p["gb"], p["ng"], yc, yg, dh1b, token, n_ex)
    token = send_early("out", [dw_out.reshape(N_DEV, W_OUT_S, D)])
    duc, g["conv_w"], g["conv_vec"] = _bwd_conv(uc, ypre, dyc, p["conv_w"], p["ln_g"], p["ln_b"], token, n_ex)
    token = send_early("in", [_dw_blocked(n1, [duc, dqk, dvg, dlr], W_IN_S, "dw_in")])
    grad_x, g["in_vec"], g["meta"] = _bwd_inproj(duc, dqk, dvg, dlr, dh1, h0, p["w_in"], p["g1"], token, lp)
    g["ffn_vec"] = part
    return grad_x, g


W_IN_S = D_IN // N_DEV
W_OUT_S = D // N_DEV
FF_S = D_FF // N_DEV
CONV_S = C_CONV // N_DEV
GATE_S = GLA_K // N_DEV
SMALL_PACK = 64
CONV_ROW = 16
GATE_ROW = 48
VEC_ROWS = 16
_VEC_ROWS = (("norm_mix_g", D), ("conv_b", C_CONV), ("conv_ln_g", C_CONV), ("conv_ln_b", C_CONV), ("gla_gate_b", GLA_K),
             ("gla_norm_g", GLA_DV), ("norm_ffn_g", D), ("norm_final_g", D))
LOSS_ROW = len(_VEC_ROWS)


def _position():
    return lax.axis_index("x"), lax.axis_index("y"), lax.axis_index("c")


def _any():
    return pl.BlockSpec(memory_space=pl.ANY)


def _stage(mats, meta, conv_w, w2):
    n_t = len(mats) + 1

    def body(*refs):
        ins = refs[0:n_t - 1]
        meta_ref, cw_ref, w2_ref = refs[n_t - 1:n_t + 2]
        lands = refs[n_t + 2:2 * n_t + 2]
        shards = refs[2 * n_t + 2:3 * n_t + 2]
        sems = refs[3 * n_t + 2]
        for s_ref, w_ref in zip(shards, ins):
            s_ref[...] = w_ref[...].astype(BF16)
        sp = shards[n_t - 1]
        sp[...] = jnp.zeros_like(sp)
        sp[0:N_META, :] = meta_ref[...]
        sp[CONV_ROW:CONV_ROW + CONV_W, 0:CONV_S] = cw_ref[...]
        sp[GATE_ROW:GATE_ROW + RANK, 0:GATE_S] = w2_ref[...]
        x, y, c = _position()
        mine = [pltpu.make_async_copy(shards[t], lands[t].at[4 * x + 2 * y + c], sems.at[t]) for t in range(n_t)]
        for cp in mine:
            cp.start()
        for cp in mine:
            cp.wait()

    shard_shapes = [jax.ShapeDtypeStruct(m.shape, BF16) for m in mats] + [jax.ShapeDtypeStruct((SMALL_PACK, 128), F32)]
    res = pl.pallas_call(
        body, name="stage",
        out_shape=[jax.ShapeDtypeStruct((N_DEV,) + s.shape, s.dtype) for s in shard_shapes] + shard_shapes,
        in_specs=[_whole_vmem()] * (n_t + 2), out_specs=[_any()] * n_t + [_whole_vmem()] * n_t,
        scratch_shapes=[pltpu.SemaphoreType.DMA((n_t,))],
        compiler_params=pltpu.CompilerParams(vmem_limit_bytes=VMEM_LIMIT),
    )(*mats, meta, conv_w, w2)
    return res[0:n_t], res[n_t:]


_HBM = pl.BlockSpec(memory_space=pltpu.HBM)
_SEM = pl.BlockSpec(memory_space=pltpu.SEMAPHORE)
_EFFECT = pltpu.SideEffectType.DATAFLOW_SIDE_EFFECTING


_N_ROUTES = {"scatter": 7, "first": 4, "forward": 3}


def _routes(mode):
    x, y, c = _position()
    me = 4 * x + 2 * y + c
    if mode == "scatter":
        out = []
        for k in range(1, N_DEV):
            px = 1 - x if k & 4 else x
            py = 1 - y if k & 2 else y
            pc = 1 - c if k & 1 else c
            out.append(((px, py, pc), 4 * px + 2 * py + pc, me))
        return out
    if mode == "first":
        return [(pos, None, me) for pos in ((x, y, 1 - c), (1 - x, y, c), (x, 1 - y, c), (1 - x, 1 - y, c))]
    assert mode == "forward"
    return [((x, y, 1 - c), 4 * px + 2 * py + c, 4 * px + 2 * py + c) for px, py in ((1 - x, y), (x, 1 - y), (1 - x, 1 - y))]


def _route_copies(mode, n, src_refs, land_refs, send_sems, recv_sems):
    nr = _N_ROUTES[mode]
    for i, (pos, src_blk, dst_blk) in enumerate(_routes(mode)):
        for t in range(n):
            src = land_refs[t] if mode == "forward" else src_refs[t]
            yield pltpu.make_async_remote_copy(
                src_ref=src if src_blk is None else src.at[src_blk], dst_ref=land_refs[t].at[dst_blk],
                send_sem=send_sems.at[nr * t + i], recv_sem=recv_sems.at[nr * t + i], device_id=pos, device_id_type=MESH)


def _in_hbm(a):
    return pltpu.with_memory_space_constraint(a, pltpu.HBM)


def _send_start(name, srcs, lands, mode, after):
    n, ns = len(lands), len(srcs)
    nsem = _N_ROUTES[mode] * n

    def body(*refs):
        src_refs, land_refs = refs[0:ns], refs[ns:ns + n]
        send_sems, recv_sems = refs[ns + n + 1:ns + n + 3]
        token = refs[2 * (ns + n) + 3]
        for cp in _route_copies(mode, n, src_refs, land_refs, send_sems, recv_sems):
            cp.start()
        token[...] = jnp.zeros_like(token)

    bufs = list(srcs) + list(lands)
    res = pl.pallas_call(
        body, name=name,
        out_shape=(pltpu.SemaphoreType.DMA((nsem,)), pltpu.SemaphoreType.DMA((nsem,)),
                   *[pltpu.HBM(b.shape, b.dtype) for b in bufs], jax.ShapeDtypeStruct((8, 128), F32)),
        in_specs=[_HBM] * len(bufs) + [_any()], out_specs=(_SEM, _SEM, *[_HBM] * len(bufs), _whole_vmem()),
        input_output_aliases={i: 2 + i for i in range(len(bufs))},
        compiler_params=pltpu.CompilerParams(has_side_effects=_EFFECT),
    )(*[_in_hbm(b) for b in bufs], after)
    return res[0], res[1], res[2:2 + ns], res[2 + ns:2 + ns + n], res[2 + ns + n]


def _send_wait(name, send_sems, recv_sems, srcs, lands, mode, after):
    n, ns = len(lands), len(srcs)
    after = after if isinstance(after, tuple) else (after,)

    def body(*refs):
        src_refs, land_refs = refs[0:ns], refs[ns:ns + n]
        send_sems, recv_sems = refs[ns + n:ns + n + 2]
        for cp in _route_copies(mode, n, src_refs, land_refs, send_sems, recv_sems):
            cp.wait_send()
            cp.wait_recv()

    bufs = list(srcs) + list(lands)
    res = pl.pallas_call(
        body, name=name,
        out_shape=tuple(pltpu.HBM(b.shape, b.dtype) for b in bufs),
        in_specs=[_HBM] * len(bufs) + [_SEM, _SEM] + [_any()] * len(after), out_specs=tuple([_HBM] * len(bufs)),
        input_output_aliases={i: i for i in range(len(bufs))},
        compiler_params=pltpu.CompilerParams(has_side_effects=_EFFECT),
    )(*bufs, send_sems, recv_sems, *after)
    return res[0:ns], res[ns:ns + n]


def _unshard_in(a_in, a_small, token):
    def body(a_ref, s_ref, token_ref, w_ref, meta_ref, cw_ref, w2_ref):
        w_ref[:, D_IN:D_INP] = jnp.zeros((D, D_INP - D_IN), BF16)
        w2_ref[...] = jnp.zeros_like(w2_ref)
        for d in range(N_DEV):
            w_ref[:, d * W_IN_S:(d + 1) * W_IN_S] = a_ref[d]
            meta_ref[:, d * 128:(d + 1) * 128] = s_ref[d, 0:N_META, :]
            cw_ref[:, d * CONV_S:(d + 1) * CONV_S] = s_ref[d, CONV_ROW:CONV_ROW + 32, 0:CONV_S]
            w2_ref[0:RANK, d * GATE_S:(d + 1) * GATE_S] = s_ref[d, GATE_ROW:GATE_ROW + RANK, 0:GATE_S].astype(BF16)

    return pl.pallas_call(
        body, name="unshard_in",
        out_shape=[jax.ShapeDtypeStruct((D, D_INP), BF16), jax.ShapeDtypeStruct((N_META, D), F32),
                   jax.ShapeDtypeStruct((32, C_CONV), F32), jax.ShapeDtypeStruct((RANK_P, GLA_K), BF16)],
        compiler_params=pltpu.CompilerParams(vmem_limit_bytes=VMEM_LIMIT),
    )(a_in, a_small, token)


def _pack_small(g):
    def body(meta_ref, cw_ref, w2_ref, in_vec, ffn_vec, conv_vec, gla_vec, sp, vp):
        sp[...] = jnp.zeros_like(sp)
        vp[...] = jnp.zeros_like(vp)
        for d in range(N_DEV):
            sp[d, 0:N_META, :] = meta_ref[:, d * 128:(d + 1) * 128]
            sp[d, CONV_ROW:CONV_ROW + 32, 0:CONV_S] = cw_ref[:, d * CONV_S:(d + 1) * CONV_S]
            sp[d, GATE_ROW:GATE_ROW + RANK, 0:GATE_S] = w2_ref[0:RANK, d * GATE_S:(d + 1) * GATE_S]
            vp[d, 0:1, :] = in_vec[0:1, :]
            vp[d, 1:4, 0:C_CONV] = conv_vec[0:3, :]
            vp[d, 4:5, 0:GLA_K] = gla_vec[0:1, :]
            vp[d, 5:6, 0:GLA_DV] = gla_vec[1:2, 0:GLA_DV]
            vp[d, 6:7, :] = ffn_vec[1:2, :]
            vp[d, 7:8, :] = ffn_vec[0:1, :]
            vp[d, LOSS_ROW:LOSS_ROW + 1, :] = ffn_vec[2:3, :]

    return pl.pallas_call(
        body, name="pack_small",
        out_shape=[jax.ShapeDtypeStruct((N_DEV, SMALL_PACK, 128), F32), jax.ShapeDtypeStruct((N_DEV, VEC_ROWS, D), F32)],
    )(g["meta"], g["conv_w"], g["w2"], g["in_vec"], g["ffn_vec"], g["conv_vec"], g["gla_vec"])


def _adamw(w, g, m, v):
    m = ADAM_B1 * m + (1.0 - ADAM_B1) * g
    v = ADAM_B2 * v + (1.0 - ADAM_B2) * (g * g)
    m_hat = m / (1.0 - ADAM_B1 ** ADAM_STEP)
    v_hat = v / (1.0 - ADAM_B2 ** ADAM_STEP)
    return -ADAM_LR * (m_hat / (jnp.sqrt(v_hat) + ADAM_EPS) + ADAM_WD * w), m, v


def _update_matrix(recv, own, me, w, m, v, name):
    _, r, c = recv.shape
    tr = _row_tile(r, 256)

    def body(me_ref, recv_ref, own_ref, w_ref, m_ref, v_ref, g_ref, d_ref, nm_ref, nv_ref):
        g = jnp.zeros((tr, c), F32)
        for s in range(N_DEV):
            g = g + jnp.where(me_ref[0] == s, own_ref[...], recv_ref[s]).astype(F32)
        g_ref[...] = g
        d_ref[...], nm_ref[...], nv_ref[...] = _adamw(w_ref[...], g, m_ref[...], v_ref[...])

    one = pl.BlockSpec((None, tr, c), lambda i, me_ref: (0, i, 0))
    return pl.pallas_call(
        body, name=name,
        grid_spec=pltpu.PrefetchScalarGridSpec(
            num_scalar_prefetch=1, grid=(r // tr,),
            in_specs=[pl.BlockSpec((N_DEV, tr, c), lambda i, me_ref: (0, i, 0)),
                      pl.BlockSpec((None, tr, c), lambda i, me_ref: (me_ref[0], i, 0)), one, one, one],
            out_specs=[one] * 4),
        out_shape=[jax.ShapeDtypeStruct((1, r, c), F32)] * 4,
        compiler_params=_params(("parallel",)),
    )(me, recv, own, w, m, v)


_SMALL = ("meta_tokens", "conv_w", "gla_w_gate2") + tuple(n for n, _ in _VEC_ROWS)


def _update_small(me, srecv, vrecv, sown, vown, w, m, v):
    n = len(_SMALL)

    def body(*refs):
        me_ref, s_ref, v_ref, so_ref, vo_ref = refs[0:5]
        w_refs, m_refs, v_refs = refs[5:5 + n], refs[5 + n:5 + 2 * n], refs[5 + 2 * n:5 + 3 * n]
        outs = refs[5 + 3 * n:]
        ssum = jnp.zeros((SMALL_PACK, 128), F32)
        vsum = jnp.zeros((VEC_ROWS, D), F32)
        for s in range(N_DEV):
            ssum = ssum + jnp.where(me_ref[0] == s, so_ref[s], s_ref[s])
            vsum = vsum + jnp.where(me_ref[0] == s, vo_ref[s], v_ref[s])
        grads = [ssum[0:N_META, :], ssum[CONV_ROW:CONV_ROW + CONV_W, 0:CONV_S], ssum[GATE_ROW:GATE_ROW + RANK, 0:GATE_S]]
        grads += [vsum[i:i + 1, 0:width] for i, (_, width) in enumerate(_VEC_ROWS)]
        for i, g in enumerate(grads):
            d, nm, nv = _adamw(w_refs[i][...], g, m_refs[i][...], v_refs[i][...])
            outs[i][...] = g
            outs[n + i][...] = d
            outs[2 * n + i][...] = nm
            outs[3 * n + i][...] = nv
        outs[4 * n][...] = vsum[LOSS_ROW:LOSS_ROW + 1, 0:128]

    shapes = [jax.ShapeDtypeStruct(t.shape, F32) for t in w]
    res = pl.pallas_call(
        body, name="update_small", out_shape=shapes * 4 + [jax.ShapeDtypeStruct((1, 128), F32)],
        in_specs=[pl.BlockSpec(memory_space=pltpu.SMEM)] + [_whole_vmem()] * (4 + 3 * n),
    )(me, srecv, vrecv, sown, vown, *w, *m, *v)
    return res[0:n], res[n:2 * n], res[2 * n:3 * n], res[3 * n:4 * n], res[4 * n]


_WEIGHTS = ("meta_tokens", "norm_mix_g", "w_in", "conv_w", "conv_b", "conv_ln_g", "conv_ln_b", "gla_w_gate2", "gla_gate_b",
            "gla_norm_g", "w_out", "norm_ffn_g", "w_ffn_gate", "w_ffn_up", "w_ffn_down", "norm_final_g")
_MATRICES = ("w_in", "w_out", "w_ffn_gate", "w_ffn_up", "w_ffn_down")
_TRANSPOSED = ("w_ffn_gate", "w_ffn_up")


def kernel(x, meta_tokens, norm_mix_g, w_in, conv_w, conv_b, conv_ln_g, conv_ln_b, gla_w_gate2, gla_gate_b, gla_norm_g, w_out, norm_ffn_g, w_ffn_gate, w_ffn_up, w_ffn_down, norm_final_g, loss_target, m_meta_tokens, m_norm_mix_g, m_w_in, m_conv_w, m_conv_b, m_conv_ln_g, m_conv_ln_b, m_gla_w_gate2, m_gla_gate_b, m_gla_norm_g, m_w_out, m_norm_ffn_g, m_w_ffn_gate, m_w_ffn_up, m_w_ffn_down, m_norm_final_g, v_meta_tokens, v_norm_mix_g, v_w_in, v_conv_w, v_conv_b, v_conv_ln_g, v_conv_ln_b, v_gla_w_gate2, v_gla_gate_b, v_gla_norm_g, v_w_out, v_norm_ffn_g, v_w_ffn_gate, v_w_ffn_up, v_w_ffn_down, v_norm_final_g):
    given = dict(locals())
    two_d = lambda a: a.reshape(1, -1) if a.ndim == 1 else a.reshape(a.shape[-2:])
    fams = [{n: given[pre + n] for n in _WEIGHTS} for pre in ("", "m_", "v_")]
    for f in fams:
        for n in _TRANSPOSED:
            f[n] = f[n].transpose(0, 2, 1)
    w = fams[0]

    lands, shards = _stage([two_d(w[n]) for n in _MATRICES], w["meta_tokens"], two_d(w["conv_w"]), two_d(w["gla_w_gate2"]))
    soon, later = (0, 5), (1, 2, 3, 4)
    pick = lambda seq, idx: [seq[i] for i in idx]
    first = _send_start("gather_first_start", pick(shards, soon), pick(lands, soon), "first", norm_mix_g)
    ffn_first = _send_start("gather_ffn_first_start", pick(shards, later), pick(lands, later), "first", first[4])
    h0, tgt_p = _pad_rows(x, loss_target)
    _, arrived = _send_wait("gather_first_wait", *first[0:4], "first", (h0, tgt_p, ffn_first[4]))
    forward = _send_start("gather_forward_start", [], arrived, "forward", ffn_first[4])
    _, (a_in, a_small) = _send_wait("gather_forward_wait", *forward[0:4], "forward", forward[4])
    w_in, meta, conv_taps, w2 = _unshard_in(a_in, a_small, forward[4])
    p = dict(meta=meta, conv_w=conv_taps, w2=w2, w_in=w_in, g1=norm_mix_g, conv_b=conv_b, ln_g=conv_ln_g, ln_b=conv_ln_b,
             gb=gla_gate_b, ng=gla_norm_g, g2=norm_ffn_g, g3=two_d(norm_final_g), token=forward[4])
    passed = {}

    def pass_on(after):
        _, arrived_ffn = _send_wait("gather_ffn_first_wait", *ffn_first[0:4], "first", after)
        passed["sent"] = _send_start("gather_ffn_forward_start", [], arrived_ffn, "forward", after)
        return passed["sent"][4]

    def late_weights(after):
        _, (a_out, a_g, a_u, a_d) = _send_wait("gather_ffn_forward_wait", *passed["sent"][0:4], "forward", after)
        return a_out.reshape(D, D), a_g.reshape(D_FF, D), a_u.reshape(D_FF, D), a_d.reshape(D_FF, D)

    sent = {}

    def send_early(tag, mats):
        landing = [_in_hbm(lax.empty(m_.shape, m_.dtype)) for m_ in mats]
        sent[tag] = _send_start("scatter_" + tag + "_start", mats, landing, "scatter", norm_mix_g)
        return sent[tag][4]

    grad_x, g = _local_step(h0, tgt_p, p, pass_on, late_weights, send_early)

    token = send_early("small", list(_pack_small(g)))
    x_, y_, c_ = _position()
    me = (4 * x_ + 2 * y_ + c_).astype(jnp.int32).reshape(1)
    res = {}

    def update_layers(token, *layers):
        for tag, names in layers:
            own, recv = _send_wait("scatter_" + tag + "_wait", *sent[tag][0:4], "scatter", token)
            for n, o_, r_ in zip(names, own, recv):
                res[n] = _update_matrix(r_, o_, me, *[f[n] for f in fams], "update_" + n)
                token = res[n][1]
        return token

    token = update_layers(token, ("ffn", ("w_ffn_gate", "w_ffn_up", "w_ffn_down")), ("out", ("w_out",)))
    (sown, vown), (srecv, vrecv) = _send_wait("scatter_small_wait", *sent["small"][0:4], "scatter", token)
    small = _update_small(me, srecv, vrecv, sown, vown, *[[two_d(f[n]) for n in _SMALL] for f in fams])
    update_layers(small[1][0], ("in", ("w_in",)))
    for i, n in enumerate(_SMALL):
        res[n] = [fam[i].reshape(w[n].shape) for fam in small[0:4]]
    for n in _TRANSPOSED:
        res[n] = [t.transpose(0, 2, 1) for t in res[n]]
    outs = [small[4][0, 0], grad_x]
    for k in range(4):
        outs += [res[n][k] for n in _WEIGHTS]
    return tuple(outs)
```

```python
import functools

import jax
import jax.numpy as jnp
from jax import lax
from jax.experimental import pallas as pl
from jax.experimental.pallas import tpu as pltpu

F32 = jnp.float32
BF16 = jnp.bfloat16

D = 1024
N_META = 16
C_CONV = 512
CONV_W = 31
GLA_H = 4
GLA_DK = 64
GLA_DV = 128
GLA_K = GLA_H * GLA_DK
GLA_V = GLA_H * GLA_DV
RANK = 16
RANK_P = 128
TAU = 16.0
CHUNK = 64
LEAD = CHUNK
ZROWS = LEAD - N_META
D_IN = 2 * C_CONV + 2 * GLA_K + 2 * GLA_V + RANK
D_INP = D_IN - RANK + RANK_P
D_FF = 2816
FF_CHUNK = 1408
FF_SPLIT = (0, 1536, D_FF)
RMS_EPS = 1e-6
LN_EPS = 1e-5
N_DEV = 8

ADAM_LR = 0.001
ADAM_B1 = 0.9
ADAM_B2 = 0.999
ADAM_EPS = 1e-08
ADAM_WD = 0.01
ADAM_STEP = 10

VMEM_LIMIT = 60 * 1024 * 1024
ROW_TILE = 1056
FFN_ROW_TILE = 352
DW_ROW_TILE = 1408
MESH = pl.DeviceIdType.MESH

_NN = (((1,), (0,)), ((), ()))
_NT = (((1,), (1,)), ((), ()))
_TN = (((0,), (0,)), ((), ()))


def _dot(a, b, dims=_NN):
    return lax.dot_general(a, b, dims, preferred_element_type=F32)


def _sigmoid(x):
    return 1.0 / (1.0 + jnp.exp(-x))


def _row_tile(rows, target):
    best = None
    for t in range(16, min(rows, target) + 1, 16):
        if rows % t == 0:
            best = t
    assert best is not None, rows
    return best


def _params(sem=None):
    return pltpu.CompilerParams(dimension_semantics=sem, vmem_limit_bytes=VMEM_LIMIT)


def _whole_vmem():
    return pl.BlockSpec(memory_space=pltpu.VMEM)


def _rows(tm, width):
    return pl.BlockSpec((tm, width), lambda i: (i, 0))


def _fixed(shape):
    return pl.BlockSpec(shape, lambda *_: (0,) * len(shape))


def _fwd_inproj(h0, g1, w_in):
    rows = h0.shape[0]
    tm = _row_tile(rows, ROW_TILE)

    def body(h_ref, g_ref, w_ref, uc_ref, qk_ref, vg_ref, lr_ref, n1_ref):
        h = h_ref[...]
        r = lax.rsqrt(jnp.mean(h * h, axis=-1, keepdims=True) + RMS_EPS)
        n = (h * r * g_ref[...]).astype(BF16)
        n1_ref[...] = n
        uc_ref[...] = _dot(n, w_ref[:, 0:1024]).astype(BF16)
        qk_ref[...] = _dot(n, w_ref[:, 1024:1536]).astype(BF16)
        vg_ref[...] = _dot(n, w_ref[:, 1536:2560]).astype(BF16)
        lr_ref[...] = _dot(n, w_ref[:, 2560:2688]).astype(BF16)

    return pl.pallas_call(
        body, name="fwd_inproj", grid=(rows // tm,),
        in_specs=[_rows(tm, D), _fixed((1, D)), _whole_vmem()],
        out_specs=[_rows(tm, 1024), _rows(tm, 512), _rows(tm, 1024), _rows(tm, RANK_P), _rows(tm, D)],
        out_shape=[jax.ShapeDtypeStruct((rows, 1024), BF16), jax.ShapeDtypeStruct((rows, 512), BF16),
                   jax.ShapeDtypeStruct((rows, 1024), BF16), jax.ShapeDtypeStruct((rows, RANK_P), BF16),
                   jax.ShapeDtypeStruct((rows, D), BF16)],
        compiler_params=_params(("parallel",)),
    )(h0, g1, w_in)


def _mid_rows(yc, yg, h0, tgt, w_out, wg, wu, wd, g2, g3, token, rows_per_example):
    rows = h0.shape[0]
    tm = _row_tile(rows, FFN_ROW_TILE)
    ff_blocks = [slice(lo, hi) for lo, hi in zip(FF_SPLIT[:-1], FF_SPLIT[1:])]

    def body(yc_ref, yg_ref, h0_ref, t_ref, wo_ref, wg_ref, wu_ref, wd_ref, g2_ref, g3_ref, token_ref,
             n2_ref, f_ref, da_ref, db_ref, dh2_ref, dh1_ref, dh1b_ref, dyc_ref, dyg_ref, part_ref):
        i = pl.program_id(0)
        h1 = h0_ref[...] + _dot(yc_ref[...], wo_ref[0:C_CONV, :]) + _dot(yg_ref[...], wo_ref[C_CONV:D, :])
        r2 = lax.rsqrt(jnp.mean(h1 * h1, axis=-1, keepdims=True) + RMS_EPS)
        xh2 = h1 * r2
        n2 = (xh2 * g2_ref[...]).astype(BF16)
        n2_ref[...] = n2
        y2 = jnp.zeros((tm, D), F32)
        for cs in ff_blocks:
            a = _dot(n2, wg_ref[cs, :], _NT)
            b = _dot(n2, wu_ref[cs, :], _NT)
            f = (a * _sigmoid(a) * b).astype(BF16)
            f_ref[:, cs] = f
            da_ref[:, cs] = a.astype(BF16)
            db_ref[:, cs] = b.astype(BF16)
            y2 = y2 + _dot(f, wd_ref[cs, :])
        h2 = h1 + y2
        r3 = lax.rsqrt(jnp.mean(h2 * h2, axis=-1, keepdims=True) + RMS_EPS)
        xh3 = h2 * r3
        g3 = g3_ref[...]
        pos = (i * tm + lax.broadcasted_iota(jnp.int32, (tm, 1), 0)) % rows_per_example
        valid = pos >= LEAD
        err = jnp.where(valid, xh3 * g3 - t_ref[...], 0.0)
        loss = 0.5 / D * jnp.sum(jnp.sum(err * err, axis=-1, keepdims=True), axis=0, keepdims=True)
        dy = err * (1.0 / D)
        dg3 = jnp.sum(dy * xh3, axis=0, keepdims=True)
        dxh = dy * g3
        dh2 = r3 * (dxh - xh3 * jnp.mean(dxh * xh3, axis=-1, keepdims=True))
        dh2b = dh2.astype(BF16)
        dh2_ref[...] = dh2b
        dn2 = jnp.zeros((tm, D), F32)
        for cs in ff_blocks:
            df = _dot(dh2b, wd_ref[cs, :], _NT)
            a = da_ref[:, cs].astype(F32)
            b = db_ref[:, cs].astype(F32)
            sg = _sigmoid(a)
            da = (df * b * sg * (1.0 + a * (1.0 - sg))).astype(BF16)
            db = (df * a * sg).astype(BF16)
            da_ref[:, cs] = da
            db_ref[:, cs] = db
            dn2 = dn2 + _dot(da, wg_ref[cs, :]) + _dot(db, wu_ref[cs, :])
        dg2 = jnp.sum(dn2 * xh2, axis=0, keepdims=True)
        dxh2 = dn2 * g2_ref[...]
        dh1 = dh2 + r2 * (dxh2 - xh2 * jnp.mean(dxh2 * xh2, axis=-1, keepdims=True))
        dh1_ref[...] = dh1
        dh1b = dh1.astype(BF16)
        dh1b_ref[...] = dh1b
        dyc_ref[...] = _dot(dh1b, wo_ref[0:C_CONV, :], _NT)
        dyg_ref[...] = _dot(dh1b, wo_ref[C_CONV:D, :], _NT)

        @pl.when(i == 0)
        def _():
            part_ref[...] = jnp.zeros_like(part_ref)

        part_ref[0:1, :] += dg3
        part_ref[1:2, :] += dg2
        part_ref[2:3, :] += jnp.broadcast_to(loss, (1, D))

    return pl.pallas_call(
        body, name="mid_rows", grid=(rows // tm,),
        in_specs=[_rows(tm, C_CONV), _rows(tm, GLA_V), _rows(tm, D), _rows(tm, D), _whole_vmem(), _whole_vmem(),
                  _whole_vmem(), _whole_vmem(), _fixed((1, D)), _fixed((1, D)), _fixed((8, 128))],
        out_specs=[_rows(tm, D), _rows(tm, D_FF), _rows(tm, D_FF), _rows(tm, D_FF), _rows(tm, D), _rows(tm, D),
                   _rows(tm, D), _rows(tm, C_CONV), _rows(tm, GLA_V), _fixed((8, D))],
        out_shape=[jax.ShapeDtypeStruct((rows, D), BF16)] + [jax.ShapeDtypeStruct((rows, D_FF), BF16)] * 3
        + [jax.ShapeDtypeStruct((rows, D), BF16), jax.ShapeDtypeStruct((rows, D), F32),
           jax.ShapeDtypeStruct((rows, D), BF16), jax.ShapeDtypeStruct((rows, C_CONV), F32),
           jax.ShapeDtypeStruct((rows, GLA_V), F32), jax.ShapeDtypeStruct((8, D), F32)],
        compiler_params=_params(("arbitrary",)),
    )(yc, yg, h0, tgt, w_out, wg, wu, wd, g2, g3, token)


def _bwd_inproj(duc, dqk, dvg, dlr, dh1, h0, w_in, g1, token, rows_per_example):
    rows = h0.shape[0]
    n_ex = rows // rows_per_example
    tm = _row_tile(rows_per_example, ROW_TILE)
    tiles_per_example = rows_per_example // tm
    n_steps = rows // tm

    def body(duc_ref, dqk_ref, dvg_ref, dlr_ref, dh1_ref, h_ref, w_ref, g_ref, token_ref, gx_ref, part_ref, dmeta_ref,
             buf_ref, sems):
        dn = (_dot(duc_ref[...], w_ref[:, 0:1024], _NT) + _dot(dqk_ref[...], w_ref[:, 1024:1536], _NT)
              + _dot(dvg_ref[...], w_ref[:, 1536:2560], _NT) + _dot(dlr_ref[...], w_ref[:, 2560:2688], _NT))
        h = h_ref[...]
        r = lax.rsqrt(jnp.mean(h * h, axis=-1, keepdims=True) + RMS_EPS)
        xh = h * r
        dg = jnp.sum(dn * xh, axis=0, keepdims=True)
        dxh = dn * g_ref[...]
        dh0 = dh1_ref[...] + r * (dxh - xh * jnp.mean(dxh * xh, axis=-1, keepdims=True))
        i = pl.program_id(0)

        def copies(step):
            slot, b, j = step % 2, step // tiles_per_example, step % tiles_per_example
            out = [(j == 0, pltpu.make_async_copy(buf_ref.at[slot, pl.ds(LEAD, tm - LEAD)],
                                                   gx_ref.at[b, pl.ds(0, tm - LEAD)], sems.at[slot]))]
            if tiles_per_example > 1:
                out.append((j != 0, pltpu.make_async_copy(
                    buf_ref.at[slot], gx_ref.at[b, pl.ds(pl.multiple_of(jnp.maximum(j * tm - LEAD, 0), 8), tm)],
                    sems.at[slot])))
            return out

        def each(step, act):
            for cond, cp in copies(step):
                pl.when(cond)(functools.partial(act, cp))

        @pl.when(i >= 2)
        def _():
            each(i - 2, lambda cp: cp.wait())

        buf_ref[i % 2] = dh0
        each(i, lambda cp: cp.start())

        @pl.when(i == n_steps - 1)
        def _():
            each(i, lambda cp: cp.wait())
            if n_steps > 1:
                each(i - 1, lambda cp: cp.wait())

        @pl.when(i == 0)
        def _():
            part_ref[...] = jnp.zeros_like(part_ref)
            dmeta_ref[...] = jnp.zeros_like(dmeta_ref)

        part_ref[0:1, :] += dg

        @pl.when(i % tiles_per_example == 0)
        def _():
            dmeta_ref[...] += dh0[ZROWS:LEAD, :]

    return pl.pallas_call(
        body, name="bwd_inproj", grid=(n_steps,),
        in_specs=[_rows(tm, 1024), _rows(tm, 512), _rows(tm, 1024), _rows(tm, RANK_P), _rows(tm, D), _rows(tm, D),
                  _whole_vmem(), _fixed((1, D)), _fixed((8, 128))],
        out_specs=[_any(), _fixed((8, D)), _fixed((N_META, D))],
        out_shape=[jax.ShapeDtypeStruct((n_ex, rows_per_example - LEAD, D), F32), jax.ShapeDtypeStruct((8, D), F32),
                   jax.ShapeDtypeStruct((N_META, D), F32)],
        scratch_shapes=[pltpu.VMEM((2, tm, D), F32), pltpu.SemaphoreType.DMA((2,))],
        compiler_params=_params(("arbitrary",)),
    )(duc, dqk, dvg, dlr, dh1, h0, w_in, g1, token)


def _dw_blocked(a, bs, width, name):
    rows, m = a.shape
    ws = [b.shape[1] for b in bs]
    assert sum(ws) >= N_DEV * width
    tk = _row_tile(rows, DW_ROW_TILE)
    nk = rows // tk

    def body(a_ref, *refs):
        b_refs, o_ref, acc_ref = refs[:len(bs)], refs[len(bs)], refs[len(bs) + 1]
        k = pl.program_id(0)

        @pl.when(k == 0)
        def _():
            acc_ref[...] = jnp.zeros_like(acc_ref)

        at = a_ref[...].T
        off = 0
        for b_ref, w in zip(b_refs, ws):
            acc_ref[:, off:off + w] += _dot(at, b_ref[...])
            off += w

        @pl.when(k == nk - 1)
        def _():
            for d in range(N_DEV):
                o_ref[d] = acc_ref[:, d * width:(d + 1) * width].astype(BF16)

    return pl.pallas_call(
        body, name=name, grid=(nk,),
        in_specs=[_rows(tk, m)] + [_rows(tk, w) for w in ws],
        out_specs=_fixed((N_DEV, m, width)),
        out_shape=jax.ShapeDtypeStruct((N_DEV, m, width), BF16),
        scratch_shapes=[pltpu.VMEM((m, sum(ws)), F32)],
        compiler_params=_params(("arbitrary",)),
    )(a, *bs)


def _matmul_tn(a, b, name):
    rows, m = a.shape
    n = b.shape[1]
    tk = _row_tile(rows, DW_ROW_TILE)
    tn = n if n <= 1024 else FF_CHUNK
    tm_ = m if m <= 1024 else FF_CHUNK
    assert n % tn == 0 and m % tm_ == 0
    nk = rows // tk

    def body(a_ref, b_ref, o_ref, acc_ref):
        k = pl.program_id(2)

        @pl.when(k == 0)
        def _():
            acc_ref[...] = jnp.zeros_like(acc_ref)

        acc_ref[...] += _dot(a_ref[...], b_ref[...], _TN)

        @pl.when(k == nk - 1)
        def _():
            o_ref[...] = acc_ref[...].astype(BF16)

    return pl.pallas_call(
        body, name=name, grid=(m // tm_, n // tn, nk),
        in_specs=[pl.BlockSpec((tk, tm_), lambda i, j, k: (k, i)), pl.BlockSpec((tk, tn), lambda i, j, k: (k, j))],
        out_specs=pl.BlockSpec((tm_, tn), lambda i, j, k: (i, j)),
        out_shape=jax.ShapeDtypeStruct((m, n), BF16),
        scratch_shapes=[pltpu.VMEM((tm_, tn), F32)],
        compiler_params=_params(("parallel", "parallel", "arbitrary")),
    )(a, b)


HALO = 32
LN_ROWS = 3 * CHUNK
LANES = 128


def _shifted(win, offsets):
    for r in range(8):
        js = [j for j, k in enumerate(offsets) if k % 8 == r]
        if js:
            rolled = win if r == 0 else pltpu.roll(win, CHUNK + HALO - r, 0)
            for j in js:
                yield j, rolled[offsets[j] - r:offsets[j] - r + CHUNK]


def _glu_into(uc_ref, vs_ref, n_chunk):
    vs_ref[0:CHUNK, :] = jnp.zeros((CHUNK, C_CONV), F32)

    def glu(i, carry):
        base = pl.multiple_of(i * CHUNK, CHUNK)
        val = uc_ref[pl.ds(base, CHUNK), 0:C_CONV].astype(F32)
        gate = uc_ref[pl.ds(base, CHUNK), C_CONV:2 * C_CONV].astype(F32)
        vs_ref[pl.ds(base + CHUNK, CHUNK), :] = val * _sigmoid(gate)
        return carry

    lax.fori_loop(0, n_chunk, glu, 0, unroll=3)


def _fwd_conv(uc, conv_w, conv_b, ln_g, ln_b, token, n_ex):
    rows = uc.shape[0]
    lp = rows // n_ex
    n_chunk = lp // CHUNK

    def body(uc_ref, w_ref, b_ref, lg_ref, lb_ref, token_ref, ypre_ref, yc_ref, vs_ref):
        _glu_into(uc_ref, vs_ref, n_chunk)

        def conv(i, carry):
            base = pl.multiple_of(i * CHUNK, CHUNK)
            for lb in range(C_CONV // LANES):
                ls = slice(lb * LANES, (lb + 1) * LANES)
                win = vs_ref[pl.ds(base + CHUNK - HALO, CHUNK + HALO), ls]
                acc = jnp.broadcast_to(b_ref[:, ls], (CHUNK, LANES))
                for j, rows_j in _shifted(win, [HALO - (CONV_W - 1) + j for j in range(CONV_W)]):
                    acc = acc + w_ref[j:j + 1, ls] * rows_j
                ypre_ref[pl.ds(base, CHUNK), ls] = acc
            y = ypre_ref[pl.ds(base, CHUNK), :]
            mu = jnp.mean(y, axis=-1, keepdims=True)
            yc_ = y - mu
            rstd = lax.rsqrt(jnp.mean(yc_ * yc_, axis=-1, keepdims=True) + LN_EPS)
            s = yc_ * rstd * lg_ref[...] + lb_ref[...]
            yc_ref[pl.ds(base, CHUNK), :] = (s * _sigmoid(s)).astype(BF16)
            return carry

        lax.fori_loop(0, n_chunk, conv, 0, unroll=3)

    ex = lambda w: pl.BlockSpec((lp, w), lambda b: (b, 0))
    return pl.pallas_call(
        body, name="fwd_conv", grid=(n_ex,),
        in_specs=[ex(2 * C_CONV), _fixed((32, C_CONV)), _fixed((1, C_CONV)), _fixed((1, C_CONV)), _fixed((1, C_CONV)),
                  _fixed((8, 128))],
        out_specs=[ex(C_CONV), ex(C_CONV)],
        out_shape=[jax.ShapeDtypeStruct((rows, C_CONV), F32), jax.ShapeDtypeStruct((rows, C_CONV), BF16)],
        scratch_shapes=[pltpu.VMEM((lp + CHUNK, C_CONV), F32)],
        compiler_params=_params(("parallel",)),
    )(uc, conv_w, conv_b, ln_g, ln_b, token)


def _bwd_conv(uc, ypre, dyc, conv_w, ln_g, ln_b, token, n_ex):
    rows = uc.shape[0]
    lp = rows // n_ex
    n_chunk = lp // CHUNK

    def body(uc_ref, ypre_ref, dyc_ref, w_ref, lg_ref, lb_ref, token_ref, duc_ref, dw_ref, dvec_ref, vs_ref, dys_ref,
             dwacc_ref):
        _glu_into(uc_ref, vs_ref, n_chunk)
        dys_ref[pl.ds(lp, CHUNK), :] = jnp.zeros((CHUNK, C_CONV), F32)
        dwacc_ref[...] = jnp.zeros_like(dwacc_ref)

        def ln_bwd(i, carry):
            dcb, dlg, dlb = carry
            base = pl.multiple_of(i * LN_ROWS, LN_ROWS)
            y = ypre_ref[pl.ds(base, LN_ROWS), :]
            mu = jnp.mean(y, axis=-1, keepdims=True)
            yc_ = y - mu
            rstd = lax.rsqrt(jnp.mean(yc_ * yc_, axis=-1, keepdims=True) + LN_EPS)
            xh = yc_ * rstd
            s = xh * lg_ref[...] + lb_ref[...]
            sg = _sigmoid(s)
            ds = dyc_ref[pl.ds(base, LN_ROWS), :] * (sg * (1.0 + s * (1.0 - sg)))
            dxh = ds * lg_ref[...]
            dy = rstd * (dxh - jnp.mean(dxh, axis=-1, keepdims=True) - xh * jnp.mean(dxh * xh, axis=-1, keepdims=True))
            dys_ref[pl.ds(base, LN_ROWS), :] = dy
            return (dcb + jnp.sum(dy, axis=0, keepdims=True), dlg + jnp.sum(ds * xh, axis=0, keepdims=True),
                    dlb + jnp.sum(ds, axis=0, keepdims=True))

        zero = jnp.zeros((1, C_CONV), F32)
        dcb, dlg, dlb = lax.fori_loop(0, lp // LN_ROWS, ln_bwd, (zero, zero, zero))

        @pl.when(pl.program_id(0) == 0)
        def _():
            dvec_ref[...] = jnp.zeros_like(dvec_ref)
            dw_ref[...] = jnp.zeros_like(dw_ref)

        dvec_ref[0:1, :] += dcb
        dvec_ref[1:2, :] += dlg
        dvec_ref[2:3, :] += dlb

        def taps(i, carry):
            base = pl.multiple_of(i * CHUNK, CHUNK)
            for lb in range(C_CONV // LANES):
                ls = slice(lb * LANES, (lb + 1) * LANES)
                dwin = dys_ref[pl.ds(base, CHUNK + HALO), ls]
                vwin = vs_ref[pl.ds(base + CHUNK - HALO, CHUNK + HALO), ls]
                dy = dwin[0:CHUNK]
                acc = jnp.zeros((CHUNK, LANES), F32)
                for j, rows_j in _shifted(dwin, [CONV_W - 1 - j for j in range(CONV_W)]):
                    acc = acc + w_ref[j:j + 1, ls] * rows_j
                for j, rows_j in _shifted(vwin, [HALO - (CONV_W - 1) + j for j in range(CONV_W)]):
                    dwacc_ref[8 * j:8 * j + 8, ls] += jnp.sum((dy * rows_j).reshape(CHUNK // 8, 8, LANES), axis=0)
                val = uc_ref[pl.ds(base, CHUNK), ls].astype(F32)
                gate = uc_ref[pl.ds(base, CHUNK), C_CONV + lb * LANES:C_CONV + (lb + 1) * LANES].astype(F32)
                sg = _sigmoid(gate)
                duc_ref[pl.ds(base, CHUNK), ls] = (acc * sg).astype(BF16)
                duc_ref[pl.ds(base, CHUNK), C_CONV + lb * LANES:C_CONV + (lb + 1) * LANES] = (
                    acc * val * sg * (1.0 - sg)).astype(BF16)
            return carry

        lax.fori_loop(0, n_chunk, taps, 0, unroll=3)
        for j in range(CONV_W):
            dw_ref[j:j + 1, :] += jnp.sum(dwacc_ref[8 * j:8 * j + 8, :], axis=0, keepdims=True)

    ex = lambda w: pl.BlockSpec((lp, w), lambda b: (b, 0))
    return pl.pallas_call(
        body, name="bwd_conv", grid=(n_ex,),
        in_specs=[ex(2 * C_CONV), ex(C_CONV), ex(C_CONV), _fixed((32, C_CONV)), _fixed((1, C_CONV)), _fixed((1, C_CONV)),
                  _fixed((8, 128))],
        out_specs=[ex(2 * C_CONV), _fixed((32, C_CONV)), _fixed((8, C_CONV))],
        out_shape=[jax.ShapeDtypeStruct((rows, 2 * C_CONV), BF16), jax.ShapeDtypeStruct((32, C_CONV), F32),
                   jax.ShapeDtypeStruct((8, C_CONV), F32)],
        scratch_shapes=[pltpu.VMEM((lp + CHUNK, C_CONV), F32), pltpu.VMEM((lp + CHUNK, C_CONV), F32),
                        pltpu.VMEM((8 * 32, C_CONV), F32)],
        compiler_params=_params(("arbitrary",)),
    )(uc, ypre, dyc, conv_w, ln_g, ln_b, token)


def _seg_chunks(n_chunk):
    return max(c for c in (11, 3, 1) if n_chunk % c == 0)


def _block_mask(shape, row_block, lane_block):
    return (lax.broadcasted_iota(jnp.int32, shape, 0) // row_block) == (lax.broadcasted_iota(jnp.int32, shape, 1) // lane_block)


def _per_head_rows(x, mask):
    return jnp.where(mask, jnp.concatenate([x] * GLA_H, axis=0), 0)


def _fold_heads(full, lane_block):
    lane = lax.broadcasted_iota(jnp.int32, (1, full.shape[1]), 1) // lane_block
    out = jnp.where(lane == 0, full[0:CHUNK], 0.0)
    for h in range(1, GLA_H):
        out = out + jnp.where(lane == h, full[h * CHUNK:(h + 1) * CHUNK], 0.0)
    return out


def _causal_heads():
    return (lax.broadcasted_iota(jnp.int32, (CHUNK, GLA_H * CHUNK), 1) % CHUNK) <= lax.broadcasted_iota(
        jnp.int32, (CHUNK, GLA_H * CHUNK), 0)


def _cumsum_rows(x):
    row = lax.broadcasted_iota(jnp.int32, x.shape, 0)
    s = 1
    while s < CHUNK:
        x = x + jnp.where(row >= s, pltpu.roll(x, s, 0), 0.0)
        s *= 2
    return x


def _rev_cumsum_rows(x):
    row = lax.broadcasted_iota(jnp.int32, x.shape, 0)
    s = 1
    while s < CHUNK:
        x = x + jnp.where(row < CHUNK - s, pltpu.roll(x, CHUNK - s, 0), 0.0)
        s *= 2
    return x


def _gate_terms(lr_ref, w2_ref, gb_ref, rs, first_pos):
    z = _dot(lr_ref[rs, :].astype(BF16), w2_ref[...]) + gb_ref[...]
    la = (jnp.minimum(z, 0.0) - jnp.log(1.0 + jnp.exp(-jnp.abs(z)))) * (1.0 / TAU)
    pos = first_pos + lax.broadcasted_iota(jnp.int32, (CHUNK, 1), 0)
    live = pos >= ZROWS
    la = jnp.where(live, la, 0.0)
    return z, live, _cumsum_rows(la)


def _fwd_gla(qk, vg, lr, w2p, gb, ng, token, n_ex):
    rows = qk.shape[0]
    lp = rows // n_ex
    n_chunk = lp // CHUNK
    sc = _seg_chunks(n_chunk)
    n_seg = n_chunk // sc
    seg = sc * CHUNK

    def body(qk_ref, vg_ref, lr_ref, w2_ref, gb_ref, ng_ref, token_ref, yg_ref, o_ref, st_ref, state_ref):
        sidx = pl.program_id(1)

        @pl.when(sidx == 0)
        def _():
            state_ref[...] = jnp.zeros_like(state_ref)

        causal = _causal_heads()
        k_mask = _block_mask((GLA_H * CHUNK, GLA_K), CHUNK, GLA_DK)
        v_mask = _block_mask((GLA_H * CHUNK, GLA_V), CHUNK, GLA_DV)
        s_mask = _block_mask((GLA_V, GLA_K), GLA_DV, GLA_DK)

        def chunk(ci, carry):
            base = pl.multiple_of(ci * CHUNK, CHUNK)
            rs = pl.ds(base, CHUNK)
            _, _, bcum = _gate_terms(lr_ref, w2_ref, gb_ref, rs, (sidx * sc + ci) * CHUNK)
            bl = bcum[CHUNK - 1:CHUNK, :]
            q = qk_ref[rs, 0:GLA_K].astype(F32)
            k = qk_ref[rs, GLA_K:2 * GLA_K].astype(F32)
            qt = (q * (GLA_DK ** -0.5) * jnp.exp(bcum)).astype(BF16)
            kt = (k * jnp.exp(-bcum)).astype(BF16)
            kh = (k * jnp.exp(bl - bcum)).astype(BF16)
            vb = vg_ref[rs, 0:GLA_V].astype(BF16)
            state = state_ref[...]
            st_ref[ci] = state
            a = jnp.where(causal, _dot(qt, _per_head_rows(kt, k_mask), _NT), 0.0)
            o = _dot(a.astype(BF16), _per_head_rows(vb, v_mask)) + _dot(qt, state.astype(BF16), _NT)
            o_ref[rs, :] = o
            for h in range(GLA_H):
                hs = slice(h * GLA_DV, (h + 1) * GLA_DV)
                oh = o[:, hs]
                ro = lax.rsqrt(jnp.mean(oh * oh, axis=-1, keepdims=True) + RMS_EPS)
                g = vg_ref[rs, GLA_V + h * GLA_DV:GLA_V + (h + 1) * GLA_DV].astype(F32)
                yg_ref[rs, hs] = (oh * ro * ng_ref[...] * g * _sigmoid(g)).astype(BF16)
            state_ref[...] = state * jnp.exp(bl) + jnp.where(s_mask, _dot(vb, kh, _TN), 0.0)
            return carry

        lax.fori_loop(0, sc, chunk, 0, unroll=True)

    sg = lambda w: pl.BlockSpec((seg, w), lambda b, s: (b * n_seg + s, 0))
    return pl.pallas_call(
        body, name="fwd_gla", grid=(n_ex, n_seg),
        in_specs=[sg(2 * GLA_K), sg(2 * GLA_V), sg(RANK_P), _fixed((RANK_P, GLA_K)), _fixed((1, GLA_K)), _fixed((1, GLA_DV)),
                  _fixed((8, 128))],
        out_specs=[sg(GLA_V), sg(GLA_V), pl.BlockSpec((sc, GLA_V, GLA_K), lambda b, s: (b * n_seg + s, 0, 0))],
        out_shape=[jax.ShapeDtypeStruct((rows, GLA_V), BF16), jax.ShapeDtypeStruct((rows, GLA_V), F32),
                   jax.ShapeDtypeStruct((n_ex * n_chunk, GLA_V, GLA_K), F32)],
        scratch_shapes=[pltpu.VMEM((GLA_V, GLA_K), F32)],
        compiler_params=_params(("parallel", "arbitrary")),
    )(qk, vg, lr, w2p, gb, ng, token)


def _bwd_gla(qk, vg, lr, o, st, dyg, w2p, gb, ng, yc, yg, dh1b, token, n_ex):
    rows = qk.shape[0]
    lp = rows // n_ex
    n_chunk = lp // CHUNK
    sc = _seg_chunks(n_chunk)
    n_seg = n_chunk // sc
    seg = sc * CHUNK

    def body(qk_ref, vg_ref, lr_ref, o_ref, st_ref, dyg_ref, w2_ref, gb_ref, ng_ref, yc_ref, yg_ref, dh1_ref, token_ref,
             dqk_ref, dvg_ref, dlr_ref, dw2_ref, dvec_ref, dwo_ref, gt_ref, dz_ref, dwo_acc):
        step = pl.program_id(1)
        sidx = n_seg - 1 - step
        first = (step == 0) & (pl.program_id(0) == 0)

        @pl.when(step == 0)
        def _():
            gt_ref[...] = jnp.zeros_like(gt_ref)

        @pl.when(first)
        def _():
            dw2_ref[...] = jnp.zeros_like(dw2_ref)
            dvec_ref[...] = jnp.zeros_like(dvec_ref)
            dwo_acc[...] = jnp.zeros_like(dwo_acc)

        d1 = dh1_ref[...]
        dwo_acc[0:C_CONV, :] += _dot(yc_ref[...], d1, _TN)
        dwo_acc[C_CONV:D, :] += _dot(yg_ref[...], d1, _TN)

        @pl.when((step == n_seg - 1) & (pl.program_id(0) == n_ex - 1))
        def _():
            dwo_ref[...] = dwo_acc[...].astype(BF16)

        causal = _causal_heads()
        k_mask = _block_mask((GLA_H * CHUNK, GLA_K), CHUNK, GLA_DK)
        v_mask = _block_mask((GLA_H * CHUNK, GLA_V), CHUNK, GLA_DV)
        s_mask = _block_mask((GLA_V, GLA_K), GLA_DV, GLA_DK)
        last_row = lax.broadcasted_iota(jnp.int32, (CHUNK, 1), 0) == CHUNK - 1
        ng = ng_ref[...]

        def chunk(ii, dng):
            ci = sc - 1 - ii
            base = pl.multiple_of(ci * CHUNK, CHUNK)
            rs = pl.ds(base, CHUNK)
            z, live, bcum = _gate_terms(lr_ref, w2_ref, gb_ref, rs, (sidx * sc + ci) * CHUNK)
            bl = bcum[CHUNK - 1:CHUNK, :]
            ebl = jnp.exp(bl)
            q = qk_ref[rs, 0:GLA_K].astype(F32)
            k = qk_ref[rs, GLA_K:2 * GLA_K].astype(F32)
            eb = jnp.exp(bcum)
            enb = jnp.exp(-bcum)
            ehb = jnp.exp(bl - bcum)
            qt = q * (GLA_DK ** -0.5) * eb
            kt = k * enb
            kh = k * ehb
            qtb = qt.astype(BF16)
            vb = vg_ref[rs, 0:GLA_V].astype(BF16)
            k_rows = _per_head_rows(kt.astype(BF16), k_mask)
            v_rows = _per_head_rows(vb, v_mask)
            gt = gt_ref[...]
            gtb = gt.astype(BF16)
            s_in = st_ref[ci]
            dos = []
            for h in range(GLA_H):
                hs = slice(h * GLA_DV, (h + 1) * GLA_DV)
                gs = slice(GLA_V + h * GLA_DV, GLA_V + (h + 1) * GLA_DV)
                oh = o_ref[rs, hs]
                ro = lax.rsqrt(jnp.mean(oh * oh, axis=-1, keepdims=True) + RMS_EPS)
                on = oh * ro
                g = vg_ref[rs, gs].astype(F32)
                sg = _sigmoid(g)
                dout = dyg_ref[rs, hs]
                dvg_ref[rs, gs] = (dout * on * ng * (sg * (1.0 + g * (1.0 - sg)))).astype(BF16)
                dw = dout * g * sg
                dng = dng + jnp.sum(dw * on, axis=0, keepdims=True)
                don = dw * ng
                dos.append((ro * (don - on * jnp.mean(don * on, axis=-1, keepdims=True))).astype(BF16))
            dob = jnp.concatenate(dos, axis=1)
            a = jnp.where(causal, _dot(qtb, k_rows, _NT), 0.0).astype(BF16)
            da = jnp.where(causal, _dot(dob, v_rows, _NT), 0.0).astype(BF16)
            dv = _fold_heads(_dot(a, dob, _TN), GLA_DV) + _dot(kh.astype(BF16), gtb, _NT)
            dvg_ref[rs, 0:GLA_V] = dv.astype(BF16)
            dkh = _dot(vb, gtb)
            dqt = _dot(da, k_rows) + _dot(dob, s_in.astype(BF16))
            dkt = _fold_heads(_dot(da, qtb, _TN), GLA_DK)
            dbl = jnp.sum(gt * s_in, axis=0, keepdims=True) * ebl + jnp.sum(dkh * kh, axis=0, keepdims=True)
            dqk_ref[rs, 0:GLA_K] = (dqt * (GLA_DK ** -0.5) * eb).astype(BF16)
            dqk_ref[rs, GLA_K:2 * GLA_K] = (dkt * enb + dkh * ehb).astype(BF16)
            db = dqt * qt - dkt * kt - dkh * kh
            db = jnp.where(last_row, db + dbl, db)
            dla = jnp.where(live, _rev_cumsum_rows(db), 0.0)
            dz_ref[rs, :] = dla * (1.0 / TAU) * (1.0 - _sigmoid(z))
            gt_ref[...] = jnp.where(s_mask, _dot(dob, qtb, _TN), 0.0) + gt * ebl
            return dng

        dng = lax.fori_loop(0, sc, chunk, jnp.zeros((1, GLA_DV), F32), unroll=True)
        dz = dz_ref[...]
        dzb = dz.astype(BF16)
        dlr_ref[...] = _dot(dzb, w2_ref[...], _NT).astype(BF16)
        dw2_ref[...] += _dot(lr_ref[...].astype(BF16), dzb, _TN)
        dvec_ref[0:1, :] += jnp.sum(dz, axis=0, keepdims=True)
        dvec_ref[1:2, 0:GLA_DV] += dng

    sg_ = lambda w: pl.BlockSpec((seg, w), lambda b, s: (b * n_seg + n_seg - 1 - s, 0))
    return pl.pallas_call(
        body, name="bwd_gla", grid=(n_ex, n_seg),
        in_specs=[sg_(2 * GLA_K), sg_(2 * GLA_V), sg_(RANK_P), sg_(GLA_V),
                  pl.BlockSpec((sc, GLA_V, GLA_K), lambda b, s: (b * n_seg + n_seg - 1 - s, 0, 0)), sg_(GLA_V),
                  _fixed((RANK_P, GLA_K)), _fixed((1, GLA_K)), _fixed((1, GLA_DV)), sg_(C_CONV), sg_(GLA_V), sg_(D),
                  _fixed((8, 128))],
        out_specs=[sg_(2 * GLA_K), sg_(2 * GLA_V), sg_(RANK_P), _fixed((RANK_P, GLA_K)), _fixed((8, GLA_K)),
                   _fixed((D, D))],
        out_shape=[jax.ShapeDtypeStruct((rows, 2 * GLA_K), BF16), jax.ShapeDtypeStruct((rows, 2 * GLA_V), BF16),
                   jax.ShapeDtypeStruct((rows, RANK_P), BF16), jax.ShapeDtypeStruct((RANK_P, GLA_K), F32),
                   jax.ShapeDtypeStruct((8, GLA_K), F32), jax.ShapeDtypeStruct((D, D), BF16)],
        scratch_shapes=[pltpu.VMEM((GLA_V, GLA_K), F32), pltpu.VMEM((seg, GLA_K), F32), pltpu.VMEM((D, D), F32)],
        compiler_params=_params(("arbitrary", "arbitrary")),
    )(qk, vg, lr, o, st, dyg, w2p, gb, ng, yc, yg, dh1b, token)


def _pad_rows(x, tgt):
    return jnp.pad(x, ((0, 0), (LEAD, 0), (0, 0))), jnp.pad(tgt, ((0, 0), (LEAD, 0), (0, 0)))


def _local_step(h0, tgt_p, p, pass_on, late_weights, send_early):
    n_ex, lp, _ = h0.shape
    rows = n_ex * lp
    meta = jnp.broadcast_to(p["meta"][None], (n_ex, N_META, D))
    h0 = lax.dynamic_update_slice(h0, meta, (0, ZROWS, 0)).reshape(rows, D)
    tgt_p = tgt_p.reshape(rows, D)

    uc, qk, vg, lr, n1 = _fwd_inproj(h0, p["g1"], p["w_in"])
    ypre, yc = _fwd_conv(uc, p["conv_w"], p["conv_b"], p["ln_g"], p["ln_b"], p["token"], n_ex)
    token = pass_on(yc)
    yg, o, st = _fwd_gla(qk, vg, lr, p["w2"], p["gb"], p["ng"], token, n_ex)
    w_out, wg, wu, wd = late_weights(yg)
    n2, f, da, db, dh2, dh1, dh1b, dyc, dyg, part = _mid_rows(
        yc, yg, h0, tgt_p, w_out, wg, wu, wd, p["g2"], p["g3"], token, lp)
    g = {}
    token = send_early("ffn", [_matmul_tn(a_, b_, name).reshape(N_DEV, FF_S, D) for a_, b_, name in (
        (da, n2, "dw_gate"), (db, n2, "dw_up"), (f, dh2, "dw_down"))])
    dqk, dvg, dlr, g["w2"], g["gla_vec"], dw_out = _bwd_gla(
        qk, vg, lr, o, st, dyg, p["w2"], p["gb"], p["ng"], yc, yg, dh1b, token, n_ex)
    token = send_early("out", [dw_out.reshape(N_DEV, W_OUT_S, D)])
    duc, g["conv_w"], g["conv_vec"] = _bwd_conv(uc, ypre, dyc, p["conv_w"], p["ln_g"], p["ln_b"], token, n_ex)
    token = send_early("in", [_dw_blocked(n1, [duc, dqk, dvg, dlr], W_IN_S, "dw_in")])
    grad_x, g["in_vec"], g["meta"] = _bwd_inproj(duc, dqk, dvg, dlr, dh1, h0, p["w_in"], p["g1"], token, lp)
    g["ffn_vec"] = part
    return grad_x, g


W_IN_S = D_IN // N_DEV
W_OUT_S = D // N_DEV
FF_S = D_FF // N_DEV
CONV_S = C_CONV // N_DEV
GATE_S = GLA_K // N_DEV
SMALL_PACK = 64
CONV_ROW = 16
GATE_ROW = 48
VEC_ROWS = 16
_VEC_ROWS = (("norm_mix_g", D), ("conv_b", C_CONV), ("conv_ln_g", C_CONV), ("conv_ln_b", C_CONV), ("gla_gate_b", GLA_K),
             ("gla_norm_g", GLA_DV), ("norm_ffn_g", D), ("norm_final_g", D))
LOSS_ROW = len(_VEC_ROWS)


def _position():
    return lax.axis_index("x"), lax.axis_index("y"), lax.axis_index("c")


def _any():
    return pl.BlockSpec(memory_space=pl.ANY)


def _stage(mats, meta, conv_w, w2):
    n_t = len(mats) + 1

    def body(*refs):
        ins = refs[0:n_t - 1]
        meta_ref, cw_ref, w2_ref = refs[n_t - 1:n_t + 2]
        lands = refs[n_t + 2:2 * n_t + 2]
        shards = refs[2 * n_t + 2:3 * n_t + 2]
        sems = refs[3 * n_t + 2]
        for s_ref, w_ref in zip(shards, ins):
            s_ref[...] = w_ref[...].astype(BF16)
        sp = shards[n_t - 1]
        sp[...] = jnp.zeros_like(sp)
        sp[0:N_META, :] = meta_ref[...]
        sp[CONV_ROW:CONV_ROW + CONV_W, 0:CONV_S] = cw_ref[...]
        sp[GATE_ROW:GATE_ROW + RANK, 0:GATE_S] = w2_ref[...]
        x, y, c = _position()
        mine = [pltpu.make_async_copy(shards[t], lands[t].at[4 * x + 2 * y + c], sems.at[t]) for t in range(n_t)]
        for cp in mine:
            cp.start()
        for cp in mine:
            cp.wait()

    shard_shapes = [jax.ShapeDtypeStruct(m.shape, BF16) for m in mats] + [jax.ShapeDtypeStruct((SMALL_PACK, 128), F32)]
    res = pl.pallas_call(
        body, name="stage",
        out_shape=[jax.ShapeDtypeStruct((N_DEV,) + s.shape, s.dtype) for s in shard_shapes] + shard_shapes,
        in_specs=[_whole_vmem()] * (n_t + 2), out_specs=[_any()] * n_t + [_whole_vmem()] * n_t,
        scratch_shapes=[pltpu.SemaphoreType.DMA((n_t,))],
        compiler_params=pltpu.CompilerParams(vmem_limit_bytes=VMEM_LIMIT),
    )(*mats, meta, conv_w, w2)
    return res[0:n_t], res[n_t:]


_HBM = pl.BlockSpec(memory_space=pltpu.HBM)
_SEM = pl.BlockSpec(memory_space=pltpu.SEMAPHORE)
_EFFECT = pltpu.SideEffectType.DATAFLOW_SIDE_EFFECTING


_N_ROUTES = {"scatter": 7, "first": 4, "forward": 3}


def _routes(mode):
    x, y, c = _position()
    me = 4 * x + 2 * y + c
    if mode == "scatter":
        out = []
        for k in range(1, N_DEV):
            px = 1 - x if k & 4 else x
            py = 1 - y if k & 2 else y
            pc = 1 - c if k & 1 else c
            out.append(((px, py, pc), 4 * px + 2 * py + pc, me))
        return out
    if mode == "first":
        return [(pos, None, me) for pos in ((x, y, 1 - c), (1 - x, y, c), (x, 1 - y, c), (1 - x, 1 - y, c))]
    assert mode == "forward"
    return [((x, y, 1 - c), 4 * px + 2 * py + c, 4 * px + 2 * py + c) for px, py in ((1 - x, y), (x, 1 - y), (1 - x, 1 - y))]


def _route_copies(mode, n, src_refs, land_refs, send_sems, recv_sems):
    nr = _N_ROUTES[mode]
    for i, (pos, src_blk, dst_blk) in enumerate(_routes(mode)):
        for t in range(n):
            src = land_refs[t] if mode == "forward" else src_refs[t]
            yield pltpu.make_async_remote_copy(
                src_ref=src if src_blk is None else src.at[src_blk], dst_ref=land_refs[t].at[dst_blk],
                send_sem=send_sems.at[nr * t + i], recv_sem=recv_sems.at[nr * t + i], device_id=pos, device_id_type=MESH)


def _in_hbm(a):
    return pltpu.with_memory_space_constraint(a, pltpu.HBM)


def _send_start(name, srcs, lands, mode, after):
    n, ns = len(lands), len(srcs)
    nsem = _N_ROUTES[mode] * n

    def body(*refs):
        src_refs, land_refs = refs[0:ns], refs[ns:ns + n]
        send_sems, recv_sems = refs[ns + n + 1:ns + n + 3]
        token = refs[2 * (ns + n) + 3]
        for cp in _route_copies(mode, n, src_refs, land_refs, send_sems, recv_sems):
            cp.start()
        token[...] = jnp.zeros_like(token)

    bufs = list(srcs) + list(lands)
    res = pl.pallas_call(
        body, name=name,
        out_shape=(pltpu.SemaphoreType.DMA((nsem,)), pltpu.SemaphoreType.DMA((nsem,)),
                   *[pltpu.HBM(b.shape, b.dtype) for b in bufs], jax.ShapeDtypeStruct((8, 128), F32)),
        in_specs=[_HBM] * len(bufs) + [_any()], out_specs=(_SEM, _SEM, *[_HBM] * len(bufs), _whole_vmem()),
        input_output_aliases={i: 2 + i for i in range(len(bufs))},
        compiler_params=pltpu.CompilerParams(has_side_effects=_EFFECT),
    )(*[_in_hbm(b) for b in bufs], after)
    return res[0], res[1], res[2:2 + ns], res[2 + ns:2 + ns + n], res[2 + ns + n]


def _send_wait(name, send_sems, recv_sems, srcs, lands, mode, after):
    n, ns = len(lands), len(srcs)
    after = after if isinstance(after, tuple) else (after,)

    def body(*refs):
        src_refs, land_refs = refs[0:ns], refs[ns:ns + n]
        send_sems, recv_sems = refs[ns + n:ns + n + 2]
        for cp in _route_copies(mode, n, src_refs, land_refs, send_sems, recv_sems):
            cp.wait_send()
            cp.wait_recv()

    bufs = list(srcs) + list(lands)
    res = pl.pallas_call(
        body, name=name,
        out_shape=tuple(pltpu.HBM(b.shape, b.dtype) for b in bufs),
        in_specs=[_HBM] * len(bufs) + [_SEM, _SEM] + [_any()] * len(after), out_specs=tuple([_HBM] * len(bufs)),
        input_output_aliases={i: i for i in range(len(bufs))},
        compiler_params=pltpu.CompilerParams(has_side_effects=_EFFECT),
    )(*bufs, send_sems, recv_sems, *after)
    return res[0:ns], res[ns:ns + n]


def _unshard_in(a_in, a_small, token):
    def body(a_ref, s_ref, token_ref, w_ref, meta_ref, cw_ref, w2_ref):
        w_ref[:, D_IN:D_INP] = jnp.zeros((D, D_INP - D_IN), BF16)
        w2_ref[...] = jnp.zeros_like(w2_ref)
        for d in range(N_DEV):
            w_ref[:, d * W_IN_S:(d + 1) * W_IN_S] = a_ref[d]
            meta_ref[:, d * 128:(d + 1) * 128] = s_ref[d, 0:N_META, :]
            cw_ref[:, d * CONV_S:(d + 1) * CONV_S] = s_ref[d, CONV_ROW:CONV_ROW + 32, 0:CONV_S]
            w2_ref[0:RANK, d * GATE_S:(d + 1) * GATE_S] = s_ref[d, GATE_ROW:GATE_ROW + RANK, 0:GATE_S].astype(BF16)

    return pl.pallas_call(
        body, name="unshard_in",
        out_shape=[jax.ShapeDtypeStruct((D, D_INP), BF16), jax.ShapeDtypeStruct((N_META, D), F32),
                   jax.ShapeDtypeStruct((32, C_CONV), F32), jax.ShapeDtypeStruct((RANK_P, GLA_K), BF16)],
        compiler_params=pltpu.CompilerParams(vmem_limit_bytes=VMEM_LIMIT),
    )(a_in, a_small, token)


def _pack_small(g):
    def body(meta_ref, cw_ref, w2_ref, in_vec, ffn_vec, conv_vec, gla_vec, sp, vp):
        sp[...] = jnp.zeros_like(sp)
        vp[...] = jnp.zeros_like(vp)
        for d in range(N_DEV):
            sp[d, 0:N_META, :] = meta_ref[:, d * 128:(d + 1) * 128]
            sp[d, CONV_ROW:CONV_ROW + 32, 0:CONV_S] = cw_ref[:, d * CONV_S:(d + 1) * CONV_S]
            sp[d, GATE_ROW:GATE_ROW + RANK, 0:GATE_S] = w2_ref[0:RANK, d * GATE_S:(d + 1) * GATE_S]
            vp[d, 0:1, :] = in_vec[0:1, :]
            vp[d, 1:4, 0:C_CONV] = conv_vec[0:3, :]
            vp[d, 4:5, 0:GLA_K] = gla_vec[0:1, :]
            vp[d, 5:6, 0:GLA_DV] = gla_vec[1:2, 0:GLA_DV]
            vp[d, 6:7, :] = ffn_vec[1:2, :]
            vp[d, 7:8, :] = ffn_vec[0:1, :]
            vp[d, LOSS_ROW:LOSS_ROW + 1, :] = ffn_vec[2:3, :]

    return pl.pallas_call(
        body, name="pack_small",
        out_shape=[jax.ShapeDtypeStruct((N_DEV, SMALL_PACK, 128), F32), jax.ShapeDtypeStruct((N_DEV, VEC_ROWS, D), F32)],
    )(g["meta"], g["conv_w"], g["w2"], g["in_vec"], g["ffn_vec"], g["conv_vec"], g["gla_vec"])


def _adamw(w, g, m, v):
    m = ADAM_B1 * m + (1.0 - ADAM_B1) * g
    v = ADAM_B2 * v + (1.0 - ADAM_B2) * (g * g)
    m_hat = m / (1.0 - ADAM_B1 ** ADAM_STEP)
    v_hat = v / (1.0 - ADAM_B2 ** ADAM_STEP)
    return -ADAM_LR * (m_hat / (jnp.sqrt(v_hat) + ADAM_EPS) + ADAM_WD * w), m, v


def _update_matrix(recv, own, me, w, m, v, name):
    _, r, c = recv.shape
    tr = _row_tile(r, 256)

    def body(me_ref, recv_ref, own_ref, w_ref, m_ref, v_ref, g_ref, d_ref, nm_ref, nv_ref):
        g = jnp.zeros((tr, c), F32)
        for s in range(N_DEV):
            g = g + jnp.where(me_ref[0] == s, own_ref[...], recv_ref[s]).astype(F32)
        g_ref[...] = g
        d_ref[...], nm_ref[...], nv_ref[...] = _adamw(w_ref[...], g, m_ref[...], v_ref[...])

    one = pl.BlockSpec((None, tr, c), lambda i, me_ref: (0, i, 0))
    return pl.pallas_call(
        body, name=name,
        grid_spec=pltpu.PrefetchScalarGridSpec(
            num_scalar_prefetch=1, grid=(r // tr,),
            in_specs=[pl.BlockSpec((N_DEV, tr, c), lambda i, me_ref: (0, i, 0)),
                      pl.BlockSpec((None, tr, c), lambda i, me_ref: (me_ref[0], i, 0)), one, one, one],
            out_specs=[one] * 4),
        out_shape=[jax.ShapeDtypeStruct((1, r, c), F32)] * 4,
        compiler_params=_params(("parallel",)),
    )(me, recv, own, w, m, v)


_SMALL = ("meta_tokens", "conv_w", "gla_w_gate2") + tuple(n for n, _ in _VEC_ROWS)


def _update_small(me, srecv, vrecv, sown, vown, w, m, v):
    n = len(_SMALL)

    def body(*refs):
        me_ref, s_ref, v_ref, so_ref, vo_ref = refs[0:5]
        w_refs, m_refs, v_refs = refs[5:5 + n], refs[5 + n:5 + 2 * n], refs[5 + 2 * n:5 + 3 * n]
        outs = refs[5 + 3 * n:]
        ssum = jnp.zeros((SMALL_PACK, 128), F32)
        vsum = jnp.zeros((VEC_ROWS, D), F32)
        for s in range(N_DEV):
            ssum = ssum + jnp.where(me_ref[0] == s, so_ref[s], s_ref[s])
            vsum = vsum + jnp.where(me_ref[0] == s, vo_ref[s], v_ref[s])
        grads = [ssum[0:N_META, :], ssum[CONV_ROW:CONV_ROW + CONV_W, 0:CONV_S], ssum[GATE_ROW:GATE_ROW + RANK, 0:GATE_S]]
        grads += [vsum[i:i + 1, 0:width] for i, (_, width) in enumerate(_VEC_ROWS)]
        for i, g in enumerate(grads):
            d, nm, nv = _adamw(w_refs[i][...], g, m_refs[i][...], v_refs[i][...])
            outs[i][...] = g
            outs[n + i][...] = d
            outs[2 * n + i][...] = nm
            outs[3 * n + i][...] = nv
        outs[4 * n][...] = vsum[LOSS_ROW:LOSS_ROW + 1, 0:128]

    shapes = [jax.ShapeDtypeStruct(t.shape, F32) for t in w]
    res = pl.pallas_call(
        body, name="update_small", out_shape=shapes * 4 + [jax.ShapeDtypeStruct((1, 128), F32)],
        in_specs=[pl.BlockSpec(memory_space=pltpu.SMEM)] + [_whole_vmem()] * (4 + 3 * n),
    )(me, srecv, vrecv, sown, vown, *w, *m, *v)
    return res[0:n], res[n:2 * n], res[2 * n:3 * n], res[3 * n:4 * n], res[4 * n]


_WEIGHTS = ("meta_tokens", "norm_mix_g", "w_in", "conv_w", "conv_b", "conv_ln_g", "conv_ln_b", "gla_w_gate2", "gla_gate_b",
            "gla_norm_g", "w_out", "norm_ffn_g", "w_ffn_gate", "w_ffn_up", "w_ffn_down", "norm_final_g")
_MATRICES = ("w_in", "w_out", "w_ffn_gate", "w_ffn_up", "w_ffn_down")
_TRANSPOSED = ("w_ffn_gate", "w_ffn_up")


def kernel(x, meta_tokens, norm_mix_g, w_in, conv_w, conv_b, conv_ln_g, conv_ln_b, gla_w_gate2, gla_gate_b, gla_norm_g, w_out, norm_ffn_g, w_ffn_gate, w_ffn_up, w_ffn_down, norm_final_g, loss_target, m_meta_tokens, m_norm_mix_g, m_w_in, m_conv_w, m_conv_b, m_conv_ln_g, m_conv_ln_b, m_gla_w_gate2, m_gla_gate_b, m_gla_norm_g, m_w_out, m_norm_ffn_g, m_w_ffn_gate, m_w_ffn_up, m_w_ffn_down, m_norm_final_g, v_meta_tokens, v_norm_mix_g, v_w_in, v_conv_w, v_conv_b, v_conv_ln_g, v_conv_ln_b, v_gla_w_gate2, v_gla_gate_b, v_gla_norm_g, v_w_out, v_norm_ffn_g, v_w_ffn_gate, v_w_ffn_up, v_w_ffn_down, v_norm_final_g):
    given = dict(locals())
    two_d = lambda a: a.reshape(1, -1) if a.ndim == 1 else a.reshape(a.shape[-2:])
    fams = [{n: given[pre + n] for n in _WEIGHTS} for pre in ("", "m_", "v_")]
    for f in fams:
        for n in _TRANSPOSED:
            f[n] = f[n].transpose(0, 2, 1)
    w = fams[0]

    lands, shards = _stage([two_d(w[n]) for n in _MATRICES], w["meta_tokens"], two_d(w["conv_w"]), two_d(w["gla_w_gate2"]))
    soon, later = (0, 5), (1, 2, 3, 4)
    pick = lambda seq, idx: [seq[i] for i in idx]
    first = _send_start("gather_first_start", pick(shards, soon), pick(lands, soon), "first", norm_mix_g)
    ffn_first = _send_start("gather_ffn_first_start", pick(shards, later), pick(lands, later), "first", first[4])
    h0, tgt_p = _pad_rows(x, loss_target)
    _, arrived = _send_wait("gather_first_wait", *first[0:4], "first", (h0, tgt_p, ffn_first[4]))
    forward = _send_start("gather_forward_start", [], arrived, "forward", ffn_first[4])
    _, (a_in, a_small) = _send_wait("gather_forward_wait", *forward[0:4], "forward", forward[4])
    w_in, meta, conv_taps, w2 = _unshard_in(a_in, a_small, forward[4])
    p = dict(meta=meta, conv_w=conv_taps, w2=w2, w_in=w_in, g1=norm_mix_g, conv_b=conv_b, ln_g=conv_ln_g, ln_b=conv_ln_b,
             gb=gla_gate_b, ng=gla_norm_g, g2=norm_ffn_g, g3=two_d(norm_final_g), token=forward[4])
    passed = {}

    def pass_on(after):
        _, arrived_ffn = _send_wait("gather_ffn_first_wait", *ffn_first[0:4], "first", after)
        passed["sent"] = _send_start("gather_ffn_forward_start", [], arrived_ffn, "forward", after)
        return passed["sent"][4]

    def late_weights(after):
        _, (a_out, a_g, a_u, a_d) = _send_wait("gather_ffn_forward_wait", *passed["sent"][0:4], "forward", after)
        return a_out.reshape(D, D), a_g.reshape(D_FF, D), a_u.reshape(D_FF, D), a_d.reshape(D_FF, D)

    sent = {}

    def send_early(tag, mats):
        landing = [_in_hbm(lax.empty(m_.shape, m_.dtype)) for m_ in mats]
        sent[tag] = _send_start("scatter_" + tag + "_start", mats, landing, "scatter", norm_mix_g)
        return sent[tag][4]

    grad_x, g = _local_step(h0, tgt_p, p, pass_on, late_weights, send_early)

    token = send_early("small", list(_pack_small(g)))
    x_, y_, c_ = _position()
    me = (4 * x_ + 2 * y_ + c_).astype(jnp.int32).reshape(1)
    res = {}
    for tag, names in (("ffn", ("w_ffn_gate", "w_ffn_up", "w_ffn_down")), ("out", ("w_out",)), ("in", ("w_in",))):
        own, recv = _send_wait("scatter_" + tag + "_wait", *sent[tag][0:4], "scatter", token)
        for n, o_, r_ in zip(names, own, recv):
            res[n] = _update_matrix(r_, o_, me, *[f[n] for f in fams], "update_" + n)
            token = res[n][1]
    (sown, vown), (srecv, vrecv) = _send_wait("scatter_small_wait", *sent["small"][0:4], "scatter", token)
    small = _update_small(me, srecv, vrecv, sown, vown, *[[two_d(f[n]) for n in _SMALL] for f in fams])
    for i, n in enumerate(_SMALL):
        res[n] = [fam[i].reshape(w[n].shape) for fam in small[0:4]]
    for n in _TRANSPOSED:
        res[n] = [t.transpose(0, 2, 1) for t in res[n]]
    outs = [small[4][0, 0], grad_x]
    for k in range(4):
        outs += [res[n][k] for n in _WEIGHTS]
    return tuple(outs)
```

```python
import functools

import jax
import jax.numpy as jnp
from jax import lax
from jax.experimental import pallas as pl
from jax.experimental.pallas import tpu as pltpu

F32 = jnp.float32
BF16 = jnp.bfloat16

D = 1024
N_META = 16
C_CONV = 512
CONV_W = 31
GLA_H = 4
GLA_DK = 64
GLA_DV = 128
GLA_K = GLA_H * GLA_DK
GLA_V = GLA_H * GLA_DV
RANK = 16
RANK_P = 128
TAU = 16.0
CHUNK = 64
LEAD = CHUNK
ZROWS = LEAD - N_META
D_IN = 2 * C_CONV + 2 * GLA_K + 2 * GLA_V + RANK
D_INP = D_IN - RANK + RANK_P
D_FF = 2816
FF_CHUNK = 1408
FF_SPLIT = (0, 1536, D_FF)
RMS_EPS = 1e-6
LN_EPS = 1e-5
N_DEV = 8

ADAM_LR = 0.001
ADAM_B1 = 0.9
ADAM_B2 = 0.999
ADAM_EPS = 1e-08
ADAM_WD = 0.01
ADAM_STEP = 10

VMEM_LIMIT = 60 * 1024 * 1024
ROW_TILE = 1056
FFN_ROW_TILE = 352
DW_ROW_TILE = 1408
MESH = pl.DeviceIdType.MESH

_NN = (((1,), (0,)), ((), ()))
_NT = (((1,), (1,)), ((), ()))
_TN = (((0,), (0,)), ((), ()))


def _dot(a, b, dims=_NN):
    return lax.dot_general(a, b, dims, preferred_element_type=F32)


def _sigmoid(x):
    return 1.0 / (1.0 + jnp.exp(-x))


def _row_tile(rows, target):
    best = None
    for t in range(16, min(rows, target) + 1, 16):
        if rows % t == 0:
            best = t
    assert best is not None, rows
    return best


def _params(sem=None):
    return pltpu.CompilerParams(dimension_semantics=sem, vmem_limit_bytes=VMEM_LIMIT)


def _whole_vmem():
    return pl.BlockSpec(memory_space=pltpu.VMEM)


def _rows(tm, width):
    return pl.BlockSpec((tm, width), lambda i: (i, 0))


def _fixed(shape):
    return pl.BlockSpec(shape, lambda *_: (0,) * len(shape))


def _fwd_inproj(h0, g1, w_in):
    rows = h0.shape[0]
    tm = _row_tile(rows, ROW_TILE)

    def body(h_ref, g_ref, w_ref, uc_ref, qk_ref, vg_ref, lr_ref, n1_ref):
        h = h_ref[...]
        r = lax.rsqrt(jnp.mean(h * h, axis=-1, keepdims=True) + RMS_EPS)
        n = (h * r * g_ref[...]).astype(BF16)
        n1_ref[...] = n
        uc_ref[...] = _dot(n, w_ref[:, 0:1024]).astype(BF16)
        qk_ref[...] = _dot(n, w_ref[:, 1024:1536]).astype(BF16)
        vg_ref[...] = _dot(n, w_ref[:, 1536:2560]).astype(BF16)
        lr_ref[...] = _dot(n, w_ref[:, 2560:2688]).astype(BF16)

    return pl.pallas_call(
        body, name="fwd_inproj", grid=(rows // tm,),
        in_specs=[_rows(tm, D), _fixed((1, D)), _whole_vmem()],
        out_specs=[_rows(tm, 1024), _rows(tm, 512), _rows(tm, 1024), _rows(tm, RANK_P), _rows(tm, D)],
        out_shape=[jax.ShapeDtypeStruct((rows, 1024), BF16), jax.ShapeDtypeStruct((rows, 512), BF16),
                   jax.ShapeDtypeStruct((rows, 1024), BF16), jax.ShapeDtypeStruct((rows, RANK_P), BF16),
                   jax.ShapeDtypeStruct((rows, D), BF16)],
        compiler_params=_params(("parallel",)),
    )(h0, g1, w_in)


def _mid_rows(yc, yg, h0, tgt, w_out, wg, wu, wd, g2, g3, token, rows_per_example):
    rows = h0.shape[0]
    tm = _row_tile(rows, FFN_ROW_TILE)
    ff_blocks = [slice(lo, hi) for lo, hi in zip(FF_SPLIT[:-1], FF_SPLIT[1:])]

    def body(yc_ref, yg_ref, h0_ref, t_ref, wo_ref, wg_ref, wu_ref, wd_ref, g2_ref, g3_ref, token_ref,
             n2_ref, f_ref, da_ref, db_ref, dh2_ref, dh1b_ref, dyc_ref, dyg_ref, part_ref):
        i = pl.program_id(0)
        h1 = h0_ref[...] + _dot(yc_ref[...], wo_ref[0:C_CONV, :]) + _dot(yg_ref[...], wo_ref[C_CONV:D, :])
        r2 = lax.rsqrt(jnp.mean(h1 * h1, axis=-1, keepdims=True) + RMS_EPS)
        xh2 = h1 * r2
        n2 = (xh2 * g2_ref[...]).astype(BF16)
        n2_ref[...] = n2
        y2 = jnp.zeros((tm, D), F32)
        for cs in ff_blocks:
            a = _dot(n2, wg_ref[cs, :], _NT)
            b = _dot(n2, wu_ref[cs, :], _NT)
            f = (a * _sigmoid(a) * b).astype(BF16)
            f_ref[:, cs] = f
            da_ref[:, cs] = a.astype(BF16)
            db_ref[:, cs] = b.astype(BF16)
            y2 = y2 + _dot(f, wd_ref[cs, :])
        h2 = h1 + y2
        r3 = lax.rsqrt(jnp.mean(h2 * h2, axis=-1, keepdims=True) + RMS_EPS)
        xh3 = h2 * r3
        g3 = g3_ref[...]
        pos = (i * tm + lax.broadcasted_iota(jnp.int32, (tm, 1), 0)) % rows_per_example
        valid = pos >= LEAD
        err = jnp.where(valid, xh3 * g3 - t_ref[...], 0.0)
        loss = 0.5 / D * jnp.sum(jnp.sum(err * err, axis=-1, keepdims=True), axis=0, keepdims=True)
        dy = err * (1.0 / D)
        dg3 = jnp.sum(dy * xh3, axis=0, keepdims=True)
        dxh = dy * g3
        dh2 = r3 * (dxh - xh3 * jnp.mean(dxh * xh3, axis=-1, keepdims=True))
        dh2b = dh2.astype(BF16)
        dh2_ref[...] = dh2b
        dn2 = jnp.zeros((tm, D), F32)
        for cs in ff_blocks:
            df = _dot(dh2b, wd_ref[cs, :], _NT)
            a = da_ref[:, cs].astype(F32)
            b = db_ref[:, cs].astype(F32)
            sg = _sigmoid(a)
            da = (df * b * sg * (1.0 + a * (1.0 - sg))).astype(BF16)
            db = (df * a * sg).astype(BF16)
            da_ref[:, cs] = da
            db_ref[:, cs] = db
            dn2 = dn2 + _dot(da, wg_ref[cs, :]) + _dot(db, wu_ref[cs, :])
        dg2 = jnp.sum(dn2 * xh2, axis=0, keepdims=True)
        dxh2 = dn2 * g2_ref[...]
        dh1 = dh2 + r2 * (dxh2 - xh2 * jnp.mean(dxh2 * xh2, axis=-1, keepdims=True))
        dh1b = dh1.astype(BF16)
        dh1b_ref[...] = dh1b
        dyc_ref[...] = _dot(dh1b, wo_ref[0:C_CONV, :], _NT)
        dyg_ref[...] = _dot(dh1b, wo_ref[C_CONV:D, :], _NT)

        @pl.when(i == 0)
        def _():
            part_ref[...] = jnp.zeros_like(part_ref)

        part_ref[0:1, :] += dg3
        part_ref[1:2, :] += dg2
        part_ref[2:3, :] += jnp.broadcast_to(loss, (1, D))

    return pl.pallas_call(
        body, name="mid_rows", grid=(rows // tm,),
        in_specs=[_rows(tm, C_CONV), _rows(tm, GLA_V), _rows(tm, D), _rows(tm, D), _whole_vmem(), _whole_vmem(),
                  _whole_vmem(), _whole_vmem(), _fixed((1, D)), _fixed((1, D)), _fixed((8, 128))],
        out_specs=[_rows(tm, D), _rows(tm, D_FF), _rows(tm, D_FF), _rows(tm, D_FF), _rows(tm, D), _rows(tm, D),
                   _rows(tm, C_CONV), _rows(tm, GLA_V), _fixed((8, D))],
        out_shape=[jax.ShapeDtypeStruct((rows, D), BF16)] + [jax.ShapeDtypeStruct((rows, D_FF), BF16)] * 3
        + [jax.ShapeDtypeStruct((rows, D), BF16), jax.ShapeDtypeStruct((rows, D), BF16),
           jax.ShapeDtypeStruct((rows, C_CONV), F32), jax.ShapeDtypeStruct((rows, GLA_V), F32),
           jax.ShapeDtypeStruct((8, D), F32)],
        compiler_params=_params(("arbitrary",)),
    )(yc, yg, h0, tgt, w_out, wg, wu, wd, g2, g3, token)


def _bwd_inproj(duc, dqk, dvg, dlr, dh1, h0, w_in, g1, token, rows_per_example):
    rows = h0.shape[0]
    n_ex = rows // rows_per_example
    tm = _row_tile(rows_per_example, ROW_TILE)
    tiles_per_example = rows_per_example // tm
    n_steps = rows // tm

    def body(duc_ref, dqk_ref, dvg_ref, dlr_ref, dh1_ref, h_ref, w_ref, g_ref, token_ref, gx_ref, part_ref, dmeta_ref,
             buf_ref, sems):
        dn = (_dot(duc_ref[...], w_ref[:, 0:1024], _NT) + _dot(dqk_ref[...], w_ref[:, 1024:1536], _NT)
              + _dot(dvg_ref[...], w_ref[:, 1536:2560], _NT) + _dot(dlr_ref[...], w_ref[:, 2560:2688], _NT))
        h = h_ref[...]
        r = lax.rsqrt(jnp.mean(h * h, axis=-1, keepdims=True) + RMS_EPS)
        xh = h * r
        dg = jnp.sum(dn * xh, axis=0, keepdims=True)
        dxh = dn * g_ref[...]
        dh0 = dh1_ref[...].astype(F32) + r * (dxh - xh * jnp.mean(dxh * xh, axis=-1, keepdims=True))
        i = pl.program_id(0)

        def copies(step):
            slot, b, j = step % 2, step // tiles_per_example, step % tiles_per_example
            out = [(j == 0, pltpu.make_async_copy(buf_ref.at[slot, pl.ds(LEAD, tm - LEAD)],
                                                   gx_ref.at[b, pl.ds(0, tm - LEAD)], sems.at[slot]))]
            if tiles_per_example > 1:
                out.append((j != 0, pltpu.make_async_copy(
                    buf_ref.at[slot], gx_ref.at[b, pl.ds(pl.multiple_of(jnp.maximum(j * tm - LEAD, 0), 8), tm)],
                    sems.at[slot])))
            return out

        def each(step, act):
            for cond, cp in copies(step):
                pl.when(cond)(functools.partial(act, cp))

        @pl.when(i >= 2)
        def _():
            each(i - 2, lambda cp: cp.wait())

        buf_ref[i % 2] = dh0
        each(i, lambda cp: cp.start())

        @pl.when(i == n_steps - 1)
        def _():
            each(i, lambda cp: cp.wait())
            if n_steps > 1:
                each(i - 1, lambda cp: cp.wait())

        @pl.when(i == 0)
        def _():
            part_ref[...] = jnp.zeros_like(part_ref)
            dmeta_ref[...] = jnp.zeros_like(dmeta_ref)

        part_ref[0:1, :] += dg

        @pl.when(i % tiles_per_example == 0)
        def _():
            dmeta_ref[...] += dh0[ZROWS:LEAD, :]

    return pl.pallas_call(
        body, name="bwd_inproj", grid=(n_steps,),
        in_specs=[_rows(tm, 1024), _rows(tm, 512), _rows(tm, 1024), _rows(tm, RANK_P), _rows(tm, D), _rows(tm, D),
                  _whole_vmem(), _fixed((1, D)), _fixed((8, 128))],
        out_specs=[_any(), _fixed((8, D)), _fixed((N_META, D))],
        out_shape=[jax.ShapeDtypeStruct((n_ex, rows_per_example - LEAD, D), F32), jax.ShapeDtypeStruct((8, D), F32),
                   jax.ShapeDtypeStruct((N_META, D), F32)],
        scratch_shapes=[pltpu.VMEM((2, tm, D), F32), pltpu.SemaphoreType.DMA((2,))],
        compiler_params=_params(("arbitrary",)),
    )(duc, dqk, dvg, dlr, dh1, h0, w_in, g1, token)


def _dw_blocked(a, bs, width, name):
    rows, m = a.shape
    ws = [b.shape[1] for b in bs]
    assert sum(ws) >= N_DEV * width
    tk = _row_tile(rows, DW_ROW_TILE)
    nk = rows // tk

    def body(a_ref, *refs):
        b_refs, o_ref, acc_ref = refs[:len(bs)], refs[len(bs)], refs[len(bs) + 1]
        k = pl.program_id(0)

        @pl.when(k == 0)
        def _():
            acc_ref[...] = jnp.zeros_like(acc_ref)

        at = a_ref[...].T
        off = 0
        for b_ref, w in zip(b_refs, ws):
            acc_ref[:, off:off + w] += _dot(at, b_ref[...])
            off += w

        @pl.when(k == nk - 1)
        def _():
            for d in range(N_DEV):
                o_ref[d] = acc_ref[:, d * width:(d + 1) * width].astype(BF16)

    return pl.pallas_call(
        body, name=name, grid=(nk,),
        in_specs=[_rows(tk, m)] + [_rows(tk, w) for w in ws],
        out_specs=_fixed((N_DEV, m, width)),
        out_shape=jax.ShapeDtypeStruct((N_DEV, m, width), BF16),
        scratch_shapes=[pltpu.VMEM((m, sum(ws)), F32)],
        compiler_params=_params(("arbitrary",)),
    )(a, *bs)


def _matmul_tn(a, b, name):
    rows, m = a.shape
    n = b.shape[1]
    tk = _row_tile(rows, DW_ROW_TILE)
    tn = n if n <= 1024 else FF_CHUNK
    tm_ = m if m <= 1024 else FF_CHUNK
    assert n % tn == 0 and m % tm_ == 0
    nk = rows // tk

    def body(a_ref, b_ref, o_ref, acc_ref):
        k = pl.program_id(2)

        @pl.when(k == 0)
        def _():
            acc_ref[...] = jnp.zeros_like(acc_ref)

        acc_ref[...] += _dot(a_ref[...], b_ref[...], _TN)

        @pl.when(k == nk - 1)
        def _():
            o_ref[...] = acc_ref[...].astype(BF16)

    return pl.pallas_call(
        body, name=name, grid=(m // tm_, n // tn, nk),
        in_specs=[pl.BlockSpec((tk, tm_), lambda i, j, k: (k, i)), pl.BlockSpec((tk, tn), lambda i, j, k: (k, j))],
        out_specs=pl.BlockSpec((tm_, tn), lambda i, j, k: (i, j)),
        out_shape=jax.ShapeDtypeStruct((m, n), BF16),
        scratch_shapes=[pltpu.VMEM((tm_, tn), F32)],
        compiler_params=_params(("parallel", "parallel", "arbitrary")),
    )(a, b)


HALO = 32
LN_ROWS = 3 * CHUNK
LANES = 128


def _shifted(win, offsets):
    for r in range(8):
        js = [j for j, k in enumerate(offsets) if k % 8 == r]
        if js:
            rolled = win if r == 0 else pltpu.roll(win, CHUNK + HALO - r, 0)
            for j in js:
                yield j, rolled[offsets[j] - r:offsets[j] - r + CHUNK]


def _glu_into(uc_ref, vs_ref, n_chunk):
    vs_ref[0:CHUNK, :] = jnp.zeros((CHUNK, C_CONV), F32)

    def glu(i, carry):
        base = pl.multiple_of(i * CHUNK, CHUNK)
        val = uc_ref[pl.ds(base, CHUNK), 0:C_CONV].astype(F32)
        gate = uc_ref[pl.ds(base, CHUNK), C_CONV:2 * C_CONV].astype(F32)
        vs_ref[pl.ds(base + CHUNK, CHUNK), :] = val * _sigmoid(gate)
        return carry

    lax.fori_loop(0, n_chunk, glu, 0, unroll=3)


def _fwd_conv(uc, conv_w, conv_b, ln_g, ln_b, token, n_ex):
    rows = uc.shape[0]
    lp = rows // n_ex
    n_chunk = lp // CHUNK

    def body(uc_ref, w_ref, b_ref, lg_ref, lb_ref, token_ref, ypre_ref, yc_ref, vs_ref):
        _glu_into(uc_ref, vs_ref, n_chunk)

        def conv(i, carry):
            base = pl.multiple_of(i * CHUNK, CHUNK)
            for lb in range(C_CONV // LANES):
                ls = slice(lb * LANES, (lb + 1) * LANES)
                win = vs_ref[pl.ds(base + CHUNK - HALO, CHUNK + HALO), ls]
                acc = jnp.broadcast_to(b_ref[:, ls], (CHUNK, LANES))
                for j, rows_j in _shifted(win, [HALO - (CONV_W - 1) + j for j in range(CONV_W)]):
                    acc = acc + w_ref[j:j + 1, ls] * rows_j
                ypre_ref[pl.ds(base, CHUNK), ls] = acc
            y = ypre_ref[pl.ds(base, CHUNK), :]
            mu = jnp.mean(y, axis=-1, keepdims=True)
            yc_ = y - mu
            rstd = lax.rsqrt(jnp.mean(yc_ * yc_, axis=-1, keepdims=True) + LN_EPS)
            s = yc_ * rstd * lg_ref[...] + lb_ref[...]
            yc_ref[pl.ds(base, CHUNK), :] = (s * _sigmoid(s)).astype(BF16)
            return carry

        lax.fori_loop(0, n_chunk, conv, 0, unroll=3)

    ex = lambda w: pl.BlockSpec((lp, w), lambda b: (b, 0))
    return pl.pallas_call(
        body, name="fwd_conv", grid=(n_ex,),
        in_specs=[ex(2 * C_CONV), _fixed((32, C_CONV)), _fixed((1, C_CONV)), _fixed((1, C_CONV)), _fixed((1, C_CONV)),
                  _fixed((8, 128))],
        out_specs=[ex(C_CONV), ex(C_CONV)],
        out_shape=[jax.ShapeDtypeStruct((rows, C_CONV), F32), jax.ShapeDtypeStruct((rows, C_CONV), BF16)],
        scratch_shapes=[pltpu.VMEM((lp + CHUNK, C_CONV), F32)],
        compiler_params=_params(("parallel",)),
    )(uc, conv_w, conv_b, ln_g, ln_b, token)


def _bwd_conv(uc, ypre, dyc, conv_w, ln_g, ln_b, token, n_ex):
    rows = uc.shape[0]
    lp = rows // n_ex
    n_chunk = lp // CHUNK

    def body(uc_ref, ypre_ref, dyc_ref, w_ref, lg_ref, lb_ref, token_ref, duc_ref, dw_ref, dvec_ref, vs_ref, dys_ref,
             dwacc_ref):
        _glu_into(uc_ref, vs_ref, n_chunk)
        dys_ref[pl.ds(lp, CHUNK), :] = jnp.zeros((CHUNK, C_CONV), F32)
        dwacc_ref[...] = jnp.zeros_like(dwacc_ref)

        def ln_bwd(i, carry):
            dcb, dlg, dlb = carry
            base = pl.multiple_of(i * LN_ROWS, LN_ROWS)
            y = ypre_ref[pl.ds(base, LN_ROWS), :]
            mu = jnp.mean(y, axis=-1, keepdims=True)
            yc_ = y - mu
            rstd = lax.rsqrt(jnp.mean(yc_ * yc_, axis=-1, keepdims=True) + LN_EPS)
            xh = yc_ * rstd
            s = xh * lg_ref[...] + lb_ref[...]
            sg = _sigmoid(s)
            ds = dyc_ref[pl.ds(base, LN_ROWS), :] * (sg * (1.0 + s * (1.0 - sg)))
            dxh = ds * lg_ref[...]
            dy = rstd * (dxh - jnp.mean(dxh, axis=-1, keepdims=True) - xh * jnp.mean(dxh * xh, axis=-1, keepdims=True))
            dys_ref[pl.ds(base, LN_ROWS), :] = dy
            return (dcb + jnp.sum(dy, axis=0, keepdims=True), dlg + jnp.sum(ds * xh, axis=0, keepdims=True),
                    dlb + jnp.sum(ds, axis=0, keepdims=True))

        zero = jnp.zeros((1, C_CONV), F32)
        dcb, dlg, dlb = lax.fori_loop(0, lp // LN_ROWS, ln_bwd, (zero, zero, zero))

        @pl.when(pl.program_id(0) == 0)
        def _():
            dvec_ref[...] = jnp.zeros_like(dvec_ref)
            dw_ref[...] = jnp.zeros_like(dw_ref)

        dvec_ref[0:1, :] += dcb
        dvec_ref[1:2, :] += dlg
        dvec_ref[2:3, :] += dlb

        def taps(i, carry):
            base = pl.multiple_of(i * CHUNK, CHUNK)
            for lb in range(C_CONV // LANES):
                ls = slice(lb * LANES, (lb + 1) * LANES)
                dwin = dys_ref[pl.ds(base, CHUNK + HALO), ls]
                vwin = vs_ref[pl.ds(base + CHUNK - HALO, CHUNK + HALO), ls]
                dy = dwin[0:CHUNK]
                acc = jnp.zeros((CHUNK, LANES), F32)
                for j, rows_j in _shifted(dwin, [CONV_W - 1 - j for j in range(CONV_W)]):
                    acc = acc + w_ref[j:j + 1, ls] * rows_j
                for j, rows_j in _shifted(vwin, [HALO - (CONV_W - 1) + j for j in range(CONV_W)]):
                    dwacc_ref[8 * j:8 * j + 8, ls] += jnp.sum((dy * rows_j).reshape(CHUNK // 8, 8, LANES), axis=0)
                val = uc_ref[pl.ds(base, CHUNK), ls].astype(F32)
                gate = uc_ref[pl.ds(base, CHUNK), C_CONV + lb * LANES:C_CONV + (lb + 1) * LANES].astype(F32)
                sg = _sigmoid(gate)
                duc_ref[pl.ds(base, CHUNK), ls] = (acc * sg).astype(BF16)
                duc_ref[pl.ds(base, CHUNK), C_CONV + lb * LANES:C_CONV + (lb + 1) * LANES] = (
                    acc * val * sg * (1.0 - sg)).astype(BF16)
            return carry

        lax.fori_loop(0, n_chunk, taps, 0, unroll=3)
        for j in range(CONV_W):
            dw_ref[j:j + 1, :] += jnp.sum(dwacc_ref[8 * j:8 * j + 8, :], axis=0, keepdims=True)

    ex = lambda w: pl.BlockSpec((lp, w), lambda b: (b, 0))
    return pl.pallas_call(
        body, name="bwd_conv", grid=(n_ex,),
        in_specs=[ex(2 * C_CONV), ex(C_CONV), ex(C_CONV), _fixed((32, C_CONV)), _fixed((1, C_CONV)), _fixed((1, C_CONV)),
                  _fixed((8, 128))],
        out_specs=[ex(2 * C_CONV), _fixed((32, C_CONV)), _fixed((8, C_CONV))],
        out_shape=[jax.ShapeDtypeStruct((rows, 2 * C_CONV), BF16), jax.ShapeDtypeStruct((32, C_CONV), F32),
                   jax.ShapeDtypeStruct((8, C_CONV), F32)],
        scratch_shapes=[pltpu.VMEM((lp + CHUNK, C_CONV), F32), pltpu.VMEM((lp + CHUNK, C_CONV), F32),
                        pltpu.VMEM((8 * 32, C_CONV), F32)],
        compiler_params=_params(("arbitrary",)),
    )(uc, ypre, dyc, conv_w, ln_g, ln_b, token)


def _seg_chunks(n_chunk):
    return max(c for c in (11, 3, 1) if n_chunk % c == 0)


def _block_mask(shape, row_block, lane_block):
    return (lax.broadcasted_iota(jnp.int32, shape, 0) // row_block) == (lax.broadcasted_iota(jnp.int32, shape, 1) // lane_block)


def _per_head_rows(x, mask):
    return jnp.where(mask, jnp.concatenate([x] * GLA_H, axis=0), 0)


def _fold_heads(full, lane_block):
    lane = lax.broadcasted_iota(jnp.int32, (1, full.shape[1]), 1) // lane_block
    out = jnp.where(lane == 0, full[0:CHUNK], 0.0)
    for h in range(1, GLA_H):
        out = out + jnp.where(lane == h, full[h * CHUNK:(h + 1) * CHUNK], 0.0)
    return out


def _causal_heads():
    return (lax.broadcasted_iota(jnp.int32, (CHUNK, GLA_H * CHUNK), 1) % CHUNK) <= lax.broadcasted_iota(
        jnp.int32, (CHUNK, GLA_H * CHUNK), 0)


def _cumsum_rows(x):
    row = lax.broadcasted_iota(jnp.int32, x.shape, 0)
    s = 1
    while s < CHUNK:
        x = x + jnp.where(row >= s, pltpu.roll(x, s, 0), 0.0)
        s *= 2
    return x


def _rev_cumsum_rows(x):
    row = lax.broadcasted_iota(jnp.int32, x.shape, 0)
    s = 1
    while s < CHUNK:
        x = x + jnp.where(row < CHUNK - s, pltpu.roll(x, CHUNK - s, 0), 0.0)
        s *= 2
    return x


def _gate_terms(lr_ref, w2_ref, gb_ref, rs, first_pos):
    z = _dot(lr_ref[rs, :].astype(BF16), w2_ref[...]) + gb_ref[...]
    la = (jnp.minimum(z, 0.0) - jnp.log(1.0 + jnp.exp(-jnp.abs(z)))) * (1.0 / TAU)
    pos = first_pos + lax.broadcasted_iota(jnp.int32, (CHUNK, 1), 0)
    live = pos >= ZROWS
    la = jnp.where(live, la, 0.0)
    return z, live, _cumsum_rows(la)


def _fwd_gla(qk, vg, lr, w2p, gb, ng, token, n_ex):
    rows = qk.shape[0]
    lp = rows // n_ex
    n_chunk = lp // CHUNK
    sc = _seg_chunks(n_chunk)
    n_seg = n_chunk // sc
    seg = sc * CHUNK

    def body(qk_ref, vg_ref, lr_ref, w2_ref, gb_ref, ng_ref, token_ref, yg_ref, o_ref, st_ref, state_ref):
        sidx = pl.program_id(1)

        @pl.when(sidx == 0)
        def _():
            state_ref[...] = jnp.zeros_like(state_ref)

        causal = _causal_heads()
        k_mask = _block_mask((GLA_H * CHUNK, GLA_K), CHUNK, GLA_DK)
        v_mask = _block_mask((GLA_H * CHUNK, GLA_V), CHUNK, GLA_DV)
        s_mask = _block_mask((GLA_V, GLA_K), GLA_DV, GLA_DK)

        def chunk(ci, carry):
            base = pl.multiple_of(ci * CHUNK, CHUNK)
            rs = pl.ds(base, CHUNK)
            _, _, bcum = _gate_terms(lr_ref, w2_ref, gb_ref, rs, (sidx * sc + ci) * CHUNK)
            bl = bcum[CHUNK - 1:CHUNK, :]
            q = qk_ref[rs, 0:GLA_K].astype(F32)
            k = qk_ref[rs, GLA_K:2 * GLA_K].astype(F32)
            qt = (q * (GLA_DK ** -0.5) * jnp.exp(bcum)).astype(BF16)
            kt = (k * jnp.exp(-bcum)).astype(BF16)
            kh = (k * jnp.exp(bl - bcum)).astype(BF16)
            vb = vg_ref[rs, 0:GLA_V].astype(BF16)
            state = state_ref[...]
            st_ref[ci] = state
            a = jnp.where(causal, _dot(qt, _per_head_rows(kt, k_mask), _NT), 0.0)
            o = _dot(a.astype(BF16), _per_head_rows(vb, v_mask)) + _dot(qt, state.astype(BF16), _NT)
            o_ref[rs, :] = o
            for h in range(GLA_H):
                hs = slice(h * GLA_DV, (h + 1) * GLA_DV)
                oh = o[:, hs]
                ro = lax.rsqrt(jnp.mean(oh * oh, axis=-1, keepdims=True) + RMS_EPS)
                g = vg_ref[rs, GLA_V + h * GLA_DV:GLA_V + (h + 1) * GLA_DV].astype(F32)
                yg_ref[rs, hs] = (oh * ro * ng_ref[...] * g * _sigmoid(g)).astype(BF16)
            state_ref[...] = state * jnp.exp(bl) + jnp.where(s_mask, _dot(vb, kh, _TN), 0.0)
            return carry

        lax.fori_loop(0, sc, chunk, 0, unroll=True)

    sg = lambda w: pl.BlockSpec((seg, w), lambda b, s: (b * n_seg + s, 0))
    return pl.pallas_call(
        body, name="fwd_gla", grid=(n_ex, n_seg),
        in_specs=[sg(2 * GLA_K), sg(2 * GLA_V), sg(RANK_P), _fixed((RANK_P, GLA_K)), _fixed((1, GLA_K)), _fixed((1, GLA_DV)),
                  _fixed((8, 128))],
        out_specs=[sg(GLA_V), sg(GLA_V), pl.BlockSpec((sc, GLA_V, GLA_K), lambda b, s: (b * n_seg + s, 0, 0))],
        out_shape=[jax.ShapeDtypeStruct((rows, GLA_V), BF16), jax.ShapeDtypeStruct((rows, GLA_V), F32),
                   jax.ShapeDtypeStruct((n_ex * n_chunk, GLA_V, GLA_K), F32)],
        scratch_shapes=[pltpu.VMEM((GLA_V, GLA_K), F32)],
        compiler_params=_params(("parallel", "arbitrary")),
    )(qk, vg, lr, w2p, gb, ng, token)


def _bwd_gla(qk, vg, lr, o, st, dyg, w2p, gb, ng, yc, yg, dh1b, token, n_ex):
    rows = qk.shape[0]
    lp = rows // n_ex
    n_chunk = lp // CHUNK
    sc = _seg_chunks(n_chunk)
    n_seg = n_chunk // sc
    seg = sc * CHUNK

    def body(qk_ref, vg_ref, lr_ref, o_ref, st_ref, dyg_ref, w2_ref, gb_ref, ng_ref, yc_ref, yg_ref, dh1_ref, token_ref,
             dqk_ref, dvg_ref, dlr_ref, dw2_ref, dvec_ref, dwo_ref, gt_ref, dz_ref, dwo_acc):
        step = pl.program_id(1)
        sidx = n_seg - 1 - step
        first = (step == 0) & (pl.program_id(0) == 0)

        @pl.when(step == 0)
        def _():
            gt_ref[...] = jnp.zeros_like(gt_ref)

        @pl.when(first)
        def _():
            dw2_ref[...] = jnp.zeros_like(dw2_ref)
            dvec_ref[...] = jnp.zeros_like(dvec_ref)
            dwo_acc[...] = jnp.zeros_like(dwo_acc)

        d1 = dh1_ref[...]
        dwo_acc[0:C_CONV, :] += _dot(yc_ref[...], d1, _TN)
        dwo_acc[C_CONV:D, :] += _dot(yg_ref[...], d1, _TN)

        @pl.when((step == n_seg - 1) & (pl.program_id(0) == n_ex - 1))
        def _():
            dwo_ref[...] = dwo_acc[...].astype(BF16)

        causal = _causal_heads()
        k_mask = _block_mask((GLA_H * CHUNK, GLA_K), CHUNK, GLA_DK)
        v_mask = _block_mask((GLA_H * CHUNK, GLA_V), CHUNK, GLA_DV)
        s_mask = _block_mask((GLA_V, GLA_K), GLA_DV, GLA_DK)
        last_row = lax.broadcasted_iota(jnp.int32, (CHUNK, 1), 0) == CHUNK - 1
        ng = ng_ref[...]

        def chunk(ii, dng):
            ci = sc - 1 - ii
            base = pl.multiple_of(ci * CHUNK, CHUNK)
            rs = pl.ds(base, CHUNK)
            z, live, bcum = _gate_terms(lr_ref, w2_ref, gb_ref, rs, (sidx * sc + ci) * CHUNK)
            bl = bcum[CHUNK - 1:CHUNK, :]
            ebl = jnp.exp(bl)
            q = qk_ref[rs, 0:GLA_K].astype(F32)
            k = qk_ref[rs, GLA_K:2 * GLA_K].astype(F32)
            eb = jnp.exp(bcum)
            enb = jnp.exp(-bcum)
            ehb = jnp.exp(bl - bcum)
            qt = q * (GLA_DK ** -0.5) * eb
            kt = k * enb
            kh = k * ehb
            qtb = qt.astype(BF16)
            vb = vg_ref[rs, 0:GLA_V].astype(BF16)
            k_rows = _per_head_rows(kt.astype(BF16), k_mask)
            v_rows = _per_head_rows(vb, v_mask)
            gt = gt_ref[...]
            gtb = gt.astype(BF16)
            s_in = st_ref[ci]
            dos = []
            for h in range(GLA_H):
                hs = slice(h * GLA_DV, (h + 1) * GLA_DV)
                gs = slice(GLA_V + h * GLA_DV, GLA_V + (h + 1) * GLA_DV)
                oh = o_ref[rs, hs]
                ro = lax.rsqrt(jnp.mean(oh * oh, axis=-1, keepdims=True) + RMS_EPS)
                on = oh * ro
                g = vg_ref[rs, gs].astype(F32)
                sg = _sigmoid(g)
                dout = dyg_ref[rs, hs]
                dvg_ref[rs, gs] = (dout * on * ng * (sg * (1.0 + g * (1.0 - sg)))).astype(BF16)
                dw = dout * g * sg
                dng = dng + jnp.sum(dw * on, axis=0, keepdims=True)
                don = dw * ng
                dos.append((ro * (don - on * jnp.mean(don * on, axis=-1, keepdims=True))).astype(BF16))
            dob = jnp.concatenate(dos, axis=1)
            a = jnp.where(causal, _dot(qtb, k_rows, _NT), 0.0).astype(BF16)
            da = jnp.where(causal, _dot(dob, v_rows, _NT), 0.0).astype(BF16)
            dv = _fold_heads(_dot(a, dob, _TN), GLA_DV) + _dot(kh.astype(BF16), gtb, _NT)
            dvg_ref[rs, 0:GLA_V] = dv.astype(BF16)
            dkh = _dot(vb, gtb)
            dqt = _dot(da, k_rows) + _dot(dob, s_in.astype(BF16))
            dkt = _fold_heads(_dot(da, qtb, _TN), GLA_DK)
            dbl = jnp.sum(gt * s_in, axis=0, keepdims=True) * ebl + jnp.sum(dkh * kh, axis=0, keepdims=True)
            dqk_ref[rs, 0:GLA_K] = (dqt * (GLA_DK ** -0.5) * eb).astype(BF16)
            dqk_ref[rs, GLA_K:2 * GLA_K] = (dkt * enb + dkh * ehb).astype(BF16)
            db = dqt * qt - dkt * kt - dkh * kh
            db = jnp.where(last_row, db + dbl, db)
            dla = jnp.where(live, _rev_cumsum_rows(db), 0.0)
            dz_ref[rs, :] = dla * (1.0 / TAU) * (1.0 - _sigmoid(z))
            gt_ref[...] = jnp.where(s_mask, _dot(dob, qtb, _TN), 0.0) + gt * ebl
            return dng

        dng = lax.fori_loop(0, sc, chunk, jnp.zeros((1, GLA_DV), F32), unroll=True)
        dz = dz_ref[...]
        dzb = dz.astype(BF16)
        dlr_ref[...] = _dot(dzb, w2_ref[...], _NT).astype(BF16)
        dw2_ref[...] += _dot(lr_ref[...].astype(BF16), dzb, _TN)
        dvec_ref[0:1, :] += jnp.sum(dz, axis=0, keepdims=True)
        dvec_ref[1:2, 0:GLA_DV] += dng

    sg_ = lambda w: pl.BlockSpec((seg, w), lambda b, s: (b * n_seg + n_seg - 1 - s, 0))
    return pl.pallas_call(
        body, name="bwd_gla", grid=(n_ex, n_seg),
        in_specs=[sg_(2 * GLA_K), sg_(2 * GLA_V), sg_(RANK_P), sg_(GLA_V),
                  pl.BlockSpec((sc, GLA_V, GLA_K), lambda b, s: (b * n_seg + n_seg - 1 - s, 0, 0)), sg_(GLA_V),
                  _fixed((RANK_P, GLA_K)), _fixed((1, GLA_K)), _fixed((1, GLA_DV)), sg_(C_CONV), sg_(GLA_V), sg_(D),
                  _fixed((8, 128))],
        out_specs=[sg_(2 * GLA_K), sg_(2 * GLA_V), sg_(RANK_P), _fixed((RANK_P, GLA_K)), _fixed((8, GLA_K)),
                   _fixed((D, D))],
        out_shape=[jax.ShapeDtypeStruct((rows, 2 * GLA_K), BF16), jax.ShapeDtypeStruct((rows, 2 * GLA_V), BF16),
                   jax.ShapeDtypeStruct((rows, RANK_P), BF16), jax.ShapeDtypeStruct((RANK_P, GLA_K), F32),
                   jax.ShapeDtypeStruct((8, GLA_K), F32), jax.ShapeDtypeStruct((D, D), BF16)],
        scratch_shapes=[pltpu.VMEM((GLA_V, GLA_K), F32), pltpu.VMEM((seg, GLA_K), F32), pltpu.VMEM((D, D), F32)],
        compiler_params=_params(("arbitrary", "arbitrary")),
    )(qk, vg, lr, o, st, dyg, w2p, gb, ng, yc, yg, dh1b, token)


def _pad_rows(x, tgt):
    return jnp.pad(x, ((0, 0), (LEAD, 0), (0, 0))), jnp.pad(tgt, ((0, 0), (LEAD, 0), (0, 0)))


def _local_step(h0, tgt_p, p, pass_on, late_weights, send_early):
    n_ex, lp, _ = h0.shape
    rows = n_ex * lp
    meta = jnp.broadcast_to(p["meta"][None], (n_ex, N_META, D))
    h0 = lax.dynamic_update_slice(h0, meta, (0, ZROWS, 0)).reshape(rows, D)
    tgt_p = tgt_p.reshape(rows, D)

    uc, qk, vg, lr, n1 = _fwd_inproj(h0, p["g1"], p["w_in"])
    ypre, yc = _fwd_conv(uc, p["conv_w"], p["conv_b"], p["ln_g"], p["ln_b"], p["token"], n_ex)
    token = pass_on(yc)
    yg, o, st = _fwd_gla(qk, vg, lr, p["w2"], p["gb"], p["ng"], token, n_ex)
    w_out, wg, wu, wd = late_weights(yg)
    n2, f, da, db, dh2, dh1b, dyc, dyg, part = _mid_rows(
        yc, yg, h0, tgt_p, w_out, wg, wu, wd, p["g2"], p["g3"], token, lp)
    g = {}
    token = send_early("ffn", [_matmul_tn(a_, b_, name).reshape(N_DEV, FF_S, D) for a_, b_, name in (
        (da, n2, "dw_gate"), (db, n2, "dw_up"), (f, dh2, "dw_down"))])
    dqk, dvg, dlr, g["w2"], g["gla_vec"], dw_out = _bwd_gla(
        qk, vg, lr, o, st, dyg, p["w2"], p["gb"], p["ng"], yc, yg, dh1b, token, n_ex)
    token = send_early("out", [dw_out.reshape(N_DEV, W_OUT_S, D)])
    duc, g["conv_w"], g["conv_vec"] = _bwd_conv(uc, ypre, dyc, p["conv_w"], p["ln_g"], p["ln_b"], token, n_ex)
    token = send_early("in", [_dw_blocked(n1, [duc, dqk, dvg, dlr], W_IN_S, "dw_in")])
    grad_x, g["in_vec"], g["meta"] = _bwd_inproj(duc, dqk, dvg, dlr, dh1b, h0, p["w_in"], p["g1"], token, lp)
    g["ffn_vec"] = part
    return grad_x, g


W_IN_S = D_IN // N_DEV
W_OUT_S = D // N_DEV
FF_S = D_FF // N_DEV
CONV_S = C_CONV // N_DEV
GATE_S = GLA_K // N_DEV
SMALL_PACK = 64
CONV_ROW = 16
GATE_ROW = 48
VEC_ROWS = 16
_VEC_ROWS = (("norm_mix_g", D), ("conv_b", C_CONV), ("conv_ln_g", C_CONV), ("conv_ln_b", C_CONV), ("gla_gate_b", GLA_K),
             ("gla_norm_g", GLA_DV), ("norm_ffn_g", D), ("norm_final_g", D))
LOSS_ROW = len(_VEC_ROWS)


def _position():
    return lax.axis_index("x"), lax.axis_index("y"), lax.axis_index("c")


def _any():
    return pl.BlockSpec(memory_space=pl.ANY)


def _stage(mats, meta, conv_w, w2):
    n_t = len(mats) + 1

    def body(*refs):
        ins = refs[0:n_t - 1]
        meta_ref, cw_ref, w2_ref = refs[n_t - 1:n_t + 2]
        lands = refs[n_t + 2:2 * n_t + 2]
        shards = refs[2 * n_t + 2:3 * n_t + 2]
        sems = refs[3 * n_t + 2]
        for s_ref, w_ref in zip(shards, ins):
            s_ref[...] = w_ref[...].astype(BF16)
        sp = shards[n_t - 1]
        sp[...] = jnp.zeros_like(sp)
        sp[0:N_META, :] = meta_ref[...]
        sp[CONV_ROW:CONV_ROW + CONV_W, 0:CONV_S] = cw_ref[...]
        sp[GATE_ROW:GATE_ROW + RANK, 0:GATE_S] = w2_ref[...]
        x, y, c = _position()
        mine = [pltpu.make_async_copy(shards[t], lands[t].at[4 * x + 2 * y + c], sems.at[t]) for t in range(n_t)]
        for cp in mine:
            cp.start()
        for cp in mine:
            cp.wait()

    shard_shapes = [jax.ShapeDtypeStruct(m.shape, BF16) for m in mats] + [jax.ShapeDtypeStruct((SMALL_PACK, 128), F32)]
    res = pl.pallas_call(
        body, name="stage",
        out_shape=[jax.ShapeDtypeStruct((N_DEV,) + s.shape, s.dtype) for s in shard_shapes] + shard_shapes,
        in_specs=[_whole_vmem()] * (n_t + 2), out_specs=[_any()] * n_t + [_whole_vmem()] * n_t,
        scratch_shapes=[pltpu.SemaphoreType.DMA((n_t,))],
        compiler_params=pltpu.CompilerParams(vmem_limit_bytes=VMEM_LIMIT),
    )(*mats, meta, conv_w, w2)
    return res[0:n_t], res[n_t:]


_HBM = pl.BlockSpec(memory_space=pltpu.HBM)
_SEM = pl.BlockSpec(memory_space=pltpu.SEMAPHORE)
_EFFECT = pltpu.SideEffectType.DATAFLOW_SIDE_EFFECTING


_N_ROUTES = {"scatter": 7, "first": 4, "forward": 3}


def _routes(mode):
    x, y, c = _position()
    me = 4 * x + 2 * y + c
    if mode == "scatter":
        out = []
        for k in range(1, N_DEV):
            px = 1 - x if k & 4 else x
            py = 1 - y if k & 2 else y
            pc = 1 - c if k & 1 else c
            out.append(((px, py, pc), 4 * px + 2 * py + pc, me))
        return out
    if mode == "first":
        return [(pos, None, me) for pos in ((x, y, 1 - c), (1 - x, y, c), (x, 1 - y, c), (1 - x, 1 - y, c))]
    assert mode == "forward"
    return [((x, y, 1 - c), 4 * px + 2 * py + c, 4 * px + 2 * py + c) for px, py in ((1 - x, y), (x, 1 - y), (1 - x, 1 - y))]


def _route_copies(mode, n, src_refs, land_refs, send_sems, recv_sems):
    nr = _N_ROUTES[mode]
    for i, (pos, src_blk, dst_blk) in enumerate(_routes(mode)):
        for t in range(n):
            src = land_refs[t] if mode == "forward" else src_refs[t]
            yield pltpu.make_async_remote_copy(
                src_ref=src if src_blk is None else src.at[src_blk], dst_ref=land_refs[t].at[dst_blk],
                send_sem=send_sems.at[nr * t + i], recv_sem=recv_sems.at[nr * t + i], device_id=pos, device_id_type=MESH)


def _in_hbm(a):
    return pltpu.with_memory_space_constraint(a, pltpu.HBM)


def _send_start(name, srcs, lands, mode, after):
    n, ns = len(lands), len(srcs)
    nsem = _N_ROUTES[mode] * n

    def body(*refs):
        src_refs, land_refs = refs[0:ns], refs[ns:ns + n]
        send_sems, recv_sems = refs[ns + n + 1:ns + n + 3]
        token = refs[2 * (ns + n) + 3]
        for cp in _route_copies(mode, n, src_refs, land_refs, send_sems, recv_sems):
            cp.start()
        token[...] = jnp.zeros_like(token)

    bufs = list(srcs) + list(lands)
    res = pl.pallas_call(
        body, name=name,
        out_shape=(pltpu.SemaphoreType.DMA((nsem,)), pltpu.SemaphoreType.DMA((nsem,)),
                   *[pltpu.HBM(b.shape, b.dtype) for b in bufs], jax.ShapeDtypeStruct((8, 128), F32)),
        in_specs=[_HBM] * len(bufs) + [_any()], out_specs=(_SEM, _SEM, *[_HBM] * len(bufs), _whole_vmem()),
        input_output_aliases={i: 2 + i for i in range(len(bufs))},
        compiler_params=pltpu.CompilerParams(has_side_effects=_EFFECT),
    )(*[_in_hbm(b) for b in bufs], after)
    return res[0], res[1], res[2:2 + ns], res[2 + ns:2 + ns + n], res[2 + ns + n]


def _send_wait(name, send_sems, recv_sems, srcs, lands, mode, after):
    n, ns = len(lands), len(srcs)
    after = after if isinstance(after, tuple) else (after,)

    def body(*refs):
        src_refs, land_refs = refs[0:ns], refs[ns:ns + n]
        send_sems, recv_sems = refs[ns + n:ns + n + 2]
        for cp in _route_copies(mode, n, src_refs, land_refs, send_sems, recv_sems):
            cp.wait_send()
            cp.wait_recv()

    bufs = list(srcs) + list(lands)
    res = pl.pallas_call(
        body, name=name,
        out_shape=tuple(pltpu.HBM(b.shape, b.dtype) for b in bufs),
        in_specs=[_HBM] * len(bufs) + [_SEM, _SEM] + [_any()] * len(after), out_specs=tuple([_HBM] * len(bufs)),
        input_output_aliases={i: i for i in range(len(bufs))},
        compiler_params=pltpu.CompilerParams(has_side_effects=_EFFECT),
    )(*bufs, send_sems, recv_sems, *after)
    return res[0:ns], res[ns:ns + n]


def _unshard_in(a_in, a_small, token):
    def body(a_ref, s_ref, token_ref, w_ref, meta_ref, cw_ref, w2_ref):
        w_ref[:, D_IN:D_INP] = jnp.zeros((D, D_INP - D_IN), BF16)
        w2_ref[...] = jnp.zeros_like(w2_ref)
        for d in range(N_DEV):
            w_ref[:, d * W_IN_S:(d + 1) * W_IN_S] = a_ref[d]
            meta_ref[:, d * 128:(d + 1) * 128] = s_ref[d, 0:N_META, :]
            cw_ref[:, d * CONV_S:(d + 1) * CONV_S] = s_ref[d, CONV_ROW:CONV_ROW + 32, 0:CONV_S]
            w2_ref[0:RANK, d * GATE_S:(d + 1) * GATE_S] = s_ref[d, GATE_ROW:GATE_ROW + RANK, 0:GATE_S].astype(BF16)

    return pl.pallas_call(
        body, name="unshard_in",
        out_shape=[jax.ShapeDtypeStruct((D, D_INP), BF16), jax.ShapeDtypeStruct((N_META, D), F32),
                   jax.ShapeDtypeStruct((32, C_CONV), F32), jax.ShapeDtypeStruct((RANK_P, GLA_K), BF16)],
        compiler_params=pltpu.CompilerParams(vmem_limit_bytes=VMEM_LIMIT),
    )(a_in, a_small, token)


def _pack_small(g):
    def body(meta_ref, cw_ref, w2_ref, in_vec, ffn_vec, conv_vec, gla_vec, sp, vp):
        sp[...] = jnp.zeros_like(sp)
        vp[...] = jnp.zeros_like(vp)
        for d in range(N_DEV):
            sp[d, 0:N_META, :] = meta_ref[:, d * 128:(d + 1) * 128]
            sp[d, CONV_ROW:CONV_ROW + 32, 0:CONV_S] = cw_ref[:, d * CONV_S:(d + 1) * CONV_S]
            sp[d, GATE_ROW:GATE_ROW + RANK, 0:GATE_S] = w2_ref[0:RANK, d * GATE_S:(d + 1) * GATE_S]
            vp[d, 0:1, :] = in_vec[0:1, :]
            vp[d, 1:4, 0:C_CONV] = conv_vec[0:3, :]
            vp[d, 4:5, 0:GLA_K] = gla_vec[0:1, :]
            vp[d, 5:6, 0:GLA_DV] = gla_vec[1:2, 0:GLA_DV]
            vp[d, 6:7, :] = ffn_vec[1:2, :]
            vp[d, 7:8, :] = ffn_vec[0:1, :]
            vp[d, LOSS_ROW:LOSS_ROW + 1, :] = ffn_vec[2:3, :]

    return pl.pallas_call(
        body, name="pack_small",
        out_shape=[jax.ShapeDtypeStruct((N_DEV, SMALL_PACK, 128), F32), jax.ShapeDtypeStruct((N_DEV, VEC_ROWS, D), F32)],
    )(g["meta"], g["conv_w"], g["w2"], g["in_vec"], g["ffn_vec"], g["conv_vec"], g["gla_vec"])


def _adamw(w, g, m, v):
    m = ADAM_B1 * m + (1.0 - ADAM_B1) * g
    v = ADAM_B2 * v + (1.0 - ADAM_B2) * (g * g)
    m_hat = m / (1.0 - ADAM_B1 ** ADAM_STEP)
    v_hat = v / (1.0 - ADAM_B2 ** ADAM_STEP)
    return -ADAM_LR * (m_hat / (jnp.sqrt(v_hat) + ADAM_EPS) + ADAM_WD * w), m, v


def _update_matrix(recv, own, me, w, m, v, name):
    _, r, c = recv.shape
    tr = _row_tile(r, 256)

    def body(me_ref, recv_ref, own_ref, w_ref, m_ref, v_ref, g_ref, d_ref, nm_ref, nv_ref):
        g = jnp.zeros((tr, c), F32)
        for s in range(N_DEV):
            g = g + jnp.where(me_ref[0] == s, own_ref[...], recv_ref[s]).astype(F32)
        g_ref[...] = g
        d_ref[...], nm_ref[...], nv_ref[...] = _adamw(w_ref[...], g, m_ref[...], v_ref[...])

    one = pl.BlockSpec((None, tr, c), lambda i, me_ref: (0, i, 0))
    return pl.pallas_call(
        body, name=name,
        grid_spec=pltpu.PrefetchScalarGridSpec(
            num_scalar_prefetch=1, grid=(r // tr,),
            in_specs=[pl.BlockSpec((N_DEV, tr, c), lambda i, me_ref: (0, i, 0)),
                      pl.BlockSpec((None, tr, c), lambda i, me_ref: (me_ref[0], i, 0)), one, one, one],
            out_specs=[one] * 4),
        out_shape=[jax.ShapeDtypeStruct((1, r, c), F32)] * 4,
        compiler_params=_params(("parallel",)),
    )(me, recv, own, w, m, v)


_SMALL = ("meta_tokens", "conv_w", "gla_w_gate2") + tuple(n for n, _ in _VEC_ROWS)


def _update_small(me, srecv, vrecv, sown, vown, w, m, v):
    n = len(_SMALL)

    def body(*refs):
        me_ref, s_ref, v_ref, so_ref, vo_ref = refs[0:5]
        w_refs, m_refs, v_refs = refs[5:5 + n], refs[5 + n:5 + 2 * n], refs[5 + 2 * n:5 + 3 * n]
        outs = refs[5 + 3 * n:]
        ssum = jnp.zeros((SMALL_PACK, 128), F32)
        vsum = jnp.zeros((VEC_ROWS, D), F32)
        for s in range(N_DEV):
            ssum = ssum + jnp.where(me_ref[0] == s, so_ref[s], s_ref[s])
            vsum = vsum + jnp.where(me_ref[0] == s, vo_ref[s], v_ref[s])
        grads = [ssum[0:N_META, :], ssum[CONV_ROW:CONV_ROW + CONV_W, 0:CONV_S], ssum[GATE_ROW:GATE_ROW + RANK, 0:GATE_S]]
        grads += [vsum[i:i + 1, 0:width] for i, (_, width) in enumerate(_VEC_ROWS)]
        for i, g in enumerate(grads):
            d, nm, nv = _adamw(w_refs[i][...], g, m_refs[i][...], v_refs[i][...])
            outs[i][...] = g
            outs[n + i][...] = d
            outs[2 * n + i][...] = nm
            outs[3 * n + i][...] = nv
        outs[4 * n][...] = vsum[LOSS_ROW:LOSS_ROW + 1, 0:128]

    shapes = [jax.ShapeDtypeStruct(t.shape, F32) for t in w]
    res = pl.pallas_call(
        body, name="update_small", out_shape=shapes * 4 + [jax.ShapeDtypeStruct((1, 128), F32)],
        in_specs=[pl.BlockSpec(memory_space=pltpu.SMEM)] + [_whole_vmem()] * (4 + 3 * n),
    )(me, srecv, vrecv, sown, vown, *w, *m, *v)
    return res[0:n], res[n:2 * n], res[2 * n:3 * n], res[3 * n:4 * n], res[4 * n]


_WEIGHTS = ("meta_tokens", "norm_mix_g", "w_in", "conv_w", "conv_b", "conv_ln_g", "conv_ln_b", "gla_w_gate2", "gla_gate_b",
            "gla_norm_g", "w_out", "norm_ffn_g", "w_ffn_gate", "w_ffn_up", "w_ffn_down", "norm_final_g")
_MATRICES = ("w_in", "w_out", "w_ffn_gate", "w_ffn_up", "w_ffn_down")
_TRANSPOSED = ("w_ffn_gate", "w_ffn_up")


def kernel(x, meta_tokens, norm_mix_g, w_in, conv_w, conv_b, conv_ln_g, conv_ln_b, gla_w_gate2, gla_gate_b, gla_norm_g, w_out, norm_ffn_g, w_ffn_gate, w_ffn_up, w_ffn_down, norm_final_g, loss_target, m_meta_tokens, m_norm_mix_g, m_w_in, m_conv_w, m_conv_b, m_conv_ln_g, m_conv_ln_b, m_gla_w_gate2, m_gla_gate_b, m_gla_norm_g, m_w_out, m_norm_ffn_g, m_w_ffn_gate, m_w_ffn_up, m_w_ffn_down, m_norm_final_g, v_meta_tokens, v_norm_mix_g, v_w_in, v_conv_w, v_conv_b, v_conv_ln_g, v_conv_ln_b, v_gla_w_gate2, v_gla_gate_b, v_gla_norm_g, v_w_out, v_norm_ffn_g, v_w_ffn_gate, v_w_ffn_up, v_w_ffn_down, v_norm_final_g):
    given = dict(locals())
    two_d = lambda a: a.reshape(1, -1) if a.ndim == 1 else a.reshape(a.shape[-2:])
    fams = [{n: given[pre + n] for n in _WEIGHTS} for pre in ("", "m_", "v_")]
    for f in fams:
        for n in _TRANSPOSED:
            f[n] = f[n].transpose(0, 2, 1)
    w = fams[0]

    lands, shards = _stage([two_d(w[n]) for n in _MATRICES], w["meta_tokens"], two_d(w["conv_w"]), two_d(w["gla_w_gate2"]))
    soon, later = (0, 5), (1, 2, 3, 4)
    pick = lambda seq, idx: [seq[i] for i in idx]
    first = _send_start("gather_first_start", pick(shards, soon), pick(lands, soon), "first", norm_mix_g)
    ffn_first = _send_start("gather_ffn_first_start", pick(shards, later), pick(lands, later), "first", first[4])
    h0, tgt_p = _pad_rows(x, loss_target)
    _, arrived = _send_wait("gather_first_wait", *first[0:4], "first", (h0, tgt_p, ffn_first[4]))
    forward = _send_start("gather_forward_start", [], arrived, "forward", ffn_first[4])
    _, (a_in, a_small) = _send_wait("gather_forward_wait", *forward[0:4], "forward", forward[4])
    w_in, meta, conv_taps, w2 = _unshard_in(a_in, a_small, forward[4])
    p = dict(meta=meta, conv_w=conv_taps, w2=w2, w_in=w_in, g1=norm_mix_g, conv_b=conv_b, ln_g=conv_ln_g, ln_b=conv_ln_b,
             gb=gla_gate_b, ng=gla_norm_g, g2=norm_ffn_g, g3=two_d(norm_final_g), token=forward[4])
    passed = {}

    def pass_on(after):
        _, arrived_ffn = _send_wait("gather_ffn_first_wait", *ffn_first[0:4], "first", after)
        passed["sent"] = _send_start("gather_ffn_forward_start", [], arrived_ffn, "forward", after)
        return passed["sent"][4]

    def late_weights(after):
        _, (a_out, a_g, a_u, a_d) = _send_wait("gather_ffn_forward_wait", *passed["sent"][0:4], "forward", after)
        return a_out.reshape(D, D), a_g.reshape(D_FF, D), a_u.reshape(D_FF, D), a_d.reshape(D_FF, D)

    sent = {}

    def send_early(tag, mats):
        landing = [_in_hbm(lax.empty(m_.shape, m_.dtype)) for m_ in mats]
        sent[tag] = _send_start("scatter_" + tag + "_start", mats, landing, "scatter", norm_mix_g)
        return sent[tag][4]

    grad_x, g = _local_step(h0, tgt_p, p, pass_on, late_weights, send_early)

    token = send_early("small", list(_pack_small(g)))
    x_, y_, c_ = _position()
    me = (4 * x_ + 2 * y_ + c_).astype(jnp.int32).reshape(1)
    res = {}
    for tag, names in (("ffn", ("w_ffn_gate", "w_ffn_up", "w_ffn_down")), ("out", ("w_out",)), ("in", ("w_in",))):
        own, recv = _send_wait("scatter_" + tag + "_wait", *sent[tag][0:4], "scatter", token)
        for n, o_, r_ in zip(names, own, recv):
            res[n] = _update_matrix(r_, o_, me, *[f[n] for f in fams], "update_" + n)
            token = res[n][1]
    (sown, vown), (srecv, vrecv) = _send_wait("scatter_small_wait", *sent["small"][0:4], "scatter", token)
    small = _update_small(me, srecv, vrecv, sown, vown, *[[two_d(f[n]) for n in _SMALL] for f in fams])
    for i, n in enumerate(_SMALL):
        res[n] = [fam[i].reshape(w[n].shape) for fam in small[0:4]]
    for n in _TRANSPOSED:
        res[n] = [t.transpose(0, 2, 1) for t in res[n]]
    outs = [small[4][0, 0], grad_x]
    for k in range(4):
        outs += [res[n][k] for n in _WEIGHTS]
    return tuple(outs)
```

```python
import functools

import jax
import jax.numpy as jnp
from jax import lax
from jax.experimental import pallas as pl
from jax.experimental.pallas import tpu as pltpu

F32 = jnp.float32
BF16 = jnp.bfloat16

D = 1024
N_META = 16
C_CONV = 512
CONV_W = 31
GLA_H = 4
GLA_DK = 64
GLA_DV = 128
GLA_K = GLA_H * GLA_DK
GLA_V = GLA_H * GLA_DV
RANK = 16
RANK_P = 128
TAU = 16.0
CHUNK = 64
LEAD = CHUNK
ZROWS = LEAD - N_META
D_IN = 2 * C_CONV + 2 * GLA_K + 2 * GLA_V + RANK
D_INP = D_IN - RANK + RANK_P
D_FF = 2816
FF_CHUNK = 1408
FF_SPLIT = (0, 1536, D_FF)
RMS_EPS = 1e-6
LN_EPS = 1e-5
N_DEV = 8

ADAM_LR = 0.001
ADAM_B1 = 0.9
ADAM_B2 = 0.999
ADAM_EPS = 1e-08
ADAM_WD = 0.01
ADAM_STEP = 10

VMEM_LIMIT = 60 * 1024 * 1024
ROW_TILE = 1056
FFN_ROW_TILE = 352
DW_ROW_TILE = 1408
MESH = pl.DeviceIdType.MESH

_NN = (((1,), (0,)), ((), ()))
_NT = (((1,), (1,)), ((), ()))
_TN = (((0,), (0,)), ((), ()))


def _dot(a, b, dims=_NN):
    return lax.dot_general(a, b, dims, preferred_element_type=F32)


def _sigmoid(x):
    return 1.0 / (1.0 + jnp.exp(-x))


def _row_tile(rows, target):
    best = None
    for t in range(16, min(rows, target) + 1, 16):
        if rows % t == 0:
            best = t
    assert best is not None, rows
    return best


def _params(sem=None):
    return pltpu.CompilerParams(dimension_semantics=sem, vmem_limit_bytes=VMEM_LIMIT)


def _whole_vmem():
    return pl.BlockSpec(memory_space=pltpu.VMEM)


def _rows(tm, width):
    return pl.BlockSpec((tm, width), lambda i: (i, 0))


def _fixed(shape):
    return pl.BlockSpec(shape, lambda *_: (0,) * len(shape))


def _fwd_inproj(h0, g1, w_in):
    rows = h0.shape[0]
    tm = _row_tile(rows, ROW_TILE)

    def body(h_ref, g_ref, w_ref, uc_ref, qk_ref, vg_ref, lr_ref, n1_ref):
        h = h_ref[...]
        r = lax.rsqrt(jnp.mean(h * h, axis=-1, keepdims=True) + RMS_EPS)
        n = (h * r * g_ref[...]).astype(BF16)
        n1_ref[...] = n
        uc_ref[...] = _dot(n, w_ref[:, 0:1024]).astype(BF16)
        qk_ref[...] = _dot(n, w_ref[:, 1024:1536]).astype(BF16)
        vg_ref[...] = _dot(n, w_ref[:, 1536:2560]).astype(BF16)
        lr_ref[...] = _dot(n, w_ref[:, 2560:2688]).astype(BF16)

    return pl.pallas_call(
        body, name="fwd_inproj", grid=(rows // tm,),
        in_specs=[_rows(tm, D), _fixed((1, D)), _whole_vmem()],
        out_specs=[_rows(tm, 1024), _rows(tm, 512), _rows(tm, 1024), _rows(tm, RANK_P), _rows(tm, D)],
        out_shape=[jax.ShapeDtypeStruct((rows, 1024), BF16), jax.ShapeDtypeStruct((rows, 512), BF16),
                   jax.ShapeDtypeStruct((rows, 1024), BF16), jax.ShapeDtypeStruct((rows, RANK_P), BF16),
                   jax.ShapeDtypeStruct((rows, D), BF16)],
        compiler_params=_params(("parallel",)),
    )(h0, g1, w_in)


def _mid_rows(yc, yg, h0, tgt, w_out, wg, wu, wd, g2, g3, token, rows_per_example):
    rows = h0.shape[0]
    tm = _row_tile(rows, FFN_ROW_TILE)
    ff_blocks = [slice(lo, hi) for lo, hi in zip(FF_SPLIT[:-1], FF_SPLIT[1:])]

    def body(yc_ref, yg_ref, h0_ref, t_ref, wo_ref, wg_ref, wu_ref, wd_ref, g2_ref, g3_ref, token_ref,
             n2_ref, f_ref, da_ref, db_ref, dh2_ref, dh1_ref, dh1b_ref, dyc_ref, dyg_ref, part_ref):
        i = pl.program_id(0)
        h1 = h0_ref[...] + _dot(yc_ref[...], wo_ref[0:C_CONV, :]) + _dot(yg_ref[...], wo_ref[C_CONV:D, :])
        r2 = lax.rsqrt(jnp.mean(h1 * h1, axis=-1, keepdims=True) + RMS_EPS)
        xh2 = h1 * r2
        n2 = (xh2 * g2_ref[...]).astype(BF16)
        n2_ref[...] = n2
        y2 = jnp.zeros((tm, D), F32)
        for cs in ff_blocks:
            a = _dot(n2, wg_ref[cs, :], _NT)
            b = _dot(n2, wu_ref[cs, :], _NT)
            f = (a * _sigmoid(a) * b).astype(BF16)
            f_ref[:, cs] = f
            da_ref[:, cs] = a.astype(BF16)
            db_ref[:, cs] = b.astype(BF16)
            y2 = y2 + _dot(f, wd_ref[cs, :])
        h2 = h1 + y2
        r3 = lax.rsqrt(jnp.mean(h2 * h2, axis=-1, keepdims=True) + RMS_EPS)
        xh3 = h2 * r3
        g3 = g3_ref[...]
        pos = (i * tm + lax.broadcasted_iota(jnp.int32, (tm, 1), 0)) % rows_per_example
        valid = pos >= LEAD
        err = jnp.where(valid, xh3 * g3 - t_ref[...], 0.0)
        loss = 0.5 / D * jnp.sum(jnp.sum(err * err, axis=-1, keepdims=True), axis=0, keepdims=True)
        dy = err * (1.0 / D)
        dg3 = jnp.sum(dy * xh3, axis=0, keepdims=True)
        dxh = dy * g3
        dh2 = r3 * (dxh - xh3 * jnp.mean(dxh * xh3, axis=-1, keepdims=True))
        dh2b = dh2.astype(BF16)
        dh2_ref[...] = dh2b
        dn2 = jnp.zeros((tm, D), F32)
        for cs in ff_blocks:
            df = _dot(dh2b, wd_ref[cs, :], _NT)
            a = da_ref[:, cs].astype(F32)
            b = db_ref[:, cs].astype(F32)
            sg = _sigmoid(a)
            da = (df * b * sg * (1.0 + a * (1.0 - sg))).astype(BF16)
            db = (df * a * sg).astype(BF16)
            da_ref[:, cs] = da
            db_ref[:, cs] = db
            dn2 = dn2 + _dot(da, wg_ref[cs, :]) + _dot(db, wu_ref[cs, :])
        dg2 = jnp.sum(dn2 * xh2, axis=0, keepdims=True)
        dxh2 = dn2 * g2_ref[...]
        dh1 = dh2 + r2 * (dxh2 - xh2 * jnp.mean(dxh2 * xh2, axis=-1, keepdims=True))
        dh1_ref[...] = dh1
        dh1b = dh1.astype(BF16)
        dh1b_ref[...] = dh1b
        dyc_ref[...] = _dot(dh1b, wo_ref[0:C_CONV, :], _NT)
        dyg_ref[...] = _dot(dh1b, wo_ref[C_CONV:D, :], _NT)

        @pl.when(i == 0)
        def _():
            part_ref[...] = jnp.zeros_like(part_ref)

        part_ref[0:1, :] += dg3
        part_ref[1:2, :] += dg2
        part_ref[2:3, :] += jnp.broadcast_to(loss, (1, D))

    return pl.pallas_call(
        body, name="mid_rows", grid=(rows // tm,),
        in_specs=[_rows(tm, C_CONV), _rows(tm, GLA_V), _rows(tm, D), _rows(tm, D), _whole_vmem(), _whole_vmem(),
                  _whole_vmem(), _whole_vmem(), _fixed((1, D)), _fixed((1, D)), _fixed((8, 128))],
        out_specs=[_rows(tm, D), _rows(tm, D_FF), _rows(tm, D_FF), _rows(tm, D_FF), _rows(tm, D), _rows(tm, D),
                   _rows(tm, D), _rows(tm, C_CONV), _rows(tm, GLA_V), _fixed((8, D))],
        out_shape=[jax.ShapeDtypeStruct((rows, D), BF16)] + [jax.ShapeDtypeStruct((rows, D_FF), BF16)] * 3
        + [jax.ShapeDtypeStruct((rows, D), BF16), jax.ShapeDtypeStruct((rows, D), F32),
           jax.ShapeDtypeStruct((rows, D), BF16), jax.ShapeDtypeStruct((rows, C_CONV), F32),
           jax.ShapeDtypeStruct((rows, GLA_V), F32), jax.ShapeDtypeStruct((8, D), F32)],
        compiler_params=_params(("arbitrary",)),
    )(yc, yg, h0, tgt, w_out, wg, wu, wd, g2, g3, token)


def _bwd_inproj(duc, dqk, dvg, dlr, dh1, h0, w_in, g1, token, rows_per_example):
    rows = h0.shape[0]
    n_ex = rows // rows_per_example
    tm = _row_tile(rows_per_example, ROW_TILE)
    tiles_per_example = rows_per_example // tm
    n_steps = rows // tm

    def body(duc_ref, dqk_ref, dvg_ref, dlr_ref, dh1_ref, h_ref, w_ref, g_ref, token_ref, gx_ref, part_ref, dmeta_ref,
             buf_ref, sems):
        dn = (_dot(duc_ref[...], w_ref[:, 0:1024], _NT) + _dot(dqk_ref[...], w_ref[:, 1024:1536], _NT)
              + _dot(dvg_ref[...], w_ref[:, 1536:2560], _NT) + _dot(dlr_ref[...], w_ref[:, 2560:2688], _NT))
        h = h_ref[...]
        r = lax.rsqrt(jnp.mean(h * h, axis=-1, keepdims=True) + RMS_EPS)
        xh = h * r
        dg = jnp.sum(dn * xh, axis=0, keepdims=True)
        dxh = dn * g_ref[...]
        dh0 = dh1_ref[...] + r * (dxh - xh * jnp.mean(dxh * xh, axis=-1, keepdims=True))
        i = pl.program_id(0)

        def copies(step):
            slot, b, j = step % 2, step // tiles_per_example, step % tiles_per_example
            out = [(j == 0, pltpu.make_async_copy(buf_ref.at[slot, pl.ds(LEAD, tm - LEAD)],
                                                   gx_ref.at[b, pl.ds(0, tm - LEAD)], sems.at[slot]))]
            if tiles_per_example > 1:
                out.append((j != 0, pltpu.make_async_copy(
                    buf_ref.at[slot], gx_ref.at[b, pl.ds(pl.multiple_of(jnp.maximum(j * tm - LEAD, 0), 8), tm)],
                    sems.at[slot])))
            return out

        def each(step, act):
            for cond, cp in copies(step):
                pl.when(cond)(functools.partial(act, cp))

        @pl.when(i >= 2)
        def _():
            each(i - 2, lambda cp: cp.wait())

        buf_ref[i % 2] = dh0
        each(i, lambda cp: cp.start())

        @pl.when(i == n_steps - 1)
        def _():
            each(i, lambda cp: cp.wait())
            if n_steps > 1:
                each(i - 1, lambda cp: cp.wait())

        @pl.when(i == 0)
        def _():
            part_ref[...] = jnp.zeros_like(part_ref)
            dmeta_ref[...] = jnp.zeros_like(dmeta_ref)

        part_ref[0:1, :] += dg

        @pl.when(i % tiles_per_example == 0)
        def _():
            dmeta_ref[...] += dh0[ZROWS:LEAD, :]

    return pl.pallas_call(
        body, name="bwd_inproj", grid=(n_steps,),
        in_specs=[_rows(tm, 1024), _rows(tm, 512), _rows(tm, 1024), _rows(tm, RANK_P), _rows(tm, D), _rows(tm, D),
                  _whole_vmem(), _fixed((1, D)), _fixed((8, 128))],
        out_specs=[_any(), _fixed((8, D)), _fixed((N_META, D))],
        out_shape=[jax.ShapeDtypeStruct((n_ex, rows_per_example - LEAD, D), F32), jax.ShapeDtypeStruct((8, D), F32),
                   jax.ShapeDtypeStruct((N_META, D), F32)],
        scratch_shapes=[pltpu.VMEM((2, tm, D), F32), pltpu.SemaphoreType.DMA((2,))],
        compiler_params=_params(("arbitrary",)),
    )(duc, dqk, dvg, dlr, dh1, h0, w_in, g1, token)


def _dw_blocked(a, bs, width, name):
    rows, m = a.shape
    ws = [b.shape[1] for b in bs]
    assert sum(ws) >= N_DEV * width
    tk = _row_tile(rows, DW_ROW_TILE)
    nk = rows // tk

    def body(a_ref, *refs):
        b_refs, o_ref, acc_ref = refs[:len(bs)], refs[len(bs)], refs[len(bs) + 1]
        k = pl.program_id(0)

        @pl.when(k == 0)
        def _():
            acc_ref[...] = jnp.zeros_like(acc_ref)

        at = a_ref[...].T
        off = 0
        for b_ref, w in zip(b_refs, ws):
            acc_ref[:, off:off + w] += _dot(at, b_ref[...])
            off += w

        @pl.when(k == nk - 1)
        def _():
            for d in range(N_DEV):
                o_ref[d] = acc_ref[:, d * width:(d + 1) * width].astype(BF16)

    return pl.pallas_call(
        body, name=name, grid=(nk,),
        in_specs=[_rows(tk, m)] + [_rows(tk, w) for w in ws],
        out_specs=_fixed((N_DEV, m, width)),
        out_shape=jax.ShapeDtypeStruct((N_DEV, m, width), BF16),
        scratch_shapes=[pltpu.VMEM((m, sum(ws)), F32)],
        compiler_params=_params(("arbitrary",)),
    )(a, *bs)


def _matmul_tn(a, b, name):
    rows, m = a.shape
    n = b.shape[1]
    tk = _row_tile(rows, DW_ROW_TILE)
    tn = n if n <= 1024 else FF_CHUNK
    tm_ = m if m <= 1024 else FF_CHUNK
    assert n % tn == 0 and m % tm_ == 0
    nk = rows // tk

    def body(a_ref, b_ref, o_ref, acc_ref):
        k = pl.program_id(2)

        @pl.when(k == 0)
        def _():
            acc_ref[...] = jnp.zeros_like(acc_ref)

        acc_ref[...] += _dot(a_ref[...], b_ref[...], _TN)

        @pl.when(k == nk - 1)
        def _():
            o_ref[...] = acc_ref[...].astype(BF16)

    return pl.pallas_call(
        body, name=name, grid=(m // tm_, n // tn, nk),
        in_specs=[pl.BlockSpec((tk, tm_), lambda i, j, k: (k, i)), pl.BlockSpec((tk, tn), lambda i, j, k: (k, j))],
        out_specs=pl.BlockSpec((tm_, tn), lambda i, j, k: (i, j)),
        out_shape=jax.ShapeDtypeStruct((m, n), BF16),
        scratch_shapes=[pltpu.VMEM((tm_, tn), F32)],
        compiler_params=_params(("parallel", "parallel", "arbitrary")),
    )(a, b)


HALO = 32
LN_ROWS = 3 * CHUNK
LANES = 128


def _shifted(win, offsets):
    for r in range(8):
        js = [j for j, k in enumerate(offsets) if k % 8 == r]
        if js:
            rolled = win if r == 0 else pltpu.roll(win, CHUNK + HALO - r, 0)
            for j in js:
                yield j, rolled[offsets[j] - r:offsets[j] - r + CHUNK]


def _glu_into(uc_ref, vs_ref, n_chunk):
    vs_ref[0:CHUNK, :] = jnp.zeros((CHUNK, C_CONV), F32)

    def glu(i, carry):
        base = pl.multiple_of(i * CHUNK, CHUNK)
        val = uc_ref[pl.ds(base, CHUNK), 0:C_CONV].astype(F32)
        gate = uc_ref[pl.ds(base, CHUNK), C_CONV:2 * C_CONV].astype(F32)
        vs_ref[pl.ds(base + CHUNK, CHUNK), :] = val * _sigmoid(gate)
        return carry

    lax.fori_loop(0, n_chunk, glu, 0, unroll=3)


def _fwd_conv(uc, conv_w, conv_b, ln_g, ln_b, token, n_ex):
    rows = uc.shape[0]
    lp = rows // n_ex
    n_chunk = lp // CHUNK

    def body(uc_ref, w_ref, b_ref, lg_ref, lb_ref, token_ref, ypre_ref, yc_ref, vs_ref):
        _glu_into(uc_ref, vs_ref, n_chunk)

        def conv(i, carry):
            base = pl.multiple_of(i * CHUNK, CHUNK)
            for lb in range(C_CONV // LANES):
                ls = slice(lb * LANES, (lb + 1) * LANES)
                win = vs_ref[pl.ds(base + CHUNK - HALO, CHUNK + HALO), ls]
                acc = jnp.broadcast_to(b_ref[:, ls], (CHUNK, LANES))
                for j, rows_j in _shifted(win, [HALO - (CONV_W - 1) + j for j in range(CONV_W)]):
                    acc = acc + w_ref[j:j + 1, ls] * rows_j
                ypre_ref[pl.ds(base, CHUNK), ls] = acc
            y = ypre_ref[pl.ds(base, CHUNK), :]
            mu = jnp.mean(y, axis=-1, keepdims=True)
            yc_ = y - mu
            rstd = lax.rsqrt(jnp.mean(yc_ * yc_, axis=-1, keepdims=True) + LN_EPS)
            s = yc_ * rstd * lg_ref[...] + lb_ref[...]
            yc_ref[pl.ds(base, CHUNK), :] = (s * _sigmoid(s)).astype(BF16)
            return carry

        lax.fori_loop(0, n_chunk, conv, 0, unroll=3)

    ex = lambda w: pl.BlockSpec((lp, w), lambda b: (b, 0))
    return pl.pallas_call(
        body, name="fwd_conv", grid=(n_ex,),
        in_specs=[ex(2 * C_CONV), _fixed((32, C_CONV)), _fixed((1, C_CONV)), _fixed((1, C_CONV)), _fixed((1, C_CONV)),
                  _fixed((8, 128))],
        out_specs=[ex(C_CONV), ex(C_CONV)],
        out_shape=[jax.ShapeDtypeStruct((rows, C_CONV), F32), jax.ShapeDtypeStruct((rows, C_CONV), BF16)],
        scratch_shapes=[pltpu.VMEM((lp + CHUNK, C_CONV), F32)],
        compiler_params=_params(("parallel",)),
    )(uc, conv_w, conv_b, ln_g, ln_b, token)


def _bwd_conv(uc, ypre, dyc, conv_w, ln_g, ln_b, token, n_ex):
    rows = uc.shape[0]
    lp = rows // n_ex
    n_chunk = lp // CHUNK

    def body(uc_ref, ypre_ref, dyc_ref, w_ref, lg_ref, lb_ref, token_ref, duc_ref, dw_ref, dvec_ref, vs_ref, dys_ref,
             dwacc_ref):
        _glu_into(uc_ref, vs_ref, n_chunk)
        dys_ref[pl.ds(lp, CHUNK), :] = jnp.zeros((CHUNK, C_CONV), F32)
        dwacc_ref[...] = jnp.zeros_like(dwacc_ref)

        def ln_bwd(i, carry):
            dcb, dlg, dlb = carry
            base = pl.multiple_of(i * LN_ROWS, LN_ROWS)
            y = ypre_ref[pl.ds(base, LN_ROWS), :]
            mu = jnp.mean(y, axis=-1, keepdims=True)
            yc_ = y - mu
            rstd = lax.rsqrt(jnp.mean(yc_ * yc_, axis=-1, keepdims=True) + LN_EPS)
            xh = yc_ * rstd
            s = xh * lg_ref[...] + lb_ref[...]
            sg = _sigmoid(s)
            ds = dyc_ref[pl.ds(base, LN_ROWS), :] * (sg * (1.0 + s * (1.0 - sg)))
            dxh = ds * lg_ref[...]
            dy = rstd * (dxh - jnp.mean(dxh, axis=-1, keepdims=True) - xh * jnp.mean(dxh * xh, axis=-1, keepdims=True))
            dys_ref[pl.ds(base, LN_ROWS), :] = dy
            return (dcb + jnp.sum(dy, axis=0, keepdims=True), dlg + jnp.sum(ds * xh, axis=0, keepdims=True),
                    dlb + jnp.sum(ds, axis=0, keepdims=True))

        zero = jnp.zeros((1, C_CONV), F32)
        dcb, dlg, dlb = lax.fori_loop(0, lp // LN_ROWS, ln_bwd, (zero, zero, zero))

        @pl.when(pl.program_id(0) == 0)
        def _():
            dvec_ref[...] = jnp.zeros_like(dvec_ref)
            dw_ref[...] = jnp.zeros_like(dw_ref)

        dvec_ref[0:1, :] += dcb
        dvec_ref[1:2, :] += dlg
        dvec_ref[2:3, :] += dlb

        def taps(i, carry):
            base = pl.multiple_of(i * CHUNK, CHUNK)
            for lb in range(C_CONV // LANES):
                ls = slice(lb * LANES, (lb + 1) * LANES)
                dwin = dys_ref[pl.ds(base, CHUNK + HALO), ls]
                vwin = vs_ref[pl.ds(base + CHUNK - HALO, CHUNK + HALO), ls]
                dy = dwin[0:CHUNK]
                acc = jnp.zeros((CHUNK, LANES), F32)
                for j, rows_j in _shifted(dwin, [CONV_W - 1 - j for j in range(CONV_W)]):
                    acc = acc + w_ref[j:j + 1, ls] * rows_j
                for j, rows_j in _shifted(vwin, [HALO - (CONV_W - 1) + j for j in range(CONV_W)]):
                    dwacc_ref[8 * j:8 * j + 8, ls] += jnp.sum((dy * rows_j).reshape(CHUNK // 8, 8, LANES), axis=0)
                val = uc_ref[pl.ds(base, CHUNK), ls].astype(F32)
                gate = uc_ref[pl.ds(base, CHUNK), C_CONV + lb * LANES:C_CONV + (lb + 1) * LANES].astype(F32)
                sg = _sigmoid(gate)
                duc_ref[pl.ds(base, CHUNK), ls] = (acc * sg).astype(BF16)
                duc_ref[pl.ds(base, CHUNK), C_CONV + lb * LANES:C_CONV + (lb + 1) * LANES] = (
                    acc * val * sg * (1.0 - sg)).astype(BF16)
            return carry

        lax.fori_loop(0, n_chunk, taps, 0, unroll=3)
        for j in range(CONV_W):
            dw_ref[j:j + 1, :] += jnp.sum(dwacc_ref[8 * j:8 * j + 8, :], axis=0, keepdims=True)

    ex = lambda w: pl.BlockSpec((lp, w), lambda b: (b, 0))
    return pl.pallas_call(
        body, name="bwd_conv", grid=(n_ex,),
        in_specs=[ex(2 * C_CONV), ex(C_CONV), ex(C_CONV), _fixed((32, C_CONV)), _fixed((1, C_CONV)), _fixed((1, C_CONV)),
                  _fixed((8, 128))],
        out_specs=[ex(2 * C_CONV), _fixed((32, C_CONV)), _fixed((8, C_CONV))],
        out_shape=[jax.ShapeDtypeStruct((rows, 2 * C_CONV), BF16), jax.ShapeDtypeStruct((32, C_CONV), F32),
                   jax.ShapeDtypeStruct((8, C_CONV), F32)],
        scratch_shapes=[pltpu.VMEM((lp + CHUNK, C_CONV), F32), pltpu.VMEM((lp + CHUNK, C_CONV), F32),
                        pltpu.VMEM((8 * 32, C_CONV), F32)],
        compiler_params=_params(("arbitrary",)),
    )(uc, ypre, dyc, conv_w, ln_g, ln_b, token)


def _seg_chunks(n_chunk):
    return max(c for c in (11, 3, 1) if n_chunk % c == 0)


def _block_mask(shape, row_block, lane_block):
    return (lax.broadcasted_iota(jnp.int32, shape, 0) // row_block) == (lax.broadcasted_iota(jnp.int32, shape, 1) // lane_block)


def _per_head_rows(x, mask):
    return jnp.where(mask, jnp.concatenate([x] * GLA_H, axis=0), 0)


def _fold_heads(full, lane_block):
    lane = lax.broadcasted_iota(jnp.int32, (1, full.shape[1]), 1) // lane_block
    out = jnp.where(lane == 0, full[0:CHUNK], 0.0)
    for h in range(1, GLA_H):
        out = out + jnp.where(lane == h, full[h * CHUNK:(h + 1) * CHUNK], 0.0)
    return out


PAIRS = GLA_H // 2


def _expand_state(blocks):
    lane = lax.broadcasted_iota(jnp.int32, (GLA_DV, 128), 1) // GLA_DK
    zero = jnp.zeros_like(blocks[0])
    rows = []
    for h in range(GLA_H):
        p, hh = divmod(h, 2)
        mine = jnp.where(lane == hh, blocks[p], 0)
        rows.append(jnp.concatenate([mine if q == p else zero for q in range(PAIRS)], axis=1))
    return jnp.concatenate(rows, axis=0)


def _compact_state(full, p):
    lane = lax.broadcasted_iota(jnp.int32, (GLA_DV, 128), 1) // GLA_DK
    ls = slice(128 * p, 128 * (p + 1))
    return jnp.where(lane == 0, full[2 * p * GLA_DV:(2 * p + 1) * GLA_DV, ls], full[(2 * p + 1) * GLA_DV:(2 * p + 2) * GLA_DV, ls])


def _causal_heads():
    return (lax.broadcasted_iota(jnp.int32, (CHUNK, GLA_H * CHUNK), 1) % CHUNK) <= lax.broadcasted_iota(
        jnp.int32, (CHUNK, GLA_H * CHUNK), 0)


def _cumsum_rows(x):
    row = lax.broadcasted_iota(jnp.int32, x.shape, 0)
    s = 1
    while s < CHUNK:
        x = x + jnp.where(row >= s, pltpu.roll(x, s, 0), 0.0)
        s *= 2
    return x


def _rev_cumsum_rows(x):
    row = lax.broadcasted_iota(jnp.int32, x.shape, 0)
    s = 1
    while s < CHUNK:
        x = x + jnp.where(row < CHUNK - s, pltpu.roll(x, CHUNK - s, 0), 0.0)
        s *= 2
    return x


def _gate_terms(lr_ref, w2_ref, gb_ref, rs, first_pos):
    z = _dot(lr_ref[rs, :].astype(BF16), w2_ref[...]) + gb_ref[...]
    la = (jnp.minimum(z, 0.0) - jnp.log(1.0 + jnp.exp(-jnp.abs(z)))) * (1.0 / TAU)
    pos = first_pos + lax.broadcasted_iota(jnp.int32, (CHUNK, 1), 0)
    live = pos >= ZROWS
    la = jnp.where(live, la, 0.0)
    return z, live, _cumsum_rows(la)


def _fwd_gla(qk, vg, lr, w2p, gb, ng, token, n_ex):
    rows = qk.shape[0]
    lp = rows // n_ex
    n_chunk = lp // CHUNK
    sc = _seg_chunks(n_chunk)
    n_seg = n_chunk // sc
    seg = sc * CHUNK

    def body(qk_ref, vg_ref, lr_ref, w2_ref, gb_ref, ng_ref, token_ref, yg_ref, o_ref, st_ref, state_ref):
        sidx = pl.program_id(1)

        @pl.when(sidx == 0)
        def _():
            state_ref[...] = jnp.zeros_like(state_ref)

        causal = _causal_heads()
        k_mask = _block_mask((GLA_H * CHUNK, GLA_K), CHUNK, GLA_DK)
        v_mask = _block_mask((GLA_H * CHUNK, GLA_V), CHUNK, GLA_DV)

        def chunk(ci, carry):
            base = pl.multiple_of(ci * CHUNK, CHUNK)
            rs = pl.ds(base, CHUNK)
            _, _, bcum = _gate_terms(lr_ref, w2_ref, gb_ref, rs, (sidx * sc + ci) * CHUNK)
            bl = bcum[CHUNK - 1:CHUNK, :]
            q = qk_ref[rs, 0:GLA_K].astype(F32)
            k = qk_ref[rs, GLA_K:2 * GLA_K].astype(F32)
            qt = (q * (GLA_DK ** -0.5) * jnp.exp(bcum)).astype(BF16)
            kt = (k * jnp.exp(-bcum)).astype(BF16)
            kh = (k * jnp.exp(bl - bcum)).astype(BF16)
            vb = vg_ref[rs, 0:GLA_V].astype(BF16)
            state = [state_ref[p] for p in range(PAIRS)]
            for p in range(PAIRS):
                st_ref[ci, p] = state[p]
            a = jnp.where(causal, _dot(qt, _per_head_rows(kt, k_mask), _NT), 0.0)
            o = _dot(a.astype(BF16), _per_head_rows(vb, v_mask)) + _dot(
                qt, _expand_state([s.astype(BF16) for s in state]), _NT)
            o_ref[rs, :] = o
            for h in range(GLA_H):
                hs = slice(h * GLA_DV, (h + 1) * GLA_DV)
                oh = o[:, hs]
                ro = lax.rsqrt(jnp.mean(oh * oh, axis=-1, keepdims=True) + RMS_EPS)
                g = vg_ref[rs, GLA_V + h * GLA_DV:GLA_V + (h + 1) * GLA_DV].astype(F32)
                yg_ref[rs, hs] = (oh * ro * ng_ref[...] * g * _sigmoid(g)).astype(BF16)
            kv = _dot(vb, kh, _TN)
            decay = jnp.exp(bl)
            for p in range(PAIRS):
                state_ref[p] = state[p] * decay[:, 128 * p:128 * (p + 1)] + _compact_state(kv, p)
            return carry

        lax.fori_loop(0, sc, chunk, 0, unroll=True)

    sg = lambda w: pl.BlockSpec((seg, w), lambda b, s: (b * n_seg + s, 0))
    return pl.pallas_call(
        body, name="fwd_gla", grid=(n_ex, n_seg),
        in_specs=[sg(2 * GLA_K), sg(2 * GLA_V), sg(RANK_P), _fixed((RANK_P, GLA_K)), _fixed((1, GLA_K)), _fixed((1, GLA_DV)),
                  _fixed((8, 128))],
        out_specs=[sg(GLA_V), sg(GLA_V), pl.BlockSpec((sc, PAIRS, GLA_DV, 128), lambda b, s: (b * n_seg + s, 0, 0, 0))],
        out_shape=[jax.ShapeDtypeStruct((rows, GLA_V), BF16), jax.ShapeDtypeStruct((rows, GLA_V), F32),
                   jax.ShapeDtypeStruct((n_ex * n_chunk, PAIRS, GLA_DV, 128), F32)],
        scratch_shapes=[pltpu.VMEM((PAIRS, GLA_DV, 128), F32)],
        compiler_params=_params(("parallel", "arbitrary")),
    )(qk, vg, lr, w2p, gb, ng, token)


def _bwd_gla(qk, vg, lr, o, st, dyg, w2p, gb, ng, yc, yg, dh1b, token, n_ex):
    rows = qk.shape[0]
    lp = rows // n_ex
    n_chunk = lp // CHUNK
    sc = _seg_chunks(n_chunk)
    n_seg = n_chunk // sc
    seg = sc * CHUNK

    def body(qk_ref, vg_ref, lr_ref, o_ref, st_ref, dyg_ref, w2_ref, gb_ref, ng_ref, yc_ref, yg_ref, dh1_ref, token_ref,
             dqk_ref, dvg_ref, dlr_ref, dw2_ref, dvec_ref, dwo_ref, gt_ref, dz_ref, dwo_acc):
        step = pl.program_id(1)
        sidx = n_seg - 1 - step
        first = (step == 0) & (pl.program_id(0) == 0)

        @pl.when(step == 0)
        def _():
            gt_ref[...] = jnp.zeros_like(gt_ref)

        @pl.when(first)
        def _():
            dw2_ref[...] = jnp.zeros_like(dw2_ref)
            dvec_ref[...] = jnp.zeros_like(dvec_ref)
            dwo_acc[...] = jnp.zeros_like(dwo_acc)

        d1 = dh1_ref[...]
        dwo_acc[0:C_CONV, :] += _dot(yc_ref[...], d1, _TN)
        dwo_acc[C_CONV:D, :] += _dot(yg_ref[...], d1, _TN)

        @pl.when((step == n_seg - 1) & (pl.program_id(0) == n_ex - 1))
        def _():
            dwo_ref[...] = dwo_acc[...].astype(BF16)

        causal = _causal_heads()
        k_mask = _block_mask((GLA_H * CHUNK, GLA_K), CHUNK, GLA_DK)
        v_mask = _block_mask((GLA_H * CHUNK, GLA_V), CHUNK, GLA_DV)
        last_row = lax.broadcasted_iota(jnp.int32, (CHUNK, 1), 0) == CHUNK - 1
        ng = ng_ref[...]

        def chunk(ii, dng):
            ci = sc - 1 - ii
            base = pl.multiple_of(ci * CHUNK, CHUNK)
            rs = pl.ds(base, CHUNK)
            z, live, bcum = _gate_terms(lr_ref, w2_ref, gb_ref, rs, (sidx * sc + ci) * CHUNK)
            bl = bcum[CHUNK - 1:CHUNK, :]
            ebl = jnp.exp(bl)
            q = qk_ref[rs, 0:GLA_K].astype(F32)
            k = qk_ref[rs, GLA_K:2 * GLA_K].astype(F32)
            eb = jnp.exp(bcum)
            enb = jnp.exp(-bcum)
            ehb = jnp.exp(bl - bcum)
            qt = q * (GLA_DK ** -0.5) * eb
            kt = k * enb
            kh = k * ehb
            qtb = qt.astype(BF16)
            vb = vg_ref[rs, 0:GLA_V].astype(BF16)
            k_rows = _per_head_rows(kt.astype(BF16), k_mask)
            v_rows = _per_head_rows(vb, v_mask)
            gt = [gt_ref[p] for p in range(PAIRS)]
            gtb = _expand_state([g_.astype(BF16) for g_ in gt])
            s_in = [st_ref[ci, p] for p in range(PAIRS)]
            dos = []
            for h in range(GLA_H):
                hs = slice(h * GLA_DV, (h + 1) * GLA_DV)
                gs = slice(GLA_V + h * GLA_DV, GLA_V + (h + 1) * GLA_DV)
                oh = o_ref[rs, hs]
                ro = lax.rsqrt(jnp.mean(oh * oh, axis=-1, keepdims=True) + RMS_EPS)
                on = oh * ro
                g = vg_ref[rs, gs].astype(F32)
                sg = _sigmoid(g)
                dout = dyg_ref[rs, hs]
                dvg_ref[rs, gs] = (dout * on * ng * (sg * (1.0 + g * (1.0 - sg)))).astype(BF16)
                dw = dout * g * sg
                dng = dng + jnp.sum(dw * on, axis=0, keepdims=True)
                don = dw * ng
                dos.append((ro * (don - on * jnp.mean(don * on, axis=-1, keepdims=True))).astype(BF16))
            dob = jnp.concatenate(dos, axis=1)
            a = jnp.where(causal, _dot(qtb, k_rows, _NT), 0.0).astype(BF16)
            da = jnp.where(causal, _dot(dob, v_rows, _NT), 0.0).astype(BF16)
            dv = _fold_heads(_dot(a, dob, _TN), GLA_DV) + _dot(kh.astype(BF16), gtb, _NT)
            dvg_ref[rs, 0:GLA_V] = dv.astype(BF16)
            dkh = _dot(vb, gtb)
            dqt = _dot(da, k_rows) + _dot(dob, _expand_state([s_.astype(BF16) for s_ in s_in]))
            dkt = _fold_heads(_dot(da, qtb, _TN), GLA_DK)
            dbl = jnp.concatenate([jnp.sum(gt[p] * s_in[p], axis=0, keepdims=True) for p in range(PAIRS)], axis=1) * ebl
            dbl = dbl + jnp.sum(dkh * kh, axis=0, keepdims=True)
            dqk_ref[rs, 0:GLA_K] = (dqt * (GLA_DK ** -0.5) * eb).astype(BF16)
            dqk_ref[rs, GLA_K:2 * GLA_K] = (dkt * enb + dkh * ehb).astype(BF16)
            db = dqt * qt - dkt * kt - dkh * kh
            db = jnp.where(last_row, db + dbl, db)
            dla = jnp.where(live, _rev_cumsum_rows(db), 0.0)
            dz_ref[rs, :] = dla * (1.0 / TAU) * (1.0 - _sigmoid(z))
            dstate = _dot(dob, qtb, _TN)
            for p in range(PAIRS):
                gt_ref[p] = _compact_state(dstate, p) + gt[p] * ebl[:, 128 * p:128 * (p + 1)]
            return dng

        dng = lax.fori_loop(0, sc, chunk, jnp.zeros((1, GLA_DV), F32), unroll=True)
        dz = dz_ref[...]
        dzb = dz.astype(BF16)
        dlr_ref[...] = _dot(dzb, w2_ref[...], _NT).astype(BF16)
        dw2_ref[...] += _dot(lr_ref[...].astype(BF16), dzb, _TN)
        dvec_ref[0:1, :] += jnp.sum(dz, axis=0, keepdims=True)
        dvec_ref[1:2, 0:GLA_DV] += dng

    sg_ = lambda w: pl.BlockSpec((seg, w), lambda b, s: (b * n_seg + n_seg - 1 - s, 0))
    return pl.pallas_call(
        body, name="bwd_gla", grid=(n_ex, n_seg),
        in_specs=[sg_(2 * GLA_K), sg_(2 * GLA_V), sg_(RANK_P), sg_(GLA_V),
                  pl.BlockSpec((sc, PAIRS, GLA_DV, 128), lambda b, s: (b * n_seg + n_seg - 1 - s, 0, 0, 0)), sg_(GLA_V),
                  _fixed((RANK_P, GLA_K)), _fixed((1, GLA_K)), _fixed((1, GLA_DV)), sg_(C_CONV), sg_(GLA_V), sg_(D),
                  _fixed((8, 128))],
        out_specs=[sg_(2 * GLA_K), sg_(2 * GLA_V), sg_(RANK_P), _fixed((RANK_P, GLA_K)), _fixed((8, GLA_K)),
                   _fixed((D, D))],
        out_shape=[jax.ShapeDtypeStruct((rows, 2 * GLA_K), BF16), jax.ShapeDtypeStruct((rows, 2 * GLA_V), BF16),
                   jax.ShapeDtypeStruct((rows, RANK_P), BF16), jax.ShapeDtypeStruct((RANK_P, GLA_K), F32),
                   jax.ShapeDtypeStruct((8, GLA_K), F32), jax.ShapeDtypeStruct((D, D), BF16)],
        scratch_shapes=[pltpu.VMEM((PAIRS, GLA_DV, 128), F32), pltpu.VMEM((seg, GLA_K), F32), pltpu.VMEM((D, D), F32)],
        compiler_params=_params(("arbitrary", "arbitrary")),
    )(qk, vg, lr, o, st, dyg, w2p, gb, ng, yc, yg, dh1b, token)


def _pad_rows(x, tgt):
    return jnp.pad(x, ((0, 0), (LEAD, 0), (0, 0))), jnp.pad(tgt, ((0, 0), (LEAD, 0), (0, 0)))


def _local_step(h0, tgt_p, p, pass_on, late_weights, send_early):
    n_ex, lp, _ = h0.shape
    rows = n_ex * lp
    meta = jnp.broadcast_to(p["meta"][None], (n_ex, N_META, D))
    h0 = lax.dynamic_update_slice(h0, meta, (0, ZROWS, 0)).reshape(rows, D)
    tgt_p = tgt_p.reshape(rows, D)

    uc, qk, vg, lr, n1 = _fwd_inproj(h0, p["g1"], p["w_in"])
    ypre, yc = _fwd_conv(uc, p["conv_w"], p["conv_b"], p["ln_g"], p["ln_b"], p["token"], n_ex)
    token = pass_on(yc)
    yg, o, st = _fwd_gla(qk, vg, lr, p["w2"], p["gb"], p["ng"], token, n_ex)
    w_out, wg, wu, wd = late_weights(yg)
    n2, f, da, db, dh2, dh1, dh1b, dyc, dyg, part = _mid_rows(
        yc, yg, h0, tgt_p, w_out, wg, wu, wd, p["g2"], p["g3"], token, lp)
    g = {}
    token = send_early("ffn", [_matmul_tn(a_, b_, name).reshape(N_DEV, FF_S, D) for a_, b_, name in (
        (da, n2, "dw_gate"), (db, n2, "dw_up"), (f, dh2, "dw_down"))])
    dqk, dvg, dlr, g["w2"], g["gla_vec"], dw_out = _bwd_gla(
        qk, vg, lr, o, st, dyg, p["w2"], p["gb"], p["ng"], yc, yg, dh1b, token, n_ex)
    token = send_early("out", [dw_out.reshape(N_DEV, W_OUT_S, D)])
    duc, g["conv_w"], g["conv_vec"] = _bwd_conv(uc, ypre, dyc, p["conv_w"], p["ln_g"], p["ln_b"], token, n_ex)
    token = send_early("in", [_dw_blocked(n1, [duc, dqk, dvg, dlr], W_IN_S, "dw_in")])
    grad_x, g["in_vec"], g["meta"] = _bwd_inproj(duc, dqk, dvg, dlr, dh1, h0, p["w_in"], p["g1"], token, lp)
    g["ffn_vec"] = part
    return grad_x, g


W_IN_S = D_IN // N_DEV
W_OUT_S = D // N_DEV
FF_S = D_FF // N_DEV
CONV_S = C_CONV // N_DEV
GATE_S = GLA_K // N_DEV
SMALL_PACK = 64
CONV_ROW = 16
GATE_ROW = 48
VEC_ROWS = 16
_VEC_ROWS = (("norm_mix_g", D), ("conv_b", C_CONV), ("conv_ln_g", C_CONV), ("conv_ln_b", C_CONV), ("gla_gate_b", GLA_K),
             ("gla_norm_g", GLA_DV), ("norm_ffn_g", D), ("norm_final_g", D))
LOSS_ROW = len(_VEC_ROWS)


def _position():
    return lax.axis_index("x"), lax.axis_index("y"), lax.axis_index("c")


def _any():
    return pl.BlockSpec(memory_space=pl.ANY)


def _stage(mats, meta, conv_w, w2):
    n_t = len(mats) + 1

    def body(*refs):
        ins = refs[0:n_t - 1]
        meta_ref, cw_ref, w2_ref = refs[n_t - 1:n_t + 2]
        lands = refs[n_t + 2:2 * n_t + 2]
        shards = refs[2 * n_t + 2:3 * n_t + 2]
        sems = refs[3 * n_t + 2]
        for s_ref, w_ref in zip(shards, ins):
            s_ref[...] = w_ref[...].astype(BF16)
        sp = shards[n_t - 1]
        sp[...] = jnp.zeros_like(sp)
        sp[0:N_META, :] = meta_ref[...]
        sp[CONV_ROW:CONV_ROW + CONV_W, 0:CONV_S] = cw_ref[...]
        sp[GATE_ROW:GATE_ROW + RANK, 0:GATE_S] = w2_ref[...]
        x, y, c = _position()
        mine = [pltpu.make_async_copy(shards[t], lands[t].at[4 * x + 2 * y + c], sems.at[t]) for t in range(n_t)]
        for cp in mine:
            cp.start()
        for cp in mine:
            cp.wait()

    shard_shapes = [jax.ShapeDtypeStruct(m.shape, BF16) for m in mats] + [jax.ShapeDtypeStruct((SMALL_PACK, 128), F32)]
    res = pl.pallas_call(
        body, name="stage",
        out_shape=[jax.ShapeDtypeStruct((N_DEV,) + s.shape, s.dtype) for s in shard_shapes] + shard_shapes,
        in_specs=[_whole_vmem()] * (n_t + 2), out_specs=[_any()] * n_t + [_whole_vmem()] * n_t,
        scratch_shapes=[pltpu.SemaphoreType.DMA((n_t,))],
        compiler_params=pltpu.CompilerParams(vmem_limit_bytes=VMEM_LIMIT),
    )(*mats, meta, conv_w, w2)
    return res[0:n_t], res[n_t:]


_HBM = pl.BlockSpec(memory_space=pltpu.HBM)
_SEM = pl.BlockSpec(memory_space=pltpu.SEMAPHORE)
_EFFECT = pltpu.SideEffectType.DATAFLOW_SIDE_EFFECTING


_N_ROUTES = {"scatter": 7, "first": 4, "forward": 3}


def _routes(mode):
    x, y, c = _position()
    me = 4 * x + 2 * y + c
    if mode == "scatter":
        out = []
        for k in range(1, N_DEV):
            px = 1 - x if k & 4 else x
            py = 1 - y if k & 2 else y
            pc = 1 - c if k & 1 else c
            out.append(((px, py, pc), 4 * px + 2 * py + pc, me))
        return out
    if mode == "first":
        return [(pos, None, me) for pos in ((x, y, 1 - c), (1 - x, y, c), (x, 1 - y, c), (1 - x, 1 - y, c))]
    assert mode == "forward"
    return [((x, y, 1 - c), 4 * px + 2 * py + c, 4 * px + 2 * py + c) for px, py in ((1 - x, y), (x, 1 - y), (1 - x, 1 - y))]


def _route_copies(mode, n, src_refs, land_refs, send_sems, recv_sems):
    nr = _N_ROUTES[mode]
    for i, (pos, src_blk, dst_blk) in enumerate(_routes(mode)):
        for t in range(n):
            src = land_refs[t] if mode == "forward" else src_refs[t]
            yield pltpu.make_async_remote_copy(
                src_ref=src if src_blk is None else src.at[src_blk], dst_ref=land_refs[t].at[dst_blk],
                send_sem=send_sems.at[nr * t + i], recv_sem=recv_sems.at[nr * t + i], device_id=pos, device_id_type=MESH)


def _in_hbm(a):
    return pltpu.with_memory_space_constraint(a, pltpu.HBM)


def _send_start(name, srcs, lands, mode, after):
    n, ns = len(lands), len(srcs)
    nsem = _N_ROUTES[mode] * n

    def body(*refs):
        src_refs, land_refs = refs[0:ns], refs[ns:ns + n]
        send_sems, recv_sems = refs[ns + n + 1:ns + n + 3]
        token = refs[2 * (ns + n) + 3]
        for cp in _route_copies(mode, n, src_refs, land_refs, send_sems, recv_sems):
            cp.start()
        token[...] = jnp.zeros_like(token)

    bufs = list(srcs) + list(lands)
    res = pl.pallas_call(
        body, name=name,
        out_shape=(pltpu.SemaphoreType.DMA((nsem,)), pltpu.SemaphoreType.DMA((nsem,)),
                   *[pltpu.HBM(b.shape, b.dtype) for b in bufs], jax.ShapeDtypeStruct((8, 128), F32)),
        in_specs=[_HBM] * len(bufs) + [_any()], out_specs=(_SEM, _SEM, *[_HBM] * len(bufs), _whole_vmem()),
        input_output_aliases={i: 2 + i for i in range(len(bufs))},
        compiler_params=pltpu.CompilerParams(has_side_effects=_EFFECT),
    )(*[_in_hbm(b) for b in bufs], after)
    return res[0], res[1], res[2:2 + ns], res[2 + ns:2 + ns + n], res[2 + ns + n]


def _send_wait(name, send_sems, recv_sems, srcs, lands, mode, after):
    n, ns = len(lands), len(srcs)
    after = after if isinstance(after, tuple) else (after,)

    def body(*refs):
        src_refs, land_refs = refs[0:ns], refs[ns:ns + n]
        send_sems, recv_sems = refs[ns + n:ns + n + 2]
        for cp in _route_copies(mode, n, src_refs, land_refs, send_sems, recv_sems):
            cp.wait_send()
            cp.wait_recv()

    bufs = list(srcs) + list(lands)
    res = pl.pallas_call(
        body, name=name,
        out_shape=tuple(pltpu.HBM(b.shape, b.dtype) for b in bufs),
        in_specs=[_HBM] * len(bufs) + [_SEM, _SEM] + [_any()] * len(after), out_specs=tuple([_HBM] * len(bufs)),
        input_output_aliases={i: i for i in range(len(bufs))},
        compiler_params=pltpu.CompilerParams(has_side_effects=_EFFECT),
    )(*bufs, send_sems, recv_sems, *after)
    return res[0:ns], res[ns:ns + n]


def _unshard_in(a_in, a_small, token):
    def body(a_ref, s_ref, token_ref, w_ref, meta_ref, cw_ref, w2_ref):
        w_ref[:, D_IN:D_INP] = jnp.zeros((D, D_INP - D_IN), BF16)
        w2_ref[...] = jnp.zeros_like(w2_ref)
        for d in range(N_DEV):
            w_ref[:, d * W_IN_S:(d + 1) * W_IN_S] = a_ref[d]
            meta_ref[:, d * 128:(d + 1) * 128] = s_ref[d, 0:N_META, :]
            cw_ref[:, d * CONV_S:(d + 1) * CONV_S] = s_ref[d, CONV_ROW:CONV_ROW + 32, 0:CONV_S]
            w2_ref[0:RANK, d * GATE_S:(d + 1) * GATE_S] = s_ref[d, GATE_ROW:GATE_ROW + RANK, 0:GATE_S].astype(BF16)

    return pl.pallas_call(
        body, name="unshard_in",
        out_shape=[jax.ShapeDtypeStruct((D, D_INP), BF16), jax.ShapeDtypeStruct((N_META, D), F32),
                   jax.ShapeDtypeStruct((32, C_CONV), F32), jax.ShapeDtypeStruct((RANK_P, GLA_K), BF16)],
        compiler_params=pltpu.CompilerParams(vmem_limit_bytes=VMEM_LIMIT),
    )(a_in, a_small, token)


def _pack_small(g):
    def body(meta_ref, cw_ref, w2_ref, in_vec, ffn_vec, conv_vec, gla_vec, sp, vp):
        sp[...] = jnp.zeros_like(sp)
        vp[...] = jnp.zeros_like(vp)
        for d in range(N_DEV):
            sp[d, 0:N_META, :] = meta_ref[:, d * 128:(d + 1) * 128]
            sp[d, CONV_ROW:CONV_ROW + 32, 0:CONV_S] = cw_ref[:, d * CONV_S:(d + 1) * CONV_S]
            sp[d, GATE_ROW:GATE_ROW + RANK, 0:GATE_S] = w2_ref[0:RANK, d * GATE_S:(d + 1) * GATE_S]
            vp[d, 0:1, :] = in_vec[0:1, :]
            vp[d, 1:4, 0:C_CONV] = conv_vec[0:3, :]
            vp[d, 4:5, 0:GLA_K] = gla_vec[0:1, :]
            vp[d, 5:6, 0:GLA_DV] = gla_vec[1:2, 0:GLA_DV]
            vp[d, 6:7, :] = ffn_vec[1:2, :]
            vp[d, 7:8, :] = ffn_vec[0:1, :]
            vp[d, LOSS_ROW:LOSS_ROW + 1, :] = ffn_vec[2:3, :]

    return pl.pallas_call(
        body, name="pack_small",
        out_shape=[jax.ShapeDtypeStruct((N_DEV, SMALL_PACK, 128), F32), jax.ShapeDtypeStruct((N_DEV, VEC_ROWS, D), F32)],
    )(g["meta"], g["conv_w"], g["w2"], g["in_vec"], g["ffn_vec"], g["conv_vec"], g["gla_vec"])


def _adamw(w, g, m, v):
    m = ADAM_B1 * m + (1.0 - ADAM_B1) * g
    v = ADAM_B2 * v + (1.0 - ADAM_B2) * (g * g)
    m_hat = m / (1.0 - ADAM_B1 ** ADAM_STEP)
    v_hat = v / (1.0 - ADAM_B2 ** ADAM_STEP)
    return -ADAM_LR * (m_hat / (jnp.sqrt(v_hat) + ADAM_EPS) + ADAM_WD * w), m, v


def _update_matrix(recv, own, me, w, m, v, name):
    _, r, c = recv.shape
    tr = _row_tile(r, 256)

    def body(me_ref, recv_ref, own_ref, w_ref, m_ref, v_ref, g_ref, d_ref, nm_ref, nv_ref):
        g = jnp.zeros((tr, c), F32)
        for s in range(N_DEV):
            g = g + jnp.where(me_ref[0] == s, own_ref[...], recv_ref[s]).astype(F32)
        g_ref[...] = g
        d_ref[...], nm_ref[...], nv_ref[...] = _adamw(w_ref[...], g, m_ref[...], v_ref[...])

    one = pl.BlockSpec((None, tr, c), lambda i, me_ref: (0, i, 0))
    return pl.pallas_call(
        body, name=name,
        grid_spec=pltpu.PrefetchScalarGridSpec(
            num_scalar_prefetch=1, grid=(r // tr,),
            in_specs=[pl.BlockSpec((N_DEV, tr, c), lambda i, me_ref: (0, i, 0)),
                      pl.BlockSpec((None, tr, c), lambda i, me_ref: (me_ref[0], i, 0)), one, one, one],
            out_specs=[one] * 4),
        out_shape=[jax.ShapeDtypeStruct((1, r, c), F32)] * 4,
        compiler_params=_params(("parallel",)),
    )(me, recv, own, w, m, v)


_SMALL = ("meta_tokens", "conv_w", "gla_w_gate2") + tuple(n for n, _ in _VEC_ROWS)


def _update_small(me, srecv, vrecv, sown, vown, w, m, v):
    n = len(_SMALL)

    def body(*refs):
        me_ref, s_ref, v_ref, so_ref, vo_ref = refs[0:5]
        w_refs, m_refs, v_refs = refs[5:5 + n], refs[5 + n:5 + 2 * n], refs[5 + 2 * n:5 + 3 * n]
        outs = refs[5 + 3 * n:]
        ssum = jnp.zeros((SMALL_PACK, 128), F32)
        vsum = jnp.zeros((VEC_ROWS, D), F32)
        for s in range(N_DEV):
            ssum = ssum + jnp.where(me_ref[0] == s, so_ref[s], s_ref[s])
            vsum = vsum + jnp.where(me_ref[0] == s, vo_ref[s], v_ref[s])
        grads = [ssum[0:N_META, :], ssum[CONV_ROW:CONV_ROW + CONV_W, 0:CONV_S], ssum[GATE_ROW:GATE_ROW + RANK, 0:GATE_S]]
        grads += [vsum[i:i + 1, 0:width] for i, (_, width) in enumerate(_VEC_ROWS)]
        for i, g in enumerate(grads):
            d, nm, nv = _adamw(w_refs[i][...], g, m_refs[i][...], v_refs[i][...])
            outs[i][...] = g
            outs[n + i][...] = d
            outs[2 * n + i][...] = nm
            outs[3 * n + i][...] = nv
        outs[4 * n][...] = vsum[LOSS_ROW:LOSS_ROW + 1, 0:128]

    shapes = [jax.ShapeDtypeStruct(t.shape, F32) for t in w]
    res = pl.pallas_call(
        body, name="update_small", out_shape=shapes * 4 + [jax.ShapeDtypeStruct((1, 128), F32)],
        in_specs=[pl.BlockSpec(memory_space=pltpu.SMEM)] + [_whole_vmem()] * (4 + 3 * n),
    )(me, srecv, vrecv, sown, vown, *w, *m, *v)
    return res[0:n], res[n:2 * n], res[2 * n:3 * n], res[3 * n:4 * n], res[4 * n]


_WEIGHTS = ("meta_tokens", "norm_mix_g", "w_in", "conv_w", "conv_b", "conv_ln_g", "conv_ln_b", "gla_w_gate2", "gla_gate_b",
            "gla_norm_g", "w_out", "norm_ffn_g", "w_ffn_gate", "w_ffn_up", "w_ffn_down", "norm_final_g")
_MATRICES = ("w_in", "w_out", "w_ffn_gate", "w_ffn_up", "w_ffn_down")
_TRANSPOSED = ("w_ffn_gate", "w_ffn_up")


def kernel(x, meta_tokens, norm_mix_g, w_in, conv_w, conv_b, conv_ln_g, conv_ln_b, gla_w_gate2, gla_gate_b, gla_norm_g, w_out, norm_ffn_g, w_ffn_gate, w_ffn_up, w_ffn_down, norm_final_g, loss_target, m_meta_tokens, m_norm_mix_g, m_w_in, m_conv_w, m_conv_b, m_conv_ln_g, m_conv_ln_b, m_gla_w_gate2, m_gla_gate_b, m_gla_norm_g, m_w_out, m_norm_ffn_g, m_w_ffn_gate, m_w_ffn_up, m_w_ffn_down, m_norm_final_g, v_meta_tokens, v_norm_mix_g, v_w_in, v_conv_w, v_conv_b, v_conv_ln_g, v_conv_ln_b, v_gla_w_gate2, v_gla_gate_b, v_gla_norm_g, v_w_out, v_norm_ffn_g, v_w_ffn_gate, v_w_ffn_up, v_w_ffn_down, v_norm_final_g):
    given = dict(locals())
    two_d = lambda a: a.reshape(1, -1) if a.ndim == 1 else a.reshape(a.shape[-2:])
    fams = [{n: given[pre + n] for n in _WEIGHTS} for pre in ("", "m_", "v_")]
    for f in fams:
        for n in _TRANSPOSED:
            f[n] = f[n].transpose(0, 2, 1)
    w = fams[0]

    lands, shards = _stage([two_d(w[n]) for n in _MATRICES], w["meta_tokens"], two_d(w["conv_w"]), two_d(w["gla_w_gate2"]))
    soon, later = (0, 5), (1, 2, 3, 4)
    pick = lambda seq, idx: [seq[i] for i in idx]
    first = _send_start("gather_first_start", pick(shards, soon), pick(lands, soon), "first", norm_mix_g)
    ffn_first = _send_start("gather_ffn_first_start", pick(shards, later), pick(lands, later), "first", first[4])
    h0, tgt_p = _pad_rows(x, loss_target)
    _, arrived = _send_wait("gather_first_wait", *first[0:4], "first", (h0, tgt_p, ffn_first[4]))
    forward = _send_start("gather_forward_start", [], arrived, "forward", ffn_first[4])
    _, (a_in, a_small) = _send_wait("gather_forward_wait", *forward[0:4], "forward", forward[4])
    w_in, meta, conv_taps, w2 = _unshard_in(a_in, a_small, forward[4])
    p = dict(meta=meta, conv_w=conv_taps, w2=w2, w_in=w_in, g1=norm_mix_g, conv_b=conv_b, ln_g=conv_ln_g, ln_b=conv_ln_b,
             gb=gla_gate_b, ng=gla_norm_g, g2=norm_ffn_g, g3=two_d(norm_final_g), token=forward[4])
    passed = {}

    def pass_on(after):
        _, arrived_ffn = _send_wait("gather_ffn_first_wait", *ffn_first[0:4], "first", after)
        passed["sent"] = _send_start("gather_ffn_forward_start", [], arrived_ffn, "forward", after)
        return passed["sent"][4]

    def late_weights(after):
        _, (a_out, a_g, a_u, a_d) = _send_wait("gather_ffn_forward_wait", *passed["sent"][0:4], "forward", after)
        return a_out.reshape(D, D), a_g.reshape(D_FF, D), a_u.reshape(D_FF, D), a_d.reshape(D_FF, D)

    sent = {}

    def send_early(tag, mats):
        landing = [_in_hbm(lax.empty(m_.shape, m_.dtype)) for m_ in mats]
        sent[tag] = _send_start("scatter_" + tag + "_start", mats, landing, "scatter", norm_mix_g)
        return sent[tag][4]

    grad_x, g = _local_step(h0, tgt_p, p, pass_on, late_weights, send_early)

    token = send_early("small", list(_pack_small(g)))
    x_, y_, c_ = _position()
    me = (4 * x_ + 2 * y_ + c_).astype(jnp.int32).reshape(1)
    res = {}
    for tag, names in (("ffn", ("w_ffn_gate", "w_ffn_up", "w_ffn_down")), ("out", ("w_out",)), ("in", ("w_in",))):
        own, recv = _send_wait("scatter_" + tag + "_wait", *sent[tag][0:4], "scatter", token)
        for n, o_, r_ in zip(names, own, recv):
            res[n] = _update_matrix(r_, o_, me, *[f[n] for f in fams], "update_" + n)
            token = res[n][1]
    (sown, vown), (srecv, vrecv) = _send_wait("scatter_small_wait", *sent["small"][0:4], "scatter", token)
    small = _update_small(me, srecv, vrecv, sown, vown, *[[two_d(f[n]) for n in _SMALL] for f in fams])
    for i, n in enumerate(_SMALL):
        res[n] = [fam[i].reshape(w[n].shape) for fam in small[0:4]]
    for n in _TRANSPOSED:
        res[n] = [t.transpose(0, 2, 1) for t in res[n]]
    outs = [small[4][0, 0], grad_x]
    for k in range(4):
        outs += [res[n][k] for n in _WEIGHTS]
    return tuple(outs)
```

```python
import functools

import jax
import jax.numpy as jnp
from jax import lax
from jax.experimental import pallas as pl
from jax.experimental.pallas import tpu as pltpu

F32 = jnp.float32
BF16 = jnp.bfloat16

D = 1024
N_META = 16
C_CONV = 512
CONV_W = 31
GLA_H = 4
GLA_DK = 64
GLA_DV = 128
GLA_K = GLA_H * GLA_DK
GLA_V = GLA_H * GLA_DV
RANK = 16
RANK_P = 128
TAU = 16.0
CHUNK = 64
LEAD = CHUNK
ZROWS = LEAD - N_META
D_IN = 2 * C_CONV + 2 * GLA_K + 2 * GLA_V + RANK
D_INP = D_IN - RANK + RANK_P
D_FF = 2816
FF_CHUNK = 1408
FF_SPLIT = (0, 1536, D_FF)
RMS_EPS = 1e-6
LN_EPS = 1e-5
N_DEV = 8

ADAM_LR = 0.001
ADAM_B1 = 0.9
ADAM_B2 = 0.999
ADAM_EPS = 1e-08
ADAM_WD = 0.01
ADAM_STEP = 10

VMEM_LIMIT = 60 * 1024 * 1024
ROW_TILE = 1056
FFN_ROW_TILE = 352
DW_ROW_TILE = 1408
MESH = pl.DeviceIdType.MESH

_NN = (((1,), (0,)), ((), ()))
_NT = (((1,), (1,)), ((), ()))
_TN = (((0,), (0,)), ((), ()))


def _dot(a, b, dims=_NN):
    return lax.dot_general(a, b, dims, preferred_element_type=F32)


def _sigmoid(x):
    return 1.0 / (1.0 + jnp.exp(-x))


def _row_tile(rows, target):
    best = None
    for t in range(16, min(rows, target) + 1, 16):
        if rows % t == 0:
            best = t
    assert best is not None, rows
    return best


def _params(sem=None):
    return pltpu.CompilerParams(dimension_semantics=sem, vmem_limit_bytes=VMEM_LIMIT)


def _whole_vmem():
    return pl.BlockSpec(memory_space=pltpu.VMEM)


def _rows(tm, width):
    return pl.BlockSpec((tm, width), lambda i: (i, 0))


def _fixed(shape):
    return pl.BlockSpec(shape, lambda *_: (0,) * len(shape))


def _fwd_inproj(h0, g1, w_in):
    rows = h0.shape[0]
    tm = _row_tile(rows, ROW_TILE)

    def body(h_ref, g_ref, w_ref, uc_ref, qk_ref, vg_ref, lr_ref, n1_ref):
        h = h_ref[...]
        r = lax.rsqrt(jnp.mean(h * h, axis=-1, keepdims=True) + RMS_EPS)
        n = (h * r * g_ref[...]).astype(BF16)
        n1_ref[...] = n
        uc_ref[...] = _dot(n, w_ref[:, 0:1024]).astype(BF16)
        qk_ref[...] = _dot(n, w_ref[:, 1024:1536]).astype(BF16)
        vg_ref[...] = _dot(n, w_ref[:, 1536:2560]).astype(BF16)
        lr_ref[...] = _dot(n, w_ref[:, 2560:2688]).astype(BF16)

    return pl.pallas_call(
        body, name="fwd_inproj", grid=(rows // tm,),
        in_specs=[_rows(tm, D), _fixed((1, D)), _whole_vmem()],
        out_specs=[_rows(tm, 1024), _rows(tm, 512), _rows(tm, 1024), _rows(tm, RANK_P), _rows(tm, D)],
        out_shape=[jax.ShapeDtypeStruct((rows, 1024), BF16), jax.ShapeDtypeStruct((rows, 512), BF16),
                   jax.ShapeDtypeStruct((rows, 1024), BF16), jax.ShapeDtypeStruct((rows, RANK_P), BF16),
                   jax.ShapeDtypeStruct((rows, D), BF16)],
        compiler_params=_params(("parallel",)),
    )(h0, g1, w_in)


def _mid_rows(yc, yg, h0, tgt, w_out, wg, wu, wd, g2, g3, token, rows_per_example):
    rows = h0.shape[0]
    tm = _row_tile(rows, FFN_ROW_TILE)
    ff_blocks = [slice(lo, hi) for lo, hi in zip(FF_SPLIT[:-1], FF_SPLIT[1:])]

    def body(yc_ref, yg_ref, h0_ref, t_ref, wo_ref, wg_ref, wu_ref, wd_ref, g2_ref, g3_ref, token_ref,
             n2_ref, f_ref, da_ref, db_ref, dh2_ref, dh1_ref, dh1b_ref, dyc_ref, dyg_ref, part_ref):
        i = pl.program_id(0)
        h1 = h0_ref[...] + _dot(yc_ref[...], wo_ref[0:C_CONV, :]) + _dot(yg_ref[...], wo_ref[C_CONV:D, :])
        r2 = lax.rsqrt(jnp.mean(h1 * h1, axis=-1, keepdims=True) + RMS_EPS)
        xh2 = h1 * r2
        n2 = (xh2 * g2_ref[...]).astype(BF16)
        n2_ref[...] = n2
        y2 = jnp.zeros((tm, D), F32)
        for cs in ff_blocks:
            a = _dot(n2, wg_ref[cs, :], _NT)
            b = _dot(n2, wu_ref[cs, :], _NT)
            f = (a * _sigmoid(a) * b).astype(BF16)
            f_ref[:, cs] = f
            da_ref[:, cs] = a.astype(BF16)
            db_ref[:, cs] = b.astype(BF16)
            y2 = y2 + _dot(f, wd_ref[cs, :])
        h2 = h1 + y2
        r3 = lax.rsqrt(jnp.mean(h2 * h2, axis=-1, keepdims=True) + RMS_EPS)
        xh3 = h2 * r3
        g3 = g3_ref[...]
        pos = (i * tm + lax.broadcasted_iota(jnp.int32, (tm, 1), 0)) % rows_per_example
        valid = pos >= LEAD
        err = jnp.where(valid, xh3 * g3 - t_ref[...], 0.0)
        loss = 0.5 / D * jnp.sum(jnp.sum(err * err, axis=-1, keepdims=True), axis=0, keepdims=True)
        dy = err * (1.0 / D)
        dg3 = jnp.sum(dy * xh3, axis=0, keepdims=True)
        dxh = dy * g3
        dh2 = r3 * (dxh - xh3 * jnp.mean(dxh * xh3, axis=-1, keepdims=True))
        dh2b = dh2.astype(BF16)
        dh2_ref[...] = dh2b
        dn2 = jnp.zeros((tm, D), F32)
        for cs in ff_blocks:
            df = _dot(dh2b, wd_ref[cs, :], _NT)
            a = da_ref[:, cs].astype(F32)
            b = db_ref[:, cs].astype(F32)
            sg = _sigmoid(a)
            da = (df * b * sg * (1.0 + a * (1.0 - sg))).astype(BF16)
            db = (df * a * sg).astype(BF16)
            da_ref[:, cs] = da
            db_ref[:, cs] = db
            dn2 = dn2 + _dot(da, wg_ref[cs, :]) + _dot(db, wu_ref[cs, :])
        dg2 = jnp.sum(dn2 * xh2, axis=0, keepdims=True)
        dxh2 = dn2 * g2_ref[...]
        dh1 = dh2 + r2 * (dxh2 - xh2 * jnp.mean(dxh2 * xh2, axis=-1, keepdims=True))
        dh1_ref[...] = dh1
        dh1b = dh1.astype(BF16)
        dh1b_ref[...] = dh1b
        dyc_ref[...] = _dot(dh1b, wo_ref[0:C_CONV, :], _NT)
        dyg_ref[...] = _dot(dh1b, wo_ref[C_CONV:D, :], _NT)

        @pl.when(i == 0)
        def _():
            part_ref[...] = jnp.zeros_like(part_ref)

        part_ref[0:1, :] += dg3
        part_ref[1:2, :] += dg2
        part_ref[2:3, :] += jnp.broadcast_to(loss, (1, D))

    return pl.pallas_call(
        body, name="mid_rows", grid=(rows // tm,),
        in_specs=[_rows(tm, C_CONV), _rows(tm, GLA_V), _rows(tm, D), _rows(tm, D), _whole_vmem(), _whole_vmem(),
                  _whole_vmem(), _whole_vmem(), _fixed((1, D)), _fixed((1, D)), _fixed((8, 128))],
        out_specs=[_rows(tm, D), _rows(tm, D_FF), _rows(tm, D_FF), _rows(tm, D_FF), _rows(tm, D), _rows(tm, D),
                   _rows(tm, D), _rows(tm, C_CONV), _rows(tm, GLA_V), _fixed((8, D))],
        out_shape=[jax.ShapeDtypeStruct((rows, D), BF16)] + [jax.ShapeDtypeStruct((rows, D_FF), BF16)] * 3
        + [jax.ShapeDtypeStruct((rows, D), BF16), jax.ShapeDtypeStruct((rows, D), F32),
           jax.ShapeDtypeStruct((rows, D), BF16), jax.ShapeDtypeStruct((rows, C_CONV), F32),
           jax.ShapeDtypeStruct((rows, GLA_V), F32), jax.ShapeDtypeStruct((8, D), F32)],
        compiler_params=_params(("arbitrary",)),
    )(yc, yg, h0, tgt, w_out, wg, wu, wd, g2, g3, token)


def _bwd_inproj(duc, dqk, dvg, dlr, dh1, h0, w_in, g1, token, rows_per_example):
    rows = h0.shape[0]
    n_ex = rows // rows_per_example
    tm = _row_tile(rows_per_example, ROW_TILE)
    tiles_per_example = rows_per_example // tm
    n_steps = rows // tm

    def body(duc_ref, dqk_ref, dvg_ref, dlr_ref, dh1_ref, h_ref, w_ref, g_ref, token_ref, gx_ref, part_ref, dmeta_ref,
             buf_ref, sems):
        dn = (_dot(duc_ref[...], w_ref[:, 0:1024], _NT) + _dot(dqk_ref[...], w_ref[:, 1024:1536], _NT)
              + _dot(dvg_ref[...], w_ref[:, 1536:2560], _NT) + _dot(dlr_ref[...], w_ref[:, 2560:2688], _NT))
        h = h_ref[...]
        r = lax.rsqrt(jnp.mean(h * h, axis=-1, keepdims=True) + RMS_EPS)
        xh = h * r
        dg = jnp.sum(dn * xh, axis=0, keepdims=True)
        dxh = dn * g_ref[...]
        dh0 = dh1_ref[...] + r * (dxh - xh * jnp.mean(dxh * xh, axis=-1, keepdims=True))
        i = pl.program_id(0)

        def copies(step):
            slot, b, j = step % 2, step // tiles_per_example, step % tiles_per_example
            out = [(j == 0, pltpu.make_async_copy(buf_ref.at[slot, pl.ds(LEAD, tm - LEAD)],
                                                   gx_ref.at[b, pl.ds(0, tm - LEAD)], sems.at[slot]))]
            if tiles_per_example > 1:
                out.append((j != 0, pltpu.make_async_copy(
                    buf_ref.at[slot], gx_ref.at[b, pl.ds(pl.multiple_of(jnp.maximum(j * tm - LEAD, 0), 8), tm)],
                    sems.at[slot])))
            return out

        def each(step, act):
            for cond, cp in copies(step):
                pl.when(cond)(functools.partial(act, cp))

        @pl.when(i >= 2)
        def _():
            each(i - 2, lambda cp: cp.wait())

        buf_ref[i % 2] = dh0
        each(i, lambda cp: cp.start())

        @pl.when(i == n_steps - 1)
        def _():
            each(i, lambda cp: cp.wait())
            if n_steps > 1:
                each(i - 1, lambda cp: cp.wait())

        @pl.when(i == 0)
        def _():
            part_ref[...] = jnp.zeros_like(part_ref)
            dmeta_ref[...] = jnp.zeros_like(dmeta_ref)

        part_ref[0:1, :] += dg

        @pl.when(i % tiles_per_example == 0)
        def _():
            dmeta_ref[...] += dh0[ZROWS:LEAD, :]

    return pl.pallas_call(
        body, name="bwd_inproj", grid=(n_steps,),
        in_specs=[_rows(tm, 1024), _rows(tm, 512), _rows(tm, 1024), _rows(tm, RANK_P), _rows(tm, D), _rows(tm, D),
                  _whole_vmem(), _fixed((1, D)), _fixed((8, 128))],
        out_specs=[_any(), _fixed((8, D)), _fixed((N_META, D))],
        out_shape=[jax.ShapeDtypeStruct((n_ex, rows_per_example - LEAD, D), F32), jax.ShapeDtypeStruct((8, D), F32),
                   jax.ShapeDtypeStruct((N_META, D), F32)],
        scratch_shapes=[pltpu.VMEM((2, tm, D), F32), pltpu.SemaphoreType.DMA((2,))],
        compiler_params=_params(("arbitrary",)),
    )(duc, dqk, dvg, dlr, dh1, h0, w_in, g1, token)


def _dw_blocked(a, bs, width, name):
    rows, m = a.shape
    ws = [b.shape[1] for b in bs]
    assert sum(ws) >= N_DEV * width
    tk = _row_tile(rows, DW_ROW_TILE)
    nk = rows // tk

    def body(a_ref, *refs):
        b_refs, o_ref, acc_ref = refs[:len(bs)], refs[len(bs)], refs[len(bs) + 1]
        k = pl.program_id(0)

        @pl.when(k == 0)
        def _():
            acc_ref[...] = jnp.zeros_like(acc_ref)

        at = a_ref[...].T
        off = 0
        for b_ref, w in zip(b_refs, ws):
            acc_ref[:, off:off + w] += _dot(at, b_ref[...])
            off += w

        @pl.when(k == nk - 1)
        def _():
            for d in range(N_DEV):
                o_ref[d] = acc_ref[:, d * width:(d + 1) * width].astype(BF16)

    return pl.pallas_call(
        body, name=name, grid=(nk,),
        in_specs=[_rows(tk, m)] + [_rows(tk, w) for w in ws],
        out_specs=_fixed((N_DEV, m, width)),
        out_shape=jax.ShapeDtypeStruct((N_DEV, m, width), BF16),
        scratch_shapes=[pltpu.VMEM((m, sum(ws)), F32)],
        compiler_params=_params(("arbitrary",)),
    )(a, *bs)


def _matmul_tn(a, b, name):
    rows, m = a.shape
    n = b.shape[1]
    tk = _row_tile(rows, DW_ROW_TILE)
    tn = n if n <= 1024 else FF_CHUNK
    tm_ = m if m <= 1024 else FF_CHUNK
    assert n % tn == 0 and m % tm_ == 0
    nk = rows // tk

    def body(a_ref, b_ref, o_ref, acc_ref):
        k = pl.program_id(2)

        @pl.when(k == 0)
        def _():
            acc_ref[...] = jnp.zeros_like(acc_ref)

        acc_ref[...] += _dot(a_ref[...], b_ref[...], _TN)

        @pl.when(k == nk - 1)
        def _():
            o_ref[...] = acc_ref[...].astype(BF16)

    return pl.pallas_call(
        body, name=name, grid=(m // tm_, n // tn, nk),
        in_specs=[pl.BlockSpec((tk, tm_), lambda i, j, k: (k, i)), pl.BlockSpec((tk, tn), lambda i, j, k: (k, j))],
        out_specs=pl.BlockSpec((tm_, tn), lambda i, j, k: (i, j)),
        out_shape=jax.ShapeDtypeStruct((m, n), BF16),
        scratch_shapes=[pltpu.VMEM((tm_, tn), F32)],
        compiler_params=_params(("parallel", "parallel", "arbitrary")),
    )(a, b)


HALO = 32
LN_ROWS = 3 * CHUNK
LANES = 128


def _shifted(win, offsets):
    for r in range(8):
        js = [j for j, k in enumerate(offsets) if k % 8 == r]
        if js:
            rolled = win if r == 0 else pltpu.roll(win, CHUNK + HALO - r, 0)
            for j in js:
                yield j, rolled[offsets[j] - r:offsets[j] - r + CHUNK]


def _glu_into(uc_ref, vs_ref, n_chunk):
    vs_ref[0:CHUNK, :] = jnp.zeros((CHUNK, C_CONV), F32)

    def glu(i, carry):
        base = pl.multiple_of(i * CHUNK, CHUNK)
        val = uc_ref[pl.ds(base, CHUNK), 0:C_CONV].astype(F32)
        gate = uc_ref[pl.ds(base, CHUNK), C_CONV:2 * C_CONV].astype(F32)
        vs_ref[pl.ds(base + CHUNK, CHUNK), :] = val * _sigmoid(gate)
        return carry

    lax.fori_loop(0, n_chunk, glu, 0, unroll=3)


def _fwd_conv(uc, conv_w, conv_b, ln_g, ln_b, token, n_ex):
    rows = uc.shape[0]
    lp = rows // n_ex
    n_chunk = lp // CHUNK

    def body(uc_ref, w_ref, b_ref, lg_ref, lb_ref, token_ref, ypre_ref, yc_ref, vs_ref):
        _glu_into(uc_ref, vs_ref, n_chunk)

        def conv(i, carry):
            base = pl.multiple_of(i * CHUNK, CHUNK)
            for lb in range(C_CONV // LANES):
                ls = slice(lb * LANES, (lb + 1) * LANES)
                win = vs_ref[pl.ds(base + CHUNK - HALO, CHUNK + HALO), ls]
                acc = jnp.broadcast_to(b_ref[:, ls], (CHUNK, LANES))
                for j, rows_j in _shifted(win, [HALO - (CONV_W - 1) + j for j in range(CONV_W)]):
                    acc = acc + w_ref[j:j + 1, ls] * rows_j
                ypre_ref[pl.ds(base, CHUNK), ls] = acc
            y = ypre_ref[pl.ds(base, CHUNK), :]
            mu = jnp.mean(y, axis=-1, keepdims=True)
            yc_ = y - mu
            rstd = lax.rsqrt(jnp.mean(yc_ * yc_, axis=-1, keepdims=True) + LN_EPS)
            s = yc_ * rstd * lg_ref[...] + lb_ref[...]
            yc_ref[pl.ds(base, CHUNK), :] = (s * _sigmoid(s)).astype(BF16)
            return carry

        lax.fori_loop(0, n_chunk, conv, 0, unroll=3)

    ex = lambda w: pl.BlockSpec((lp, w), lambda b: (b, 0))
    return pl.pallas_call(
        body, name="fwd_conv", grid=(n_ex,),
        in_specs=[ex(2 * C_CONV), _fixed((32, C_CONV)), _fixed((1, C_CONV)), _fixed((1, C_CONV)), _fixed((1, C_CONV)),
                  _fixed((8, 128))],
        out_specs=[ex(C_CONV), ex(C_CONV)],
        out_shape=[jax.ShapeDtypeStruct((rows, C_CONV), F32), jax.ShapeDtypeStruct((rows, C_CONV), BF16)],
        scratch_shapes=[pltpu.VMEM((lp + CHUNK, C_CONV), F32)],
        compiler_params=_params(("parallel",)),
    )(uc, conv_w, conv_b, ln_g, ln_b, token)


def _bwd_conv(uc, ypre, dyc, conv_w, ln_g, ln_b, token, n_ex):
    rows = uc.shape[0]
    lp = rows // n_ex
    n_chunk = lp // CHUNK

    def body(uc_ref, ypre_ref, dyc_ref, w_ref, lg_ref, lb_ref, token_ref, duc_ref, dw_ref, dvec_ref, vs_ref, dys_ref,
             dwacc_ref):
        _glu_into(uc_ref, vs_ref, n_chunk)
        dys_ref[pl.ds(lp, CHUNK), :] = jnp.zeros((CHUNK, C_CONV), F32)
        dwacc_ref[...] = jnp.zeros_like(dwacc_ref)

        def ln_bwd(i, carry):
            dcb, dlg, dlb = carry
            base = pl.multiple_of(i * LN_ROWS, LN_ROWS)
            y = ypre_ref[pl.ds(base, LN_ROWS), :]
            mu = jnp.mean(y, axis=-1, keepdims=True)
            yc_ = y - mu
            rstd = lax.rsqrt(jnp.mean(yc_ * yc_, axis=-1, keepdims=True) + LN_EPS)
            xh = yc_ * rstd
            s = xh * lg_ref[...] + lb_ref[...]
            sg = _sigmoid(s)
            ds = dyc_ref[pl.ds(base, LN_ROWS), :] * (sg * (1.0 + s * (1.0 - sg)))
            dxh = ds * lg_ref[...]
            dy = rstd * (dxh - jnp.mean(dxh, axis=-1, keepdims=True) - xh * jnp.mean(dxh * xh, axis=-1, keepdims=True))
            dys_ref[pl.ds(base, LN_ROWS), :] = dy
            return (dcb + jnp.sum(dy, axis=0, keepdims=True), dlg + jnp.sum(ds * xh, axis=0, keepdims=True),
                    dlb + jnp.sum(ds, axis=0, keepdims=True))

        zero = jnp.zeros((1, C_CONV), F32)
        dcb, dlg, dlb = lax.fori_loop(0, lp // LN_ROWS, ln_bwd, (zero, zero, zero))

        @pl.when(pl.program_id(0) == 0)
        def _():
            dvec_ref[...] = jnp.zeros_like(dvec_ref)
            dw_ref[...] = jnp.zeros_like(dw_ref)

        dvec_ref[0:1, :] += dcb
        dvec_ref[1:2, :] += dlg
        dvec_ref[2:3, :] += dlb

        def taps(i, carry):
            base = pl.multiple_of(i * CHUNK, CHUNK)
            for lb in range(C_CONV // LANES):
                ls = slice(lb * LANES, (lb + 1) * LANES)
                dwin = dys_ref[pl.ds(base, CHUNK + HALO), ls]
                vwin = vs_ref[pl.ds(base + CHUNK - HALO, CHUNK + HALO), ls]
                dy = dwin[0:CHUNK]
                acc = jnp.zeros((CHUNK, LANES), F32)
                for j, rows_j in _shifted(dwin, [CONV_W - 1 - j for j in range(CONV_W)]):
                    acc = acc + w_ref[j:j + 1, ls] * rows_j
                for j, rows_j in _shifted(vwin, [HALO - (CONV_W - 1) + j for j in range(CONV_W)]):
                    dwacc_ref[8 * j:8 * j + 8, ls] += jnp.sum((dy * rows_j).reshape(CHUNK // 8, 8, LANES), axis=0)
                val = uc_ref[pl.ds(base, CHUNK), ls].astype(F32)
                gate = uc_ref[pl.ds(base, CHUNK), C_CONV + lb * LANES:C_CONV + (lb + 1) * LANES].astype(F32)
                sg = _sigmoid(gate)
                duc_ref[pl.ds(base, CHUNK), ls] = (acc * sg).astype(BF16)
                duc_ref[pl.ds(base, CHUNK), C_CONV + lb * LANES:C_CONV + (lb + 1) * LANES] = (
                    acc * val * sg * (1.0 - sg)).astype(BF16)
            return carry

        lax.fori_loop(0, n_chunk, taps, 0, unroll=3)
        for j in range(CONV_W):
            dw_ref[j:j + 1, :] += jnp.sum(dwacc_ref[8 * j:8 * j + 8, :], axis=0, keepdims=True)

    ex = lambda w: pl.BlockSpec((lp, w), lambda b: (b, 0))
    return pl.pallas_call(
        body, name="bwd_conv", grid=(n_ex,),
        in_specs=[ex(2 * C_CONV), ex(C_CONV), ex(C_CONV), _fixed((32, C_CONV)), _fixed((1, C_CONV)), _fixed((1, C_CONV)),
                  _fixed((8, 128))],
        out_specs=[ex(2 * C_CONV), _fixed((32, C_CONV)), _fixed((8, C_CONV))],
        out_shape=[jax.ShapeDtypeStruct((rows, 2 * C_CONV), BF16), jax.ShapeDtypeStruct((32, C_CONV), F32),
                   jax.ShapeDtypeStruct((8, C_CONV), F32)],
        scratch_shapes=[pltpu.VMEM((lp + CHUNK, C_CONV), F32), pltpu.VMEM((lp + CHUNK, C_CONV), F32),
                        pltpu.VMEM((8 * 32, C_CONV), F32)],
        compiler_params=_params(("arbitrary",)),
    )(uc, ypre, dyc, conv_w, ln_g, ln_b, token)


def _seg_chunks(n_chunk):
    return max(c for c in (11, 3, 1) if n_chunk % c == 0)


def _block_mask(shape, row_block, lane_block):
    return (lax.broadcasted_iota(jnp.int32, shape, 0) // row_block) == (lax.broadcasted_iota(jnp.int32, shape, 1) // lane_block)


def _per_head_rows(x, mask):
    return jnp.where(mask, jnp.concatenate([x] * GLA_H, axis=0), 0)


def _fold_heads(full, lane_block):
    lane = lax.broadcasted_iota(jnp.int32, (1, full.shape[1]), 1) // lane_block
    out = jnp.where(lane == 0, full[0:CHUNK], 0.0)
    for h in range(1, GLA_H):
        out = out + jnp.where(lane == h, full[h * CHUNK:(h + 1) * CHUNK], 0.0)
    return out


PAIRS = GLA_H // 2


def _expand_state(blocks):
    lane = lax.broadcasted_iota(jnp.int32, (GLA_DV, 128), 1) // GLA_DK
    zero = jnp.zeros_like(blocks[0])
    rows = []
    for h in range(GLA_H):
        p, hh = divmod(h, 2)
        mine = jnp.where(lane == hh, blocks[p], 0)
        rows.append(jnp.concatenate([mine if q == p else zero for q in range(PAIRS)], axis=1))
    return jnp.concatenate(rows, axis=0)


def _compact_state(full, p):
    lane = lax.broadcasted_iota(jnp.int32, (GLA_DV, 128), 1) // GLA_DK
    ls = slice(128 * p, 128 * (p + 1))
    return jnp.where(lane == 0, full[2 * p * GLA_DV:(2 * p + 1) * GLA_DV, ls], full[(2 * p + 1) * GLA_DV:(2 * p + 2) * GLA_DV, ls])


def _causal_heads():
    return (lax.broadcasted_iota(jnp.int32, (CHUNK, GLA_H * CHUNK), 1) % CHUNK) <= lax.broadcasted_iota(
        jnp.int32, (CHUNK, GLA_H * CHUNK), 0)


def _cumsum_rows(x):
    row = lax.broadcasted_iota(jnp.int32, x.shape, 0)
    s = 1
    while s < CHUNK:
        x = x + jnp.where(row >= s, pltpu.roll(x, s, 0), 0.0)
        s *= 2
    return x


def _rev_cumsum_rows(x):
    row = lax.broadcasted_iota(jnp.int32, x.shape, 0)
    s = 1
    while s < CHUNK:
        x = x + jnp.where(row < CHUNK - s, pltpu.roll(x, CHUNK - s, 0), 0.0)
        s *= 2
    return x


def _gate_terms(lr_ref, w2_ref, gb_ref, rs, first_pos):
    z = _dot(lr_ref[rs, :].astype(BF16), w2_ref[...]) + gb_ref[...]
    la = (jnp.minimum(z, 0.0) - jnp.log(1.0 + jnp.exp(-jnp.abs(z)))) * (1.0 / TAU)
    pos = first_pos + lax.broadcasted_iota(jnp.int32, (CHUNK, 1), 0)
    live = pos >= ZROWS
    la = jnp.where(live, la, 0.0)
    return z, live, _cumsum_rows(la)


def _fwd_gla(qk, vg, lr, w2p, gb, ng, token, n_ex):
    rows = qk.shape[0]
    lp = rows // n_ex
    n_chunk = lp // CHUNK
    sc = _seg_chunks(n_chunk)
    n_seg = n_chunk // sc
    seg = sc * CHUNK

    def body(qk_ref, vg_ref, lr_ref, w2_ref, gb_ref, ng_ref, token_ref, yg_ref, o_ref, st_ref, state_ref):
        sidx = pl.program_id(1)

        @pl.when(sidx == 0)
        def _():
            state_ref[...] = jnp.zeros_like(state_ref)

        causal = _causal_heads()
        k_mask = _block_mask((GLA_H * CHUNK, GLA_K), CHUNK, GLA_DK)
        v_mask = _block_mask((GLA_H * CHUNK, GLA_V), CHUNK, GLA_DV)

        def chunk(ci, carry):
            base = pl.multiple_of(ci * CHUNK, CHUNK)
            rs = pl.ds(base, CHUNK)
            _, _, bcum = _gate_terms(lr_ref, w2_ref, gb_ref, rs, (sidx * sc + ci) * CHUNK)
            bl = bcum[CHUNK - 1:CHUNK, :]
            q = qk_ref[rs, 0:GLA_K].astype(F32)
            k = qk_ref[rs, GLA_K:2 * GLA_K].astype(F32)
            qt = (q * (GLA_DK ** -0.5) * jnp.exp(bcum)).astype(BF16)
            kt = (k * jnp.exp(-bcum)).astype(BF16)
            kh = (k * jnp.exp(bl - bcum)).astype(BF16)
            vb = vg_ref[rs, 0:GLA_V].astype(BF16)
            state = [state_ref[p] for p in range(PAIRS)]
            for p in range(PAIRS):
                st_ref[ci, p] = state[p]
            a = jnp.where(causal, _dot(qt, _per_head_rows(kt, k_mask), _NT), 0.0)
            o = _dot(a.astype(BF16), _per_head_rows(vb, v_mask)) + _dot(
                qt, _expand_state([s.astype(BF16) for s in state]), _NT)
            o_ref[rs, :] = o
            for h in range(GLA_H):
                hs = slice(h * GLA_DV, (h + 1) * GLA_DV)
                oh = o[:, hs]
                ro = lax.rsqrt(jnp.mean(oh * oh, axis=-1, keepdims=True) + RMS_EPS)
                g = vg_ref[rs, GLA_V + h * GLA_DV:GLA_V + (h + 1) * GLA_DV].astype(F32)
                yg_ref[rs, hs] = (oh * ro * ng_ref[...] * g * _sigmoid(g)).astype(BF16)
            kv = _dot(vb, kh, _TN)
            decay = jnp.exp(bl)
            for p in range(PAIRS):
                state_ref[p] = state[p] * decay[:, 128 * p:128 * (p + 1)] + _compact_state(kv, p)
            return carry

        lax.fori_loop(0, sc, chunk, 0, unroll=True)

    sg = lambda w: pl.BlockSpec((seg, w), lambda b, s: (b * n_seg + s, 0))
    return pl.pallas_call(
        body, name="fwd_gla", grid=(n_ex, n_seg),
        in_specs=[sg(2 * GLA_K), sg(2 * GLA_V), sg(RANK_P), _fixed((RANK_P, GLA_K)), _fixed((1, GLA_K)), _fixed((1, GLA_DV)),
                  _fixed((8, 128))],
        out_specs=[sg(GLA_V), sg(GLA_V), pl.BlockSpec((sc, PAIRS, GLA_DV, 128), lambda b, s: (b * n_seg + s, 0, 0, 0))],
        out_shape=[jax.ShapeDtypeStruct((rows, GLA_V), BF16), jax.ShapeDtypeStruct((rows, GLA_V), F32),
                   jax.ShapeDtypeStruct((n_ex * n_chunk, PAIRS, GLA_DV, 128), F32)],
        scratch_shapes=[pltpu.VMEM((PAIRS, GLA_DV, 128), F32)],
        compiler_params=_params(("parallel", "arbitrary")),
    )(qk, vg, lr, w2p, gb, ng, token)


def _bwd_gla(qk, vg, lr, o, st, dyg, w2p, gb, ng, yc, yg, dh1b, token, n_ex):
    rows = qk.shape[0]
    lp = rows // n_ex
    n_chunk = lp // CHUNK
    sc = _seg_chunks(n_chunk)
    n_seg = n_chunk // sc
    seg = sc * CHUNK

    def body(qk_ref, vg_ref, lr_ref, o_ref, st_ref, dyg_ref, w2_ref, gb_ref, ng_ref, yc_ref, yg_ref, dh1_ref, token_ref,
             dqk_ref, dvg_ref, dlr_ref, dw2_ref, dvec_ref, dwo_ref, gt_ref, dz_ref, dwo_acc):
        step = pl.program_id(1)
        sidx = n_seg - 1 - step
        first = (step == 0) & (pl.program_id(0) == 0)

        @pl.when(step == 0)
        def _():
            gt_ref[...] = jnp.zeros_like(gt_ref)

        @pl.when(first)
        def _():
            dw2_ref[...] = jnp.zeros_like(dw2_ref)
            dvec_ref[...] = jnp.zeros_like(dvec_ref)
            dwo_acc[...] = jnp.zeros_like(dwo_acc)

        d1 = dh1_ref[...]
        dwo_acc[0:C_CONV, :] += _dot(yc_ref[...], d1, _TN)
        dwo_acc[C_CONV:D, :] += _dot(yg_ref[...], d1, _TN)

        @pl.when((step == n_seg - 1) & (pl.program_id(0) == n_ex - 1))
        def _():
            dwo_ref[...] = dwo_acc[...].astype(BF16)

        causal = _causal_heads()
        k_mask = _block_mask((GLA_H * CHUNK, GLA_K), CHUNK, GLA_DK)
        v_mask = _block_mask((GLA_H * CHUNK, GLA_V), CHUNK, GLA_DV)
        last_row = lax.broadcasted_iota(jnp.int32, (CHUNK, 1), 0) == CHUNK - 1
        ng = ng_ref[...]

        def chunk(ii, dng):
            ci = sc - 1 - ii
            base = pl.multiple_of(ci * CHUNK, CHUNK)
            rs = pl.ds(base, CHUNK)
            z, live, bcum = _gate_terms(lr_ref, w2_ref, gb_ref, rs, (sidx * sc + ci) * CHUNK)
            bl = bcum[CHUNK - 1:CHUNK, :]
            ebl = jnp.exp(bl)
            q = qk_ref[rs, 0:GLA_K].astype(F32)
            k = qk_ref[rs, GLA_K:2 * GLA_K].astype(F32)
            eb = jnp.exp(bcum)
            enb = jnp.exp(-bcum)
            ehb = jnp.exp(bl - bcum)
            qt = q * (GLA_DK ** -0.5) * eb
            kt = k * enb
            kh = k * ehb
            qtb = qt.astype(BF16)
            vb = vg_ref[rs, 0:GLA_V].astype(BF16)
            k_rows = _per_head_rows(kt.astype(BF16), k_mask)
            v_rows = _per_head_rows(vb, v_mask)
            gt = [gt_ref[p] for p in range(PAIRS)]
            gtb = _expand_state([g_.astype(BF16) for g_ in gt])
            s_in = [st_ref[ci, p] for p in range(PAIRS)]
            dos = []
            for h in range(GLA_H):
                hs = slice(h * GLA_DV, (h + 1) * GLA_DV)
                gs = slice(GLA_V + h * GLA_DV, GLA_V + (h + 1) * GLA_DV)
                oh = o_ref[rs, hs]
                ro = lax.rsqrt(jnp.mean(oh * oh, axis=-1, keepdims=True) + RMS_EPS)
                on = oh * ro
                g = vg_ref[rs, gs].astype(F32)
                sg = _sigmoid(g)
                dout = dyg_ref[rs, hs]
                dvg_ref[rs, gs] = (dout * on * ng * (sg * (1.0 + g * (1.0 - sg)))).astype(BF16)
                dw = dout * g * sg
                dng = dng + jnp.sum(dw * on, axis=0, keepdims=True)
                don = dw * ng
                dos.append((ro * (don - on * jnp.mean(don * on, axis=-1, keepdims=True))).astype(BF16))
            dob = jnp.concatenate(dos, axis=1)
            a = jnp.where(causal, _dot(qtb, k_rows, _NT), 0.0).astype(BF16)
            da = jnp.where(causal, _dot(dob, v_rows, _NT), 0.0).astype(BF16)
            dv = _fold_heads(_dot(a, dob, _TN), GLA_DV) + _dot(kh.astype(BF16), gtb, _NT)
            dvg_ref[rs, 0:GLA_V] = dv.astype(BF16)
            dkh = _dot(vb, gtb)
            dqt = _dot(da, k_rows) + _dot(dob, _expand_state([s_.astype(BF16) for s_ in s_in]))
            dkt = _fold_heads(_dot(da, qtb, _TN), GLA_DK)
            dbl = jnp.concatenate([jnp.sum(gt[p] * s_in[p], axis=0, keepdims=True) for p in range(PAIRS)], axis=1) * ebl
            dbl = dbl + jnp.sum(dkh * kh, axis=0, keepdims=True)
            dqk_ref[rs, 0:GLA_K] = (dqt * (GLA_DK ** -0.5) * eb).astype(BF16)
            dqk_ref[rs, GLA_K:2 * GLA_K] = (dkt * enb + dkh * ehb).astype(BF16)
            db = dqt * qt - dkt * kt - dkh * kh
            db = jnp.where(last_row, db + dbl, db)
            dla = jnp.where(live, _rev_cumsum_rows(db), 0.0)
            dz_ref[rs, :] = dla * (1.0 / TAU) * (1.0 - _sigmoid(z))
            dstate = _dot(dob, qtb, _TN)
            for p in range(PAIRS):
                gt_ref[p] = _compact_state(dstate, p) + gt[p] * ebl[:, 128 * p:128 * (p + 1)]
            return dng

        dng = lax.fori_loop(0, sc, chunk, jnp.zeros((1, GLA_DV), F32), unroll=True)
        dz = dz_ref[...]
        dzb = dz.astype(BF16)
        dlr_ref[...] = _dot(dzb, w2_ref[...], _NT).astype(BF16)
        dw2_ref[...] += _dot(lr_ref[...].astype(BF16), dzb, _TN)
        dvec_ref[0:1, :] += jnp.sum(dz, axis=0, keepdims=True)
        dvec_ref[1:2, 0:GLA_DV] += dng

    sg_ = lambda w: pl.BlockSpec((seg, w), lambda b, s: (b * n_seg + n_seg - 1 - s, 0))
    return pl.pallas_call(
        body, name="bwd_gla", grid=(n_ex, n_seg),
        in_specs=[sg_(2 * GLA_K), sg_(2 * GLA_V), sg_(RANK_P), sg_(GLA_V),
                  pl.BlockSpec((sc, PAIRS, GLA_DV, 128), lambda b, s: (b * n_seg + n_seg - 1 - s, 0, 0, 0)), sg_(GLA_V),
                  _fixed((RANK_P, GLA_K)), _fixed((1, GLA_K)), _fixed((1, GLA_DV)), sg_(C_CONV), sg_(GLA_V), sg_(D),
                  _fixed((8, 128))],
        out_specs=[sg_(2 * GLA_K), sg_(2 * GLA_V), sg_(RANK_P), _fixed((RANK_P, GLA_K)), _fixed((8, GLA_K)),
                   _fixed((D, D))],
        out_shape=[jax.ShapeDtypeStruct((rows, 2 * GLA_K), BF16), jax.ShapeDtypeStruct((rows, 2 * GLA_V), BF16),
                   jax.ShapeDtypeStruct((rows, RANK_P), BF16), jax.ShapeDtypeStruct((RANK_P, GLA_K), F32),
                   jax.ShapeDtypeStruct((8, GLA_K), F32), jax.ShapeDtypeStruct((D, D), BF16)],
        scratch_shapes=[pltpu.VMEM((PAIRS, GLA_DV, 128), F32), pltpu.VMEM((seg, GLA_K), F32), pltpu.VMEM((D, D), F32)],
        compiler_params=_params(("arbitrary", "arbitrary")),
    )(qk, vg, lr, o, st, dyg, w2p, gb, ng, yc, yg, dh1b, token)


def _pad_rows(x, tgt):
    return jnp.pad(x, ((0, 0), (LEAD, 0), (0, 0))), jnp.pad(tgt, ((0, 0), (LEAD, 0), (0, 0)))


def _local_step(h0, tgt_p, p, pass_on, late_weights, send_early):
    n_ex, lp, _ = h0.shape
    rows = n_ex * lp
    meta = jnp.broadcast_to(p["meta"][None], (n_ex, N_META, D))
    h0 = lax.dynamic_update_slice(h0, meta, (0, ZROWS, 0)).reshape(rows, D)
    tgt_p = tgt_p.reshape(rows, D)

    uc, qk, vg, lr, n1 = _fwd_inproj(h0, p["g1"], p["w_in"])
    ypre, yc = _fwd_conv(uc, p["conv_w"], p["conv_b"], p["ln_g"], p["ln_b"], p["token"], n_ex)
    token = pass_on(yc)
    yg, o, st = _fwd_gla(qk, vg, lr, p["w2"], p["gb"], p["ng"], token, n_ex)
    w_out, wg, wu, wd = late_weights(yg)
    n2, f, da, db, dh2, dh1, dh1b, dyc, dyg, part = _mid_rows(
        yc, yg, h0, tgt_p, w_out, wg, wu, wd, p["g2"], p["g3"], token, lp)
    g = {}
    token = send_early("ffn", [_matmul_tn(a_, b_, name).reshape(N_DEV, FF_S, D) for a_, b_, name in (
        (da, n2, "dw_gate"), (db, n2, "dw_up"), (f, dh2, "dw_down"))])
    dqk, dvg, dlr, g["w2"], g["gla_vec"], dw_out = _bwd_gla(
        qk, vg, lr, o, st, dyg, p["w2"], p["gb"], p["ng"], yc, yg, dh1b, token, n_ex)
    token = send_early("out", [dw_out.reshape(N_DEV, W_OUT_S, D)])
    duc, g["conv_w"], g["conv_vec"] = _bwd_conv(uc, ypre, dyc, p["conv_w"], p["ln_g"], p["ln_b"], token, n_ex)
    token = send_early("in", [_dw_blocked(n1, [duc, dqk, dvg, dlr], W_IN_S, "dw_in")])
    grad_x, g["in_vec"], g["meta"] = _bwd_inproj(duc, dqk, dvg, dlr, dh1, h0, p["w_in"], p["g1"], token, lp)
    g["ffn_vec"] = part
    return grad_x, g


W_IN_S = D_IN // N_DEV
W_OUT_S = D // N_DEV
FF_S = D_FF // N_DEV
CONV_S = C_CONV // N_DEV
GATE_S = GLA_K // N_DEV
SMALL_PACK = 64
CONV_ROW = 16
GATE_ROW = 48
VEC_ROWS = 16
_VEC_ROWS = (("norm_mix_g", D), ("conv_b", C_CONV), ("conv_ln_g", C_CONV), ("conv_ln_b", C_CONV), ("gla_gate_b", GLA_K),
             ("gla_norm_g", GLA_DV), ("norm_ffn_g", D), ("norm_final_g", D))
LOSS_ROW = len(_VEC_ROWS)


def _position():
    return lax.axis_index("x"), lax.axis_index("y"), lax.axis_index("c")


def _any():
    return pl.BlockSpec(memory_space=pl.ANY)


def _stage(mats, meta, conv_w, w2):
    n_t = len(mats) + 1

    def body(*refs):
        ins = refs[0:n_t - 1]
        meta_ref, cw_ref, w2_ref = refs[n_t - 1:n_t + 2]
        lands = refs[n_t + 2:2 * n_t + 2]
        shards = refs[2 * n_t + 2:3 * n_t + 2]
        sems = refs[3 * n_t + 2]
        for s_ref, w_ref in zip(shards, ins):
            s_ref[...] = w_ref[...].astype(BF16)
        sp = shards[n_t - 1]
        sp[...] = jnp.zeros_like(sp)
        sp[0:N_META, :] = meta_ref[...]
        sp[CONV_ROW:CONV_ROW + CONV_W, 0:CONV_S] = cw_ref[...]
        sp[GATE_ROW:GATE_ROW + RANK, 0:GATE_S] = w2_ref[...]
        x, y, c = _position()
        mine = [pltpu.make_async_copy(shards[t], lands[t].at[4 * x + 2 * y + c], sems.at[t]) for t in range(n_t)]
        for cp in mine:
            cp.start()
        for cp in mine:
            cp.wait()

    shard_shapes = [jax.ShapeDtypeStruct(m.shape, BF16) for m in mats] + [jax.ShapeDtypeStruct((SMALL_PACK, 128), F32)]
    res = pl.pallas_call(
        body, name="stage",
        out_shape=[jax.ShapeDtypeStruct((N_DEV,) + s.shape, s.dtype) for s in shard_shapes] + shard_shapes,
        in_specs=[_whole_vmem()] * (n_t + 2), out_specs=[_any()] * n_t + [_whole_vmem()] * n_t,
        scratch_shapes=[pltpu.SemaphoreType.DMA((n_t,))],
        compiler_params=pltpu.CompilerParams(vmem_limit_bytes=VMEM_LIMIT),
    )(*mats, meta, conv_w, w2)
    return res[0:n_t], res[n_t:]


_HBM = pl.BlockSpec(memory_space=pltpu.HBM)
_SEM = pl.BlockSpec(memory_space=pltpu.SEMAPHORE)
_EFFECT = pltpu.SideEffectType.DATAFLOW_SIDE_EFFECTING


_N_ROUTES = {"scatter": 7, "first": 4, "forward": 3}


def _routes(mode):
    x, y, c = _position()
    me = 4 * x + 2 * y + c
    if mode == "scatter":
        out = []
        for k in range(1, N_DEV):
            px = 1 - x if k & 4 else x
            py = 1 - y if k & 2 else y
            pc = 1 - c if k & 1 else c
            out.append(((px, py, pc), 4 * px + 2 * py + pc, me))
        return out
    if mode == "first":
        return [(pos, None, me) for pos in ((x, y, 1 - c), (1 - x, y, c), (x, 1 - y, c), (1 - x, 1 - y, c))]
    assert mode == "forward"
    return [((x, y, 1 - c), 4 * px + 2 * py + c, 4 * px + 2 * py + c) for px, py in ((1 - x, y), (x, 1 - y), (1 - x, 1 - y))]


def _route_copies(mode, n, src_refs, land_refs, send_sems, recv_sems):
    nr = _N_ROUTES[mode]
    for i, (pos, src_blk, dst_blk) in enumerate(_routes(mode)):
        for t in range(n):
            src = land_refs[t] if mode == "forward" else src_refs[t]
            yield pltpu.make_async_remote_copy(
                src_ref=src if src_blk is None else src.at[src_blk], dst_ref=land_refs[t].at[dst_blk],
                send_sem=send_sems.at[nr * t + i], recv_sem=recv_sems.at[nr * t + i], device_id=pos, device_id_type=MESH)


def _in_hbm(a):
    return pltpu.with_memory_space_constraint(a, pltpu.HBM)


def _send_start(name, groups, mode, after):
    sizes = [(len(s), len(l)) for s, l in groups]
    bufs = [b for s, l in groups for b in list(s) + list(l)]
    nb, ng = len(bufs), len(groups)

    def body(*refs):
        sems = refs[nb + 1:nb + 1 + 2 * ng]
        token = refs[2 * nb + 2 * ng + 1]
        off = 0
        for gi, (ns, n) in enumerate(sizes):
            for cp in _route_copies(mode, n, refs[off:off + ns], refs[off + ns:off + ns + n], sems[2 * gi], sems[2 * gi + 1]):
                cp.start()
            off += ns + n
        token[...] = jnp.zeros_like(token)

    res = pl.pallas_call(
        body, name=name,
        out_shape=(*[pltpu.SemaphoreType.DMA((_N_ROUTES[mode] * n,)) for _, n in sizes for _ in range(2)],
                   *[pltpu.HBM(b.shape, b.dtype) for b in bufs], jax.ShapeDtypeStruct((8, 128), F32)),
        in_specs=[_HBM] * nb + [_any()], out_specs=(*[_SEM] * (2 * ng), *[_HBM] * nb, _whole_vmem()),
        input_output_aliases={i: 2 * ng + i for i in range(nb)},
        compiler_params=pltpu.CompilerParams(has_side_effects=_EFFECT),
    )(*[_in_hbm(b) for b in bufs], after)
    handles, off = [], 2 * ng
    for gi, (ns, n) in enumerate(sizes):
        handles.append((res[2 * gi], res[2 * gi + 1], res[off:off + ns], res[off + ns:off + ns + n]))
        off += ns + n
    return handles, res[2 * ng + nb]


def _send_wait(name, send_sems, recv_sems, srcs, lands, mode, after):
    n, ns = len(lands), len(srcs)
    after = after if isinstance(after, tuple) else (after,)

    def body(*refs):
        src_refs, land_refs = refs[0:ns], refs[ns:ns + n]
        send_sems, recv_sems = refs[ns + n:ns + n + 2]
        for cp in _route_copies(mode, n, src_refs, land_refs, send_sems, recv_sems):
            cp.wait_send()
            cp.wait_recv()

    bufs = list(srcs) + list(lands)
    res = pl.pallas_call(
        body, name=name,
        out_shape=tuple(pltpu.HBM(b.shape, b.dtype) for b in bufs),
        in_specs=[_HBM] * len(bufs) + [_SEM, _SEM] + [_any()] * len(after), out_specs=tuple([_HBM] * len(bufs)),
        input_output_aliases={i: i for i in range(len(bufs))},
        compiler_params=pltpu.CompilerParams(has_side_effects=_EFFECT),
    )(*bufs, send_sems, recv_sems, *after)
    return res[0:ns], res[ns:ns + n]


def _unshard_in(a_in, a_small, token):
    def body(a_ref, s_ref, token_ref, w_ref, meta_ref, cw_ref, w2_ref):
        w_ref[:, D_IN:D_INP] = jnp.zeros((D, D_INP - D_IN), BF16)
        w2_ref[...] = jnp.zeros_like(w2_ref)
        for d in range(N_DEV):
            w_ref[:, d * W_IN_S:(d + 1) * W_IN_S] = a_ref[d]
            meta_ref[:, d * 128:(d + 1) * 128] = s_ref[d, 0:N_META, :]
            cw_ref[:, d * CONV_S:(d + 1) * CONV_S] = s_ref[d, CONV_ROW:CONV_ROW + 32, 0:CONV_S]
            w2_ref[0:RANK, d * GATE_S:(d + 1) * GATE_S] = s_ref[d, GATE_ROW:GATE_ROW + RANK, 0:GATE_S].astype(BF16)

    return pl.pallas_call(
        body, name="unshard_in",
        out_shape=[jax.ShapeDtypeStruct((D, D_INP), BF16), jax.ShapeDtypeStruct((N_META, D), F32),
                   jax.ShapeDtypeStruct((32, C_CONV), F32), jax.ShapeDtypeStruct((RANK_P, GLA_K), BF16)],
        compiler_params=pltpu.CompilerParams(vmem_limit_bytes=VMEM_LIMIT),
    )(a_in, a_small, token)


def _pack_small(g):
    def body(meta_ref, cw_ref, w2_ref, in_vec, ffn_vec, conv_vec, gla_vec, sp, vp):
        sp[...] = jnp.zeros_like(sp)
        vp[...] = jnp.zeros_like(vp)
        for d in range(N_DEV):
            sp[d, 0:N_META, :] = meta_ref[:, d * 128:(d + 1) * 128]
            sp[d, CONV_ROW:CONV_ROW + 32, 0:CONV_S] = cw_ref[:, d * CONV_S:(d + 1) * CONV_S]
            sp[d, GATE_ROW:GATE_ROW + RANK, 0:GATE_S] = w2_ref[0:RANK, d * GATE_S:(d + 1) * GATE_S]
            vp[d, 0:1, :] = in_vec[0:1, :]
            vp[d, 1:4, 0:C_CONV] = conv_vec[0:3, :]
            vp[d, 4:5, 0:GLA_K] = gla_vec[0:1, :]
            vp[d, 5:6, 0:GLA_DV] = gla_vec[1:2, 0:GLA_DV]
            vp[d, 6:7, :] = ffn_vec[1:2, :]
            vp[d, 7:8, :] = ffn_vec[0:1, :]
            vp[d, LOSS_ROW:LOSS_ROW + 1, :] = ffn_vec[2:3, :]

    return pl.pallas_call(
        body, name="pack_small",
        out_shape=[jax.ShapeDtypeStruct((N_DEV, SMALL_PACK, 128), F32), jax.ShapeDtypeStruct((N_DEV, VEC_ROWS, D), F32)],
    )(g["meta"], g["conv_w"], g["w2"], g["in_vec"], g["ffn_vec"], g["conv_vec"], g["gla_vec"])


def _adamw(w, g, m, v):
    m = ADAM_B1 * m + (1.0 - ADAM_B1) * g
    v = ADAM_B2 * v + (1.0 - ADAM_B2) * (g * g)
    m_hat = m / (1.0 - ADAM_B1 ** ADAM_STEP)
    v_hat = v / (1.0 - ADAM_B2 ** ADAM_STEP)
    return -ADAM_LR * (m_hat / (jnp.sqrt(v_hat) + ADAM_EPS) + ADAM_WD * w), m, v


def _update_matrix(recv, own, me, w, m, v, name):
    _, r, c = recv.shape
    tr = _row_tile(r, 256)

    def body(me_ref, recv_ref, own_ref, w_ref, m_ref, v_ref, g_ref, d_ref, nm_ref, nv_ref):
        g = jnp.zeros((tr, c), F32)
        for s in range(N_DEV):
            g = g + jnp.where(me_ref[0] == s, own_ref[...], recv_ref[s]).astype(F32)
        g_ref[...] = g
        d_ref[...], nm_ref[...], nv_ref[...] = _adamw(w_ref[...], g, m_ref[...], v_ref[...])

    one = pl.BlockSpec((None, tr, c), lambda i, me_ref: (0, i, 0))
    return pl.pallas_call(
        body, name=name,
        grid_spec=pltpu.PrefetchScalarGridSpec(
            num_scalar_prefetch=1, grid=(r // tr,),
            in_specs=[pl.BlockSpec((N_DEV, tr, c), lambda i, me_ref: (0, i, 0)),
                      pl.BlockSpec((None, tr, c), lambda i, me_ref: (me_ref[0], i, 0)), one, one, one],
            out_specs=[one] * 4),
        out_shape=[jax.ShapeDtypeStruct((1, r, c), F32)] * 4,
        compiler_params=_params(("parallel",)),
    )(me, recv, own, w, m, v)


_SMALL = ("meta_tokens", "conv_w", "gla_w_gate2") + tuple(n for n, _ in _VEC_ROWS)


def _update_small(me, srecv, vrecv, sown, vown, w, m, v):
    n = len(_SMALL)

    def body(*refs):
        me_ref, s_ref, v_ref, so_ref, vo_ref = refs[0:5]
        w_refs, m_refs, v_refs = refs[5:5 + n], refs[5 + n:5 + 2 * n], refs[5 + 2 * n:5 + 3 * n]
        outs = refs[5 + 3 * n:]
        ssum = jnp.zeros((SMALL_PACK, 128), F32)
        vsum = jnp.zeros((VEC_ROWS, D), F32)
        for s in range(N_DEV):
            ssum = ssum + jnp.where(me_ref[0] == s, so_ref[s], s_ref[s])
            vsum = vsum + jnp.where(me_ref[0] == s, vo_ref[s], v_ref[s])
        grads = [ssum[0:N_META, :], ssum[CONV_ROW:CONV_ROW + CONV_W, 0:CONV_S], ssum[GATE_ROW:GATE_ROW + RANK, 0:GATE_S]]
        grads += [vsum[i:i + 1, 0:width] for i, (_, width) in enumerate(_VEC_ROWS)]
        for i, g in enumerate(grads):
            d, nm, nv = _adamw(w_refs[i][...], g, m_refs[i][...], v_refs[i][...])
            outs[i][...] = g
            outs[n + i][...] = d
            outs[2 * n + i][...] = nm
            outs[3 * n + i][...] = nv
        outs[4 * n][...] = vsum[LOSS_ROW:LOSS_ROW + 1, 0:128]

    shapes = [jax.ShapeDtypeStruct(t.shape, F32) for t in w]
    res = pl.pallas_call(
        body, name="update_small", out_shape=shapes * 4 + [jax.ShapeDtypeStruct((1, 128), F32)],
        in_specs=[pl.BlockSpec(memory_space=pltpu.SMEM)] + [_whole_vmem()] * (4 + 3 * n),
    )(me, srecv, vrecv, sown, vown, *w, *m, *v)
    return res[0:n], res[n:2 * n], res[2 * n:3 * n], res[3 * n:4 * n], res[4 * n]


_WEIGHTS = ("meta_tokens", "norm_mix_g", "w_in", "conv_w", "conv_b", "conv_ln_g", "conv_ln_b", "gla_w_gate2", "gla_gate_b",
            "gla_norm_g", "w_out", "norm_ffn_g", "w_ffn_gate", "w_ffn_up", "w_ffn_down", "norm_final_g")
_MATRICES = ("w_in", "w_out", "w_ffn_gate", "w_ffn_up", "w_ffn_down")
_TRANSPOSED = ("w_ffn_gate", "w_ffn_up")


def kernel(x, meta_tokens, norm_mix_g, w_in, conv_w, conv_b, conv_ln_g, conv_ln_b, gla_w_gate2, gla_gate_b, gla_norm_g, w_out, norm_ffn_g, w_ffn_gate, w_ffn_up, w_ffn_down, norm_final_g, loss_target, m_meta_tokens, m_norm_mix_g, m_w_in, m_conv_w, m_conv_b, m_conv_ln_g, m_conv_ln_b, m_gla_w_gate2, m_gla_gate_b, m_gla_norm_g, m_w_out, m_norm_ffn_g, m_w_ffn_gate, m_w_ffn_up, m_w_ffn_down, m_norm_final_g, v_meta_tokens, v_norm_mix_g, v_w_in, v_conv_w, v_conv_b, v_conv_ln_g, v_conv_ln_b, v_gla_w_gate2, v_gla_gate_b, v_gla_norm_g, v_w_out, v_norm_ffn_g, v_w_ffn_gate, v_w_ffn_up, v_w_ffn_down, v_norm_final_g):
    given = dict(locals())
    two_d = lambda a: a.reshape(1, -1) if a.ndim == 1 else a.reshape(a.shape[-2:])
    fams = [{n: given[pre + n] for n in _WEIGHTS} for pre in ("", "m_", "v_")]
    for f in fams:
        for n in _TRANSPOSED:
            f[n] = f[n].transpose(0, 2, 1)
    w = fams[0]

    lands, shards = _stage([two_d(w[n]) for n in _MATRICES], w["meta_tokens"], two_d(w["conv_w"]), two_d(w["gla_w_gate2"]))
    soon, later = (0, 5), (1, 2, 3, 4)
    pick = lambda seq, idx: [seq[i] for i in idx]
    (first, ffn_first), started = _send_start(
        "gather_first_start", [(pick(shards, soon), pick(lands, soon)), (pick(shards, later), pick(lands, later))],
        "first", norm_mix_g)
    h0, tgt_p = _pad_rows(x, loss_target)
    _, arrived = _send_wait("gather_first_wait", *first, "first", (h0, tgt_p, started))
    (forward,), token = _send_start("gather_forward_start", [([], arrived)], "forward", started)
    _, (a_in, a_small) = _send_wait("gather_forward_wait", *forward, "forward", token)
    w_in, meta, conv_taps, w2 = _unshard_in(a_in, a_small, token)
    p = dict(meta=meta, conv_w=conv_taps, w2=w2, w_in=w_in, g1=norm_mix_g, conv_b=conv_b, ln_g=conv_ln_g, ln_b=conv_ln_b,
             gb=gla_gate_b, ng=gla_norm_g, g2=norm_ffn_g, g3=two_d(norm_final_g), token=token)
    passed = {}

    def pass_on(after):
        _, arrived_ffn = _send_wait("gather_ffn_first_wait", *ffn_first, "first", after)
        (passed["sent"],), token = _send_start("gather_ffn_forward_start", [([], arrived_ffn)], "forward", after)
        return token

    def late_weights(after):
        _, (a_out, a_g, a_u, a_d) = _send_wait("gather_ffn_forward_wait", *passed["sent"], "forward", after)
        return a_out.reshape(D, D), a_g.reshape(D_FF, D), a_u.reshape(D_FF, D), a_d.reshape(D_FF, D)

    sent = {}

    def send_early(tag, mats):
        landing = [_in_hbm(lax.empty(m_.shape, m_.dtype)) for m_ in mats]
        (sent[tag],), token = _send_start("scatter_" + tag + "_start", [(mats, landing)], "scatter", norm_mix_g)
        return token

    grad_x, g = _local_step(h0, tgt_p, p, pass_on, late_weights, send_early)

    token = send_early("small", list(_pack_small(g)))
    x_, y_, c_ = _position()
    me = (4 * x_ + 2 * y_ + c_).astype(jnp.int32).reshape(1)
    res = {}
    for tag, names in (("ffn", ("w_ffn_gate", "w_ffn_up", "w_ffn_down")), ("out", ("w_out",)), ("in", ("w_in",))):
        own, recv = _send_wait("scatter_" + tag + "_wait", *sent[tag], "scatter", token)
        for n, o_, r_ in zip(names, own, recv):
            res[n] = _update_matrix(r_, o_, me, *[f[n] for f in fams], "update_" + n)
            token = res[n][1]
    (sown, vown), (srecv, vrecv) = _send_wait("scatter_small_wait", *sent["small"], "scatter", token)
    small = _update_small(me, srecv, vrecv, sown, vown, *[[two_d(f[n]) for n in _SMALL] for f in fams])
    for i, n in enumerate(_SMALL):
        res[n] = [fam[i].reshape(w[n].shape) for fam in small[0:4]]
    for n in _TRANSPOSED:
        res[n] = [t.transpose(0, 2, 1) for t in res[n]]
    outs = [small[4][0, 0], grad_x]
    for k in range(4):
        outs += [res[n][k] for n in _WEIGHTS]
    return tuple(outs)
```

```python
import functools

import jax
import jax.numpy as jnp
from jax import lax
from jax.experimental import pallas as pl
from jax.experimental.pallas import tpu as pltpu

F32 = jnp.float32
BF16 = jnp.bfloat16

D = 1024
N_META = 16
C_CONV = 512
CONV_W = 31
GLA_H = 4
GLA_DK = 64
GLA_DV = 128
GLA_K = GLA_H * GLA_DK
GLA_V = GLA_H * GLA_DV
RANK = 16
RANK_P = 128
TAU = 16.0
CHUNK = 64
LEAD = CHUNK
ZROWS = LEAD - N_META
D_IN = 2 * C_CONV + 2 * GLA_K + 2 * GLA_V + RANK
D_INP = D_IN - RANK + RANK_P
D_FF = 2816
FF_CHUNK = 1408
FF_SPLIT = (0, 1536, D_FF)
RMS_EPS = 1e-6
LN_EPS = 1e-5
N_DEV = 8

ADAM_LR = 0.001
ADAM_B1 = 0.9
ADAM_B2 = 0.999
ADAM_EPS = 1e-08
ADAM_WD = 0.01
ADAM_STEP = 10

VMEM_LIMIT = 60 * 1024 * 1024
ROW_TILE = 1056
FFN_ROW_TILE = 352
DW_ROW_TILE = 1408
MESH = pl.DeviceIdType.MESH

_NN = (((1,), (0,)), ((), ()))
_NT = (((1,), (1,)), ((), ()))
_TN = (((0,), (0,)), ((), ()))


def _dot(a, b, dims=_NN):
    return lax.dot_general(a, b, dims, preferred_element_type=F32)


def _sigmoid(x):
    return 1.0 / (1.0 + jnp.exp(-x))


def _row_tile(rows, target):
    best = None
    for t in range(16, min(rows, target) + 1, 16):
        if rows % t == 0:
            best = t
    assert best is not None, rows
    return best


def _params(sem=None):
    return pltpu.CompilerParams(dimension_semantics=sem, vmem_limit_bytes=VMEM_LIMIT)


def _whole_vmem():
    return pl.BlockSpec(memory_space=pltpu.VMEM)


def _rows(tm, width):
    return pl.BlockSpec((tm, width), lambda i: (i, 0))


def _fixed(shape):
    return pl.BlockSpec(shape, lambda *_: (0,) * len(shape))


def _fwd_inproj(h0, g1, w_in):
    rows = h0.shape[0]
    tm = _row_tile(rows, ROW_TILE)

    def body(h_ref, g_ref, w_ref, uc_ref, qk_ref, vg_ref, lr_ref, n1_ref):
        h = h_ref[...]
        r = lax.rsqrt(jnp.mean(h * h, axis=-1, keepdims=True) + RMS_EPS)
        n = (h * r * g_ref[...]).astype(BF16)
        n1_ref[...] = n
        uc_ref[...] = _dot(n, w_ref[:, 0:1024]).astype(BF16)
        qk_ref[...] = _dot(n, w_ref[:, 1024:1536]).astype(BF16)
        vg_ref[...] = _dot(n, w_ref[:, 1536:2560]).astype(BF16)
        lr_ref[...] = _dot(n, w_ref[:, 2560:2688]).astype(BF16)

    return pl.pallas_call(
        body, name="fwd_inproj", grid=(rows // tm,),
        in_specs=[_rows(tm, D), _fixed((1, D)), _whole_vmem()],
        out_specs=[_rows(tm, 1024), _rows(tm, 512), _rows(tm, 1024), _rows(tm, RANK_P), _rows(tm, D)],
        out_shape=[jax.ShapeDtypeStruct((rows, 1024), BF16), jax.ShapeDtypeStruct((rows, 512), BF16),
                   jax.ShapeDtypeStruct((rows, 1024), BF16), jax.ShapeDtypeStruct((rows, RANK_P), BF16),
                   jax.ShapeDtypeStruct((rows, D), BF16)],
        compiler_params=_params(("parallel",)),
    )(h0, g1, w_in)


def _mid_rows(yc, yg, h0, tgt, w_out, wg, wu, wd, g2, g3, token, rows_per_example):
    rows = h0.shape[0]
    tm = _row_tile(rows, FFN_ROW_TILE)
    ff_blocks = [slice(lo, hi) for lo, hi in zip(FF_SPLIT[:-1], FF_SPLIT[1:])]

    def body(yc_ref, yg_ref, h0_ref, t_ref, wo_ref, wg_ref, wu_ref, wd_ref, g2_ref, g3_ref, token_ref,
             n2_ref, f_ref, da_ref, db_ref, dh2_ref, dh1_ref, dh1b_ref, dyc_ref, dyg_ref, part_ref):
        i = pl.program_id(0)
        h1 = h0_ref[...] + _dot(yc_ref[...], wo_ref[0:C_CONV, :]) + _dot(yg_ref[...], wo_ref[C_CONV:D, :])
        r2 = lax.rsqrt(jnp.mean(h1 * h1, axis=-1, keepdims=True) + RMS_EPS)
        xh2 = h1 * r2
        n2 = (xh2 * g2_ref[...]).astype(BF16)
        n2_ref[...] = n2
        y2 = jnp.zeros((tm, D), F32)
        for cs in ff_blocks:
            a = _dot(n2, wg_ref[cs, :], _NT)
            b = _dot(n2, wu_ref[cs, :], _NT)
            f = (a * _sigmoid(a) * b).astype(BF16)
            f_ref[:, cs] = f
            da_ref[:, cs] = a.astype(BF16)
            db_ref[:, cs] = b.astype(BF16)
            y2 = y2 + _dot(f, wd_ref[cs, :])
        h2 = h1 + y2
        r3 = lax.rsqrt(jnp.mean(h2 * h2, axis=-1, keepdims=True) + RMS_EPS)
        xh3 = h2 * r3
        g3 = g3_ref[...]
        pos = (i * tm + lax.broadcasted_iota(jnp.int32, (tm, 1), 0)) % rows_per_example
        valid = pos >= LEAD
        err = jnp.where(valid, xh3 * g3 - t_ref[...], 0.0)
        loss = 0.5 / D * jnp.sum(jnp.sum(err * err, axis=-1, keepdims=True), axis=0, keepdims=True)
        dy = err * (1.0 / D)
        dg3 = jnp.sum(dy * xh3, axis=0, keepdims=True)
        dxh = dy * g3
        dh2 = r3 * (dxh - xh3 * jnp.mean(dxh * xh3, axis=-1, keepdims=True))
        dh2b = dh2.astype(BF16)
        dh2_ref[...] = dh2b
        dn2 = jnp.zeros((tm, D), F32)
        for cs in ff_blocks:
            df = _dot(dh2b, wd_ref[cs, :], _NT)
            a = da_ref[:, cs].astype(F32)
            b = db_ref[:, cs].astype(F32)
            sg = _sigmoid(a)
            da = (df * b * sg * (1.0 + a * (1.0 - sg))).astype(BF16)
            db = (df * a * sg).astype(BF16)
            da_ref[:, cs] = da
            db_ref[:, cs] = db
            dn2 = dn2 + _dot(da, wg_ref[cs, :]) + _dot(db, wu_ref[cs, :])
        dg2 = jnp.sum(dn2 * xh2, axis=0, keepdims=True)
        dxh2 = dn2 * g2_ref[...]
        dh1 = dh2 + r2 * (dxh2 - xh2 * jnp.mean(dxh2 * xh2, axis=-1, keepdims=True))
        dh1_ref[...] = dh1
        dh1b = dh1.astype(BF16)
        dh1b_ref[...] = dh1b
        dyc_ref[...] = _dot(dh1b, wo_ref[0:C_CONV, :], _NT)
        dyg_ref[...] = _dot(dh1b, wo_ref[C_CONV:D, :], _NT)

        @pl.when(i == 0)
        def _():
            part_ref[...] = jnp.zeros_like(part_ref)

        part_ref[0:1, :] += dg3
        part_ref[1:2, :] += dg2
        part_ref[2:3, :] += jnp.broadcast_to(loss, (1, D))

    return pl.pallas_call(
        body, name="mid_rows", grid=(rows // tm,),
        in_specs=[_rows(tm, C_CONV), _rows(tm, GLA_V), _rows(tm, D), _rows(tm, D), _whole_vmem(), _whole_vmem(),
                  _whole_vmem(), _whole_vmem(), _fixed((1, D)), _fixed((1, D)), _fixed((8, 128))],
        out_specs=[_rows(tm, D), _rows(tm, D_FF), _rows(tm, D_FF), _rows(tm, D_FF), _rows(tm, D), _rows(tm, D),
                   _rows(tm, D), _rows(tm, C_CONV), _rows(tm, GLA_V), _fixed((8, D))],
        out_shape=[jax.ShapeDtypeStruct((rows, D), BF16)] + [jax.ShapeDtypeStruct((rows, D_FF), BF16)] * 3
        + [jax.ShapeDtypeStruct((rows, D), BF16), jax.ShapeDtypeStruct((rows, D), F32),
           jax.ShapeDtypeStruct((rows, D), BF16), jax.ShapeDtypeStruct((rows, C_CONV), F32),
           jax.ShapeDtypeStruct((rows, GLA_V), F32), jax.ShapeDtypeStruct((8, D), F32)],
        compiler_params=_params(("arbitrary",)),
    )(yc, yg, h0, tgt, w_out, wg, wu, wd, g2, g3, token)


def _bwd_inproj(duc, dqk, dvg, dlr, dh1, h0, w_in, g1, token, rows_per_example):
    rows = h0.shape[0]
    n_ex = rows // rows_per_example
    tm = _row_tile(rows_per_example, ROW_TILE)
    tiles_per_example = rows_per_example // tm
    n_steps = rows // tm

    def body(duc_ref, dqk_ref, dvg_ref, dlr_ref, dh1_ref, h_ref, w_ref, g_ref, token_ref, gx_ref, part_ref, dmeta_ref,
             buf_ref, sems):
        dn = (_dot(duc_ref[...], w_ref[:, 0:1024], _NT) + _dot(dqk_ref[...], w_ref[:, 1024:1536], _NT)
              + _dot(dvg_ref[...], w_ref[:, 1536:2560], _NT) + _dot(dlr_ref[...], w_ref[:, 2560:2688], _NT))
        h = h_ref[...]
        r = lax.rsqrt(jnp.mean(h * h, axis=-1, keepdims=True) + RMS_EPS)
        xh = h * r
        dg = jnp.sum(dn * xh, axis=0, keepdims=True)
        dxh = dn * g_ref[...]
        dh0 = dh1_ref[...] + r * (dxh - xh * jnp.mean(dxh * xh, axis=-1, keepdims=True))
        i = pl.program_id(0)

        def copies(step):
            slot, b, j = step % 2, step // tiles_per_example, step % tiles_per_example
            out = [(j == 0, pltpu.make_async_copy(buf_ref.at[slot, pl.ds(LEAD, tm - LEAD)],
                                                   gx_ref.at[b, pl.ds(0, tm - LEAD)], sems.at[slot]))]
            if tiles_per_example > 1:
                out.append((j != 0, pltpu.make_async_copy(
                    buf_ref.at[slot], gx_ref.at[b, pl.ds(pl.multiple_of(jnp.maximum(j * tm - LEAD, 0), 8), tm)],
                    sems.at[slot])))
            return out

        def each(step, act):
            for cond, cp in copies(step):
                pl.when(cond)(functools.partial(act, cp))

        @pl.when(i >= 2)
        def _():
            each(i - 2, lambda cp: cp.wait())

        buf_ref[i % 2] = dh0
        each(i, lambda cp: cp.start())

        @pl.when(i == n_steps - 1)
        def _():
            each(i, lambda cp: cp.wait())
            if n_steps > 1:
                each(i - 1, lambda cp: cp.wait())

        @pl.when(i == 0)
        def _():
            part_ref[...] = jnp.zeros_like(part_ref)
            dmeta_ref[...] = jnp.zeros_like(dmeta_ref)

        part_ref[0:1, :] += dg

        @pl.when(i % tiles_per_example == 0)
        def _():
            dmeta_ref[...] += dh0[ZROWS:LEAD, :]

    return pl.pallas_call(
        body, name="bwd_inproj", grid=(n_steps,),
        in_specs=[_rows(tm, 1024), _rows(tm, 512), _rows(tm, 1024), _rows(tm, RANK_P), _rows(tm, D), _rows(tm, D),
                  _whole_vmem(), _fixed((1, D)), _fixed((8, 128))],
        out_specs=[_any(), _fixed((8, D)), _fixed((N_META, D))],
        out_shape=[jax.ShapeDtypeStruct((n_ex, rows_per_example - LEAD, D), F32), jax.ShapeDtypeStruct((8, D), F32),
                   jax.ShapeDtypeStruct((N_META, D), F32)],
        scratch_shapes=[pltpu.VMEM((2, tm, D), F32), pltpu.SemaphoreType.DMA((2,))],
        compiler_params=_params(("arbitrary",)),
    )(duc, dqk, dvg, dlr, dh1, h0, w_in, g1, token)


def _dw_blocked(a, bs, width, name):
    rows, m = a.shape
    ws = [b.shape[1] for b in bs]
    assert sum(ws) >= N_DEV * width
    tk = _row_tile(rows, DW_ROW_TILE)
    nk = rows // tk

    def body(a_ref, *refs):
        b_refs, o_ref, acc_ref = refs[:len(bs)], refs[len(bs)], refs[len(bs) + 1]
        k = pl.program_id(0)

        @pl.when(k == 0)
        def _():
            acc_ref[...] = jnp.zeros_like(acc_ref)

        at = a_ref[...].T
        off = 0
        for b_ref, w in zip(b_refs, ws):
            acc_ref[:, off:off + w] += _dot(at, b_ref[...])
            off += w

        @pl.when(k == nk - 1)
        def _():
            for d in range(N_DEV):
                o_ref[d] = acc_ref[:, d * width:(d + 1) * width].astype(BF16)

    return pl.pallas_call(
        body, name=name, grid=(nk,),
        in_specs=[_rows(tk, m)] + [_rows(tk, w) for w in ws],
        out_specs=_fixed((N_DEV, m, width)),
        out_shape=jax.ShapeDtypeStruct((N_DEV, m, width), BF16),
        scratch_shapes=[pltpu.VMEM((m, sum(ws)), F32)],
        compiler_params=_params(("arbitrary",)),
    )(a, *bs)


def _matmul_tn(a, b, name):
    rows, m = a.shape
    n = b.shape[1]
    tk = _row_tile(rows, DW_ROW_TILE)
    tn = n if n <= 1024 else FF_CHUNK
    tm_ = m if m <= 1024 else FF_CHUNK
    assert n % tn == 0 and m % tm_ == 0
    nk = rows // tk

    def body(a_ref, b_ref, o_ref, acc_ref):
        k = pl.program_id(2)

        @pl.when(k == 0)
        def _():
            acc_ref[...] = jnp.zeros_like(acc_ref)

        acc_ref[...] += _dot(a_ref[...], b_ref[...], _TN)

        @pl.when(k == nk - 1)
        def _():
            o_ref[...] = acc_ref[...].astype(BF16)

    return pl.pallas_call(
        body, name=name, grid=(m // tm_, n // tn, nk),
        in_specs=[pl.BlockSpec((tk, tm_), lambda i, j, k: (k, i)), pl.BlockSpec((tk, tn), lambda i, j, k: (k, j))],
        out_specs=pl.BlockSpec((tm_, tn), lambda i, j, k: (i, j)),
        out_shape=jax.ShapeDtypeStruct((m, n), BF16),
        scratch_shapes=[pltpu.VMEM((tm_, tn), F32)],
        compiler_params=_params(("parallel", "parallel", "arbitrary")),
    )(a, b)


HALO = 32
LN_ROWS = 352
LANES = 128


def _shifted(win, offsets):
    for r in range(8):
        js = [j for j, k in enumerate(offsets) if k % 8 == r]
        if js:
            rolled = win if r == 0 else pltpu.roll(win, CHUNK + HALO - r, 0)
            for j in js:
                yield j, rolled[offsets[j] - r:offsets[j] - r + CHUNK]


def _glu_into(uc_ref, vs_ref, n_chunk):
    vs_ref[0:CHUNK, :] = jnp.zeros((CHUNK, C_CONV), F32)

    def glu(i, carry):
        base = pl.multiple_of(i * CHUNK, CHUNK)
        val = uc_ref[pl.ds(base, CHUNK), 0:C_CONV].astype(F32)
        gate = uc_ref[pl.ds(base, CHUNK), C_CONV:2 * C_CONV].astype(F32)
        vs_ref[pl.ds(base + CHUNK, CHUNK), :] = val * _sigmoid(gate)
        return carry

    lax.fori_loop(0, n_chunk, glu, 0, unroll=3)


def _fwd_conv(uc, conv_w, conv_b, ln_g, ln_b, token, n_ex):
    rows = uc.shape[0]
    lp = rows // n_ex
    n_chunk = lp // CHUNK

    def body(uc_ref, w_ref, b_ref, lg_ref, lb_ref, token_ref, ypre_ref, yc_ref, vs_ref):
        _glu_into(uc_ref, vs_ref, n_chunk)

        def conv(i, carry):
            base = pl.multiple_of(i * CHUNK, CHUNK)
            for lb in range(C_CONV // LANES):
                ls = slice(lb * LANES, (lb + 1) * LANES)
                win = vs_ref[pl.ds(base + CHUNK - HALO, CHUNK + HALO), ls]
                acc = jnp.broadcast_to(b_ref[:, ls], (CHUNK, LANES))
                for j, rows_j in _shifted(win, [HALO - (CONV_W - 1) + j for j in range(CONV_W)]):
                    acc = acc + w_ref[j:j + 1, ls] * rows_j
                ypre_ref[pl.ds(base, CHUNK), ls] = acc
            return carry

        lax.fori_loop(0, n_chunk, conv, 0, unroll=3)

        ln_rows = _row_tile(lp, LN_ROWS)

        def norm(i, carry):
            base = pl.multiple_of(i * ln_rows, 16)
            y = ypre_ref[pl.ds(base, ln_rows), :]
            mu = jnp.mean(y, axis=-1, keepdims=True)
            yc_ = y - mu
            rstd = lax.rsqrt(jnp.mean(yc_ * yc_, axis=-1, keepdims=True) + LN_EPS)
            s = yc_ * rstd * lg_ref[...] + lb_ref[...]
            yc_ref[pl.ds(base, ln_rows), :] = (s * _sigmoid(s)).astype(BF16)
            return carry

        lax.fori_loop(0, lp // ln_rows, norm, 0)

    ex = lambda w: pl.BlockSpec((lp, w), lambda b: (b, 0))
    return pl.pallas_call(
        body, name="fwd_conv", grid=(n_ex,),
        in_specs=[ex(2 * C_CONV), _fixed((32, C_CONV)), _fixed((1, C_CONV)), _fixed((1, C_CONV)), _fixed((1, C_CONV)),
                  _fixed((8, 128))],
        out_specs=[ex(C_CONV), ex(C_CONV)],
        out_shape=[jax.ShapeDtypeStruct((rows, C_CONV), F32), jax.ShapeDtypeStruct((rows, C_CONV), BF16)],
        scratch_shapes=[pltpu.VMEM((lp + CHUNK, C_CONV), F32)],
        compiler_params=_params(("parallel",)),
    )(uc, conv_w, conv_b, ln_g, ln_b, token)


def _bwd_conv(uc, ypre, dyc, conv_w, ln_g, ln_b, token, n_ex):
    rows = uc.shape[0]
    lp = rows // n_ex
    n_chunk = lp // CHUNK

    def body(uc_ref, ypre_ref, dyc_ref, w_ref, lg_ref, lb_ref, token_ref, duc_ref, dw_ref, dvec_ref, vs_ref, dys_ref,
             dwacc_ref):
        _glu_into(uc_ref, vs_ref, n_chunk)
        dys_ref[pl.ds(lp, CHUNK), :] = jnp.zeros((CHUNK, C_CONV), F32)
        dwacc_ref[...] = jnp.zeros_like(dwacc_ref)

        ln_rows = _row_tile(lp, LN_ROWS)

        def ln_bwd(i, carry):
            dcb, dlg, dlb = carry
            base = pl.multiple_of(i * ln_rows, 16)
            y = ypre_ref[pl.ds(base, ln_rows), :]
            mu = jnp.mean(y, axis=-1, keepdims=True)
            yc_ = y - mu
            rstd = lax.rsqrt(jnp.mean(yc_ * yc_, axis=-1, keepdims=True) + LN_EPS)
            xh = yc_ * rstd
            s = xh * lg_ref[...] + lb_ref[...]
            sg = _sigmoid(s)
            ds = dyc_ref[pl.ds(base, ln_rows), :] * (sg * (1.0 + s * (1.0 - sg)))
            dxh = ds * lg_ref[...]
            dy = rstd * (dxh - jnp.mean(dxh, axis=-1, keepdims=True) - xh * jnp.mean(dxh * xh, axis=-1, keepdims=True))
            dys_ref[pl.ds(base, ln_rows), :] = dy
            return (dcb + jnp.sum(dy, axis=0, keepdims=True), dlg + jnp.sum(ds * xh, axis=0, keepdims=True),
                    dlb + jnp.sum(ds, axis=0, keepdims=True))

        zero = jnp.zeros((1, C_CONV), F32)
        dcb, dlg, dlb = lax.fori_loop(0, lp // ln_rows, ln_bwd, (zero, zero, zero))

        @pl.when(pl.program_id(0) == 0)
        def _():
            dvec_ref[...] = jnp.zeros_like(dvec_ref)
            dw_ref[...] = jnp.zeros_like(dw_ref)

        dvec_ref[0:1, :] += dcb
        dvec_ref[1:2, :] += dlg
        dvec_ref[2:3, :] += dlb

        def taps(i, carry):
            base = pl.multiple_of(i * CHUNK, CHUNK)
            for lb in range(C_CONV // LANES):
                ls = slice(lb * LANES, (lb + 1) * LANES)
                dwin = dys_ref[pl.ds(base, CHUNK + HALO), ls]
                vwin = vs_ref[pl.ds(base + CHUNK - HALO, CHUNK + HALO), ls]
                dy = dwin[0:CHUNK]
                acc = jnp.zeros((CHUNK, LANES), F32)
                for j, rows_j in _shifted(dwin, [CONV_W - 1 - j for j in range(CONV_W)]):
                    acc = acc + w_ref[j:j + 1, ls] * rows_j
                for j, rows_j in _shifted(vwin, [HALO - (CONV_W - 1) + j for j in range(CONV_W)]):
                    dwacc_ref[8 * j:8 * j + 8, ls] += jnp.sum((dy * rows_j).reshape(CHUNK // 8, 8, LANES), axis=0)
                val = uc_ref[pl.ds(base, CHUNK), ls].astype(F32)
                gate = uc_ref[pl.ds(base, CHUNK), C_CONV + lb * LANES:C_CONV + (lb + 1) * LANES].astype(F32)
                sg = _sigmoid(gate)
                duc_ref[pl.ds(base, CHUNK), ls] = (acc * sg).astype(BF16)
                duc_ref[pl.ds(base, CHUNK), C_CONV + lb * LANES:C_CONV + (lb + 1) * LANES] = (
                    acc * val * sg * (1.0 - sg)).astype(BF16)
            return carry

        lax.fori_loop(0, n_chunk, taps, 0, unroll=3)
        for j in range(CONV_W):
            dw_ref[j:j + 1, :] += jnp.sum(dwacc_ref[8 * j:8 * j + 8, :], axis=0, keepdims=True)

    ex = lambda w: pl.BlockSpec((lp, w), lambda b: (b, 0))
    return pl.pallas_call(
        body, name="bwd_conv", grid=(n_ex,),
        in_specs=[ex(2 * C_CONV), ex(C_CONV), ex(C_CONV), _fixed((32, C_CONV)), _fixed((1, C_CONV)), _fixed((1, C_CONV)),
                  _fixed((8, 128))],
        out_specs=[ex(2 * C_CONV), _fixed((32, C_CONV)), _fixed((8, C_CONV))],
        out_shape=[jax.ShapeDtypeStruct((rows, 2 * C_CONV), BF16), jax.ShapeDtypeStruct((32, C_CONV), F32),
                   jax.ShapeDtypeStruct((8, C_CONV), F32)],
        scratch_shapes=[pltpu.VMEM((lp + CHUNK, C_CONV), F32), pltpu.VMEM((lp + CHUNK, C_CONV), F32),
                        pltpu.VMEM((8 * 32, C_CONV), F32)],
        compiler_params=_params(("arbitrary",)),
    )(uc, ypre, dyc, conv_w, ln_g, ln_b, token)


def _seg_chunks(n_chunk):
    return max(c for c in (11, 3, 1) if n_chunk % c == 0)


def _block_mask(shape, row_block, lane_block):
    return (lax.broadcasted_iota(jnp.int32, shape, 0) // row_block) == (lax.broadcasted_iota(jnp.int32, shape, 1) // lane_block)


def _per_head_rows(x, mask):
    return jnp.where(mask, jnp.concatenate([x] * GLA_H, axis=0), 0)


def _fold_heads(full, lane_block):
    lane = lax.broadcasted_iota(jnp.int32, (1, full.shape[1]), 1) // lane_block
    out = jnp.where(lane == 0, full[0:CHUNK], 0.0)
    for h in range(1, GLA_H):
        out = out + jnp.where(lane == h, full[h * CHUNK:(h + 1) * CHUNK], 0.0)
    return out


PAIRS = GLA_H // 2


def _expand_state(blocks):
    lane = lax.broadcasted_iota(jnp.int32, (GLA_DV, 128), 1) // GLA_DK
    zero = jnp.zeros_like(blocks[0])
    rows = []
    for h in range(GLA_H):
        p, hh = divmod(h, 2)
        mine = jnp.where(lane == hh, blocks[p], 0)
        rows.append(jnp.concatenate([mine if q == p else zero for q in range(PAIRS)], axis=1))
    return jnp.concatenate(rows, axis=0)


def _compact_state(full, p):
    lane = lax.broadcasted_iota(jnp.int32, (GLA_DV, 128), 1) // GLA_DK
    ls = slice(128 * p, 128 * (p + 1))
    return jnp.where(lane == 0, full[2 * p * GLA_DV:(2 * p + 1) * GLA_DV, ls], full[(2 * p + 1) * GLA_DV:(2 * p + 2) * GLA_DV, ls])


def _causal_heads():
    return (lax.broadcasted_iota(jnp.int32, (CHUNK, GLA_H * CHUNK), 1) % CHUNK) <= lax.broadcasted_iota(
        jnp.int32, (CHUNK, GLA_H * CHUNK), 0)


def _cumsum_rows(x):
    row = lax.broadcasted_iota(jnp.int32, x.shape, 0)
    s = 1
    while s < CHUNK:
        x = x + jnp.where(row >= s, pltpu.roll(x, s, 0), 0.0)
        s *= 2
    return x


def _rev_cumsum_rows(x):
    row = lax.broadcasted_iota(jnp.int32, x.shape, 0)
    s = 1
    while s < CHUNK:
        x = x + jnp.where(row < CHUNK - s, pltpu.roll(x, CHUNK - s, 0), 0.0)
        s *= 2
    return x


def _gate_terms(lr_ref, w2_ref, gb_ref, rs, first_pos):
    z = _dot(lr_ref[rs, :].astype(BF16), w2_ref[...]) + gb_ref[...]
    la = (jnp.minimum(z, 0.0) - jnp.log(1.0 + jnp.exp(-jnp.abs(z)))) * (1.0 / TAU)
    pos = first_pos + lax.broadcasted_iota(jnp.int32, (CHUNK, 1), 0)
    live = pos >= ZROWS
    la = jnp.where(live, la, 0.0)
    return z, live, _cumsum_rows(la)


def _fwd_gla(qk, vg, lr, w2p, gb, ng, token, n_ex):
    rows = qk.shape[0]
    lp = rows // n_ex
    n_chunk = lp // CHUNK
    sc = _seg_chunks(n_chunk)
    n_seg = n_chunk // sc
    seg = sc * CHUNK

    def body(qk_ref, vg_ref, lr_ref, w2_ref, gb_ref, ng_ref, token_ref, yg_ref, o_ref, st_ref, state_ref):
        sidx = pl.program_id(1)

        @pl.when(sidx == 0)
        def _():
            state_ref[...] = jnp.zeros_like(state_ref)

        causal = _causal_heads()
        k_mask = _block_mask((GLA_H * CHUNK, GLA_K), CHUNK, GLA_DK)
        v_mask = _block_mask((GLA_H * CHUNK, GLA_V), CHUNK, GLA_DV)

        def chunk(ci, carry):
            base = pl.multiple_of(ci * CHUNK, CHUNK)
            rs = pl.ds(base, CHUNK)
            _, _, bcum = _gate_terms(lr_ref, w2_ref, gb_ref, rs, (sidx * sc + ci) * CHUNK)
            bl = bcum[CHUNK - 1:CHUNK, :]
            q = qk_ref[rs, 0:GLA_K].astype(F32)
            k = qk_ref[rs, GLA_K:2 * GLA_K].astype(F32)
            qt = (q * (GLA_DK ** -0.5) * jnp.exp(bcum)).astype(BF16)
            kt = (k * jnp.exp(-bcum)).astype(BF16)
            kh = (k * jnp.exp(bl - bcum)).astype(BF16)
            vb = vg_ref[rs, 0:GLA_V].astype(BF16)
            state = [state_ref[p] for p in range(PAIRS)]
            for p in range(PAIRS):
                st_ref[ci, p] = state[p]
            a = jnp.where(causal, _dot(qt, _per_head_rows(kt, k_mask), _NT), 0.0)
            o = _dot(a.astype(BF16), _per_head_rows(vb, v_mask)) + _dot(
                qt, _expand_state([s.astype(BF16) for s in state]), _NT)
            o_ref[rs, :] = o
            for h in range(GLA_H):
                hs = slice(h * GLA_DV, (h + 1) * GLA_DV)
                oh = o[:, hs]
                ro = lax.rsqrt(jnp.mean(oh * oh, axis=-1, keepdims=True) + RMS_EPS)
                g = vg_ref[rs, GLA_V + h * GLA_DV:GLA_V + (h + 1) * GLA_DV].astype(F32)
                yg_ref[rs, hs] = (oh * ro * ng_ref[...] * g * _sigmoid(g)).astype(BF16)
            kv = _dot(vb, kh, _TN)
            decay = jnp.exp(bl)
            for p in range(PAIRS):
                state_ref[p] = state[p] * decay[:, 128 * p:128 * (p + 1)] + _compact_state(kv, p)
            return carry

        lax.fori_loop(0, sc, chunk, 0, unroll=True)

    sg = lambda w: pl.BlockSpec((seg, w), lambda b, s: (b * n_seg + s, 0))
    return pl.pallas_call(
        body, name="fwd_gla", grid=(n_ex, n_seg),
        in_specs=[sg(2 * GLA_K), sg(2 * GLA_V), sg(RANK_P), _fixed((RANK_P, GLA_K)), _fixed((1, GLA_K)), _fixed((1, GLA_DV)),
                  _fixed((8, 128))],
        out_specs=[sg(GLA_V), sg(GLA_V), pl.BlockSpec((sc, PAIRS, GLA_DV, 128), lambda b, s: (b * n_seg + s, 0, 0, 0))],
        out_shape=[jax.ShapeDtypeStruct((rows, GLA_V), BF16), jax.ShapeDtypeStruct((rows, GLA_V), F32),
                   jax.ShapeDtypeStruct((n_ex * n_chunk, PAIRS, GLA_DV, 128), F32)],
        scratch_shapes=[pltpu.VMEM((PAIRS, GLA_DV, 128), F32)],
        compiler_params=_params(("parallel", "arbitrary")),
    )(qk, vg, lr, w2p, gb, ng, token)


def _bwd_gla(qk, vg, lr, o, st, dyg, w2p, gb, ng, yc, yg, dh1b, token, n_ex):
    rows = qk.shape[0]
    lp = rows // n_ex
    n_chunk = lp // CHUNK
    sc = _seg_chunks(n_chunk)
    n_seg = n_chunk // sc
    seg = sc * CHUNK

    def body(qk_ref, vg_ref, lr_ref, o_ref, st_ref, dyg_ref, w2_ref, gb_ref, ng_ref, yc_ref, yg_ref, dh1_ref, token_ref,
             dqk_ref, dvg_ref, dlr_ref, dw2_ref, dvec_ref, dwo_ref, gt_ref, dz_ref, dwo_acc):
        step = pl.program_id(1)
        sidx = n_seg - 1 - step
        first = (step == 0) & (pl.program_id(0) == 0)

        @pl.when(step == 0)
        def _():
            gt_ref[...] = jnp.zeros_like(gt_ref)

        @pl.when(first)
        def _():
            dw2_ref[...] = jnp.zeros_like(dw2_ref)
            dvec_ref[...] = jnp.zeros_like(dvec_ref)
            dwo_acc[...] = jnp.zeros_like(dwo_acc)

        d1 = dh1_ref[...]
        dwo_acc[0:C_CONV, :] += _dot(yc_ref[...], d1, _TN)
        dwo_acc[C_CONV:D, :] += _dot(yg_ref[...], d1, _TN)

        @pl.when((step == n_seg - 1) & (pl.program_id(0) == n_ex - 1))
        def _():
            dwo_ref[...] = dwo_acc[...].astype(BF16)

        causal = _causal_heads()
        k_mask = _block_mask((GLA_H * CHUNK, GLA_K), CHUNK, GLA_DK)
        v_mask = _block_mask((GLA_H * CHUNK, GLA_V), CHUNK, GLA_DV)
        last_row = lax.broadcasted_iota(jnp.int32, (CHUNK, 1), 0) == CHUNK - 1
        ng = ng_ref[...]

        def chunk(ii, dng):
            ci = sc - 1 - ii
            base = pl.multiple_of(ci * CHUNK, CHUNK)
            rs = pl.ds(base, CHUNK)
            z, live, bcum = _gate_terms(lr_ref, w2_ref, gb_ref, rs, (sidx * sc + ci) * CHUNK)
            bl = bcum[CHUNK - 1:CHUNK, :]
            ebl = jnp.exp(bl)
            q = qk_ref[rs, 0:GLA_K].astype(F32)
            k = qk_ref[rs, GLA_K:2 * GLA_K].astype(F32)
            eb = jnp.exp(bcum)
            enb = jnp.exp(-bcum)
            ehb = jnp.exp(bl - bcum)
            qt = q * (GLA_DK ** -0.5) * eb
            kt = k * enb
            kh = k * ehb
            qtb = qt.astype(BF16)
            vb = vg_ref[rs, 0:GLA_V].astype(BF16)
            k_rows = _per_head_rows(kt.astype(BF16), k_mask)
            v_rows = _per_head_rows(vb, v_mask)
            gt = [gt_ref[p] for p in range(PAIRS)]
            gtb = _expand_state([g_.astype(BF16) for g_ in gt])
            s_in = [st_ref[ci, p] for p in range(PAIRS)]
            dos = []
            for h in range(GLA_H):
                hs = slice(h * GLA_DV, (h + 1) * GLA_DV)
                gs = slice(GLA_V + h * GLA_DV, GLA_V + (h + 1) * GLA_DV)
                oh = o_ref[rs, hs]
                ro = lax.rsqrt(jnp.mean(oh * oh, axis=-1, keepdims=True) + RMS_EPS)
                on = oh * ro
                g = vg_ref[rs, gs].astype(F32)
                sg = _sigmoid(g)
                dout = dyg_ref[rs, hs]
                dvg_ref[rs, gs] = (dout * on * ng * (sg * (1.0 + g * (1.0 - sg)))).astype(BF16)
                dw = dout * g * sg
                dng = dng + jnp.sum(dw * on, axis=0, keepdims=True)
                don = dw * ng
                dos.append((ro * (don - on * jnp.mean(don * on, axis=-1, keepdims=True))).astype(BF16))
            dob = jnp.concatenate(dos, axis=1)
            a = jnp.where(causal, _dot(qtb, k_rows, _NT), 0.0).astype(BF16)
            da = jnp.where(causal, _dot(dob, v_rows, _NT), 0.0).astype(BF16)
            dv = _fold_heads(_dot(a, dob, _TN), GLA_DV) + _dot(kh.astype(BF16), gtb, _NT)
            dvg_ref[rs, 0:GLA_V] = dv.astype(BF16)
            dkh = _dot(vb, gtb)
            dqt = _dot(da, k_rows) + _dot(dob, _expand_state([s_.astype(BF16) for s_ in s_in]))
            dkt = _fold_heads(_dot(da, qtb, _TN), GLA_DK)
            dbl = jnp.concatenate([jnp.sum(gt[p] * s_in[p], axis=0, keepdims=True) for p in range(PAIRS)], axis=1) * ebl
            dbl = dbl + jnp.sum(dkh * kh, axis=0, keepdims=True)
            dqk_ref[rs, 0:GLA_K] = (dqt * (GLA_DK ** -0.5) * eb).astype(BF16)
            dqk_ref[rs, GLA_K:2 * GLA_K] = (dkt * enb + dkh * ehb).astype(BF16)
            db = dqt * qt - dkt * kt - dkh * kh
            db = jnp.where(last_row, db + dbl, db)
            dla = jnp.where(live, _rev_cumsum_rows(db), 0.0)
            dz_ref[rs, :] = dla * (1.0 / TAU) * (1.0 - _sigmoid(z))
            dstate = _dot(dob, qtb, _TN)
            for p in range(PAIRS):
                gt_ref[p] = _compact_state(dstate, p) + gt[p] * ebl[:, 128 * p:128 * (p + 1)]
            return dng

        dng = lax.fori_loop(0, sc, chunk, jnp.zeros((1, GLA_DV), F32), unroll=True)
        dz = dz_ref[...]
        dzb = dz.astype(BF16)
        dlr_ref[...] = _dot(dzb, w2_ref[...], _NT).astype(BF16)
        dw2_ref[...] += _dot(lr_ref[...].astype(BF16), dzb, _TN)
        dvec_ref[0:1, :] += jnp.sum(dz, axis=0, keepdims=True)
        dvec_ref[1:2, 0:GLA_DV] += dng

    sg_ = lambda w: pl.BlockSpec((seg, w), lambda b, s: (b * n_seg + n_seg - 1 - s, 0))
    return pl.pallas_call(
        body, name="bwd_gla", grid=(n_ex, n_seg),
        in_specs=[sg_(2 * GLA_K), sg_(2 * GLA_V), sg_(RANK_P), sg_(GLA_V),
                  pl.BlockSpec((sc, PAIRS, GLA_DV, 128), lambda b, s: (b * n_seg + n_seg - 1 - s, 0, 0, 0)), sg_(GLA_V),
                  _fixed((RANK_P, GLA_K)), _fixed((1, GLA_K)), _fixed((1, GLA_DV)), sg_(C_CONV), sg_(GLA_V), sg_(D),
                  _fixed((8, 128))],
        out_specs=[sg_(2 * GLA_K), sg_(2 * GLA_V), sg_(RANK_P), _fixed((RANK_P, GLA_K)), _fixed((8, GLA_K)),
                   _fixed((D, D))],
        out_shape=[jax.ShapeDtypeStruct((rows, 2 * GLA_K), BF16), jax.ShapeDtypeStruct((rows, 2 * GLA_V), BF16),
                   jax.ShapeDtypeStruct((rows, RANK_P), BF16), jax.ShapeDtypeStruct((RANK_P, GLA_K), F32),
                   jax.ShapeDtypeStruct((8, GLA_K), F32), jax.ShapeDtypeStruct((D, D), BF16)],
        scratch_shapes=[pltpu.VMEM((PAIRS, GLA_DV, 128), F32), pltpu.VMEM((seg, GLA_K), F32), pltpu.VMEM((D, D), F32)],
        compiler_params=_params(("arbitrary", "arbitrary")),
    )(qk, vg, lr, o, st, dyg, w2p, gb, ng, yc, yg, dh1b, token)


def _pad_rows(x, tgt):
    return jnp.pad(x, ((0, 0), (LEAD, 0), (0, 0))), jnp.pad(tgt, ((0, 0), (LEAD, 0), (0, 0)))


def _local_step(h0, tgt_p, p, pass_on, late_weights, send_early):
    n_ex, lp, _ = h0.shape
    rows = n_ex * lp
    meta = jnp.broadcast_to(p["meta"][None], (n_ex, N_META, D))
    h0 = lax.dynamic_update_slice(h0, meta, (0, ZROWS, 0)).reshape(rows, D)
    tgt_p = tgt_p.reshape(rows, D)

    uc, qk, vg, lr, n1 = _fwd_inproj(h0, p["g1"], p["w_in"])
    ypre, yc = _fwd_conv(uc, p["conv_w"], p["conv_b"], p["ln_g"], p["ln_b"], p["token"], n_ex)
    token = pass_on(yc)
    yg, o, st = _fwd_gla(qk, vg, lr, p["w2"], p["gb"], p["ng"], token, n_ex)
    w_out, wg, wu, wd = late_weights(yg)
    n2, f, da, db, dh2, dh1, dh1b, dyc, dyg, part = _mid_rows(
        yc, yg, h0, tgt_p, w_out, wg, wu, wd, p["g2"], p["g3"], token, lp)
    g = {}
    token = send_early("ffn", [_matmul_tn(a_, b_, name).reshape(N_DEV, FF_S, D) for a_, b_, name in (
        (da, n2, "dw_gate"), (db, n2, "dw_up"), (f, dh2, "dw_down"))])
    dqk, dvg, dlr, g["w2"], g["gla_vec"], dw_out = _bwd_gla(
        qk, vg, lr, o, st, dyg, p["w2"], p["gb"], p["ng"], yc, yg, dh1b, token, n_ex)
    token = send_early("out", [dw_out.reshape(N_DEV, W_OUT_S, D)])
    duc, g["conv_w"], g["conv_vec"] = _bwd_conv(uc, ypre, dyc, p["conv_w"], p["ln_g"], p["ln_b"], token, n_ex)
    token = send_early("in", [_dw_blocked(n1, [duc, dqk, dvg, dlr], W_IN_S, "dw_in")])
    grad_x, g["in_vec"], g["meta"] = _bwd_inproj(duc, dqk, dvg, dlr, dh1, h0, p["w_in"], p["g1"], token, lp)
    g["ffn_vec"] = part
    return grad_x, g


W_IN_S = D_IN // N_DEV
W_OUT_S = D // N_DEV
FF_S = D_FF // N_DEV
CONV_S = C_CONV // N_DEV
GATE_S = GLA_K // N_DEV
SMALL_PACK = 64
CONV_ROW = 16
GATE_ROW = 48
VEC_ROWS = 16
_VEC_ROWS = (("norm_mix_g", D), ("conv_b", C_CONV), ("conv_ln_g", C_CONV), ("conv_ln_b", C_CONV), ("gla_gate_b", GLA_K),
             ("gla_norm_g", GLA_DV), ("norm_ffn_g", D), ("norm_final_g", D))
LOSS_ROW = len(_VEC_ROWS)


def _position():
    return lax.axis_index("x"), lax.axis_index("y"), lax.axis_index("c")


def _any():
    return pl.BlockSpec(memory_space=pl.ANY)


def _stage(mats, meta, conv_w, w2):
    n_t = len(mats) + 1

    def body(*refs):
        ins = refs[0:n_t - 1]
        meta_ref, cw_ref, w2_ref = refs[n_t - 1:n_t + 2]
        lands = refs[n_t + 2:2 * n_t + 2]
        shards = refs[2 * n_t + 2:3 * n_t + 2]
        sems = refs[3 * n_t + 2]
        for s_ref, w_ref in zip(shards, ins):
            s_ref[...] = w_ref[...].astype(BF16)
        sp = shards[n_t - 1]
        sp[...] = jnp.zeros_like(sp)
        sp[0:N_META, :] = meta_ref[...]
        sp[CONV_ROW:CONV_ROW + CONV_W, 0:CONV_S] = cw_ref[...]
        sp[GATE_ROW:GATE_ROW + RANK, 0:GATE_S] = w2_ref[...]
        x, y, c = _position()
        mine = [pltpu.make_async_copy(shards[t], lands[t].at[4 * x + 2 * y + c], sems.at[t]) for t in range(n_t)]
        for cp in mine:
            cp.start()
        for cp in mine:
            cp.wait()

    shard_shapes = [jax.ShapeDtypeStruct(m.shape, BF16) for m in mats] + [jax.ShapeDtypeStruct((SMALL_PACK, 128), F32)]
    res = pl.pallas_call(
        body, name="stage",
        out_shape=[jax.ShapeDtypeStruct((N_DEV,) + s.shape, s.dtype) for s in shard_shapes] + shard_shapes,
        in_specs=[_whole_vmem()] * (n_t + 2), out_specs=[_any()] * n_t + [_whole_vmem()] * n_t,
        scratch_shapes=[pltpu.SemaphoreType.DMA((n_t,))],
        compiler_params=pltpu.CompilerParams(vmem_limit_bytes=VMEM_LIMIT),
    )(*mats, meta, conv_w, w2)
    return res[0:n_t], res[n_t:]


_HBM = pl.BlockSpec(memory_space=pltpu.HBM)
_SEM = pl.BlockSpec(memory_space=pltpu.SEMAPHORE)
_EFFECT = pltpu.SideEffectType.DATAFLOW_SIDE_EFFECTING


_N_ROUTES = {"scatter": 7, "first": 4, "forward": 3}


def _routes(mode):
    x, y, c = _position()
    me = 4 * x + 2 * y + c
    if mode == "scatter":
        out = []
        for k in range(1, N_DEV):
            px = 1 - x if k & 4 else x
            py = 1 - y if k & 2 else y
            pc = 1 - c if k & 1 else c
            out.append(((px, py, pc), 4 * px + 2 * py + pc, me))
        return out
    if mode == "first":
        return [(pos, None, me) for pos in ((x, y, 1 - c), (1 - x, y, c), (x, 1 - y, c), (1 - x, 1 - y, c))]
    assert mode == "forward"
    return [((x, y, 1 - c), 4 * px + 2 * py + c, 4 * px + 2 * py + c) for px, py in ((1 - x, y), (x, 1 - y), (1 - x, 1 - y))]


def _route_copies(mode, n, src_refs, land_refs, send_sems, recv_sems):
    nr = _N_ROUTES[mode]
    for i, (pos, src_blk, dst_blk) in enumerate(_routes(mode)):
        for t in range(n):
            src = land_refs[t] if mode == "forward" else src_refs[t]
            yield pltpu.make_async_remote_copy(
                src_ref=src if src_blk is None else src.at[src_blk], dst_ref=land_refs[t].at[dst_blk],
                send_sem=send_sems.at[nr * t + i], recv_sem=recv_sems.at[nr * t + i], device_id=pos, device_id_type=MESH)


def _in_hbm(a):
    return pltpu.with_memory_space_constraint(a, pltpu.HBM)


def _send_start(name, groups, mode, after):
    sizes = [(len(s), len(l)) for s, l in groups]
    bufs = [b for s, l in groups for b in list(s) + list(l)]
    nb, ng = len(bufs), len(groups)

    def body(*refs):
        sems = refs[nb + 1:nb + 1 + 2 * ng]
        token = refs[2 * nb + 2 * ng + 1]
        off = 0
        for gi, (ns, n) in enumerate(sizes):
            for cp in _route_copies(mode, n, refs[off:off + ns], refs[off + ns:off + ns + n], sems[2 * gi], sems[2 * gi + 1]):
                cp.start()
            off += ns + n
        token[...] = jnp.zeros_like(token)

    res = pl.pallas_call(
        body, name=name,
        out_shape=(*[pltpu.SemaphoreType.DMA((_N_ROUTES[mode] * n,)) for _, n in sizes for _ in range(2)],
                   *[pltpu.HBM(b.shape, b.dtype) for b in bufs], jax.ShapeDtypeStruct((8, 128), F32)),
        in_specs=[_HBM] * nb + [_any()], out_specs=(*[_SEM] * (2 * ng), *[_HBM] * nb, _whole_vmem()),
        input_output_aliases={i: 2 * ng + i for i in range(nb)},
        compiler_params=pltpu.CompilerParams(has_side_effects=_EFFECT),
    )(*[_in_hbm(b) for b in bufs], after)
    handles, off = [], 2 * ng
    for gi, (ns, n) in enumerate(sizes):
        handles.append((res[2 * gi], res[2 * gi + 1], res[off:off + ns], res[off + ns:off + ns + n]))
        off += ns + n
    return handles, res[2 * ng + nb]


def _send_wait(name, send_sems, recv_sems, srcs, lands, mode, after):
    n, ns = len(lands), len(srcs)
    after = after if isinstance(after, tuple) else (after,)

    def body(*refs):
        src_refs, land_refs = refs[0:ns], refs[ns:ns + n]
        send_sems, recv_sems = refs[ns + n:ns + n + 2]
        for cp in _route_copies(mode, n, src_refs, land_refs, send_sems, recv_sems):
            cp.wait_send()
            cp.wait_recv()

    bufs = list(srcs) + list(lands)
    res = pl.pallas_call(
        body, name=name,
        out_shape=tuple(pltpu.HBM(b.shape, b.dtype) for b in bufs),
        in_specs=[_HBM] * len(bufs) + [_SEM, _SEM] + [_any()] * len(after), out_specs=tuple([_HBM] * len(bufs)),
        input_output_aliases={i: i for i in range(len(bufs))},
        compiler_params=pltpu.CompilerParams(has_side_effects=_EFFECT),
    )(*bufs, send_sems, recv_sems, *after)
    return res[0:ns], res[ns:ns + n]


def _unshard_in(a_in, a_small, token):
    def body(a_ref, s_ref, token_ref, w_ref, meta_ref, cw_ref, w2_ref):
        w_ref[:, D_IN:D_INP] = jnp.zeros((D, D_INP - D_IN), BF16)
        w2_ref[...] = jnp.zeros_like(w2_ref)
        for d in range(N_DEV):
            w_ref[:, d * W_IN_S:(d + 1) * W_IN_S] = a_ref[d]
            meta_ref[:, d * 128:(d + 1) * 128] = s_ref[d, 0:N_META, :]
            cw_ref[:, d * CONV_S:(d + 1) * CONV_S] = s_ref[d, CONV_ROW:CONV_ROW + 32, 0:CONV_S]
            w2_ref[0:RANK, d * GATE_S:(d + 1) * GATE_S] = s_ref[d, GATE_ROW:GATE_ROW + RANK, 0:GATE_S].astype(BF16)

    return pl.pallas_call(
        body, name="unshard_in",
        out_shape=[jax.ShapeDtypeStruct((D, D_INP), BF16), jax.ShapeDtypeStruct((N_META, D), F32),
                   jax.ShapeDtypeStruct((32, C_CONV), F32), jax.ShapeDtypeStruct((RANK_P, GLA_K), BF16)],
        compiler_params=pltpu.CompilerParams(vmem_limit_bytes=VMEM_LIMIT),
    )(a_in, a_small, token)


def _pack_small(g):
    def body(meta_ref, cw_ref, w2_ref, in_vec, ffn_vec, conv_vec, gla_vec, sp, vp):
        sp[...] = jnp.zeros_like(sp)
        vp[...] = jnp.zeros_like(vp)
        for d in range(N_DEV):
            sp[d, 0:N_META, :] = meta_ref[:, d * 128:(d + 1) * 128]
            sp[d, CONV_ROW:CONV_ROW + 32, 0:CONV_S] = cw_ref[:, d * CONV_S:(d + 1) * CONV_S]
            sp[d, GATE_ROW:GATE_ROW + RANK, 0:GATE_S] = w2_ref[0:RANK, d * GATE_S:(d + 1) * GATE_S]
            vp[d, 0:1, :] = in_vec[0:1, :]
            vp[d, 1:4, 0:C_CONV] = conv_vec[0:3, :]
            vp[d, 4:5, 0:GLA_K] = gla_vec[0:1, :]
            vp[d, 5:6, 0:GLA_DV] = gla_vec[1:2, 0:GLA_DV]
            vp[d, 6:7, :] = ffn_vec[1:2, :]
            vp[d, 7:8, :] = ffn_vec[0:1, :]
            vp[d, LOSS_ROW:LOSS_ROW + 1, :] = ffn_vec[2:3, :]

    return pl.pallas_call(
        body, name="pack_small",
        out_shape=[jax.ShapeDtypeStruct((N_DEV, SMALL_PACK, 128), F32), jax.ShapeDtypeStruct((N_DEV, VEC_ROWS, D), F32)],
    )(g["meta"], g["conv_w"], g["w2"], g["in_vec"], g["ffn_vec"], g["conv_vec"], g["gla_vec"])


def _adamw(w, g, m, v):
    m = ADAM_B1 * m + (1.0 - ADAM_B1) * g
    v = ADAM_B2 * v + (1.0 - ADAM_B2) * (g * g)
    m_hat = m / (1.0 - ADAM_B1 ** ADAM_STEP)
    v_hat = v / (1.0 - ADAM_B2 ** ADAM_STEP)
    return -ADAM_LR * (m_hat / (jnp.sqrt(v_hat) + ADAM_EPS) + ADAM_WD * w), m, v


def _update_matrix(recv, own, me, w, m, v, name):
    _, r, c = recv.shape
    tr = _row_tile(r, 256)

    def body(me_ref, recv_ref, own_ref, w_ref, m_ref, v_ref, g_ref, d_ref, nm_ref, nv_ref):
        g = jnp.zeros((tr, c), F32)
        for s in range(N_DEV):
            g = g + jnp.where(me_ref[0] == s, own_ref[...], recv_ref[s]).astype(F32)
        g_ref[...] = g
        d_ref[...], nm_ref[...], nv_ref[...] = _adamw(w_ref[...], g, m_ref[...], v_ref[...])

    one = pl.BlockSpec((None, tr, c), lambda i, me_ref: (0, i, 0))
    return pl.pallas_call(
        body, name=name,
        grid_spec=pltpu.PrefetchScalarGridSpec(
            num_scalar_prefetch=1, grid=(r // tr,),
            in_specs=[pl.BlockSpec((N_DEV, tr, c), lambda i, me_ref: (0, i, 0)),
                      pl.BlockSpec((None, tr, c), lambda i, me_ref: (me_ref[0], i, 0)), one, one, one],
            out_specs=[one] * 4),
        out_shape=[jax.ShapeDtypeStruct((1, r, c), F32)] * 4,
        compiler_params=_params(("parallel",)),
    )(me, recv, own, w, m, v)


_SMALL = ("meta_tokens", "conv_w", "gla_w_gate2") + tuple(n for n, _ in _VEC_ROWS)


def _update_small(me, srecv, vrecv, sown, vown, w, m, v):
    n = len(_SMALL)

    def body(*refs):
        me_ref, s_ref, v_ref, so_ref, vo_ref = refs[0:5]
        w_refs, m_refs, v_refs = refs[5:5 + n], refs[5 + n:5 + 2 * n], refs[5 + 2 * n:5 + 3 * n]
        outs = refs[5 + 3 * n:]
        ssum = jnp.zeros((SMALL_PACK, 128), F32)
        vsum = jnp.zeros((VEC_ROWS, D), F32)
        for s in range(N_DEV):
            ssum = ssum + jnp.where(me_ref[0] == s, so_ref[s], s_ref[s])
            vsum = vsum + jnp.where(me_ref[0] == s, vo_ref[s], v_ref[s])
        grads = [ssum[0:N_META, :], ssum[CONV_ROW:CONV_ROW + CONV_W, 0:CONV_S], ssum[GATE_ROW:GATE_ROW + RANK, 0:GATE_S]]
        grads += [vsum[i:i + 1, 0:width] for i, (_, width) in enumerate(_VEC_ROWS)]
        for i, g in enumerate(grads):
            d, nm, nv = _adamw(w_refs[i][...], g, m_refs[i][...], v_refs[i][...])
            outs[i][...] = g
            outs[n + i][...] = d
            outs[2 * n + i][...] = nm
            outs[3 * n + i][...] = nv
        outs[4 * n][...] = vsum[LOSS_ROW:LOSS_ROW + 1, 0:128]

    shapes = [jax.ShapeDtypeStruct(t.shape, F32) for t in w]
    res = pl.pallas_call(
        body, name="update_small", out_shape=shapes * 4 + [jax.ShapeDtypeStruct((1, 128), F32)],
        in_specs=[pl.BlockSpec(memory_space=pltpu.SMEM)] + [_whole_vmem()] * (4 + 3 * n),
    )(me, srecv, vrecv, sown, vown, *w, *m, *v)
    return res[0:n], res[n:2 * n], res[2 * n:3 * n], res[3 * n:4 * n], res[4 * n]


_WEIGHTS = ("meta_tokens", "norm_mix_g", "w_in", "conv_w", "conv_b", "conv_ln_g", "conv_ln_b", "gla_w_gate2", "gla_gate_b",
            "gla_norm_g", "w_out", "norm_ffn_g", "w_ffn_gate", "w_ffn_up", "w_ffn_down", "norm_final_g")
_MATRICES = ("w_in", "w_out", "w_ffn_gate", "w_ffn_up", "w_ffn_down")
_TRANSPOSED = ("w_ffn_gate", "w_ffn_up")


def kernel(x, meta_tokens, norm_mix_g, w_in, conv_w, conv_b, conv_ln_g, conv_ln_b, gla_w_gate2, gla_gate_b, gla_norm_g, w_out, norm_ffn_g, w_ffn_gate, w_ffn_up, w_ffn_down, norm_final_g, loss_target, m_meta_tokens, m_norm_mix_g, m_w_in, m_conv_w, m_conv_b, m_conv_ln_g, m_conv_ln_b, m_gla_w_gate2, m_gla_gate_b, m_gla_norm_g, m_w_out, m_norm_ffn_g, m_w_ffn_gate, m_w_ffn_up, m_w_ffn_down, m_norm_final_g, v_meta_tokens, v_norm_mix_g, v_w_in, v_conv_w, v_conv_b, v_conv_ln_g, v_conv_ln_b, v_gla_w_gate2, v_gla_gate_b, v_gla_norm_g, v_w_out, v_norm_ffn_g, v_w_ffn_gate, v_w_ffn_up, v_w_ffn_down, v_norm_final_g):
    given = dict(locals())
    two_d = lambda a: a.reshape(1, -1) if a.ndim == 1 else a.reshape(a.shape[-2:])
    fams = [{n: given[pre + n] for n in _WEIGHTS} for pre in ("", "m_", "v_")]
    for f in fams:
        for n in _TRANSPOSED:
            f[n] = f[n].transpose(0, 2, 1)
    w = fams[0]

    lands, shards = _stage([two_d(w[n]) for n in _MATRICES], w["meta_tokens"], two_d(w["conv_w"]), two_d(w["gla_w_gate2"]))
    soon, later = (0, 5), (1, 2, 3, 4)
    pick = lambda seq, idx: [seq[i] for i in idx]
    (first, ffn_first), started = _send_start(
        "gather_first_start", [(pick(shards, soon), pick(lands, soon)), (pick(shards, later), pick(lands, later))],
        "first", norm_mix_g)
    h0, tgt_p = _pad_rows(x, loss_target)
    _, arrived = _send_wait("gather_first_wait", *first, "first", (h0, tgt_p, started))
    (forward,), token = _send_start("gather_forward_start", [([], arrived)], "forward", started)
    _, (a_in, a_small) = _send_wait("gather_forward_wait", *forward, "forward", token)
    w_in, meta, conv_taps, w2 = _unshard_in(a_in, a_small, token)
    p = dict(meta=meta, conv_w=conv_taps, w2=w2, w_in=w_in, g1=norm_mix_g, conv_b=conv_b, ln_g=conv_ln_g, ln_b=conv_ln_b,
             gb=gla_gate_b, ng=gla_norm_g, g2=norm_ffn_g, g3=two_d(norm_final_g), token=token)
    passed = {}

    def pass_on(after):
        _, arrived_ffn = _send_wait("gather_ffn_first_wait", *ffn_first, "first", after)
        (passed["sent"],), token = _send_start("gather_ffn_forward_start", [([], arrived_ffn)], "forward", after)
        return token

    def late_weights(after):
        _, (a_out, a_g, a_u, a_d) = _send_wait("gather_ffn_forward_wait", *passed["sent"], "forward", after)
        return a_out.reshape(D, D), a_g.reshape(D_FF, D), a_u.reshape(D_FF, D), a_d.reshape(D_FF, D)

    sent = {}

    def send_early(tag, mats):
        landing = [_in_hbm(lax.empty(m_.shape, m_.dtype)) for m_ in mats]
        (sent[tag],), token = _send_start("scatter_" + tag + "_start", [(mats, landing)], "scatter", norm_mix_g)
        return token

    grad_x, g = _local_step(h0, tgt_p, p, pass_on, late_weights, send_early)

    token = send_early("small", list(_pack_small(g)))
    x_, y_, c_ = _position()
    me = (4 * x_ + 2 * y_ + c_).astype(jnp.int32).reshape(1)
    res = {}
    for tag, names in (("ffn", ("w_ffn_gate", "w_ffn_up", "w_ffn_down")), ("out", ("w_out",)), ("in", ("w_in",))):
        own, recv = _send_wait("scatter_" + tag + "_wait", *sent[tag], "scatter", token)
        for n, o_, r_ in zip(names, own, recv):
            res[n] = _update_matrix(r_, o_, me, *[f[n] for f in fams], "update_" + n)
            token = res[n][1]
    (sown, vown), (srecv, vrecv) = _send_wait("scatter_small_wait", *sent["small"], "scatter", token)
    small = _update_small(me, srecv, vrecv, sown, vown, *[[two_d(f[n]) for n in _SMALL] for f in fams])
    for i, n in enumerate(_SMALL):
        res[n] = [fam[i].reshape(w[n].shape) for fam in small[0:4]]
    for n in _TRANSPOSED:
        res[n] = [t.transpose(0, 2, 1) for t in res[n]]
    outs = [small[4][0, 0], grad_x]
    for k in range(4):
        outs += [res[n][k] for n in _WEIGHTS]
    return tuple(outs)
```

```python
import functools

import jax
import jax.numpy as jnp
from jax import lax
from jax.experimental import pallas as pl
from jax.experimental.pallas import tpu as pltpu

F32 = jnp.float32
BF16 = jnp.bfloat16

D = 1024
N_META = 16
C_CONV = 512
CONV_W = 31
GLA_H = 4
GLA_DK = 64
GLA_DV = 128
GLA_K = GLA_H * GLA_DK
GLA_V = GLA_H * GLA_DV
RANK = 16
RANK_P = 128
TAU = 16.0
CHUNK = 64
LEAD = CHUNK
ZROWS = LEAD - N_META
D_IN = 2 * C_CONV + 2 * GLA_K + 2 * GLA_V + RANK
D_INP = D_IN - RANK + RANK_P
D_FF = 2816
FF_CHUNK = 1408
FF_SPLIT = (0, 1536, D_FF)
RMS_EPS = 1e-6
LN_EPS = 1e-5
N_DEV = 8

ADAM_LR = 0.001
ADAM_B1 = 0.9
ADAM_B2 = 0.999
ADAM_EPS = 1e-08
ADAM_WD = 0.01
ADAM_STEP = 10

VMEM_LIMIT = 60 * 1024 * 1024
ROW_TILE = 1056
FFN_ROW_TILE = 352
DW_ROW_TILE = 1408
MESH = pl.DeviceIdType.MESH

_NN = (((1,), (0,)), ((), ()))
_NT = (((1,), (1,)), ((), ()))
_TN = (((0,), (0,)), ((), ()))


def _dot(a, b, dims=_NN):
    return lax.dot_general(a, b, dims, preferred_element_type=F32)


def _sigmoid(x):
    return 1.0 / (1.0 + jnp.exp(-x))


def _row_tile(rows, target):
    best = None
    for t in range(16, min(rows, target) + 1, 16):
        if rows % t == 0:
            best = t
    assert best is not None, rows
    return best


def _params(sem=None):
    return pltpu.CompilerParams(dimension_semantics=sem, vmem_limit_bytes=VMEM_LIMIT)


def _whole_vmem():
    return pl.BlockSpec(memory_space=pltpu.VMEM)


def _rows(tm, width):
    return pl.BlockSpec((tm, width), lambda i: (i, 0))


def _fixed(shape):
    return pl.BlockSpec(shape, lambda *_: (0,) * len(shape))


def _fwd_inproj(h0, g1, w_in):
    rows = h0.shape[0]
    tm = _row_tile(rows, ROW_TILE)

    def body(h_ref, g_ref, w_ref, uc_ref, qk_ref, vg_ref, lr_ref, n1_ref):
        h = h_ref[...]
        r = lax.rsqrt(jnp.mean(h * h, axis=-1, keepdims=True) + RMS_EPS)
        n = (h * r * g_ref[...]).astype(BF16)
        n1_ref[...] = n
        uc_ref[...] = _dot(n, w_ref[:, 0:1024]).astype(BF16)
        qk_ref[...] = _dot(n, w_ref[:, 1024:1536]).astype(BF16)
        vg_ref[...] = _dot(n, w_ref[:, 1536:2560]).astype(BF16)
        lr_ref[...] = _dot(n, w_ref[:, 2560:2688]).astype(BF16)

    return pl.pallas_call(
        body, name="fwd_inproj", grid=(rows // tm,),
        in_specs=[_rows(tm, D), _fixed((1, D)), _whole_vmem()],
        out_specs=[_rows(tm, 1024), _rows(tm, 512), _rows(tm, 1024), _rows(tm, RANK_P), _rows(tm, D)],
        out_shape=[jax.ShapeDtypeStruct((rows, 1024), BF16), jax.ShapeDtypeStruct((rows, 512), BF16),
                   jax.ShapeDtypeStruct((rows, 1024), BF16), jax.ShapeDtypeStruct((rows, RANK_P), BF16),
                   jax.ShapeDtypeStruct((rows, D), BF16)],
        compiler_params=_params(("parallel",)),
    )(h0, g1, w_in)


def _mid_rows(yc, yg, h0, tgt, w_out, wg, wu, wd, g2, g3, token, rows_per_example):
    rows = h0.shape[0]
    tm = _row_tile(rows, FFN_ROW_TILE)
    ff_blocks = [slice(lo, hi) for lo, hi in zip(FF_SPLIT[:-1], FF_SPLIT[1:])]

    def body(yc_ref, yg_ref, h0_ref, t_ref, wo_ref, wg_ref, wu_ref, wd_ref, g2_ref, g3_ref, token_ref,
             n2_ref, f_ref, da_ref, db_ref, dh2_ref, dh1_ref, dh1b_ref, dyc_ref, dyg_ref, part_ref):
        i = pl.program_id(0)
        h1 = h0_ref[...] + _dot(yc_ref[...], wo_ref[0:C_CONV, :]) + _dot(yg_ref[...], wo_ref[C_CONV:D, :])
        r2 = lax.rsqrt(jnp.mean(h1 * h1, axis=-1, keepdims=True) + RMS_EPS)
        xh2 = h1 * r2
        n2 = (xh2 * g2_ref[...]).astype(BF16)
        n2_ref[...] = n2
        y2 = jnp.zeros((tm, D), F32)
        for cs in ff_blocks:
            a = _dot(n2, wg_ref[cs, :], _NT)
            b = _dot(n2, wu_ref[cs, :], _NT)
            f = (a * _sigmoid(a) * b).astype(BF16)
            f_ref[:, cs] = f
            da_ref[:, cs] = a.astype(BF16)
            db_ref[:, cs] = b.astype(BF16)
            y2 = y2 + _dot(f, wd_ref[cs, :])
        h2 = h1 + y2
        r3 = lax.rsqrt(jnp.mean(h2 * h2, axis=-1, keepdims=True) + RMS_EPS)
        xh3 = h2 * r3
        g3 = g3_ref[...]
        pos = (i * tm + lax.broadcasted_iota(jnp.int32, (tm, 1), 0)) % rows_per_example
        valid = pos >= LEAD
        err = jnp.where(valid, xh3 * g3 - t_ref[...], 0.0)
        loss = 0.5 / D * jnp.sum(jnp.sum(err * err, axis=-1, keepdims=True), axis=0, keepdims=True)
        dy = err * (1.0 / D)
        dg3 = jnp.sum(dy * xh3, axis=0, keepdims=True)
        dxh = dy * g3
        dh2 = r3 * (dxh - xh3 * jnp.mean(dxh * xh3, axis=-1, keepdims=True))
        dh2b = dh2.astype(BF16)
        dh2_ref[...] = dh2b
        dn2 = jnp.zeros((tm, D), F32)
        for cs in ff_blocks:
            df = _dot(dh2b, wd_ref[cs, :], _NT)
            a = da_ref[:, cs].astype(F32)
            b = db_ref[:, cs].astype(F32)
            sg = _sigmoid(a)
            da = (df * b * sg * (1.0 + a * (1.0 - sg))).astype(BF16)
            db = (df * a * sg).astype(BF16)
            da_ref[:, cs] = da
            db_ref[:, cs] = db
            dn2 = dn2 + _dot(da, wg_ref[cs, :]) + _dot(db, wu_ref[cs, :])
        dg2 = jnp.sum(dn2 * xh2, axis=0, keepdims=True)
        dxh2 = dn2 * g2_ref[...]
        dh1 = dh2 + r2 * (dxh2 - xh2 * jnp.mean(dxh2 * xh2, axis=-1, keepdims=True))
        dh1_ref[...] = dh1
        dh1b = dh1.astype(BF16)
        dh1b_ref[...] = dh1b
        dyc_ref[...] = _dot(dh1b, wo_ref[0:C_CONV, :], _NT)
        dyg_ref[...] = _dot(dh1b, wo_ref[C_CONV:D, :], _NT)

        @pl.when(i == 0)
        def _():
            part_ref[...] = jnp.zeros_like(part_ref)

        part_ref[0:1, :] += dg3
        part_ref[1:2, :] += dg2
        part_ref[2:3, :] += jnp.broadcast_to(loss, (1, D))

    return pl.pallas_call(
        body, name="mid_rows", grid=(rows // tm,),
        in_specs=[_rows(tm, C_CONV), _rows(tm, GLA_V), _rows(tm, D), _rows(tm, D), _whole_vmem(), _whole_vmem(),
                  _whole_vmem(), _whole_vmem(), _fixed((1, D)), _fixed((1, D)), _fixed((8, 128))],
        out_specs=[_rows(tm, D), _rows(tm, D_FF), _rows(tm, D_FF), _rows(tm, D_FF), _rows(tm, D), _rows(tm, D),
                   _rows(tm, D), _rows(tm, C_CONV), _rows(tm, GLA_V), _fixed((8, D))],
        out_shape=[jax.ShapeDtypeStruct((rows, D), BF16)] + [jax.ShapeDtypeStruct((rows, D_FF), BF16)] * 3
        + [jax.ShapeDtypeStruct((rows, D), BF16), jax.ShapeDtypeStruct((rows, D), F32),
           jax.ShapeDtypeStruct((rows, D), BF16), jax.ShapeDtypeStruct((rows, C_CONV), F32),
           jax.ShapeDtypeStruct((rows, GLA_V), F32), jax.ShapeDtypeStruct((8, D), F32)],
        compiler_params=_params(("arbitrary",)),
    )(yc, yg, h0, tgt, w_out, wg, wu, wd, g2, g3, token)


def _bwd_inproj(duc, dqk, dvg, dlr, dh1, h0, w_in, g1, token, rows_per_example):
    rows = h0.shape[0]
    n_ex = rows // rows_per_example
    tm = _row_tile(rows_per_example, ROW_TILE)
    tiles_per_example = rows_per_example // tm
    n_steps = rows // tm

    def body(duc_ref, dqk_ref, dvg_ref, dlr_ref, dh1_ref, h_ref, w_ref, g_ref, token_ref, gx_ref, part_ref, dmeta_ref,
             buf_ref, sems):
        dn = (_dot(duc_ref[...], w_ref[:, 0:1024], _NT) + _dot(dqk_ref[...], w_ref[:, 1024:1536], _NT)
              + _dot(dvg_ref[...], w_ref[:, 1536:2560], _NT) + _dot(dlr_ref[...], w_ref[:, 2560:2688], _NT))
        h = h_ref[...]
        r = lax.rsqrt(jnp.mean(h * h, axis=-1, keepdims=True) + RMS_EPS)
        xh = h * r
        dg = jnp.sum(dn * xh, axis=0, keepdims=True)
        dxh = dn * g_ref[...]
        dh0 = dh1_ref[...] + r * (dxh - xh * jnp.mean(dxh * xh, axis=-1, keepdims=True))
        i = pl.program_id(0)

        def copies(step):
            slot, b, j = step % 2, step // tiles_per_example, step % tiles_per_example
            out = [(j == 0, pltpu.make_async_copy(buf_ref.at[slot, pl.ds(LEAD, tm - LEAD)],
                                                   gx_ref.at[b, pl.ds(0, tm - LEAD)], sems.at[slot]))]
            if tiles_per_example > 1:
                out.append((j != 0, pltpu.make_async_copy(
                    buf_ref.at[slot], gx_ref.at[b, pl.ds(pl.multiple_of(jnp.maximum(j * tm - LEAD, 0), 8), tm)],
                    sems.at[slot])))
            return out

        def each(step, act):
            for cond, cp in copies(step):
                pl.when(cond)(functools.partial(act, cp))

        @pl.when(i >= 2)
        def _():
            each(i - 2, lambda cp: cp.wait())

        buf_ref[i % 2] = dh0
        each(i, lambda cp: cp.start())

        @pl.when(i == n_steps - 1)
        def _():
            each(i, lambda cp: cp.wait())
            if n_steps > 1:
                each(i - 1, lambda cp: cp.wait())

        @pl.when(i == 0)
        def _():
            part_ref[...] = jnp.zeros_like(part_ref)
            dmeta_ref[...] = jnp.zeros_like(dmeta_ref)

        part_ref[0:1, :] += dg

        @pl.when(i % tiles_per_example == 0)
        def _():
            dmeta_ref[...] += dh0[ZROWS:LEAD, :]

    return pl.pallas_call(
        body, name="bwd_inproj", grid=(n_steps,),
        in_specs=[_rows(tm, 1024), _rows(tm, 512), _rows(tm, 1024), _rows(tm, RANK_P), _rows(tm, D), _rows(tm, D),
                  _whole_vmem(), _fixed((1, D)), _fixed((8, 128))],
        out_specs=[_any(), _fixed((8, D)), _fixed((N_META, D))],
        out_shape=[jax.ShapeDtypeStruct((n_ex, rows_per_example - LEAD, D), F32), jax.ShapeDtypeStruct((8, D), F32),
                   jax.ShapeDtypeStruct((N_META, D), F32)],
        scratch_shapes=[pltpu.VMEM((2, tm, D), F32), pltpu.SemaphoreType.DMA((2,))],
        compiler_params=_params(("arbitrary",)),
    )(duc, dqk, dvg, dlr, dh1, h0, w_in, g1, token)


def _dw_blocked(a, bs, width, name):
    rows, m = a.shape
    ws = [b.shape[1] for b in bs]
    assert sum(ws) >= N_DEV * width
    tk = _row_tile(rows, DW_ROW_TILE)
    nk = rows // tk

    def body(a_ref, *refs):
        b_refs, o_ref, acc_ref = refs[:len(bs)], refs[len(bs)], refs[len(bs) + 1]
        k = pl.program_id(0)

        @pl.when(k == 0)
        def _():
            acc_ref[...] = jnp.zeros_like(acc_ref)

        at = a_ref[...].T
        off = 0
        for b_ref, w in zip(b_refs, ws):
            acc_ref[:, off:off + w] += _dot(at, b_ref[...])
            off += w

        @pl.when(k == nk - 1)
        def _():
            for d in range(N_DEV):
                o_ref[d] = acc_ref[:, d * width:(d + 1) * width].astype(BF16)

    return pl.pallas_call(
        body, name=name, grid=(nk,),
        in_specs=[_rows(tk, m)] + [_rows(tk, w) for w in ws],
        out_specs=_fixed((N_DEV, m, width)),
        out_shape=jax.ShapeDtypeStruct((N_DEV, m, width), BF16),
        scratch_shapes=[pltpu.VMEM((m, sum(ws)), F32)],
        compiler_params=_params(("arbitrary",)),
    )(a, *bs)


def _matmul_tn(a, b, name):
    rows, m = a.shape
    n = b.shape[1]
    tk = _row_tile(rows, DW_ROW_TILE)
    tn = n if n <= 1024 else FF_CHUNK
    tm_ = m if m <= 1024 else FF_CHUNK
    assert n % tn == 0 and m % tm_ == 0
    nk = rows // tk

    def body(a_ref, b_ref, o_ref, acc_ref):
        k = pl.program_id(2)

        @pl.when(k == 0)
        def _():
            acc_ref[...] = jnp.zeros_like(acc_ref)

        acc_ref[...] += _dot(a_ref[...], b_ref[...], _TN)

        @pl.when(k == nk - 1)
        def _():
            o_ref[...] = acc_ref[...].astype(BF16)

    return pl.pallas_call(
        body, name=name, grid=(m // tm_, n // tn, nk),
        in_specs=[pl.BlockSpec((tk, tm_), lambda i, j, k: (k, i)), pl.BlockSpec((tk, tn), lambda i, j, k: (k, j))],
        out_specs=pl.BlockSpec((tm_, tn), lambda i, j, k: (i, j)),
        out_shape=jax.ShapeDtypeStruct((m, n), BF16),
        scratch_shapes=[pltpu.VMEM((tm_, tn), F32)],
        compiler_params=_params(("parallel", "parallel", "arbitrary")),
    )(a, b)


HALO = 32
LN_ROWS = 352
LANES = 128


def _shifted(win, offsets):
    for r in range(8):
        js = [j for j, k in enumerate(offsets) if k % 8 == r]
        if js:
            rolled = win if r == 0 else pltpu.roll(win, CHUNK + HALO - r, 0)
            for j in js:
                yield j, rolled[offsets[j] - r:offsets[j] - r + CHUNK]


def _glu_into(uc_ref, vs_ref, n_chunk):
    vs_ref[0:CHUNK, :] = jnp.zeros((CHUNK, C_CONV), F32)

    def glu(i, carry):
        base = pl.multiple_of(i * CHUNK, CHUNK)
        val = uc_ref[pl.ds(base, CHUNK), 0:C_CONV].astype(F32)
        gate = uc_ref[pl.ds(base, CHUNK), C_CONV:2 * C_CONV].astype(F32)
        vs_ref[pl.ds(base + CHUNK, CHUNK), :] = val * _sigmoid(gate)
        return carry

    lax.fori_loop(0, n_chunk, glu, 0, unroll=3)


def _fwd_conv(uc, conv_w, conv_b, ln_g, ln_b, token, n_ex):
    rows = uc.shape[0]
    lp = rows // n_ex
    n_chunk = lp // CHUNK

    def body(uc_ref, w_ref, b_ref, lg_ref, lb_ref, token_ref, ypre_ref, yc_ref, vs_ref):
        _glu_into(uc_ref, vs_ref, n_chunk)

        def conv(i, carry):
            base = pl.multiple_of(i * CHUNK, CHUNK)
            for lb in range(C_CONV // LANES):
                ls = slice(lb * LANES, (lb + 1) * LANES)
                win = vs_ref[pl.ds(base + CHUNK - HALO, CHUNK + HALO), ls]
                acc = jnp.broadcast_to(b_ref[:, ls], (CHUNK, LANES))
                for j, rows_j in _shifted(win, [HALO - (CONV_W - 1) + j for j in range(CONV_W)]):
                    acc = acc + w_ref[j:j + 1, ls] * rows_j
                ypre_ref[pl.ds(base, CHUNK), ls] = acc
            return carry

        lax.fori_loop(0, n_chunk, conv, 0, unroll=3)

        ln_rows = _row_tile(lp, LN_ROWS)

        def norm(i, carry):
            base = pl.multiple_of(i * ln_rows, 16)
            y = ypre_ref[pl.ds(base, ln_rows), :]
            mu = jnp.mean(y, axis=-1, keepdims=True)
            yc_ = y - mu
            rstd = lax.rsqrt(jnp.mean(yc_ * yc_, axis=-1, keepdims=True) + LN_EPS)
            s = yc_ * rstd * lg_ref[...] + lb_ref[...]
            yc_ref[pl.ds(base, ln_rows), :] = (s * _sigmoid(s)).astype(BF16)
            return carry

        lax.fori_loop(0, lp // ln_rows, norm, 0)

    ex = lambda w: pl.BlockSpec((lp, w), lambda b: (b, 0))
    return pl.pallas_call(
        body, name="fwd_conv", grid=(n_ex,),
        in_specs=[ex(2 * C_CONV), _fixed((32, C_CONV)), _fixed((1, C_CONV)), _fixed((1, C_CONV)), _fixed((1, C_CONV)),
                  _fixed((8, 128))],
        out_specs=[ex(C_CONV), ex(C_CONV)],
        out_shape=[jax.ShapeDtypeStruct((rows, C_CONV), F32), jax.ShapeDtypeStruct((rows, C_CONV), BF16)],
        scratch_shapes=[pltpu.VMEM((lp + CHUNK, C_CONV), F32)],
        compiler_params=_params(("parallel",)),
    )(uc, conv_w, conv_b, ln_g, ln_b, token)


def _bwd_conv(uc, ypre, dyc, conv_w, ln_g, ln_b, token, n_ex):
    rows = uc.shape[0]
    lp = rows // n_ex
    n_chunk = lp // CHUNK

    def body(uc_ref, ypre_ref, dyc_ref, w_ref, lg_ref, lb_ref, token_ref, duc_ref, dw_ref, dvec_ref, vs_ref, dys_ref,
             dwacc_ref):
        _glu_into(uc_ref, vs_ref, n_chunk)
        dys_ref[pl.ds(lp, CHUNK), :] = jnp.zeros((CHUNK, C_CONV), F32)
        dwacc_ref[...] = jnp.zeros_like(dwacc_ref)

        ln_rows = _row_tile(lp, LN_ROWS)

        def ln_bwd(i, carry):
            dcb, dlg, dlb = carry
            base = pl.multiple_of(i * ln_rows, 16)
            y = ypre_ref[pl.ds(base, ln_rows), :]
            mu = jnp.mean(y, axis=-1, keepdims=True)
            yc_ = y - mu
            rstd = lax.rsqrt(jnp.mean(yc_ * yc_, axis=-1, keepdims=True) + LN_EPS)
            xh = yc_ * rstd
            s = xh * lg_ref[...] + lb_ref[...]
            sg = _sigmoid(s)
            ds = dyc_ref[pl.ds(base, ln_rows), :] * (sg * (1.0 + s * (1.0 - sg)))
            dxh = ds * lg_ref[...]
            dy = rstd * (dxh - jnp.mean(dxh, axis=-1, keepdims=True) - xh * jnp.mean(dxh * xh, axis=-1, keepdims=True))
            dys_ref[pl.ds(base, ln_rows), :] = dy
            return (dcb + jnp.sum(dy, axis=0, keepdims=True), dlg + jnp.sum(ds * xh, axis=0, keepdims=True),
                    dlb + jnp.sum(ds, axis=0, keepdims=True))

        zero = jnp.zeros((1, C_CONV), F32)
        dcb, dlg, dlb = lax.fori_loop(0, lp // ln_rows, ln_bwd, (zero, zero, zero))

        @pl.when(pl.program_id(0) == 0)
        def _():
            dvec_ref[...] = jnp.zeros_like(dvec_ref)
            dw_ref[...] = jnp.zeros_like(dw_ref)

        dvec_ref[0:1, :] += dcb
        dvec_ref[1:2, :] += dlg
        dvec_ref[2:3, :] += dlb

        def taps(i, carry):
            base = pl.multiple_of(i * CHUNK, CHUNK)
            for lb in range(C_CONV // LANES):
                ls = slice(lb * LANES, (lb + 1) * LANES)
                dwin = dys_ref[pl.ds(base, CHUNK + HALO), ls]
                vwin = vs_ref[pl.ds(base + CHUNK - HALO, CHUNK + HALO), ls]
                dy = dwin[0:CHUNK]
                acc = jnp.zeros((CHUNK, LANES), F32)
                for j, rows_j in _shifted(dwin, [CONV_W - 1 - j for j in range(CONV_W)]):
                    acc = acc + w_ref[j:j + 1, ls] * rows_j
                for j, rows_j in _shifted(vwin, [HALO - (CONV_W - 1) + j for j in range(CONV_W)]):
                    dwacc_ref[8 * j:8 * j + 8, ls] += jnp.sum((dy * rows_j).reshape(CHUNK // 8, 8, LANES), axis=0)
                val = uc_ref[pl.ds(base, CHUNK), ls].astype(F32)
                gate = uc_ref[pl.ds(base, CHUNK), C_CONV + lb * LANES:C_CONV + (lb + 1) * LANES].astype(F32)
                sg = _sigmoid(gate)
                duc_ref[pl.ds(base, CHUNK), ls] = (acc * sg).astype(BF16)
                duc_ref[pl.ds(base, CHUNK), C_CONV + lb * LANES:C_CONV + (lb + 1) * LANES] = (
                    acc * val * sg * (1.0 - sg)).astype(BF16)
            return carry

        lax.fori_loop(0, n_chunk, taps, 0, unroll=3)
        for j in range(CONV_W):
            dw_ref[j:j + 1, :] += jnp.sum(dwacc_ref[8 * j:8 * j + 8, :], axis=0, keepdims=True)

    ex = lambda w: pl.BlockSpec((lp, w), lambda b: (b, 0))
    return pl.pallas_call(
        body, name="bwd_conv", grid=(n_ex,),
        in_specs=[ex(2 * C_CONV), ex(C_CONV), ex(C_CONV), _fixed((32, C_CONV)), _fixed((1, C_CONV)), _fixed((1, C_CONV)),
                  _fixed((8, 128))],
        out_specs=[ex(2 * C_CONV), _fixed((32, C_CONV)), _fixed((8, C_CONV))],
        out_shape=[jax.ShapeDtypeStruct((rows, 2 * C_CONV), BF16), jax.ShapeDtypeStruct((32, C_CONV), F32),
                   jax.ShapeDtypeStruct((8, C_CONV), F32)],
        scratch_shapes=[pltpu.VMEM((lp + CHUNK, C_CONV), F32), pltpu.VMEM((lp + CHUNK, C_CONV), F32),
                        pltpu.VMEM((8 * 32, C_CONV), F32)],
        compiler_params=_params(("arbitrary",)),
    )(uc, ypre, dyc, conv_w, ln_g, ln_b, token)


def _seg_chunks(n_chunk):
    return max(c for c in (11, 3, 1) if n_chunk % c == 0)


def _block_mask(shape, row_block, lane_block):
    return (lax.broadcasted_iota(jnp.int32, shape, 0) // row_block) == (lax.broadcasted_iota(jnp.int32, shape, 1) // lane_block)


def _per_head_rows(x, mask):
    return jnp.where(mask, jnp.concatenate([x] * GLA_H, axis=0), 0)


def _fold_heads(full, lane_block):
    lane = lax.broadcasted_iota(jnp.int32, (1, full.shape[1]), 1) // lane_block
    out = jnp.where(lane == 0, full[0:CHUNK], 0.0)
    for h in range(1, GLA_H):
        out = out + jnp.where(lane == h, full[h * CHUNK:(h + 1) * CHUNK], 0.0)
    return out


PAIRS = GLA_H // 2


def _expand_state(blocks):
    lane = lax.broadcasted_iota(jnp.int32, (GLA_DV, 128), 1) // GLA_DK
    zero = jnp.zeros_like(blocks[0])
    rows = []
    for h in range(GLA_H):
        p, hh = divmod(h, 2)
        mine = jnp.where(lane == hh, blocks[p], 0)
        rows.append(jnp.concatenate([mine if q == p else zero for q in range(PAIRS)], axis=1))
    return jnp.concatenate(rows, axis=0)


def _compact_state(full, p):
    lane = lax.broadcasted_iota(jnp.int32, (GLA_DV, 128), 1) // GLA_DK
    ls = slice(128 * p, 128 * (p + 1))
    return jnp.where(lane == 0, full[2 * p * GLA_DV:(2 * p + 1) * GLA_DV, ls], full[(2 * p + 1) * GLA_DV:(2 * p + 2) * GLA_DV, ls])


def _causal_heads():
    return (lax.broadcasted_iota(jnp.int32, (CHUNK, GLA_H * CHUNK), 1) % CHUNK) <= lax.broadcasted_iota(
        jnp.int32, (CHUNK, GLA_H * CHUNK), 0)


def _cumsum_rows(x):
    row = lax.broadcasted_iota(jnp.int32, x.shape, 0)
    s = 1
    while s < CHUNK:
        x = x + jnp.where(row >= s, pltpu.roll(x, s, 0), 0.0)
        s *= 2
    return x


def _rev_cumsum_rows(x):
    row = lax.broadcasted_iota(jnp.int32, x.shape, 0)
    s = 1
    while s < CHUNK:
        x = x + jnp.where(row < CHUNK - s, pltpu.roll(x, CHUNK - s, 0), 0.0)
        s *= 2
    return x


def _gate_terms(lr_ref, w2_ref, gb_ref, rs, first_pos):
    z = _dot(lr_ref[rs, :].astype(BF16), w2_ref[...]) + gb_ref[...]
    la = (jnp.minimum(z, 0.0) - jnp.log(1.0 + jnp.exp(-jnp.abs(z)))) * (1.0 / TAU)
    pos = first_pos + lax.broadcasted_iota(jnp.int32, (CHUNK, 1), 0)
    live = pos >= ZROWS
    la = jnp.where(live, la, 0.0)
    return z, live, _cumsum_rows(la)


def _fwd_gla(qk, vg, lr, w2p, gb, ng, token, n_ex):
    rows = qk.shape[0]
    lp = rows // n_ex
    n_chunk = lp // CHUNK
    sc = _seg_chunks(n_chunk)
    n_seg = n_chunk // sc
    seg = sc * CHUNK

    def body(qk_ref, vg_ref, lr_ref, w2_ref, gb_ref, ng_ref, token_ref, yg_ref, o_ref, st_ref, state_ref):
        sidx = pl.program_id(1)

        @pl.when(sidx == 0)
        def _():
            state_ref[...] = jnp.zeros_like(state_ref)

        causal = _causal_heads()
        k_mask = _block_mask((GLA_H * CHUNK, GLA_K), CHUNK, GLA_DK)
        v_mask = _block_mask((GLA_H * CHUNK, GLA_V), CHUNK, GLA_DV)

        def chunk(ci, carry):
            base = pl.multiple_of(ci * CHUNK, CHUNK)
            rs = pl.ds(base, CHUNK)
            _, _, bcum = _gate_terms(lr_ref, w2_ref, gb_ref, rs, (sidx * sc + ci) * CHUNK)
            bl = bcum[CHUNK - 1:CHUNK, :]
            q = qk_ref[rs, 0:GLA_K].astype(F32)
            k = qk_ref[rs, GLA_K:2 * GLA_K].astype(F32)
            qt = (q * (GLA_DK ** -0.5) * jnp.exp(bcum)).astype(BF16)
            kt = (k * jnp.exp(-bcum)).astype(BF16)
            kh = (k * jnp.exp(bl - bcum)).astype(BF16)
            vb = vg_ref[rs, 0:GLA_V].astype(BF16)
            state = [state_ref[p] for p in range(PAIRS)]
            for p in range(PAIRS):
                st_ref[ci, p] = state[p]
            a = jnp.where(causal, _dot(qt, _per_head_rows(kt, k_mask), _NT), 0.0)
            o = _dot(a.astype(BF16), _per_head_rows(vb, v_mask)) + _dot(
                qt, _expand_state([s.astype(BF16) for s in state]), _NT)
            o_ref[rs, :] = o
            for h in range(GLA_H):
                hs = slice(h * GLA_DV, (h + 1) * GLA_DV)
                oh = o[:, hs]
                ro = lax.rsqrt(jnp.mean(oh * oh, axis=-1, keepdims=True) + RMS_EPS)
                g = vg_ref[rs, GLA_V + h * GLA_DV:GLA_V + (h + 1) * GLA_DV].astype(F32)
                yg_ref[rs, hs] = (oh * ro * ng_ref[...] * g * _sigmoid(g)).astype(BF16)
            kv = _dot(vb, kh, _TN)
            decay = jnp.exp(bl)
            for p in range(PAIRS):
                state_ref[p] = state[p] * decay[:, 128 * p:128 * (p + 1)] + _compact_state(kv, p)
            return carry

        lax.fori_loop(0, sc, chunk, 0, unroll=True)

    sg = lambda w: pl.BlockSpec((seg, w), lambda b, s: (b * n_seg + s, 0))
    return pl.pallas_call(
        body, name="fwd_gla", grid=(n_ex, n_seg),
        in_specs=[sg(2 * GLA_K), sg(2 * GLA_V), sg(RANK_P), _fixed((RANK_P, GLA_K)), _fixed((1, GLA_K)), _fixed((1, GLA_DV)),
                  _fixed((8, 128))],
        out_specs=[sg(GLA_V), sg(GLA_V), pl.BlockSpec((sc, PAIRS, GLA_DV, 128), lambda b, s: (b * n_seg + s, 0, 0, 0))],
        out_shape=[jax.ShapeDtypeStruct((rows, GLA_V), BF16), jax.ShapeDtypeStruct((rows, GLA_V), F32),
                   jax.ShapeDtypeStruct((n_ex * n_chunk, PAIRS, GLA_DV, 128), F32)],
        scratch_shapes=[pltpu.VMEM((PAIRS, GLA_DV, 128), F32)],
        compiler_params=_params(("parallel", "arbitrary")),
    )(qk, vg, lr, w2p, gb, ng, token)


def _bwd_gla(qk, vg, lr, o, st, dyg, w2p, gb, ng, yc, yg, dh1b, token, n_ex):
    rows = qk.shape[0]
    lp = rows // n_ex
    n_chunk = lp // CHUNK
    sc = _seg_chunks(n_chunk)
    n_seg = n_chunk // sc
    seg = sc * CHUNK

    def body(qk_ref, vg_ref, lr_ref, o_ref, st_ref, dyg_ref, w2_ref, gb_ref, ng_ref, yc_ref, yg_ref, dh1_ref, token_ref,
             dqk_ref, dvg_ref, dlr_ref, dw2_ref, dvec_ref, dwo_ref, gt_ref, dz_ref, dwo_acc):
        step = pl.program_id(1)
        sidx = n_seg - 1 - step
        first = (step == 0) & (pl.program_id(0) == 0)

        @pl.when(step == 0)
        def _():
            gt_ref[...] = jnp.zeros_like(gt_ref)

        @pl.when(first)
        def _():
            dw2_ref[...] = jnp.zeros_like(dw2_ref)
            dvec_ref[...] = jnp.zeros_like(dvec_ref)
            dwo_acc[...] = jnp.zeros_like(dwo_acc)

        d1 = dh1_ref[...]
        dwo_acc[0:C_CONV, :] += _dot(yc_ref[...], d1, _TN)
        dwo_acc[C_CONV:D, :] += _dot(yg_ref[...], d1, _TN)

        @pl.when((step == n_seg - 1) & (pl.program_id(0) == n_ex - 1))
        def _():
            dwo_ref[...] = dwo_acc[...].astype(BF16)

        causal = _causal_heads()
        k_mask = _block_mask((GLA_H * CHUNK, GLA_K), CHUNK, GLA_DK)
        v_mask = _block_mask((GLA_H * CHUNK, GLA_V), CHUNK, GLA_DV)
        last_row = lax.broadcasted_iota(jnp.int32, (CHUNK, 1), 0) == CHUNK - 1
        ng = ng_ref[...]

        def chunk(ii, dng):
            ci = sc - 1 - ii
            base = pl.multiple_of(ci * CHUNK, CHUNK)
            rs = pl.ds(base, CHUNK)
            z, live, bcum = _gate_terms(lr_ref, w2_ref, gb_ref, rs, (sidx * sc + ci) * CHUNK)
            bl = bcum[CHUNK - 1:CHUNK, :]
            ebl = jnp.exp(bl)
            q = qk_ref[rs, 0:GLA_K].astype(F32)
            k = qk_ref[rs, GLA_K:2 * GLA_K].astype(F32)
            eb = jnp.exp(bcum)
            enb = jnp.exp(-bcum)
            ehb = jnp.exp(bl - bcum)
            qt = q * (GLA_DK ** -0.5) * eb
            kt = k * enb
            kh = k * ehb
            qtb = qt.astype(BF16)
            vb = vg_ref[rs, 0:GLA_V].astype(BF16)
            k_rows = _per_head_rows(kt.astype(BF16), k_mask)
            v_rows = _per_head_rows(vb, v_mask)
            gt = [gt_ref[p] for p in range(PAIRS)]
            gtb = _expand_state([g_.astype(BF16) for g_ in gt])
            s_in = [st_ref[ci, p] for p in range(PAIRS)]
            dos = []
            for h in range(GLA_H):
                hs = slice(h * GLA_DV, (h + 1) * GLA_DV)
                gs = slice(GLA_V + h * GLA_DV, GLA_V + (h + 1) * GLA_DV)
                oh = o_ref[rs, hs]
                ro = lax.rsqrt(jnp.mean(oh * oh, axis=-1, keepdims=True) + RMS_EPS)
                on = oh * ro
                g = vg_ref[rs, gs].astype(F32)
                sg = _sigmoid(g)
                dout = dyg_ref[rs, hs]
                dvg_ref[rs, gs] = (dout * on * ng * (sg * (1.0 + g * (1.0 - sg)))).astype(BF16)
                dw = dout * g * sg
                dng = dng + jnp.sum(dw * on, axis=0, keepdims=True)
                don = dw * ng
                dos.append((ro * (don - on * jnp.mean(don * on, axis=-1, keepdims=True))).astype(BF16))
            dob = jnp.concatenate(dos, axis=1)
            a = jnp.where(causal, _dot(qtb, k_rows, _NT), 0.0).astype(BF16)
            da = jnp.where(causal, _dot(dob, v_rows, _NT), 0.0).astype(BF16)
            dv = _fold_heads(_dot(a, dob, _TN), GLA_DV) + _dot(kh.astype(BF16), gtb, _NT)
            dvg_ref[rs, 0:GLA_V] = dv.astype(BF16)
            dkh = _dot(vb, gtb)
            dqt = _dot(da, k_rows) + _dot(dob, _expand_state([s_.astype(BF16) for s_ in s_in]))
            dkt = _fold_heads(_dot(da, qtb, _TN), GLA_DK)
            dbl = jnp.concatenate([jnp.sum(gt[p] * s_in[p], axis=0, keepdims=True) for p in range(PAIRS)], axis=1) * ebl
            dbl = dbl + jnp.sum(dkh * kh, axis=0, keepdims=True)
            dqk_ref[rs, 0:GLA_K] = (dqt * (GLA_DK ** -0.5) * eb).astype(BF16)
            dqk_ref[rs, GLA_K:2 * GLA_K] = (dkt * enb + dkh * ehb).astype(BF16)
            db = dqt * qt - dkt * kt - dkh * kh
            db = jnp.where(last_row, db + dbl, db)
            dla = jnp.where(live, _rev_cumsum_rows(db), 0.0)
            dz_ref[rs, :] = dla * (1.0 / TAU) * (1.0 - _sigmoid(z))
            dstate = _dot(dob, qtb, _TN)
            for p in range(PAIRS):
                gt_ref[p] = _compact_state(dstate, p) + gt[p] * ebl[:, 128 * p:128 * (p + 1)]
            return dng

        dng = lax.fori_loop(0, sc, chunk, jnp.zeros((1, GLA_DV), F32), unroll=True)
        dz = dz_ref[...]
        dzb = dz.astype(BF16)
        dlr_ref[...] = _dot(dzb, w2_ref[...], _NT).astype(BF16)
        dw2_ref[...] += _dot(lr_ref[...].astype(BF16), dzb, _TN)
        dvec_ref[0:1, :] += jnp.sum(dz, axis=0, keepdims=True)
        dvec_ref[1:2, 0:GLA_DV] += dng

    sg_ = lambda w: pl.BlockSpec((seg, w), lambda b, s: (b * n_seg + n_seg - 1 - s, 0))
    return pl.pallas_call(
        body, name="bwd_gla", grid=(n_ex, n_seg),
        in_specs=[sg_(2 * GLA_K), sg_(2 * GLA_V), sg_(RANK_P), sg_(GLA_V),
                  pl.BlockSpec((sc, PAIRS, GLA_DV, 128), lambda b, s: (b * n_seg + n_seg - 1 - s, 0, 0, 0)), sg_(GLA_V),
                  _fixed((RANK_P, GLA_K)), _fixed((1, GLA_K)), _fixed((1, GLA_DV)), sg_(C_CONV), sg_(GLA_V), sg_(D),
                  _fixed((8, 128))],
        out_specs=[sg_(2 * GLA_K), sg_(2 * GLA_V), sg_(RANK_P), _fixed((RANK_P, GLA_K)), _fixed((8, GLA_K)),
                   _fixed((D, D))],
        out_shape=[jax.ShapeDtypeStruct((rows, 2 * GLA_K), BF16), jax.ShapeDtypeStruct((rows, 2 * GLA_V), BF16),
                   jax.ShapeDtypeStruct((rows, RANK_P), BF16), jax.ShapeDtypeStruct((RANK_P, GLA_K), F32),
                   jax.ShapeDtypeStruct((8, GLA_K), F32), jax.ShapeDtypeStruct((D, D), BF16)],
        scratch_shapes=[pltpu.VMEM((PAIRS, GLA_DV, 128), F32), pltpu.VMEM((seg, GLA_K), F32), pltpu.VMEM((D, D), F32)],
        compiler_params=_params(("arbitrary", "arbitrary")),
    )(qk, vg, lr, o, st, dyg, w2p, gb, ng, yc, yg, dh1b, token)


def _pad_rows(x, tgt):
    return jnp.pad(x, ((0, 0), (LEAD, 0), (0, 0))), jnp.pad(tgt, ((0, 0), (LEAD, 0), (0, 0)))


def _local_step(h0, tgt_p, p, pass_on, late_weights, send_early):
    n_ex, lp, _ = h0.shape
    rows = n_ex * lp
    meta = jnp.broadcast_to(p["meta"][None], (n_ex, N_META, D))
    h0 = lax.dynamic_update_slice(h0, meta, (0, ZROWS, 0)).reshape(rows, D)
    tgt_p = tgt_p.reshape(rows, D)

    uc, qk, vg, lr, n1 = _fwd_inproj(h0, p["g1"], p["w_in"])
    ypre, yc = _fwd_conv(uc, p["conv_w"], p["conv_b"], p["ln_g"], p["ln_b"], p["token"], n_ex)
    yg, o, st = _fwd_gla(qk, vg, lr, p["w2"], p["gb"], p["ng"], p["token"], n_ex)
    token = pass_on((yc, yg))
    w_out, wg, wu, wd = late_weights(token)
    n2, f, da, db, dh2, dh1, dh1b, dyc, dyg, part = _mid_rows(
        yc, yg, h0, tgt_p, w_out, wg, wu, wd, p["g2"], p["g3"], token, lp)
    g = {}
    token = send_early("ffn", [_matmul_tn(a_, b_, name).reshape(N_DEV, FF_S, D) for a_, b_, name in (
        (da, n2, "dw_gate"), (db, n2, "dw_up"), (f, dh2, "dw_down"))])
    dqk, dvg, dlr, g["w2"], g["gla_vec"], dw_out = _bwd_gla(
        qk, vg, lr, o, st, dyg, p["w2"], p["gb"], p["ng"], yc, yg, dh1b, token, n_ex)
    token = send_early("out", [dw_out.reshape(N_DEV, W_OUT_S, D)])
    duc, g["conv_w"], g["conv_vec"] = _bwd_conv(uc, ypre, dyc, p["conv_w"], p["ln_g"], p["ln_b"], token, n_ex)
    token = send_early("in", [_dw_blocked(n1, [duc, dqk, dvg, dlr], W_IN_S, "dw_in")])
    grad_x, g["in_vec"], g["meta"] = _bwd_inproj(duc, dqk, dvg, dlr, dh1, h0, p["w_in"], p["g1"], token, lp)
    g["ffn_vec"] = part
    return grad_x, g


W_IN_S = D_IN // N_DEV
W_OUT_S = D // N_DEV
FF_S = D_FF // N_DEV
CONV_S = C_CONV // N_DEV
GATE_S = GLA_K // N_DEV
SMALL_PACK = 64
CONV_ROW = 16
GATE_ROW = 48
VEC_ROWS = 16
_VEC_ROWS = (("norm_mix_g", D), ("conv_b", C_CONV), ("conv_ln_g", C_CONV), ("conv_ln_b", C_CONV), ("gla_gate_b", GLA_K),
             ("gla_norm_g", GLA_DV), ("norm_ffn_g", D), ("norm_final_g", D))
LOSS_ROW = len(_VEC_ROWS)


def _position():
    return lax.axis_index("x"), lax.axis_index("y"), lax.axis_index("c")


def _any():
    return pl.BlockSpec(memory_space=pl.ANY)


def _stage(mats, meta, conv_w, w2):
    n_t = len(mats) + 1

    def body(*refs):
        ins = refs[0:n_t - 1]
        meta_ref, cw_ref, w2_ref = refs[n_t - 1:n_t + 2]
        lands = refs[n_t + 2:2 * n_t + 2]
        shards = refs[2 * n_t + 2:3 * n_t + 2]
        sems = refs[3 * n_t + 2]
        for s_ref, w_ref in zip(shards, ins):
            s_ref[...] = w_ref[...].astype(BF16)
        sp = shards[n_t - 1]
        sp[...] = jnp.zeros_like(sp)
        sp[0:N_META, :] = meta_ref[...]
        sp[CONV_ROW:CONV_ROW + CONV_W, 0:CONV_S] = cw_ref[...]
        sp[GATE_ROW:GATE_ROW + RANK, 0:GATE_S] = w2_ref[...]
        x, y, c = _position()
        mine = [pltpu.make_async_copy(shards[t], lands[t].at[4 * x + 2 * y + c], sems.at[t]) for t in range(n_t)]
        for cp in mine:
            cp.start()
        for cp in mine:
            cp.wait()

    shard_shapes = [jax.ShapeDtypeStruct(m.shape, BF16) for m in mats] + [jax.ShapeDtypeStruct((SMALL_PACK, 128), F32)]
    res = pl.pallas_call(
        body, name="stage",
        out_shape=[jax.ShapeDtypeStruct((N_DEV,) + s.shape, s.dtype) for s in shard_shapes] + shard_shapes,
        in_specs=[_whole_vmem()] * (n_t + 2), out_specs=[_any()] * n_t + [_whole_vmem()] * n_t,
        scratch_shapes=[pltpu.SemaphoreType.DMA((n_t,))],
        compiler_params=pltpu.CompilerParams(vmem_limit_bytes=VMEM_LIMIT),
    )(*mats, meta, conv_w, w2)
    return res[0:n_t], res[n_t:]


_HBM = pl.BlockSpec(memory_space=pltpu.HBM)
_SEM = pl.BlockSpec(memory_space=pltpu.SEMAPHORE)
_EFFECT = pltpu.SideEffectType.DATAFLOW_SIDE_EFFECTING


_N_ROUTES = {"scatter": 7, "first": 4, "forward": 3}


def _routes(mode):
    x, y, c = _position()
    me = 4 * x + 2 * y + c
    if mode == "scatter":
        out = []
        for k in range(1, N_DEV):
            px = 1 - x if k & 4 else x
            py = 1 - y if k & 2 else y
            pc = 1 - c if k & 1 else c
            out.append(((px, py, pc), 4 * px + 2 * py + pc, me))
        return out
    if mode == "first":
        return [(pos, None, me) for pos in ((x, y, 1 - c), (1 - x, y, c), (x, 1 - y, c), (1 - x, 1 - y, c))]
    assert mode == "forward"
    return [((x, y, 1 - c), 4 * px + 2 * py + c, 4 * px + 2 * py + c) for px, py in ((1 - x, y), (x, 1 - y), (1 - x, 1 - y))]


def _route_copies(mode, n, src_refs, land_refs, send_sems, recv_sems):
    nr = _N_ROUTES[mode]
    for i, (pos, src_blk, dst_blk) in enumerate(_routes(mode)):
        for t in range(n):
            src = land_refs[t] if mode == "forward" else src_refs[t]
            yield pltpu.make_async_remote_copy(
                src_ref=src if src_blk is None else src.at[src_blk], dst_ref=land_refs[t].at[dst_blk],
                send_sem=send_sems.at[nr * t + i], recv_sem=recv_sems.at[nr * t + i], device_id=pos, device_id_type=MESH)


def _in_hbm(a):
    return pltpu.with_memory_space_constraint(a, pltpu.HBM)


def _send_start(name, groups, mode, after):
    sizes = [(len(s), len(l)) for s, l in groups]
    bufs = [b for s, l in groups for b in list(s) + list(l)]
    nb, ng = len(bufs), len(groups)

    def body(*refs):
        sems = refs[nb + 1:nb + 1 + 2 * ng]
        token = refs[2 * nb + 2 * ng + 1]
        off = 0
        for gi, (ns, n) in enumerate(sizes):
            for cp in _route_copies(mode, n, refs[off:off + ns], refs[off + ns:off + ns + n], sems[2 * gi], sems[2 * gi + 1]):
                cp.start()
            off += ns + n
        token[...] = jnp.zeros_like(token)

    res = pl.pallas_call(
        body, name=name,
        out_shape=(*[pltpu.SemaphoreType.DMA((_N_ROUTES[mode] * n,)) for _, n in sizes for _ in range(2)],
                   *[pltpu.HBM(b.shape, b.dtype) for b in bufs], jax.ShapeDtypeStruct((8, 128), F32)),
        in_specs=[_HBM] * nb + [_any()], out_specs=(*[_SEM] * (2 * ng), *[_HBM] * nb, _whole_vmem()),
        input_output_aliases={i: 2 * ng + i for i in range(nb)},
        compiler_params=pltpu.CompilerParams(has_side_effects=_EFFECT),
    )(*[_in_hbm(b) for b in bufs], after)
    handles, off = [], 2 * ng
    for gi, (ns, n) in enumerate(sizes):
        handles.append((res[2 * gi], res[2 * gi + 1], res[off:off + ns], res[off + ns:off + ns + n]))
        off += ns + n
    return handles, res[2 * ng + nb]


def _send_wait(name, send_sems, recv_sems, srcs, lands, mode, after):
    n, ns = len(lands), len(srcs)
    after = after if isinstance(after, tuple) else (after,)

    def body(*refs):
        src_refs, land_refs = refs[0:ns], refs[ns:ns + n]
        send_sems, recv_sems = refs[ns + n:ns + n + 2]
        for cp in _route_copies(mode, n, src_refs, land_refs, send_sems, recv_sems):
            cp.wait_send()
            cp.wait_recv()

    bufs = list(srcs) + list(lands)
    res = pl.pallas_call(
        body, name=name,
        out_shape=tuple(pltpu.HBM(b.shape, b.dtype) for b in bufs),
        in_specs=[_HBM] * len(bufs) + [_SEM, _SEM] + [_any()] * len(after), out_specs=tuple([_HBM] * len(bufs)),
        input_output_aliases={i: i for i in range(len(bufs))},
        compiler_params=pltpu.CompilerParams(has_side_effects=_EFFECT),
    )(*bufs, send_sems, recv_sems, *after)
    return res[0:ns], res[ns:ns + n]


def _unshard_in(a_in, a_small, token):
    def body(a_ref, s_ref, token_ref, w_ref, meta_ref, cw_ref, w2_ref):
        w_ref[:, D_IN:D_INP] = jnp.zeros((D, D_INP - D_IN), BF16)
        w2_ref[...] = jnp.zeros_like(w2_ref)
        for d in range(N_DEV):
            w_ref[:, d * W_IN_S:(d + 1) * W_IN_S] = a_ref[d]
            meta_ref[:, d * 128:(d + 1) * 128] = s_ref[d, 0:N_META, :]
            cw_ref[:, d * CONV_S:(d + 1) * CONV_S] = s_ref[d, CONV_ROW:CONV_ROW + 32, 0:CONV_S]
            w2_ref[0:RANK, d * GATE_S:(d + 1) * GATE_S] = s_ref[d, GATE_ROW:GATE_ROW + RANK, 0:GATE_S].astype(BF16)

    return pl.pallas_call(
        body, name="unshard_in",
        out_shape=[jax.ShapeDtypeStruct((D, D_INP), BF16), jax.ShapeDtypeStruct((N_META, D), F32),
                   jax.ShapeDtypeStruct((32, C_CONV), F32), jax.ShapeDtypeStruct((RANK_P, GLA_K), BF16)],
        compiler_params=pltpu.CompilerParams(vmem_limit_bytes=VMEM_LIMIT),
    )(a_in, a_small, token)


def _pack_small(g):
    def body(meta_ref, cw_ref, w2_ref, in_vec, ffn_vec, conv_vec, gla_vec, sp, vp):
        sp[...] = jnp.zeros_like(sp)
        vp[...] = jnp.zeros_like(vp)
        for d in range(N_DEV):
            sp[d, 0:N_META, :] = meta_ref[:, d * 128:(d + 1) * 128]
            sp[d, CONV_ROW:CONV_ROW + 32, 0:CONV_S] = cw_ref[:, d * CONV_S:(d + 1) * CONV_S]
            sp[d, GATE_ROW:GATE_ROW + RANK, 0:GATE_S] = w2_ref[0:RANK, d * GATE_S:(d + 1) * GATE_S]
            vp[d, 0:1, :] = in_vec[0:1, :]
            vp[d, 1:4, 0:C_CONV] = conv_vec[0:3, :]
            vp[d, 4:5, 0:GLA_K] = gla_vec[0:1, :]
            vp[d, 5:6, 0:GLA_DV] = gla_vec[1:2, 0:GLA_DV]
            vp[d, 6:7, :] = ffn_vec[1:2, :]
            vp[d, 7:8, :] = ffn_vec[0:1, :]
            vp[d, LOSS_ROW:LOSS_ROW + 1, :] = ffn_vec[2:3, :]

    return pl.pallas_call(
        body, name="pack_small",
        out_shape=[jax.ShapeDtypeStruct((N_DEV, SMALL_PACK, 128), F32), jax.ShapeDtypeStruct((N_DEV, VEC_ROWS, D), F32)],
    )(g["meta"], g["conv_w"], g["w2"], g["in_vec"], g["ffn_vec"], g["conv_vec"], g["gla_vec"])


def _adamw(w, g, m, v):
    m = ADAM_B1 * m + (1.0 - ADAM_B1) * g
    v = ADAM_B2 * v + (1.0 - ADAM_B2) * (g * g)
    m_hat = m / (1.0 - ADAM_B1 ** ADAM_STEP)
    v_hat = v / (1.0 - ADAM_B2 ** ADAM_STEP)
    return -ADAM_LR * (m_hat / (jnp.sqrt(v_hat) + ADAM_EPS) + ADAM_WD * w), m, v


def _update_matrix(recv, own, me, w, m, v, name):
    _, r, c = recv.shape
    tr = _row_tile(r, 256)

    def body(me_ref, recv_ref, own_ref, w_ref, m_ref, v_ref, g_ref, d_ref, nm_ref, nv_ref):
        g = jnp.zeros((tr, c), F32)
        for s in range(N_DEV):
            g = g + jnp.where(me_ref[0] == s, own_ref[...], recv_ref[s]).astype(F32)
        g_ref[...] = g
        d_ref[...], nm_ref[...], nv_ref[...] = _adamw(w_ref[...], g, m_ref[...], v_ref[...])

    one = pl.BlockSpec((None, tr, c), lambda i, me_ref: (0, i, 0))
    return pl.pallas_call(
        body, name=name,
        grid_spec=pltpu.PrefetchScalarGridSpec(
            num_scalar_prefetch=1, grid=(r // tr,),
            in_specs=[pl.BlockSpec((N_DEV, tr, c), lambda i, me_ref: (0, i, 0)),
                      pl.BlockSpec((None, tr, c), lambda i, me_ref: (me_ref[0], i, 0)), one, one, one],
            out_specs=[one] * 4),
        out_shape=[jax.ShapeDtypeStruct((1, r, c), F32)] * 4,
        compiler_params=_params(("parallel",)),
    )(me, recv, own, w, m, v)


_SMALL = ("meta_tokens", "conv_w", "gla_w_gate2") + tuple(n for n, _ in _VEC_ROWS)


def _update_small(me, srecv, vrecv, sown, vown, w, m, v):
    n = len(_SMALL)

    def body(*refs):
        me_ref, s_ref, v_ref, so_ref, vo_ref = refs[0:5]
        w_refs, m_refs, v_refs = refs[5:5 + n], refs[5 + n:5 + 2 * n], refs[5 + 2 * n:5 + 3 * n]
        outs = refs[5 + 3 * n:]
        ssum = jnp.zeros((SMALL_PACK, 128), F32)
        vsum = jnp.zeros((VEC_ROWS, D), F32)
        for s in range(N_DEV):
            ssum = ssum + jnp.where(me_ref[0] == s, so_ref[s], s_ref[s])
            vsum = vsum + jnp.where(me_ref[0] == s, vo_ref[s], v_ref[s])
        grads = [ssum[0:N_META, :], ssum[CONV_ROW:CONV_ROW + CONV_W, 0:CONV_S], ssum[GATE_ROW:GATE_ROW + RANK, 0:GATE_S]]
        grads += [vsum[i:i + 1, 0:width] for i, (_, width) in enumerate(_VEC_ROWS)]
        for i, g in enumerate(grads):
            d, nm, nv = _adamw(w_refs[i][...], g, m_refs[i][...], v_refs[i][...])
            outs[i][...] = g
            outs[n + i][...] = d
            outs[2 * n + i][...] = nm
            outs[3 * n + i][...] = nv
        outs[4 * n][...] = vsum[LOSS_ROW:LOSS_ROW + 1, 0:128]

    shapes = [jax.ShapeDtypeStruct(t.shape, F32) for t in w]
    res = pl.pallas_call(
        body, name="update_small", out_shape=shapes * 4 + [jax.ShapeDtypeStruct((1, 128), F32)],
        in_specs=[pl.BlockSpec(memory_space=pltpu.SMEM)] + [_whole_vmem()] * (4 + 3 * n),
    )(me, srecv, vrecv, sown, vown, *w, *m, *v)
    return res[0:n], res[n:2 * n], res[2 * n:3 * n], res[3 * n:4 * n], res[4 * n]


_WEIGHTS = ("meta_tokens", "norm_mix_g", "w_in", "conv_w", "conv_b", "conv_ln_g", "conv_ln_b", "gla_w_gate2", "gla_gate_b",
            "gla_norm_g", "w_out", "norm_ffn_g", "w_ffn_gate", "w_ffn_up", "w_ffn_down", "norm_final_g")
_MATRICES = ("w_in", "w_out", "w_ffn_gate", "w_ffn_up", "w_ffn_down")
_TRANSPOSED = ("w_ffn_gate", "w_ffn_up")


def kernel(x, meta_tokens, norm_mix_g, w_in, conv_w, conv_b, conv_ln_g, conv_ln_b, gla_w_gate2, gla_gate_b, gla_norm_g, w_out, norm_ffn_g, w_ffn_gate, w_ffn_up, w_ffn_down, norm_final_g, loss_target, m_meta_tokens, m_norm_mix_g, m_w_in, m_conv_w, m_conv_b, m_conv_ln_g, m_conv_ln_b, m_gla_w_gate2, m_gla_gate_b, m_gla_norm_g, m_w_out, m_norm_ffn_g, m_w_ffn_gate, m_w_ffn_up, m_w_ffn_down, m_norm_final_g, v_meta_tokens, v_norm_mix_g, v_w_in, v_conv_w, v_conv_b, v_conv_ln_g, v_conv_ln_b, v_gla_w_gate2, v_gla_gate_b, v_gla_norm_g, v_w_out, v_norm_ffn_g, v_w_ffn_gate, v_w_ffn_up, v_w_ffn_down, v_norm_final_g):
    given = dict(locals())
    two_d = lambda a: a.reshape(1, -1) if a.ndim == 1 else a.reshape(a.shape[-2:])
    fams = [{n: given[pre + n] for n in _WEIGHTS} for pre in ("", "m_", "v_")]
    for f in fams:
        for n in _TRANSPOSED:
            f[n] = f[n].transpose(0, 2, 1)
    w = fams[0]

    lands, shards = _stage([two_d(w[n]) for n in _MATRICES], w["meta_tokens"], two_d(w["conv_w"]), two_d(w["gla_w_gate2"]))
    soon, later = (0, 5), (1, 2, 3, 4)
    pick = lambda seq, idx: [seq[i] for i in idx]
    (first, ffn_first), started = _send_start(
        "gather_first_start", [(pick(shards, soon), pick(lands, soon)), (pick(shards, later), pick(lands, later))],
        "first", norm_mix_g)
    h0, tgt_p = _pad_rows(x, loss_target)
    _, arrived = _send_wait("gather_first_wait", *first, "first", (h0, tgt_p, started))
    (forward,), token = _send_start("gather_forward_start", [([], arrived)], "forward", started)
    _, (a_in, a_small) = _send_wait("gather_forward_wait", *forward, "forward", token)
    w_in, meta, conv_taps, w2 = _unshard_in(a_in, a_small, token)
    p = dict(meta=meta, conv_w=conv_taps, w2=w2, w_in=w_in, g1=norm_mix_g, conv_b=conv_b, ln_g=conv_ln_g, ln_b=conv_ln_b,
             gb=gla_gate_b, ng=gla_norm_g, g2=norm_ffn_g, g3=two_d(norm_final_g), token=token)
    passed = {}

    def pass_on(after):
        _, arrived_ffn = _send_wait("gather_ffn_first_wait", *ffn_first, "first", after)
        (passed["sent"],), token = _send_start("gather_ffn_forward_start", [([], arrived_ffn)], "forward", norm_mix_g)
        return token

    def late_weights(after):
        _, (a_out, a_g, a_u, a_d) = _send_wait("gather_ffn_forward_wait", *passed["sent"], "forward", after)
        return a_out.reshape(D, D), a_g.reshape(D_FF, D), a_u.reshape(D_FF, D), a_d.reshape(D_FF, D)

    sent = {}

    def send_early(tag, mats):
        landing = [_in_hbm(lax.empty(m_.shape, m_.dtype)) for m_ in mats]
        (sent[tag],), token = _send_start("scatter_" + tag + "_start", [(mats, landing)], "scatter", norm_mix_g)
        return token

    grad_x, g = _local_step(h0, tgt_p, p, pass_on, late_weights, send_early)

    token = send_early("small", list(_pack_small(g)))
    x_, y_, c_ = _position()
    me = (4 * x_ + 2 * y_ + c_).astype(jnp.int32).reshape(1)
    res = {}
    for tag, names in (("ffn", ("w_ffn_gate", "w_ffn_up", "w_ffn_down")), ("out", ("w_out",)), ("in", ("w_in",))):
        own, recv = _send_wait("scatter_" + tag + "_wait", *sent[tag], "scatter", token)
        for n, o_, r_ in zip(names, own, recv):
            res[n] = _update_matrix(r_, o_, me, *[f[n] for f in fams], "update_" + n)
            token = res[n][1]
    (sown, vown), (srecv, vrecv) = _send_wait("scatter_small_wait", *sent["small"], "scatter", token)
    small = _update_small(me, srecv, vrecv, sown, vown, *[[two_d(f[n]) for n in _SMALL] for f in fams])
    for i, n in enumerate(_SMALL):
        res[n] = [fam[i].reshape(w[n].shape) for fam in small[0:4]]
    for n in _TRANSPOSED:
        res[n] = [t.transpose(0, 2, 1) for t in res[n]]
    outs = [small[4][0, 0], grad_x]
    for k in range(4):
        outs += [res[n][k] for n in _WEIGHTS]
    return tuple(outs)
```

```python
import functools

import jax
import jax.numpy as jnp
from jax import lax
from jax.experimental import pallas as pl
from jax.experimental.pallas import tpu as pltpu

F32 = jnp.float32
BF16 = jnp.bfloat16

D = 1024
N_META = 16
C_CONV = 512
CONV_W = 31
GLA_H = 4
GLA_DK = 64
GLA_DV = 128
GLA_K = GLA_H * GLA_DK
GLA_V = GLA_H * GLA_DV
RANK = 16
RANK_P = 128
TAU = 16.0
CHUNK = 64
LEAD = CHUNK
ZROWS = LEAD - N_META
D_IN = 2 * C_CONV + 2 * GLA_K + 2 * GLA_V + RANK
D_INP = D_IN - RANK + RANK_P
D_FF = 2816
FF_CHUNK = 1408
FF_SPLIT = (0, 1536, D_FF)
RMS_EPS = 1e-6
LN_EPS = 1e-5
N_DEV = 8

ADAM_LR = 0.001
ADAM_B1 = 0.9
ADAM_B2 = 0.999
ADAM_EPS = 1e-08
ADAM_WD = 0.01
ADAM_STEP = 10

VMEM_LIMIT = 60 * 1024 * 1024
ROW_TILE = 1056
FFN_ROW_TILE = 352
DW_ROW_TILE = 1408
MESH = pl.DeviceIdType.MESH

_NN = (((1,), (0,)), ((), ()))
_NT = (((1,), (1,)), ((), ()))
_TN = (((0,), (0,)), ((), ()))


def _dot(a, b, dims=_NN):
    return lax.dot_general(a, b, dims, preferred_element_type=F32)


def _sigmoid(x):
    return 1.0 / (1.0 + jnp.exp(-x))


def _row_tile(rows, target):
    best = None
    for t in range(16, min(rows, target) + 1, 16):
        if rows % t == 0:
            best = t
    assert best is not None, rows
    return best


def _params(sem=None):
    return pltpu.CompilerParams(dimension_semantics=sem, vmem_limit_bytes=VMEM_LIMIT)


def _whole_vmem():
    return pl.BlockSpec(memory_space=pltpu.VMEM)


def _rows(tm, width):
    return pl.BlockSpec((tm, width), lambda i: (i, 0))


def _fixed(shape):
    return pl.BlockSpec(shape, lambda *_: (0,) * len(shape))


def _fwd_inproj(h0, g1, w_in):
    rows = h0.shape[0]
    tm = _row_tile(rows, ROW_TILE)

    def body(h_ref, g_ref, w_ref, uc_ref, qk_ref, vg_ref, lr_ref, n1_ref):
        h = h_ref[...]
        r = lax.rsqrt(jnp.mean(h * h, axis=-1, keepdims=True) + RMS_EPS)
        n = (h * r * g_ref[...]).astype(BF16)
        n1_ref[...] = n
        uc_ref[...] = _dot(n, w_ref[:, 0:1024]).astype(BF16)
        qk_ref[...] = _dot(n, w_ref[:, 1024:1536]).astype(BF16)
        vg_ref[...] = _dot(n, w_ref[:, 1536:2560]).astype(BF16)
        lr_ref[...] = _dot(n, w_ref[:, 2560:2688]).astype(BF16)

    return pl.pallas_call(
        body, name="fwd_inproj", grid=(rows // tm,),
        in_specs=[_rows(tm, D), _fixed((1, D)), _whole_vmem()],
        out_specs=[_rows(tm, 1024), _rows(tm, 512), _rows(tm, 1024), _rows(tm, RANK_P), _rows(tm, D)],
        out_shape=[jax.ShapeDtypeStruct((rows, 1024), BF16), jax.ShapeDtypeStruct((rows, 512), BF16),
                   jax.ShapeDtypeStruct((rows, 1024), BF16), jax.ShapeDtypeStruct((rows, RANK_P), BF16),
                   jax.ShapeDtypeStruct((rows, D), BF16)],
        compiler_params=_params(("parallel",)),
    )(h0, g1, w_in)


def _mid_rows(yc, yg, h0, tgt, w_out, wg, wu, wd, g2, g3, token, rows_per_example):
    rows = h0.shape[0]
    tm = _row_tile(rows, FFN_ROW_TILE)
    ff_blocks = [slice(lo, hi) for lo, hi in zip(FF_SPLIT[:-1], FF_SPLIT[1:])]

    def body(yc_ref, yg_ref, h0_ref, t_ref, wo_ref, wg_ref, wu_ref, wd_ref, g2_ref, g3_ref, token_ref,
             n2_ref, f_ref, da_ref, db_ref, dh2_ref, dh1_ref, dh1b_ref, dyc_ref, dyg_ref, part_ref):
        i = pl.program_id(0)
        h1 = h0_ref[...] + _dot(yc_ref[...], wo_ref[0:C_CONV, :]) + _dot(yg_ref[...], wo_ref[C_CONV:D, :])
        r2 = lax.rsqrt(jnp.mean(h1 * h1, axis=-1, keepdims=True) + RMS_EPS)
        xh2 = h1 * r2
        n2 = (xh2 * g2_ref[...]).astype(BF16)
        n2_ref[...] = n2
        y2 = jnp.zeros((tm, D), F32)
        for cs in ff_blocks:
            a = _dot(n2, wg_ref[cs, :], _NT)
            b = _dot(n2, wu_ref[cs, :], _NT)
            f = (a * _sigmoid(a) * b).astype(BF16)
            f_ref[:, cs] = f
            da_ref[:, cs] = a.astype(BF16)
            db_ref[:, cs] = b.astype(BF16)
            y2 = y2 + _dot(f, wd_ref[cs, :])
        h2 = h1 + y2
        r3 = lax.rsqrt(jnp.mean(h2 * h2, axis=-1, keepdims=True) + RMS_EPS)
        xh3 = h2 * r3
        g3 = g3_ref[...]
        pos = (i * tm + lax.broadcasted_iota(jnp.int32, (tm, 1), 0)) % rows_per_example
        valid = pos >= LEAD
        err = jnp.where(valid, xh3 * g3 - t_ref[...], 0.0)
        loss = 0.5 / D * jnp.sum(jnp.sum(err * err, axis=-1, keepdims=True), axis=0, keepdims=True)
        dy = err * (1.0 / D)
        dg3 = jnp.sum(dy * xh3, axis=0, keepdims=True)
        dxh = dy * g3
        dh2 = r3 * (dxh - xh3 * jnp.mean(dxh * xh3, axis=-1, keepdims=True))
        dh2b = dh2.astype(BF16)
        dh2_ref[...] = dh2b
        dn2 = jnp.zeros((tm, D), F32)
        for cs in ff_blocks:
            df = _dot(dh2b, wd_ref[cs, :], _NT)
            a = da_ref[:, cs].astype(F32)
            b = db_ref[:, cs].astype(F32)
            sg = _sigmoid(a)
            da = (df * b * sg * (1.0 + a * (1.0 - sg))).astype(BF16)
            db = (df * a * sg).astype(BF16)
            da_ref[:, cs] = da
            db_ref[:, cs] = db
            dn2 = dn2 + _dot(da, wg_ref[cs, :]) + _dot(db, wu_ref[cs, :])
        dg2 = jnp.sum(dn2 * xh2, axis=0, keepdims=True)
        dxh2 = dn2 * g2_ref[...]
        dh1 = dh2 + r2 * (dxh2 - xh2 * jnp.mean(dxh2 * xh2, axis=-1, keepdims=True))
        dh1_ref[...] = dh1
        dh1b = dh1.astype(BF16)
        dh1b_ref[...] = dh1b
        dyc_ref[...] = _dot(dh1b, wo_ref[0:C_CONV, :], _NT)
        dyg_ref[...] = _dot(dh1b, wo_ref[C_CONV:D, :], _NT)

        @pl.when(i == 0)
        def _():
            part_ref[...] = jnp.zeros_like(part_ref)

        part_ref[0:1, :] += dg3
        part_ref[1:2, :] += dg2
        part_ref[2:3, :] += jnp.broadcast_to(loss, (1, D))

    return pl.pallas_call(
        body, name="mid_rows", grid=(rows // tm,),
        in_specs=[_rows(tm, C_CONV), _rows(tm, GLA_V), _rows(tm, D), _rows(tm, D), _whole_vmem(), _whole_vmem(),
                  _whole_vmem(), _whole_vmem(), _fixed((1, D)), _fixed((1, D)), _fixed((8, 128))],
        out_specs=[_rows(tm, D), _rows(tm, D_FF), _rows(tm, D_FF), _rows(tm, D_FF), _rows(tm, D), _rows(tm, D),
                   _rows(tm, D), _rows(tm, C_CONV), _rows(tm, GLA_V), _fixed((8, D))],
        out_shape=[jax.ShapeDtypeStruct((rows, D), BF16)] + [jax.ShapeDtypeStruct((rows, D_FF), BF16)] * 3
        + [jax.ShapeDtypeStruct((rows, D), BF16), jax.ShapeDtypeStruct((rows, D), F32),
           jax.ShapeDtypeStruct((rows, D), BF16), jax.ShapeDtypeStruct((rows, C_CONV), F32),
           jax.ShapeDtypeStruct((rows, GLA_V), F32), jax.ShapeDtypeStruct((8, D), F32)],
        compiler_params=_params(("arbitrary",)),
    )(yc, yg, h0, tgt, w_out, wg, wu, wd, g2, g3, token)


def _bwd_inproj(duc, dqk, dvg, dlr, dh1, h0, w_in, g1, token, rows_per_example):
    rows = h0.shape[0]
    n_ex = rows // rows_per_example
    tm = _row_tile(rows_per_example, ROW_TILE)
    tiles_per_example = rows_per_example // tm
    n_steps = rows // tm

    def body(duc_ref, dqk_ref, dvg_ref, dlr_ref, dh1_ref, h_ref, w_ref, g_ref, token_ref, gx_ref, part_ref, dmeta_ref,
             buf_ref, sems):
        dn = (_dot(duc_ref[...], w_ref[:, 0:1024], _NT) + _dot(dqk_ref[...], w_ref[:, 1024:1536], _NT)
              + _dot(dvg_ref[...], w_ref[:, 1536:2560], _NT) + _dot(dlr_ref[...], w_ref[:, 2560:2688], _NT))
        h = h_ref[...]
        r = lax.rsqrt(jnp.mean(h * h, axis=-1, keepdims=True) + RMS_EPS)
        xh = h * r
        dg = jnp.sum(dn * xh, axis=0, keepdims=True)
        dxh = dn * g_ref[...]
        dh0 = dh1_ref[...] + r * (dxh - xh * jnp.mean(dxh * xh, axis=-1, keepdims=True))
        i = pl.program_id(0)

        def copies(step):
            slot, b, j = step % 2, step // tiles_per_example, step % tiles_per_example
            out = [(j == 0, pltpu.make_async_copy(buf_ref.at[slot, pl.ds(LEAD, tm - LEAD)],
                                                   gx_ref.at[b, pl.ds(0, tm - LEAD)], sems.at[slot]))]
            if tiles_per_example > 1:
                out.append((j != 0, pltpu.make_async_copy(
                    buf_ref.at[slot], gx_ref.at[b, pl.ds(pl.multiple_of(jnp.maximum(j * tm - LEAD, 0), 8), tm)],
                    sems.at[slot])))
            return out

        def each(step, act):
            for cond, cp in copies(step):
                pl.when(cond)(functools.partial(act, cp))

        @pl.when(i >= 2)
        def _():
            each(i - 2, lambda cp: cp.wait())

        buf_ref[i % 2] = dh0
        each(i, lambda cp: cp.start())

        @pl.when(i == n_steps - 1)
        def _():
            each(i, lambda cp: cp.wait())
            if n_steps > 1:
                each(i - 1, lambda cp: cp.wait())

        @pl.when(i == 0)
        def _():
            part_ref[...] = jnp.zeros_like(part_ref)
            dmeta_ref[...] = jnp.zeros_like(dmeta_ref)

        part_ref[0:1, :] += dg

        @pl.when(i % tiles_per_example == 0)
        def _():
            dmeta_ref[...] += dh0[ZROWS:LEAD, :]

    return pl.pallas_call(
        body, name="bwd_inproj", grid=(n_steps,),
        in_specs=[_rows(tm, 1024), _rows(tm, 512), _rows(tm, 1024), _rows(tm, RANK_P), _rows(tm, D), _rows(tm, D),
                  _whole_vmem(), _fixed((1, D)), _fixed((8, 128))],
        out_specs=[_any(), _fixed((8, D)), _fixed((N_META, D))],
        out_shape=[jax.ShapeDtypeStruct((n_ex, rows_per_example - LEAD, D), F32), jax.ShapeDtypeStruct((8, D), F32),
                   jax.ShapeDtypeStruct((N_META, D), F32)],
        scratch_shapes=[pltpu.VMEM((2, tm, D), F32), pltpu.SemaphoreType.DMA((2,))],
        compiler_params=_params(("arbitrary",)),
    )(duc, dqk, dvg, dlr, dh1, h0, w_in, g1, token)


def _dw_blocked(a, bs, width, name):
    rows, m = a.shape
    ws = [b.shape[1] for b in bs]
    assert sum(ws) >= N_DEV * width
    tk = _row_tile(rows, DW_ROW_TILE)
    nk = rows // tk

    def body(a_ref, *refs):
        b_refs, o_ref, acc_ref = refs[:len(bs)], refs[len(bs)], refs[len(bs) + 1]
        k = pl.program_id(0)

        @pl.when(k == 0)
        def _():
            acc_ref[...] = jnp.zeros_like(acc_ref)

        at = a_ref[...].T
        off = 0
        for b_ref, w in zip(b_refs, ws):
            acc_ref[:, off:off + w] += _dot(at, b_ref[...])
            off += w

        @pl.when(k == nk - 1)
        def _():
            for d in range(N_DEV):
                o_ref[d] = acc_ref[:, d * width:(d + 1) * width].astype(BF16)

    return pl.pallas_call(
        body, name=name, grid=(nk,),
        in_specs=[_rows(tk, m)] + [_rows(tk, w) for w in ws],
        out_specs=_fixed((N_DEV, m, width)),
        out_shape=jax.ShapeDtypeStruct((N_DEV, m, width), BF16),
        scratch_shapes=[pltpu.VMEM((m, sum(ws)), F32)],
        compiler_params=_params(("arbitrary",)),
    )(a, *bs)


def _matmul_tn(a, b, name):
    rows, m = a.shape
    n = b.shape[1]
    tk = _row_tile(rows, DW_ROW_TILE)
    tn = n if n <= 1024 else FF_CHUNK
    tm_ = m if m <= 1024 else FF_CHUNK
    assert n % tn == 0 and m % tm_ == 0
    nk = rows // tk

    def body(a_ref, b_ref, o_ref, acc_ref):
        k = pl.program_id(2)

        @pl.when(k == 0)
        def _():
            acc_ref[...] = jnp.zeros_like(acc_ref)

        acc_ref[...] += _dot(a_ref[...], b_ref[...], _TN)

        @pl.when(k == nk - 1)
        def _():
            o_ref[...] = acc_ref[...].astype(BF16)

    return pl.pallas_call(
        body, name=name, grid=(m // tm_, n // tn, nk),
        in_specs=[pl.BlockSpec((tk, tm_), lambda i, j, k: (k, i)), pl.BlockSpec((tk, tn), lambda i, j, k: (k, j))],
        out_specs=pl.BlockSpec((tm_, tn), lambda i, j, k: (i, j)),
        out_shape=jax.ShapeDtypeStruct((m, n), BF16),
        scratch_shapes=[pltpu.VMEM((tm_, tn), F32)],
        compiler_params=_params(("parallel", "parallel", "arbitrary")),
    )(a, b)


HALO = 32
LN_ROWS = 352
LANES = 128


def _shifted(win, offsets):
    for r in range(8):
        js = [j for j, k in enumerate(offsets) if k % 8 == r]
        if js:
            rolled = win if r == 0 else pltpu.roll(win, CHUNK + HALO - r, 0)
            for j in js:
                yield j, rolled[offsets[j] - r:offsets[j] - r + CHUNK]


def _glu_into(uc_ref, vs_ref, n_chunk):
    vs_ref[0:CHUNK, :] = jnp.zeros((CHUNK, C_CONV), F32)

    def glu(i, carry):
        base = pl.multiple_of(i * CHUNK, CHUNK)
        val = uc_ref[pl.ds(base, CHUNK), 0:C_CONV].astype(F32)
        gate = uc_ref[pl.ds(base, CHUNK), C_CONV:2 * C_CONV].astype(F32)
        vs_ref[pl.ds(base + CHUNK, CHUNK), :] = val * _sigmoid(gate)
        return carry

    lax.fori_loop(0, n_chunk, glu, 0, unroll=3)


def _fwd_conv(uc, conv_w, conv_b, ln_g, ln_b, token, n_ex):
    rows = uc.shape[0]
    lp = rows // n_ex
    n_chunk = lp // CHUNK

    def body(uc_ref, w_ref, b_ref, lg_ref, lb_ref, token_ref, ypre_ref, yc_ref, vs_ref):
        _glu_into(uc_ref, vs_ref, n_chunk)

        def conv(i, carry):
            base = pl.multiple_of(i * CHUNK, CHUNK)
            for lb in range(C_CONV // LANES):
                ls = slice(lb * LANES, (lb + 1) * LANES)
                win = vs_ref[pl.ds(base + CHUNK - HALO, CHUNK + HALO), ls]
                acc = jnp.broadcast_to(b_ref[:, ls], (CHUNK, LANES))
                for j, rows_j in _shifted(win, [HALO - (CONV_W - 1) + j for j in range(CONV_W)]):
                    acc = acc + w_ref[j:j + 1, ls] * rows_j
                ypre_ref[pl.ds(base, CHUNK), ls] = acc
            return carry

        lax.fori_loop(0, n_chunk, conv, 0, unroll=3)

        ln_rows = _row_tile(lp, LN_ROWS)

        def norm(i, carry):
            base = pl.multiple_of(i * ln_rows, 16)
            y = ypre_ref[pl.ds(base, ln_rows), :]
            mu = jnp.mean(y, axis=-1, keepdims=True)
            yc_ = y - mu
            rstd = lax.rsqrt(jnp.mean(yc_ * yc_, axis=-1, keepdims=True) + LN_EPS)
            s = yc_ * rstd * lg_ref[...] + lb_ref[...]
            yc_ref[pl.ds(base, ln_rows), :] = (s * _sigmoid(s)).astype(BF16)
            return carry

        lax.fori_loop(0, lp // ln_rows, norm, 0)

    ex = lambda w: pl.BlockSpec((lp, w), lambda b: (b, 0))
    return pl.pallas_call(
        body, name="fwd_conv", grid=(n_ex,),
        in_specs=[ex(2 * C_CONV), _fixed((32, C_CONV)), _fixed((1, C_CONV)), _fixed((1, C_CONV)), _fixed((1, C_CONV)),
                  _fixed((8, 128))],
        out_specs=[ex(C_CONV), ex(C_CONV)],
        out_shape=[jax.ShapeDtypeStruct((rows, C_CONV), F32), jax.ShapeDtypeStruct((rows, C_CONV), BF16)],
        scratch_shapes=[pltpu.VMEM((lp + CHUNK, C_CONV), F32)],
        compiler_params=_params(("parallel",)),
    )(uc, conv_w, conv_b, ln_g, ln_b, token)


def _bwd_conv(uc, ypre, dyc, conv_w, ln_g, ln_b, token, n_ex):
    rows = uc.shape[0]
    lp = rows // n_ex
    n_chunk = lp // CHUNK

    def body(uc_ref, ypre_ref, dyc_ref, w_ref, lg_ref, lb_ref, token_ref, duc_ref, dw_ref, dvec_ref, vs_ref, dys_ref,
             dwacc_ref):
        _glu_into(uc_ref, vs_ref, n_chunk)
        dys_ref[pl.ds(lp, CHUNK), :] = jnp.zeros((CHUNK, C_CONV), F32)
        dwacc_ref[...] = jnp.zeros_like(dwacc_ref)

        ln_rows = _row_tile(lp, LN_ROWS)

        def ln_bwd(i, carry):
            dcb, dlg, dlb = carry
            base = pl.multiple_of(i * ln_rows, 16)
            y = ypre_ref[pl.ds(base, ln_rows), :]
            mu = jnp.mean(y, axis=-1, keepdims=True)
            yc_ = y - mu
            rstd = lax.rsqrt(jnp.mean(yc_ * yc_, axis=-1, keepdims=True) + LN_EPS)
            xh = yc_ * rstd
            s = xh * lg_ref[...] + lb_ref[...]
            sg = _sigmoid(s)
            ds = dyc_ref[pl.ds(base, ln_rows), :] * (sg * (1.0 + s * (1.0 - sg)))
            dxh = ds * lg_ref[...]
            dy = rstd * (dxh - jnp.mean(dxh, axis=-1, keepdims=True) - xh * jnp.mean(dxh * xh, axis=-1, keepdims=True))
            dys_ref[pl.ds(base, ln_rows), :] = dy
            return (dcb + jnp.sum(dy, axis=0, keepdims=True), dlg + jnp.sum(ds * xh, axis=0, keepdims=True),
                    dlb + jnp.sum(ds, axis=0, keepdims=True))

        zero = jnp.zeros((1, C_CONV), F32)
        dcb, dlg, dlb = lax.fori_loop(0, lp // ln_rows, ln_bwd, (zero, zero, zero))

        @pl.when(pl.program_id(0) == 0)
        def _():
            dvec_ref[...] = jnp.zeros_like(dvec_ref)
            dw_ref[...] = jnp.zeros_like(dw_ref)

        dvec_ref[0:1, :] += dcb
        dvec_ref[1:2, :] += dlg
        dvec_ref[2:3, :] += dlb

        def taps(i, carry):
            base = pl.multiple_of(i * CHUNK, CHUNK)
            for lb in range(C_CONV // LANES):
                ls = slice(lb * LANES, (lb + 1) * LANES)
                dwin = dys_ref[pl.ds(base, CHUNK + HALO), ls]
                vwin = vs_ref[pl.ds(base + CHUNK - HALO, CHUNK + HALO), ls]
                dy = dwin[0:CHUNK]
                acc = jnp.zeros((CHUNK, LANES), F32)
                for j, rows_j in _shifted(dwin, [CONV_W - 1 - j for j in range(CONV_W)]):
                    acc = acc + w_ref[j:j + 1, ls] * rows_j
                for j, rows_j in _shifted(vwin, [HALO - (CONV_W - 1) + j for j in range(CONV_W)]):
                    dwacc_ref[8 * j:8 * j + 8, ls] += jnp.sum((dy * rows_j).reshape(CHUNK // 8, 8, LANES), axis=0)
                val = uc_ref[pl.ds(base, CHUNK), ls].astype(F32)
                gate = uc_ref[pl.ds(base, CHUNK), C_CONV + lb * LANES:C_CONV + (lb + 1) * LANES].astype(F32)
                sg = _sigmoid(gate)
                duc_ref[pl.ds(base, CHUNK), ls] = (acc * sg).astype(BF16)
                duc_ref[pl.ds(base, CHUNK), C_CONV + lb * LANES:C_CONV + (lb + 1) * LANES] = (
                    acc * val * sg * (1.0 - sg)).astype(BF16)
            return carry

        lax.fori_loop(0, n_chunk, taps, 0, unroll=3)
        for j in range(CONV_W):
            dw_ref[j:j + 1, :] += jnp.sum(dwacc_ref[8 * j:8 * j + 8, :], axis=0, keepdims=True)

    ex = lambda w: pl.BlockSpec((lp, w), lambda b: (b, 0))
    return pl.pallas_call(
        body, name="bwd_conv", grid=(n_ex,),
        in_specs=[ex(2 * C_CONV), ex(C_CONV), ex(C_CONV), _fixed((32, C_CONV)), _fixed((1, C_CONV)), _fixed((1, C_CONV)),
                  _fixed((8, 128))],
        out_specs=[ex(2 * C_CONV), _fixed((32, C_CONV)), _fixed((8, C_CONV))],
        out_shape=[jax.ShapeDtypeStruct((rows, 2 * C_CONV), BF16), jax.ShapeDtypeStruct((32, C_CONV), F32),
                   jax.ShapeDtypeStruct((8, C_CONV), F32)],
        scratch_shapes=[pltpu.VMEM((lp + CHUNK, C_CONV), F32), pltpu.VMEM((lp + CHUNK, C_CONV), F32),
                        pltpu.VMEM((8 * 32, C_CONV), F32)],
        compiler_params=_params(("arbitrary",)),
    )(uc, ypre, dyc, conv_w, ln_g, ln_b, token)


def _seg_chunks(n_chunk):
    return max(c for c in (11, 3, 1) if n_chunk % c == 0)


def _block_mask(shape, row_block, lane_block):
    return (lax.broadcasted_iota(jnp.int32, shape, 0) // row_block) == (lax.broadcasted_iota(jnp.int32, shape, 1) // lane_block)


def _per_head_rows(x, mask):
    return jnp.where(mask, jnp.concatenate([x] * GLA_H, axis=0), 0)


def _fold_heads(full, lane_block):
    lane = lax.broadcasted_iota(jnp.int32, (1, full.shape[1]), 1) // lane_block
    out = jnp.where(lane == 0, full[0:CHUNK], 0.0)
    for h in range(1, GLA_H):
        out = out + jnp.where(lane == h, full[h * CHUNK:(h + 1) * CHUNK], 0.0)
    return out


PAIRS = GLA_H // 2


def _expand_state(blocks):
    lane = lax.broadcasted_iota(jnp.int32, (GLA_DV, 128), 1) // GLA_DK
    zero = jnp.zeros_like(blocks[0])
    rows = []
    for h in range(GLA_H):
        p, hh = divmod(h, 2)
        mine = jnp.where(lane == hh, blocks[p], 0)
        rows.append(jnp.concatenate([mine if q == p else zero for q in range(PAIRS)], axis=1))
    return jnp.concatenate(rows, axis=0)


def _compact_state(full, p):
    lane = lax.broadcasted_iota(jnp.int32, (GLA_DV, 128), 1) // GLA_DK
    ls = slice(128 * p, 128 * (p + 1))
    return jnp.where(lane == 0, full[2 * p * GLA_DV:(2 * p + 1) * GLA_DV, ls], full[(2 * p + 1) * GLA_DV:(2 * p + 2) * GLA_DV, ls])


def _causal_heads():
    return (lax.broadcasted_iota(jnp.int32, (CHUNK, GLA_H * CHUNK), 1) % CHUNK) <= lax.broadcasted_iota(
        jnp.int32, (CHUNK, GLA_H * CHUNK), 0)


def _cumsum_rows(x):
    row = lax.broadcasted_iota(jnp.int32, x.shape, 0)
    s = 1
    while s < CHUNK:
        x = x + jnp.where(row >= s, pltpu.roll(x, s, 0), 0.0)
        s *= 2
    return x


def _rev_cumsum_rows(x):
    row = lax.broadcasted_iota(jnp.int32, x.shape, 0)
    s = 1
    while s < CHUNK:
        x = x + jnp.where(row < CHUNK - s, pltpu.roll(x, CHUNK - s, 0), 0.0)
        s *= 2
    return x


def _gate_terms(lr_ref, w2_ref, gb_ref, rs, first_pos):
    z = _dot(lr_ref[rs, :].astype(BF16), w2_ref[...]) + gb_ref[...]
    la = (jnp.minimum(z, 0.0) - jnp.log(1.0 + jnp.exp(-jnp.abs(z)))) * (1.0 / TAU)
    pos = first_pos + lax.broadcasted_iota(jnp.int32, (CHUNK, 1), 0)
    live = pos >= ZROWS
    la = jnp.where(live, la, 0.0)
    return z, live, _cumsum_rows(la)


def _fwd_gla(qk, vg, lr, w2p, gb, ng, token, n_ex):
    rows = qk.shape[0]
    lp = rows // n_ex
    n_chunk = lp // CHUNK
    sc = _seg_chunks(n_chunk)
    n_seg = n_chunk // sc
    seg = sc * CHUNK

    def body(qk_ref, vg_ref, lr_ref, w2_ref, gb_ref, ng_ref, token_ref, yg_ref, o_ref, st_ref, state_ref):
        sidx = pl.program_id(1)

        @pl.when(sidx == 0)
        def _():
            state_ref[...] = jnp.zeros_like(state_ref)

        causal = _causal_heads()
        k_mask = _block_mask((GLA_H * CHUNK, GLA_K), CHUNK, GLA_DK)
        v_mask = _block_mask((GLA_H * CHUNK, GLA_V), CHUNK, GLA_DV)

        def chunk(ci, carry):
            base = pl.multiple_of(ci * CHUNK, CHUNK)
            rs = pl.ds(base, CHUNK)
            _, _, bcum = _gate_terms(lr_ref, w2_ref, gb_ref, rs, (sidx * sc + ci) * CHUNK)
            bl = bcum[CHUNK - 1:CHUNK, :]
            q = qk_ref[rs, 0:GLA_K].astype(F32)
            k = qk_ref[rs, GLA_K:2 * GLA_K].astype(F32)
            qt = (q * (GLA_DK ** -0.5) * jnp.exp(bcum)).astype(BF16)
            kt = (k * jnp.exp(-bcum)).astype(BF16)
            kh = (k * jnp.exp(bl - bcum)).astype(BF16)
            vb = vg_ref[rs, 0:GLA_V].astype(BF16)
            state = [state_ref[p] for p in range(PAIRS)]
            for p in range(PAIRS):
                st_ref[ci, p] = state[p]
            a = jnp.where(causal, _dot(qt, _per_head_rows(kt, k_mask), _NT), 0.0)
            o = _dot(a.astype(BF16), _per_head_rows(vb, v_mask)) + _dot(
                qt, _expand_state([s.astype(BF16) for s in state]), _NT)
            o_ref[rs, :] = o
            for h in range(GLA_H):
                hs = slice(h * GLA_DV, (h + 1) * GLA_DV)
                oh = o[:, hs]
                ro = lax.rsqrt(jnp.mean(oh * oh, axis=-1, keepdims=True) + RMS_EPS)
                g = vg_ref[rs, GLA_V + h * GLA_DV:GLA_V + (h + 1) * GLA_DV].astype(F32)
                yg_ref[rs, hs] = (oh * ro * ng_ref[...] * g * _sigmoid(g)).astype(BF16)
            kv = _dot(vb, kh, _TN)
            decay = jnp.exp(bl)
            for p in range(PAIRS):
                state_ref[p] = state[p] * decay[:, 128 * p:128 * (p + 1)] + _compact_state(kv, p)
            return carry

        lax.fori_loop(0, sc, chunk, 0, unroll=True)

    sg = lambda w: pl.BlockSpec((seg, w), lambda b, s: (b * n_seg + s, 0))
    return pl.pallas_call(
        body, name="fwd_gla", grid=(n_ex, n_seg),
        in_specs=[sg(2 * GLA_K), sg(2 * GLA_V), sg(RANK_P), _fixed((RANK_P, GLA_K)), _fixed((1, GLA_K)), _fixed((1, GLA_DV)),
                  _fixed((8, 128))],
        out_specs=[sg(GLA_V), sg(GLA_V), pl.BlockSpec((sc, PAIRS, GLA_DV, 128), lambda b, s: (b * n_seg + s, 0, 0, 0))],
        out_shape=[jax.ShapeDtypeStruct((rows, GLA_V), BF16), jax.ShapeDtypeStruct((rows, GLA_V), F32),
                   jax.ShapeDtypeStruct((n_ex * n_chunk, PAIRS, GLA_DV, 128), F32)],
        scratch_shapes=[pltpu.VMEM((PAIRS, GLA_DV, 128), F32)],
        compiler_params=_params(("parallel", "arbitrary")),
    )(qk, vg, lr, w2p, gb, ng, token)


def _bwd_gla(qk, vg, lr, o, st, dyg, w2p, gb, ng, yc, yg, dh1b, token, n_ex):
    rows = qk.shape[0]
    lp = rows // n_ex
    n_chunk = lp // CHUNK
    sc = _seg_chunks(n_chunk)
    n_seg = n_chunk // sc
    seg = sc * CHUNK

    def body(qk_ref, vg_ref, lr_ref, o_ref, st_ref, dyg_ref, w2_ref, gb_ref, ng_ref, yc_ref, yg_ref, dh1_ref, token_ref,
             dqk_ref, dvg_ref, dlr_ref, dw2_ref, dvec_ref, dwo_ref, gt_ref, dz_ref, dwo_acc):
        step = pl.program_id(1)
        sidx = n_seg - 1 - step
        first = (step == 0) & (pl.program_id(0) == 0)

        @pl.when(step == 0)
        def _():
            gt_ref[...] = jnp.zeros_like(gt_ref)

        @pl.when(first)
        def _():
            dw2_ref[...] = jnp.zeros_like(dw2_ref)
            dvec_ref[...] = jnp.zeros_like(dvec_ref)
            dwo_acc[...] = jnp.zeros_like(dwo_acc)

        d1 = dh1_ref[...]
        dwo_acc[0:C_CONV, :] += _dot(yc_ref[...], d1, _TN)
        dwo_acc[C_CONV:D, :] += _dot(yg_ref[...], d1, _TN)

        @pl.when((step == n_seg - 1) & (pl.program_id(0) == n_ex - 1))
        def _():
            dwo_ref[...] = dwo_acc[...].astype(BF16)

        causal = _causal_heads()
        k_mask = _block_mask((GLA_H * CHUNK, GLA_K), CHUNK, GLA_DK)
        v_mask = _block_mask((GLA_H * CHUNK, GLA_V), CHUNK, GLA_DV)
        last_row = lax.broadcasted_iota(jnp.int32, (CHUNK, 1), 0) == CHUNK - 1
        ng = ng_ref[...]

        def chunk(ii, dng):
            ci = sc - 1 - ii
            base = pl.multiple_of(ci * CHUNK, CHUNK)
            rs = pl.ds(base, CHUNK)
            z, live, bcum = _gate_terms(lr_ref, w2_ref, gb_ref, rs, (sidx * sc + ci) * CHUNK)
            bl = bcum[CHUNK - 1:CHUNK, :]
            ebl = jnp.exp(bl)
            q = qk_ref[rs, 0:GLA_K].astype(F32)
            k = qk_ref[rs, GLA_K:2 * GLA_K].astype(F32)
            eb = jnp.exp(bcum)
            enb = jnp.exp(-bcum)
            ehb = jnp.exp(bl - bcum)
            qt = q * (GLA_DK ** -0.5) * eb
            kt = k * enb
            kh = k * ehb
            qtb = qt.astype(BF16)
            vb = vg_ref[rs, 0:GLA_V].astype(BF16)
            k_rows = _per_head_rows(kt.astype(BF16), k_mask)
            v_rows = _per_head_rows(vb, v_mask)
            gt = [gt_ref[p] for p in range(PAIRS)]
            gtb = _expand_state([g_.astype(BF16) for g_ in gt])
            s_in = [st_ref[ci, p] for p in range(PAIRS)]
            dos = []
            for h in range(GLA_H):
                hs = slice(h * GLA_DV, (h + 1) * GLA_DV)
                gs = slice(GLA_V + h * GLA_DV, GLA_V + (h + 1) * GLA_DV)
                oh = o_ref[rs, hs]
                ro = lax.rsqrt(jnp.mean(oh * oh, axis=-1, keepdims=True) + RMS_EPS)
                on = oh * ro
                g = vg_ref[rs, gs].astype(F32)
                sg = _sigmoid(g)
                dout = dyg_ref[rs, hs]
                dvg_ref[rs, gs] = (dout * on * ng * (sg * (1.0 + g * (1.0 - sg)))).astype(BF16)
                dw = dout * g * sg
                dng = dng + jnp.sum(dw * on, axis=0, keepdims=True)
                don = dw * ng
                dos.append((ro * (don - on * jnp.mean(don * on, axis=-1, keepdims=True))).astype(BF16))
            dob = jnp.concatenate(dos, axis=1)
            a = jnp.where(causal, _dot(qtb, k_rows, _NT), 0.0).astype(BF16)
            da = jnp.where(causal, _dot(dob, v_rows, _NT), 0.0).astype(BF16)
            dv = _fold_heads(_dot(a, dob, _TN), GLA_DV) + _dot(kh.astype(BF16), gtb, _NT)
            dvg_ref[rs, 0:GLA_V] = dv.astype(BF16)
            dkh = _dot(vb, gtb)
            dqt = _dot(da, k_rows) + _dot(dob, _expand_state([s_.astype(BF16) for s_ in s_in]))
            dkt = _fold_heads(_dot(da, qtb, _TN), GLA_DK)
            dbl = jnp.concatenate([jnp.sum(gt[p] * s_in[p], axis=0, keepdims=True) for p in range(PAIRS)], axis=1) * ebl
            dbl = dbl + jnp.sum(dkh * kh, axis=0, keepdims=True)
            dqk_ref[rs, 0:GLA_K] = (dqt * (GLA_DK ** -0.5) * eb).astype(BF16)
            dqk_ref[rs, GLA_K:2 * GLA_K] = (dkt * enb + dkh * ehb).astype(BF16)
            db = dqt * qt - dkt * kt - dkh * kh
            db = jnp.where(last_row, db + dbl, db)
            dla = jnp.where(live, _rev_cumsum_rows(db), 0.0)
            dz_ref[rs, :] = dla * (1.0 / TAU) * (1.0 - _sigmoid(z))
            dstate = _dot(dob, qtb, _TN)
            for p in range(PAIRS):
                gt_ref[p] = _compact_state(dstate, p) + gt[p] * ebl[:, 128 * p:128 * (p + 1)]
            return dng

        dng = lax.fori_loop(0, sc, chunk, jnp.zeros((1, GLA_DV), F32), unroll=True)
        dz = dz_ref[...]
        dzb = dz.astype(BF16)
        dlr_ref[...] = _dot(dzb, w2_ref[...], _NT).astype(BF16)
        dw2_ref[...] += _dot(lr_ref[...].astype(BF16), dzb, _TN)
        dvec_ref[0:1, :] += jnp.sum(dz, axis=0, keepdims=True)
        dvec_ref[1:2, 0:GLA_DV] += dng

    sg_ = lambda w: pl.BlockSpec((seg, w), lambda b, s: (b * n_seg + n_seg - 1 - s, 0))
    return pl.pallas_call(
        body, name="bwd_gla", grid=(n_ex, n_seg),
        in_specs=[sg_(2 * GLA_K), sg_(2 * GLA_V), sg_(RANK_P), sg_(GLA_V),
                  pl.BlockSpec((sc, PAIRS, GLA_DV, 128), lambda b, s: (b * n_seg + n_seg - 1 - s, 0, 0, 0)), sg_(GLA_V),
                  _fixed((RANK_P, GLA_K)), _fixed((1, GLA_K)), _fixed((1, GLA_DV)), sg_(C_CONV), sg_(GLA_V), sg_(D),
                  _fixed((8, 128))],
        out_specs=[sg_(2 * GLA_K), sg_(2 * GLA_V), sg_(RANK_P), _fixed((RANK_P, GLA_K)), _fixed((8, GLA_K)),
                   _fixed((D, D))],
        out_shape=[jax.ShapeDtypeStruct((rows, 2 * GLA_K), BF16), jax.ShapeDtypeStruct((rows, 2 * GLA_V), BF16),
                   jax.ShapeDtypeStruct((rows, RANK_P), BF16), jax.ShapeDtypeStruct((RANK_P, GLA_K), F32),
                   jax.ShapeDtypeStruct((8, GLA_K), F32), jax.ShapeDtypeStruct((D, D), BF16)],
        scratch_shapes=[pltpu.VMEM((PAIRS, GLA_DV, 128), F32), pltpu.VMEM((seg, GLA_K), F32), pltpu.VMEM((D, D), F32)],
        compiler_params=_params(("arbitrary", "arbitrary")),
    )(qk, vg, lr, o, st, dyg, w2p, gb, ng, yc, yg, dh1b, token)


def _pad_rows(x, tgt):
    return jnp.pad(x, ((0, 0), (LEAD, 0), (0, 0))), jnp.pad(tgt, ((0, 0), (LEAD, 0), (0, 0)))


def _local_step(h0, tgt_p, p, pass_on, late_weights, send_early):
    n_ex, lp, _ = h0.shape
    rows = n_ex * lp
    meta = jnp.broadcast_to(p["meta"][None], (n_ex, N_META, D))
    h0 = lax.dynamic_update_slice(h0, meta, (0, ZROWS, 0)).reshape(rows, D)
    tgt_p = tgt_p.reshape(rows, D)

    uc, qk, vg, lr, n1 = _fwd_inproj(h0, p["g1"], p["w_in"])
    ypre, yc = _fwd_conv(uc, p["conv_w"], p["conv_b"], p["ln_g"], p["ln_b"], p["token"], n_ex)
    yg, o, st = _fwd_gla(qk, vg, lr, p["w2"], p["gb"], p["ng"], p["token"], n_ex)
    token = pass_on((yc, yg))
    w_out, wg, wu, wd = late_weights(token)
    n2, f, da, db, dh2, dh1, dh1b, dyc, dyg, part = _mid_rows(
        yc, yg, h0, tgt_p, w_out, wg, wu, wd, p["g2"], p["g3"], token, lp)
    g = {}
    token = send_early("ffn", [_matmul_tn(a_, b_, name).reshape(N_DEV, FF_S, D) for a_, b_, name in (
        (da, n2, "dw_gate"), (db, n2, "dw_up"), (f, dh2, "dw_down"))])
    dqk, dvg, dlr, g["w2"], g["gla_vec"], dw_out = _bwd_gla(
        qk, vg, lr, o, st, dyg, p["w2"], p["gb"], p["ng"], yc, yg, dh1b, token, n_ex)
    token = send_early("out", [dw_out.reshape(N_DEV, W_OUT_S, D)])
    duc, g["conv_w"], g["conv_vec"] = _bwd_conv(uc, ypre, dyc, p["conv_w"], p["ln_g"], p["ln_b"], token, n_ex)
    token = send_early("in", [_dw_blocked(n1, [duc, dqk, dvg, dlr], W_IN_S, "dw_in")])
    grad_x, g["in_vec"], g["meta"] = _bwd_inproj(duc, dqk, dvg, dlr, dh1, h0, p["w_in"], p["g1"], token, lp)
    g["ffn_vec"] = part
    return grad_x, g


W_IN_S = D_IN // N_DEV
W_OUT_S = D // N_DEV
FF_S = D_FF // N_DEV
CONV_S = C_CONV // N_DEV
GATE_S = GLA_K // N_DEV
SMALL_PACK = 64
CONV_ROW = 16
GATE_ROW = 48
VEC_ROWS = 16
_VEC_ROWS = (("norm_mix_g", D), ("conv_b", C_CONV), ("conv_ln_g", C_CONV), ("conv_ln_b", C_CONV), ("gla_gate_b", GLA_K),
             ("gla_norm_g", GLA_DV), ("norm_ffn_g", D), ("norm_final_g", D))
LOSS_ROW = len(_VEC_ROWS)


def _position():
    return lax.axis_index("x"), lax.axis_index("y"), lax.axis_index("c")


def _any():
    return pl.BlockSpec(memory_space=pl.ANY)


def _stage(mats, meta, conv_w, w2):
    n_t = len(mats) + 1

    def body(*refs):
        ins = refs[0:n_t - 1]
        meta_ref, cw_ref, w2_ref = refs[n_t - 1:n_t + 2]
        lands = refs[n_t + 2:2 * n_t + 2]
        shards = refs[2 * n_t + 2:3 * n_t + 2]
        sems = refs[3 * n_t + 2]
        for s_ref, w_ref in zip(shards, ins):
            s_ref[...] = w_ref[...].astype(BF16)
        sp = shards[n_t - 1]
        sp[...] = jnp.zeros_like(sp)
        sp[0:N_META, :] = meta_ref[...]
        sp[CONV_ROW:CONV_ROW + CONV_W, 0:CONV_S] = cw_ref[...]
        sp[GATE_ROW:GATE_ROW + RANK, 0:GATE_S] = w2_ref[...]
        x, y, c = _position()
        mine = [pltpu.make_async_copy(shards[t], lands[t].at[4 * x + 2 * y + c], sems.at[t]) for t in range(n_t)]
        for cp in mine:
            cp.start()
        for cp in mine:
            cp.wait()

    shard_shapes = [jax.ShapeDtypeStruct(m.shape, BF16) for m in mats] + [jax.ShapeDtypeStruct((SMALL_PACK, 128), F32)]
    res = pl.pallas_call(
        body, name="stage",
        out_shape=[jax.ShapeDtypeStruct((N_DEV,) + s.shape, s.dtype) for s in shard_shapes] + shard_shapes,
        in_specs=[_whole_vmem()] * (n_t + 2), out_specs=[_any()] * n_t + [_whole_vmem()] * n_t,
        scratch_shapes=[pltpu.SemaphoreType.DMA((n_t,))],
        compiler_params=pltpu.CompilerParams(vmem_limit_bytes=VMEM_LIMIT),
    )(*mats, meta, conv_w, w2)
    return res[0:n_t], res[n_t:]


_HBM = pl.BlockSpec(memory_space=pltpu.HBM)
_SEM = pl.BlockSpec(memory_space=pltpu.SEMAPHORE)
_EFFECT = pltpu.SideEffectType.DATAFLOW_SIDE_EFFECTING


_N_ROUTES = {"scatter": 7, "first": 4, "forward": 3}


def _routes(mode):
    x, y, c = _position()
    me = 4 * x + 2 * y + c
    if mode == "scatter":
        out = []
        for k in (6, 7, 4, 5, 2, 3, 1):
            px = 1 - x if k & 4 else x
            py = 1 - y if k & 2 else y
            pc = 1 - c if k & 1 else c
            out.append(((px, py, pc), 4 * px + 2 * py + pc, me))
        return out
    if mode == "first":
        return [(pos, None, me) for pos in ((x, y, 1 - c), (1 - x, y, c), (x, 1 - y, c), (1 - x, 1 - y, c))]
    assert mode == "forward"
    return [((x, y, 1 - c), 4 * px + 2 * py + c, 4 * px + 2 * py + c) for px, py in ((1 - x, y), (x, 1 - y), (1 - x, 1 - y))]


def _route_copies(mode, n, src_refs, land_refs, send_sems, recv_sems):
    nr = _N_ROUTES[mode]
    for i, (pos, src_blk, dst_blk) in enumerate(_routes(mode)):
        for t in range(n):
            src = land_refs[t] if mode == "forward" else src_refs[t]
            yield pltpu.make_async_remote_copy(
                src_ref=src if src_blk is None else src.at[src_blk], dst_ref=land_refs[t].at[dst_blk],
                send_sem=send_sems.at[nr * t + i], recv_sem=recv_sems.at[nr * t + i], device_id=pos, device_id_type=MESH)


def _in_hbm(a):
    return pltpu.with_memory_space_constraint(a, pltpu.HBM)


def _send_start(name, groups, mode, after):
    sizes = [(len(s), len(l)) for s, l in groups]
    bufs = [b for s, l in groups for b in list(s) + list(l)]
    nb, ng = len(bufs), len(groups)

    def body(*refs):
        sems = refs[nb + 1:nb + 1 + 2 * ng]
        token = refs[2 * nb + 2 * ng + 1]
        off = 0
        for gi, (ns, n) in enumerate(sizes):
            for cp in _route_copies(mode, n, refs[off:off + ns], refs[off + ns:off + ns + n], sems[2 * gi], sems[2 * gi + 1]):
                cp.start()
            off += ns + n
        token[...] = jnp.zeros_like(token)

    res = pl.pallas_call(
        body, name=name,
        out_shape=(*[pltpu.SemaphoreType.DMA((_N_ROUTES[mode] * n,)) for _, n in sizes for _ in range(2)],
                   *[pltpu.HBM(b.shape, b.dtype) for b in bufs], jax.ShapeDtypeStruct((8, 128), F32)),
        in_specs=[_HBM] * nb + [_any()], out_specs=(*[_SEM] * (2 * ng), *[_HBM] * nb, _whole_vmem()),
        input_output_aliases={i: 2 * ng + i for i in range(nb)},
        compiler_params=pltpu.CompilerParams(has_side_effects=_EFFECT),
    )(*[_in_hbm(b) for b in bufs], after)
    handles, off = [], 2 * ng
    for gi, (ns, n) in enumerate(sizes):
        handles.append((res[2 * gi], res[2 * gi + 1], res[off:off + ns], res[off + ns:off + ns + n]))
        off += ns + n
    return handles, res[2 * ng + nb]


def _send_wait(name, send_sems, recv_sems, srcs, lands, mode, after):
    n, ns = len(lands), len(srcs)
    after = after if isinstance(after, tuple) else (after,)

    def body(*refs):
        src_refs, land_refs = refs[0:ns], refs[ns:ns + n]
        send_sems, recv_sems = refs[ns + n:ns + n + 2]
        for cp in _route_copies(mode, n, src_refs, land_refs, send_sems, recv_sems):
            cp.wait_send()
            cp.wait_recv()

    bufs = list(srcs) + list(lands)
    res = pl.pallas_call(
        body, name=name,
        out_shape=tuple(pltpu.HBM(b.shape, b.dtype) for b in bufs),
        in_specs=[_HBM] * len(bufs) + [_SEM, _SEM] + [_any()] * len(after), out_specs=tuple([_HBM] * len(bufs)),
        input_output_aliases={i: i for i in range(len(bufs))},
        compiler_params=pltpu.CompilerParams(has_side_effects=_EFFECT),
    )(*bufs, send_sems, recv_sems, *after)
    return res[0:ns], res[ns:ns + n]


def _unshard_in(a_in, a_small, token):
    def body(a_ref, s_ref, token_ref, w_ref, meta_ref, cw_ref, w2_ref):
        w_ref[:, D_IN:D_INP] = jnp.zeros((D, D_INP - D_IN), BF16)
        w2_ref[...] = jnp.zeros_like(w2_ref)
        for d in range(N_DEV):
            w_ref[:, d * W_IN_S:(d + 1) * W_IN_S] = a_ref[d]
            meta_ref[:, d * 128:(d + 1) * 128] = s_ref[d, 0:N_META, :]
            cw_ref[:, d * CONV_S:(d + 1) * CONV_S] = s_ref[d, CONV_ROW:CONV_ROW + 32, 0:CONV_S]
            w2_ref[0:RANK, d * GATE_S:(d + 1) * GATE_S] = s_ref[d, GATE_ROW:GATE_ROW + RANK, 0:GATE_S].astype(BF16)

    return pl.pallas_call(
        body, name="unshard_in",
        out_shape=[jax.ShapeDtypeStruct((D, D_INP), BF16), jax.ShapeDtypeStruct((N_META, D), F32),
                   jax.ShapeDtypeStruct((32, C_CONV), F32), jax.ShapeDtypeStruct((RANK_P, GLA_K), BF16)],
        compiler_params=pltpu.CompilerParams(vmem_limit_bytes=VMEM_LIMIT),
    )(a_in, a_small, token)


def _pack_small(g):
    def body(meta_ref, cw_ref, w2_ref, in_vec, ffn_vec, conv_vec, gla_vec, sp, vp):
        sp[...] = jnp.zeros_like(sp)
        vp[...] = jnp.zeros_like(vp)
        for d in range(N_DEV):
            sp[d, 0:N_META, :] = meta_ref[:, d * 128:(d + 1) * 128]
            sp[d, CONV_ROW:CONV_ROW + 32, 0:CONV_S] = cw_ref[:, d * CONV_S:(d + 1) * CONV_S]
            sp[d, GATE_ROW:GATE_ROW + RANK, 0:GATE_S] = w2_ref[0:RANK, d * GATE_S:(d + 1) * GATE_S]
            vp[d, 0:1, :] = in_vec[0:1, :]
            vp[d, 1:4, 0:C_CONV] = conv_vec[0:3, :]
            vp[d, 4:5, 0:GLA_K] = gla_vec[0:1, :]
            vp[d, 5:6, 0:GLA_DV] = gla_vec[1:2, 0:GLA_DV]
            vp[d, 6:7, :] = ffn_vec[1:2, :]
            vp[d, 7:8, :] = ffn_vec[0:1, :]
            vp[d, LOSS_ROW:LOSS_ROW + 1, :] = ffn_vec[2:3, :]

    return pl.pallas_call(
        body, name="pack_small",
        out_shape=[jax.ShapeDtypeStruct((N_DEV, SMALL_PACK, 128), F32), jax.ShapeDtypeStruct((N_DEV, VEC_ROWS, D), F32)],
    )(g["meta"], g["conv_w"], g["w2"], g["in_vec"], g["ffn_vec"], g["conv_vec"], g["gla_vec"])


def _adamw(w, g, m, v):
    m = ADAM_B1 * m + (1.0 - ADAM_B1) * g
    v = ADAM_B2 * v + (1.0 - ADAM_B2) * (g * g)
    m_hat = m / (1.0 - ADAM_B1 ** ADAM_STEP)
    v_hat = v / (1.0 - ADAM_B2 ** ADAM_STEP)
    return -ADAM_LR * (m_hat / (jnp.sqrt(v_hat) + ADAM_EPS) + ADAM_WD * w), m, v


def _update_matrix(recv, own, me, w, m, v, name):
    _, r, c = recv.shape
    tr = _row_tile(r, 256)

    def body(me_ref, recv_ref, own_ref, w_ref, m_ref, v_ref, g_ref, d_ref, nm_ref, nv_ref):
        g = jnp.zeros((tr, c), F32)
        for s in range(N_DEV):
            g = g + jnp.where(me_ref[0] == s, own_ref[...], recv_ref[s]).astype(F32)
        g_ref[...] = g
        d_ref[...], nm_ref[...], nv_ref[...] = _adamw(w_ref[...], g, m_ref[...], v_ref[...])

    one = pl.BlockSpec((None, tr, c), lambda i, me_ref: (0, i, 0))
    return pl.pallas_call(
        body, name=name,
        grid_spec=pltpu.PrefetchScalarGridSpec(
            num_scalar_prefetch=1, grid=(r // tr,),
            in_specs=[pl.BlockSpec((N_DEV, tr, c), lambda i, me_ref: (0, i, 0)),
                      pl.BlockSpec((None, tr, c), lambda i, me_ref: (me_ref[0], i, 0)), one, one, one],
            out_specs=[one] * 4),
        out_shape=[jax.ShapeDtypeStruct((1, r, c), F32)] * 4,
        compiler_params=_params(("parallel",)),
    )(me, recv, own, w, m, v)


_SMALL = ("meta_tokens", "conv_w", "gla_w_gate2") + tuple(n for n, _ in _VEC_ROWS)


def _update_small(me, srecv, vrecv, sown, vown, w, m, v):
    n = len(_SMALL)

    def body(*refs):
        me_ref, s_ref, v_ref, so_ref, vo_ref = refs[0:5]
        w_refs, m_refs, v_refs = refs[5:5 + n], refs[5 + n:5 + 2 * n], refs[5 + 2 * n:5 + 3 * n]
        outs = refs[5 + 3 * n:]
        ssum = jnp.zeros((SMALL_PACK, 128), F32)
        vsum = jnp.zeros((VEC_ROWS, D), F32)
        for s in range(N_DEV):
            ssum = ssum + jnp.where(me_ref[0] == s, so_ref[s], s_ref[s])
            vsum = vsum + jnp.where(me_ref[0] == s, vo_ref[s], v_ref[s])
        grads = [ssum[0:N_META, :], ssum[CONV_ROW:CONV_ROW + CONV_W, 0:CONV_S], ssum[GATE_ROW:GATE_ROW + RANK, 0:GATE_S]]
        grads += [vsum[i:i + 1, 0:width] for i, (_, width) in enumerate(_VEC_ROWS)]
        for i, g in enumerate(grads):
            d, nm, nv = _adamw(w_refs[i][...], g, m_refs[i][...], v_refs[i][...])
            outs[i][...] = g
            outs[n + i][...] = d
            outs[2 * n + i][...] = nm
            outs[3 * n + i][...] = nv
        outs[4 * n][...] = vsum[LOSS_ROW:LOSS_ROW + 1, 0:128]

    shapes = [jax.ShapeDtypeStruct(t.shape, F32) for t in w]
    res = pl.pallas_call(
        body, name="update_small", out_shape=shapes * 4 + [jax.ShapeDtypeStruct((1, 128), F32)],
        in_specs=[pl.BlockSpec(memory_space=pltpu.SMEM)] + [_whole_vmem()] * (4 + 3 * n),
    )(me, srecv, vrecv, sown, vown, *w, *m, *v)
    return res[0:n], res[n:2 * n], res[2 * n:3 * n], res[3 * n:4 * n], res[4 * n]


_WEIGHTS = ("meta_tokens", "norm_mix_g", "w_in", "conv_w", "conv_b", "conv_ln_g", "conv_ln_b", "gla_w_gate2", "gla_gate_b",
            "gla_norm_g", "w_out", "norm_ffn_g", "w_ffn_gate", "w_ffn_up", "w_ffn_down", "norm_final_g")
_MATRICES = ("w_in", "w_out", "w_ffn_gate", "w_ffn_up", "w_ffn_down")
_TRANSPOSED = ("w_ffn_gate", "w_ffn_up")


def kernel(x, meta_tokens, norm_mix_g, w_in, conv_w, conv_b, conv_ln_g, conv_ln_b, gla_w_gate2, gla_gate_b, gla_norm_g, w_out, norm_ffn_g, w_ffn_gate, w_ffn_up, w_ffn_down, norm_final_g, loss_target, m_meta_tokens, m_norm_mix_g, m_w_in, m_conv_w, m_conv_b, m_conv_ln_g, m_conv_ln_b, m_gla_w_gate2, m_gla_gate_b, m_gla_norm_g, m_w_out, m_norm_ffn_g, m_w_ffn_gate, m_w_ffn_up, m_w_ffn_down, m_norm_final_g, v_meta_tokens, v_norm_mix_g, v_w_in, v_conv_w, v_conv_b, v_conv_ln_g, v_conv_ln_b, v_gla_w_gate2, v_gla_gate_b, v_gla_norm_g, v_w_out, v_norm_ffn_g, v_w_ffn_gate, v_w_ffn_up, v_w_ffn_down, v_norm_final_g):
    given = dict(locals())
    two_d = lambda a: a.reshape(1, -1) if a.ndim == 1 else a.reshape(a.shape[-2:])
    fams = [{n: given[pre + n] for n in _WEIGHTS} for pre in ("", "m_", "v_")]
    for f in fams:
        for n in _TRANSPOSED:
            f[n] = f[n].transpose(0, 2, 1)
    w = fams[0]

    lands, shards = _stage([two_d(w[n]) for n in _MATRICES], w["meta_tokens"], two_d(w["conv_w"]), two_d(w["gla_w_gate2"]))
    soon, later = (0, 5), (1, 2, 3, 4)
    pick = lambda seq, idx: [seq[i] for i in idx]
    (first, ffn_first), started = _send_start(
        "gather_first_start", [(pick(shards, soon), pick(lands, soon)), (pick(shards, later), pick(lands, later))],
        "first", norm_mix_g)
    h0, tgt_p = _pad_rows(x, loss_target)
    _, arrived = _send_wait("gather_first_wait", *first, "first", (h0, tgt_p, started))
    (forward,), token = _send_start("gather_forward_start", [([], arrived)], "forward", started)
    _, (a_in, a_small) = _send_wait("gather_forward_wait", *forward, "forward", token)
    w_in, meta, conv_taps, w2 = _unshard_in(a_in, a_small, token)
    p = dict(meta=meta, conv_w=conv_taps, w2=w2, w_in=w_in, g1=norm_mix_g, conv_b=conv_b, ln_g=conv_ln_g, ln_b=conv_ln_b,
             gb=gla_gate_b, ng=gla_norm_g, g2=norm_ffn_g, g3=two_d(norm_final_g), token=token)
    passed = {}

    def pass_on(after):
        _, arrived_ffn = _send_wait("gather_ffn_first_wait", *ffn_first, "first", after)
        (passed["sent"],), token = _send_start("gather_ffn_forward_start", [([], arrived_ffn)], "forward", norm_mix_g)
        return token

    def late_weights(after):
        _, (a_out, a_g, a_u, a_d) = _send_wait("gather_ffn_forward_wait", *passed["sent"], "forward", after)
        return a_out.reshape(D, D), a_g.reshape(D_FF, D), a_u.reshape(D_FF, D), a_d.reshape(D_FF, D)

    sent = {}

    def send_early(tag, mats):
        landing = [_in_hbm(lax.empty(m_.shape, m_.dtype)) for m_ in mats]
        (sent[tag],), token = _send_start("scatter_" + tag + "_start", [(mats, landing)], "scatter", norm_mix_g)
        return token

    grad_x, g = _local_step(h0, tgt_p, p, pass_on, late_weights, send_early)

    token = send_early("small", list(_pack_small(g)))
    x_, y_, c_ = _position()
    me = (4 * x_ + 2 * y_ + c_).astype(jnp.int32).reshape(1)
    res = {}
    for tag, names in (("ffn", ("w_ffn_gate", "w_ffn_up", "w_ffn_down")), ("out", ("w_out",)), ("in", ("w_in",))):
        own, recv = _send_wait("scatter_" + tag + "_wait", *sent[tag], "scatter", token)
        for n, o_, r_ in zip(names, own, recv):
            res[n] = _update_matrix(r_, o_, me, *[f[n] for f in fams], "update_" + n)
            token = res[n][1]
    (sown, vown), (srecv, vrecv) = _send_wait("scatter_small_wait", *sent["small"], "scatter", token)
    small = _update_small(me, srecv, vrecv, sown, vown, *[[two_d(f[n]) for n in _SMALL] for f in fams])
    for i, n in enumerate(_SMALL):
        res[n] = [fam[i].reshape(w[n].shape) for fam in small[0:4]]
    for n in _TRANSPOSED:
        res[n] = [t.transpose(0, 2, 1) for t in res[n]]
    outs = [small[4][0, 0], grad_x]
    for k in range(4):
        outs += [res[n][k] for n in _WEIGHTS]
    return tuple(outs)
```

```python
import functools

import jax
import jax.numpy as jnp
from jax import lax
from jax.experimental import pallas as pl
from jax.experimental.pallas import tpu as pltpu

F32 = jnp.float32
BF16 = jnp.bfloat16

D = 1024
N_META = 16
C_CONV = 512
CONV_W = 31
GLA_H = 4
GLA_DK = 64
GLA_DV = 128
GLA_K = GLA_H * GLA_DK
GLA_V = GLA_H * GLA_DV
RANK = 16
RANK_P = 128
TAU = 16.0
CHUNK = 64
LEAD = CHUNK
ZROWS = LEAD - N_META
D_IN = 2 * C_CONV + 2 * GLA_K + 2 * GLA_V + RANK
D_INP = D_IN - RANK + RANK_P
D_FF = 2816
FF_CHUNK = 1408
FF_SPLIT = (0, 1536, D_FF)
RMS_EPS = 1e-6
LN_EPS = 1e-5
N_DEV = 8

ADAM_LR = 0.001
ADAM_B1 = 0.9
ADAM_B2 = 0.999
ADAM_EPS = 1e-08
ADAM_WD = 0.01
ADAM_STEP = 10

VMEM_LIMIT = 60 * 1024 * 1024
ROW_TILE = 1056
FFN_ROW_TILE = 352
DW_ROW_TILE = 1408
MESH = pl.DeviceIdType.MESH

_NN = (((1,), (0,)), ((), ()))
_NT = (((1,), (1,)), ((), ()))
_TN = (((0,), (0,)), ((), ()))


def _dot(a, b, dims=_NN):
    return lax.dot_general(a, b, dims, preferred_element_type=F32)


def _sigmoid(x):
    return 1.0 / (1.0 + jnp.exp(-x))


def _row_tile(rows, target):
    best = None
    for t in range(16, min(rows, target) + 1, 16):
        if rows % t == 0:
            best = t
    assert best is not None, rows
    return best


def _params(sem=None):
    return pltpu.CompilerParams(dimension_semantics=sem, vmem_limit_bytes=VMEM_LIMIT)


def _whole_vmem():
    return pl.BlockSpec(memory_space=pltpu.VMEM)


def _rows(tm, width):
    return pl.BlockSpec((tm, width), lambda i: (i, 0))


def _fixed(shape):
    return pl.BlockSpec(shape, lambda *_: (0,) * len(shape))


def _fwd_inproj(h0, g1, w_in):
    rows = h0.shape[0]
    tm = _row_tile(rows, ROW_TILE)

    def body(h_ref, g_ref, w_ref, uc_ref, qk_ref, vg_ref, lr_ref, n1_ref):
        h = h_ref[...]
        r = lax.rsqrt(jnp.mean(h * h, axis=-1, keepdims=True) + RMS_EPS)
        n = (h * r * g_ref[...]).astype(BF16)
        n1_ref[...] = n
        uc_ref[...] = _dot(n, w_ref[:, 0:1024]).astype(BF16)
        qk_ref[...] = _dot(n, w_ref[:, 1024:1536]).astype(BF16)
        vg_ref[...] = _dot(n, w_ref[:, 1536:2560]).astype(BF16)
        lr_ref[...] = _dot(n, w_ref[:, 2560:2688]).astype(BF16)

    return pl.pallas_call(
        body, name="fwd_inproj", grid=(rows // tm,),
        in_specs=[_rows(tm, D), _fixed((1, D)), _whole_vmem()],
        out_specs=[_rows(tm, 1024), _rows(tm, 512), _rows(tm, 1024), _rows(tm, RANK_P), _rows(tm, D)],
        out_shape=[jax.ShapeDtypeStruct((rows, 1024), BF16), jax.ShapeDtypeStruct((rows, 512), BF16),
                   jax.ShapeDtypeStruct((rows, 1024), BF16), jax.ShapeDtypeStruct((rows, RANK_P), BF16),
                   jax.ShapeDtypeStruct((rows, D), BF16)],
        compiler_params=_params(("parallel",)),
    )(h0, g1, w_in)


def _mid_rows(yc, yg, h0, tgt, w_out, wg, wu, wd, g2, g3, token, rows_per_example):
    rows = h0.shape[0]
    tm = _row_tile(rows, FFN_ROW_TILE)
    ff_blocks = [slice(lo, hi) for lo, hi in zip(FF_SPLIT[:-1], FF_SPLIT[1:])]

    def body(yc_ref, yg_ref, h0_ref, t_ref, wo_ref, wg_ref, wu_ref, wd_ref, g2_ref, g3_ref, token_ref,
             n2_ref, f_ref, da_ref, db_ref, dh2_ref, dh1_ref, dh1b_ref, dyc_ref, dyg_ref, part_ref):
        i = pl.program_id(0)
        h1 = h0_ref[...] + _dot(yc_ref[...], wo_ref[0:C_CONV, :]) + _dot(yg_ref[...], wo_ref[C_CONV:D, :])
        r2 = lax.rsqrt(jnp.mean(h1 * h1, axis=-1, keepdims=True) + RMS_EPS)
        xh2 = h1 * r2
        n2 = (xh2 * g2_ref[...]).astype(BF16)
        n2_ref[...] = n2
        y2 = jnp.zeros((tm, D), F32)
        for cs in ff_blocks:
            a = _dot(n2, wg_ref[cs, :], _NT)
            b = _dot(n2, wu_ref[cs, :], _NT)
            f = (a * _sigmoid(a) * b).astype(BF16)
            f_ref[:, cs] = f
            da_ref[:, cs] = a.astype(BF16)
            db_ref[:, cs] = b.astype(BF16)
            y2 = y2 + _dot(f, wd_ref[cs, :])
        h2 = h1 + y2
        r3 = lax.rsqrt(jnp.mean(h2 * h2, axis=-1, keepdims=True) + RMS_EPS)
        xh3 = h2 * r3
        g3 = g3_ref[...]
        pos = (i * tm + lax.broadcasted_iota(jnp.int32, (tm, 1), 0)) % rows_per_example
        valid = pos >= LEAD
        err = jnp.where(valid, xh3 * g3 - t_ref[...], 0.0)
        loss = 0.5 / D * jnp.sum(jnp.sum(err * err, axis=-1, keepdims=True), axis=0, keepdims=True)
        dy = err * (1.0 / D)
        dg3 = jnp.sum(dy * xh3, axis=0, keepdims=True)
        dxh = dy * g3
        dh2 = r3 * (dxh - xh3 * jnp.mean(dxh * xh3, axis=-1, keepdims=True))
        dh2b = dh2.astype(BF16)
        dh2_ref[...] = dh2b
        dn2 = jnp.zeros((tm, D), F32)
        for cs in ff_blocks:
            df = _dot(dh2b, wd_ref[cs, :], _NT)
            a = da_ref[:, cs].astype(F32)
            b = db_ref[:, cs].astype(F32)
            sg = _sigmoid(a)
            da = (df * b * sg * (1.0 + a * (1.0 - sg))).astype(BF16)
            db = (df * a * sg).astype(BF16)
            da_ref[:, cs] = da
            db_ref[:, cs] = db
            dn2 = dn2 + _dot(da, wg_ref[cs, :]) + _dot(db, wu_ref[cs, :])
        dg2 = jnp.sum(dn2 * xh2, axis=0, keepdims=True)
        dxh2 = dn2 * g2_ref[...]
        dh1 = dh2 + r2 * (dxh2 - xh2 * jnp.mean(dxh2 * xh2, axis=-1, keepdims=True))
        dh1_ref[...] = dh1
        dh1b = dh1.astype(BF16)
        dh1b_ref[...] = dh1b
        dyc_ref[...] = _dot(dh1b, wo_ref[0:C_CONV, :], _NT)
        dyg_ref[...] = _dot(dh1b, wo_ref[C_CONV:D, :], _NT)

        @pl.when(i == 0)
        def _():
            part_ref[...] = jnp.zeros_like(part_ref)

        part_ref[0:1, :] += dg3
        part_ref[1:2, :] += dg2
        part_ref[2:3, :] += jnp.broadcast_to(loss, (1, D))

    return pl.pallas_call(
        body, name="mid_rows", grid=(rows // tm,),
        in_specs=[_rows(tm, C_CONV), _rows(tm, GLA_V), _rows(tm, D), _rows(tm, D), _whole_vmem(), _whole_vmem(),
                  _whole_vmem(), _whole_vmem(), _fixed((1, D)), _fixed((1, D)), _fixed((8, 128))],
        out_specs=[_rows(tm, D), _rows(tm, D_FF), _rows(tm, D_FF), _rows(tm, D_FF), _rows(tm, D), _rows(tm, D),
                   _rows(tm, D), _rows(tm, C_CONV), _rows(tm, GLA_V), _fixed((8, D))],
        out_shape=[jax.ShapeDtypeStruct((rows, D), BF16)] + [jax.ShapeDtypeStruct((rows, D_FF), BF16)] * 3
        + [jax.ShapeDtypeStruct((rows, D), BF16), jax.ShapeDtypeStruct((rows, D), F32),
           jax.ShapeDtypeStruct((rows, D), BF16), jax.ShapeDtypeStruct((rows, C_CONV), F32),
           jax.ShapeDtypeStruct((rows, GLA_V), F32), jax.ShapeDtypeStruct((8, D), F32)],
        compiler_params=_params(("arbitrary",)),
    )(yc, yg, h0, tgt, w_out, wg, wu, wd, g2, g3, token)


def _bwd_inproj(duc, dqk, dvg, dlr, dh1, h0, w_in, g1, token, rows_per_example):
    rows = h0.shape[0]
    n_ex = rows // rows_per_example
    tm = _row_tile(rows_per_example, ROW_TILE)
    tiles_per_example = rows_per_example // tm
    n_steps = rows // tm

    def body(duc_ref, dqk_ref, dvg_ref, dlr_ref, dh1_ref, h_ref, w_ref, g_ref, token_ref, gx_ref, part_ref, dmeta_ref,
             buf_ref, sems):
        dn = (_dot(duc_ref[...], w_ref[:, 0:1024], _NT) + _dot(dqk_ref[...], w_ref[:, 1024:1536], _NT)
              + _dot(dvg_ref[...], w_ref[:, 1536:2560], _NT) + _dot(dlr_ref[...], w_ref[:, 2560:2688], _NT))
        h = h_ref[...]
        r = lax.rsqrt(jnp.mean(h * h, axis=-1, keepdims=True) + RMS_EPS)
        xh = h * r
        dg = jnp.sum(dn * xh, axis=0, keepdims=True)
        dxh = dn * g_ref[...]
        dh0 = dh1_ref[...] + r * (dxh - xh * jnp.mean(dxh * xh, axis=-1, keepdims=True))
        i = pl.program_id(0)

        def copies(step):
            slot, b, j = step % 2, step // tiles_per_example, step % tiles_per_example
            out = [(j == 0, pltpu.make_async_copy(buf_ref.at[slot, pl.ds(LEAD, tm - LEAD)],
                                                   gx_ref.at[b, pl.ds(0, tm - LEAD)], sems.at[slot]))]
            if tiles_per_example > 1:
                out.append((j != 0, pltpu.make_async_copy(
                    buf_ref.at[slot], gx_ref.at[b, pl.ds(pl.multiple_of(jnp.maximum(j * tm - LEAD, 0), 8), tm)],
                    sems.at[slot])))
            return out

        def each(step, act):
            for cond, cp in copies(step):
                pl.when(cond)(functools.partial(act, cp))

        @pl.when(i >= 2)
        def _():
            each(i - 2, lambda cp: cp.wait())

        buf_ref[i % 2] = dh0
        each(i, lambda cp: cp.start())

        @pl.when(i == n_steps - 1)
        def _():
            each(i, lambda cp: cp.wait())
            if n_steps > 1:
                each(i - 1, lambda cp: cp.wait())

        @pl.when(i == 0)
        def _():
            part_ref[...] = jnp.zeros_like(part_ref)
            dmeta_ref[...] = jnp.zeros_like(dmeta_ref)

        part_ref[0:1, :] += dg

        @pl.when(i % tiles_per_example == 0)
        def _():
            dmeta_ref[...] += dh0[ZROWS:LEAD, :]

    return pl.pallas_call(
        body, name="bwd_inproj", grid=(n_steps,),
        in_specs=[_rows(tm, 1024), _rows(tm, 512), _rows(tm, 1024), _rows(tm, RANK_P), _rows(tm, D), _rows(tm, D),
                  _whole_vmem(), _fixed((1, D)), _fixed((8, 128))],
        out_specs=[_any(), _fixed((8, D)), _fixed((N_META, D))],
        out_shape=[jax.ShapeDtypeStruct((n_ex, rows_per_example - LEAD, D), F32), jax.ShapeDtypeStruct((8, D), F32),
                   jax.ShapeDtypeStruct((N_META, D), F32)],
        scratch_shapes=[pltpu.VMEM((2, tm, D), F32), pltpu.SemaphoreType.DMA((2,))],
        compiler_params=_params(("arbitrary",)),
    )(duc, dqk, dvg, dlr, dh1, h0, w_in, g1, token)


def _dw_blocked(a, bs, width, name):
    rows, m = a.shape
    ws = [b.shape[1] for b in bs]
    assert sum(ws) >= N_DEV * width
    tk = _row_tile(rows, DW_ROW_TILE)
    nk = rows // tk

    def body(a_ref, *refs):
        b_refs, o_ref, acc_ref = refs[:len(bs)], refs[len(bs)], refs[len(bs) + 1]
        k = pl.program_id(0)

        @pl.when(k == 0)
        def _():
            acc_ref[...] = jnp.zeros_like(acc_ref)

        at = a_ref[...].T
        off = 0
        for b_ref, w in zip(b_refs, ws):
            acc_ref[:, off:off + w] += _dot(at, b_ref[...])
            off += w

        @pl.when(k == nk - 1)
        def _():
            for d in range(N_DEV):
                o_ref[d] = acc_ref[:, d * width:(d + 1) * width].astype(BF16)

    return pl.pallas_call(
        body, name=name, grid=(nk,),
        in_specs=[_rows(tk, m)] + [_rows(tk, w) for w in ws],
        out_specs=_fixed((N_DEV, m, width)),
        out_shape=jax.ShapeDtypeStruct((N_DEV, m, width), BF16),
        scratch_shapes=[pltpu.VMEM((m, sum(ws)), F32)],
        compiler_params=_params(("arbitrary",)),
    )(a, *bs)


def _matmul_tn(a, b, name):
    rows, m = a.shape
    n = b.shape[1]
    tk = _row_tile(rows, DW_ROW_TILE)
    tn = n if n <= 1024 else FF_CHUNK
    tm_ = m if m <= 1024 else FF_CHUNK
    assert n % tn == 0 and m % tm_ == 0
    nk = rows // tk

    def body(a_ref, b_ref, o_ref, acc_ref):
        k = pl.program_id(2)

        @pl.when(k == 0)
        def _():
            acc_ref[...] = jnp.zeros_like(acc_ref)

        acc_ref[...] += _dot(a_ref[...], b_ref[...], _TN)

        @pl.when(k == nk - 1)
        def _():
            o_ref[...] = acc_ref[...].astype(BF16)

    return pl.pallas_call(
        body, name=name, grid=(m // tm_, n // tn, nk),
        in_specs=[pl.BlockSpec((tk, tm_), lambda i, j, k: (k, i)), pl.BlockSpec((tk, tn), lambda i, j, k: (k, j))],
        out_specs=pl.BlockSpec((tm_, tn), lambda i, j, k: (i, j)),
        out_shape=jax.ShapeDtypeStruct((m, n), BF16),
        scratch_shapes=[pltpu.VMEM((tm_, tn), F32)],
        compiler_params=_params(("parallel", "parallel", "arbitrary")),
    )(a, b)


HALO = 32
LN_ROWS = 352
LANES = 128


def _shifted(win, offsets):
    for r in range(8):
        js = [j for j, k in enumerate(offsets) if k % 8 == r]
        if js:
            rolled = win if r == 0 else pltpu.roll(win, CHUNK + HALO - r, 0)
            for j in js:
                yield j, rolled[offsets[j] - r:offsets[j] - r + CHUNK]


def _glu_into(uc_ref, vs_ref, n_chunk):
    vs_ref[0:CHUNK, :] = jnp.zeros((CHUNK, C_CONV), F32)

    def glu(i, carry):
        base = pl.multiple_of(i * CHUNK, CHUNK)
        val = uc_ref[pl.ds(base, CHUNK), 0:C_CONV].astype(F32)
        gate = uc_ref[pl.ds(base, CHUNK), C_CONV:2 * C_CONV].astype(F32)
        vs_ref[pl.ds(base + CHUNK, CHUNK), :] = val * _sigmoid(gate)
        return carry

    lax.fori_loop(0, n_chunk, glu, 0, unroll=3)


def _fwd_conv(uc, conv_w, conv_b, ln_g, ln_b, token, n_ex, first_ex, count, prev):
    rows = uc.shape[0]
    lp = rows // n_ex
    n_chunk = lp // CHUNK
    prev = () if prev is None else tuple(prev)

    def body(uc_ref, w_ref, b_ref, lg_ref, lb_ref, token_ref, *refs):
        ypre_ref, yc_ref, vs_ref = refs[len(prev):]
        _glu_into(uc_ref, vs_ref, n_chunk)

        def conv(i, carry):
            base = pl.multiple_of(i * CHUNK, CHUNK)
            for lb in range(C_CONV // LANES):
                ls = slice(lb * LANES, (lb + 1) * LANES)
                win = vs_ref[pl.ds(base + CHUNK - HALO, CHUNK + HALO), ls]
                acc = jnp.broadcast_to(b_ref[:, ls], (CHUNK, LANES))
                for j, rows_j in _shifted(win, [HALO - (CONV_W - 1) + j for j in range(CONV_W)]):
                    acc = acc + w_ref[j:j + 1, ls] * rows_j
                ypre_ref[pl.ds(base, CHUNK), ls] = acc
            return carry

        lax.fori_loop(0, n_chunk, conv, 0, unroll=3)

        ln_rows = _row_tile(lp, LN_ROWS)

        def norm(i, carry):
            base = pl.multiple_of(i * ln_rows, 16)
            y = ypre_ref[pl.ds(base, ln_rows), :]
            mu = jnp.mean(y, axis=-1, keepdims=True)
            yc_ = y - mu
            rstd = lax.rsqrt(jnp.mean(yc_ * yc_, axis=-1, keepdims=True) + LN_EPS)
            s = yc_ * rstd * lg_ref[...] + lb_ref[...]
            yc_ref[pl.ds(base, ln_rows), :] = (s * _sigmoid(s)).astype(BF16)
            return carry

        lax.fori_loop(0, lp // ln_rows, norm, 0)

    ex = lambda w: pl.BlockSpec((lp, w), lambda b: (b + first_ex, 0))
    return pl.pallas_call(
        body, name="fwd_conv_%d" % first_ex, grid=(count,),
        in_specs=[ex(2 * C_CONV), _fixed((32, C_CONV)), _fixed((1, C_CONV)), _fixed((1, C_CONV)), _fixed((1, C_CONV)),
                  _fixed((8, 128))] + [_any()] * len(prev),
        out_specs=[ex(C_CONV), ex(C_CONV)],
        out_shape=[jax.ShapeDtypeStruct((rows, C_CONV), F32), jax.ShapeDtypeStruct((rows, C_CONV), BF16)],
        input_output_aliases={6 + i: i for i in range(len(prev))},
        scratch_shapes=[pltpu.VMEM((lp + CHUNK, C_CONV), F32)],
        compiler_params=_params(("parallel",)),
    )(uc, conv_w, conv_b, ln_g, ln_b, token, *prev)


def _bwd_conv(uc, ypre, dyc, conv_w, ln_g, ln_b, token, n_ex):
    rows = uc.shape[0]
    lp = rows // n_ex
    n_chunk = lp // CHUNK

    def body(uc_ref, ypre_ref, dyc_ref, w_ref, lg_ref, lb_ref, token_ref, duc_ref, dw_ref, dvec_ref, vs_ref, dys_ref,
             dwacc_ref):
        _glu_into(uc_ref, vs_ref, n_chunk)
        dys_ref[pl.ds(lp, CHUNK), :] = jnp.zeros((CHUNK, C_CONV), F32)
        dwacc_ref[...] = jnp.zeros_like(dwacc_ref)

        ln_rows = _row_tile(lp, LN_ROWS)

        def ln_bwd(i, carry):
            dcb, dlg, dlb = carry
            base = pl.multiple_of(i * ln_rows, 16)
            y = ypre_ref[pl.ds(base, ln_rows), :]
            mu = jnp.mean(y, axis=-1, keepdims=True)
            yc_ = y - mu
            rstd = lax.rsqrt(jnp.mean(yc_ * yc_, axis=-1, keepdims=True) + LN_EPS)
            xh = yc_ * rstd
            s = xh * lg_ref[...] + lb_ref[...]
            sg = _sigmoid(s)
            ds = dyc_ref[pl.ds(base, ln_rows), :] * (sg * (1.0 + s * (1.0 - sg)))
            dxh = ds * lg_ref[...]
            dy = rstd * (dxh - jnp.mean(dxh, axis=-1, keepdims=True) - xh * jnp.mean(dxh * xh, axis=-1, keepdims=True))
            dys_ref[pl.ds(base, ln_rows), :] = dy
            return (dcb + jnp.sum(dy, axis=0, keepdims=True), dlg + jnp.sum(ds * xh, axis=0, keepdims=True),
                    dlb + jnp.sum(ds, axis=0, keepdims=True))

        zero = jnp.zeros((1, C_CONV), F32)
        dcb, dlg, dlb = lax.fori_loop(0, lp // ln_rows, ln_bwd, (zero, zero, zero))

        @pl.when(pl.program_id(0) == 0)
        def _():
            dvec_ref[...] = jnp.zeros_like(dvec_ref)
            dw_ref[...] = jnp.zeros_like(dw_ref)

        dvec_ref[0:1, :] += dcb
        dvec_ref[1:2, :] += dlg
        dvec_ref[2:3, :] += dlb

        def taps(i, carry):
            base = pl.multiple_of(i * CHUNK, CHUNK)
            for lb in range(C_CONV // LANES):
                ls = slice(lb * LANES, (lb + 1) * LANES)
                dwin = dys_ref[pl.ds(base, CHUNK + HALO), ls]
                vwin = vs_ref[pl.ds(base + CHUNK - HALO, CHUNK + HALO), ls]
                dy = dwin[0:CHUNK]
                acc = jnp.zeros((CHUNK, LANES), F32)
                for j, rows_j in _shifted(dwin, [CONV_W - 1 - j for j in range(CONV_W)]):
                    acc = acc + w_ref[j:j + 1, ls] * rows_j
                for j, rows_j in _shifted(vwin, [HALO - (CONV_W - 1) + j for j in range(CONV_W)]):
                    dwacc_ref[8 * j:8 * j + 8, ls] += jnp.sum((dy * rows_j).reshape(CHUNK // 8, 8, LANES), axis=0)
                val = uc_ref[pl.ds(base, CHUNK), ls].astype(F32)
                gate = uc_ref[pl.ds(base, CHUNK), C_CONV + lb * LANES:C_CONV + (lb + 1) * LANES].astype(F32)
                sg = _sigmoid(gate)
                duc_ref[pl.ds(base, CHUNK), ls] = (acc * sg).astype(BF16)
                duc_ref[pl.ds(base, CHUNK), C_CONV + lb * LANES:C_CONV + (lb + 1) * LANES] = (
                    acc * val * sg * (1.0 - sg)).astype(BF16)
            return carry

        lax.fori_loop(0, n_chunk, taps, 0, unroll=3)
        for j in range(CONV_W):
            dw_ref[j:j + 1, :] += jnp.sum(dwacc_ref[8 * j:8 * j + 8, :], axis=0, keepdims=True)

    ex = lambda w: pl.BlockSpec((lp, w), lambda b: (b, 0))
    return pl.pallas_call(
        body, name="bwd_conv", grid=(n_ex,),
        in_specs=[ex(2 * C_CONV), ex(C_CONV), ex(C_CONV), _fixed((32, C_CONV)), _fixed((1, C_CONV)), _fixed((1, C_CONV)),
                  _fixed((8, 128))],
        out_specs=[ex(2 * C_CONV), _fixed((32, C_CONV)), _fixed((8, C_CONV))],
        out_shape=[jax.ShapeDtypeStruct((rows, 2 * C_CONV), BF16), jax.ShapeDtypeStruct((32, C_CONV), F32),
                   jax.ShapeDtypeStruct((8, C_CONV), F32)],
        scratch_shapes=[pltpu.VMEM((lp + CHUNK, C_CONV), F32), pltpu.VMEM((lp + CHUNK, C_CONV), F32),
                        pltpu.VMEM((8 * 32, C_CONV), F32)],
        compiler_params=_params(("arbitrary",)),
    )(uc, ypre, dyc, conv_w, ln_g, ln_b, token)


def _seg_chunks(n_chunk):
    return max(c for c in (11, 3, 1) if n_chunk % c == 0)


def _block_mask(shape, row_block, lane_block):
    return (lax.broadcasted_iota(jnp.int32, shape, 0) // row_block) == (lax.broadcasted_iota(jnp.int32, shape, 1) // lane_block)


def _per_head_rows(x, mask):
    return jnp.where(mask, jnp.concatenate([x] * GLA_H, axis=0), 0)


def _fold_heads(full, lane_block):
    lane = lax.broadcasted_iota(jnp.int32, (1, full.shape[1]), 1) // lane_block
    out = jnp.where(lane == 0, full[0:CHUNK], 0.0)
    for h in range(1, GLA_H):
        out = out + jnp.where(lane == h, full[h * CHUNK:(h + 1) * CHUNK], 0.0)
    return out


PAIRS = GLA_H // 2


def _expand_state(blocks):
    lane = lax.broadcasted_iota(jnp.int32, (GLA_DV, 128), 1) // GLA_DK
    zero = jnp.zeros_like(blocks[0])
    rows = []
    for h in range(GLA_H):
        p, hh = divmod(h, 2)
        mine = jnp.where(lane == hh, blocks[p], 0)
        rows.append(jnp.concatenate([mine if q == p else zero for q in range(PAIRS)], axis=1))
    return jnp.concatenate(rows, axis=0)


def _compact_state(full, p):
    lane = lax.broadcasted_iota(jnp.int32, (GLA_DV, 128), 1) // GLA_DK
    ls = slice(128 * p, 128 * (p + 1))
    return jnp.where(lane == 0, full[2 * p * GLA_DV:(2 * p + 1) * GLA_DV, ls], full[(2 * p + 1) * GLA_DV:(2 * p + 2) * GLA_DV, ls])


def _causal_heads():
    return (lax.broadcasted_iota(jnp.int32, (CHUNK, GLA_H * CHUNK), 1) % CHUNK) <= lax.broadcasted_iota(
        jnp.int32, (CHUNK, GLA_H * CHUNK), 0)


def _cumsum_rows(x):
    row = lax.broadcasted_iota(jnp.int32, x.shape, 0)
    s = 1
    while s < CHUNK:
        x = x + jnp.where(row >= s, pltpu.roll(x, s, 0), 0.0)
        s *= 2
    return x


def _rev_cumsum_rows(x):
    row = lax.broadcasted_iota(jnp.int32, x.shape, 0)
    s = 1
    while s < CHUNK:
        x = x + jnp.where(row < CHUNK - s, pltpu.roll(x, CHUNK - s, 0), 0.0)
        s *= 2
    return x


def _gate_terms(lr_ref, w2_ref, gb_ref, rs, first_pos):
    z = _dot(lr_ref[rs, :].astype(BF16), w2_ref[...]) + gb_ref[...]
    la = (jnp.minimum(z, 0.0) - jnp.log(1.0 + jnp.exp(-jnp.abs(z)))) * (1.0 / TAU)
    pos = first_pos + lax.broadcasted_iota(jnp.int32, (CHUNK, 1), 0)
    live = pos >= ZROWS
    la = jnp.where(live, la, 0.0)
    return z, live, _cumsum_rows(la)


def _fwd_gla(qk, vg, lr, w2p, gb, ng, token, n_ex):
    rows = qk.shape[0]
    lp = rows // n_ex
    n_chunk = lp // CHUNK
    sc = _seg_chunks(n_chunk)
    n_seg = n_chunk // sc
    seg = sc * CHUNK

    def body(qk_ref, vg_ref, lr_ref, w2_ref, gb_ref, ng_ref, token_ref, yg_ref, o_ref, st_ref, state_ref):
        sidx = pl.program_id(1)

        @pl.when(sidx == 0)
        def _():
            state_ref[...] = jnp.zeros_like(state_ref)

        causal = _causal_heads()
        k_mask = _block_mask((GLA_H * CHUNK, GLA_K), CHUNK, GLA_DK)
        v_mask = _block_mask((GLA_H * CHUNK, GLA_V), CHUNK, GLA_DV)

        def chunk(ci, carry):
            base = pl.multiple_of(ci * CHUNK, CHUNK)
            rs = pl.ds(base, CHUNK)
            _, _, bcum = _gate_terms(lr_ref, w2_ref, gb_ref, rs, (sidx * sc + ci) * CHUNK)
            bl = bcum[CHUNK - 1:CHUNK, :]
            q = qk_ref[rs, 0:GLA_K].astype(F32)
            k = qk_ref[rs, GLA_K:2 * GLA_K].astype(F32)
            qt = (q * (GLA_DK ** -0.5) * jnp.exp(bcum)).astype(BF16)
            kt = (k * jnp.exp(-bcum)).astype(BF16)
            kh = (k * jnp.exp(bl - bcum)).astype(BF16)
            vb = vg_ref[rs, 0:GLA_V].astype(BF16)
            state = [state_ref[p] for p in range(PAIRS)]
            for p in range(PAIRS):
                st_ref[ci, p] = state[p]
            a = jnp.where(causal, _dot(qt, _per_head_rows(kt, k_mask), _NT), 0.0)
            o = _dot(a.astype(BF16), _per_head_rows(vb, v_mask)) + _dot(
                qt, _expand_state([s.astype(BF16) for s in state]), _NT)
            o_ref[rs, :] = o
            for h in range(GLA_H):
                hs = slice(h * GLA_DV, (h + 1) * GLA_DV)
                oh = o[:, hs]
                ro = lax.rsqrt(jnp.mean(oh * oh, axis=-1, keepdims=True) + RMS_EPS)
                g = vg_ref[rs, GLA_V + h * GLA_DV:GLA_V + (h + 1) * GLA_DV].astype(F32)
                yg_ref[rs, hs] = (oh * ro * ng_ref[...] * g * _sigmoid(g)).astype(BF16)
            kv = _dot(vb, kh, _TN)
            decay = jnp.exp(bl)
            for p in range(PAIRS):
                state_ref[p] = state[p] * decay[:, 128 * p:128 * (p + 1)] + _compact_state(kv, p)
            return carry

        lax.fori_loop(0, sc, chunk, 0, unroll=True)

    sg = lambda w: pl.BlockSpec((seg, w), lambda b, s: (b * n_seg + s, 0))
    return pl.pallas_call(
        body, name="fwd_gla", grid=(n_ex, n_seg),
        in_specs=[sg(2 * GLA_K), sg(2 * GLA_V), sg(RANK_P), _fixed((RANK_P, GLA_K)), _fixed((1, GLA_K)), _fixed((1, GLA_DV)),
                  _fixed((8, 128))],
        out_specs=[sg(GLA_V), sg(GLA_V), pl.BlockSpec((sc, PAIRS, GLA_DV, 128), lambda b, s: (b * n_seg + s, 0, 0, 0))],
        out_shape=[jax.ShapeDtypeStruct((rows, GLA_V), BF16), jax.ShapeDtypeStruct((rows, GLA_V), F32),
                   jax.ShapeDtypeStruct((n_ex * n_chunk, PAIRS, GLA_DV, 128), F32)],
        scratch_shapes=[pltpu.VMEM((PAIRS, GLA_DV, 128), F32)],
        compiler_params=_params(("parallel", "arbitrary")),
    )(qk, vg, lr, w2p, gb, ng, token)


def _bwd_gla(qk, vg, lr, o, st, dyg, w2p, gb, ng, yc, yg, dh1b, token, n_ex):
    rows = qk.shape[0]
    lp = rows // n_ex
    n_chunk = lp // CHUNK
    sc = _seg_chunks(n_chunk)
    n_seg = n_chunk // sc
    seg = sc * CHUNK

    def body(qk_ref, vg_ref, lr_ref, o_ref, st_ref, dyg_ref, w2_ref, gb_ref, ng_ref, yc_ref, yg_ref, dh1_ref, token_ref,
             dqk_ref, dvg_ref, dlr_ref, dw2_ref, dvec_ref, dwo_ref, gt_ref, dz_ref, dwo_acc):
        step = pl.program_id(1)
        sidx = n_seg - 1 - step
        first = (step == 0) & (pl.program_id(0) == 0)

        @pl.when(step == 0)
        def _():
            gt_ref[...] = jnp.zeros_like(gt_ref)

        @pl.when(first)
        def _():
            dw2_ref[...] = jnp.zeros_like(dw2_ref)
            dvec_ref[...] = jnp.zeros_like(dvec_ref)
            dwo_acc[...] = jnp.zeros_like(dwo_acc)

        d1 = dh1_ref[...]
        dwo_acc[0:C_CONV, :] += _dot(yc_ref[...], d1, _TN)
        dwo_acc[C_CONV:D, :] += _dot(yg_ref[...], d1, _TN)

        @pl.when((step == n_seg - 1) & (pl.program_id(0) == n_ex - 1))
        def _():
            dwo_ref[...] = dwo_acc[...].astype(BF16)

        causal = _causal_heads()
        k_mask = _block_mask((GLA_H * CHUNK, GLA_K), CHUNK, GLA_DK)
        v_mask = _block_mask((GLA_H * CHUNK, GLA_V), CHUNK, GLA_DV)
        last_row = lax.broadcasted_iota(jnp.int32, (CHUNK, 1), 0) == CHUNK - 1
        ng = ng_ref[...]

        def chunk(ii, dng):
            ci = sc - 1 - ii
            base = pl.multiple_of(ci * CHUNK, CHUNK)
            rs = pl.ds(base, CHUNK)
            z, live, bcum = _gate_terms(lr_ref, w2_ref, gb_ref, rs, (sidx * sc + ci) * CHUNK)
            bl = bcum[CHUNK - 1:CHUNK, :]
            ebl = jnp.exp(bl)
            q = qk_ref[rs, 0:GLA_K].astype(F32)
            k = qk_ref[rs, GLA_K:2 * GLA_K].astype(F32)
            eb = jnp.exp(bcum)
            enb = jnp.exp(-bcum)
            ehb = jnp.exp(bl - bcum)
            qt = q * (GLA_DK ** -0.5) * eb
            kt = k * enb
            kh = k * ehb
            qtb = qt.astype(BF16)
            vb = vg_ref[rs, 0:GLA_V].astype(BF16)
            k_rows = _per_head_rows(kt.astype(BF16), k_mask)
            v_rows = _per_head_rows(vb, v_mask)
            gt = [gt_ref[p] for p in range(PAIRS)]
            gtb = _expand_state([g_.astype(BF16) for g_ in gt])
            s_in = [st_ref[ci, p] for p in range(PAIRS)]
            dos = []
            for h in range(GLA_H):
                hs = slice(h * GLA_DV, (h + 1) * GLA_DV)
                gs = slice(GLA_V + h * GLA_DV, GLA_V + (h + 1) * GLA_DV)
                oh = o_ref[rs, hs]
                ro = lax.rsqrt(jnp.mean(oh * oh, axis=-1, keepdims=True) + RMS_EPS)
                on = oh * ro
                g = vg_ref[rs, gs].astype(F32)
                sg = _sigmoid(g)
                dout = dyg_ref[rs, hs]
                dvg_ref[rs, gs] = (dout * on * ng * (sg * (1.0 + g * (1.0 - sg)))).astype(BF16)
                dw = dout * g * sg
                dng = dng + jnp.sum(dw * on, axis=0, keepdims=True)
                don = dw * ng
                dos.append((ro * (don - on * jnp.mean(don * on, axis=-1, keepdims=True))).astype(BF16))
            dob = jnp.concatenate(dos, axis=1)
            a = jnp.where(causal, _dot(qtb, k_rows, _NT), 0.0).astype(BF16)
            da = jnp.where(causal, _dot(dob, v_rows, _NT), 0.0).astype(BF16)
            dv = _fold_heads(_dot(a, dob, _TN), GLA_DV) + _dot(kh.astype(BF16), gtb, _NT)
            dvg_ref[rs, 0:GLA_V] = dv.astype(BF16)
            dkh = _dot(vb, gtb)
            dqt = _dot(da, k_rows) + _dot(dob, _expand_state([s_.astype(BF16) for s_ in s_in]))
            dkt = _fold_heads(_dot(da, qtb, _TN), GLA_DK)
            dbl = jnp.concatenate([jnp.sum(gt[p] * s_in[p], axis=0, keepdims=True) for p in range(PAIRS)], axis=1) * ebl
            dbl = dbl + jnp.sum(dkh * kh, axis=0, keepdims=True)
            dqk_ref[rs, 0:GLA_K] = (dqt * (GLA_DK ** -0.5) * eb).astype(BF16)
            dqk_ref[rs, GLA_K:2 * GLA_K] = (dkt * enb + dkh * ehb).astype(BF16)
            db = dqt * qt - dkt * kt - dkh * kh
            db = jnp.where(last_row, db + dbl, db)
            dla = jnp.where(live, _rev_cumsum_rows(db), 0.0)
            dz_ref[rs, :] = dla * (1.0 / TAU) * (1.0 - _sigmoid(z))
            dstate = _dot(dob, qtb, _TN)
            for p in range(PAIRS):
                gt_ref[p] = _compact_state(dstate, p) + gt[p] * ebl[:, 128 * p:128 * (p + 1)]
            return dng

        dng = lax.fori_loop(0, sc, chunk, jnp.zeros((1, GLA_DV), F32), unroll=True)
        dz = dz_ref[...]
        dzb = dz.astype(BF16)
        dlr_ref[...] = _dot(dzb, w2_ref[...], _NT).astype(BF16)
        dw2_ref[...] += _dot(lr_ref[...].astype(BF16), dzb, _TN)
        dvec_ref[0:1, :] += jnp.sum(dz, axis=0, keepdims=True)
        dvec_ref[1:2, 0:GLA_DV] += dng

    sg_ = lambda w: pl.BlockSpec((seg, w), lambda b, s: (b * n_seg + n_seg - 1 - s, 0))
    return pl.pallas_call(
        body, name="bwd_gla", grid=(n_ex, n_seg),
        in_specs=[sg_(2 * GLA_K), sg_(2 * GLA_V), sg_(RANK_P), sg_(GLA_V),
                  pl.BlockSpec((sc, PAIRS, GLA_DV, 128), lambda b, s: (b * n_seg + n_seg - 1 - s, 0, 0, 0)), sg_(GLA_V),
                  _fixed((RANK_P, GLA_K)), _fixed((1, GLA_K)), _fixed((1, GLA_DV)), sg_(C_CONV), sg_(GLA_V), sg_(D),
                  _fixed((8, 128))],
        out_specs=[sg_(2 * GLA_K), sg_(2 * GLA_V), sg_(RANK_P), _fixed((RANK_P, GLA_K)), _fixed((8, GLA_K)),
                   _fixed((D, D))],
        out_shape=[jax.ShapeDtypeStruct((rows, 2 * GLA_K), BF16), jax.ShapeDtypeStruct((rows, 2 * GLA_V), BF16),
                   jax.ShapeDtypeStruct((rows, RANK_P), BF16), jax.ShapeDtypeStruct((RANK_P, GLA_K), F32),
                   jax.ShapeDtypeStruct((8, GLA_K), F32), jax.ShapeDtypeStruct((D, D), BF16)],
        scratch_shapes=[pltpu.VMEM((PAIRS, GLA_DV, 128), F32), pltpu.VMEM((seg, GLA_K), F32), pltpu.VMEM((D, D), F32)],
        compiler_params=_params(("arbitrary", "arbitrary")),
    )(qk, vg, lr, o, st, dyg, w2p, gb, ng, yc, yg, dh1b, token)


def _pad_rows(x, tgt):
    return jnp.pad(x, ((0, 0), (LEAD, 0), (0, 0))), jnp.pad(tgt, ((0, 0), (LEAD, 0), (0, 0)))


def _local_step(h0, tgt_p, p, pass_on, late_weights, send_early):
    n_ex, lp, _ = h0.shape
    rows = n_ex * lp
    meta = jnp.broadcast_to(p["meta"][None], (n_ex, N_META, D))
    h0 = lax.dynamic_update_slice(h0, meta, (0, ZROWS, 0)).reshape(rows, D)
    tgt_p = tgt_p.reshape(rows, D)

    uc, qk, vg, lr, n1 = _fwd_inproj(h0, p["g1"], p["w_in"])
    conv_args = (uc, p["conv_w"], p["conv_b"], p["ln_g"], p["ln_b"])
    some = _fwd_conv(*conv_args, p["token"], n_ex, 0, n_ex // 2, None)
    yg, o, st = _fwd_gla(qk, vg, lr, p["w2"], p["gb"], p["ng"], p["token"], n_ex)
    token = pass_on((some[1], yg))
    ypre, yc = _fwd_conv(*conv_args, token, n_ex, n_ex // 2, n_ex - n_ex // 2, some)
    w_out, wg, wu, wd = late_weights(yc)
    n2, f, da, db, dh2, dh1, dh1b, dyc, dyg, part = _mid_rows(
        yc, yg, h0, tgt_p, w_out, wg, wu, wd, p["g2"], p["g3"], token, lp)
    g = {}
    token = send_early("ffn", [_matmul_tn(a_, b_, name).reshape(N_DEV, FF_S, D) for a_, b_, name in (
        (da, n2, "dw_gate"), (db, n2, "dw_up"), (f, dh2, "dw_down"))])
    dqk, dvg, dlr, g["w2"], g["gla_vec"], dw_out = _bwd_gla(
        qk, vg, lr, o, st, dyg, p["w2"], p["gb"], p["ng"], yc, yg, dh1b, token, n_ex)
    token = send_early("out", [dw_out.reshape(N_DEV, W_OUT_S, D)])
    duc, g["conv_w"], g["conv_vec"] = _bwd_conv(uc, ypre, dyc, p["conv_w"], p["ln_g"], p["ln_b"], token, n_ex)
    token = send_early("in", [_dw_blocked(n1, [duc, dqk, dvg, dlr], W_IN_S, "dw_in")])
    grad_x, g["in_vec"], g["meta"] = _bwd_inproj(duc, dqk, dvg, dlr, dh1, h0, p["w_in"], p["g1"], token, lp)
    g["ffn_vec"] = part
    return grad_x, g


W_IN_S = D_IN // N_DEV
W_OUT_S = D // N_DEV
FF_S = D_FF // N_DEV
CONV_S = C_CONV // N_DEV
GATE_S = GLA_K // N_DEV
SMALL_PACK = 64
CONV_ROW = 16
GATE_ROW = 48
VEC_ROWS = 16
_VEC_ROWS = (("norm_mix_g", D), ("conv_b", C_CONV), ("conv_ln_g", C_CONV), ("conv_ln_b", C_CONV), ("gla_gate_b", GLA_K),
             ("gla_norm_g", GLA_DV), ("norm_ffn_g", D), ("norm_final_g", D))
LOSS_ROW = len(_VEC_ROWS)


def _position():
    return lax.axis_index("x"), lax.axis_index("y"), lax.axis_index("c")


def _any():
    return pl.BlockSpec(memory_space=pl.ANY)


def _stage(mats, meta, conv_w, w2):
    n_t = len(mats) + 1

    def body(*refs):
        ins = refs[0:n_t - 1]
        meta_ref, cw_ref, w2_ref = refs[n_t - 1:n_t + 2]
        lands = refs[n_t + 2:2 * n_t + 2]
        shards = refs[2 * n_t + 2:3 * n_t + 2]
        sems = refs[3 * n_t + 2]
        for s_ref, w_ref in zip(shards, ins):
            s_ref[...] = w_ref[...].astype(BF16)
        sp = shards[n_t - 1]
        sp[...] = jnp.zeros_like(sp)
        sp[0:N_META, :] = meta_ref[...]
        sp[CONV_ROW:CONV_ROW + CONV_W, 0:CONV_S] = cw_ref[...]
        sp[GATE_ROW:GATE_ROW + RANK, 0:GATE_S] = w2_ref[...]
        x, y, c = _position()
        mine = [pltpu.make_async_copy(shards[t], lands[t].at[4 * x + 2 * y + c], sems.at[t]) for t in range(n_t)]
        for cp in mine:
            cp.start()
        for cp in mine:
            cp.wait()

    shard_shapes = [jax.ShapeDtypeStruct(m.shape, BF16) for m in mats] + [jax.ShapeDtypeStruct((SMALL_PACK, 128), F32)]
    res = pl.pallas_call(
        body, name="stage",
        out_shape=[jax.ShapeDtypeStruct((N_DEV,) + s.shape, s.dtype) for s in shard_shapes] + shard_shapes,
        in_specs=[_whole_vmem()] * (n_t + 2), out_specs=[_any()] * n_t + [_whole_vmem()] * n_t,
        scratch_shapes=[pltpu.SemaphoreType.DMA((n_t,))],
        compiler_params=pltpu.CompilerParams(vmem_limit_bytes=VMEM_LIMIT),
    )(*mats, meta, conv_w, w2)
    return res[0:n_t], res[n_t:]


_HBM = pl.BlockSpec(memory_space=pltpu.HBM)
_SEM = pl.BlockSpec(memory_space=pltpu.SEMAPHORE)
_EFFECT = pltpu.SideEffectType.DATAFLOW_SIDE_EFFECTING


_N_ROUTES = {"scatter": 7, "first": 4, "forward": 3}


def _routes(mode):
    x, y, c = _position()
    me = 4 * x + 2 * y + c
    if mode == "scatter":
        out = []
        for k in range(1, N_DEV):
            px = 1 - x if k & 4 else x
            py = 1 - y if k & 2 else y
            pc = 1 - c if k & 1 else c
            out.append(((px, py, pc), 4 * px + 2 * py + pc, me))
        return out
    if mode == "first":
        return [(pos, None, me) for pos in ((x, y, 1 - c), (1 - x, y, c), (x, 1 - y, c), (1 - x, 1 - y, c))]
    assert mode == "forward"
    return [((x, y, 1 - c), 4 * px + 2 * py + c, 4 * px + 2 * py + c) for px, py in ((1 - x, y), (x, 1 - y), (1 - x, 1 - y))]


def _route_copies(mode, n, src_refs, land_refs, send_sems, recv_sems):
    nr = _N_ROUTES[mode]
    for i, (pos, src_blk, dst_blk) in enumerate(_routes(mode)):
        for t in range(n):
            src = land_refs[t] if mode == "forward" else src_refs[t]
            yield pltpu.make_async_remote_copy(
                src_ref=src if src_blk is None else src.at[src_blk], dst_ref=land_refs[t].at[dst_blk],
                send_sem=send_sems.at[nr * t + i], recv_sem=recv_sems.at[nr * t + i], device_id=pos, device_id_type=MESH)


def _in_hbm(a):
    return pltpu.with_memory_space_constraint(a, pltpu.HBM)


def _send_start(name, groups, mode, after):
    sizes = [(len(s), len(l)) for s, l in groups]
    bufs = [b for s, l in groups for b in list(s) + list(l)]
    nb, ng = len(bufs), len(groups)

    def body(*refs):
        sems = refs[nb + 1:nb + 1 + 2 * ng]
        token = refs[2 * nb + 2 * ng + 1]
        off = 0
        for gi, (ns, n) in enumerate(sizes):
            for cp in _route_copies(mode, n, refs[off:off + ns], refs[off + ns:off + ns + n], sems[2 * gi], sems[2 * gi + 1]):
                cp.start()
            off += ns + n
        token[...] = jnp.zeros_like(token)

    res = pl.pallas_call(
        body, name=name,
        out_shape=(*[pltpu.SemaphoreType.DMA((_N_ROUTES[mode] * n,)) for _, n in sizes for _ in range(2)],
                   *[pltpu.HBM(b.shape, b.dtype) for b in bufs], jax.ShapeDtypeStruct((8, 128), F32)),
        in_specs=[_HBM] * nb + [_any()], out_specs=(*[_SEM] * (2 * ng), *[_HBM] * nb, _whole_vmem()),
        input_output_aliases={i: 2 * ng + i for i in range(nb)},
        compiler_params=pltpu.CompilerParams(has_side_effects=_EFFECT),
    )(*[_in_hbm(b) for b in bufs], after)
    handles, off = [], 2 * ng
    for gi, (ns, n) in enumerate(sizes):
        handles.append((res[2 * gi], res[2 * gi + 1], res[off:off + ns], res[off + ns:off + ns + n]))
        off += ns + n
    return handles, res[2 * ng + nb]


def _send_wait(name, send_sems, recv_sems, srcs, lands, mode, after):
    n, ns = len(lands), len(srcs)
    after = after if isinstance(after, tuple) else (after,)

    def body(*refs):
        src_refs, land_refs = refs[0:ns], refs[ns:ns + n]
        send_sems, recv_sems = refs[ns + n:ns + n + 2]
        for cp in _route_copies(mode, n, src_refs, land_refs, send_sems, recv_sems):
            cp.wait_send()
            cp.wait_recv()

    bufs = list(srcs) + list(lands)
    res = pl.pallas_call(
        body, name=name,
        out_shape=tuple(pltpu.HBM(b.shape, b.dtype) for b in bufs),
        in_specs=[_HBM] * len(bufs) + [_SEM, _SEM] + [_any()] * len(after), out_specs=tuple([_HBM] * len(bufs)),
        input_output_aliases={i: i for i in range(len(bufs))},
        compiler_params=pltpu.CompilerParams(has_side_effects=_EFFECT),
    )(*bufs, send_sems, recv_sems, *after)
    return res[0:ns], res[ns:ns + n]


def _unshard_in(a_in, a_small, token):
    def body(a_ref, s_ref, token_ref, w_ref, meta_ref, cw_ref, w2_ref):
        w_ref[:, D_IN:D_INP] = jnp.zeros((D, D_INP - D_IN), BF16)
        w2_ref[...] = jnp.zeros_like(w2_ref)
        for d in range(N_DEV):
            w_ref[:, d * W_IN_S:(d + 1) * W_IN_S] = a_ref[d]
            meta_ref[:, d * 128:(d + 1) * 128] = s_ref[d, 0:N_META, :]
            cw_ref[:, d * CONV_S:(d + 1) * CONV_S] = s_ref[d, CONV_ROW:CONV_ROW + 32, 0:CONV_S]
            w2_ref[0:RANK, d * GATE_S:(d + 1) * GATE_S] = s_ref[d, GATE_ROW:GATE_ROW + RANK, 0:GATE_S].astype(BF16)

    return pl.pallas_call(
        body, name="unshard_in",
        out_shape=[jax.ShapeDtypeStruct((D, D_INP), BF16), jax.ShapeDtypeStruct((N_META, D), F32),
                   jax.ShapeDtypeStruct((32, C_CONV), F32), jax.ShapeDtypeStruct((RANK_P, GLA_K), BF16)],
        compiler_params=pltpu.CompilerParams(vmem_limit_bytes=VMEM_LIMIT),
    )(a_in, a_small, token)


def _pack_small(g):
    def body(meta_ref, cw_ref, w2_ref, in_vec, ffn_vec, conv_vec, gla_vec, sp, vp):
        sp[...] = jnp.zeros_like(sp)
        vp[...] = jnp.zeros_like(vp)
        for d in range(N_DEV):
            sp[d, 0:N_META, :] = meta_ref[:, d * 128:(d + 1) * 128]
            sp[d, CONV_ROW:CONV_ROW + 32, 0:CONV_S] = cw_ref[:, d * CONV_S:(d + 1) * CONV_S]
            sp[d, GATE_ROW:GATE_ROW + RANK, 0:GATE_S] = w2_ref[0:RANK, d * GATE_S:(d + 1) * GATE_S]
            vp[d, 0:1, :] = in_vec[0:1, :]
            vp[d, 1:4, 0:C_CONV] = conv_vec[0:3, :]
            vp[d, 4:5, 0:GLA_K] = gla_vec[0:1, :]
            vp[d, 5:6, 0:GLA_DV] = gla_vec[1:2, 0:GLA_DV]
            vp[d, 6:7, :] = ffn_vec[1:2, :]
            vp[d, 7:8, :] = ffn_vec[0:1, :]
            vp[d, LOSS_ROW:LOSS_ROW + 1, :] = ffn_vec[2:3, :]

    return pl.pallas_call(
        body, name="pack_small",
        out_shape=[jax.ShapeDtypeStruct((N_DEV, SMALL_PACK, 128), F32), jax.ShapeDtypeStruct((N_DEV, VEC_ROWS, D), F32)],
    )(g["meta"], g["conv_w"], g["w2"], g["in_vec"], g["ffn_vec"], g["conv_vec"], g["gla_vec"])


def _adamw(w, g, m, v):
    m = ADAM_B1 * m + (1.0 - ADAM_B1) * g
    v = ADAM_B2 * v + (1.0 - ADAM_B2) * (g * g)
    m_hat = m / (1.0 - ADAM_B1 ** ADAM_STEP)
    v_hat = v / (1.0 - ADAM_B2 ** ADAM_STEP)
    return -ADAM_LR * (m_hat / (jnp.sqrt(v_hat) + ADAM_EPS) + ADAM_WD * w), m, v


def _update_matrix(recv, own, me, w, m, v, name):
    _, r, c = recv.shape
    tr = _row_tile(r, 256)

    def body(me_ref, recv_ref, own_ref, w_ref, m_ref, v_ref, g_ref, d_ref, nm_ref, nv_ref):
        g = jnp.zeros((tr, c), F32)
        for s in range(N_DEV):
            g = g + jnp.where(me_ref[0] == s, own_ref[...], recv_ref[s]).astype(F32)
        g_ref[...] = g
        d_ref[...], nm_ref[...], nv_ref[...] = _adamw(w_ref[...], g, m_ref[...], v_ref[...])

    one = pl.BlockSpec((None, tr, c), lambda i, me_ref: (0, i, 0))
    return pl.pallas_call(
        body, name=name,
        grid_spec=pltpu.PrefetchScalarGridSpec(
            num_scalar_prefetch=1, grid=(r // tr,),
            in_specs=[pl.BlockSpec((N_DEV, tr, c), lambda i, me_ref: (0, i, 0)),
                      pl.BlockSpec((None, tr, c), lambda i, me_ref: (me_ref[0], i, 0)), one, one, one],
            out_specs=[one] * 4),
        out_shape=[jax.ShapeDtypeStruct((1, r, c), F32)] * 4,
        compiler_params=_params(("parallel",)),
    )(me, recv, own, w, m, v)


_SMALL = ("meta_tokens", "conv_w", "gla_w_gate2") + tuple(n for n, _ in _VEC_ROWS)


def _update_small(me, srecv, vrecv, sown, vown, w, m, v):
    n = len(_SMALL)

    def body(*refs):
        me_ref, s_ref, v_ref, so_ref, vo_ref = refs[0:5]
        w_refs, m_refs, v_refs = refs[5:5 + n], refs[5 + n:5 + 2 * n], refs[5 + 2 * n:5 + 3 * n]
        outs = refs[5 + 3 * n:]
        ssum = jnp.zeros((SMALL_PACK, 128), F32)
        vsum = jnp.zeros((VEC_ROWS, D), F32)
        for s in range(N_DEV):
            ssum = ssum + jnp.where(me_ref[0] == s, so_ref[s], s_ref[s])
            vsum = vsum + jnp.where(me_ref[0] == s, vo_ref[s], v_ref[s])
        grads = [ssum[0:N_META, :], ssum[CONV_ROW:CONV_ROW + CONV_W, 0:CONV_S], ssum[GATE_ROW:GATE_ROW + RANK, 0:GATE_S]]
        grads += [vsum[i:i + 1, 0:width] for i, (_, width) in enumerate(_VEC_ROWS)]
        for i, g in enumerate(grads):
            d, nm, nv = _adamw(w_refs[i][...], g, m_refs[i][...], v_refs[i][...])
            outs[i][...] = g
            outs[n + i][...] = d
            outs[2 * n + i][...] = nm
            outs[3 * n + i][...] = nv
        outs[4 * n][...] = vsum[LOSS_ROW:LOSS_ROW + 1, 0:128]

    shapes = [jax.ShapeDtypeStruct(t.shape, F32) for t in w]
    res = pl.pallas_call(
        body, name="update_small", out_shape=shapes * 4 + [jax.ShapeDtypeStruct((1, 128), F32)],
        in_specs=[pl.BlockSpec(memory_space=pltpu.SMEM)] + [_whole_vmem()] * (4 + 3 * n),
    )(me, srecv, vrecv, sown, vown, *w, *m, *v)
    return res[0:n], res[n:2 * n], res[2 * n:3 * n], res[3 * n:4 * n], res[4 * n]


_WEIGHTS = ("meta_tokens", "norm_mix_g", "w_in", "conv_w", "conv_b", "conv_ln_g", "conv_ln_b", "gla_w_gate2", "gla_gate_b",
            "gla_norm_g", "w_out", "norm_ffn_g", "w_ffn_gate", "w_ffn_up", "w_ffn_down", "norm_final_g")
_MATRICES = ("w_in", "w_out", "w_ffn_gate", "w_ffn_up", "w_ffn_down")
_TRANSPOSED = ("w_ffn_gate", "w_ffn_up")


def kernel(x, meta_tokens, norm_mix_g, w_in, conv_w, conv_b, conv_ln_g, conv_ln_b, gla_w_gate2, gla_gate_b, gla_norm_g, w_out, norm_ffn_g, w_ffn_gate, w_ffn_up, w_ffn_down, norm_final_g, loss_target, m_meta_tokens, m_norm_mix_g, m_w_in, m_conv_w, m_conv_b, m_conv_ln_g, m_conv_ln_b, m_gla_w_gate2, m_gla_gate_b, m_gla_norm_g, m_w_out, m_norm_ffn_g, m_w_ffn_gate, m_w_ffn_up, m_w_ffn_down, m_norm_final_g, v_meta_tokens, v_norm_mix_g, v_w_in, v_conv_w, v_conv_b, v_conv_ln_g, v_conv_ln_b, v_gla_w_gate2, v_gla_gate_b, v_gla_norm_g, v_w_out, v_norm_ffn_g, v_w_ffn_gate, v_w_ffn_up, v_w_ffn_down, v_norm_final_g):
    given = dict(locals())
    two_d = lambda a: a.reshape(1, -1) if a.ndim == 1 else a.reshape(a.shape[-2:])
    fams = [{n: given[pre + n] for n in _WEIGHTS} for pre in ("", "m_", "v_")]
    for f in fams:
        for n in _TRANSPOSED:
            f[n] = f[n].transpose(0, 2, 1)
    w = fams[0]

    lands, shards = _stage([two_d(w[n]) for n in _MATRICES], w["meta_tokens"], two_d(w["conv_w"]), two_d(w["gla_w_gate2"]))
    soon, later = (0, 5), (1, 2, 3, 4)
    pick = lambda seq, idx: [seq[i] for i in idx]
    (first, ffn_first), started = _send_start(
        "gather_first_start", [(pick(shards, soon), pick(lands, soon)), (pick(shards, later), pick(lands, later))],
        "first", norm_mix_g)
    h0, tgt_p = _pad_rows(x, loss_target)
    _, arrived = _send_wait("gather_first_wait", *first, "first", (h0, tgt_p, started))
    (forward,), token = _send_start("gather_forward_start", [([], arrived)], "forward", started)
    _, (a_in, a_small) = _send_wait("gather_forward_wait", *forward, "forward", token)
    w_in, meta, conv_taps, w2 = _unshard_in(a_in, a_small, token)
    p = dict(meta=meta, conv_w=conv_taps, w2=w2, w_in=w_in, g1=norm_mix_g, conv_b=conv_b, ln_g=conv_ln_g, ln_b=conv_ln_b,
             gb=gla_gate_b, ng=gla_norm_g, g2=norm_ffn_g, g3=two_d(norm_final_g), token=token)
    passed = {}

    def pass_on(after):
        _, arrived_ffn = _send_wait("gather_ffn_first_wait", *ffn_first, "first", after)
        (passed["sent"],), token = _send_start("gather_ffn_forward_start", [([], arrived_ffn)], "forward", norm_mix_g)
        return token

    def late_weights(after):
        _, (a_out, a_g, a_u, a_d) = _send_wait("gather_ffn_forward_wait", *passed["sent"], "forward", after)
        return a_out.reshape(D, D), a_g.reshape(D_FF, D), a_u.reshape(D_FF, D), a_d.reshape(D_FF, D)

    sent = {}

    def send_early(tag, mats):
        landing = [_in_hbm(lax.empty(m_.shape, m_.dtype)) for m_ in mats]
        (sent[tag],), token = _send_start("scatter_" + tag + "_start", [(mats, landing)], "scatter", norm_mix_g)
        return token

    grad_x, g = _local_step(h0, tgt_p, p, pass_on, late_weights, send_early)

    token = send_early("small", list(_pack_small(g)))
    x_, y_, c_ = _position()
    me = (4 * x_ + 2 * y_ + c_).astype(jnp.int32).reshape(1)
    res = {}
    for tag, names in (("ffn", ("w_ffn_gate", "w_ffn_up", "w_ffn_down")), ("out", ("w_out",)), ("in", ("w_in",))):
        own, recv = _send_wait("scatter_" + tag + "_wait", *sent[tag], "scatter", token)
        for n, o_, r_ in zip(names, own, recv):
            res[n] = _update_matrix(r_, o_, me, *[f[n] for f in fams], "update_" + n)
            token = res[n][1]
    (sown, vown), (srecv, vrecv) = _send_wait("scatter_small_wait", *sent["small"], "scatter", token)
    small = _update_small(me, srecv, vrecv, sown, vown, *[[two_d(f[n]) for n in _SMALL] for f in fams])
    for i, n in enumerate(_SMALL):
        res[n] = [fam[i].reshape(w[n].shape) for fam in small[0:4]]
    for n in _TRANSPOSED:
        res[n] = [t.transpose(0, 2, 1) for t in res[n]]
    outs = [small[4][0, 0], grad_x]
    for k in range(4):
        outs += [res[n][k] for n in _WEIGHTS]
    return tuple(outs)
```

```python
import functools

import jax
import jax.numpy as jnp
from jax import lax
from jax.experimental import pallas as pl
from jax.experimental.pallas import tpu as pltpu

F32 = jnp.float32
BF16 = jnp.bfloat16

D = 1024
N_META = 16
C_CONV = 512
CONV_W = 31
GLA_H = 4
GLA_DK = 64
GLA_DV = 128
GLA_K = GLA_H * GLA_DK
GLA_V = GLA_H * GLA_DV
RANK = 16
RANK_P = 128
TAU = 16.0
CHUNK = 64
LEAD = CHUNK
ZROWS = LEAD - N_META
D_IN = 2 * C_CONV + 2 * GLA_K + 2 * GLA_V + RANK
D_INP = D_IN - RANK + RANK_P
D_FF = 2816
FF_CHUNK = 1408
FF_SPLIT = (0, 1536, D_FF)
RMS_EPS = 1e-6
LN_EPS = 1e-5
N_DEV = 8

ADAM_LR = 0.001
ADAM_B1 = 0.9
ADAM_B2 = 0.999
ADAM_EPS = 1e-08
ADAM_WD = 0.01
ADAM_STEP = 10

VMEM_LIMIT = 60 * 1024 * 1024
ROW_TILE = 1056
FFN_ROW_TILE = 352
DW_ROW_TILE = 1408
MESH = pl.DeviceIdType.MESH

_NN = (((1,), (0,)), ((), ()))
_NT = (((1,), (1,)), ((), ()))
_TN = (((0,), (0,)), ((), ()))


def _dot(a, b, dims=_NN):
    return lax.dot_general(a, b, dims, preferred_element_type=F32)


def _sigmoid(x):
    return 1.0 / (1.0 + jnp.exp(-x))


def _row_tile(rows, target):
    best = None
    for t in range(16, min(rows, target) + 1, 16):
        if rows % t == 0:
            best = t
    assert best is not None, rows
    return best


def _params(sem=None):
    return pltpu.CompilerParams(dimension_semantics=sem, vmem_limit_bytes=VMEM_LIMIT)


def _whole_vmem():
    return pl.BlockSpec(memory_space=pltpu.VMEM)


def _rows(tm, width):
    return pl.BlockSpec((tm, width), lambda i: (i, 0))


def _fixed(shape):
    return pl.BlockSpec(shape, lambda *_: (0,) * len(shape))


def _fwd_inproj(h0, g1, w_in):
    rows = h0.shape[0]
    tm = _row_tile(rows, ROW_TILE)

    def body(h_ref, g_ref, w_ref, uc_ref, qk_ref, vg_ref, lr_ref, n1_ref):
        h = h_ref[...]
        r = lax.rsqrt(jnp.mean(h * h, axis=-1, keepdims=True) + RMS_EPS)
        n = (h * r * g_ref[...]).astype(BF16)
        n1_ref[...] = n
        uc_ref[...] = _dot(n, w_ref[:, 0:1024]).astype(BF16)
        qk_ref[...] = _dot(n, w_ref[:, 1024:1536]).astype(BF16)
        vg_ref[...] = _dot(n, w_ref[:, 1536:2560]).astype(BF16)
        lr_ref[...] = _dot(n, w_ref[:, 2560:2688]).astype(BF16)

    return pl.pallas_call(
        body, name="fwd_inproj", grid=(rows // tm,),
        in_specs=[_rows(tm, D), _fixed((1, D)), _whole_vmem()],
        out_specs=[_rows(tm, 1024), _rows(tm, 512), _rows(tm, 1024), _rows(tm, RANK_P), _rows(tm, D)],
        out_shape=[jax.ShapeDtypeStruct((rows, 1024), BF16), jax.ShapeDtypeStruct((rows, 512), BF16),
                   jax.ShapeDtypeStruct((rows, 1024), BF16), jax.ShapeDtypeStruct((rows, RANK_P), BF16),
                   jax.ShapeDtypeStruct((rows, D), BF16)],
        compiler_params=_params(("parallel",)),
    )(h0, g1, w_in)


def _mid_rows(yc, yg, h0, tgt, w_out, wg, wu, wd, g2, g3, token, rows_per_example):
    rows = h0.shape[0]
    tm = _row_tile(rows, FFN_ROW_TILE)
    ff_blocks = [slice(lo, hi) for lo, hi in zip(FF_SPLIT[:-1], FF_SPLIT[1:])]

    def body(yc_ref, yg_ref, h0_ref, t_ref, wo_ref, wg_ref, wu_ref, wd_ref, g2_ref, g3_ref, token_ref,
             n2_ref, f_ref, da_ref, db_ref, dh2_ref, dh1_ref, dh1b_ref, dyc_ref, dyg_ref, part_ref):
        i = pl.program_id(0)
        h1 = h0_ref[...] + _dot(yc_ref[...], wo_ref[0:C_CONV, :]) + _dot(yg_ref[...], wo_ref[C_CONV:D, :])
        r2 = lax.rsqrt(jnp.mean(h1 * h1, axis=-1, keepdims=True) + RMS_EPS)
        xh2 = h1 * r2
        n2 = (xh2 * g2_ref[...]).astype(BF16)
        n2_ref[...] = n2
        y2 = jnp.zeros((tm, D), F32)
        for cs in ff_blocks:
            a = _dot(n2, wg_ref[cs, :], _NT)
            b = _dot(n2, wu_ref[cs, :], _NT)
            f = (a * _sigmoid(a) * b).astype(BF16)
            f_ref[:, cs] = f
            da_ref[:, cs] = a.astype(BF16)
            db_ref[:, cs] = b.astype(BF16)
            y2 = y2 + _dot(f, wd_ref[cs, :])
        h2 = h1 + y2
        r3 = lax.rsqrt(jnp.mean(h2 * h2, axis=-1, keepdims=True) + RMS_EPS)
        xh3 = h2 * r3
        g3 = g3_ref[...]
        pos = (i * tm + lax.broadcasted_iota(jnp.int32, (tm, 1), 0)) % rows_per_example
        valid = pos >= LEAD
        err = jnp.where(valid, xh3 * g3 - t_ref[...], 0.0)
        loss = 0.5 / D * jnp.sum(jnp.sum(err * err, axis=-1, keepdims=True), axis=0, keepdims=True)
        dy = err * (1.0 / D)
        dg3 = jnp.sum(dy * xh3, axis=0, keepdims=True)
        dxh = dy * g3
        dh2 = r3 * (dxh - xh3 * jnp.mean(dxh * xh3, axis=-1, keepdims=True))
        dh2b = dh2.astype(BF16)
        dh2_ref[...] = dh2b
        dn2 = jnp.zeros((tm, D), F32)
        for cs in ff_blocks:
            df = _dot(dh2b, wd_ref[cs, :], _NT)
            a = da_ref[:, cs].astype(F32)
            b = db_ref[:, cs].astype(F32)
            sg = _sigmoid(a)
            da = (df * b * sg * (1.0 + a * (1.0 - sg))).astype(BF16)
            db = (df * a * sg).astype(BF16)
            da_ref[:, cs] = da
            db_ref[:, cs] = db
            dn2 = dn2 + _dot(da, wg_ref[cs, :]) + _dot(db, wu_ref[cs, :])
        dg2 = jnp.sum(dn2 * xh2, axis=0, keepdims=True)
        dxh2 = dn2 * g2_ref[...]
        dh1 = dh2 + r2 * (dxh2 - xh2 * jnp.mean(dxh2 * xh2, axis=-1, keepdims=True))
        dh1_ref[...] = dh1
        dh1b = dh1.astype(BF16)
        dh1b_ref[...] = dh1b
        dyc_ref[...] = _dot(dh1b, wo_ref[0:C_CONV, :], _NT)
        dyg_ref[...] = _dot(dh1b, wo_ref[C_CONV:D, :], _NT)

        @pl.when(i == 0)
        def _():
            part_ref[...] = jnp.zeros_like(part_ref)

        part_ref[0:1, :] += dg3
        part_ref[1:2, :] += dg2
        part_ref[2:3, :] += jnp.broadcast_to(loss, (1, D))

    return pl.pallas_call(
        body, name="mid_rows", grid=(rows // tm,),
        in_specs=[_rows(tm, C_CONV), _rows(tm, GLA_V), _rows(tm, D), _rows(tm, D), _whole_vmem(), _whole_vmem(),
                  _whole_vmem(), _whole_vmem(), _fixed((1, D)), _fixed((1, D)), _fixed((8, 128))],
        out_specs=[_rows(tm, D), _rows(tm, D_FF), _rows(tm, D_FF), _rows(tm, D_FF), _rows(tm, D), _rows(tm, D),
                   _rows(tm, D), _rows(tm, C_CONV), _rows(tm, GLA_V), _fixed((8, D))],
        out_shape=[jax.ShapeDtypeStruct((rows, D), BF16)] + [jax.ShapeDtypeStruct((rows, D_FF), BF16)] * 3
        + [jax.ShapeDtypeStruct((rows, D), BF16), jax.ShapeDtypeStruct((rows, D), F32),
           jax.ShapeDtypeStruct((rows, D), BF16), jax.ShapeDtypeStruct((rows, C_CONV), F32),
           jax.ShapeDtypeStruct((rows, GLA_V), F32), jax.ShapeDtypeStruct((8, D), F32)],
        compiler_params=_params(("arbitrary",)),
    )(yc, yg, h0, tgt, w_out, wg, wu, wd, g2, g3, token)


def _bwd_inproj(duc, dqk, dvg, dlr, dh1, h0, w_in, g1, token, rows_per_example):
    rows = h0.shape[0]
    n_ex = rows // rows_per_example
    tm = _row_tile(rows_per_example, ROW_TILE)
    tiles_per_example = rows_per_example // tm
    n_steps = rows // tm

    def body(duc_ref, dqk_ref, dvg_ref, dlr_ref, dh1_ref, h_ref, w_ref, g_ref, token_ref, gx_ref, part_ref, dmeta_ref,
             buf_ref, sems):
        dn = (_dot(duc_ref[...], w_ref[:, 0:1024], _NT) + _dot(dqk_ref[...], w_ref[:, 1024:1536], _NT)
              + _dot(dvg_ref[...], w_ref[:, 1536:2560], _NT) + _dot(dlr_ref[...], w_ref[:, 2560:2688], _NT))
        h = h_ref[...]
        r = lax.rsqrt(jnp.mean(h * h, axis=-1, keepdims=True) + RMS_EPS)
        xh = h * r
        dg = jnp.sum(dn * xh, axis=0, keepdims=True)
        dxh = dn * g_ref[...]
        dh0 = dh1_ref[...] + r * (dxh - xh * jnp.mean(dxh * xh, axis=-1, keepdims=True))
        i = pl.program_id(0)

        def copies(step):
            slot, b, j = step % 2, step // tiles_per_example, step % tiles_per_example
            out = [(j == 0, pltpu.make_async_copy(buf_ref.at[slot, pl.ds(LEAD, tm - LEAD)],
                                                   gx_ref.at[b, pl.ds(0, tm - LEAD)], sems.at[slot]))]
            if tiles_per_example > 1:
                out.append((j != 0, pltpu.make_async_copy(
                    buf_ref.at[slot], gx_ref.at[b, pl.ds(pl.multiple_of(jnp.maximum(j * tm - LEAD, 0), 8), tm)],
                    sems.at[slot])))
            return out

        def each(step, act):
            for cond, cp in copies(step):
                pl.when(cond)(functools.partial(act, cp))

        @pl.when(i >= 2)
        def _():
            each(i - 2, lambda cp: cp.wait())

        buf_ref[i % 2] = dh0
        each(i, lambda cp: cp.start())

        @pl.when(i == n_steps - 1)
        def _():
            each(i, lambda cp: cp.wait())
            if n_steps > 1:
                each(i - 1, lambda cp: cp.wait())

        @pl.when(i == 0)
        def _():
            part_ref[...] = jnp.zeros_like(part_ref)
            dmeta_ref[...] = jnp.zeros_like(dmeta_ref)

        part_ref[0:1, :] += dg

        @pl.when(i % tiles_per_example == 0)
        def _():
            dmeta_ref[...] += dh0[ZROWS:LEAD, :]

    return pl.pallas_call(
        body, name="bwd_inproj", grid=(n_steps,),
        in_specs=[_rows(tm, 1024), _rows(tm, 512), _rows(tm, 1024), _rows(tm, RANK_P), _rows(tm, D), _rows(tm, D),
                  _whole_vmem(), _fixed((1, D)), _fixed((8, 128))],
        out_specs=[_any(), _fixed((8, D)), _fixed((N_META, D))],
        out_shape=[jax.ShapeDtypeStruct((n_ex, rows_per_example - LEAD, D), F32), jax.ShapeDtypeStruct((8, D), F32),
                   jax.ShapeDtypeStruct((N_META, D), F32)],
        scratch_shapes=[pltpu.VMEM((2, tm, D), F32), pltpu.SemaphoreType.DMA((2,))],
        compiler_params=_params(("arbitrary",)),
    )(duc, dqk, dvg, dlr, dh1, h0, w_in, g1, token)


def _dw_blocked(a, bs, width, name):
    rows, m = a.shape
    ws = [b.shape[1] for b in bs]
    assert sum(ws) >= N_DEV * width
    tk = _row_tile(rows, DW_ROW_TILE)
    nk = rows // tk

    def body(a_ref, *refs):
        b_refs, o_ref, acc_ref = refs[:len(bs)], refs[len(bs)], refs[len(bs) + 1]
        k = pl.program_id(0)

        @pl.when(k == 0)
        def _():
            acc_ref[...] = jnp.zeros_like(acc_ref)

        at = a_ref[...].T
        off = 0
        for b_ref, w in zip(b_refs, ws):
            acc_ref[:, off:off + w] += _dot(at, b_ref[...])
            off += w

        @pl.when(k == nk - 1)
        def _():
            for d in range(N_DEV):
                o_ref[d] = acc_ref[:, d * width:(d + 1) * width].astype(BF16)

    return pl.pallas_call(
        body, name=name, grid=(nk,),
        in_specs=[_rows(tk, m)] + [_rows(tk, w) for w in ws],
        out_specs=_fixed((N_DEV, m, width)),
        out_shape=jax.ShapeDtypeStruct((N_DEV, m, width), BF16),
        scratch_shapes=[pltpu.VMEM((m, sum(ws)), F32)],
        compiler_params=_params(("arbitrary",)),
    )(a, *bs)


def _matmul_tn(a, b, name):
    rows, m = a.shape
    n = b.shape[1]
    tk = _row_tile(rows, DW_ROW_TILE)
    tn = n if n <= 1024 else FF_CHUNK
    tm_ = m
    assert n % tn == 0 and m % tm_ == 0
    nk = rows // tk

    def body(a_ref, b_ref, o_ref, acc_ref):
        k = pl.program_id(2)

        @pl.when(k == 0)
        def _():
            acc_ref[...] = jnp.zeros_like(acc_ref)

        acc_ref[...] += _dot(a_ref[...], b_ref[...], _TN)

        @pl.when(k == nk - 1)
        def _():
            o_ref[...] = acc_ref[...].astype(BF16)

    return pl.pallas_call(
        body, name=name, grid=(m // tm_, n // tn, nk),
        in_specs=[pl.BlockSpec((tk, tm_), lambda i, j, k: (k, i)), pl.BlockSpec((tk, tn), lambda i, j, k: (k, j))],
        out_specs=pl.BlockSpec((tm_, tn), lambda i, j, k: (i, j)),
        out_shape=jax.ShapeDtypeStruct((m, n), BF16),
        scratch_shapes=[pltpu.VMEM((tm_, tn), F32)],
        compiler_params=_params(("parallel", "parallel", "arbitrary")),
    )(a, b)


HALO = 32
LN_ROWS = 352
LANES = 128


def _shifted(win, offsets):
    for r in range(8):
        js = [j for j, k in enumerate(offsets) if k % 8 == r]
        if js:
            rolled = win if r == 0 else pltpu.roll(win, CHUNK + HALO - r, 0)
            for j in js:
                yield j, rolled[offsets[j] - r:offsets[j] - r + CHUNK]


def _glu_into(uc_ref, vs_ref, n_chunk):
    vs_ref[0:CHUNK, :] = jnp.zeros((CHUNK, C_CONV), F32)

    def glu(i, carry):
        base = pl.multiple_of(i * CHUNK, CHUNK)
        val = uc_ref[pl.ds(base, CHUNK), 0:C_CONV].astype(F32)
        gate = uc_ref[pl.ds(base, CHUNK), C_CONV:2 * C_CONV].astype(F32)
        vs_ref[pl.ds(base + CHUNK, CHUNK), :] = val * _sigmoid(gate)
        return carry

    lax.fori_loop(0, n_chunk, glu, 0, unroll=3)


def _fwd_conv(uc, conv_w, conv_b, ln_g, ln_b, token, n_ex):
    rows = uc.shape[0]
    lp = rows // n_ex
    n_chunk = lp // CHUNK

    def body(uc_ref, w_ref, b_ref, lg_ref, lb_ref, token_ref, ypre_ref, yc_ref, vs_ref):
        _glu_into(uc_ref, vs_ref, n_chunk)

        def conv(i, carry):
            base = pl.multiple_of(i * CHUNK, CHUNK)
            for lb in range(C_CONV // LANES):
                ls = slice(lb * LANES, (lb + 1) * LANES)
                win = vs_ref[pl.ds(base + CHUNK - HALO, CHUNK + HALO), ls]
                acc = jnp.broadcast_to(b_ref[:, ls], (CHUNK, LANES))
                for j, rows_j in _shifted(win, [HALO - (CONV_W - 1) + j for j in range(CONV_W)]):
                    acc = acc + w_ref[j:j + 1, ls] * rows_j
                ypre_ref[pl.ds(base, CHUNK), ls] = acc
            return carry

        lax.fori_loop(0, n_chunk, conv, 0, unroll=3)

        ln_rows = _row_tile(lp, LN_ROWS)

        def norm(i, carry):
            base = pl.multiple_of(i * ln_rows, 16)
            y = ypre_ref[pl.ds(base, ln_rows), :]
            mu = jnp.mean(y, axis=-1, keepdims=True)
            yc_ = y - mu
            rstd = lax.rsqrt(jnp.mean(yc_ * yc_, axis=-1, keepdims=True) + LN_EPS)
            s = yc_ * rstd * lg_ref[...] + lb_ref[...]
            yc_ref[pl.ds(base, ln_rows), :] = (s * _sigmoid(s)).astype(BF16)
            return carry

        lax.fori_loop(0, lp // ln_rows, norm, 0)

    ex = lambda w: pl.BlockSpec((lp, w), lambda b: (b, 0))
    return pl.pallas_call(
        body, name="fwd_conv", grid=(n_ex,),
        in_specs=[ex(2 * C_CONV), _fixed((32, C_CONV)), _fixed((1, C_CONV)), _fixed((1, C_CONV)), _fixed((1, C_CONV)),
                  _fixed((8, 128))],
        out_specs=[ex(C_CONV), ex(C_CONV)],
        out_shape=[jax.ShapeDtypeStruct((rows, C_CONV), F32), jax.ShapeDtypeStruct((rows, C_CONV), BF16)],
        scratch_shapes=[pltpu.VMEM((lp + CHUNK, C_CONV), F32)],
        compiler_params=_params(("parallel",)),
    )(uc, conv_w, conv_b, ln_g, ln_b, token)


def _bwd_conv(uc, ypre, dyc, conv_w, ln_g, ln_b, token, n_ex):
    rows = uc.shape[0]
    lp = rows // n_ex
    n_chunk = lp // CHUNK

    def body(uc_ref, ypre_ref, dyc_ref, w_ref, lg_ref, lb_ref, token_ref, duc_ref, dw_ref, dvec_ref, vs_ref, dys_ref,
             dwacc_ref):
        _glu_into(uc_ref, vs_ref, n_chunk)
        dys_ref[pl.ds(lp, CHUNK), :] = jnp.zeros((CHUNK, C_CONV), F32)
        dwacc_ref[...] = jnp.zeros_like(dwacc_ref)

        ln_rows = _row_tile(lp, LN_ROWS)

        def ln_bwd(i, carry):
            dcb, dlg, dlb = carry
            base = pl.multiple_of(i * ln_rows, 16)
            y = ypre_ref[pl.ds(base, ln_rows), :]
            mu = jnp.mean(y, axis=-1, keepdims=True)
            yc_ = y - mu
            rstd = lax.rsqrt(jnp.mean(yc_ * yc_, axis=-1, keepdims=True) + LN_EPS)
            xh = yc_ * rstd
            s = xh * lg_ref[...] + lb_ref[...]
            sg = _sigmoid(s)
            ds = dyc_ref[pl.ds(base, ln_rows), :] * (sg * (1.0 + s * (1.0 - sg)))
            dxh = ds * lg_ref[...]
            dy = rstd * (dxh - jnp.mean(dxh, axis=-1, keepdims=True) - xh * jnp.mean(dxh * xh, axis=-1, keepdims=True))
            dys_ref[pl.ds(base, ln_rows), :] = dy
            return (dcb + jnp.sum(dy, axis=0, keepdims=True), dlg + jnp.sum(ds * xh, axis=0, keepdims=True),
                    dlb + jnp.sum(ds, axis=0, keepdims=True))

        zero = jnp.zeros((1, C_CONV), F32)
        dcb, dlg, dlb = lax.fori_loop(0, lp // ln_rows, ln_bwd, (zero, zero, zero))

        @pl.when(pl.program_id(0) == 0)
        def _():
            dvec_ref[...] = jnp.zeros_like(dvec_ref)
            dw_ref[...] = jnp.zeros_like(dw_ref)

        dvec_ref[0:1, :] += dcb
        dvec_ref[1:2, :] += dlg
        dvec_ref[2:3, :] += dlb

        def taps(i, carry):
            base = pl.multiple_of(i * CHUNK, CHUNK)
            for lb in range(C_CONV // LANES):
                ls = slice(lb * LANES, (lb + 1) * LANES)
                dwin = dys_ref[pl.ds(base, CHUNK + HALO), ls]
                vwin = vs_ref[pl.ds(base + CHUNK - HALO, CHUNK + HALO), ls]
                dy = dwin[0:CHUNK]
                acc = jnp.zeros((CHUNK, LANES), F32)
                for j, rows_j in _shifted(dwin, [CONV_W - 1 - j for j in range(CONV_W)]):
                    acc = acc + w_ref[j:j + 1, ls] * rows_j
                for j, rows_j in _shifted(vwin, [HALO - (CONV_W - 1) + j for j in range(CONV_W)]):
                    dwacc_ref[8 * j:8 * j + 8, ls] += jnp.sum((dy * rows_j).reshape(CHUNK // 8, 8, LANES), axis=0)
                val = uc_ref[pl.ds(base, CHUNK), ls].astype(F32)
                gate = uc_ref[pl.ds(base, CHUNK), C_CONV + lb * LANES:C_CONV + (lb + 1) * LANES].astype(F32)
                sg = _sigmoid(gate)
                duc_ref[pl.ds(base, CHUNK), ls] = (acc * sg).astype(BF16)
                duc_ref[pl.ds(base, CHUNK), C_CONV + lb * LANES:C_CONV + (lb + 1) * LANES] = (
                    acc * val * sg * (1.0 - sg)).astype(BF16)
            return carry

        lax.fori_loop(0, n_chunk, taps, 0, unroll=3)
        for j in range(CONV_W):
            dw_ref[j:j + 1, :] += jnp.sum(dwacc_ref[8 * j:8 * j + 8, :], axis=0, keepdims=True)

    ex = lambda w: pl.BlockSpec((lp, w), lambda b: (b, 0))
    return pl.pallas_call(
        body, name="bwd_conv", grid=(n_ex,),
        in_specs=[ex(2 * C_CONV), ex(C_CONV), ex(C_CONV), _fixed((32, C_CONV)), _fixed((1, C_CONV)), _fixed((1, C_CONV)),
                  _fixed((8, 128))],
        out_specs=[ex(2 * C_CONV), _fixed((32, C_CONV)), _fixed((8, C_CONV))],
        out_shape=[jax.ShapeDtypeStruct((rows, 2 * C_CONV), BF16), jax.ShapeDtypeStruct((32, C_CONV), F32),
                   jax.ShapeDtypeStruct((8, C_CONV), F32)],
        scratch_shapes=[pltpu.VMEM((lp + CHUNK, C_CONV), F32), pltpu.VMEM((lp + CHUNK, C_CONV), F32),
                        pltpu.VMEM((8 * 32, C_CONV), F32)],
        compiler_params=_params(("arbitrary",)),
    )(uc, ypre, dyc, conv_w, ln_g, ln_b, token)


def _seg_chunks(n_chunk):
    return max(c for c in (11, 3, 1) if n_chunk % c == 0)


def _block_mask(shape, row_block, lane_block):
    return (lax.broadcasted_iota(jnp.int32, shape, 0) // row_block) == (lax.broadcasted_iota(jnp.int32, shape, 1) // lane_block)


def _per_head_rows(x, mask):
    return jnp.where(mask, jnp.concatenate([x] * GLA_H, axis=0), 0)


def _fold_heads(full, lane_block):
    lane = lax.broadcasted_iota(jnp.int32, (1, full.shape[1]), 1) // lane_block
    out = jnp.where(lane == 0, full[0:CHUNK], 0.0)
    for h in range(1, GLA_H):
        out = out + jnp.where(lane == h, full[h * CHUNK:(h + 1) * CHUNK], 0.0)
    return out


PAIRS = GLA_H // 2


def _expand_state(blocks):
    lane = lax.broadcasted_iota(jnp.int32, (GLA_DV, 128), 1) // GLA_DK
    zero = jnp.zeros_like(blocks[0])
    rows = []
    for h in range(GLA_H):
        p, hh = divmod(h, 2)
        mine = jnp.where(lane == hh, blocks[p], 0)
        rows.append(jnp.concatenate([mine if q == p else zero for q in range(PAIRS)], axis=1))
    return jnp.concatenate(rows, axis=0)


def _compact_state(full, p):
    lane = lax.broadcasted_iota(jnp.int32, (GLA_DV, 128), 1) // GLA_DK
    ls = slice(128 * p, 128 * (p + 1))
    return jnp.where(lane == 0, full[2 * p * GLA_DV:(2 * p + 1) * GLA_DV, ls], full[(2 * p + 1) * GLA_DV:(2 * p + 2) * GLA_DV, ls])


def _causal_heads():
    return (lax.broadcasted_iota(jnp.int32, (CHUNK, GLA_H * CHUNK), 1) % CHUNK) <= lax.broadcasted_iota(
        jnp.int32, (CHUNK, GLA_H * CHUNK), 0)


def _cumsum_rows(x):
    row = lax.broadcasted_iota(jnp.int32, x.shape, 0)
    s = 1
    while s < CHUNK:
        x = x + jnp.where(row >= s, pltpu.roll(x, s, 0), 0.0)
        s *= 2
    return x


def _rev_cumsum_rows(x):
    row = lax.broadcasted_iota(jnp.int32, x.shape, 0)
    s = 1
    while s < CHUNK:
        x = x + jnp.where(row < CHUNK - s, pltpu.roll(x, CHUNK - s, 0), 0.0)
        s *= 2
    return x


def _gate_terms(lr_ref, w2_ref, gb_ref, rs, first_pos):
    z = _dot(lr_ref[rs, :].astype(BF16), w2_ref[...]) + gb_ref[...]
    la = (jnp.minimum(z, 0.0) - jnp.log(1.0 + jnp.exp(-jnp.abs(z)))) * (1.0 / TAU)
    pos = first_pos + lax.broadcasted_iota(jnp.int32, (CHUNK, 1), 0)
    live = pos >= ZROWS
    la = jnp.where(live, la, 0.0)
    return z, live, _cumsum_rows(la)


def _fwd_gla(qk, vg, lr, w2p, gb, ng, token, n_ex):
    rows = qk.shape[0]
    lp = rows // n_ex
    n_chunk = lp // CHUNK
    sc = _seg_chunks(n_chunk)
    n_seg = n_chunk // sc
    seg = sc * CHUNK

    def body(qk_ref, vg_ref, lr_ref, w2_ref, gb_ref, ng_ref, token_ref, yg_ref, o_ref, st_ref, state_ref):
        sidx = pl.program_id(1)

        @pl.when(sidx == 0)
        def _():
            state_ref[...] = jnp.zeros_like(state_ref)

        causal = _causal_heads()
        k_mask = _block_mask((GLA_H * CHUNK, GLA_K), CHUNK, GLA_DK)
        v_mask = _block_mask((GLA_H * CHUNK, GLA_V), CHUNK, GLA_DV)

        def chunk(ci, carry):
            base = pl.multiple_of(ci * CHUNK, CHUNK)
            rs = pl.ds(base, CHUNK)
            _, _, bcum = _gate_terms(lr_ref, w2_ref, gb_ref, rs, (sidx * sc + ci) * CHUNK)
            bl = bcum[CHUNK - 1:CHUNK, :]
            q = qk_ref[rs, 0:GLA_K].astype(F32)
            k = qk_ref[rs, GLA_K:2 * GLA_K].astype(F32)
            qt = (q * (GLA_DK ** -0.5) * jnp.exp(bcum)).astype(BF16)
            kt = (k * jnp.exp(-bcum)).astype(BF16)
            kh = (k * jnp.exp(bl - bcum)).astype(BF16)
            vb = vg_ref[rs, 0:GLA_V].astype(BF16)
            state = [state_ref[p] for p in range(PAIRS)]
            for p in range(PAIRS):
                st_ref[ci, p] = state[p]
            a = jnp.where(causal, _dot(qt, _per_head_rows(kt, k_mask), _NT), 0.0)
            o = _dot(a.astype(BF16), _per_head_rows(vb, v_mask)) + _dot(
                qt, _expand_state([s.astype(BF16) for s in state]), _NT)
            o_ref[rs, :] = o
            for h in range(GLA_H):
                hs = slice(h * GLA_DV, (h + 1) * GLA_DV)
                oh = o[:, hs]
                ro = lax.rsqrt(jnp.mean(oh * oh, axis=-1, keepdims=True) + RMS_EPS)
                g = vg_ref[rs, GLA_V + h * GLA_DV:GLA_V + (h + 1) * GLA_DV].astype(F32)
                yg_ref[rs, hs] = (oh * ro * ng_ref[...] * g * _sigmoid(g)).astype(BF16)
            kv = _dot(vb, kh, _TN)
            decay = jnp.exp(bl)
            for p in range(PAIRS):
                state_ref[p] = state[p] * decay[:, 128 * p:128 * (p + 1)] + _compact_state(kv, p)
            return carry

        lax.fori_loop(0, sc, chunk, 0, unroll=True)

    sg = lambda w: pl.BlockSpec((seg, w), lambda b, s: (b * n_seg + s, 0))
    return pl.pallas_call(
        body, name="fwd_gla", grid=(n_ex, n_seg),
        in_specs=[sg(2 * GLA_K), sg(2 * GLA_V), sg(RANK_P), _fixed((RANK_P, GLA_K)), _fixed((1, GLA_K)), _fixed((1, GLA_DV)),
                  _fixed((8, 128))],
        out_specs=[sg(GLA_V), sg(GLA_V), pl.BlockSpec((sc, PAIRS, GLA_DV, 128), lambda b, s: (b * n_seg + s, 0, 0, 0))],
        out_shape=[jax.ShapeDtypeStruct((rows, GLA_V), BF16), jax.ShapeDtypeStruct((rows, GLA_V), F32),
                   jax.ShapeDtypeStruct((n_ex * n_chunk, PAIRS, GLA_DV, 128), F32)],
        scratch_shapes=[pltpu.VMEM((PAIRS, GLA_DV, 128), F32)],
        compiler_params=_params(("parallel", "arbitrary")),
    )(qk, vg, lr, w2p, gb, ng, token)


def _bwd_gla(qk, vg, lr, o, st, dyg, w2p, gb, ng, yc, yg, dh1b, token, n_ex):
    rows = qk.shape[0]
    lp = rows // n_ex
    n_chunk = lp // CHUNK
    sc = _seg_chunks(n_chunk)
    n_seg = n_chunk // sc
    seg = sc * CHUNK

    def body(qk_ref, vg_ref, lr_ref, o_ref, st_ref, dyg_ref, w2_ref, gb_ref, ng_ref, yc_ref, yg_ref, dh1_ref, token_ref,
             dqk_ref, dvg_ref, dlr_ref, dw2_ref, dvec_ref, dwo_ref, gt_ref, dz_ref, dwo_acc):
        step = pl.program_id(1)
        sidx = n_seg - 1 - step
        first = (step == 0) & (pl.program_id(0) == 0)

        @pl.when(step == 0)
        def _():
            gt_ref[...] = jnp.zeros_like(gt_ref)

        @pl.when(first)
        def _():
            dw2_ref[...] = jnp.zeros_like(dw2_ref)
            dvec_ref[...] = jnp.zeros_like(dvec_ref)
            dwo_acc[...] = jnp.zeros_like(dwo_acc)

        d1 = dh1_ref[...]
        dwo_acc[0:C_CONV, :] += _dot(yc_ref[...], d1, _TN)
        dwo_acc[C_CONV:D, :] += _dot(yg_ref[...], d1, _TN)

        @pl.when((step == n_seg - 1) & (pl.program_id(0) == n_ex - 1))
        def _():
            dwo_ref[...] = dwo_acc[...].astype(BF16)

        causal = _causal_heads()
        k_mask = _block_mask((GLA_H * CHUNK, GLA_K), CHUNK, GLA_DK)
        v_mask = _block_mask((GLA_H * CHUNK, GLA_V), CHUNK, GLA_DV)
        last_row = lax.broadcasted_iota(jnp.int32, (CHUNK, 1), 0) == CHUNK - 1
        ng = ng_ref[...]

        def chunk(ii, dng):
            ci = sc - 1 - ii
            base = pl.multiple_of(ci * CHUNK, CHUNK)
            rs = pl.ds(base, CHUNK)
            z, live, bcum = _gate_terms(lr_ref, w2_ref, gb_ref, rs, (sidx * sc + ci) * CHUNK)
            bl = bcum[CHUNK - 1:CHUNK, :]
            ebl = jnp.exp(bl)
            q = qk_ref[rs, 0:GLA_K].astype(F32)
            k = qk_ref[rs, GLA_K:2 * GLA_K].astype(F32)
            eb = jnp.exp(bcum)
            enb = jnp.exp(-bcum)
            ehb = jnp.exp(bl - bcum)
            qt = q * (GLA_DK ** -0.5) * eb
            kt = k * enb
            kh = k * ehb
            qtb = qt.astype(BF16)
            vb = vg_ref[rs, 0:GLA_V].astype(BF16)
            k_rows = _per_head_rows(kt.astype(BF16), k_mask)
            v_rows = _per_head_rows(vb, v_mask)
            gt = [gt_ref[p] for p in range(PAIRS)]
            gtb = _expand_state([g_.astype(BF16) for g_ in gt])
            s_in = [st_ref[ci, p] for p in range(PAIRS)]
            dos = []
            for h in range(GLA_H):
                hs = slice(h * GLA_DV, (h + 1) * GLA_DV)
                gs = slice(GLA_V + h * GLA_DV, GLA_V + (h + 1) * GLA_DV)
                oh = o_ref[rs, hs]
                ro = lax.rsqrt(jnp.mean(oh * oh, axis=-1, keepdims=True) + RMS_EPS)
                on = oh * ro
                g = vg_ref[rs, gs].astype(F32)
                sg = _sigmoid(g)
                dout = dyg_ref[rs, hs]
                dvg_ref[rs, gs] = (dout * on * ng * (sg * (1.0 + g * (1.0 - sg)))).astype(BF16)
                dw = dout * g * sg
                dng = dng + jnp.sum(dw * on, axis=0, keepdims=True)
                don = dw * ng
                dos.append((ro * (don - on * jnp.mean(don * on, axis=-1, keepdims=True))).astype(BF16))
            dob = jnp.concatenate(dos, axis=1)
            a = jnp.where(causal, _dot(qtb, k_rows, _NT), 0.0).astype(BF16)
            da = jnp.where(causal, _dot(dob, v_rows, _NT), 0.0).astype(BF16)
            dv = _fold_heads(_dot(a, dob, _TN), GLA_DV) + _dot(kh.astype(BF16), gtb, _NT)
            dvg_ref[rs, 0:GLA_V] = dv.astype(BF16)
            dkh = _dot(vb, gtb)
            dqt = _dot(da, k_rows) + _dot(dob, _expand_state([s_.astype(BF16) for s_ in s_in]))
            dkt = _fold_heads(_dot(da, qtb, _TN), GLA_DK)
            dbl = jnp.concatenate([jnp.sum(gt[p] * s_in[p], axis=0, keepdims=True) for p in range(PAIRS)], axis=1) * ebl
            dbl = dbl + jnp.sum(dkh * kh, axis=0, keepdims=True)
            dqk_ref[rs, 0:GLA_K] = (dqt * (GLA_DK ** -0.5) * eb).astype(BF16)
            dqk_ref[rs, GLA_K:2 * GLA_K] = (dkt * enb + dkh * ehb).astype(BF16)
            db = dqt * qt - dkt * kt - dkh * kh
            db = jnp.where(last_row, db + dbl, db)
            dla = jnp.where(live, _rev_cumsum_rows(db), 0.0)
            dz_ref[rs, :] = dla * (1.0 / TAU) * (1.0 - _sigmoid(z))
            dstate = _dot(dob, qtb, _TN)
            for p in range(PAIRS):
                gt_ref[p] = _compact_state(dstate, p) + gt[p] * ebl[:, 128 * p:128 * (p + 1)]
            return dng

        dng = lax.fori_loop(0, sc, chunk, jnp.zeros((1, GLA_DV), F32), unroll=True)
        dz = dz_ref[...]
        dzb = dz.astype(BF16)
        dlr_ref[...] = _dot(dzb, w2_ref[...], _NT).astype(BF16)
        dw2_ref[...] += _dot(lr_ref[...].astype(BF16), dzb, _TN)
        dvec_ref[0:1, :] += jnp.sum(dz, axis=0, keepdims=True)
        dvec_ref[1:2, 0:GLA_DV] += dng

    sg_ = lambda w: pl.BlockSpec((seg, w), lambda b, s: (b * n_seg + n_seg - 1 - s, 0))
    return pl.pallas_call(
        body, name="bwd_gla", grid=(n_ex, n_seg),
        in_specs=[sg_(2 * GLA_K), sg_(2 * GLA_V), sg_(RANK_P), sg_(GLA_V),
                  pl.BlockSpec((sc, PAIRS, GLA_DV, 128), lambda b, s: (b * n_seg + n_seg - 1 - s, 0, 0, 0)), sg_(GLA_V),
                  _fixed((RANK_P, GLA_K)), _fixed((1, GLA_K)), _fixed((1, GLA_DV)), sg_(C_CONV), sg_(GLA_V), sg_(D),
                  _fixed((8, 128))],
        out_specs=[sg_(2 * GLA_K), sg_(2 * GLA_V), sg_(RANK_P), _fixed((RANK_P, GLA_K)), _fixed((8, GLA_K)),
                   _fixed((D, D))],
        out_shape=[jax.ShapeDtypeStruct((rows, 2 * GLA_K), BF16), jax.ShapeDtypeStruct((rows, 2 * GLA_V), BF16),
                   jax.ShapeDtypeStruct((rows, RANK_P), BF16), jax.ShapeDtypeStruct((RANK_P, GLA_K), F32),
                   jax.ShapeDtypeStruct((8, GLA_K), F32), jax.ShapeDtypeStruct((D, D), BF16)],
        scratch_shapes=[pltpu.VMEM((PAIRS, GLA_DV, 128), F32), pltpu.VMEM((seg, GLA_K), F32), pltpu.VMEM((D, D), F32)],
        compiler_params=_params(("arbitrary", "arbitrary")),
    )(qk, vg, lr, o, st, dyg, w2p, gb, ng, yc, yg, dh1b, token)


def _pad_rows(x, tgt):
    return jnp.pad(x, ((0, 0), (LEAD, 0), (0, 0))), jnp.pad(tgt, ((0, 0), (LEAD, 0), (0, 0)))


def _local_step(h0, tgt_p, p, pass_on, late_weights, send_early):
    n_ex, lp, _ = h0.shape
    rows = n_ex * lp
    meta = jnp.broadcast_to(p["meta"][None], (n_ex, N_META, D))
    h0 = lax.dynamic_update_slice(h0, meta, (0, ZROWS, 0)).reshape(rows, D)
    tgt_p = tgt_p.reshape(rows, D)

    uc, qk, vg, lr, n1 = _fwd_inproj(h0, p["g1"], p["w_in"])
    ypre, yc = _fwd_conv(uc, p["conv_w"], p["conv_b"], p["ln_g"], p["ln_b"], p["token"], n_ex)
    yg, o, st = _fwd_gla(qk, vg, lr, p["w2"], p["gb"], p["ng"], p["token"], n_ex)
    token = pass_on((yc, yg))
    w_out, wg, wu, wd = late_weights(token)
    n2, f, da, db, dh2, dh1, dh1b, dyc, dyg, part = _mid_rows(
        yc, yg, h0, tgt_p, w_out, wg, wu, wd, p["g2"], p["g3"], token, lp)
    g = {}
    token = send_early("ffn", [_matmul_tn(a_, b_, name).reshape(N_DEV, FF_S, D) for a_, b_, name in (
        (da, n2, "dw_gate"), (db, n2, "dw_up"), (f, dh2, "dw_down"))])
    dqk, dvg, dlr, g["w2"], g["gla_vec"], dw_out = _bwd_gla(
        qk, vg, lr, o, st, dyg, p["w2"], p["gb"], p["ng"], yc, yg, dh1b, token, n_ex)
    token = send_early("out", [dw_out.reshape(N_DEV, W_OUT_S, D)])
    duc, g["conv_w"], g["conv_vec"] = _bwd_conv(uc, ypre, dyc, p["conv_w"], p["ln_g"], p["ln_b"], token, n_ex)
    token = send_early("in", [_dw_blocked(n1, [duc, dqk, dvg, dlr], W_IN_S, "dw_in")])
    grad_x, g["in_vec"], g["meta"] = _bwd_inproj(duc, dqk, dvg, dlr, dh1, h0, p["w_in"], p["g1"], token, lp)
    g["ffn_vec"] = part
    return grad_x, g


W_IN_S = D_IN // N_DEV
W_OUT_S = D // N_DEV
FF_S = D_FF // N_DEV
CONV_S = C_CONV // N_DEV
GATE_S = GLA_K // N_DEV
SMALL_PACK = 64
CONV_ROW = 16
GATE_ROW = 48
VEC_ROWS = 16
_VEC_ROWS = (("norm_mix_g", D), ("conv_b", C_CONV), ("conv_ln_g", C_CONV), ("conv_ln_b", C_CONV), ("gla_gate_b", GLA_K),
             ("gla_norm_g", GLA_DV), ("norm_ffn_g", D), ("norm_final_g", D))
LOSS_ROW = len(_VEC_ROWS)


def _position():
    return lax.axis_index("x"), lax.axis_index("y"), lax.axis_index("c")


def _any():
    return pl.BlockSpec(memory_space=pl.ANY)


def _stage(mats, meta, conv_w, w2):
    n_t = len(mats) + 1

    def body(*refs):
        ins = refs[0:n_t - 1]
        meta_ref, cw_ref, w2_ref = refs[n_t - 1:n_t + 2]
        lands = refs[n_t + 2:2 * n_t + 2]
        shards = refs[2 * n_t + 2:3 * n_t + 2]
        sems = refs[3 * n_t + 2]
        for s_ref, w_ref in zip(shards, ins):
            s_ref[...] = w_ref[...].astype(BF16)
        sp = shards[n_t - 1]
        sp[...] = jnp.zeros_like(sp)
        sp[0:N_META, :] = meta_ref[...]
        sp[CONV_ROW:CONV_ROW + CONV_W, 0:CONV_S] = cw_ref[...]
        sp[GATE_ROW:GATE_ROW + RANK, 0:GATE_S] = w2_ref[...]
        x, y, c = _position()
        mine = [pltpu.make_async_copy(shards[t], lands[t].at[4 * x + 2 * y + c], sems.at[t]) for t in range(n_t)]
        for cp in mine:
            cp.start()
        for cp in mine:
            cp.wait()

    shard_shapes = [jax.ShapeDtypeStruct(m.shape, BF16) for m in mats] + [jax.ShapeDtypeStruct((SMALL_PACK, 128), F32)]
    res = pl.pallas_call(
        body, name="stage",
        out_shape=[jax.ShapeDtypeStruct((N_DEV,) + s.shape, s.dtype) for s in shard_shapes] + shard_shapes,
        in_specs=[_whole_vmem()] * (n_t + 2), out_specs=[_any()] * n_t + [_whole_vmem()] * n_t,
        scratch_shapes=[pltpu.SemaphoreType.DMA((n_t,))],
        compiler_params=pltpu.CompilerParams(vmem_limit_bytes=VMEM_LIMIT),
    )(*mats, meta, conv_w, w2)
    return res[0:n_t], res[n_t:]


_HBM = pl.BlockSpec(memory_space=pltpu.HBM)
_SEM = pl.BlockSpec(memory_space=pltpu.SEMAPHORE)
_EFFECT = pltpu.SideEffectType.DATAFLOW_SIDE_EFFECTING


_N_ROUTES = {"scatter": 7, "first": 4, "forward": 3}


def _routes(mode):
    x, y, c = _position()
    me = 4 * x + 2 * y + c
    if mode == "scatter":
        out = []
        for k in range(1, N_DEV):
            px = 1 - x if k & 4 else x
            py = 1 - y if k & 2 else y
            pc = 1 - c if k & 1 else c
            out.append(((px, py, pc), 4 * px + 2 * py + pc, me))
        return out
    if mode == "first":
        return [(pos, None, me) for pos in ((x, y, 1 - c), (1 - x, y, c), (x, 1 - y, c), (1 - x, 1 - y, c))]
    assert mode == "forward"
    return [((x, y, 1 - c), 4 * px + 2 * py + c, 4 * px + 2 * py + c) for px, py in ((1 - x, y), (x, 1 - y), (1 - x, 1 - y))]


def _route_copies(mode, n, src_refs, land_refs, send_sems, recv_sems):
    nr = _N_ROUTES[mode]
    for i, (pos, src_blk, dst_blk) in enumerate(_routes(mode)):
        for t in range(n):
            src = land_refs[t] if mode == "forward" else src_refs[t]
            yield pltpu.make_async_remote_copy(
                src_ref=src if src_blk is None else src.at[src_blk], dst_ref=land_refs[t].at[dst_blk],
                send_sem=send_sems.at[nr * t + i], recv_sem=recv_sems.at[nr * t + i], device_id=pos, device_id_type=MESH)


def _in_hbm(a):
    return pltpu.with_memory_space_constraint(a, pltpu.HBM)


def _send_start(name, groups, mode, after):
    sizes = [(len(s), len(l)) for s, l in groups]
    bufs = [b for s, l in groups for b in list(s) + list(l)]
    nb, ng = len(bufs), len(groups)

    def body(*refs):
        sems = refs[nb + 1:nb + 1 + 2 * ng]
        token = refs[2 * nb + 2 * ng + 1]
        off = 0
        for gi, (ns, n) in enumerate(sizes):
            for cp in _route_copies(mode, n, refs[off:off + ns], refs[off + ns:off + ns + n], sems[2 * gi], sems[2 * gi + 1]):
                cp.start()
            off += ns + n
        token[...] = jnp.zeros_like(token)

    res = pl.pallas_call(
        body, name=name,
        out_shape=(*[pltpu.SemaphoreType.DMA((_N_ROUTES[mode] * n,)) for _, n in sizes for _ in range(2)],
                   *[pltpu.HBM(b.shape, b.dtype) for b in bufs], jax.ShapeDtypeStruct((8, 128), F32)),
        in_specs=[_HBM] * nb + [_any()], out_specs=(*[_SEM] * (2 * ng), *[_HBM] * nb, _whole_vmem()),
        input_output_aliases={i: 2 * ng + i for i in range(nb)},
        compiler_params=pltpu.CompilerParams(has_side_effects=_EFFECT),
    )(*[_in_hbm(b) for b in bufs], after)
    handles, off = [], 2 * ng
    for gi, (ns, n) in enumerate(sizes):
        handles.append((res[2 * gi], res[2 * gi + 1], res[off:off + ns], res[off + ns:off + ns + n]))
        off += ns + n
    return handles, res[2 * ng + nb]


def _send_wait(name, send_sems, recv_sems, srcs, lands, mode, after):
    n, ns = len(lands), len(srcs)
    after = after if isinstance(after, tuple) else (after,)

    def body(*refs):
        src_refs, land_refs = refs[0:ns], refs[ns:ns + n]
        send_sems, recv_sems = refs[ns + n:ns + n + 2]
        for cp in _route_copies(mode, n, src_refs, land_refs, send_sems, recv_sems):
            cp.wait_send()
            cp.wait_recv()

    bufs = list(srcs) + list(lands)
    res = pl.pallas_call(
        body, name=name,
        out_shape=tuple(pltpu.HBM(b.shape, b.dtype) for b in bufs),
        in_specs=[_HBM] * len(bufs) + [_SEM, _SEM] + [_any()] * len(after), out_specs=tuple([_HBM] * len(bufs)),
        input_output_aliases={i: i for i in range(len(bufs))},
        compiler_params=pltpu.CompilerParams(has_side_effects=_EFFECT),
    )(*bufs, send_sems, recv_sems, *after)
    return res[0:ns], res[ns:ns + n]


def _unshard_in(a_in, a_small, token):
    def body(a_ref, s_ref, token_ref, w_ref, meta_ref, cw_ref, w2_ref):
        w_ref[:, D_IN:D_INP] = jnp.zeros((D, D_INP - D_IN), BF16)
        w2_ref[...] = jnp.zeros_like(w2_ref)
        for d in range(N_DEV):
            w_ref[:, d * W_IN_S:(d + 1) * W_IN_S] = a_ref[d]
            meta_ref[:, d * 128:(d + 1) * 128] = s_ref[d, 0:N_META, :]
            cw_ref[:, d * CONV_S:(d + 1) * CONV_S] = s_ref[d, CONV_ROW:CONV_ROW + 32, 0:CONV_S]
            w2_ref[0:RANK, d * GATE_S:(d + 1) * GATE_S] = s_ref[d, GATE_ROW:GATE_ROW + RANK, 0:GATE_S].astype(BF16)

    return pl.pallas_call(
        body, name="unshard_in",
        out_shape=[jax.ShapeDtypeStruct((D, D_INP), BF16), jax.ShapeDtypeStruct((N_META, D), F32),
                   jax.ShapeDtypeStruct((32, C_CONV), F32), jax.ShapeDtypeStruct((RANK_P, GLA_K), BF16)],
        compiler_params=pltpu.CompilerParams(vmem_limit_bytes=VMEM_LIMIT),
    )(a_in, a_small, token)


def _pack_small(g):
    def body(meta_ref, cw_ref, w2_ref, in_vec, ffn_vec, conv_vec, gla_vec, sp, vp):
        sp[...] = jnp.zeros_like(sp)
        vp[...] = jnp.zeros_like(vp)
        for d in range(N_DEV):
            sp[d, 0:N_META, :] = meta_ref[:, d * 128:(d + 1) * 128]
            sp[d, CONV_ROW:CONV_ROW + 32, 0:CONV_S] = cw_ref[:, d * CONV_S:(d + 1) * CONV_S]
            sp[d, GATE_ROW:GATE_ROW + RANK, 0:GATE_S] = w2_ref[0:RANK, d * GATE_S:(d + 1) * GATE_S]
            vp[d, 0:1, :] = in_vec[0:1, :]
            vp[d, 1:4, 0:C_CONV] = conv_vec[0:3, :]
            vp[d, 4:5, 0:GLA_K] = gla_vec[0:1, :]
            vp[d, 5:6, 0:GLA_DV] = gla_vec[1:2, 0:GLA_DV]
            vp[d, 6:7, :] = ffn_vec[1:2, :]
            vp[d, 7:8, :] = ffn_vec[0:1, :]
            vp[d, LOSS_ROW:LOSS_ROW + 1, :] = ffn_vec[2:3, :]

    return pl.pallas_call(
        body, name="pack_small",
        out_shape=[jax.ShapeDtypeStruct((N_DEV, SMALL_PACK, 128), F32), jax.ShapeDtypeStruct((N_DEV, VEC_ROWS, D), F32)],
    )(g["meta"], g["conv_w"], g["w2"], g["in_vec"], g["ffn_vec"], g["conv_vec"], g["gla_vec"])


def _adamw(w, g, m, v):
    m = ADAM_B1 * m + (1.0 - ADAM_B1) * g
    v = ADAM_B2 * v + (1.0 - ADAM_B2) * (g * g)
    m_hat = m / (1.0 - ADAM_B1 ** ADAM_STEP)
    v_hat = v / (1.0 - ADAM_B2 ** ADAM_STEP)
    return -ADAM_LR * (m_hat / (jnp.sqrt(v_hat) + ADAM_EPS) + ADAM_WD * w), m, v


def _update_matrix(recv, own, me, w, m, v, name):
    _, r, c = recv.shape
    tr = _row_tile(r, 256)

    def body(me_ref, recv_ref, own_ref, w_ref, m_ref, v_ref, g_ref, d_ref, nm_ref, nv_ref):
        g = jnp.zeros((tr, c), F32)
        for s in range(N_DEV):
            g = g + jnp.where(me_ref[0] == s, own_ref[...], recv_ref[s]).astype(F32)
        g_ref[...] = g
        d_ref[...], nm_ref[...], nv_ref[...] = _adamw(w_ref[...], g, m_ref[...], v_ref[...])

    one = pl.BlockSpec((None, tr, c), lambda i, me_ref: (0, i, 0))
    return pl.pallas_call(
        body, name=name,
        grid_spec=pltpu.PrefetchScalarGridSpec(
            num_scalar_prefetch=1, grid=(r // tr,),
            in_specs=[pl.BlockSpec((N_DEV, tr, c), lambda i, me_ref: (0, i, 0)),
                      pl.BlockSpec((None, tr, c), lambda i, me_ref: (me_ref[0], i, 0)), one, one, one],
            out_specs=[one] * 4),
        out_shape=[jax.ShapeDtypeStruct((1, r, c), F32)] * 4,
        compiler_params=_params(("parallel",)),
    )(me, recv, own, w, m, v)


_SMALL = ("meta_tokens", "conv_w", "gla_w_gate2") + tuple(n for n, _ in _VEC_ROWS)


def _update_small(me, srecv, vrecv, sown, vown, w, m, v):
    n = len(_SMALL)

    def body(*refs):
        me_ref, s_ref, v_ref, so_ref, vo_ref = refs[0:5]
        w_refs, m_refs, v_refs = refs[5:5 + n], refs[5 + n:5 + 2 * n], refs[5 + 2 * n:5 + 3 * n]
        outs = refs[5 + 3 * n:]
        ssum = jnp.zeros((SMALL_PACK, 128), F32)
        vsum = jnp.zeros((VEC_ROWS, D), F32)
        for s in range(N_DEV):
            ssum = ssum + jnp.where(me_ref[0] == s, so_ref[s], s_ref[s])
            vsum = vsum + jnp.where(me_ref[0] == s, vo_ref[s], v_ref[s])
        grads = [ssum[0:N_META, :], ssum[CONV_ROW:CONV_ROW + CONV_W, 0:CONV_S], ssum[GATE_ROW:GATE_ROW + RANK, 0:GATE_S]]
        grads += [vsum[i:i + 1, 0:width] for i, (_, width) in enumerate(_VEC_ROWS)]
        for i, g in enumerate(grads):
            d, nm, nv = _adamw(w_refs[i][...], g, m_refs[i][...], v_refs[i][...])
            outs[i][...] = g
            outs[n + i][...] = d
            outs[2 * n + i][...] = nm
            outs[3 * n + i][...] = nv
        outs[4 * n][...] = vsum[LOSS_ROW:LOSS_ROW + 1, 0:128]

    shapes = [jax.ShapeDtypeStruct(t.shape, F32) for t in w]
    res = pl.pallas_call(
        body, name="update_small", out_shape=shapes * 4 + [jax.ShapeDtypeStruct((1, 128), F32)],
        in_specs=[pl.BlockSpec(memory_space=pltpu.SMEM)] + [_whole_vmem()] * (4 + 3 * n),
    )(me, srecv, vrecv, sown, vown, *w, *m, *v)
    return res[0:n], res[n:2 * n], res[2 * n:3 * n], res[3 * n:4 * n], res[4 * n]


_WEIGHTS = ("meta_tokens", "norm_mix_g", "w_in", "conv_w", "conv_b", "conv_ln_g", "conv_ln_b", "gla_w_gate2", "gla_gate_b",
            "gla_norm_g", "w_out", "norm_ffn_g", "w_ffn_gate", "w_ffn_up", "w_ffn_down", "norm_final_g")
_MATRICES = ("w_in", "w_out", "w_ffn_gate", "w_ffn_up", "w_ffn_down")
_TRANSPOSED = ("w_ffn_gate", "w_ffn_up")


def kernel(x, meta_tokens, norm_mix_g, w_in, conv_w, conv_b, conv_ln_g, conv_ln_b, gla_w_gate2, gla_gate_b, gla_norm_g, w_out, norm_ffn_g, w_ffn_gate, w_ffn_up, w_ffn_down, norm_final_g, loss_target, m_meta_tokens, m_norm_mix_g, m_w_in, m_conv_w, m_conv_b, m_conv_ln_g, m_conv_ln_b, m_gla_w_gate2, m_gla_gate_b, m_gla_norm_g, m_w_out, m_norm_ffn_g, m_w_ffn_gate, m_w_ffn_up, m_w_ffn_down, m_norm_final_g, v_meta_tokens, v_norm_mix_g, v_w_in, v_conv_w, v_conv_b, v_conv_ln_g, v_conv_ln_b, v_gla_w_gate2, v_gla_gate_b, v_gla_norm_g, v_w_out, v_norm_ffn_g, v_w_ffn_gate, v_w_ffn_up, v_w_ffn_down, v_norm_final_g):
    given = dict(locals())
    two_d = lambda a: a.reshape(1, -1) if a.ndim == 1 else a.reshape(a.shape[-2:])
    fams = [{n: given[pre + n] for n in _WEIGHTS} for pre in ("", "m_", "v_")]
    for f in fams:
        for n in _TRANSPOSED:
            f[n] = f[n].transpose(0, 2, 1)
    w = fams[0]

    lands, shards = _stage([two_d(w[n]) for n in _MATRICES], w["meta_tokens"], two_d(w["conv_w"]), two_d(w["gla_w_gate2"]))
    soon, later = (0, 5), (1, 2, 3, 4)
    pick = lambda seq, idx: [seq[i] for i in idx]
    (first, ffn_first), started = _send_start(
        "gather_first_start", [(pick(shards, soon), pick(lands, soon)), (pick(shards, later), pick(lands, later))],
        "first", norm_mix_g)
    h0, tgt_p = _pad_rows(x, loss_target)
    _, arrived = _send_wait("gather_first_wait", *first, "first", (h0, tgt_p, started))
    (forward,), token = _send_start("gather_forward_start", [([], arrived)], "forward", started)
    _, (a_in, a_small) = _send_wait("gather_forward_wait", *forward, "forward", token)
    w_in, meta, conv_taps, w2 = _unshard_in(a_in, a_small, token)
    p = dict(meta=meta, conv_w=conv_taps, w2=w2, w_in=w_in, g1=norm_mix_g, conv_b=conv_b, ln_g=conv_ln_g, ln_b=conv_ln_b,
             gb=gla_gate_b, ng=gla_norm_g, g2=norm_ffn_g, g3=two_d(norm_final_g), token=token)
    passed = {}

    def pass_on(after):
        _, arrived_ffn = _send_wait("gather_ffn_first_wait", *ffn_first, "first", after)
        (passed["sent"],), token = _send_start("gather_ffn_forward_start", [([], arrived_ffn)], "forward", norm_mix_g)
        return token

    def late_weights(after):
        _, (a_out, a_g, a_u, a_d) = _send_wait("gather_ffn_forward_wait", *passed["sent"], "forward", after)
        return a_out.reshape(D, D), a_g.reshape(D_FF, D), a_u.reshape(D_FF, D), a_d.reshape(D_FF, D)

    sent = {}

    def send_early(tag, mats):
        landing = [_in_hbm(lax.empty(m_.shape, m_.dtype)) for m_ in mats]
        (sent[tag],), token = _send_start("scatter_" + tag + "_start", [(mats, landing)], "scatter", norm_mix_g)
        return token

    grad_x, g = _local_step(h0, tgt_p, p, pass_on, late_weights, send_early)

    token = send_early("small", list(_pack_small(g)))
    x_, y_, c_ = _position()
    me = (4 * x_ + 2 * y_ + c_).astype(jnp.int32).reshape(1)
    res = {}
    for tag, names in (("ffn", ("w_ffn_gate", "w_ffn_up", "w_ffn_down")), ("out", ("w_out",)), ("in", ("w_in",))):
        own, recv = _send_wait("scatter_" + tag + "_wait", *sent[tag], "scatter", token)
        for n, o_, r_ in zip(names, own, recv):
            res[n] = _update_matrix(r_, o_, me, *[f[n] for f in fams], "update_" + n)
            token = res[n][1]
    (sown, vown), (srecv, vrecv) = _send_wait("scatter_small_wait", *sent["small"], "scatter", token)
    small = _update_small(me, srecv, vrecv, sown, vown, *[[two_d(f[n]) for n in _SMALL] for f in fams])
    for i, n in enumerate(_SMALL):
        res[n] = [fam[i].reshape(w[n].shape) for fam in small[0:4]]
    for n in _TRANSPOSED:
        res[n] = [t.transpose(0, 2, 1) for t in res[n]]
    outs = [small[4][0, 0], grad_x]
    for k in range(4):
        outs += [res[n][k] for n in _WEIGHTS]
    return tuple(outs)
```

```python
import functools

import jax
import jax.numpy as jnp
from jax import lax
from jax.experimental import pallas as pl
from jax.experimental.pallas import tpu as pltpu

F32 = jnp.float32
BF16 = jnp.bfloat16

D = 1024
N_META = 16
C_CONV = 512
CONV_W = 31
GLA_H = 4
GLA_DK = 64
GLA_DV = 128
GLA_K = GLA_H * GLA_DK
GLA_V = GLA_H * GLA_DV
RANK = 16
RANK_P = 128
TAU = 16.0
CHUNK = 64
LEAD = CHUNK
ZROWS = LEAD - N_META
D_IN = 2 * C_CONV + 2 * GLA_K + 2 * GLA_V + RANK
D_INP = D_IN - RANK + RANK_P
D_FF = 2816
FF_CHUNK = 1408
FF_SPLIT = (0, 1536, D_FF)
RMS_EPS = 1e-6
LN_EPS = 1e-5
N_DEV = 8

ADAM_LR = 0.001
ADAM_B1 = 0.9
ADAM_B2 = 0.999
ADAM_EPS = 1e-08
ADAM_WD = 0.01
ADAM_STEP = 10

VMEM_LIMIT = 60 * 1024 * 1024
ROW_TILE = 1056
FFN_ROW_TILE = 352
DW_ROW_TILE = 1408
MESH = pl.DeviceIdType.MESH

_NN = (((1,), (0,)), ((), ()))
_NT = (((1,), (1,)), ((), ()))
_TN = (((0,), (0,)), ((), ()))


def _dot(a, b, dims=_NN):
    return lax.dot_general(a, b, dims, preferred_element_type=F32)


def _sigmoid(x):
    return 1.0 / (1.0 + jnp.exp(-x))


def _row_tile(rows, target):
    best = None
    for t in range(16, min(rows, target) + 1, 16):
        if rows % t == 0:
            best = t
    assert best is not None, rows
    return best


def _params(sem=None):
    return pltpu.CompilerParams(dimension_semantics=sem, vmem_limit_bytes=VMEM_LIMIT)


def _whole_vmem():
    return pl.BlockSpec(memory_space=pltpu.VMEM)


def _rows(tm, width):
    return pl.BlockSpec((tm, width), lambda i: (i, 0))


def _fixed(shape):
    return pl.BlockSpec(shape, lambda *_: (0,) * len(shape))


def _fwd_inproj(h0, g1, w_in):
    rows = h0.shape[0]
    tm = _row_tile(rows, ROW_TILE)

    def body(h_ref, g_ref, w_ref, uc_ref, qk_ref, vg_ref, lr_ref, n1_ref):
        h = h_ref[...]
        r = lax.rsqrt(jnp.mean(h * h, axis=-1, keepdims=True) + RMS_EPS)
        n = (h * r * g_ref[...]).astype(BF16)
        n1_ref[...] = n
        uc_ref[...] = _dot(n, w_ref[:, 0:1024]).astype(BF16)
        qk_ref[...] = _dot(n, w_ref[:, 1024:1536]).astype(BF16)
        vg_ref[...] = _dot(n, w_ref[:, 1536:2560]).astype(BF16)
        lr_ref[...] = _dot(n, w_ref[:, 2560:2688]).astype(BF16)

    return pl.pallas_call(
        body, name="fwd_inproj", grid=(rows // tm,),
        in_specs=[_rows(tm, D), _fixed((1, D)), _whole_vmem()],
        out_specs=[_rows(tm, 1024), _rows(tm, 512), _rows(tm, 1024), _rows(tm, RANK_P), _rows(tm, D)],
        out_shape=[jax.ShapeDtypeStruct((rows, 1024), BF16), jax.ShapeDtypeStruct((rows, 512), BF16),
                   jax.ShapeDtypeStruct((rows, 1024), BF16), jax.ShapeDtypeStruct((rows, RANK_P), BF16),
                   jax.ShapeDtypeStruct((rows, D), BF16)],
        compiler_params=_params(("parallel",)),
    )(h0, g1, w_in)


def _mid_rows(yc, yg, h0, tgt, w_out, wg, wu, wd, g2, g3, token, rows_per_example):
    rows = h0.shape[0]
    tm = _row_tile(rows, FFN_ROW_TILE)
    ff_blocks = [slice(lo, hi) for lo, hi in zip(FF_SPLIT[:-1], FF_SPLIT[1:])]

    def body(yc_ref, yg_ref, h0_ref, t_ref, wo_ref, wg_ref, wu_ref, wd_ref, g2_ref, g3_ref, token_ref,
             n2_ref, f_ref, da_ref, db_ref, dh2_ref, dh1_ref, dh1b_ref, dyc_ref, dyg_ref, part_ref):
        i = pl.program_id(0)
        h1 = h0_ref[...] + _dot(yc_ref[...], wo_ref[0:C_CONV, :]) + _dot(yg_ref[...], wo_ref[C_CONV:D, :])
        r2 = lax.rsqrt(jnp.mean(h1 * h1, axis=-1, keepdims=True) + RMS_EPS)
        xh2 = h1 * r2
        n2 = (xh2 * g2_ref[...]).astype(BF16)
        n2_ref[...] = n2
        y2 = jnp.zeros((tm, D), F32)
        for cs in ff_blocks:
            a = _dot(n2, wg_ref[cs, :], _NT)
            b = _dot(n2, wu_ref[cs, :], _NT)
            f = (a * _sigmoid(a) * b).astype(BF16)
            f_ref[:, cs] = f
            da_ref[:, cs] = a.astype(BF16)
            db_ref[:, cs] = b.astype(BF16)
            y2 = y2 + _dot(f, wd_ref[cs, :])
        h2 = h1 + y2
        r3 = lax.rsqrt(jnp.mean(h2 * h2, axis=-1, keepdims=True) + RMS_EPS)
        xh3 = h2 * r3
        g3 = g3_ref[...]
        pos = (i * tm + lax.broadcasted_iota(jnp.int32, (tm, 1), 0)) % rows_per_example
        valid = pos >= LEAD
        err = jnp.where(valid, xh3 * g3 - t_ref[...], 0.0)
        loss = 0.5 / D * jnp.sum(jnp.sum(err * err, axis=-1, keepdims=True), axis=0, keepdims=True)
        dy = err * (1.0 / D)
        dg3 = jnp.sum(dy * xh3, axis=0, keepdims=True)
        dxh = dy * g3
        dh2 = r3 * (dxh - xh3 * jnp.mean(dxh * xh3, axis=-1, keepdims=True))
        dh2b = dh2.astype(BF16)
        dh2_ref[...] = dh2b
        dn2 = jnp.zeros((tm, D), F32)
        for cs in ff_blocks:
            df = _dot(dh2b, wd_ref[cs, :], _NT)
            a = da_ref[:, cs].astype(F32)
            b = db_ref[:, cs].astype(F32)
            sg = _sigmoid(a)
            da = (df * b * sg * (1.0 + a * (1.0 - sg))).astype(BF16)
            db = (df * a * sg).astype(BF16)
            da_ref[:, cs] = da
            db_ref[:, cs] = db
            dn2 = dn2 + _dot(da, wg_ref[cs, :]) + _dot(db, wu_ref[cs, :])
        dg2 = jnp.sum(dn2 * xh2, axis=0, keepdims=True)
        dxh2 = dn2 * g2_ref[...]
        dh1 = dh2 + r2 * (dxh2 - xh2 * jnp.mean(dxh2 * xh2, axis=-1, keepdims=True))
        dh1_ref[...] = dh1
        dh1b = dh1.astype(BF16)
        dh1b_ref[...] = dh1b
        dyc_ref[...] = _dot(dh1b, wo_ref[0:C_CONV, :], _NT)
        dyg_ref[...] = _dot(dh1b, wo_ref[C_CONV:D, :], _NT)

        @pl.when(i == 0)
        def _():
            part_ref[...] = jnp.zeros_like(part_ref)

        part_ref[0:1, :] += dg3
        part_ref[1:2, :] += dg2
        part_ref[2:3, :] += jnp.broadcast_to(loss, (1, D))

    return pl.pallas_call(
        body, name="mid_rows", grid=(rows // tm,),
        in_specs=[_rows(tm, C_CONV), _rows(tm, GLA_V), _rows(tm, D), _rows(tm, D), _whole_vmem(), _whole_vmem(),
                  _whole_vmem(), _whole_vmem(), _fixed((1, D)), _fixed((1, D)), _fixed((8, 128))],
        out_specs=[_rows(tm, D), _rows(tm, D_FF), _rows(tm, D_FF), _rows(tm, D_FF), _rows(tm, D), _rows(tm, D),
                   _rows(tm, D), _rows(tm, C_CONV), _rows(tm, GLA_V), _fixed((8, D))],
        out_shape=[jax.ShapeDtypeStruct((rows, D), BF16)] + [jax.ShapeDtypeStruct((rows, D_FF), BF16)] * 3
        + [jax.ShapeDtypeStruct((rows, D), BF16), jax.ShapeDtypeStruct((rows, D), F32),
           jax.ShapeDtypeStruct((rows, D), BF16), jax.ShapeDtypeStruct((rows, C_CONV), F32),
           jax.ShapeDtypeStruct((rows, GLA_V), F32), jax.ShapeDtypeStruct((8, D), F32)],
        compiler_params=_params(("arbitrary",)),
    )(yc, yg, h0, tgt, w_out, wg, wu, wd, g2, g3, token)


def _bwd_inproj(duc, dqk, dvg, dlr, dh1, h0, w_in, g1, token, rows_per_example):
    rows = h0.shape[0]
    n_ex = rows // rows_per_example
    tm = _row_tile(rows_per_example, ROW_TILE)
    tiles_per_example = rows_per_example // tm
    n_steps = rows // tm

    def body(duc_ref, dqk_ref, dvg_ref, dlr_ref, dh1_ref, h_ref, w_ref, g_ref, token_ref, gx_ref, part_ref, dmeta_ref,
             buf_ref, sems):
        dn = (_dot(duc_ref[...], w_ref[:, 0:1024], _NT) + _dot(dqk_ref[...], w_ref[:, 1024:1536], _NT)
              + _dot(dvg_ref[...], w_ref[:, 1536:2560], _NT) + _dot(dlr_ref[...], w_ref[:, 2560:2688], _NT))
        h = h_ref[...]
        r = lax.rsqrt(jnp.mean(h * h, axis=-1, keepdims=True) + RMS_EPS)
        xh = h * r
        dg = jnp.sum(dn * xh, axis=0, keepdims=True)
        dxh = dn * g_ref[...]
        dh0 = dh1_ref[...] + r * (dxh - xh * jnp.mean(dxh * xh, axis=-1, keepdims=True))
        i = pl.program_id(0)

        def copies(step):
            slot, b, j = step % 2, step // tiles_per_example, step % tiles_per_example
            out = [(j == 0, pltpu.make_async_copy(buf_ref.at[slot, pl.ds(LEAD, tm - LEAD)],
                                                   gx_ref.at[b, pl.ds(0, tm - LEAD)], sems.at[slot]))]
            if tiles_per_example > 1:
                out.append((j != 0, pltpu.make_async_copy(
                    buf_ref.at[slot], gx_ref.at[b, pl.ds(pl.multiple_of(jnp.maximum(j * tm - LEAD, 0), 8), tm)],
                    sems.at[slot])))
            return out

        def each(step, act):
            for cond, cp in copies(step):
                pl.when(cond)(functools.partial(act, cp))

        @pl.when(i >= 2)
        def _():
            each(i - 2, lambda cp: cp.wait())

        buf_ref[i % 2] = dh0
        each(i, lambda cp: cp.start())

        @pl.when(i == n_steps - 1)
        def _():
            each(i, lambda cp: cp.wait())
            if n_steps > 1:
                each(i - 1, lambda cp: cp.wait())

        @pl.when(i == 0)
        def _():
            part_ref[...] = jnp.zeros_like(part_ref)
            dmeta_ref[...] = jnp.zeros_like(dmeta_ref)

        part_ref[0:1, :] += dg

        @pl.when(i % tiles_per_example == 0)
        def _():
            dmeta_ref[...] += dh0[ZROWS:LEAD, :]

    return pl.pallas_call(
        body, name="bwd_inproj", grid=(n_steps,),
        in_specs=[_rows(tm, 1024), _rows(tm, 512), _rows(tm, 1024), _rows(tm, RANK_P), _rows(tm, D), _rows(tm, D),
                  _whole_vmem(), _fixed((1, D)), _fixed((8, 128))],
        out_specs=[_any(), _fixed((8, D)), _fixed((N_META, D))],
        out_shape=[jax.ShapeDtypeStruct((n_ex, rows_per_example - LEAD, D), F32), jax.ShapeDtypeStruct((8, D), F32),
                   jax.ShapeDtypeStruct((N_META, D), F32)],
        scratch_shapes=[pltpu.VMEM((2, tm, D), F32), pltpu.SemaphoreType.DMA((2,))],
        compiler_params=_params(("arbitrary",)),
    )(duc, dqk, dvg, dlr, dh1, h0, w_in, g1, token)


def _dw_blocked(a, bs, width, name):
    rows, m = a.shape
    ws = [b.shape[1] for b in bs]
    assert sum(ws) >= N_DEV * width
    tk = _row_tile(rows, DW_ROW_TILE)
    nk = rows // tk

    def body(a_ref, *refs):
        b_refs, o_ref, acc_ref = refs[:len(bs)], refs[len(bs)], refs[len(bs) + 1]
        k = pl.program_id(0)

        @pl.when(k == 0)
        def _():
            acc_ref[...] = jnp.zeros_like(acc_ref)

        at = a_ref[...].T
        off = 0
        for b_ref, w in zip(b_refs, ws):
            acc_ref[:, off:off + w] += _dot(at, b_ref[...])
            off += w

        @pl.when(k == nk - 1)
        def _():
            for d in range(N_DEV):
                o_ref[d] = acc_ref[:, d * width:(d + 1) * width].astype(BF16)

    return pl.pallas_call(
        body, name=name, grid=(nk,),
        in_specs=[_rows(tk, m)] + [_rows(tk, w) for w in ws],
        out_specs=_fixed((N_DEV, m, width)),
        out_shape=jax.ShapeDtypeStruct((N_DEV, m, width), BF16),
        scratch_shapes=[pltpu.VMEM((m, sum(ws)), F32)],
        compiler_params=_params(("arbitrary",)),
    )(a, *bs)


def _matmul_tn(a, b, name):
    rows, m = a.shape
    n = b.shape[1]
    tk = _row_tile(rows, DW_ROW_TILE)
    tn = n if n <= 1024 else FF_CHUNK
    tm_ = m if m <= 1024 else FF_CHUNK
    assert n % tn == 0 and m % tm_ == 0
    nk = rows // tk

    def body(a_ref, b_ref, o_ref, acc_ref):
        k = pl.program_id(2)

        @pl.when(k == 0)
        def _():
            acc_ref[...] = jnp.zeros_like(acc_ref)

        acc_ref[...] += _dot(a_ref[...], b_ref[...], _TN)

        @pl.when(k == nk - 1)
        def _():
            o_ref[...] = acc_ref[...].astype(BF16)

    return pl.pallas_call(
        body, name=name, grid=(m // tm_, n // tn, nk),
        in_specs=[pl.BlockSpec((tk, tm_), lambda i, j, k: (k, i)), pl.BlockSpec((tk, tn), lambda i, j, k: (k, j))],
        out_specs=pl.BlockSpec((tm_, tn), lambda i, j, k: (i, j)),
        out_shape=jax.ShapeDtypeStruct((m, n), BF16),
        scratch_shapes=[pltpu.VMEM((tm_, tn), F32)],
        compiler_params=_params(("parallel", "parallel", "arbitrary")),
    )(a, b)


HALO = 32
LN_ROWS = 352
LANES = 128


def _shifted(win, offsets):
    for r in range(8):
        js = [j for j, k in enumerate(offsets) if k % 8 == r]
        if js:
            rolled = win if r == 0 else pltpu.roll(win, CHUNK + HALO - r, 0)
            for j in js:
                yield j, rolled[offsets[j] - r:offsets[j] - r + CHUNK]


def _glu_into(uc_ref, vs_ref, n_chunk):
    vs_ref[0:CHUNK, :] = jnp.zeros((CHUNK, C_CONV), F32)

    def glu(i, carry):
        base = pl.multiple_of(i * CHUNK, CHUNK)
        val = uc_ref[pl.ds(base, CHUNK), 0:C_CONV].astype(F32)
        gate = uc_ref[pl.ds(base, CHUNK), C_CONV:2 * C_CONV].astype(F32)
        vs_ref[pl.ds(base + CHUNK, CHUNK), :] = val * _sigmoid(gate)
        return carry

    lax.fori_loop(0, n_chunk, glu, 0, unroll=3)


def _fwd_conv(uc, conv_w, conv_b, ln_g, ln_b, token, n_ex):
    rows = uc.shape[0]
    lp = rows // n_ex
    n_chunk = lp // CHUNK

    def body(uc_ref, w_ref, b_ref, lg_ref, lb_ref, token_ref, ypre_ref, yc_ref, vs_ref):
        _glu_into(uc_ref, vs_ref, n_chunk)

        def conv(i, carry):
            base = pl.multiple_of(i * CHUNK, CHUNK)
            for lb in range(C_CONV // LANES):
                ls = slice(lb * LANES, (lb + 1) * LANES)
                win = vs_ref[pl.ds(base + CHUNK - HALO, CHUNK + HALO), ls]
                acc = jnp.broadcast_to(b_ref[:, ls], (CHUNK, LANES))
                for j, rows_j in _shifted(win, [HALO - (CONV_W - 1) + j for j in range(CONV_W)]):
                    acc = acc + w_ref[j:j + 1, ls] * rows_j
                ypre_ref[pl.ds(base, CHUNK), ls] = acc
            return carry

        lax.fori_loop(0, n_chunk, conv, 0, unroll=3)

        ln_rows = _row_tile(lp, LN_ROWS)

        def norm(i, carry):
            base = pl.multiple_of(i * ln_rows, 16)
            y = ypre_ref[pl.ds(base, ln_rows), :]
            mu = jnp.mean(y, axis=-1, keepdims=True)
            yc_ = y - mu
            rstd = lax.rsqrt(jnp.mean(yc_ * yc_, axis=-1, keepdims=True) + LN_EPS)
            s = yc_ * rstd * lg_ref[...] + lb_ref[...]
            yc_ref[pl.ds(base, ln_rows), :] = (s * _sigmoid(s)).astype(BF16)
            return carry

        lax.fori_loop(0, lp // ln_rows, norm, 0)

    ex = lambda w: pl.BlockSpec((lp, w), lambda b: (b, 0))
    return pl.pallas_call(
        body, name="fwd_conv", grid=(n_ex,),
        in_specs=[ex(2 * C_CONV), _fixed((32, C_CONV)), _fixed((1, C_CONV)), _fixed((1, C_CONV)), _fixed((1, C_CONV)),
                  _fixed((8, 128))],
        out_specs=[ex(C_CONV), ex(C_CONV)],
        out_shape=[jax.ShapeDtypeStruct((rows, C_CONV), F32), jax.ShapeDtypeStruct((rows, C_CONV), BF16)],
        scratch_shapes=[pltpu.VMEM((lp + CHUNK, C_CONV), F32)],
        compiler_params=_params(("parallel",)),
    )(uc, conv_w, conv_b, ln_g, ln_b, token)


def _bwd_conv(uc, ypre, dyc, conv_w, ln_g, ln_b, token, n_ex):
    rows = uc.shape[0]
    lp = rows // n_ex
    n_chunk = lp // CHUNK

    def body(uc_ref, ypre_ref, dyc_ref, w_ref, lg_ref, lb_ref, token_ref, duc_ref, dw_ref, dvec_ref, vs_ref, dys_ref,
             dwacc_ref):
        _glu_into(uc_ref, vs_ref, n_chunk)
        dys_ref[pl.ds(lp, CHUNK), :] = jnp.zeros((CHUNK, C_CONV), F32)
        dwacc_ref[...] = jnp.zeros_like(dwacc_ref)

        ln_rows = _row_tile(lp, LN_ROWS)

        def ln_bwd(i, carry):
            dcb, dlg, dlb = carry
            base = pl.multiple_of(i * ln_rows, 16)
            y = ypre_ref[pl.ds(base, ln_rows), :]
            mu = jnp.mean(y, axis=-1, keepdims=True)
            yc_ = y - mu
            rstd = lax.rsqrt(jnp.mean(yc_ * yc_, axis=-1, keepdims=True) + LN_EPS)
            xh = yc_ * rstd
            s = xh * lg_ref[...] + lb_ref[...]
            sg = _sigmoid(s)
            ds = dyc_ref[pl.ds(base, ln_rows), :] * (sg * (1.0 + s * (1.0 - sg)))
            dxh = ds * lg_ref[...]
            dy = rstd * (dxh - jnp.mean(dxh, axis=-1, keepdims=True) - xh * jnp.mean(dxh * xh, axis=-1, keepdims=True))
            dys_ref[pl.ds(base, ln_rows), :] = dy
            return (dcb + jnp.sum(dy, axis=0, keepdims=True), dlg + jnp.sum(ds * xh, axis=0, keepdims=True),
                    dlb + jnp.sum(ds, axis=0, keepdims=True))

        zero = jnp.zeros((1, C_CONV), F32)
        dcb, dlg, dlb = lax.fori_loop(0, lp // ln_rows, ln_bwd, (zero, zero, zero))

        @pl.when(pl.program_id(0) == 0)
        def _():
            dvec_ref[...] = jnp.zeros_like(dvec_ref)
            dw_ref[...] = jnp.zeros_like(dw_ref)

        dvec_ref[0:1, :] += dcb
        dvec_ref[1:2, :] += dlg
        dvec_ref[2:3, :] += dlb

        def taps(i, carry):
            base = pl.multiple_of(i * CHUNK, CHUNK)
            for lb in range(C_CONV // LANES):
                ls = slice(lb * LANES, (lb + 1) * LANES)
                dwin = dys_ref[pl.ds(base, CHUNK + HALO), ls]
                vwin = vs_ref[pl.ds(base + CHUNK - HALO, CHUNK + HALO), ls]
                dy = dwin[0:CHUNK]
                acc = jnp.zeros((CHUNK, LANES), F32)
                for j, rows_j in _shifted(dwin, [CONV_W - 1 - j for j in range(CONV_W)]):
                    acc = acc + w_ref[j:j + 1, ls] * rows_j
                for j, rows_j in _shifted(vwin, [HALO - (CONV_W - 1) + j for j in range(CONV_W)]):
                    dwacc_ref[8 * j:8 * j + 8, ls] += jnp.sum((dy * rows_j).reshape(CHUNK // 8, 8, LANES), axis=0)
                val = uc_ref[pl.ds(base, CHUNK), ls].astype(F32)
                gate = uc_ref[pl.ds(base, CHUNK), C_CONV + lb * LANES:C_CONV + (lb + 1) * LANES].astype(F32)
                sg = _sigmoid(gate)
                duc_ref[pl.ds(base, CHUNK), ls] = (acc * sg).astype(BF16)
                duc_ref[pl.ds(base, CHUNK), C_CONV + lb * LANES:C_CONV + (lb + 1) * LANES] = (
                    acc * val * sg * (1.0 - sg)).astype(BF16)
            return carry

        lax.fori_loop(0, n_chunk, taps, 0, unroll=3)
        for j in range(CONV_W):
            dw_ref[j:j + 1, :] += jnp.sum(dwacc_ref[8 * j:8 * j + 8, :], axis=0, keepdims=True)

    ex = lambda w: pl.BlockSpec((lp, w), lambda b: (b, 0))
    return pl.pallas_call(
        body, name="bwd_conv", grid=(n_ex,),
        in_specs=[ex(2 * C_CONV), ex(C_CONV), ex(C_CONV), _fixed((32, C_CONV)), _fixed((1, C_CONV)), _fixed((1, C_CONV)),
                  _fixed((8, 128))],
        out_specs=[ex(2 * C_CONV), _fixed((32, C_CONV)), _fixed((8, C_CONV))],
        out_shape=[jax.ShapeDtypeStruct((rows, 2 * C_CONV), BF16), jax.ShapeDtypeStruct((32, C_CONV), F32),
                   jax.ShapeDtypeStruct((8, C_CONV), F32)],
        scratch_shapes=[pltpu.VMEM((lp + CHUNK, C_CONV), F32), pltpu.VMEM((lp + CHUNK, C_CONV), F32),
                        pltpu.VMEM((8 * 32, C_CONV), F32)],
        compiler_params=_params(("arbitrary",)),
    )(uc, ypre, dyc, conv_w, ln_g, ln_b, token)


def _seg_chunks(n_chunk):
    return max(c for c in (11, 3, 1) if n_chunk % c == 0)


def _block_mask(shape, row_block, lane_block):
    return (lax.broadcasted_iota(jnp.int32, shape, 0) // row_block) == (lax.broadcasted_iota(jnp.int32, shape, 1) // lane_block)


def _per_head_rows(x, mask):
    return jnp.where(mask, jnp.concatenate([x] * GLA_H, axis=0), 0)


def _fold_heads(full, lane_block):
    lane = lax.broadcasted_iota(jnp.int32, (1, full.shape[1]), 1) // lane_block
    out = jnp.where(lane == 0, full[0:CHUNK], 0.0)
    for h in range(1, GLA_H):
        out = out + jnp.where(lane == h, full[h * CHUNK:(h + 1) * CHUNK], 0.0)
    return out


PAIRS = GLA_H // 2


def _expand_state(blocks):
    lane = lax.broadcasted_iota(jnp.int32, (GLA_DV, 128), 1) // GLA_DK
    zero = jnp.zeros_like(blocks[0])
    rows = []
    for h in range(GLA_H):
        p, hh = divmod(h, 2)
        mine = jnp.where(lane == hh, blocks[p], 0)
        rows.append(jnp.concatenate([mine if q == p else zero for q in range(PAIRS)], axis=1))
    return jnp.concatenate(rows, axis=0)


def _compact_state(full, p):
    lane = lax.broadcasted_iota(jnp.int32, (GLA_DV, 128), 1) // GLA_DK
    ls = slice(128 * p, 128 * (p + 1))
    return jnp.where(lane == 0, full[2 * p * GLA_DV:(2 * p + 1) * GLA_DV, ls], full[(2 * p + 1) * GLA_DV:(2 * p + 2) * GLA_DV, ls])


def _causal_heads():
    return (lax.broadcasted_iota(jnp.int32, (CHUNK, GLA_H * CHUNK), 1) % CHUNK) <= lax.broadcasted_iota(
        jnp.int32, (CHUNK, GLA_H * CHUNK), 0)


def _cumsum_rows(x):
    row = lax.broadcasted_iota(jnp.int32, x.shape, 0)
    s = 1
    while s < CHUNK:
        x = x + jnp.where(row >= s, pltpu.roll(x, s, 0), 0.0)
        s *= 2
    return x


def _rev_cumsum_rows(x):
    row = lax.broadcasted_iota(jnp.int32, x.shape, 0)
    s = 1
    while s < CHUNK:
        x = x + jnp.where(row < CHUNK - s, pltpu.roll(x, CHUNK - s, 0), 0.0)
        s *= 2
    return x


def _gate_terms(lr_ref, w2_ref, gb_ref, rs, first_pos):
    z = _dot(lr_ref[rs, :].astype(BF16), w2_ref[...]) + gb_ref[...]
    la = (jnp.minimum(z, 0.0) - jnp.log(1.0 + jnp.exp(-jnp.abs(z)))) * (1.0 / TAU)
    pos = first_pos + lax.broadcasted_iota(jnp.int32, (CHUNK, 1), 0)
    live = pos >= ZROWS
    la = jnp.where(live, la, 0.0)
    return z, live, _cumsum_rows(la)


def _fwd_gla(qk, vg, lr, w2p, gb, ng, token, n_ex):
    rows = qk.shape[0]
    lp = rows // n_ex
    n_chunk = lp // CHUNK
    sc = _seg_chunks(n_chunk)
    n_seg = n_chunk // sc
    seg = sc * CHUNK

    def body(qk_ref, vg_ref, lr_ref, w2_ref, gb_ref, ng_ref, token_ref, yg_ref, o_ref, st_ref, state_ref):
        sidx = pl.program_id(1)

        @pl.when(sidx == 0)
        def _():
            state_ref[...] = jnp.zeros_like(state_ref)

        causal = _causal_heads()
        k_mask = _block_mask((GLA_H * CHUNK, GLA_K), CHUNK, GLA_DK)
        v_mask = _block_mask((GLA_H * CHUNK, GLA_V), CHUNK, GLA_DV)

        def chunk(ci, carry):
            base = pl.multiple_of(ci * CHUNK, CHUNK)
            rs = pl.ds(base, CHUNK)
            _, _, bcum = _gate_terms(lr_ref, w2_ref, gb_ref, rs, (sidx * sc + ci) * CHUNK)
            bl = bcum[CHUNK - 1:CHUNK, :]
            q = qk_ref[rs, 0:GLA_K].astype(F32)
            k = qk_ref[rs, GLA_K:2 * GLA_K].astype(F32)
            qt = (q * (GLA_DK ** -0.5) * jnp.exp(bcum)).astype(BF16)
            kt = (k * jnp.exp(-bcum)).astype(BF16)
            kh = (k * jnp.exp(bl - bcum)).astype(BF16)
            vb = vg_ref[rs, 0:GLA_V].astype(BF16)
            state = [state_ref[p] for p in range(PAIRS)]
            for p in range(PAIRS):
                st_ref[ci, p] = state[p]
            a = jnp.where(causal, _dot(qt, _per_head_rows(kt, k_mask), _NT), 0.0)
            o = _dot(a.astype(BF16), _per_head_rows(vb, v_mask)) + _dot(
                qt, _expand_state([s.astype(BF16) for s in state]), _NT)
            o_ref[rs, :] = o
            for h in range(GLA_H):
                hs = slice(h * GLA_DV, (h + 1) * GLA_DV)
                oh = o[:, hs]
                ro = lax.rsqrt(jnp.mean(oh * oh, axis=-1, keepdims=True) + RMS_EPS)
                g = vg_ref[rs, GLA_V + h * GLA_DV:GLA_V + (h + 1) * GLA_DV].astype(F32)
                yg_ref[rs, hs] = (oh * ro * ng_ref[...] * g * _sigmoid(g)).astype(BF16)
            kv = _dot(vb, kh, _TN)
            decay = jnp.exp(bl)
            for p in range(PAIRS):
                state_ref[p] = state[p] * decay[:, 128 * p:128 * (p + 1)] + _compact_state(kv, p)
            return carry

        lax.fori_loop(0, sc, chunk, 0, unroll=True)

    sg = lambda w: pl.BlockSpec((seg, w), lambda b, s: (b * n_seg + s, 0))
    return pl.pallas_call(
        body, name="fwd_gla", grid=(n_ex, n_seg),
        in_specs=[sg(2 * GLA_K), sg(2 * GLA_V), sg(RANK_P), _fixed((RANK_P, GLA_K)), _fixed((1, GLA_K)), _fixed((1, GLA_DV)),
                  _fixed((8, 128))],
        out_specs=[sg(GLA_V), sg(GLA_V), pl.BlockSpec((sc, PAIRS, GLA_DV, 128), lambda b, s: (b * n_seg + s, 0, 0, 0))],
        out_shape=[jax.ShapeDtypeStruct((rows, GLA_V), BF16), jax.ShapeDtypeStruct((rows, GLA_V), F32),
                   jax.ShapeDtypeStruct((n_ex * n_chunk, PAIRS, GLA_DV, 128), F32)],
        scratch_shapes=[pltpu.VMEM((PAIRS, GLA_DV, 128), F32)],
        compiler_params=_params(("parallel", "arbitrary")),
    )(qk, vg, lr, w2p, gb, ng, token)


def _bwd_gla(qk, vg, lr, o, st, dyg, w2p, gb, ng, yc, yg, dh1b, token, n_ex):
    rows = qk.shape[0]
    lp = rows // n_ex
    n_chunk = lp // CHUNK
    sc = _seg_chunks(n_chunk)
    n_seg = n_chunk // sc
    seg = sc * CHUNK

    def body(qk_ref, vg_ref, lr_ref, o_ref, st_ref, dyg_ref, w2_ref, gb_ref, ng_ref, yc_ref, yg_ref, dh1_ref, token_ref,
             dqk_ref, dvg_ref, dlr_ref, dw2_ref, dvec_ref, dwo_ref, gt_ref, dz_ref, dwo_acc):
        step = pl.program_id(1)
        sidx = n_seg - 1 - step
        first = (step == 0) & (pl.program_id(0) == 0)

        @pl.when(step == 0)
        def _():
            gt_ref[...] = jnp.zeros_like(gt_ref)

        @pl.when(first)
        def _():
            dw2_ref[...] = jnp.zeros_like(dw2_ref)
            dvec_ref[...] = jnp.zeros_like(dvec_ref)
            dwo_acc[...] = jnp.zeros_like(dwo_acc)

        d1 = dh1_ref[...]
        dwo_acc[0:C_CONV, :] += _dot(yc_ref[...], d1, _TN)
        dwo_acc[C_CONV:D, :] += _dot(yg_ref[...], d1, _TN)

        @pl.when((step == n_seg - 1) & (pl.program_id(0) == n_ex - 1))
        def _():
            dwo_ref[...] = dwo_acc[...].astype(BF16)

        causal = _causal_heads()
        k_mask = _block_mask((GLA_H * CHUNK, GLA_K), CHUNK, GLA_DK)
        v_mask = _block_mask((GLA_H * CHUNK, GLA_V), CHUNK, GLA_DV)
        last_row = lax.broadcasted_iota(jnp.int32, (CHUNK, 1), 0) == CHUNK - 1
        ng = ng_ref[...]

        def chunk(ii, dng):
            ci = sc - 1 - ii
            base = pl.multiple_of(ci * CHUNK, CHUNK)
            rs = pl.ds(base, CHUNK)
            z, live, bcum = _gate_terms(lr_ref, w2_ref, gb_ref, rs, (sidx * sc + ci) * CHUNK)
            bl = bcum[CHUNK - 1:CHUNK, :]
            ebl = jnp.exp(bl)
            q = qk_ref[rs, 0:GLA_K].astype(F32)
            k = qk_ref[rs, GLA_K:2 * GLA_K].astype(F32)
            eb = jnp.exp(bcum)
            enb = jnp.exp(-bcum)
            ehb = jnp.exp(bl - bcum)
            qt = q * (GLA_DK ** -0.5) * eb
            kt = k * enb
            kh = k * ehb
            qtb = qt.astype(BF16)
            vb = vg_ref[rs, 0:GLA_V].astype(BF16)
            k_rows = _per_head_rows(kt.astype(BF16), k_mask)
            v_rows = _per_head_rows(vb, v_mask)
            gt = [gt_ref[p] for p in range(PAIRS)]
            gtb = _expand_state([g_.astype(BF16) for g_ in gt])
            s_in = [st_ref[ci, p] for p in range(PAIRS)]
            dos = []
            for h in range(GLA_H):
                hs = slice(h * GLA_DV, (h + 1) * GLA_DV)
                gs = slice(GLA_V + h * GLA_DV, GLA_V + (h + 1) * GLA_DV)
                oh = o_ref[rs, hs]
                ro = lax.rsqrt(jnp.mean(oh * oh, axis=-1, keepdims=True) + RMS_EPS)
                on = oh * ro
                g = vg_ref[rs, gs].astype(F32)
                sg = _sigmoid(g)
                dout = dyg_ref[rs, hs]
                dvg_ref[rs, gs] = (dout * on * ng * (sg * (1.0 + g * (1.0 - sg)))).astype(BF16)
                dw = dout * g * sg
                dng = dng + jnp.sum(dw * on, axis=0, keepdims=True)
                don = dw * ng
                dos.append((ro * (don - on * jnp.mean(don * on, axis=-1, keepdims=True))).astype(BF16))
            dob = jnp.concatenate(dos, axis=1)
            a = jnp.where(causal, _dot(qtb, k_rows, _NT), 0.0).astype(BF16)
            da = jnp.where(causal, _dot(dob, v_rows, _NT), 0.0).astype(BF16)
            dv = _fold_heads(_dot(a, dob, _TN), GLA_DV) + _dot(kh.astype(BF16), gtb, _NT)
            dvg_ref[rs, 0:GLA_V] = dv.astype(BF16)
            dkh = _dot(vb, gtb)
            dqt = _dot(da, k_rows) + _dot(dob, _expand_state([s_.astype(BF16) for s_ in s_in]))
            dkt = _fold_heads(_dot(da, qtb, _TN), GLA_DK)
            dbl = jnp.concatenate([jnp.sum(gt[p] * s_in[p], axis=0, keepdims=True) for p in range(PAIRS)], axis=1) * ebl
            dbl = dbl + jnp.sum(dkh * kh, axis=0, keepdims=True)
            dqk_ref[rs, 0:GLA_K] = (dqt * (GLA_DK ** -0.5) * eb).astype(BF16)
            dqk_ref[rs, GLA_K:2 * GLA_K] = (dkt * enb + dkh * ehb).astype(BF16)
            db = dqt * qt - dkt * kt - dkh * kh
            db = jnp.where(last_row, db + dbl, db)
            dla = jnp.where(live, _rev_cumsum_rows(db), 0.0)
            dz_ref[rs, :] = dla * (1.0 / TAU) * (1.0 - _sigmoid(z))
            dstate = _dot(dob, qtb, _TN)
            for p in range(PAIRS):
                gt_ref[p] = _compact_state(dstate, p) + gt[p] * ebl[:, 128 * p:128 * (p + 1)]
            return dng

        dng = lax.fori_loop(0, sc, chunk, jnp.zeros((1, GLA_DV), F32), unroll=True)
        dz = dz_ref[...]
        dzb = dz.astype(BF16)
        dlr_ref[...] = _dot(dzb, w2_ref[...], _NT).astype(BF16)
        dw2_ref[...] += _dot(lr_ref[...].astype(BF16), dzb, _TN)
        dvec_ref[0:1, :] += jnp.sum(dz, axis=0, keepdims=True)
        dvec_ref[1:2, 0:GLA_DV] += dng

    sg_ = lambda w: pl.BlockSpec((seg, w), lambda b, s: (b * n_seg + n_seg - 1 - s, 0))
    return pl.pallas_call(
        body, name="bwd_gla", grid=(n_ex, n_seg),
        in_specs=[sg_(2 * GLA_K), sg_(2 * GLA_V), sg_(RANK_P), sg_(GLA_V),
                  pl.BlockSpec((sc, PAIRS, GLA_DV, 128), lambda b, s: (b * n_seg + n_seg - 1 - s, 0, 0, 0)), sg_(GLA_V),
                  _fixed((RANK_P, GLA_K)), _fixed((1, GLA_K)), _fixed((1, GLA_DV)), sg_(C_CONV), sg_(GLA_V), sg_(D),
                  _fixed((8, 128))],
        out_specs=[sg_(2 * GLA_K), sg_(2 * GLA_V), sg_(RANK_P), _fixed((RANK_P, GLA_K)), _fixed((8, GLA_K)),
                   _fixed((D, D))],
        out_shape=[jax.ShapeDtypeStruct((rows, 2 * GLA_K), BF16), jax.ShapeDtypeStruct((rows, 2 * GLA_V), BF16),
                   jax.ShapeDtypeStruct((rows, RANK_P), BF16), jax.ShapeDtypeStruct((RANK_P, GLA_K), F32),
                   jax.ShapeDtypeStruct((8, GLA_K), F32), jax.ShapeDtypeStruct((D, D), BF16)],
        scratch_shapes=[pltpu.VMEM((PAIRS, GLA_DV, 128), F32), pltpu.VMEM((seg, GLA_K), F32), pltpu.VMEM((D, D), F32)],
        compiler_params=_params(("arbitrary", "arbitrary")),
    )(qk, vg, lr, o, st, dyg, w2p, gb, ng, yc, yg, dh1b, token)


def _pad_rows(x, tgt):
    return jnp.pad(x, ((0, 0), (LEAD, 0), (0, 0))), jnp.pad(tgt, ((0, 0), (LEAD, 0), (0, 0)))


def _local_step(h0, tgt_p, p, pass_on, late_weights, send_early):
    n_ex, lp, _ = h0.shape
    rows = n_ex * lp
    meta = jnp.broadcast_to(p["meta"][None], (n_ex, N_META, D))
    h0 = lax.dynamic_update_slice(h0, meta, (0, ZROWS, 0)).reshape(rows, D)
    tgt_p = tgt_p.reshape(rows, D)

    uc, qk, vg, lr, n1 = _fwd_inproj(h0, p["g1"], p["w_in"])
    ypre, yc = _fwd_conv(uc, p["conv_w"], p["conv_b"], p["ln_g"], p["ln_b"], p["token"], n_ex)
    yg, o, st = _fwd_gla(qk, vg, lr, p["w2"], p["gb"], p["ng"], p["token"], n_ex)
    token = pass_on((yc, yg))
    w_out, wg, wu, wd = late_weights(token)
    n2, f, da, db, dh2, dh1, dh1b, dyc, dyg, part = _mid_rows(
        yc, yg, h0, tgt_p, w_out, wg, wu, wd, p["g2"], p["g3"], token, lp)
    g = {}
    token = send_early("ffn", [_matmul_tn(a_, b_, name).reshape(N_DEV, FF_S, D) for a_, b_, name in (
        (da, n2, "dw_gate"), (db, n2, "dw_up"), (f, dh2, "dw_down"))])
    dqk, dvg, dlr, g["w2"], g["gla_vec"], dw_out = _bwd_gla(
        qk, vg, lr, o, st, dyg, p["w2"], p["gb"], p["ng"], yc, yg, dh1b, token, n_ex)
    token = send_early("out", [dw_out.reshape(N_DEV, W_OUT_S, D)])
    duc, g["conv_w"], g["conv_vec"] = _bwd_conv(uc, ypre, dyc, p["conv_w"], p["ln_g"], p["ln_b"], token, n_ex)
    token = send_early("in", [_dw_blocked(n1, [duc, dqk, dvg, dlr], W_IN_S, "dw_in")])
    grad_x, g["in_vec"], g["meta"] = _bwd_inproj(duc, dqk, dvg, dlr, dh1, h0, p["w_in"], p["g1"], token, lp)
    g["ffn_vec"] = part
    return grad_x, g


W_IN_S = D_IN // N_DEV
W_OUT_S = D // N_DEV
FF_S = D_FF // N_DEV
CONV_S = C_CONV // N_DEV
GATE_S = GLA_K // N_DEV
SMALL_PACK = 64
CONV_ROW = 16
GATE_ROW = 48
VEC_ROWS = 16
_VEC_ROWS = (("norm_mix_g", D), ("conv_b", C_CONV), ("conv_ln_g", C_CONV), ("conv_ln_b", C_CONV), ("gla_gate_b", GLA_K),
             ("gla_norm_g", GLA_DV), ("norm_ffn_g", D), ("norm_final_g", D))
LOSS_ROW = len(_VEC_ROWS)


def _position():
    return lax.axis_index("x"), lax.axis_index("y"), lax.axis_index("c")


def _any():
    return pl.BlockSpec(memory_space=pl.ANY)


def _stage(mats, meta, conv_w, w2):
    n_t = len(mats) + 1

    def body(*refs):
        ins = refs[0:n_t - 1]
        meta_ref, cw_ref, w2_ref = refs[n_t - 1:n_t + 2]
        lands = refs[n_t + 2:2 * n_t + 2]
        shards = refs[2 * n_t + 2:3 * n_t + 2]
        sems = refs[3 * n_t + 2]
        for s_ref, w_ref in zip(shards, ins):
            s_ref[...] = w_ref[...].astype(BF16)
        sp = shards[n_t - 1]
        sp[...] = jnp.zeros_like(sp)
        sp[0:N_META, :] = meta_ref[...]
        sp[CONV_ROW:CONV_ROW + CONV_W, 0:CONV_S] = cw_ref[...]
        sp[GATE_ROW:GATE_ROW + RANK, 0:GATE_S] = w2_ref[...]
        x, y, c = _position()
        mine = [pltpu.make_async_copy(shards[t], lands[t].at[4 * x + 2 * y + c], sems.at[t]) for t in range(n_t)]
        for cp in mine:
            cp.start()
        for cp in mine:
            cp.wait()

    shard_shapes = [jax.ShapeDtypeStruct(m.shape, BF16) for m in mats] + [jax.ShapeDtypeStruct((SMALL_PACK, 128), F32)]
    res = pl.pallas_call(
        body, name="stage",
        out_shape=[jax.ShapeDtypeStruct((N_DEV,) + s.shape, s.dtype) for s in shard_shapes] + shard_shapes,
        in_specs=[_whole_vmem()] * (n_t + 2), out_specs=[_any()] * n_t + [_whole_vmem()] * n_t,
        scratch_shapes=[pltpu.SemaphoreType.DMA((n_t,))],
        compiler_params=pltpu.CompilerParams(vmem_limit_bytes=VMEM_LIMIT),
    )(*mats, meta, conv_w, w2)
    return res[0:n_t], res[n_t:]


def _stage_mats(mats):
    n_t = len(mats)

    def body(*refs):
        ins, lands, shards, sems = refs[0:n_t], refs[n_t:2 * n_t], refs[2 * n_t:3 * n_t], refs[3 * n_t]
        for s_ref, w_ref in zip(shards, ins):
            s_ref[...] = w_ref[...].astype(BF16)
        x, y, c = _position()
        mine = [pltpu.make_async_copy(shards[t], lands[t].at[4 * x + 2 * y + c], sems.at[t]) for t in range(n_t)]
        for cp in mine:
            cp.start()
        for cp in mine:
            cp.wait()

    shard_shapes = [jax.ShapeDtypeStruct(m.shape, BF16) for m in mats]
    res = pl.pallas_call(
        body, name="stage_late",
        out_shape=[jax.ShapeDtypeStruct((N_DEV,) + s.shape, s.dtype) for s in shard_shapes] + shard_shapes,
        in_specs=[_whole_vmem()] * n_t, out_specs=[_any()] * n_t + [_whole_vmem()] * n_t,
        scratch_shapes=[pltpu.SemaphoreType.DMA((n_t,))],
        compiler_params=pltpu.CompilerParams(vmem_limit_bytes=VMEM_LIMIT),
    )(*mats)
    return res[0:n_t], res[n_t:]


_HBM = pl.BlockSpec(memory_space=pltpu.HBM)
_SEM = pl.BlockSpec(memory_space=pltpu.SEMAPHORE)
_EFFECT = pltpu.SideEffectType.DATAFLOW_SIDE_EFFECTING


_N_ROUTES = {"scatter": 7, "first": 4, "forward": 3}


def _routes(mode):
    x, y, c = _position()
    me = 4 * x + 2 * y + c
    if mode == "scatter":
        out = []
        for k in range(1, N_DEV):
            px = 1 - x if k & 4 else x
            py = 1 - y if k & 2 else y
            pc = 1 - c if k & 1 else c
            out.append(((px, py, pc), 4 * px + 2 * py + pc, me))
        return out
    if mode == "first":
        return [(pos, None, me) for pos in ((x, y, 1 - c), (1 - x, y, c), (x, 1 - y, c), (1 - x, 1 - y, c))]
    assert mode == "forward"
    return [((x, y, 1 - c), 4 * px + 2 * py + c, 4 * px + 2 * py + c) for px, py in ((1 - x, y), (x, 1 - y), (1 - x, 1 - y))]


def _route_copies(mode, n, src_refs, land_refs, send_sems, recv_sems):
    nr = _N_ROUTES[mode]
    for i, (pos, src_blk, dst_blk) in enumerate(_routes(mode)):
        for t in range(n):
            src = land_refs[t] if mode == "forward" else src_refs[t]
            yield pltpu.make_async_remote_copy(
                src_ref=src if src_blk is None else src.at[src_blk], dst_ref=land_refs[t].at[dst_blk],
                send_sem=send_sems.at[nr * t + i], recv_sem=recv_sems.at[nr * t + i], device_id=pos, device_id_type=MESH)


def _in_hbm(a):
    return pltpu.with_memory_space_constraint(a, pltpu.HBM)


def _send_start(name, groups, mode, after):
    sizes = [(len(s), len(l)) for s, l in groups]
    bufs = [b for s, l in groups for b in list(s) + list(l)]
    nb, ng = len(bufs), len(groups)

    def body(*refs):
        sems = refs[nb + 1:nb + 1 + 2 * ng]
        token = refs[2 * nb + 2 * ng + 1]
        off = 0
        for gi, (ns, n) in enumerate(sizes):
            for cp in _route_copies(mode, n, refs[off:off + ns], refs[off + ns:off + ns + n], sems[2 * gi], sems[2 * gi + 1]):
                cp.start()
            off += ns + n
        token[...] = jnp.zeros_like(token)

    res = pl.pallas_call(
        body, name=name,
        out_shape=(*[pltpu.SemaphoreType.DMA((_N_ROUTES[mode] * n,)) for _, n in sizes for _ in range(2)],
                   *[pltpu.HBM(b.shape, b.dtype) for b in bufs], jax.ShapeDtypeStruct((8, 128), F32)),
        in_specs=[_HBM] * nb + [_any()], out_specs=(*[_SEM] * (2 * ng), *[_HBM] * nb, _whole_vmem()),
        input_output_aliases={i: 2 * ng + i for i in range(nb)},
        compiler_params=pltpu.CompilerParams(has_side_effects=_EFFECT),
    )(*[_in_hbm(b) for b in bufs], after)
    handles, off = [], 2 * ng
    for gi, (ns, n) in enumerate(sizes):
        handles.append((res[2 * gi], res[2 * gi + 1], res[off:off + ns], res[off + ns:off + ns + n]))
        off += ns + n
    return handles, res[2 * ng + nb]


def _send_wait(name, send_sems, recv_sems, srcs, lands, mode, after):
    n, ns = len(lands), len(srcs)
    after = after if isinstance(after, tuple) else (after,)

    def body(*refs):
        src_refs, land_refs = refs[0:ns], refs[ns:ns + n]
        send_sems, recv_sems = refs[ns + n:ns + n + 2]
        for cp in _route_copies(mode, n, src_refs, land_refs, send_sems, recv_sems):
            cp.wait_send()
            cp.wait_recv()

    bufs = list(srcs) + list(lands)
    res = pl.pallas_call(
        body, name=name,
        out_shape=tuple(pltpu.HBM(b.shape, b.dtype) for b in bufs),
        in_specs=[_HBM] * len(bufs) + [_SEM, _SEM] + [_any()] * len(after), out_specs=tuple([_HBM] * len(bufs)),
        input_output_aliases={i: i for i in range(len(bufs))},
        compiler_params=pltpu.CompilerParams(has_side_effects=_EFFECT),
    )(*bufs, send_sems, recv_sems, *after)
    return res[0:ns], res[ns:ns + n]


def _unshard_in(a_in, a_small, token):
    def body(a_ref, s_ref, token_ref, w_ref, meta_ref, cw_ref, w2_ref):
        w_ref[:, D_IN:D_INP] = jnp.zeros((D, D_INP - D_IN), BF16)
        w2_ref[...] = jnp.zeros_like(w2_ref)
        for d in range(N_DEV):
            w_ref[:, d * W_IN_S:(d + 1) * W_IN_S] = a_ref[d]
            meta_ref[:, d * 128:(d + 1) * 128] = s_ref[d, 0:N_META, :]
            cw_ref[:, d * CONV_S:(d + 1) * CONV_S] = s_ref[d, CONV_ROW:CONV_ROW + 32, 0:CONV_S]
            w2_ref[0:RANK, d * GATE_S:(d + 1) * GATE_S] = s_ref[d, GATE_ROW:GATE_ROW + RANK, 0:GATE_S].astype(BF16)

    return pl.pallas_call(
        body, name="unshard_in",
        out_shape=[jax.ShapeDtypeStruct((D, D_INP), BF16), jax.ShapeDtypeStruct((N_META, D), F32),
                   jax.ShapeDtypeStruct((32, C_CONV), F32), jax.ShapeDtypeStruct((RANK_P, GLA_K), BF16)],
        compiler_params=pltpu.CompilerParams(vmem_limit_bytes=VMEM_LIMIT),
    )(a_in, a_small, token)


def _pack_small(g):
    def body(meta_ref, cw_ref, w2_ref, in_vec, ffn_vec, conv_vec, gla_vec, sp, vp):
        sp[...] = jnp.zeros_like(sp)
        vp[...] = jnp.zeros_like(vp)
        for d in range(N_DEV):
            sp[d, 0:N_META, :] = meta_ref[:, d * 128:(d + 1) * 128]
            sp[d, CONV_ROW:CONV_ROW + 32, 0:CONV_S] = cw_ref[:, d * CONV_S:(d + 1) * CONV_S]
            sp[d, GATE_ROW:GATE_ROW + RANK, 0:GATE_S] = w2_ref[0:RANK, d * GATE_S:(d + 1) * GATE_S]
            vp[d, 0:1, :] = in_vec[0:1, :]
            vp[d, 1:4, 0:C_CONV] = conv_vec[0:3, :]
            vp[d, 4:5, 0:GLA_K] = gla_vec[0:1, :]
            vp[d, 5:6, 0:GLA_DV] = gla_vec[1:2, 0:GLA_DV]
            vp[d, 6:7, :] = ffn_vec[1:2, :]
            vp[d, 7:8, :] = ffn_vec[0:1, :]
            vp[d, LOSS_ROW:LOSS_ROW + 1, :] = ffn_vec[2:3, :]

    return pl.pallas_call(
        body, name="pack_small",
        out_shape=[jax.ShapeDtypeStruct((N_DEV, SMALL_PACK, 128), F32), jax.ShapeDtypeStruct((N_DEV, VEC_ROWS, D), F32)],
    )(g["meta"], g["conv_w"], g["w2"], g["in_vec"], g["ffn_vec"], g["conv_vec"], g["gla_vec"])


def _adamw(w, g, m, v):
    m = ADAM_B1 * m + (1.0 - ADAM_B1) * g
    v = ADAM_B2 * v + (1.0 - ADAM_B2) * (g * g)
    m_hat = m / (1.0 - ADAM_B1 ** ADAM_STEP)
    v_hat = v / (1.0 - ADAM_B2 ** ADAM_STEP)
    return -ADAM_LR * (m_hat / (jnp.sqrt(v_hat) + ADAM_EPS) + ADAM_WD * w), m, v


def _update_matrix(recv, own, me, w, m, v, name):
    _, r, c = recv.shape
    tr = _row_tile(r, 256)

    def body(me_ref, recv_ref, own_ref, w_ref, m_ref, v_ref, g_ref, d_ref, nm_ref, nv_ref):
        g = jnp.zeros((tr, c), F32)
        for s in range(N_DEV):
            g = g + jnp.where(me_ref[0] == s, own_ref[...], recv_ref[s]).astype(F32)
        g_ref[...] = g
        d_ref[...], nm_ref[...], nv_ref[...] = _adamw(w_ref[...], g, m_ref[...], v_ref[...])

    one = pl.BlockSpec((None, tr, c), lambda i, me_ref: (0, i, 0))
    return pl.pallas_call(
        body, name=name,
        grid_spec=pltpu.PrefetchScalarGridSpec(
            num_scalar_prefetch=1, grid=(r // tr,),
            in_specs=[pl.BlockSpec((N_DEV, tr, c), lambda i, me_ref: (0, i, 0)),
                      pl.BlockSpec((None, tr, c), lambda i, me_ref: (me_ref[0], i, 0)), one, one, one],
            out_specs=[one] * 4),
        out_shape=[jax.ShapeDtypeStruct((1, r, c), F32)] * 4,
        compiler_params=_params(("parallel",)),
    )(me, recv, own, w, m, v)


_SMALL = ("meta_tokens", "conv_w", "gla_w_gate2") + tuple(n for n, _ in _VEC_ROWS)


def _update_small(me, srecv, vrecv, sown, vown, w, m, v):
    n = len(_SMALL)

    def body(*refs):
        me_ref, s_ref, v_ref, so_ref, vo_ref = refs[0:5]
        w_refs, m_refs, v_refs = refs[5:5 + n], refs[5 + n:5 + 2 * n], refs[5 + 2 * n:5 + 3 * n]
        outs = refs[5 + 3 * n:]
        ssum = jnp.zeros((SMALL_PACK, 128), F32)
        vsum = jnp.zeros((VEC_ROWS, D), F32)
        for s in range(N_DEV):
            ssum = ssum + jnp.where(me_ref[0] == s, so_ref[s], s_ref[s])
            vsum = vsum + jnp.where(me_ref[0] == s, vo_ref[s], v_ref[s])
        grads = [ssum[0:N_META, :], ssum[CONV_ROW:CONV_ROW + CONV_W, 0:CONV_S], ssum[GATE_ROW:GATE_ROW + RANK, 0:GATE_S]]
        grads += [vsum[i:i + 1, 0:width] for i, (_, width) in enumerate(_VEC_ROWS)]
        for i, g in enumerate(grads):
            d, nm, nv = _adamw(w_refs[i][...], g, m_refs[i][...], v_refs[i][...])
            outs[i][...] = g
            outs[n + i][...] = d
            outs[2 * n + i][...] = nm
            outs[3 * n + i][...] = nv
        outs[4 * n][...] = vsum[LOSS_ROW:LOSS_ROW + 1, 0:128]

    shapes = [jax.ShapeDtypeStruct(t.shape, F32) for t in w]
    res = pl.pallas_call(
        body, name="update_small", out_shape=shapes * 4 + [jax.ShapeDtypeStruct((1, 128), F32)],
        in_specs=[pl.BlockSpec(memory_space=pltpu.SMEM)] + [_whole_vmem()] * (4 + 3 * n),
    )(me, srecv, vrecv, sown, vown, *w, *m, *v)
    return res[0:n], res[n:2 * n], res[2 * n:3 * n], res[3 * n:4 * n], res[4 * n]


_WEIGHTS = ("meta_tokens", "norm_mix_g", "w_in", "conv_w", "conv_b", "conv_ln_g", "conv_ln_b", "gla_w_gate2", "gla_gate_b",
            "gla_norm_g", "w_out", "norm_ffn_g", "w_ffn_gate", "w_ffn_up", "w_ffn_down", "norm_final_g")
_MATRICES = ("w_in", "w_out", "w_ffn_gate", "w_ffn_up", "w_ffn_down")
_TRANSPOSED = ("w_ffn_gate", "w_ffn_up")


def kernel(x, meta_tokens, norm_mix_g, w_in, conv_w, conv_b, conv_ln_g, conv_ln_b, gla_w_gate2, gla_gate_b, gla_norm_g, w_out, norm_ffn_g, w_ffn_gate, w_ffn_up, w_ffn_down, norm_final_g, loss_target, m_meta_tokens, m_norm_mix_g, m_w_in, m_conv_w, m_conv_b, m_conv_ln_g, m_conv_ln_b, m_gla_w_gate2, m_gla_gate_b, m_gla_norm_g, m_w_out, m_norm_ffn_g, m_w_ffn_gate, m_w_ffn_up, m_w_ffn_down, m_norm_final_g, v_meta_tokens, v_norm_mix_g, v_w_in, v_conv_w, v_conv_b, v_conv_ln_g, v_conv_ln_b, v_gla_w_gate2, v_gla_gate_b, v_gla_norm_g, v_w_out, v_norm_ffn_g, v_w_ffn_gate, v_w_ffn_up, v_w_ffn_down, v_norm_final_g):
    given = dict(locals())
    two_d = lambda a: a.reshape(1, -1) if a.ndim == 1 else a.reshape(a.shape[-2:])
    fams = [{n: given[pre + n] for n in _WEIGHTS} for pre in ("", "m_", "v_")]
    for f in fams:
        for n in _TRANSPOSED:
            f[n] = f[n].transpose(0, 2, 1)
    w = fams[0]

    lands, shards = _stage([two_d(w[_MATRICES[0]])], w["meta_tokens"], two_d(w["conv_w"]), two_d(w["gla_w_gate2"]))
    (first,), started = _send_start("gather_first_start", [(list(shards), list(lands))], "first", norm_mix_g)
    lands_late, shards_late = _stage_mats([two_d(w[n]) for n in _MATRICES[1:]])
    (ffn_first,), started = _send_start(
        "gather_ffn_first_start", [(list(shards_late), list(lands_late))], "first", started)
    h0, tgt_p = _pad_rows(x, loss_target)
    _, arrived = _send_wait("gather_first_wait", *first, "first", (h0, tgt_p, started))
    (forward,), token = _send_start("gather_forward_start", [([], arrived)], "forward", started)
    _, (a_in, a_small) = _send_wait("gather_forward_wait", *forward, "forward", token)
    w_in, meta, conv_taps, w2 = _unshard_in(a_in, a_small, token)
    p = dict(meta=meta, conv_w=conv_taps, w2=w2, w_in=w_in, g1=norm_mix_g, conv_b=conv_b, ln_g=conv_ln_g, ln_b=conv_ln_b,
             gb=gla_gate_b, ng=gla_norm_g, g2=norm_ffn_g, g3=two_d(norm_final_g), token=token)
    passed = {}

    def pass_on(after):
        _, arrived_ffn = _send_wait("gather_ffn_first_wait", *ffn_first, "first", after)
        (passed["sent"],), token = _send_start("gather_ffn_forward_start", [([], arrived_ffn)], "forward", norm_mix_g)
        return token

    def late_weights(after):
        _, (a_out, a_g, a_u, a_d) = _send_wait("gather_ffn_forward_wait", *passed["sent"], "forward", after)
        return a_out.reshape(D, D), a_g.reshape(D_FF, D), a_u.reshape(D_FF, D), a_d.reshape(D_FF, D)

    sent = {}

    def send_early(tag, mats):
        landing = [_in_hbm(lax.empty(m_.shape, m_.dtype)) for m_ in mats]
        (sent[tag],), token = _send_start("scatter_" + tag + "_start", [(mats, landing)], "scatter", norm_mix_g)
        return token

    grad_x, g = _local_step(h0, tgt_p, p, pass_on, late_weights, send_early)

    token = send_early("small", list(_pack_small(g)))
    x_, y_, c_ = _position()
    me = (4 * x_ + 2 * y_ + c_).astype(jnp.int32).reshape(1)
    res = {}
    for tag, names in (("ffn", ("w_ffn_gate", "w_ffn_up", "w_ffn_down")), ("out", ("w_out",)), ("in", ("w_in",))):
        own, recv = _send_wait("scatter_" + tag + "_wait", *sent[tag], "scatter", token)
        for n, o_, r_ in zip(names, own, recv):
            res[n] = _update_matrix(r_, o_, me, *[f[n] for f in fams], "update_" + n)
            token = res[n][1]
    (sown, vown), (srecv, vrecv) = _send_wait("scatter_small_wait", *sent["small"], "scatter", token)
    small = _update_small(me, srecv, vrecv, sown, vown, *[[two_d(f[n]) for n in _SMALL] for f in fams])
    for i, n in enumerate(_SMALL):
        res[n] = [fam[i].reshape(w[n].shape) for fam in small[0:4]]
    for n in _TRANSPOSED:
        res[n] = [t.transpose(0, 2, 1) for t in res[n]]
    outs = [small[4][0, 0], grad_x]
    for k in range(4):
        outs += [res[n][k] for n in _WEIGHTS]
    return tuple(outs)
```
